```python
import jax, jax.numpy as jnp
from jax import lax
import numpy as np

D_MODEL = 1024
BATCH = 16
SEQ = 2048
DEPTH = 4

SSM_EXPAND = 2
SSM_WIDTH = SSM_EXPAND * D_MODEL
SSM_HEAD_DIM = 64
SSM_HEADS = SSM_WIDTH // SSM_HEAD_DIM
SSM_GROUPS = 2
SSM_STATE = 128
CONV_WIDTH = 4
CHUNK = 128
CONV_CH = SSM_WIDTH + 2 * SSM_GROUPS * SSM_STATE
POOL_WIDTH = D_MODEL
POOL_WINDOWS = (2, 4, 8, 16)
POOL_GROUPS = 4
POOL_GROUP_DIM = POOL_WIDTH // POOL_GROUPS
SB_WIDTH = D_MODEL
SB_HEAD_DIM = 64
SB_HEADS = SB_WIDTH // SB_HEAD_DIM
SB_BLOCK = 128
N_BRANCHES = 3
EPS = 1e-6
IN_SIZES = (SSM_WIDTH, CONV_CH, SSM_HEADS, POOL_WIDTH, POOL_WIDTH, 3 * SB_WIDTH, SB_WIDTH, N_BRANCHES * D_MODEL)
IN_COLS = SSM_WIDTH + CONV_CH + SSM_HEADS + 2 * POOL_WIDTH + 4 * SB_WIDTH + N_BRANCHES * D_MODEL

kernel_name = "hybrid_ssd_pool_stickbreak_gated_block"


def _split_points():
    pts, run = [], 0
    for s in IN_SIZES[:-1]:
        run += s
        pts.append(run)
    return pts


def rms_norm(x, w):
    xf = x.astype(jnp.float32)
    var = jnp.mean(xf * xf, axis=-1, keepdims=True)
    return (xf * lax.rsqrt(var + EPS)).astype(x.dtype) * w


def causal_dwconv(u, w, b):
    S = u.shape[1]
    up = jnp.pad(u, ((0, 0), (CONV_WIDTH - 1, 0), (0, 0)))
    out = b
    for k in range(CONV_WIDTH):
        out = out + up[:, k:k + S] * w[k]
    return out


def segsum(a):
    T = a.shape[-1]
    cs = jnp.cumsum(a, axis=-1)
    seg = cs[..., :, None] - cs[..., None, :]
    mask = jnp.tril(jnp.ones((T, T), dtype=bool))
    return jnp.where(mask, seg, -jnp.inf)


def ssd_chunked(xh, dt, a, Bg, Cg):
    Bsz, S, H, P = xh.shape
    G, N = Bg.shape[2], Bg.shape[3]
    hpg = H // G
    nc = S // CHUNK
    xdt = (xh * dt[..., None]).reshape(Bsz, nc, CHUNK, G, hpg, P)
    adt = (dt.astype(jnp.float32) * a.astype(jnp.float32)).reshape(Bsz, nc, CHUNK, G, hpg)
    adt = jnp.moveaxis(adt, 2, -1)
    Bc = Bg.reshape(Bsz, nc, CHUNK, G, N)
    Cc = Cg.reshape(Bsz, nc, CHUNK, G, N)
    a_cum = jnp.cumsum(adt, axis=-1)
    decay_in = jnp.exp(segsum(adt))
    cb = jnp.einsum('bclgn,bcsgn->bcgls', Cc, Bc)
    y_diag = jnp.einsum('bcghls,bcsghp->bclghp', cb[:, :, :, None] * decay_in, xdt)
    decay_states = jnp.exp(a_cum[..., -1:] - a_cum)
    states = jnp.einsum('bclgn,bcghl,bclghp->bcghpn', Bc, decay_states, xdt)
    chunk_decay = jnp.exp(a_cum[..., -1])

    def step(carry, inp):
        st, dec = inp
        return carry * dec[..., None, None] + st, carry

    init = jnp.zeros_like(states[:, 0])
    _, prev = lax.scan(step, init, (jnp.moveaxis(states, 1, 0), jnp.moveaxis(chunk_decay, 1, 0)))
    prev = jnp.moveaxis(prev, 0, 1)
    y_off = jnp.einsum('bclgn,bcghpn,bcghl->bclghp', Cc, prev, jnp.exp(a_cum))
    return (y_diag + y_off).reshape(Bsz, S, H, P)


def mamba2_branch(z, xbc, dt_raw, conv_w, conv_b, dt_bias, a_log, d_skip, ssm_norm_w):
    Bsz, S, _ = z.shape
    xbc = jax.nn.silu(causal_dwconv(xbc, conv_w, conv_b))
    xs, Bg, Cg = jnp.split(xbc, [SSM_WIDTH, SSM_WIDTH + SSM_GROUPS * SSM_STATE], axis=-1)
    xh = xs.reshape(Bsz, S, SSM_HEADS, SSM_HEAD_DIM)
    Bg = Bg.reshape(Bsz, S, SSM_GROUPS, SSM_STATE)
    Cg = Cg.reshape(Bsz, S, SSM_GROUPS, SSM_STATE)
    dt = jax.nn.softplus((dt_raw + dt_bias).astype(jnp.float32))
    a = -jnp.exp(a_log.astype(jnp.float32))
    y = ssd_chunked(xh, dt, a, Bg, Cg) + xh * d_skip[:, None]
    y = y.reshape(Bsz, S, SSM_WIDTH).astype(z.dtype)
    return rms_norm(y * jax.nn.silu(z), ssm_norm_w)


def pool_branch(u, gate, pool_w, pool_scale):
    Bsz, S, _ = u.shape
    uf = u.astype(jnp.float32).reshape(Bsz, S, POOL_GROUPS, POOL_GROUP_DIM)
    cs = jnp.cumsum(uf, axis=1)
    pos = jnp.arange(S)
    pooled = []
    for g, w in enumerate(POOL_WINDOWS):
        csw = jnp.pad(cs[:, :, g], ((0, 0), (w, 0), (0, 0)))
        win_sum = csw[:, w:] - csw[:, :S]
        cnt = jnp.minimum(pos + 1, w).astype(jnp.float32)
        pooled.append(win_sum / cnt[None, :, None])
    mixed = jnp.stack(pooled, axis=2) - uf
    mixed = jnp.einsum('bsgi,gio->bsgo', mixed.astype(u.dtype), pool_w).reshape(Bsz, S, POOL_WIDTH)
    return (mixed * pool_scale * jax.nn.silu(gate)).astype(u.dtype)


def stick_breaking_branch(qkv, gate):
    Bsz, S, _ = qkv.shape
    q, k, v = jnp.split(qkv, 3, axis=-1)

    def heads(t):
        return t.reshape(Bsz, S, SB_HEADS, SB_HEAD_DIM).transpose(0, 2, 1, 3)

    q, k, v = heads(q), heads(k), heads(v)
    scale = SB_HEAD_DIM ** -0.5
    outs = []
    for i in range(S // SB_BLOCK):
        q0 = i * SB_BLOCK
        kend = q0 + SB_BLOCK
        qb = q[:, :, q0:kend]
        kb = k[:, :, :kend]
        vb = v[:, :, :kend]
        z = jnp.einsum('bhtd,bhsd->bhts', qb, kb).astype(jnp.float32) * scale
        causal = (q0 + jnp.arange(SB_BLOCK))[:, None] > jnp.arange(kend)[None, :]
        log_beta = jax.nn.log_sigmoid(z)
        log_one_minus = jnp.where(causal, jax.nn.log_sigmoid(-z), 0.0)
        later = lax.cumsum(log_one_minus, axis=3, reverse=True) - log_one_minus
        att = jnp.where(causal, jnp.exp(log_beta + later), 0.0)
        outs.append(jnp.einsum('bhts,bhsd->bhtd', att.astype(vb.dtype), vb))
    o = jnp.concatenate(outs, axis=2).transpose(0, 2, 1, 3).reshape(Bsz, S, SB_WIDTH)
    return o * jax.nn.silu(gate)


def hybrid_layer(x, norm_w, w_in, conv_w, conv_b, dt_bias, a_log, d_skip, ssm_norm_w,
                 pool_w, pool_scale, w_proj_ssm, w_proj_pool, w_proj_sb, w_out):
    Bsz, S, D = x.shape
    h = rms_norm(x, norm_w)
    proj = h @ w_in
    z, xbc, dt_raw, pool_u, pool_gate, qkv, sb_gate, merge = jnp.split(proj, _split_points(), axis=-1)
    y_ssm = mamba2_branch(z, xbc, dt_raw, conv_w, conv_b, dt_bias, a_log, d_skip, ssm_norm_w) @ w_proj_ssm
    y_pool = pool_branch(pool_u, pool_gate, pool_w, pool_scale) @ w_proj_pool
    y_sb = stick_breaking_branch(qkv, sb_gate) @ w_proj_sb
    g = jax.nn.sigmoid(merge.astype(jnp.float32)).reshape(Bsz, S, N_BRANCHES, D).astype(x.dtype)
    merged = g[:, :, 0] * y_ssm + g[:, :, 1] * y_pool + g[:, :, 2] * y_sb
    return x + (merged @ w_out).astype(x.dtype)


def _fwd_setup_inputs(seed: int = 0) -> dict:
    key = jax.random.key(seed)
    ks = jax.random.split(key, 17)
    f32 = jnp.float32
    nrm = lambda k, shape, s: jax.random.normal(k, shape, f32) * s
    dt = jnp.exp(jax.random.uniform(ks[5], (DEPTH, SSM_HEADS), f32, float(np.log(1e-3)), float(np.log(1e-1))))
    return {
        "x": nrm(ks[0], (BATCH, SEQ, D_MODEL), 1.0),
        "norm_w": 1.0 + nrm(ks[1], (DEPTH, D_MODEL), 0.02),
        "w_in": nrm(ks[2], (DEPTH, D_MODEL, IN_COLS), D_MODEL ** -0.5),
        "conv_w": nrm(ks[3], (DEPTH, CONV_WIDTH, CONV_CH), CONV_WIDTH ** -0.5),
        "conv_b": nrm(ks[4], (DEPTH, CONV_CH), 0.02),
        "dt_bias": dt + jnp.log(-jnp.expm1(-dt)),
        "a_log": jnp.log(jax.random.uniform(ks[6], (DEPTH, SSM_HEADS), f32, 1.0, 16.0)),
        "d_skip": 1.0 + nrm(ks[7], (DEPTH, SSM_HEADS), 0.02),
        "ssm_norm_w": 1.0 + nrm(ks[8], (DEPTH, SSM_WIDTH), 0.02),
        "pool_w": nrm(ks[9], (DEPTH, POOL_GROUPS, POOL_GROUP_DIM, POOL_GROUP_DIM), POOL_GROUP_DIM ** -0.5),
        "pool_scale": 1.0 + nrm(ks[10], (DEPTH, POOL_WIDTH), 0.02),
        "w_proj_ssm": nrm(ks[11], (DEPTH, SSM_WIDTH, D_MODEL), SSM_WIDTH ** -0.5),
        "w_proj_pool": nrm(ks[12], (DEPTH, POOL_WIDTH, D_MODEL), POOL_WIDTH ** -0.5),
        "w_proj_sb": nrm(ks[13], (DEPTH, SB_WIDTH, D_MODEL), SB_WIDTH ** -0.5),
        "w_out": nrm(ks[14], (DEPTH, D_MODEL, D_MODEL), (N_BRANCHES * D_MODEL) ** -0.5),
        "final_norm_w": 1.0 + nrm(ks[15], (D_MODEL,), 0.02),
    }


def _fwd_reference(x, norm_w, w_in, conv_w, conv_b, dt_bias, a_log, d_skip, ssm_norm_w,
              pool_w, pool_scale, w_proj_ssm, w_proj_pool, w_proj_sb, w_out, final_norm_w):
    for l in range(DEPTH):
        x = hybrid_layer(x, norm_w[l], w_in[l], conv_w[l], conv_b[l], dt_bias[l], a_log[l],
                         d_skip[l], ssm_norm_w[l], pool_w[l], pool_scale[l], w_proj_ssm[l],
                         w_proj_pool[l], w_proj_sb[l], w_out[l])
    return rms_norm(x, final_norm_w)


import jax as _jax
import jax.numpy as _jnp

TWIN_FORMAT = 'train_step'
FWD_PARAMS = ['x', 'norm_w', 'w_in', 'conv_w', 'conv_b', 'dt_bias', 'a_log', 'd_skip', 'ssm_norm_w', 'pool_w', 'pool_scale', 'w_proj_ssm', 'w_proj_pool', 'w_proj_sb', 'w_out', 'final_norm_w']
TWIN_WEIGHTS = ['norm_w', 'w_in', 'conv_w', 'conv_b', 'dt_bias', 'a_log', 'd_skip', 'ssm_norm_w', 'pool_w', 'pool_scale', 'w_proj_ssm', 'w_proj_pool', 'w_proj_sb', 'w_out', 'final_norm_w']
TWIN_DIFF_INPUT = 'x'
TWIN_INPUTS = ['x', 'norm_w', 'w_in', 'conv_w', 'conv_b', 'dt_bias', 'a_log', 'd_skip', 'ssm_norm_w', 'pool_w', 'pool_scale', 'w_proj_ssm', 'w_proj_pool', 'w_proj_sb', 'w_out', 'final_norm_w', 'loss_target', 'm_norm_w', 'm_w_in', 'm_conv_w', 'm_conv_b', 'm_dt_bias', 'm_a_log', 'm_d_skip', 'm_ssm_norm_w', 'm_pool_w', 'm_pool_scale', 'm_w_proj_ssm', 'm_w_proj_pool', 'm_w_proj_sb', 'm_w_out', 'm_final_norm_w', 'v_norm_w', 'v_w_in', 'v_conv_w', 'v_conv_b', 'v_dt_bias', 'v_a_log', 'v_d_skip', 'v_ssm_norm_w', 'v_pool_w', 'v_pool_scale', 'v_w_proj_ssm', 'v_w_proj_pool', 'v_w_proj_sb', 'v_w_out', 'v_final_norm_w']
TWIN_OUTPUTS = ['loss', 'grad_x', 'grad_norm_w', 'grad_w_in', 'grad_conv_w', 'grad_conv_b', 'grad_dt_bias', 'grad_a_log', 'grad_d_skip', 'grad_ssm_norm_w', 'grad_pool_w', 'grad_pool_scale', 'grad_w_proj_ssm', 'grad_w_proj_pool', 'grad_w_proj_sb', 'grad_w_out', 'grad_final_norm_w', 'delta_norm_w', 'delta_w_in', 'delta_conv_w', 'delta_conv_b', 'delta_dt_bias', 'delta_a_log', 'delta_d_skip', 'delta_ssm_norm_w', 'delta_pool_w', 'delta_pool_scale', 'delta_w_proj_ssm', 'delta_w_proj_pool', 'delta_w_proj_sb', 'delta_w_out', 'delta_final_norm_w', 'new_m_norm_w', 'new_m_w_in', 'new_m_conv_w', 'new_m_conv_b', 'new_m_dt_bias', 'new_m_a_log', 'new_m_d_skip', 'new_m_ssm_norm_w', 'new_m_pool_w', 'new_m_pool_scale', 'new_m_w_proj_ssm', 'new_m_w_proj_pool', 'new_m_w_proj_sb', 'new_m_w_out', 'new_m_final_norm_w', 'new_v_norm_w', 'new_v_w_in', 'new_v_conv_w', 'new_v_conv_b', 'new_v_dt_bias', 'new_v_a_log', 'new_v_d_skip', 'new_v_ssm_norm_w', 'new_v_pool_w', 'new_v_pool_scale', 'new_v_w_proj_ssm', 'new_v_w_proj_pool', 'new_v_w_proj_sb', 'new_v_w_out', 'new_v_final_norm_w']
TWIN_LEAF_KINDS = {'loss': 'loss', 'grad_x': 'grad_x', 'grad_norm_w': 'grad_w', 'grad_w_in': 'grad_w', 'grad_conv_w': 'grad_w', 'grad_conv_b': 'grad_w', 'grad_dt_bias': 'grad_w', 'grad_a_log': 'grad_w', 'grad_d_skip': 'grad_w', 'grad_ssm_norm_w': 'grad_w', 'grad_pool_w': 'grad_w', 'grad_pool_scale': 'grad_w', 'grad_w_proj_ssm': 'grad_w', 'grad_w_proj_pool': 'grad_w', 'grad_w_proj_sb': 'grad_w', 'grad_w_out': 'grad_w', 'grad_final_norm_w': 'grad_w', 'delta_norm_w': 'delta_w', 'delta_w_in': 'delta_w', 'delta_conv_w': 'delta_w', 'delta_conv_b': 'delta_w', 'delta_dt_bias': 'delta_w', 'delta_a_log': 'delta_w', 'delta_d_skip': 'delta_w', 'delta_ssm_norm_w': 'delta_w', 'delta_pool_w': 'delta_w', 'delta_pool_scale': 'delta_w', 'delta_w_proj_ssm': 'delta_w', 'delta_w_proj_pool': 'delta_w', 'delta_w_proj_sb': 'delta_w', 'delta_w_out': 'delta_w', 'delta_final_norm_w': 'delta_w', 'new_m_norm_w': 'new_m', 'new_m_w_in': 'new_m', 'new_m_conv_w': 'new_m', 'new_m_conv_b': 'new_m', 'new_m_dt_bias': 'new_m', 'new_m_a_log': 'new_m', 'new_m_d_skip': 'new_m', 'new_m_ssm_norm_w': 'new_m', 'new_m_pool_w': 'new_m', 'new_m_pool_scale': 'new_m', 'new_m_w_proj_ssm': 'new_m', 'new_m_w_proj_pool': 'new_m', 'new_m_w_proj_sb': 'new_m', 'new_m_w_out': 'new_m', 'new_m_final_norm_w': 'new_m', 'new_v_norm_w': 'new_v', 'new_v_w_in': 'new_v', 'new_v_conv_w': 'new_v', 'new_v_conv_b': 'new_v', 'new_v_dt_bias': 'new_v', 'new_v_a_log': 'new_v', 'new_v_d_skip': 'new_v', 'new_v_ssm_norm_w': 'new_v', 'new_v_pool_w': 'new_v', 'new_v_pool_scale': 'new_v', 'new_v_w_proj_ssm': 'new_v', 'new_v_w_proj_pool': 'new_v', 'new_v_w_proj_sb': 'new_v', 'new_v_w_out': 'new_v', 'new_v_final_norm_w': 'new_v'}


def _forward(args):
    return _fwd_reference(*[args[k] for k in FWD_PARAMS])


def _output_shape():
    out = _jax.eval_shape(lambda: _forward(_fwd_setup_inputs(0)))
    return out.shape, out.dtype

N_MICROBATCH = 1
ADAM_LR = 0.001
ADAM_B1 = 0.9
ADAM_B2 = 0.999
ADAM_EPS = 1e-08
ADAM_WD = 0.01
ADAM_STEP = 10
PER_EXAMPLE_BATCH_AXIS = {'x': 0, 'loss_target': 0}
SHARED_INPUTS = []
_WEIGHT_DTYPES = {'norm_w': _jnp.float32, 'w_in': _jnp.float32, 'conv_w': _jnp.float32, 'conv_b': _jnp.float32, 'dt_bias': _jnp.float32, 'a_log': _jnp.float32, 'd_skip': _jnp.float32, 'ssm_norm_w': _jnp.float32, 'pool_w': _jnp.float32, 'pool_scale': _jnp.float32, 'w_proj_ssm': _jnp.float32, 'w_proj_pool': _jnp.float32, 'w_proj_sb': _jnp.float32, 'w_out': _jnp.float32, 'final_norm_w': _jnp.float32}
MOMENT_SCALE = {'norm_w': 1.010947e-01, 'w_in': 2.703479e-02, 'conv_w': 3.649873e-02, 'conv_b': 4.921737e-02, 'dt_bias': 1.055185e-01, 'a_log': 1.280666e-01, 'd_skip': 2.111296e-01, 'ssm_norm_w': 3.860049e-02, 'pool_w': 2.788549e-02, 'pool_scale': 2.814841e-02, 'w_proj_ssm': 5.368403e-02, 'w_proj_pool': 2.790154e-02, 'w_proj_sb': 2.067121e-02, 'w_out': 1.104367e-01, 'final_norm_w': 3.199599e+01}


def _to_microbatches(a, axis):
    t = _jnp.moveaxis(a, axis, 0)
    t = t.reshape((N_MICROBATCH, t.shape[0] // N_MICROBATCH) + t.shape[1:])
    return _jnp.moveaxis(t, 1, axis + 1)


def setup_inputs(seed: int = 0) -> dict:
    inp = _fwd_setup_inputs(seed)
    key = _jax.random.fold_in(_jax.random.key(seed), 7919)
    shape, _ = _output_shape()
    out = dict(inp)
    out["loss_target"] = _jax.random.normal(_jax.random.fold_in(key, 0), shape, _jnp.float32)
    for i, name in enumerate(TWIN_WEIGHTS):
        w = inp[name].astype(_jnp.float32)
        if MOMENT_SCALE is None:
            s = _jnp.sqrt(_jnp.mean(_jnp.square(w)) + 1e-30)
        else:
            s = MOMENT_SCALE[name]
        km, kv = _jax.random.split(_jax.random.fold_in(key, i + 1))
        out[name] = w
        out["m_" + name] = s * _jax.random.normal(km, w.shape, _jnp.float32)
        out["v_" + name] = (s * s) * _jax.random.uniform(kv, w.shape, _jnp.float32, 0.5, 1.5)
    if N_MICROBATCH > 1:
        for name, axis in PER_EXAMPLE_BATCH_AXIS.items():
            out[name] = _to_microbatches(out[name], axis)
    return {'x': out['x'], 'norm_w': out['norm_w'], 'w_in': out['w_in'], 'conv_w': out['conv_w'], 'conv_b': out['conv_b'], 'dt_bias': out['dt_bias'], 'a_log': out['a_log'], 'd_skip': out['d_skip'], 'ssm_norm_w': out['ssm_norm_w'], 'pool_w': out['pool_w'], 'pool_scale': out['pool_scale'], 'w_proj_ssm': out['w_proj_ssm'], 'w_proj_pool': out['w_proj_pool'], 'w_proj_sb': out['w_proj_sb'], 'w_out': out['w_out'], 'final_norm_w': out['final_norm_w'], 'loss_target': out['loss_target'], 'm_norm_w': out['m_norm_w'], 'm_w_in': out['m_w_in'], 'm_conv_w': out['m_conv_w'], 'm_conv_b': out['m_conv_b'], 'm_dt_bias': out['m_dt_bias'], 'm_a_log': out['m_a_log'], 'm_d_skip': out['m_d_skip'], 'm_ssm_norm_w': out['m_ssm_norm_w'], 'm_pool_w': out['m_pool_w'], 'm_pool_scale': out['m_pool_scale'], 'm_w_proj_ssm': out['m_w_proj_ssm'], 'm_w_proj_pool': out['m_w_proj_pool'], 'm_w_proj_sb': out['m_w_proj_sb'], 'm_w_out': out['m_w_out'], 'm_final_norm_w': out['m_final_norm_w'], 'v_norm_w': out['v_norm_w'], 'v_w_in': out['v_w_in'], 'v_conv_w': out['v_conv_w'], 'v_conv_b': out['v_conv_b'], 'v_dt_bias': out['v_dt_bias'], 'v_a_log': out['v_a_log'], 'v_d_skip': out['v_d_skip'], 'v_ssm_norm_w': out['v_ssm_norm_w'], 'v_pool_w': out['v_pool_w'], 'v_pool_scale': out['v_pool_scale'], 'v_w_proj_ssm': out['v_w_proj_ssm'], 'v_w_proj_pool': out['v_w_proj_pool'], 'v_w_proj_sb': out['v_w_proj_sb'], 'v_w_out': out['v_w_out'], 'v_final_norm_w': out['v_final_norm_w']}


def _loss(weights, diff, rest, loss_target):
    with _jax.named_scope("forward"):
        args = {**rest, TWIN_DIFF_INPUT: diff, **{k: w.astype(_WEIGHT_DTYPES[k]) for k, w in weights.items()}}
        y = _forward(args)
    with _jax.named_scope("loss_head"):
        err = _jnp.square(y.astype(_jnp.float32) - loss_target)
        return 0.5 * _jnp.sum(_jnp.mean(err, axis=-1)) if err.ndim else 0.5 * err


def _adamw(w, g, m, v):
    m = ADAM_B1 * m + (1.0 - ADAM_B1) * g
    v = ADAM_B2 * v + (1.0 - ADAM_B2) * _jnp.square(g)
    m_hat = m / (1.0 - ADAM_B1 ** ADAM_STEP)
    v_hat = v / (1.0 - ADAM_B2 ** ADAM_STEP)
    delta = -ADAM_LR * (m_hat / (_jnp.sqrt(v_hat) + ADAM_EPS) + ADAM_WD * w)
    return delta, m, v


def reference(x, norm_w, w_in, conv_w, conv_b, dt_bias, a_log, d_skip, ssm_norm_w, pool_w, pool_scale, w_proj_ssm, w_proj_pool, w_proj_sb, w_out, final_norm_w, loss_target, m_norm_w, m_w_in, m_conv_w, m_conv_b, m_dt_bias, m_a_log, m_d_skip, m_ssm_norm_w, m_pool_w, m_pool_scale, m_w_proj_ssm, m_w_proj_pool, m_w_proj_sb, m_w_out, m_final_norm_w, v_norm_w, v_w_in, v_conv_w, v_conv_b, v_dt_bias, v_a_log, v_d_skip, v_ssm_norm_w, v_pool_w, v_pool_scale, v_w_proj_ssm, v_w_proj_pool, v_w_proj_sb, v_w_out, v_final_norm_w):
    given = dict(x=x, norm_w=norm_w, w_in=w_in, conv_w=conv_w, conv_b=conv_b, dt_bias=dt_bias, a_log=a_log, d_skip=d_skip, ssm_norm_w=ssm_norm_w, pool_w=pool_w, pool_scale=pool_scale, w_proj_ssm=w_proj_ssm, w_proj_pool=w_proj_pool, w_proj_sb=w_proj_sb, w_out=w_out, final_norm_w=final_norm_w, loss_target=loss_target, m_norm_w=m_norm_w, m_w_in=m_w_in, m_conv_w=m_conv_w, m_conv_b=m_conv_b, m_dt_bias=m_dt_bias, m_a_log=m_a_log, m_d_skip=m_d_skip, m_ssm_norm_w=m_ssm_norm_w, m_pool_w=m_pool_w, m_pool_scale=m_pool_scale, m_w_proj_ssm=m_w_proj_ssm, m_w_proj_pool=m_w_proj_pool, m_w_proj_sb=m_w_proj_sb, m_w_out=m_w_out, m_final_norm_w=m_final_norm_w, v_norm_w=v_norm_w, v_w_in=v_w_in, v_conv_w=v_conv_w, v_conv_b=v_conv_b, v_dt_bias=v_dt_bias, v_a_log=v_a_log, v_d_skip=v_d_skip, v_ssm_norm_w=v_ssm_norm_w, v_pool_w=v_pool_w, v_pool_scale=v_pool_scale, v_w_proj_ssm=v_w_proj_ssm, v_w_proj_pool=v_w_proj_pool, v_w_proj_sb=v_w_proj_sb, v_w_out=v_w_out, v_final_norm_w=v_final_norm_w)
    weights = {n: given[n] for n in TWIN_WEIGHTS}
    shared = {n: given[n] for n in SHARED_INPUTS}
    per_example = {n: given[n] for n in ['x']}
    grad_fn = _jax.value_and_grad(_loss, argnums=(0, 1))

    def one_microbatch(ex, loss_target):
        ex = dict(ex)
        diff = ex.pop(TWIN_DIFF_INPUT)
        return grad_fn(weights, diff, {**shared, **ex}, loss_target)

    if N_MICROBATCH == 1:
        loss, (grad_w, grad_x) = one_microbatch(per_example, given["loss_target"])
    else:
        def body(carry, xs):
            loss_sum, grad_sum = carry
            l_k, (gw_k, gx_k) = one_microbatch(xs[0], xs[1])
            with _jax.named_scope("update"):
                return (loss_sum + l_k, _jax.tree.map(_jnp.add, grad_sum, gw_k)), gx_k

        init = (_jnp.zeros((), _jnp.float32), _jax.tree.map(_jnp.zeros_like, weights))
        (loss, grad_w), grad_x = _jax.lax.scan(body, init, (per_example, given["loss_target"]))
    with _jax.named_scope("update"):
        delta_w, new_m, new_v = {}, {}, {}
        for n in TWIN_WEIGHTS:
            delta_w[n], new_m[n], new_v[n] = _adamw(weights[n], grad_w[n], given["m_" + n], given["v_" + n])
    return (loss, grad_x, *[grad_w[n] for n in TWIN_WEIGHTS], *[delta_w[n] for n in TWIN_WEIGHTS],
            *[new_m[n] for n in TWIN_WEIGHTS], *[new_v[n] for n in TWIN_WEIGHTS])
```

```python
import functools

import jax
import jax.numpy as jnp
from jax import lax
from jax.experimental import pallas as pl
from jax.experimental.pallas import tpu as pltpu

F32 = jnp.float32
BF16 = jnp.bfloat16

N_DEV = 8
DEPTH = 4
D = 1024
SSM_W = 2048
N_HEADS = 32
N_PAIRS = 16
N_GROUPS = 2
N_STATE = 128
CHUNK = 128
CONV_CH = 2560
CONV_K = 4
POOL_W = 1024
POOL_G = 4
POOL_GD = 256
SB_W = 1024
SB_PAIRS = 8
QB = 128
EPS = 1e-6
IN_COLS = 13856

PC = 14336
OFF_MERGE = 0
OFF_SBG = 3072
OFF_PU = 4096
OFF_PG = 5120
OFF_Z = 6144
OFF_QKV = 8192
OFF_XBC = 11264
OFF_DT = 13824

ADAM_LR = 0.001
ADAM_B1 = 0.9
ADAM_B2 = 0.999
ADAM_EPS = 1e-08
ADAM_WD = 0.01
ADAM_STEP = 10

VMEM_LIMIT = 56 * 1024 * 1024

_NN = (((1,), (0,)), ((), ()))
_NT = (((1,), (1,)), ((), ()))
_TN = (((0,), (0,)), ((), ()))


def _dot(a, b, dn=_NN):
    return lax.dot_general(a, b, dn, preferred_element_type=F32)


def _sigmoid(x):
    return 1.0 / (1.0 + jnp.exp(-x))


def _softplus(x):
    return jnp.maximum(x, 0.0) + jnp.log(1.0 + jnp.exp(-jnp.abs(x)))


def _split2(x):
    hi = x.astype(BF16)
    lo = (x - hi.astype(F32)).astype(BF16)
    return hi, lo


def _split3(x):
    hi = x.astype(BF16)
    r = x - hi.astype(F32)
    mid = r.astype(BF16)
    lo = (r - mid.astype(F32)).astype(BF16)
    return hi, mid, lo


def _params(*sem):
    return pltpu.CompilerParams(dimension_semantics=sem, vmem_limit_bytes=VMEM_LIMIT)


def matmul(a, b, mode, out_dtype, name, residual=None, tm=1024, tn=1024, tk=1024):
    if mode == "nn":
        (m, k), (k2, n) = a.shape, b.shape
    elif mode == "nt":
        (m, k), (n, k2) = a.shape, b.shape
    else:
        (k, m), (k2, n) = a.shape, b.shape
    assert k == k2
    tm, tn, tk = min(tm, m), min(tn, n), min(tk, k)
    assert m % tm == 0 and n % tn == 0 and k % tk == 0
    nk = k // tk
    dn = {"nn": _NN, "nt": _NT, "tn": _TN}[mode]
    a_spec = pl.BlockSpec((tk, tm), lambda i, j, kk: (kk, i)) if mode == "tn" else pl.BlockSpec((tm, tk), lambda i, j, kk: (i, kk))
    b_spec = pl.BlockSpec((tn, tk), lambda i, j, kk: (j, kk)) if mode == "nt" else pl.BlockSpec((tk, tn), lambda i, j, kk: (kk, j))
    in_specs = [a_spec, b_spec]
    args = [a, b]
    if residual is not None:
        in_specs.append(pl.BlockSpec((tm, tn), lambda i, j, kk: (i, j)))
        args.append(residual)

    def body(*refs):
        if residual is not None:
            a_ref, b_ref, r_ref, o_ref, acc_ref = refs
        else:
            a_ref, b_ref, o_ref, acc_ref = refs
            r_ref = None
        kk = pl.program_id(2)
        p = _dot(a_ref[...], b_ref[...], dn)

        def finish(val):
            if r_ref is not None:
                val = val + r_ref[...]
            o_ref[...] = val.astype(out_dtype)

        if nk == 1:
            finish(p)
        else:
            @pl.when(kk == 0)
            def _():
                acc_ref[...] = p

            @pl.when(kk > 0)
            def _():
                acc_ref[...] += p

            @pl.when(kk == nk - 1)
            def _():
                finish(acc_ref[...])

    return pl.pallas_call(
        body, name=name,
        grid=(m // tm, n // tn, nk),
        in_specs=in_specs,
        out_specs=pl.BlockSpec((tm, tn), lambda i, j, kk: (i, j)),
        out_shape=jax.ShapeDtypeStruct((m, n), out_dtype),
        scratch_shapes=[pltpu.VMEM((tm, tn) if nk > 1 else (8, 128), F32)],
        compiler_params=_params("parallel", "parallel", "arbitrary"),
    )(*args)


def rmsnorm_fwd(x, w, name):
    t, d = x.shape
    tr = min(512, t)

    def body(x_ref, w_ref, h_ref):
        xv = x_ref[...]
        r = lax.rsqrt(jnp.mean(xv * xv, axis=-1, keepdims=True) + EPS)
        h_ref[...] = (xv * r * w_ref[...]).astype(BF16)

    return pl.pallas_call(
        body, name=name, grid=(t // tr,),
        in_specs=[pl.BlockSpec((tr, d), lambda i: (i, 0)), pl.BlockSpec((1, d), lambda i: (0, 0))],
        out_specs=pl.BlockSpec((tr, d), lambda i: (i, 0)),
        out_shape=jax.ShapeDtypeStruct((t, d), BF16),
        compiler_params=_params("parallel"),
    )(x, w.reshape(1, d))


def rmsnorm_bwd(dh, x, w, dres, name):
    t, d = x.shape
    tr = min(512, t)

    def body(dh_ref, x_ref, w_ref, dres_ref, dx_ref, dw_ref):
        xv = x_ref[...]
        r = lax.rsqrt(jnp.mean(xv * xv, axis=-1, keepdims=True) + EPS)
        xh = xv * r
        g = dh_ref[...].astype(F32)
        dxh = g * w_ref[...]
        dx_ref[...] = dres_ref[...] + r * (dxh - xh * jnp.mean(dxh * xh, axis=-1, keepdims=True))
        part = jnp.sum(g * xh, axis=0, keepdims=True)

        @pl.when(pl.program_id(0) == 0)
        def _():
            dw_ref[...] = part

        @pl.when(pl.program_id(0) > 0)
        def _():
            dw_ref[...] += part

    return pl.pallas_call(
        body, name=name, grid=(t // tr,),
        in_specs=[pl.BlockSpec((tr, d), lambda i: (i, 0)), pl.BlockSpec((tr, d), lambda i: (i, 0)),
                  pl.BlockSpec((1, d), lambda i: (0, 0)), pl.BlockSpec((tr, d), lambda i: (i, 0))],
        out_specs=[pl.BlockSpec((tr, d), lambda i: (i, 0)), pl.BlockSpec((1, d), lambda i: (0, 0))],
        out_shape=[jax.ShapeDtypeStruct((t, d), F32), jax.ShapeDtypeStruct((1, d), F32)],
        compiler_params=_params("arbitrary"),
    )(dh, x, w.reshape(1, d), dres)


def final_loss(x, w, target, name):
    t, d = x.shape
    tr = min(512, t)

    def body(x_ref, w_ref, tg_ref, loss_ref, dx_ref, dw_ref):
        xv = x_ref[...]
        r = lax.rsqrt(jnp.mean(xv * xv, axis=-1, keepdims=True) + EPS)
        xh = xv * r
        err = xh * w_ref[...] - tg_ref[...]
        lpart = 0.5 * jnp.sum(jnp.mean(err * err, axis=-1, keepdims=True), axis=0, keepdims=True)
        dy = err * (1.0 / d)
        dxh = dy * w_ref[...]
        dx_ref[...] = r * (dxh - xh * jnp.mean(dxh * xh, axis=-1, keepdims=True))
        part = jnp.sum(dy * xh, axis=0, keepdims=True)

        @pl.when(pl.program_id(0) == 0)
        def _():
            dw_ref[...] = part
            loss_ref[...] = jnp.broadcast_to(lpart, (1, 128))

        @pl.when(pl.program_id(0) > 0)
        def _():
            dw_ref[...] += part
            loss_ref[...] += jnp.broadcast_to(lpart, (1, 128))

    return pl.pallas_call(
        body, name=name, grid=(t // tr,),
        in_specs=[pl.BlockSpec((tr, d), lambda i: (i, 0)), pl.BlockSpec((1, d), lambda i: (0, 0)),
                  pl.BlockSpec((tr, d), lambda i: (i, 0))],
        out_specs=[pl.BlockSpec((1, 128), lambda i: (0, 0)), pl.BlockSpec((tr, d), lambda i: (i, 0)),
                   pl.BlockSpec((1, d), lambda i: (0, 0))],
        out_shape=[jax.ShapeDtypeStruct((1, 128), F32), jax.ShapeDtypeStruct((t, d), F32),
                   jax.ShapeDtypeStruct((1, d), F32)],
        compiler_params=_params("arbitrary"),
    )(x, w.reshape(1, d), target)


CONV_BW = 256


def _shift_down(u, s, row):
    return jnp.where(row >= s, pltpu.roll(u, s, axis=0), 0.0)


def _shift_up(u, s, row, n):
    return jnp.where(row < n - s, pltpu.roll(u, n - s, axis=0), 0.0)


def _conv_pre(u, w, b, row):
    acc = b + w[CONV_K - 1:CONV_K, :] * u
    for k in range(CONV_K - 1):
        acc = acc + w[k:k + 1, :] * _shift_down(u, CONV_K - 1 - k, row)
    return acc


def conv_fwd(proj, conv_w, conv_b, bl, s, name):
    t = bl * s
    nb = CONV_CH // CONV_BW
    off = OFF_XBC // CONV_BW

    def body(u_ref, w_ref, b_ref, o_ref):
        u = u_ref[...].astype(F32)
        row = lax.broadcasted_iota(jnp.int32, u.shape, 0)
        xc = _conv_pre(u, w_ref[...], b_ref[...], row)
        o_ref[...] = (xc * _sigmoid(xc)).astype(BF16)

    return pl.pallas_call(
        body, name=name, grid=(bl, nb),
        in_specs=[pl.BlockSpec((s, CONV_BW), lambda b, j: (b, off + j)),
                  pl.BlockSpec((CONV_K, CONV_BW), lambda b, j: (0, j)),
                  pl.BlockSpec((1, CONV_BW), lambda b, j: (0, j))],
        out_specs=pl.BlockSpec((s, CONV_BW), lambda b, j: (b, j)),
        out_shape=jax.ShapeDtypeStruct((t, CONV_CH), BF16),
        compiler_params=_params("parallel", "parallel"),
    )(proj, conv_w, conv_b.reshape(1, CONV_CH))


def conv_bwd(dxa, proj, conv_w, conv_b, bl, s, name):
    t = bl * s
    nb = CONV_CH // CONV_BW
    off = OFF_XBC // CONV_BW

    def body(d_ref, u_ref, w_ref, b_ref, du_ref, dw_ref, db_ref):
        u = u_ref[...].astype(F32)
        w = w_ref[...]
        row = lax.broadcasted_iota(jnp.int32, u.shape, 0)
        xc = _conv_pre(u, w, b_ref[...], row)
        sg = _sigmoid(xc)
        dxc = d_ref[...].astype(F32) * sg * (1.0 + xc * (1.0 - sg))
        du = w[CONV_K - 1:CONV_K, :] * dxc
        dws = [None] * CONV_K
        dws[CONV_K - 1] = jnp.sum(dxc * u, axis=0, keepdims=True)
        for k in range(CONV_K - 1):
            sh = CONV_K - 1 - k
            du = du + w[k:k + 1, :] * _shift_up(dxc, sh, row, s)
            dws[k] = jnp.sum(dxc * _shift_down(u, sh, row), axis=0, keepdims=True)
        du_ref[...] = du.astype(BF16)
        krow = lax.broadcasted_iota(jnp.int32, (8, CONV_BW), 0)
        dwv = sum(jnp.where(krow == k, dws[k], 0.0) for k in range(CONV_K))
        dbv = jnp.sum(dxc, axis=0, keepdims=True)

        @pl.when(pl.program_id(1) == 0)
        def _():
            dw_ref[...] = dwv
            db_ref[...] = dbv

        @pl.when(pl.program_id(1) > 0)
        def _():
            dw_ref[...] += dwv
            db_ref[...] += dbv

    du, dw, db = pl.pallas_call(
        body, name=name, grid=(nb, bl),
        in_specs=[pl.BlockSpec((s, CONV_BW), lambda j, b: (b, j)),
                  pl.BlockSpec((s, CONV_BW), lambda j, b: (b, off + j)),
                  pl.BlockSpec((CONV_K, CONV_BW), lambda j, b: (0, j)),
                  pl.BlockSpec((1, CONV_BW), lambda j, b: (0, j))],
        out_specs=[pl.BlockSpec((s, CONV_BW), lambda j, b: (b, j)),
                   pl.BlockSpec((8, CONV_BW), lambda j, b: (0, j)),
                   pl.BlockSpec((1, CONV_BW), lambda j, b: (0, j))],
        out_shape=[jax.ShapeDtypeStruct((t, CONV_CH), BF16), jax.ShapeDtypeStruct((8, CONV_CH), F32),
                   jax.ShapeDtypeStruct((1, CONV_CH), F32)],
        compiler_params=_params("parallel", "arbitrary"),
    )(dxa, proj, conv_w, conv_b.reshape(1, CONV_CH))
    return du, dw[:CONV_K], db[0]


def _tri(shape, cmp):
    r = lax.broadcasted_iota(jnp.int32, shape, 0)
    c = lax.broadcasted_iota(jnp.int32, shape, 1)
    return cmp(r, c)


def dt_fwd(proj, dt_bias, a_log, t, name):
    nchunks = t // CHUNK
    bias = jnp.zeros((1, 128), F32).at[0, :N_HEADS].set(dt_bias)
    alog = jnp.zeros((1, 128), F32).at[0, :N_HEADS].set(a_log)

    def body(raw_ref, b_ref, al_ref, dt_ref, ac_ref):
        raw = raw_ref[...].astype(F32)
        dt = _softplus(raw + b_ref[...])
        adt = dt * (-jnp.exp(al_ref[...]))
        low = _tri((CHUNK, CHUNK), lambda r, c: r >= c).astype(BF16)
        acum = sum(_dot(low, part) for part in _split3(adt))
        dt_ref[...] = dt.T[:N_HEADS]
        ac_ref[...] = acum.T[:N_HEADS]

    return pl.pallas_call(
        body, name=name, grid=(nchunks,),
        in_specs=[pl.BlockSpec((CHUNK, 128), lambda i: (i, OFF_DT // 128)),
                  pl.BlockSpec((1, 128), lambda i: (0, 0)), pl.BlockSpec((1, 128), lambda i: (0, 0))],
        out_specs=[pl.BlockSpec((None, N_HEADS, CHUNK), lambda i: (i, 0, 0))] * 2,
        out_shape=[jax.ShapeDtypeStruct((nchunks, N_HEADS, CHUNK), F32)] * 2,
        compiler_params=_params("parallel"),
    )(proj, bias, alog)


def dt_bwd(ddtT, dacT, dtT, proj, dt_bias, a_log, t, name):
    nchunks = t // CHUNK
    bias = dt_bias.reshape(N_HEADS, 1)
    alog = a_log.reshape(N_HEADS, 1)

    def body(ddt_ref, dac_ref, dt_ref, raw_ref, b_ref, al_ref, draw_ref, da_ref, db_ref):
        a = -jnp.exp(al_ref[...])
        upp = _tri((CHUNK, CHUNK), lambda r, c: r >= c).astype(BF16)
        dadt = sum(_dot(part, upp) for part in _split3(dac_ref[...]))
        ddt = ddt_ref[...] + dadt * a
        rawT = raw_ref[...].astype(F32).T[:N_HEADS]
        draw = ddt * _sigmoid(rawT + b_ref[...])
        padded = jnp.concatenate([draw, jnp.zeros((128 - N_HEADS, CHUNK), F32)], axis=0)
        draw_ref[...] = padded.T.astype(BF16)
        dav = dadt * dt_ref[...]

        @pl.when(pl.program_id(0) == 0)
        def _():
            da_ref[...] = dav
            db_ref[...] = draw

        @pl.when(pl.program_id(0) > 0)
        def _():
            da_ref[...] += dav
            db_ref[...] += draw

    draw, da, db = pl.pallas_call(
        body, name=name, grid=(nchunks,),
        in_specs=[pl.BlockSpec((None, N_HEADS, CHUNK), lambda i: (i, 0, 0))] * 3
        + [pl.BlockSpec((CHUNK, 128), lambda i: (i, OFF_DT // 128)),
           pl.BlockSpec((N_HEADS, 1), lambda i: (0, 0)), pl.BlockSpec((N_HEADS, 1), lambda i: (0, 0))],
        out_specs=[pl.BlockSpec((CHUNK, 128), lambda i: (i, 0)),
                   pl.BlockSpec((N_HEADS, CHUNK), lambda i: (0, 0)), pl.BlockSpec((N_HEADS, CHUNK), lambda i: (0, 0))],
        out_shape=[jax.ShapeDtypeStruct((t, 128), BF16), jax.ShapeDtypeStruct((N_HEADS, CHUNK), F32),
                   jax.ShapeDtypeStruct((N_HEADS, CHUNK), F32)],
        compiler_params=_params("arbitrary"),
    )(ddtT, dacT, dtT, proj, bias, alog)
    return draw, jnp.sum(da, axis=1), jnp.sum(db, axis=1)


def _colb(r):
    return jnp.broadcast_to(r, (CHUNK, 128)).T


def _ssd_common(x_ref, b_ref, c_ref, dt_ref, ac_ref):
    x = x_ref[...].astype(F32)
    bm = b_ref[...]
    cm = c_ref[...]
    dtr = dt_ref[...]
    acr = ac_ref[...]
    lane = lax.broadcasted_iota(jnp.int32, (CHUNK, 128), 1)
    left = lane < 64
    ac_cols = [_colb(acr[e:e + 1]) for e in range(2)]
    dtl = jnp.where(left, _colb(dtr[0:1]), _colb(dtr[1:2]))
    acl = jnp.where(left, ac_cols[0], ac_cols[1])
    xdt = x * dtl
    cb = _dot(cm, bm, _NT)
    tri = _tri((CHUNK, CHUNK), lambda r, c: r >= c)
    dks = [jnp.exp(jnp.where(tri, ac_cols[e] - acr[e:e + 1], -1e30)) for e in range(2)]
    aclast = acl[CHUNK - 1:CHUNK, :]
    return x, bm, cm, left, dtl, acl, xdt, cb, dks, aclast


def ssd_fwd(xa, dtT, acT, dskip_l, bl, s, name):
    t = bl * s
    nc = s // CHUNK
    dt4 = dtT.reshape(bl * nc, N_PAIRS, 2, CHUNK)
    ac4 = acT.reshape(bl * nc, N_PAIRS, 2, CHUNK)

    def body(x_ref, b_ref, c_ref, dt_ref, ac_ref, dsk_ref, y_ref, prev_ref, st_ref):
        c = pl.program_id(1)
        hp = pl.program_id(2)

        @pl.when(c == 0)
        def _():
            st_ref[hp] = jnp.zeros((N_STATE, 128), F32)

        x, bm, cm, left, dtl, acl, xdt, cb, dks, aclast = _ssd_common(x_ref, b_ref, c_ref, dt_ref, ac_ref)
        xdt_b = xdt.astype(BF16)
        ys = [_dot((cb * dks[e]).astype(BF16), xdt_b) for e in range(2)]
        st = st_ref[hp]
        y_off = _dot(cm, st.astype(BF16)) * jnp.exp(acl)
        y_ref[...] = (jnp.where(left, ys[0], ys[1]) + y_off + x * dsk_ref[...]).astype(BF16)
        xw = (xdt * jnp.exp(aclast - acl)).astype(BF16)
        prev_ref[...] = st
        st_ref[hp] = st * jnp.exp(aclast) + _dot(bm, xw, _TN)

    row = lambda b, c, hp: b * nc + c
    return pl.pallas_call(
        body, name=name, grid=(bl, nc, N_PAIRS),
        in_specs=[pl.BlockSpec((CHUNK, 128), lambda b, c, hp: (row(b, c, hp), hp)),
                  pl.BlockSpec((CHUNK, 128), lambda b, c, hp: (row(b, c, hp), 16 + hp // 8)),
                  pl.BlockSpec((CHUNK, 128), lambda b, c, hp: (row(b, c, hp), 18 + hp // 8)),
                  pl.BlockSpec((None, None, 2, CHUNK), lambda b, c, hp: (row(b, c, hp), hp, 0, 0)),
                  pl.BlockSpec((None, None, 2, CHUNK), lambda b, c, hp: (row(b, c, hp), hp, 0, 0)),
                  pl.BlockSpec((1, 128), lambda b, c, hp: (0, hp))],
        out_specs=[pl.BlockSpec((CHUNK, 128), lambda b, c, hp: (row(b, c, hp), hp)),
                   pl.BlockSpec((None, None, N_STATE, 128), lambda b, c, hp: (row(b, c, hp), hp, 0, 0))],
        out_shape=[jax.ShapeDtypeStruct((t, SSM_W), BF16),
                   jax.ShapeDtypeStruct((bl * nc, N_PAIRS, N_STATE, 128), F32)],
        scratch_shapes=[pltpu.VMEM((N_PAIRS, N_STATE, 128), F32)],
        compiler_params=_params("parallel", "arbitrary", "arbitrary"),
    )(xa, xa, xa, dt4, ac4, dskip_l)


def ssd_bwd(dy, xa, dtT, acT, dskip_l, prev, bl, s, name):
    t = bl * s
    nc = s // CHUNK
    dt4 = dtT.reshape(bl * nc, N_PAIRS, 2, CHUNK)
    ac4 = acT.reshape(bl * nc, N_PAIRS, 2, CHUNK)

    def body(dy_ref, x_ref, b_ref, c_ref, dt_ref, ac_ref, dsk_ref, prev_ref,
             dx_ref, db_ref, dc_ref, dd_ref, dsk_out_ref, dp_ref):
        b = pl.program_id(0)
        cr = pl.program_id(1)
        hp = pl.program_id(2)

        @pl.when(cr == 0)
        def _():
            dp_ref[hp] = jnp.zeros((N_STATE, 128), F32)

        @pl.when((b == 0) & (cr == 0) & (hp == 0))
        def _():
            dsk_out_ref[...] = jnp.zeros(dsk_out_ref.shape, F32)

        x, bm, cm, left, dtl, acl, xdt, cb, dks, aclast = _ssd_common(x_ref, b_ref, c_ref, dt_ref, ac_ref)
        lane = lax.broadcasted_iota(jnp.int32, (CHUNK, 128), 1)
        dyv = dy_ref[...].astype(F32)
        dy_b = dyv.astype(BF16)
        xdt_b = xdt.astype(BF16)
        st = prev_ref[...]
        st_b = st.astype(BF16)
        ea = jnp.exp(acl)
        ds = jnp.exp(aclast - acl)
        cdl = jnp.exp(aclast)
        xw = xdt * ds
        masks = [left, jnp.logical_not(left)]

        dsk_out_ref[hp] = dsk_out_ref[hp] + jnp.sum(dyv * x, axis=0, keepdims=True)

        yo = _dot(cm, st_b)
        dyo_b = (dyv * ea).astype(BF16)
        yoff_term = dyv * yo * ea
        dc_acc = _dot(dyo_b, st_b, _NT)
        dst = _dot(cm, dyo_b, _TN)
        dsv = dp_ref[hp]
        dsv_b = dsv.astype(BF16)
        dxw = _dot(bm, dsv_b)
        db_acc = _dot(xw.astype(BF16), dsv_b, _NT)
        dxdt = dxw * ds
        qv = dxw * xw
        end_term = dsv * st * cdl
        dp_ref[hp] = dsv * cdl + dst

        dcb = jnp.zeros((CHUNK, CHUNK), F32)
        cols = jnp.zeros((CHUNK, 128), F32)
        rows = []
        for e in range(2):
            m = cb * dks[e]
            dy_e = jnp.where(masks[e], dyv, 0.0).astype(BF16)
            dm = _dot(dy_e, xdt_b, _NT)
            w = dm * m
            dcb = dcb + dm * dks[e]
            dxdt = dxdt + jnp.where(masks[e], _dot(m.astype(BF16), dy_b, _TN), 0.0)
            dac_col = (jnp.sum(w, axis=1, keepdims=True)
                       + jnp.sum(jnp.where(masks[e], yoff_term - qv, 0.0), axis=1, keepdims=True))
            cols = jnp.where(lane == 2 + e, dac_col, cols)
            tail = (jnp.sum(jnp.where(masks[e], qv, 0.0)) + jnp.sum(jnp.where(masks[e], end_term, 0.0)))
            lrow = lax.broadcasted_iota(jnp.int32, (1, CHUNK), 1)
            rows.append(jnp.where(lrow == CHUNK - 1, tail, 0.0) - jnp.sum(w, axis=0, keepdims=True))
        dcb_b = dcb.astype(BF16)
        dc_acc = dc_acc + _dot(dcb_b, bm)
        db_acc = db_acc + _dot(dcb_b, cm, _TN)
        dx_ref[...] = (dxdt * dtl + dyv * dsk_ref[...]).astype(BF16)
        ddt_l = dxdt * x
        for e in range(2):
            cols = jnp.where(lane == e, jnp.sum(jnp.where(masks[e], ddt_l, 0.0), axis=1, keepdims=True), cols)
        krow = lax.broadcasted_iota(jnp.int32, (8, CHUNK), 0)
        dd_ref[...] = cols.T[0:8] + jnp.where(krow == 2, rows[0], 0.0) + jnp.where(krow == 3, rows[1], 0.0)

        @pl.when(hp % 8 == 0)
        def _():
            db_ref[...] = db_acc
            dc_ref[...] = dc_acc

        @pl.when(hp % 8 != 0)
        def _():
            db_ref[...] += db_acc
            dc_ref[...] += dc_acc

    row = lambda b, c, hp: b * nc + (nc - 1 - c)
    dx, db, dc, dd, dsk = pl.pallas_call(
        body, name=name, grid=(bl, nc, N_PAIRS),
        in_specs=[pl.BlockSpec((CHUNK, 128), lambda b, c, hp: (row(b, c, hp), hp)),
                  pl.BlockSpec((CHUNK, 128), lambda b, c, hp: (row(b, c, hp), hp)),
                  pl.BlockSpec((CHUNK, 128), lambda b, c, hp: (row(b, c, hp), 16 + hp // 8)),
                  pl.BlockSpec((CHUNK, 128), lambda b, c, hp: (row(b, c, hp), 18 + hp // 8)),
                  pl.BlockSpec((None, None, 2, CHUNK), lambda b, c, hp: (row(b, c, hp), hp, 0, 0)),
                  pl.BlockSpec((None, None, 2, CHUNK), lambda b, c, hp: (row(b, c, hp), hp, 0, 0)),
                  pl.BlockSpec((1, 128), lambda b, c, hp: (0, hp)),
                  pl.BlockSpec((None, None, N_STATE, 128), lambda b, c, hp: (row(b, c, hp), hp, 0, 0))],
        out_specs=[pl.BlockSpec((CHUNK, 128), lambda b, c, hp: (row(b, c, hp), hp)),
                   pl.BlockSpec((CHUNK, 128), lambda b, c, hp: (row(b, c, hp), hp // 8)),
                   pl.BlockSpec((CHUNK, 128), lambda b, c, hp: (row(b, c, hp), hp // 8)),
                   pl.BlockSpec((None, None, 8, CHUNK), lambda b, c, hp: (row(b, c, hp), hp, 0, 0)),
                   pl.BlockSpec((N_PAIRS, 1, 128), lambda b, c, hp: (0, 0, 0))],
        out_shape=[jax.ShapeDtypeStruct((t, SSM_W), BF16),
                   jax.ShapeDtypeStruct((t, N_GROUPS * N_STATE), F32),
                   jax.ShapeDtypeStruct((t, N_GROUPS * N_STATE), F32),
                   jax.ShapeDtypeStruct((bl * nc, N_PAIRS, 8, CHUNK), F32),
                   jax.ShapeDtypeStruct((N_PAIRS, 1, 128), F32)],
        scratch_shapes=[pltpu.VMEM((N_PAIRS, N_STATE, 128), F32)],
        compiler_params=_params("arbitrary", "arbitrary", "arbitrary"),
    )(dy, xa, xa, xa, dt4, ac4, dskip_l, prev)
    ddtT = dd[:, :, 0:2, :].reshape(bl * nc, N_HEADS, CHUNK)
    dacT = dd[:, :, 2:4, :].reshape(bl * nc, N_HEADS, CHUNK)
    return dx, db, dc, ddtT, dacT, dsk.reshape(N_PAIRS, 128)


def gnorm_fwd(y, proj, w, name):
    t = y.shape[0]
    tr = min(256, t)
    zb = OFF_Z // SSM_W

    def body(y_ref, z_ref, w_ref, o_ref):
        z = z_ref[...].astype(F32)
        yg = y_ref[...].astype(F32) * z * _sigmoid(z)
        r = lax.rsqrt(jnp.mean(yg * yg, axis=-1, keepdims=True) + EPS)
        o_ref[...] = (yg * r * w_ref[...]).astype(BF16)

    return pl.pallas_call(
        body, name=name, grid=(t // tr,),
        in_specs=[pl.BlockSpec((tr, SSM_W), lambda i: (i, 0)), pl.BlockSpec((tr, SSM_W), lambda i: (i, zb)),
                  pl.BlockSpec((1, SSM_W), lambda i: (0, 0))],
        out_specs=pl.BlockSpec((tr, SSM_W), lambda i: (i, 0)),
        out_shape=jax.ShapeDtypeStruct((t, SSM_W), BF16),
        compiler_params=_params("parallel"),
    )(y, proj, w.reshape(1, SSM_W))


def gnorm_bwd(ds, y, proj, w, name):
    t = y.shape[0]
    tr = min(256, t)
    zb = OFF_Z // SSM_W

    def body(ds_ref, y_ref, z_ref, w_ref, dy_ref, dz_ref, dw_ref):
        z = z_ref[...].astype(F32)
        yv = y_ref[...].astype(F32)
        sg = _sigmoid(z)
        sz = z * sg
        yg = yv * sz
        r = lax.rsqrt(jnp.mean(yg * yg, axis=-1, keepdims=True) + EPS)
        xh = yg * r
        g = ds_ref[...].astype(F32)
        dxh = g * w_ref[...]
        dyg = r * (dxh - xh * jnp.mean(dxh * xh, axis=-1, keepdims=True))
        dy_ref[...] = (dyg * sz).astype(BF16)
        dz_ref[...] = (dyg * yv * sg * (1.0 + z * (1.0 - sg))).astype(BF16)
        part = jnp.sum(g * xh, axis=0, keepdims=True)

        @pl.when(pl.program_id(0) == 0)
        def _():
            dw_ref[...] = part

        @pl.when(pl.program_id(0) > 0)
        def _():
            dw_ref[...] += part

    return pl.pallas_call(
        body, name=name, grid=(t // tr,),
        in_specs=[pl.BlockSpec((tr, SSM_W), lambda i: (i, 0)), pl.BlockSpec((tr, SSM_W), lambda i: (i, 0)),
                  pl.BlockSpec((tr, SSM_W), lambda i: (i, zb)), pl.BlockSpec((1, SSM_W), lambda i: (0, 0))],
        out_specs=[pl.BlockSpec((tr, SSM_W), lambda i: (i, 0)), pl.BlockSpec((tr, SSM_W), lambda i: (i, 0)),
                   pl.BlockSpec((1, SSM_W), lambda i: (0, 0))],
        out_shape=[jax.ShapeDtypeStruct((t, SSM_W), BF16), jax.ShapeDtypeStruct((t, SSM_W), BF16),
                   jax.ShapeDtypeStruct((1, SSM_W), F32)],
        compiler_params=_params("arbitrary"),
    )(ds, y, proj, w.reshape(1, SSM_W))


def _pool_mixed(u, g, row):
    win = 2 << g
    acc = u
    for k in range(g + 1):
        acc = acc + _shift_down(acc, 1 << k, row)
    inv = 1.0 / jnp.minimum(row + 1, win).astype(F32)
    return acc * inv - u, inv


def pool_fwd(proj, pool_w, pool_scale, bl, s, name):
    t = bl * s

    def body(u_ref, g_ref, w_ref, sc_ref, o_ref):
        row = lax.broadcasted_iota(jnp.int32, (s, POOL_GD), 0)
        for g in range(POOL_G):
            cs = slice(g * POOL_GD, (g + 1) * POOL_GD)
            u = u_ref[:, cs].astype(F32)
            mixed, _ = _pool_mixed(u, g, row)
            pm = _dot(mixed.astype(BF16), w_ref[g])
            gate = g_ref[:, cs].astype(F32)
            o_ref[:, cs] = (pm * sc_ref[:, cs] * gate * _sigmoid(gate)).astype(BF16)

    return pl.pallas_call(
        body, name=name, grid=(bl,),
        in_specs=[pl.BlockSpec((s, POOL_W), lambda b: (b, OFF_PU // POOL_W)),
                  pl.BlockSpec((s, POOL_W), lambda b: (b, OFF_PG // POOL_W)),
                  pl.BlockSpec((POOL_G, POOL_GD, POOL_GD), lambda b: (0, 0, 0)),
                  pl.BlockSpec((1, POOL_W), lambda b: (0, 0))],
        out_specs=pl.BlockSpec((s, POOL_W), lambda b: (b, 0)),
        out_shape=jax.ShapeDtypeStruct((t, POOL_W), BF16),
        compiler_params=_params("parallel"),
    )(proj, proj, pool_w, pool_scale.reshape(1, POOL_W))


def pool_bwd(dp, proj, pool_w, pool_scale, bl, s, name):
    t = bl * s

    def body(dp_ref, u_ref, g_ref, w_ref, sc_ref, du_ref, dg_ref, dw_ref, dsc_ref):
        row = lax.broadcasted_iota(jnp.int32, (s, POOL_GD), 0)
        first = pl.program_id(0) == 0
        for g in range(POOL_G):
            cs = slice(g * POOL_GD, (g + 1) * POOL_GD)
            u = u_ref[:, cs].astype(F32)
            mixed, inv = _pool_mixed(u, g, row)
            mixed_b = mixed.astype(BF16)
            wg = w_ref[g]
            pm = _dot(mixed_b, wg)
            gate = g_ref[:, cs].astype(F32)
            sg = _sigmoid(gate)
            d = dp_ref[:, cs].astype(F32)
            sc = sc_ref[:, cs]
            dpm = (d * sc * gate * sg).astype(BF16)
            dg_ref[:, cs] = (d * pm * sc * sg * (1.0 + gate * (1.0 - sg))).astype(BF16)
            dsc = jnp.sum(d * pm * gate * sg, axis=0, keepdims=True)
            dwg = _dot(mixed_b, dpm, _TN)
            dmixed = _dot(dpm, wg, _NT)
            acc = dmixed * inv
            for k in range(g + 1):
                acc = acc + _shift_up(acc, 1 << k, row, s)
            du_ref[:, cs] = (acc - dmixed).astype(BF16)

            @pl.when(first)
            def _():
                dw_ref[g] = dwg
                dsc_ref[:, cs] = dsc

            @pl.when(jnp.logical_not(first))
            def _():
                dw_ref[g] = dw_ref[g] + dwg
                dsc_ref[:, cs] = dsc_ref[:, cs] + dsc

    return pl.pallas_call(
        body, name=name, grid=(bl,),
        in_specs=[pl.BlockSpec((s, POOL_W), lambda b: (b, 0)),
                  pl.BlockSpec((s, POOL_W), lambda b: (b, OFF_PU // POOL_W)),
                  pl.BlockSpec((s, POOL_W), lambda b: (b, OFF_PG // POOL_W)),
                  pl.BlockSpec((POOL_G, POOL_GD, POOL_GD), lambda b: (0, 0, 0)),
                  pl.BlockSpec((1, POOL_W), lambda b: (0, 0))],
        out_specs=[pl.BlockSpec((s, POOL_W), lambda b: (b, 0)), pl.BlockSpec((s, POOL_W), lambda b: (b, 0)),
                   pl.BlockSpec((POOL_G, POOL_GD, POOL_GD), lambda b: (0, 0, 0)),
                   pl.BlockSpec((1, POOL_W), lambda b: (0, 0))],
        out_shape=[jax.ShapeDtypeStruct((t, POOL_W), BF16), jax.ShapeDtypeStruct((t, POOL_W), BF16),
                   jax.ShapeDtypeStruct((POOL_G, POOL_GD, POOL_GD), F32), jax.ShapeDtypeStruct((1, POOL_W), F32)],
        compiler_params=_params("arbitrary"),
    )(dp, proj, proj, pool_w, pool_scale.reshape(1, POOL_W))


SB_SCALE = 64 ** -0.5


def _sb_scores(qe, kj, diag, rr):
    z = _dot(qe, kj, _NT) * SB_SCALE
    lb = jnp.minimum(z, 0.0) - jnp.log(1.0 + jnp.exp(-jnp.abs(z)))
    mask = jnp.logical_or(_tri((QB, QB), lambda r, c: r > c), jnp.logical_not(diag))
    lom = jnp.where(mask, lb - z, 0.0)
    upper = _tri((QB, QB), lambda r, c: r > c).astype(BF16)
    hi, lo = _split2(lom)
    later = _dot(hi, upper) + _dot(lo, upper) + rr
    return lb, lom, mask, later


def sb_fwd(proj, bl, s, name):
    t = bl * s
    nq = s // QB
    qb0, kb0, vb0, gb0 = OFF_QKV // 128, OFF_QKV // 128 + 8, OFF_QKV // 128 + 16, OFF_SBG // 128

    def body(q_ref, k_ref, v_ref, g_ref, og_ref, o_ref, r_ref):
        i = pl.program_id(2)
        q = q_ref[...]
        lane = lax.broadcasted_iota(jnp.int32, (QB, 128), 1)
        left = lane < 64
        outs = []
        rtile = q.astype(F32) * 0.0
        for e in range(2):
            qe = jnp.where(left if e == 0 else jnp.logical_not(left), q, jnp.zeros_like(q))

            def step(jj, carry):
                acc, rr, rt = carry
                j = i - jj
                rows = pl.ds(pl.multiple_of(j * QB, QB), QB)
                lb, lom, mask, later = _sb_scores(qe, k_ref[rows, :], jj == 0, rr)
                att = jnp.where(mask, jnp.exp(lb + later), 0.0)
                acc = acc + _dot(att.astype(BF16), v_ref[rows, :])
                rt = jnp.where(lane == e * 16 + j, rr, rt)
                return acc, rr + jnp.sum(lom, axis=1, keepdims=True), rt

            acc, _, rtile = lax.fori_loop(
                0, i + 1, step, (jnp.zeros((QB, 128), F32), jnp.zeros((QB, 1), F32), rtile))
            outs.append(acc)
        o = jnp.where(left, outs[0], outs[1])
        gate = g_ref[...].astype(F32)
        o_ref[...] = o.astype(BF16)
        og_ref[...] = (o * gate * _sigmoid(gate)).astype(BF16)
        r_ref[...] = rtile

    return pl.pallas_call(
        body, name=name, grid=(bl, SB_PAIRS, nq),
        in_specs=[pl.BlockSpec((QB, 128), lambda b, hp, i: (b * nq + i, qb0 + hp)),
                  pl.BlockSpec((s, 128), lambda b, hp, i: (b, kb0 + hp)),
                  pl.BlockSpec((s, 128), lambda b, hp, i: (b, vb0 + hp)),
                  pl.BlockSpec((QB, 128), lambda b, hp, i: (b * nq + i, gb0 + hp))],
        out_specs=[pl.BlockSpec((QB, 128), lambda b, hp, i: (b * nq + i, hp)),
                   pl.BlockSpec((QB, 128), lambda b, hp, i: (b * nq + i, hp)),
                   pl.BlockSpec((None, None, QB, 128), lambda b, hp, i: (b * nq + i, hp, 0, 0))],
        out_shape=[jax.ShapeDtypeStruct((t, SB_W), BF16), jax.ShapeDtypeStruct((t, SB_W), BF16),
                   jax.ShapeDtypeStruct((bl * nq, SB_PAIRS, QB, 128), F32)],
        compiler_params=_params("parallel", "parallel", "arbitrary"),
    )(proj, proj, proj, proj)


def sb_bwd(dsb, o, rsave, proj, bl, s, name):
    t = bl * s
    nq = s // QB
    qb0, kb0, vb0, gb0 = OFF_QKV // 128, OFF_QKV // 128 + 8, OFF_QKV // 128 + 16, OFF_SBG // 128

    def body(d_ref, o_ref, r_ref, q_ref, k_ref, v_ref, g_ref, dq_ref, dk_ref, dv_ref, dg_ref, dk_acc, dv_acc):
        i = pl.program_id(2)

        @pl.when(i == 0)
        def _():
            dk_acc[...] = jnp.zeros(dk_acc.shape, F32)
            dv_acc[...] = jnp.zeros(dv_acc.shape, F32)

        q = q_ref[...]
        gate = g_ref[...].astype(F32)
        sg = _sigmoid(gate)
        d = d_ref[...].astype(F32)
        dg_ref[...] = (d * o_ref[...].astype(F32) * sg * (1.0 + gate * (1.0 - sg))).astype(BF16)
        do = (d * gate * sg).astype(BF16)
        rtile = r_ref[...]
        lane = lax.broadcasted_iota(jnp.int32, (QB, 128), 1)
        left = lane < 64
        lower = _tri((QB, QB), lambda r, c: r < c).astype(BF16)
        dqs = []
        for e in range(2):
            me = left if e == 0 else jnp.logical_not(left)
            qe = jnp.where(me, q, jnp.zeros_like(q))
            doe = jnp.where(me, do, jnp.zeros_like(do))

            def step(j, carry):
                dq, gcar = carry
                rows = pl.ds(pl.multiple_of(j * QB, QB), QB)
                kj = k_ref[rows, :]
                vj = v_ref[rows, :]
                rr = jnp.sum(jnp.where(lane == e * 16 + j, rtile, 0.0), axis=1, keepdims=True)
                lb, lom, mask, later = _sb_scores(qe, kj, j == i, rr)
                att = jnp.where(mask, jnp.exp(lb + later), 0.0)
                datt = _dot(doe, vj, _NT)
                de = att * datt
                gpre = _dot(de.astype(BF16), lower) + gcar
                sig = jnp.exp(lb)
                dz = (jnp.where(mask, de * (1.0 - sig) - gpre * sig, 0.0) * SB_SCALE).astype(BF16)
                dq = dq + _dot(dz, kj)
                dk_acc[rows, :] = dk_acc[rows, :] + jnp.where(me, _dot(dz, qe, _TN), 0.0)
                dv_acc[rows, :] = dv_acc[rows, :] + jnp.where(me, _dot(att.astype(BF16), doe, _TN), 0.0)
                return dq, gcar + jnp.sum(de, axis=1, keepdims=True)

            dq, _ = lax.fori_loop(0, i + 1, step, (jnp.zeros((QB, 128), F32), jnp.zeros((QB, 1), F32)))
            dqs.append(dq)
        dq_ref[...] = jnp.where(left, dqs[0], dqs[1]).astype(BF16)

        @pl.when(i == nq - 1)
        def _():
            dk_ref[...] = dk_acc[...].astype(BF16)
            dv_ref[...] = dv_acc[...].astype(BF16)

    blk = lambda b, hp, i: (b * nq + i, hp)
    return pl.pallas_call(
        body, name=name, grid=(bl, SB_PAIRS, nq),
        in_specs=[pl.BlockSpec((QB, 128), blk), pl.BlockSpec((QB, 128), blk),
                  pl.BlockSpec((None, None, QB, 128), lambda b, hp, i: (b * nq + i, hp, 0, 0)),
                  pl.BlockSpec((QB, 128), lambda b, hp, i: (b * nq + i, qb0 + hp)),
                  pl.BlockSpec((s, 128), lambda b, hp, i: (b, kb0 + hp)),
                  pl.BlockSpec((s, 128), lambda b, hp, i: (b, vb0 + hp)),
                  pl.BlockSpec((QB, 128), lambda b, hp, i: (b * nq + i, gb0 + hp))],
        out_specs=[pl.BlockSpec((QB, 128), blk),
                   pl.BlockSpec((s, 128), lambda b, hp, i: (b, hp)),
                   pl.BlockSpec((s, 128), lambda b, hp, i: (b, hp)),
                   pl.BlockSpec((QB, 128), blk)],
        out_shape=[jax.ShapeDtypeStruct((t, SB_W), BF16)] * 4,
        scratch_shapes=[pltpu.VMEM((s, 128), F32), pltpu.VMEM((s, 128), F32)],
        compiler_params=_params("parallel", "parallel", "arbitrary"),
    )(dsb, o, rsave, proj, proj, proj, proj)


def merge_fwd(proj, ys, yp, yb, name):
    t = ys.shape[0]
    tr = min(512, t)

    def body(m_ref, ys_ref, yp_ref, yb_ref, o_ref):
        acc = jnp.zeros((tr, D), F32)
        for k, ref in enumerate((ys_ref, yp_ref, yb_ref)):
            acc = acc + _sigmoid(m_ref[:, k * D:(k + 1) * D].astype(F32)) * ref[...].astype(F32)
        o_ref[...] = acc.astype(BF16)

    rowblk = pl.BlockSpec((tr, D), lambda i: (i, 0))
    return pl.pallas_call(
        body, name=name, grid=(t // tr,),
        in_specs=[pl.BlockSpec((tr, 3 * D), lambda i: (i, 0)), rowblk, rowblk, rowblk],
        out_specs=rowblk,
        out_shape=jax.ShapeDtypeStruct((t, D), BF16),
        compiler_params=_params("parallel"),
    )(proj, ys, yp, yb)


def merge_bwd(dm, proj, ys, yp, yb, name):
    t = ys.shape[0]
    tr = min(512, t)

    def body(dm_ref, m_ref, ys_ref, yp_ref, yb_ref, d0_ref, d1_ref, d2_ref, dl_ref):
        dmv = dm_ref[...].astype(F32)
        for k, (ref, dref) in enumerate(((ys_ref, d0_ref), (yp_ref, d1_ref), (yb_ref, d2_ref))):
            g = _sigmoid(m_ref[:, k * D:(k + 1) * D].astype(F32))
            dref[...] = (g * dmv).astype(BF16)
            dl_ref[:, k * D:(k + 1) * D] = (dmv * ref[...].astype(F32) * g * (1.0 - g)).astype(BF16)

    rowblk = pl.BlockSpec((tr, D), lambda i: (i, 0))
    wide = pl.BlockSpec((tr, 3 * D), lambda i: (i, 0))
    return pl.pallas_call(
        body, name=name, grid=(t // tr,),
        in_specs=[rowblk, wide, rowblk, rowblk, rowblk],
        out_specs=[rowblk, rowblk, rowblk, wide],
        out_shape=[jax.ShapeDtypeStruct((t, D), BF16)] * 3 + [jax.ShapeDtypeStruct((t, 3 * D), BF16)],
        compiler_params=_params("parallel"),
    )(dm, proj, ys, yp, yb)


def layer_fwd(x, lw, bl, s, tag):
    t = bl * s
    h = rmsnorm_fwd(x, lw["norm_w"], f"norm_fwd{tag}")
    proj = matmul(h, lw["w_in"], "nn", BF16, f"in_proj{tag}", tn=2048)
    xa = conv_fwd(proj, lw["conv_w"], lw["conv_b"], bl, s, f"conv_fwd{tag}")
    dtT, acT = dt_fwd(proj, lw["dt_bias"], lw["a_log"], t, f"dt_fwd{tag}")
    dskip_l = jnp.repeat(lw["d_skip"], 64).reshape(1, SSM_W)
    y, prev = ssd_fwd(xa, dtT, acT, dskip_l, bl, s, f"ssd_fwd{tag}")
    s_out = gnorm_fwd(y, proj, lw["ssm_norm_w"], f"gnorm_fwd{tag}")
    p_out = pool_fwd(proj, lw["pool_w"], lw["pool_scale"], bl, s, f"pool_fwd{tag}")
    sb_out, sb_o, sb_r = sb_fwd(proj, bl, s, f"sb_fwd{tag}")
    ys = matmul(s_out, lw["w_proj_ssm"], "nn", BF16, f"proj_ssm{tag}")
    yp = matmul(p_out, lw["w_proj_pool"], "nn", BF16, f"proj_pool{tag}")
    yb = matmul(sb_out, lw["w_proj_sb"], "nn", BF16, f"proj_sb{tag}")
    merged = merge_fwd(proj, ys, yp, yb, f"merge_fwd{tag}")
    x_next = matmul(merged, lw["w_out"], "nn", F32, f"out_proj{tag}", residual=x)
    saved = dict(x=x, h=h, proj=proj, xa=xa, dtT=dtT, acT=acT, y=y, prev=prev, s_out=s_out, p_out=p_out,
                 sb_out=sb_out, sb_o=sb_o, sb_r=sb_r, ys=ys, yp=yp, yb=yb, merged=merged)
    return x_next, saved


def layer_bwd(dx, lw, sv, bl, s, tag):
    t = bl * s
    g = {}
    dx_b = dx.astype(BF16)
    dmerged = matmul(dx_b, lw["w_out"], "nt", BF16, f"d_merged{tag}")
    g["w_out"] = matmul(sv["merged"], dx_b, "tn", F32, f"dw_out{tag}")
    dys, dyp, dyb, dlogit = merge_bwd(dmerged, sv["proj"], sv["ys"], sv["yp"], sv["yb"], f"merge_bwd{tag}")
    ds_out = matmul(dys, lw["w_proj_ssm"], "nt", BF16, f"d_sout{tag}")
    g["w_proj_ssm"] = matmul(sv["s_out"], dys, "tn", F32, f"dw_proj_ssm{tag}")
    dp_out = matmul(dyp, lw["w_proj_pool"], "nt", BF16, f"d_pout{tag}")
    g["w_proj_pool"] = matmul(sv["p_out"], dyp, "tn", F32, f"dw_proj_pool{tag}")
    dsb_out = matmul(dyb, lw["w_proj_sb"], "nt", BF16, f"d_sbout{tag}")
    g["w_proj_sb"] = matmul(sv["sb_out"], dyb, "tn", F32, f"dw_proj_sb{tag}")
    dy, dz, dnw = gnorm_bwd(ds_out, sv["y"], sv["proj"], lw["ssm_norm_w"], f"gnorm_bwd{tag}")
    g["ssm_norm_w"] = dnw[0]
    dskip_l = jnp.repeat(lw["d_skip"], 64).reshape(1, SSM_W)
    dxs, db, dc, ddtT, dacT, dsk = ssd_bwd(dy, sv["xa"], sv["dtT"], sv["acT"], dskip_l, sv["prev"], bl, s, f"ssd_bwd{tag}")
    g["d_skip"] = jnp.sum(dsk.reshape(N_HEADS, 64), axis=1)
    ddt_raw, da, dbias = dt_bwd(ddtT, dacT, sv["dtT"], sv["proj"], lw["dt_bias"], lw["a_log"], t, f"dt_bwd{tag}")
    g["a_log"] = da * (-jnp.exp(lw["a_log"]))
    g["dt_bias"] = dbias
    dxa = jnp.concatenate([dxs, db.astype(BF16), dc.astype(BF16)], axis=1)
    dxbc, dcw, dcb = conv_bwd(dxa, sv["proj"], lw["conv_w"], lw["conv_b"], bl, s, f"conv_bwd{tag}")
    g["conv_w"] = dcw
    g["conv_b"] = dcb
    dpu, dpg, dpw, dpsc = pool_bwd(dp_out, sv["proj"], lw["pool_w"], lw["pool_scale"], bl, s, f"pool_bwd{tag}")
    g["pool_w"] = dpw
    g["pool_scale"] = dpsc[0]
    dq, dk, dv, dsbg = sb_bwd(dsb_out, sv["sb_o"], sv["sb_r"], sv["proj"], bl, s, f"sb_bwd{tag}")
    dproj = jnp.concatenate(
        [dlogit, dsbg, dpu, dpg, dz, dq, dk, dv, dxbc, ddt_raw, jnp.zeros((t, PC - OFF_DT - 128), BF16)], axis=1)
    g["w_in"] = matmul(sv["h"], dproj, "tn", F32, f"dw_in{tag}", tn=2048)
    dh = matmul(dproj, lw["w_in"], "nt", F32, f"d_h{tag}")
    dx_in, dnorm = rmsnorm_bwd(dh, sv["x"], lw["norm_w"], dx, f"norm_bwd{tag}")
    g["norm_w"] = dnorm[0]
    return dx_in, g


def pad_w_in(w):
    z, xbc, dt, pu, pg, qkv, sbg, mg = jnp.split(w, [2048, 4608, 4640, 5664, 6688, 9760, 10784], axis=1)
    return jnp.concatenate([mg, sbg, pu, pg, z, qkv, xbc, dt, jnp.zeros((w.shape[0], PC - IN_COLS), w.dtype)], axis=1)


def unpad_w_in(wp):
    pieces = [(OFF_Z, 2048), (OFF_XBC, 2560), (OFF_DT, 32), (OFF_PU, 1024), (OFF_PG, 1024), (OFF_QKV, 3072),
              (OFF_SBG, 1024), (OFF_MERGE, 3072)]
    return jnp.concatenate([wp[:, o:o + n] for o, n in pieces], axis=1)


MESH = pl.DeviceIdType.MESH
ANY = pl.BlockSpec(memory_space=pl.ANY)


def _coords():
    return lax.axis_index("x"), lax.axis_index("y"), lax.axis_index("c")


def _peer(p):
    x, y, c = _coords()
    return (1 - x if p & 4 else x, 1 - y if p & 2 else y, 1 - c if p & 1 else c)


def _flat(pos):
    return 4 * pos[0] + 2 * pos[1] + pos[2]


def exchange(v, gather, name):
    out_shape = (N_DEV,) + tuple(v.shape) if gather else tuple(v.shape)

    def body(v_ref, out_ref, send_sems, recv_sems, local_sem):
        me = _flat(_coords())
        src_of = (lambda k: v_ref) if gather else (lambda k: v_ref.at[k])

        def copy(p, landing):
            peer = _peer(p)
            return pltpu.make_async_remote_copy(
                src_ref=src_of(_flat(peer)), dst_ref=out_ref.at[landing], send_sem=send_sems.at[p - 1],
                recv_sem=recv_sems.at[p - 1], device_id=peer, device_id_type=MESH)

        local = pltpu.make_async_copy(src_of(me), out_ref.at[me], local_sem)
        local.start()
        sends = [copy(p, me) for p in range(1, N_DEV)]
        for cp in sends:
            cp.start()
        for p in range(1, N_DEV):
            copy(p, _flat(_peer(p))).wait_recv()
        for cp in sends:
            cp.wait_send()
        local.wait()

    return pl.pallas_call(
        body, name=name,
        in_specs=[ANY], out_specs=ANY,
        out_shape=jax.ShapeDtypeStruct(out_shape, v.dtype),
        scratch_shapes=[pltpu.SemaphoreType.DMA((N_DEV - 1,)), pltpu.SemaphoreType.DMA((N_DEV - 1,)),
                        pltpu.SemaphoreType.DMA],
    )(v)


def sum_slabs(v, name):
    _, r, c = v.shape
    tr = 128 if r % 128 == 0 else r

    def body(v_ref, o_ref):
        acc = v_ref[0].astype(F32)
        for k in range(1, N_DEV):
            acc = acc + v_ref[k].astype(F32)
        o_ref[...] = acc

    return pl.pallas_call(
        body, name=name, grid=(r // tr,),
        in_specs=[pl.BlockSpec((N_DEV, tr, c), lambda i: (0, i, 0))],
        out_specs=pl.BlockSpec((tr, c), lambda i: (i, 0)),
        out_shape=jax.ShapeDtypeStruct((r, c), F32),
        compiler_params=_params("parallel"),
    )(v)


def adamw(w, g, m, v, name):
    r, c = w.shape
    tr = next((cand for cand in (256, 128, 64, 32, 16, 8) if r % cand == 0), r)

    def body(w_ref, g_ref, m_ref, v_ref, d_ref, mo_ref, vo_ref):
        gv = g_ref[...]
        mn = ADAM_B1 * m_ref[...] + (1.0 - ADAM_B1) * gv
        vn = ADAM_B2 * v_ref[...] + (1.0 - ADAM_B2) * (gv * gv)
        m_hat = mn / (1.0 - ADAM_B1 ** ADAM_STEP)
        v_hat = vn / (1.0 - ADAM_B2 ** ADAM_STEP)
        d_ref[...] = -ADAM_LR * (m_hat / (jnp.sqrt(v_hat) + ADAM_EPS) + ADAM_WD * w_ref[...])
        mo_ref[...] = mn
        vo_ref[...] = vn

    blk = pl.BlockSpec((tr, c), lambda i: (i, 0))
    return pl.pallas_call(
        body, name=name, grid=(r // tr,),
        in_specs=[blk] * 4, out_specs=[blk] * 3,
        out_shape=[jax.ShapeDtypeStruct((r, c), F32)] * 3,
        compiler_params=_params("parallel"),
    )(w, g, m, v)


BIG = ("w_in", "w_proj_ssm", "w_proj_pool", "w_proj_sb", "w_out", "pool_w")
BIG_SHARD = {"w_in": (D, IN_COLS // N_DEV), "w_proj_ssm": (SSM_W // N_DEV, D), "w_proj_pool": (POOL_W // N_DEV, D),
             "w_proj_sb": (SB_W // N_DEV, D), "w_out": (D // N_DEV, D), "pool_w": (POOL_G, POOL_GD // N_DEV, POOL_GD)}
PACK_C = 1024
PACK_R = 2432
PACK_N = PACK_R * PACK_C

REPLICATED = ("norm_w", "conv_b", "dt_bias", "a_log", "d_skip", "ssm_norm_w", "pool_scale")
WEIGHTS = ("norm_w", "w_in", "conv_w", "conv_b", "dt_bias", "a_log", "d_skip", "ssm_norm_w", "pool_w",
           "pool_scale", "w_proj_ssm", "w_proj_pool", "w_proj_sb", "w_out", "final_norm_w")


def _size(shape):
    n = 1
    for d in shape:
        n *= d
    return n


def _pad_flat(flat, n):
    return jnp.concatenate([flat, jnp.zeros((n - flat.shape[0],), flat.dtype)])


def pack_shards(parts):
    flat = jnp.concatenate([parts[n].reshape(-1) for n in BIG])
    return _pad_flat(flat, PACK_N).reshape(PACK_R, PACK_C)


def unpack_shards(packed):
    flat = packed.reshape(-1)
    out, off = {}, 0
    for n in BIG:
        size = _size(BIG_SHARD[n])
        out[n] = flat[off:off + size].reshape(BIG_SHARD[n])
        off += size
    return out


def unpack_gathered(g):
    flat = g.reshape(N_DEV, PACK_N)
    out, off = {}, 0
    for n in BIG:
        size = _size(BIG_SHARD[n])
        seg = flat[:, off:off + size]
        off += size
        if n == "w_in":
            w = seg.reshape(N_DEV, D, IN_COLS // N_DEV).transpose(1, 0, 2).reshape(D, IN_COLS)
            out[n] = pad_w_in(w)
        elif n == "pool_w":
            out[n] = seg.reshape(N_DEV, POOL_G, POOL_GD // N_DEV, POOL_GD).transpose(1, 0, 2, 3).reshape(
                POOL_G, POOL_GD, POOL_GD)
        else:
            out[n] = seg.reshape(N_DEV * BIG_SHARD[n][0], D)
    return out


def pack_slabs(g):
    segs = []
    for n in BIG:
        if n == "w_in":
            w = unpad_w_in(g[n]).reshape(D, N_DEV, IN_COLS // N_DEV).transpose(1, 0, 2)
        elif n == "pool_w":
            w = g[n].reshape(POOL_G, N_DEV, POOL_GD // N_DEV, POOL_GD).transpose(1, 0, 2, 3)
        else:
            w = g[n]
        segs.append(w.reshape(N_DEV, -1).astype(BF16))
    flat = jnp.concatenate(segs, axis=1)
    flat = jnp.concatenate([flat, jnp.zeros((N_DEV, PACK_N - flat.shape[1]), BF16)], axis=1)
    return flat.reshape(N_DEV, PACK_R, PACK_C)


SMALL_ROWS = 544


def pack_small(vals):
    flat = jnp.concatenate([v.reshape(-1) for v in vals])
    return _pad_flat(flat, SMALL_ROWS * 128).reshape(SMALL_ROWS, 128)


def unpack_small(packed, shapes):
    flat = packed.reshape(-1)
    out, off = [], 0
    for shp in shapes:
        out.append(flat[off:off + _size(shp)].reshape(shp))
        off += _size(shp)
    return out


def kernel(x, norm_w, w_in, conv_w, conv_b, dt_bias, a_log, d_skip, ssm_norm_w, pool_w, pool_scale, w_proj_ssm, w_proj_pool, w_proj_sb, w_out, final_norm_w, loss_target, m_norm_w, m_w_in, m_conv_w, m_conv_b, m_dt_bias, m_a_log, m_d_skip, m_ssm_norm_w, m_pool_w, m_pool_scale, m_w_proj_ssm, m_w_proj_pool, m_w_proj_sb, m_w_out, m_final_norm_w, v_norm_w, v_w_in, v_conv_w, v_conv_b, v_dt_bias, v_a_log, v_d_skip, v_ssm_norm_w, v_pool_w, v_pool_scale, v_w_proj_ssm, v_w_proj_pool, v_w_proj_sb, v_w_out, v_final_norm_w):
    wts = dict(norm_w=norm_w, w_in=w_in, conv_w=conv_w, conv_b=conv_b, dt_bias=dt_bias, a_log=a_log, d_skip=d_skip,
               ssm_norm_w=ssm_norm_w, pool_w=pool_w, pool_scale=pool_scale, w_proj_ssm=w_proj_ssm,
               w_proj_pool=w_proj_pool, w_proj_sb=w_proj_sb, w_out=w_out, final_norm_w=final_norm_w)
    mom = dict(norm_w=m_norm_w, w_in=m_w_in, conv_w=m_conv_w, conv_b=m_conv_b, dt_bias=m_dt_bias, a_log=m_a_log,
               d_skip=m_d_skip, ssm_norm_w=m_ssm_norm_w, pool_w=m_pool_w, pool_scale=m_pool_scale,
               w_proj_ssm=m_w_proj_ssm, w_proj_pool=m_w_proj_pool, w_proj_sb=m_w_proj_sb, w_out=m_w_out,
               final_norm_w=m_final_norm_w)
    var = dict(norm_w=v_norm_w, w_in=v_w_in, conv_w=v_conv_w, conv_b=v_conv_b, dt_bias=v_dt_bias, a_log=v_a_log,
               d_skip=v_d_skip, ssm_norm_w=v_ssm_norm_w, pool_w=v_pool_w, pool_scale=v_pool_scale,
               w_proj_ssm=v_w_proj_ssm, w_proj_pool=v_w_proj_pool, w_proj_sb=v_w_proj_sb, w_out=v_w_out,
               final_norm_w=v_final_norm_w)
    bl, s, _ = x.shape
    t = bl * s
    me = _flat(_coords())

    cw = exchange(conv_w.reshape(40, 128), True, "gather_conv_w")
    conv_w_full = cw.reshape(N_DEV, DEPTH, CONV_K, CONV_CH // N_DEV).transpose(1, 2, 0, 3).reshape(
        DEPTH, CONV_K, CONV_CH)

    xc = x.reshape(t, D)
    layer_w, saved = [], []
    for l in range(DEPTH):
        packed = pack_shards({n: wts[n][l].astype(BF16) for n in BIG})
        lw = unpack_gathered(exchange(packed, True, f"gather_w{l}"))
        for n in REPLICATED:
            lw[n] = wts[n][l]
        lw["conv_w"] = conv_w_full[l]
        xc, sv = layer_fwd(xc, lw, bl, s, f"_l{l}")
        layer_w.append(lw)
        saved.append(sv)

    loss_part, dx, dfinal = final_loss(xc, final_norm_w, loss_target.reshape(t, D), "final_loss")
    loss = lax.psum(loss_part[0, 0], ("x", "y", "c"))

    grads = [None] * DEPTH
    big_sum = [None] * DEPTH
    for l in reversed(range(DEPTH)):
        dx, g = layer_bwd(dx, layer_w[l], saved[l], bl, s, f"_l{l}")
        grads[l] = g
        got = exchange(pack_slabs(g), False, f"scatter_g{l}")
        big_sum[l] = unpack_shards(sum_slabs(got, f"sum_g{l}"))
    grad_x = dx.reshape(bl, s, D)

    small_names = REPLICATED + ("conv_w",)
    small_vals = [jnp.stack([grads[l][n] for l in range(DEPTH)]) for n in small_names] + [dfinal[0]]
    small_shapes = [v.shape for v in small_vals]
    small_all = exchange(pack_small(small_vals), True, "gather_small")
    small_sum = unpack_small(sum_slabs(small_all, "sum_small"), small_shapes)
    gsum = dict(zip(small_names + ("final_norm_w",), small_sum))
    conv_g_full = gsum["conv_w"]
    gsum["conv_w"] = lax.dynamic_slice_in_dim(conv_g_full, me * (CONV_CH // N_DEV), CONV_CH // N_DEV, axis=2)
    for n in BIG:
        gsum[n] = jnp.stack([big_sum[l][n] for l in range(DEPTH)])

    delta, new_m, new_v = {}, {}, {}
    for n in BIG + ("conv_w",):
        shp = wts[n].shape
        two_d = (-1, shp[-1])
        d2, m2, v2 = adamw(wts[n].reshape(two_d), gsum[n].reshape(two_d), mom[n].reshape(two_d),
                           var[n].reshape(two_d), f"adamw_{n}")
        delta[n], new_m[n], new_v[n] = d2.reshape(shp), m2.reshape(shp), v2.reshape(shp)
    rep = REPLICATED + ("final_norm_w",)
    rep_shapes = [wts[n].shape for n in rep]
    d2, m2, v2 = adamw(pack_small([wts[n] for n in rep]), pack_small([gsum[n] for n in rep]),
                       pack_small([mom[n] for n in rep]), pack_small([var[n] for n in rep]), "adamw_small")
    for n, dv, mv, vv in zip(rep, unpack_small(d2, rep_shapes), unpack_small(m2, rep_shapes),
                             unpack_small(v2, rep_shapes)):
        delta[n], new_m[n], new_v[n] = dv, mv, vv

    return (loss, grad_x, *[gsum[n] for n in WEIGHTS], *[delta[n] for n in WEIGHTS],
            *[new_m[n] for n in WEIGHTS], *[new_v[n] for n in WEIGHTS])
```

```python
import functools

import jax
import jax.numpy as jnp
from jax import lax
from jax.experimental import pallas as pl
from jax.experimental.pallas import tpu as pltpu

F32 = jnp.float32
BF16 = jnp.bfloat16

N_DEV = 8
DEPTH = 4
D = 1024
SSM_W = 2048
N_HEADS = 32
N_PAIRS = 16
N_GROUPS = 2
N_STATE = 128
CHUNK = 128
CONV_CH = 2560
CONV_K = 4
POOL_W = 1024
POOL_G = 4
POOL_GD = 256
SB_W = 1024
SB_PAIRS = 8
QB = 256
EPS = 1e-6
IN_COLS = 13856

PC = 14336
OFF_MERGE = 0
OFF_SBG = 3072
OFF_PU = 4096
OFF_PG = 5120
OFF_Z = 6144
OFF_QKV = 8192
OFF_XBC = 11264
OFF_DT = 13824

ADAM_LR = 0.001
ADAM_B1 = 0.9
ADAM_B2 = 0.999
ADAM_EPS = 1e-08
ADAM_WD = 0.01
ADAM_STEP = 10

VMEM_LIMIT = 56 * 1024 * 1024

_NN = (((1,), (0,)), ((), ()))
_NT = (((1,), (1,)), ((), ()))
_TN = (((0,), (0,)), ((), ()))


def _dot(a, b, dn=_NN):
    return lax.dot_general(a, b, dn, preferred_element_type=F32)


def _sigmoid(x):
    return 1.0 / (1.0 + jnp.exp(-x))


def _softplus(x):
    return jnp.maximum(x, 0.0) + jnp.log(1.0 + jnp.exp(-jnp.abs(x)))


def _split2(x):
    hi = x.astype(BF16)
    lo = (x - hi.astype(F32)).astype(BF16)
    return hi, lo


def _split3(x):
    hi = x.astype(BF16)
    r = x - hi.astype(F32)
    mid = r.astype(BF16)
    lo = (r - mid.astype(F32)).astype(BF16)
    return hi, mid, lo


def _params(*sem):
    return pltpu.CompilerParams(dimension_semantics=sem, vmem_limit_bytes=VMEM_LIMIT)


def matmul(a, b, mode, out_dtype, name, residual=None, tm=1024, tn=1024, tk=1024):
    if mode == "nn":
        (m, k), (k2, n) = a.shape, b.shape
    elif mode == "nt":
        (m, k), (n, k2) = a.shape, b.shape
    else:
        (k, m), (k2, n) = a.shape, b.shape
    assert k == k2
    tm, tn, tk = min(tm, m), min(tn, n), min(tk, k)
    assert m % tm == 0 and n % tn == 0 and k % tk == 0
    nk = k // tk
    dn = {"nn": _NN, "nt": _NT, "tn": _TN}[mode]
    a_spec = pl.BlockSpec((tk, tm), lambda i, j, kk: (kk, i)) if mode == "tn" else pl.BlockSpec((tm, tk), lambda i, j, kk: (i, kk))
    b_spec = pl.BlockSpec((tn, tk), lambda i, j, kk: (j, kk)) if mode == "nt" else pl.BlockSpec((tk, tn), lambda i, j, kk: (kk, j))
    in_specs = [a_spec, b_spec]
    args = [a, b]
    if residual is not None:
        in_specs.append(pl.BlockSpec((tm, tn), lambda i, j, kk: (i, j)))
        args.append(residual)

    def body(*refs):
        if residual is not None:
            a_ref, b_ref, r_ref, o_ref, acc_ref = refs
        else:
            a_ref, b_ref, o_ref, acc_ref = refs
            r_ref = None
        kk = pl.program_id(2)
        p = _dot(a_ref[...], b_ref[...], dn)

        def finish(val):
            if r_ref is not None:
                val = val + r_ref[...]
            o_ref[...] = val.astype(out_dtype)

        if nk == 1:
            finish(p)
        else:
            @pl.when(kk == 0)
            def _():
                acc_ref[...] = p

            @pl.when(kk > 0)
            def _():
                acc_ref[...] += p

            @pl.when(kk == nk - 1)
            def _():
                finish(acc_ref[...])

    return pl.pallas_call(
        body, name=name,
        grid=(m // tm, n // tn, nk),
        in_specs=in_specs,
        out_specs=pl.BlockSpec((tm, tn), lambda i, j, kk: (i, j)),
        out_shape=jax.ShapeDtypeStruct((m, n), out_dtype),
        scratch_shapes=[pltpu.VMEM((tm, tn) if nk > 1 else (8, 128), F32)],
        compiler_params=_params("parallel", "parallel", "arbitrary"),
    )(*args)


def rmsnorm_fwd(x, w, name):
    t, d = x.shape
    tr = min(512, t)

    def body(x_ref, w_ref, h_ref):
        xv = x_ref[...]
        r = lax.rsqrt(jnp.mean(xv * xv, axis=-1, keepdims=True) + EPS)
        h_ref[...] = (xv * r * w_ref[...]).astype(BF16)

    return pl.pallas_call(
        body, name=name, grid=(t // tr,),
        in_specs=[pl.BlockSpec((tr, d), lambda i: (i, 0)), pl.BlockSpec((1, d), lambda i: (0, 0))],
        out_specs=pl.BlockSpec((tr, d), lambda i: (i, 0)),
        out_shape=jax.ShapeDtypeStruct((t, d), BF16),
        compiler_params=_params("parallel"),
    )(x, w.reshape(1, d))


def rmsnorm_bwd(dh, x, w, dres, name):
    t, d = x.shape
    tr = min(512, t)

    def body(dh_ref, x_ref, w_ref, dres_ref, dx_ref, dw_ref):
        xv = x_ref[...]
        r = lax.rsqrt(jnp.mean(xv * xv, axis=-1, keepdims=True) + EPS)
        xh = xv * r
        g = dh_ref[...].astype(F32)
        dxh = g * w_ref[...]
        dx_ref[...] = dres_ref[...] + r * (dxh - xh * jnp.mean(dxh * xh, axis=-1, keepdims=True))
        part = jnp.sum(g * xh, axis=0, keepdims=True)

        @pl.when(pl.program_id(0) == 0)
        def _():
            dw_ref[...] = part

        @pl.when(pl.program_id(0) > 0)
        def _():
            dw_ref[...] += part

    return pl.pallas_call(
        body, name=name, grid=(t // tr,),
        in_specs=[pl.BlockSpec((tr, d), lambda i: (i, 0)), pl.BlockSpec((tr, d), lambda i: (i, 0)),
                  pl.BlockSpec((1, d), lambda i: (0, 0)), pl.BlockSpec((tr, d), lambda i: (i, 0))],
        out_specs=[pl.BlockSpec((tr, d), lambda i: (i, 0)), pl.BlockSpec((1, d), lambda i: (0, 0))],
        out_shape=[jax.ShapeDtypeStruct((t, d), F32), jax.ShapeDtypeStruct((1, d), F32)],
        compiler_params=_params("arbitrary"),
    )(dh, x, w.reshape(1, d), dres)


def final_loss(x, w, target, name):
    t, d = x.shape
    tr = min(512, t)

    def body(x_ref, w_ref, tg_ref, loss_ref, dx_ref, dw_ref):
        xv = x_ref[...]
        r = lax.rsqrt(jnp.mean(xv * xv, axis=-1, keepdims=True) + EPS)
        xh = xv * r
        err = xh * w_ref[...] - tg_ref[...]
        lpart = 0.5 * jnp.sum(jnp.mean(err * err, axis=-1, keepdims=True), axis=0, keepdims=True)
        dy = err * (1.0 / d)
        dxh = dy * w_ref[...]
        dx_ref[...] = r * (dxh - xh * jnp.mean(dxh * xh, axis=-1, keepdims=True))
        part = jnp.sum(dy * xh, axis=0, keepdims=True)

        @pl.when(pl.program_id(0) == 0)
        def _():
            dw_ref[...] = part
            loss_ref[...] = jnp.broadcast_to(lpart, (1, 128))

        @pl.when(pl.program_id(0) > 0)
        def _():
            dw_ref[...] += part
            loss_ref[...] += jnp.broadcast_to(lpart, (1, 128))

    return pl.pallas_call(
        body, name=name, grid=(t // tr,),
        in_specs=[pl.BlockSpec((tr, d), lambda i: (i, 0)), pl.BlockSpec((1, d), lambda i: (0, 0)),
                  pl.BlockSpec((tr, d), lambda i: (i, 0))],
        out_specs=[pl.BlockSpec((1, 128), lambda i: (0, 0)), pl.BlockSpec((tr, d), lambda i: (i, 0)),
                   pl.BlockSpec((1, d), lambda i: (0, 0))],
        out_shape=[jax.ShapeDtypeStruct((1, 128), F32), jax.ShapeDtypeStruct((t, d), F32),
                   jax.ShapeDtypeStruct((1, d), F32)],
        compiler_params=_params("arbitrary"),
    )(x, w.reshape(1, d), target)


CONV_BW = 256


def _shift_down(u, s, row):
    return jnp.where(row >= s, pltpu.roll(u, s, axis=0), 0.0)


def _shift_up(u, s, row, n):
    return jnp.where(row < n - s, pltpu.roll(u, n - s, axis=0), 0.0)


def _conv_pre(u, w, b, row):
    acc = b + w[CONV_K - 1:CONV_K, :] * u
    for k in range(CONV_K - 1):
        acc = acc + w[k:k + 1, :] * _shift_down(u, CONV_K - 1 - k, row)
    return acc


def conv_fwd(proj, conv_w, conv_b, bl, s, name):
    t = bl * s
    nb = CONV_CH // CONV_BW
    off = OFF_XBC // CONV_BW

    def body(u_ref, w_ref, b_ref, o_ref):
        u = u_ref[...].astype(F32)
        row = lax.broadcasted_iota(jnp.int32, u.shape, 0)
        xc = _conv_pre(u, w_ref[...], b_ref[...], row)
        o_ref[...] = (xc * _sigmoid(xc)).astype(BF16)

    return pl.pallas_call(
        body, name=name, grid=(bl, nb),
        in_specs=[pl.BlockSpec((s, CONV_BW), lambda b, j: (b, off + j)),
                  pl.BlockSpec((CONV_K, CONV_BW), lambda b, j: (0, j)),
                  pl.BlockSpec((1, CONV_BW), lambda b, j: (0, j))],
        out_specs=pl.BlockSpec((s, CONV_BW), lambda b, j: (b, j)),
        out_shape=jax.ShapeDtypeStruct((t, CONV_CH), BF16),
        compiler_params=_params("parallel", "parallel"),
    )(proj, conv_w, conv_b.reshape(1, CONV_CH))


def conv_bwd(dxa, proj, conv_w, conv_b, bl, s, name):
    t = bl * s
    nb = CONV_CH // CONV_BW
    off = OFF_XBC // CONV_BW

    def body(d_ref, u_ref, w_ref, b_ref, du_ref, dw_ref, db_ref):
        u = u_ref[...].astype(F32)
        w = w_ref[...]
        row = lax.broadcasted_iota(jnp.int32, u.shape, 0)
        xc = _conv_pre(u, w, b_ref[...], row)
        sg = _sigmoid(xc)
        dxc = d_ref[...].astype(F32) * sg * (1.0 + xc * (1.0 - sg))
        du = w[CONV_K - 1:CONV_K, :] * dxc
        dws = [None] * CONV_K
        dws[CONV_K - 1] = jnp.sum(dxc * u, axis=0, keepdims=True)
        for k in range(CONV_K - 1):
            sh = CONV_K - 1 - k
            du = du + w[k:k + 1, :] * _shift_up(dxc, sh, row, s)
            dws[k] = jnp.sum(dxc * _shift_down(u, sh, row), axis=0, keepdims=True)
        du_ref[...] = du.astype(BF16)
        krow = lax.broadcasted_iota(jnp.int32, (8, CONV_BW), 0)
        dwv = sum(jnp.where(krow == k, dws[k], 0.0) for k in range(CONV_K))
        dbv = jnp.sum(dxc, axis=0, keepdims=True)

        @pl.when(pl.program_id(1) == 0)
        def _():
            dw_ref[...] = dwv
            db_ref[...] = dbv

        @pl.when(pl.program_id(1) > 0)
        def _():
            dw_ref[...] += dwv
            db_ref[...] += dbv

    du, dw, db = pl.pallas_call(
        body, name=name, grid=(nb, bl),
        in_specs=[pl.BlockSpec((s, CONV_BW), lambda j, b: (b, j)),
                  pl.BlockSpec((s, CONV_BW), lambda j, b: (b, off + j)),
                  pl.BlockSpec((CONV_K, CONV_BW), lambda j, b: (0, j)),
                  pl.BlockSpec((1, CONV_BW), lambda j, b: (0, j))],
        out_specs=[pl.BlockSpec((s, CONV_BW), lambda j, b: (b, j)),
                   pl.BlockSpec((8, CONV_BW), lambda j, b: (0, j)),
                   pl.BlockSpec((1, CONV_BW), lambda j, b: (0, j))],
        out_shape=[jax.ShapeDtypeStruct((t, CONV_CH), BF16), jax.ShapeDtypeStruct((8, CONV_CH), F32),
                   jax.ShapeDtypeStruct((1, CONV_CH), F32)],
        compiler_params=_params("parallel", "arbitrary"),
    )(dxa, proj, conv_w, conv_b.reshape(1, CONV_CH))
    return du, dw[:CONV_K], db[0]


def _tri(shape, cmp):
    r = lax.broadcasted_iota(jnp.int32, shape, 0)
    c = lax.broadcasted_iota(jnp.int32, shape, 1)
    return cmp(r, c)


def dt_fwd(proj, dt_bias, a_log, t, name):
    nchunks = t // CHUNK
    bias = jnp.zeros((1, 128), F32).at[0, :N_HEADS].set(dt_bias)
    alog = jnp.zeros((1, 128), F32).at[0, :N_HEADS].set(a_log)

    def body(raw_ref, b_ref, al_ref, dt_ref, ac_ref):
        raw = raw_ref[...].astype(F32)
        dt = _softplus(raw + b_ref[...])
        adt = dt * (-jnp.exp(al_ref[...]))
        low = _tri((CHUNK, CHUNK), lambda r, c: r >= c).astype(BF16)
        acum = sum(_dot(low, part) for part in _split3(adt))
        dt_ref[...] = dt.T[:N_HEADS]
        ac_ref[...] = acum.T[:N_HEADS]

    return pl.pallas_call(
        body, name=name, grid=(nchunks,),
        in_specs=[pl.BlockSpec((CHUNK, 128), lambda i: (i, OFF_DT // 128)),
                  pl.BlockSpec((1, 128), lambda i: (0, 0)), pl.BlockSpec((1, 128), lambda i: (0, 0))],
        out_specs=[pl.BlockSpec((None, N_HEADS, CHUNK), lambda i: (i, 0, 0))] * 2,
        out_shape=[jax.ShapeDtypeStruct((nchunks, N_HEADS, CHUNK), F32)] * 2,
        compiler_params=_params("parallel"),
    )(proj, bias, alog)


def dt_bwd(ddtT, dacT, dtT, proj, dt_bias, a_log, t, name):
    nchunks = t // CHUNK
    bias = dt_bias.reshape(N_HEADS, 1)
    alog = a_log.reshape(N_HEADS, 1)

    def body(ddt_ref, dac_ref, dt_ref, raw_ref, b_ref, al_ref, draw_ref, da_ref, db_ref):
        a = -jnp.exp(al_ref[...])
        upp = _tri((CHUNK, CHUNK), lambda r, c: r >= c).astype(BF16)
        dadt = sum(_dot(part, upp) for part in _split3(dac_ref[...]))
        ddt = ddt_ref[...] + dadt * a
        rawT = raw_ref[...].astype(F32).T[:N_HEADS]
        draw = ddt * _sigmoid(rawT + b_ref[...])
        padded = jnp.concatenate([draw, jnp.zeros((128 - N_HEADS, CHUNK), F32)], axis=0)
        draw_ref[...] = padded.T.astype(BF16)
        dav = dadt * dt_ref[...]

        @pl.when(pl.program_id(0) == 0)
        def _():
            da_ref[...] = dav
            db_ref[...] = draw

        @pl.when(pl.program_id(0) > 0)
        def _():
            da_ref[...] += dav
            db_ref[...] += draw

    draw, da, db = pl.pallas_call(
        body, name=name, grid=(nchunks,),
        in_specs=[pl.BlockSpec((None, N_HEADS, CHUNK), lambda i: (i, 0, 0))] * 3
        + [pl.BlockSpec((CHUNK, 128), lambda i: (i, OFF_DT // 128)),
           pl.BlockSpec((N_HEADS, 1), lambda i: (0, 0)), pl.BlockSpec((N_HEADS, 1), lambda i: (0, 0))],
        out_specs=[pl.BlockSpec((CHUNK, 128), lambda i: (i, 0)),
                   pl.BlockSpec((N_HEADS, CHUNK), lambda i: (0, 0)), pl.BlockSpec((N_HEADS, CHUNK), lambda i: (0, 0))],
        out_shape=[jax.ShapeDtypeStruct((t, 128), BF16), jax.ShapeDtypeStruct((N_HEADS, CHUNK), F32),
                   jax.ShapeDtypeStruct((N_HEADS, CHUNK), F32)],
        compiler_params=_params("arbitrary"),
    )(ddtT, dacT, dtT, proj, bias, alog)
    return draw, jnp.sum(da, axis=1), jnp.sum(db, axis=1)


def _colb(r):
    return jnp.broadcast_to(r, (CHUNK, 128)).T


def _ssd_common(x_ref, b_ref, c_ref, dt_ref, ac_ref):
    x = x_ref[...].astype(F32)
    bm = b_ref[...]
    cm = c_ref[...]
    dtr = dt_ref[...]
    acr = ac_ref[...]
    lane = lax.broadcasted_iota(jnp.int32, (CHUNK, 128), 1)
    left = lane < 64
    ac_cols = [_colb(acr[e:e + 1]) for e in range(2)]
    dtl = jnp.where(left, _colb(dtr[0:1]), _colb(dtr[1:2]))
    acl = jnp.where(left, ac_cols[0], ac_cols[1])
    xdt = x * dtl
    cb = _dot(cm, bm, _NT)
    tri = _tri((CHUNK, CHUNK), lambda r, c: r >= c)
    dks = [jnp.exp(jnp.where(tri, ac_cols[e] - acr[e:e + 1], -1e30)) for e in range(2)]
    aclast = acl[CHUNK - 1:CHUNK, :]
    return x, bm, cm, left, dtl, acl, xdt, cb, dks, aclast


def ssd_fwd(xa, dtT, acT, dskip_l, bl, s, name):
    t = bl * s
    nc = s // CHUNK
    dt4 = dtT.reshape(bl * nc, N_PAIRS, 2, CHUNK)
    ac4 = acT.reshape(bl * nc, N_PAIRS, 2, CHUNK)

    def body(x_ref, b_ref, c_ref, dt_ref, ac_ref, dsk_ref, y_ref, prev_ref, st_ref):
        c = pl.program_id(1)
        hp = pl.program_id(2)

        @pl.when(c == 0)
        def _():
            st_ref[hp] = jnp.zeros((N_STATE, 128), F32)

        x, bm, cm, left, dtl, acl, xdt, cb, dks, aclast = _ssd_common(x_ref, b_ref, c_ref, dt_ref, ac_ref)
        xdt_b = xdt.astype(BF16)
        ys = [_dot((cb * dks[e]).astype(BF16), xdt_b) for e in range(2)]
        st = st_ref[hp]
        y_off = _dot(cm, st.astype(BF16)) * jnp.exp(acl)
        y_ref[...] = (jnp.where(left, ys[0], ys[1]) + y_off + x * dsk_ref[...]).astype(BF16)
        xw = (xdt * jnp.exp(aclast - acl)).astype(BF16)
        prev_ref[...] = st
        st_ref[hp] = st * jnp.exp(aclast) + _dot(bm, xw, _TN)

    row = lambda b, c, hp: b * nc + c
    return pl.pallas_call(
        body, name=name, grid=(bl, nc, N_PAIRS),
        in_specs=[pl.BlockSpec((CHUNK, 128), lambda b, c, hp: (row(b, c, hp), hp)),
                  pl.BlockSpec((CHUNK, 128), lambda b, c, hp: (row(b, c, hp), 16 + hp // 8)),
                  pl.BlockSpec((CHUNK, 128), lambda b, c, hp: (row(b, c, hp), 18 + hp // 8)),
                  pl.BlockSpec((None, None, 2, CHUNK), lambda b, c, hp: (row(b, c, hp), hp, 0, 0)),
                  pl.BlockSpec((None, None, 2, CHUNK), lambda b, c, hp: (row(b, c, hp), hp, 0, 0)),
                  pl.BlockSpec((1, 128), lambda b, c, hp: (0, hp))],
        out_specs=[pl.BlockSpec((CHUNK, 128), lambda b, c, hp: (row(b, c, hp), hp)),
                   pl.BlockSpec((None, None, N_STATE, 128), lambda b, c, hp: (row(b, c, hp), hp, 0, 0))],
        out_shape=[jax.ShapeDtypeStruct((t, SSM_W), BF16),
                   jax.ShapeDtypeStruct((bl * nc, N_PAIRS, N_STATE, 128), F32)],
        scratch_shapes=[pltpu.VMEM((N_PAIRS, N_STATE, 128), F32)],
        compiler_params=_params("parallel", "arbitrary", "arbitrary"),
    )(xa, xa, xa, dt4, ac4, dskip_l)


def ssd_bwd(dy, xa, dtT, acT, dskip_l, prev, bl, s, name):
    t = bl * s
    nc = s // CHUNK
    dt4 = dtT.reshape(bl * nc, N_PAIRS, 2, CHUNK)
    ac4 = acT.reshape(bl * nc, N_PAIRS, 2, CHUNK)

    def body(dy_ref, x_ref, b_ref, c_ref, dt_ref, ac_ref, dsk_ref, prev_ref,
             dx_ref, db_ref, dc_ref, dd_ref, dsk_out_ref, dp_ref):
        b = pl.program_id(0)
        cr = pl.program_id(1)
        hp = pl.program_id(2)

        @pl.when(cr == 0)
        def _():
            dp_ref[hp] = jnp.zeros((N_STATE, 128), F32)

        @pl.when((b == 0) & (cr == 0) & (hp == 0))
        def _():
            dsk_out_ref[...] = jnp.zeros(dsk_out_ref.shape, F32)

        x, bm, cm, left, dtl, acl, xdt, cb, dks, aclast = _ssd_common(x_ref, b_ref, c_ref, dt_ref, ac_ref)
        lane = lax.broadcasted_iota(jnp.int32, (CHUNK, 128), 1)
        dyv = dy_ref[...].astype(F32)
        dy_b = dyv.astype(BF16)
        xdt_b = xdt.astype(BF16)
        st = prev_ref[...]
        st_b = st.astype(BF16)
        ea = jnp.exp(acl)
        ds = jnp.exp(aclast - acl)
        cdl = jnp.exp(aclast)
        xw = xdt * ds
        masks = [left, jnp.logical_not(left)]

        dsk_out_ref[hp] = dsk_out_ref[hp] + jnp.sum(dyv * x, axis=0, keepdims=True)

        yo = _dot(cm, st_b)
        dyo_b = (dyv * ea).astype(BF16)
        yoff_term = dyv * yo * ea
        dc_acc = _dot(dyo_b, st_b, _NT)
        dst = _dot(cm, dyo_b, _TN)
        dsv = dp_ref[hp]
        dsv_b = dsv.astype(BF16)
        dxw = _dot(bm, dsv_b)
        db_acc = _dot(xw.astype(BF16), dsv_b, _NT)
        dxdt = dxw * ds
        qv = dxw * xw
        end_term = dsv * st * cdl
        dp_ref[hp] = dsv * cdl + dst

        dcb = jnp.zeros((CHUNK, CHUNK), F32)
        cols = jnp.zeros((CHUNK, 128), F32)
        rows = []
        for e in range(2):
            m = cb * dks[e]
            dy_e = jnp.where(masks[e], dyv, 0.0).astype(BF16)
            dm = _dot(dy_e, xdt_b, _NT)
            w = dm * m
            dcb = dcb + dm * dks[e]
            dxdt = dxdt + jnp.where(masks[e], _dot(m.astype(BF16), dy_b, _TN), 0.0)
            dac_col = (jnp.sum(w, axis=1, keepdims=True)
                       + jnp.sum(jnp.where(masks[e], yoff_term - qv, 0.0), axis=1, keepdims=True))
            cols = jnp.where(lane == 2 + e, dac_col, cols)
            tail = (jnp.sum(jnp.where(masks[e], qv, 0.0)) + jnp.sum(jnp.where(masks[e], end_term, 0.0)))
            lrow = lax.broadcasted_iota(jnp.int32, (1, CHUNK), 1)
            rows.append(jnp.where(lrow == CHUNK - 1, tail, 0.0) - jnp.sum(w, axis=0, keepdims=True))
        dcb_b = dcb.astype(BF16)
        dc_acc = dc_acc + _dot(dcb_b, bm)
        db_acc = db_acc + _dot(dcb_b, cm, _TN)
        dx_ref[...] = (dxdt * dtl + dyv * dsk_ref[...]).astype(BF16)
        ddt_l = dxdt * x
        for e in range(2):
            cols = jnp.where(lane == e, jnp.sum(jnp.where(masks[e], ddt_l, 0.0), axis=1, keepdims=True), cols)
        krow = lax.broadcasted_iota(jnp.int32, (8, CHUNK), 0)
        dd_ref[...] = cols.T[0:8] + jnp.where(krow == 2, rows[0], 0.0) + jnp.where(krow == 3, rows[1], 0.0)

        @pl.when(hp % 8 == 0)
        def _():
            db_ref[...] = db_acc
            dc_ref[...] = dc_acc

        @pl.when(hp % 8 != 0)
        def _():
            db_ref[...] += db_acc
            dc_ref[...] += dc_acc

    row = lambda b, c, hp: b * nc + (nc - 1 - c)
    dx, db, dc, dd, dsk = pl.pallas_call(
        body, name=name, grid=(bl, nc, N_PAIRS),
        in_specs=[pl.BlockSpec((CHUNK, 128), lambda b, c, hp: (row(b, c, hp), hp)),
                  pl.BlockSpec((CHUNK, 128), lambda b, c, hp: (row(b, c, hp), hp)),
                  pl.BlockSpec((CHUNK, 128), lambda b, c, hp: (row(b, c, hp), 16 + hp // 8)),
                  pl.BlockSpec((CHUNK, 128), lambda b, c, hp: (row(b, c, hp), 18 + hp // 8)),
                  pl.BlockSpec((None, None, 2, CHUNK), lambda b, c, hp: (row(b, c, hp), hp, 0, 0)),
                  pl.BlockSpec((None, None, 2, CHUNK), lambda b, c, hp: (row(b, c, hp), hp, 0, 0)),
                  pl.BlockSpec((1, 128), lambda b, c, hp: (0, hp)),
                  pl.BlockSpec((None, None, N_STATE, 128), lambda b, c, hp: (row(b, c, hp), hp, 0, 0))],
        out_specs=[pl.BlockSpec((CHUNK, 128), lambda b, c, hp: (row(b, c, hp), hp)),
                   pl.BlockSpec((CHUNK, 128), lambda b, c, hp: (row(b, c, hp), hp // 8)),
                   pl.BlockSpec((CHUNK, 128), lambda b, c, hp: (row(b, c, hp), hp // 8)),
                   pl.BlockSpec((None, None, 8, CHUNK), lambda b, c, hp: (row(b, c, hp), hp, 0, 0)),
                   pl.BlockSpec((N_PAIRS, 1, 128), lambda b, c, hp: (0, 0, 0))],
        out_shape=[jax.ShapeDtypeStruct((t, SSM_W), BF16),
                   jax.ShapeDtypeStruct((t, N_GROUPS * N_STATE), F32),
                   jax.ShapeDtypeStruct((t, N_GROUPS * N_STATE), F32),
                   jax.ShapeDtypeStruct((bl * nc, N_PAIRS, 8, CHUNK), F32),
                   jax.ShapeDtypeStruct((N_PAIRS, 1, 128), F32)],
        scratch_shapes=[pltpu.VMEM((N_PAIRS, N_STATE, 128), F32)],
        compiler_params=_params("arbitrary", "arbitrary", "arbitrary"),
    )(dy, xa, xa, xa, dt4, ac4, dskip_l, prev)
    ddtT = dd[:, :, 0:2, :].reshape(bl * nc, N_HEADS, CHUNK)
    dacT = dd[:, :, 2:4, :].reshape(bl * nc, N_HEADS, CHUNK)
    return dx, db, dc, ddtT, dacT, dsk.reshape(N_PAIRS, 128)


def gnorm_fwd(y, proj, w, name):
    t = y.shape[0]
    tr = min(256, t)
    zb = OFF_Z // SSM_W

    def body(y_ref, z_ref, w_ref, o_ref):
        z = z_ref[...].astype(F32)
        yg = y_ref[...].astype(F32) * z * _sigmoid(z)
        r = lax.rsqrt(jnp.mean(yg * yg, axis=-1, keepdims=True) + EPS)
        o_ref[...] = (yg * r * w_ref[...]).astype(BF16)

    return pl.pallas_call(
        body, name=name, grid=(t // tr,),
        in_specs=[pl.BlockSpec((tr, SSM_W), lambda i: (i, 0)), pl.BlockSpec((tr, SSM_W), lambda i: (i, zb)),
                  pl.BlockSpec((1, SSM_W), lambda i: (0, 0))],
        out_specs=pl.BlockSpec((tr, SSM_W), lambda i: (i, 0)),
        out_shape=jax.ShapeDtypeStruct((t, SSM_W), BF16),
        compiler_params=_params("parallel"),
    )(y, proj, w.reshape(1, SSM_W))


def gnorm_bwd(ds, y, proj, w, name):
    t = y.shape[0]
    tr = min(256, t)
    zb = OFF_Z // SSM_W

    def body(ds_ref, y_ref, z_ref, w_ref, dy_ref, dz_ref, dw_ref):
        z = z_ref[...].astype(F32)
        yv = y_ref[...].astype(F32)
        sg = _sigmoid(z)
        sz = z * sg
        yg = yv * sz
        r = lax.rsqrt(jnp.mean(yg * yg, axis=-1, keepdims=True) + EPS)
        xh = yg * r
        g = ds_ref[...].astype(F32)
        dxh = g * w_ref[...]
        dyg = r * (dxh - xh * jnp.mean(dxh * xh, axis=-1, keepdims=True))
        dy_ref[...] = (dyg * sz).astype(BF16)
        dz_ref[...] = (dyg * yv * sg * (1.0 + z * (1.0 - sg))).astype(BF16)
        part = jnp.sum(g * xh, axis=0, keepdims=True)

        @pl.when(pl.program_id(0) == 0)
        def _():
            dw_ref[...] = part

        @pl.when(pl.program_id(0) > 0)
        def _():
            dw_ref[...] += part

    return pl.pallas_call(
        body, name=name, grid=(t // tr,),
        in_specs=[pl.BlockSpec((tr, SSM_W), lambda i: (i, 0)), pl.BlockSpec((tr, SSM_W), lambda i: (i, 0)),
                  pl.BlockSpec((tr, SSM_W), lambda i: (i, zb)), pl.BlockSpec((1, SSM_W), lambda i: (0, 0))],
        out_specs=[pl.BlockSpec((tr, SSM_W), lambda i: (i, 0)), pl.BlockSpec((tr, SSM_W), lambda i: (i, 0)),
                   pl.BlockSpec((1, SSM_W), lambda i: (0, 0))],
        out_shape=[jax.ShapeDtypeStruct((t, SSM_W), BF16), jax.ShapeDtypeStruct((t, SSM_W), BF16),
                   jax.ShapeDtypeStruct((1, SSM_W), F32)],
        compiler_params=_params("arbitrary"),
    )(ds, y, proj, w.reshape(1, SSM_W))


def _pool_mixed(u, g, row):
    win = 2 << g
    acc = u
    for k in range(g + 1):
        acc = acc + _shift_down(acc, 1 << k, row)
    inv = 1.0 / jnp.minimum(row + 1, win).astype(F32)
    return acc * inv - u, inv


def pool_fwd(proj, pool_w, pool_scale, bl, s, name):
    t = bl * s

    def body(u_ref, g_ref, w_ref, sc_ref, o_ref):
        row = lax.broadcasted_iota(jnp.int32, (s, POOL_GD), 0)
        for g in range(POOL_G):
            cs = slice(g * POOL_GD, (g + 1) * POOL_GD)
            u = u_ref[:, cs].astype(F32)
            mixed, _ = _pool_mixed(u, g, row)
            pm = _dot(mixed.astype(BF16), w_ref[g])
            gate = g_ref[:, cs].astype(F32)
            o_ref[:, cs] = (pm * sc_ref[:, cs] * gate * _sigmoid(gate)).astype(BF16)

    return pl.pallas_call(
        body, name=name, grid=(bl,),
        in_specs=[pl.BlockSpec((s, POOL_W), lambda b: (b, OFF_PU // POOL_W)),
                  pl.BlockSpec((s, POOL_W), lambda b: (b, OFF_PG // POOL_W)),
                  pl.BlockSpec((POOL_G, POOL_GD, POOL_GD), lambda b: (0, 0, 0)),
                  pl.BlockSpec((1, POOL_W), lambda b: (0, 0))],
        out_specs=pl.BlockSpec((s, POOL_W), lambda b: (b, 0)),
        out_shape=jax.ShapeDtypeStruct((t, POOL_W), BF16),
        compiler_params=_params("parallel"),
    )(proj, proj, pool_w, pool_scale.reshape(1, POOL_W))


def pool_bwd(dp, proj, pool_w, pool_scale, bl, s, name):
    t = bl * s

    def body(dp_ref, u_ref, g_ref, w_ref, sc_ref, du_ref, dg_ref, dw_ref, dsc_ref):
        row = lax.broadcasted_iota(jnp.int32, (s, POOL_GD), 0)
        first = pl.program_id(0) == 0
        for g in range(POOL_G):
            cs = slice(g * POOL_GD, (g + 1) * POOL_GD)
            u = u_ref[:, cs].astype(F32)
            mixed, inv = _pool_mixed(u, g, row)
            mixed_b = mixed.astype(BF16)
            wg = w_ref[g]
            pm = _dot(mixed_b, wg)
            gate = g_ref[:, cs].astype(F32)
            sg = _sigmoid(gate)
            d = dp_ref[:, cs].astype(F32)
            sc = sc_ref[:, cs]
            dpm = (d * sc * gate * sg).astype(BF16)
            dg_ref[:, cs] = (d * pm * sc * sg * (1.0 + gate * (1.0 - sg))).astype(BF16)
            dsc = jnp.sum(d * pm * gate * sg, axis=0, keepdims=True)
            dwg = _dot(mixed_b, dpm, _TN)
            dmixed = _dot(dpm, wg, _NT)
            acc = dmixed * inv
            for k in range(g + 1):
                acc = acc + _shift_up(acc, 1 << k, row, s)
            du_ref[:, cs] = (acc - dmixed).astype(BF16)

            @pl.when(first)
            def _():
                dw_ref[g] = dwg
                dsc_ref[:, cs] = dsc

            @pl.when(jnp.logical_not(first))
            def _():
                dw_ref[g] = dw_ref[g] + dwg
                dsc_ref[:, cs] = dsc_ref[:, cs] + dsc

    return pl.pallas_call(
        body, name=name, grid=(bl,),
        in_specs=[pl.BlockSpec((s, POOL_W), lambda b: (b, 0)),
                  pl.BlockSpec((s, POOL_W), lambda b: (b, OFF_PU // POOL_W)),
                  pl.BlockSpec((s, POOL_W), lambda b: (b, OFF_PG // POOL_W)),
                  pl.BlockSpec((POOL_G, POOL_GD, POOL_GD), lambda b: (0, 0, 0)),
                  pl.BlockSpec((1, POOL_W), lambda b: (0, 0))],
        out_specs=[pl.BlockSpec((s, POOL_W), lambda b: (b, 0)), pl.BlockSpec((s, POOL_W), lambda b: (b, 0)),
                   pl.BlockSpec((POOL_G, POOL_GD, POOL_GD), lambda b: (0, 0, 0)),
                   pl.BlockSpec((1, POOL_W), lambda b: (0, 0))],
        out_shape=[jax.ShapeDtypeStruct((t, POOL_W), BF16), jax.ShapeDtypeStruct((t, POOL_W), BF16),
                   jax.ShapeDtypeStruct((POOL_G, POOL_GD, POOL_GD), F32), jax.ShapeDtypeStruct((1, POOL_W), F32)],
        compiler_params=_params("arbitrary"),
    )(dp, proj, proj, pool_w, pool_scale.reshape(1, POOL_W))


SB_SCALE = 64 ** -0.5


KB = 256


def _sb_block(qe, kj, mask, rr, upper):
    z = _dot(qe, kj, _NT)
    lb = jnp.minimum(z, 0.0) - jnp.log(1.0 + jnp.exp(-jnp.abs(z)))
    lom = jnp.where(mask, lb - z, 0.0)
    later = _dot(lom.astype(BF16), upper) + rr
    return lb, lom, later


def _sb_masks(i):
    lane = lax.broadcasted_iota(jnp.int32, (QB, 128), 1)
    row = lax.broadcasted_iota(jnp.int32, (2 * QB, KB), 0) % QB
    col = lax.broadcasted_iota(jnp.int32, (2 * QB, KB), 1)
    causal = lambda jb: col + (jb * KB - i * QB) < row
    return lane, lane < 64, causal


def _stack_heads(x, left):
    zero = jnp.zeros_like(x)
    return jnp.concatenate([jnp.where(left, x, zero), jnp.where(left, zero, x)], axis=0)


def sb_fwd(proj, bl, s, name):
    t = bl * s
    nq = s // QB
    qb0, kb0, vb0, gb0 = OFF_QKV // 128, OFF_QKV // 128 + 8, OFF_QKV // 128 + 16, OFF_SBG // 128

    def body(q_ref, k_ref, v_ref, g_ref, og_ref, o_ref, r_ref):
        i = pl.program_id(2)
        qs = q_ref[...] * SB_SCALE
        lane, left, causal = _sb_masks(i)
        qcat = _stack_heads(qs, left)
        upper = _tri((KB, KB), lambda r, c: r > c).astype(BF16)
        zero = qcat.astype(F32) * 0.0

        def step(jj, carry):
            acc, rr, rt = carry
            jb = i - jj
            rows = pl.ds(pl.multiple_of(jb * KB, KB), KB)
            mask = causal(jb)
            lb, lom, later = _sb_block(qcat, k_ref[rows, :], mask, rr, upper)
            att = jnp.where(mask, jnp.exp(lb + later), 0.0)
            acc = acc + _dot(att.astype(BF16), v_ref[rows, :])
            rt = jnp.where(lane == jb, rr[:QB], jnp.where(lane == 8 + jb, rr[QB:], rt))
            return acc, rr + jnp.sum(lom, axis=1, keepdims=True), rt

        acc, _, rtile = lax.fori_loop(0, i + 1, step, (zero, zero[:, :1], zero[:QB]))
        o = jnp.where(left, acc[:QB], acc[QB:])
        gate = g_ref[...].astype(F32)
        o_ref[...] = o.astype(BF16)
        og_ref[...] = (o * gate * _sigmoid(gate)).astype(BF16)
        r_ref[...] = rtile

    return pl.pallas_call(
        body, name=name, grid=(bl, SB_PAIRS, nq),
        in_specs=[pl.BlockSpec((QB, 128), lambda b, hp, i: (b * nq + i, qb0 + hp)),
                  pl.BlockSpec((s, 128), lambda b, hp, i: (b, kb0 + hp)),
                  pl.BlockSpec((s, 128), lambda b, hp, i: (b, vb0 + hp)),
                  pl.BlockSpec((QB, 128), lambda b, hp, i: (b * nq + i, gb0 + hp))],
        out_specs=[pl.BlockSpec((QB, 128), lambda b, hp, i: (b * nq + i, hp)),
                   pl.BlockSpec((QB, 128), lambda b, hp, i: (b * nq + i, hp)),
                   pl.BlockSpec((None, None, QB, 128), lambda b, hp, i: (b * nq + i, hp, 0, 0))],
        out_shape=[jax.ShapeDtypeStruct((t, SB_W), BF16), jax.ShapeDtypeStruct((t, SB_W), BF16),
                   jax.ShapeDtypeStruct((bl * nq, SB_PAIRS, QB, 128), F32)],
        compiler_params=_params("parallel", "parallel", "arbitrary"),
    )(proj, proj, proj, proj)


def sb_bwd(dsb, o, rsave, proj, bl, s, name):
    t = bl * s
    nq = s // QB
    qb0, kb0, vb0, gb0 = OFF_QKV // 128, OFF_QKV // 128 + 8, OFF_QKV // 128 + 16, OFF_SBG // 128

    def body(d_ref, o_ref, r_ref, q_ref, k_ref, v_ref, g_ref, dq_ref, dk_ref, dv_ref, dg_ref, dk_acc, dv_acc):
        i = pl.program_id(2)

        @pl.when(i == 0)
        def _():
            dk_acc[...] = jnp.zeros(dk_acc.shape, F32)
            dv_acc[...] = jnp.zeros(dv_acc.shape, F32)

        qs = q_ref[...] * SB_SCALE
        gate = g_ref[...].astype(F32)
        sg = _sigmoid(gate)
        d = d_ref[...].astype(F32)
        dg_ref[...] = (d * o_ref[...].astype(F32) * sg * (1.0 + gate * (1.0 - sg))).astype(BF16)
        do = (d * gate * sg).astype(BF16)
        rtile = r_ref[...]
        lane, left, causal = _sb_masks(i)
        qcat = _stack_heads(qs, left)
        docat = _stack_heads(do, left)
        upper = _tri((KB, KB), lambda r, c: r > c).astype(BF16)
        lower = _tri((KB, KB), lambda r, c: r < c).astype(BF16)
        zero = qcat.astype(F32) * 0.0

        def step(jb, carry):
            dq, gcar = carry
            rows = pl.ds(pl.multiple_of(jb * KB, KB), KB)
            kj = k_ref[rows, :]
            vj = v_ref[rows, :]
            mask = causal(jb)
            rr = jnp.concatenate(
                [jnp.sum(jnp.where(lane == jb, rtile, 0.0), axis=1, keepdims=True),
                 jnp.sum(jnp.where(lane == 8 + jb, rtile, 0.0), axis=1, keepdims=True)], axis=0)
            lb, lom, later = _sb_block(qcat, kj, mask, rr, upper)
            att = jnp.where(mask, jnp.exp(lb + later), 0.0)
            de = att * _dot(docat, vj, _NT)
            gpre = _dot(de.astype(BF16), lower) + gcar
            sig = jnp.exp(lb)
            dz = jnp.where(mask, de * (1.0 - sig) - gpre * sig, 0.0).astype(BF16)
            dk_acc[rows, :] = dk_acc[rows, :] + _dot(dz, qcat, _TN)
            dv_acc[rows, :] = dv_acc[rows, :] + _dot(att.astype(BF16), docat, _TN)
            return dq + _dot(dz, kj), gcar + jnp.sum(de, axis=1, keepdims=True)

        dq, _ = lax.fori_loop(0, i + 1, step, (zero, zero[:, :1]))
        dq_ref[...] = (jnp.where(left, dq[:QB], dq[QB:]) * SB_SCALE).astype(BF16)

        @pl.when(i == nq - 1)
        def _():
            dk_ref[...] = dk_acc[...].astype(BF16)
            dv_ref[...] = dv_acc[...].astype(BF16)

    blk = lambda b, hp, i: (b * nq + i, hp)
    return pl.pallas_call(
        body, name=name, grid=(bl, SB_PAIRS, nq),
        in_specs=[pl.BlockSpec((QB, 128), blk), pl.BlockSpec((QB, 128), blk),
                  pl.BlockSpec((None, None, QB, 128), lambda b, hp, i: (b * nq + i, hp, 0, 0)),
                  pl.BlockSpec((QB, 128), lambda b, hp, i: (b * nq + i, qb0 + hp)),
                  pl.BlockSpec((s, 128), lambda b, hp, i: (b, kb0 + hp)),
                  pl.BlockSpec((s, 128), lambda b, hp, i: (b, vb0 + hp)),
                  pl.BlockSpec((QB, 128), lambda b, hp, i: (b * nq + i, gb0 + hp))],
        out_specs=[pl.BlockSpec((QB, 128), blk),
                   pl.BlockSpec((s, 128), lambda b, hp, i: (b, hp)),
                   pl.BlockSpec((s, 128), lambda b, hp, i: (b, hp)),
                   pl.BlockSpec((QB, 128), blk)],
        out_shape=[jax.ShapeDtypeStruct((t, SB_W), BF16)] * 4,
        scratch_shapes=[pltpu.VMEM((s, 128), F32), pltpu.VMEM((s, 128), F32)],
        compiler_params=_params("parallel", "parallel", "arbitrary"),
    )(dsb, o, rsave, proj, proj, proj, proj)


def merge_fwd(proj, ys, yp, yb, name):
    t = ys.shape[0]
    tr = min(512, t)

    def body(m_ref, ys_ref, yp_ref, yb_ref, o_ref):
        acc = jnp.zeros((tr, D), F32)
        for k, ref in enumerate((ys_ref, yp_ref, yb_ref)):
            acc = acc + _sigmoid(m_ref[:, k * D:(k + 1) * D].astype(F32)) * ref[...].astype(F32)
        o_ref[...] = acc.astype(BF16)

    rowblk = pl.BlockSpec((tr, D), lambda i: (i, 0))
    return pl.pallas_call(
        body, name=name, grid=(t // tr,),
        in_specs=[pl.BlockSpec((tr, 3 * D), lambda i: (i, 0)), rowblk, rowblk, rowblk],
        out_specs=rowblk,
        out_shape=jax.ShapeDtypeStruct((t, D), BF16),
        compiler_params=_params("parallel"),
    )(proj, ys, yp, yb)


def merge_bwd(dm, proj, ys, yp, yb, name):
    t = ys.shape[0]
    tr = min(512, t)

    def body(dm_ref, m_ref, ys_ref, yp_ref, yb_ref, d0_ref, d1_ref, d2_ref, dl_ref):
        dmv = dm_ref[...].astype(F32)
        for k, (ref, dref) in enumerate(((ys_ref, d0_ref), (yp_ref, d1_ref), (yb_ref, d2_ref))):
            g = _sigmoid(m_ref[:, k * D:(k + 1) * D].astype(F32))
            dref[...] = (g * dmv).astype(BF16)
            dl_ref[:, k * D:(k + 1) * D] = (dmv * ref[...].astype(F32) * g * (1.0 - g)).astype(BF16)

    rowblk = pl.BlockSpec((tr, D), lambda i: (i, 0))
    wide = pl.BlockSpec((tr, 3 * D), lambda i: (i, 0))
    return pl.pallas_call(
        body, name=name, grid=(t // tr,),
        in_specs=[rowblk, wide, rowblk, rowblk, rowblk],
        out_specs=[rowblk, rowblk, rowblk, wide],
        out_shape=[jax.ShapeDtypeStruct((t, D), BF16)] * 3 + [jax.ShapeDtypeStruct((t, 3 * D), BF16)],
        compiler_params=_params("parallel"),
    )(dm, proj, ys, yp, yb)


def layer_fwd(x, lw, bl, s, tag):
    t = bl * s
    h = rmsnorm_fwd(x, lw["norm_w"], f"norm_fwd{tag}")
    proj = matmul(h, lw["w_in"], "nt", BF16, f"in_proj{tag}", tn=2048)
    xa = conv_fwd(proj, lw["conv_w"], lw["conv_b"], bl, s, f"conv_fwd{tag}")
    dtT, acT = dt_fwd(proj, lw["dt_bias"], lw["a_log"], t, f"dt_fwd{tag}")
    dskip_l = jnp.repeat(lw["d_skip"], 64).reshape(1, SSM_W)
    y, prev = ssd_fwd(xa, dtT, acT, dskip_l, bl, s, f"ssd_fwd{tag}")
    s_out = gnorm_fwd(y, proj, lw["ssm_norm_w"], f"gnorm_fwd{tag}")
    p_out = pool_fwd(proj, lw["pool_w"], lw["pool_scale"], bl, s, f"pool_fwd{tag}")
    sb_out, sb_o, sb_r = sb_fwd(proj, bl, s, f"sb_fwd{tag}")
    ys = matmul(s_out, lw["w_proj_ssm"], "nn", BF16, f"proj_ssm{tag}")
    yp = matmul(p_out, lw["w_proj_pool"], "nn", BF16, f"proj_pool{tag}")
    yb = matmul(sb_out, lw["w_proj_sb"], "nn", BF16, f"proj_sb{tag}")
    merged = merge_fwd(proj, ys, yp, yb, f"merge_fwd{tag}")
    x_next = matmul(merged, lw["w_out"], "nn", F32, f"out_proj{tag}", residual=x)
    saved = dict(x=x, h=h, proj=proj, xa=xa, dtT=dtT, acT=acT, y=y, prev=prev, s_out=s_out, p_out=p_out,
                 sb_out=sb_out, sb_o=sb_o, sb_r=sb_r, ys=ys, yp=yp, yb=yb, merged=merged)
    return x_next, saved


def layer_bwd(dx, lw, sv, bl, s, tag):
    t = bl * s
    g = {}
    dx_b = dx.astype(BF16)
    dmerged = matmul(dx_b, lw["w_out"], "nt", BF16, f"d_merged{tag}")
    g["w_out"] = matmul(sv["merged"], dx_b, "tn", F32, f"dw_out{tag}")
    dys, dyp, dyb, dlogit = merge_bwd(dmerged, sv["proj"], sv["ys"], sv["yp"], sv["yb"], f"merge_bwd{tag}")
    ds_out = matmul(dys, lw["w_proj_ssm"], "nt", BF16, f"d_sout{tag}")
    g["w_proj_ssm"] = matmul(sv["s_out"], dys, "tn", F32, f"dw_proj_ssm{tag}")
    dp_out = matmul(dyp, lw["w_proj_pool"], "nt", BF16, f"d_pout{tag}")
    g["w_proj_pool"] = matmul(sv["p_out"], dyp, "tn", F32, f"dw_proj_pool{tag}")
    dsb_out = matmul(dyb, lw["w_proj_sb"], "nt", BF16, f"d_sbout{tag}")
    g["w_proj_sb"] = matmul(sv["sb_out"], dyb, "tn", F32, f"dw_proj_sb{tag}")
    dy, dz, dnw = gnorm_bwd(ds_out, sv["y"], sv["proj"], lw["ssm_norm_w"], f"gnorm_bwd{tag}")
    g["ssm_norm_w"] = dnw[0]
    dskip_l = jnp.repeat(lw["d_skip"], 64).reshape(1, SSM_W)
    dxs, db, dc, ddtT, dacT, dsk = ssd_bwd(dy, sv["xa"], sv["dtT"], sv["acT"], dskip_l, sv["prev"], bl, s, f"ssd_bwd{tag}")
    g["d_skip"] = jnp.sum(dsk.reshape(N_HEADS, 64), axis=1)
    ddt_raw, da, dbias = dt_bwd(ddtT, dacT, sv["dtT"], sv["proj"], lw["dt_bias"], lw["a_log"], t, f"dt_bwd{tag}")
    g["a_log"] = da * (-jnp.exp(lw["a_log"]))
    g["dt_bias"] = dbias
    dxa = jnp.concatenate([dxs, db.astype(BF16), dc.astype(BF16)], axis=1)
    dxbc, dcw, dcb = conv_bwd(dxa, sv["proj"], lw["conv_w"], lw["conv_b"], bl, s, f"conv_bwd{tag}")
    g["conv_w"] = dcw
    g["conv_b"] = dcb
    dpu, dpg, dpw, dpsc = pool_bwd(dp_out, sv["proj"], lw["pool_w"], lw["pool_scale"], bl, s, f"pool_bwd{tag}")
    g["pool_w"] = dpw
    g["pool_scale"] = dpsc[0]
    dq, dk, dv, dsbg = sb_bwd(dsb_out, sv["sb_o"], sv["sb_r"], sv["proj"], bl, s, f"sb_bwd{tag}")
    dproj = jnp.concatenate(
        [dlogit, dsbg, dpu, dpg, dz, dq, dk, dv, dxbc, ddt_raw, jnp.zeros((t, PC - OFF_DT - 128), BF16)], axis=1)
    g["w_in"] = matmul(dproj, sv["h"], "tn", F32, f"dw_in{tag}")
    dh = matmul(dproj, lw["w_in"], "nn", F32, f"d_h{tag}")
    dx_in, dnorm = rmsnorm_bwd(dh, sv["x"], lw["norm_w"], dx, f"norm_bwd{tag}")
    g["norm_w"] = dnorm[0]
    return dx_in, g


_PAD_PIECES = ((10784, 3072), (9760, 1024), (4640, 1024), (5664, 1024), (0, 2048), (6688, 3072), (2048, 2560), (4608, 32))
_UNPAD_PIECES = ((OFF_Z, 2048), (OFF_XBC, 2560), (OFF_DT, 32), (OFF_PU, 1024), (OFF_PG, 1024), (OFF_QKV, 3072),
                 (OFF_SBG, 1024), (OFF_MERGE, 3072))


def pad_rows(wt):
    pieces = [wt[o:o + n] for o, n in _PAD_PIECES]
    return jnp.concatenate(pieces + [jnp.zeros((PC - IN_COLS, wt.shape[1]), wt.dtype)], axis=0)


def unpad_rows(wp):
    return jnp.concatenate([wp[o:o + n] for o, n in _UNPAD_PIECES], axis=0)


MESH = pl.DeviceIdType.MESH
ANY = pl.BlockSpec(memory_space=pl.ANY)


def _coords():
    return lax.axis_index("x"), lax.axis_index("y"), lax.axis_index("c")


def _peer(p):
    x, y, c = _coords()
    return (1 - x if p & 4 else x, 1 - y if p & 2 else y, 1 - c if p & 1 else c)


def _flat(pos):
    return 4 * pos[0] + 2 * pos[1] + pos[2]


def exchange(v, gather, name):
    out_shape = (N_DEV,) + tuple(v.shape) if gather else tuple(v.shape)

    def body(v_ref, out_ref, send_sems, recv_sems, local_sem):
        me = _flat(_coords())
        src_of = (lambda k: v_ref) if gather else (lambda k: v_ref.at[k])

        def copy(p, landing):
            peer = _peer(p)
            return pltpu.make_async_remote_copy(
                src_ref=src_of(_flat(peer)), dst_ref=out_ref.at[landing], send_sem=send_sems.at[p - 1],
                recv_sem=recv_sems.at[p - 1], device_id=peer, device_id_type=MESH)

        local = pltpu.make_async_copy(src_of(me), out_ref.at[me], local_sem)
        local.start()
        sends = [copy(p, me) for p in range(1, N_DEV)]
        for cp in sends:
            cp.start()
        for p in range(1, N_DEV):
            copy(p, _flat(_peer(p))).wait_recv()
        for cp in sends:
            cp.wait_send()
        local.wait()

    return pl.pallas_call(
        body, name=name,
        in_specs=[ANY], out_specs=ANY,
        out_shape=jax.ShapeDtypeStruct(out_shape, v.dtype),
        scratch_shapes=[pltpu.SemaphoreType.DMA((N_DEV - 1,)), pltpu.SemaphoreType.DMA((N_DEV - 1,)),
                        pltpu.SemaphoreType.DMA],
    )(v)


def sum_slabs(v, name):
    _, r, c = v.shape
    tr = 128 if r % 128 == 0 else r

    def body(v_ref, o_ref):
        acc = v_ref[0].astype(F32)
        for k in range(1, N_DEV):
            acc = acc + v_ref[k].astype(F32)
        o_ref[...] = acc

    return pl.pallas_call(
        body, name=name, grid=(r // tr,),
        in_specs=[pl.BlockSpec((N_DEV, tr, c), lambda i: (0, i, 0))],
        out_specs=pl.BlockSpec((tr, c), lambda i: (i, 0)),
        out_shape=jax.ShapeDtypeStruct((r, c), F32),
        compiler_params=_params("parallel"),
    )(v)


def adamw(w, g, m, v, name):
    r, c = w.shape
    tr = next((cand for cand in (256, 128, 64, 32, 16, 8) if r % cand == 0), r)

    def body(w_ref, g_ref, m_ref, v_ref, d_ref, mo_ref, vo_ref):
        gv = g_ref[...]
        mn = ADAM_B1 * m_ref[...] + (1.0 - ADAM_B1) * gv
        vn = ADAM_B2 * v_ref[...] + (1.0 - ADAM_B2) * (gv * gv)
        m_hat = mn / (1.0 - ADAM_B1 ** ADAM_STEP)
        v_hat = vn / (1.0 - ADAM_B2 ** ADAM_STEP)
        d_ref[...] = -ADAM_LR * (m_hat / (jnp.sqrt(v_hat) + ADAM_EPS) + ADAM_WD * w_ref[...])
        mo_ref[...] = mn
        vo_ref[...] = vn

    blk = pl.BlockSpec((tr, c), lambda i: (i, 0))
    return pl.pallas_call(
        body, name=name, grid=(r // tr,),
        in_specs=[blk] * 4, out_specs=[blk] * 3,
        out_shape=[jax.ShapeDtypeStruct((r, c), F32)] * 3,
        compiler_params=_params("parallel"),
    )(w, g, m, v)


BIG = ("w_proj_ssm", "w_proj_pool", "w_proj_sb", "w_out", "pool_w", "w_in")
SHARD_IN = IN_COLS // N_DEV
BIG_ROWS = {"w_proj_ssm": SSM_W // N_DEV, "w_proj_pool": POOL_W // N_DEV, "w_proj_sb": SB_W // N_DEV,
            "w_out": D // N_DEV, "pool_w": POOL_G * (POOL_GD // N_DEV) * POOL_GD // D, "w_in": SHARD_IN}
PACK_C = D
PACK_R = 2432

REPLICATED = ("norm_w", "conv_b", "dt_bias", "a_log", "d_skip", "ssm_norm_w", "pool_scale")
WEIGHTS = ("norm_w", "w_in", "conv_w", "conv_b", "dt_bias", "a_log", "d_skip", "ssm_norm_w", "pool_w",
           "pool_scale", "w_proj_ssm", "w_proj_pool", "w_proj_sb", "w_out", "final_norm_w")


def _size(shape):
    n = 1
    for d in shape:
        n *= d
    return n


def _pad_flat(flat, n):
    return jnp.concatenate([flat, jnp.zeros((n - flat.shape[0],), flat.dtype)])


def _row_offsets():
    offs, off = {}, 0
    for n in BIG:
        offs[n] = off
        off += BIG_ROWS[n]
    return offs, off


def pack_shards(parts):
    rows = [parts[n].reshape(BIG_ROWS[n], PACK_C) for n in BIG]
    rows[-1] = jnp.pad(rows[-1], ((0, PACK_R - _row_offsets()[1]), (0, 0)))
    return jnp.concatenate(rows, axis=0)


def unpack_shards(packed):
    offs, _ = _row_offsets()
    out = {}
    for n in BIG:
        seg = packed[offs[n]:offs[n] + BIG_ROWS[n]]
        if n == "w_in":
            out[n] = seg.T
        elif n == "pool_w":
            out[n] = seg.reshape(POOL_G, POOL_GD // N_DEV, POOL_GD)
        else:
            out[n] = seg
    return out


def unpack_gathered(g):
    offs, _ = _row_offsets()
    out = {}
    for n in BIG:
        seg = g[:, offs[n]:offs[n] + BIG_ROWS[n], :]
        if n == "w_in":
            out[n] = pad_rows(seg.reshape(IN_COLS, D))
        elif n == "pool_w":
            out[n] = seg.reshape(N_DEV, POOL_G, POOL_GD // N_DEV, POOL_GD).transpose(1, 0, 2, 3).reshape(
                POOL_G, POOL_GD, POOL_GD)
        else:
            out[n] = seg.reshape(N_DEV * BIG_ROWS[n], D)
    return out


def pack_slabs(g):
    segs = []
    for n in BIG:
        if n == "w_in":
            w = unpad_rows(g[n])
        elif n == "pool_w":
            w = g[n].reshape(POOL_G, N_DEV, POOL_GD // N_DEV, POOL_GD).transpose(1, 0, 2, 3)
        else:
            w = g[n]
        segs.append(w.reshape(N_DEV, BIG_ROWS[n], PACK_C).astype(BF16))
    segs[-1] = jnp.pad(segs[-1], ((0, 0), (0, PACK_R - _row_offsets()[1]), (0, 0)))
    return jnp.concatenate(segs, axis=1)


SMALL_ROWS = 544


def pack_small(vals):
    flat = jnp.concatenate([v.reshape(-1) for v in vals])
    return _pad_flat(flat, SMALL_ROWS * 128).reshape(SMALL_ROWS, 128)


def unpack_small(packed, shapes):
    flat = packed.reshape(-1)
    out, off = [], 0
    for shp in shapes:
        out.append(flat[off:off + _size(shp)].reshape(shp))
        off += _size(shp)
    return out


def kernel(x, norm_w, w_in, conv_w, conv_b, dt_bias, a_log, d_skip, ssm_norm_w, pool_w, pool_scale, w_proj_ssm, w_proj_pool, w_proj_sb, w_out, final_norm_w, loss_target, m_norm_w, m_w_in, m_conv_w, m_conv_b, m_dt_bias, m_a_log, m_d_skip, m_ssm_norm_w, m_pool_w, m_pool_scale, m_w_proj_ssm, m_w_proj_pool, m_w_proj_sb, m_w_out, m_final_norm_w, v_norm_w, v_w_in, v_conv_w, v_conv_b, v_dt_bias, v_a_log, v_d_skip, v_ssm_norm_w, v_pool_w, v_pool_scale, v_w_proj_ssm, v_w_proj_pool, v_w_proj_sb, v_w_out, v_final_norm_w):
    wts = dict(norm_w=norm_w, w_in=w_in, conv_w=conv_w, conv_b=conv_b, dt_bias=dt_bias, a_log=a_log, d_skip=d_skip,
               ssm_norm_w=ssm_norm_w, pool_w=pool_w, pool_scale=pool_scale, w_proj_ssm=w_proj_ssm,
               w_proj_pool=w_proj_pool, w_proj_sb=w_proj_sb, w_out=w_out, final_norm_w=final_norm_w)
    mom = dict(norm_w=m_norm_w, w_in=m_w_in, conv_w=m_conv_w, conv_b=m_conv_b, dt_bias=m_dt_bias, a_log=m_a_log,
               d_skip=m_d_skip, ssm_norm_w=m_ssm_norm_w, pool_w=m_pool_w, pool_scale=m_pool_scale,
               w_proj_ssm=m_w_proj_ssm, w_proj_pool=m_w_proj_pool, w_proj_sb=m_w_proj_sb, w_out=m_w_out,
               final_norm_w=m_final_norm_w)
    var = dict(norm_w=v_norm_w, w_in=v_w_in, conv_w=v_conv_w, conv_b=v_conv_b, dt_bias=v_dt_bias, a_log=v_a_log,
               d_skip=v_d_skip, ssm_norm_w=v_ssm_norm_w, pool_w=v_pool_w, pool_scale=v_pool_scale,
               w_proj_ssm=v_w_proj_ssm, w_proj_pool=v_w_proj_pool, w_proj_sb=v_w_proj_sb, w_out=v_w_out,
               final_norm_w=v_final_norm_w)
    bl, s, _ = x.shape
    t = bl * s
    me = _flat(_coords())

    cw = exchange(conv_w.reshape(40, 128), True, "gather_conv_w")
    conv_w_full = cw.reshape(N_DEV, DEPTH, CONV_K, CONV_CH // N_DEV).transpose(1, 2, 0, 3).reshape(
        DEPTH, CONV_K, CONV_CH)

    xc = x.reshape(t, D)
    layer_w, saved = [], []
    for l in range(DEPTH):
        packed = pack_shards({n: (wts[n][l].T if n == "w_in" else wts[n][l]).astype(BF16) for n in BIG})
        lw = unpack_gathered(exchange(packed, True, f"gather_w{l}"))
        for n in REPLICATED:
            lw[n] = wts[n][l]
        lw["conv_w"] = conv_w_full[l]
        xc, sv = layer_fwd(xc, lw, bl, s, f"_l{l}")
        layer_w.append(lw)
        saved.append(sv)

    loss_part, dx, dfinal = final_loss(xc, final_norm_w, loss_target.reshape(t, D), "final_loss")
    loss = lax.psum(loss_part[0, 0], ("x", "y", "c"))

    grads = [None] * DEPTH
    big_sum = [None] * DEPTH
    for l in reversed(range(DEPTH)):
        dx, g = layer_bwd(dx, layer_w[l], saved[l], bl, s, f"_l{l}")
        grads[l] = g
        got = exchange(pack_slabs(g), False, f"scatter_g{l}")
        big_sum[l] = unpack_shards(sum_slabs(got, f"sum_g{l}"))
    grad_x = dx.reshape(bl, s, D)

    small_names = REPLICATED + ("conv_w",)
    small_vals = [jnp.stack([grads[l][n] for l in range(DEPTH)]) for n in small_names] + [dfinal[0]]
    small_shapes = [v.shape for v in small_vals]
    small_all = exchange(pack_small(small_vals), True, "gather_small")
    small_sum = unpack_small(sum_slabs(small_all, "sum_small"), small_shapes)
    gsum = dict(zip(small_names + ("final_norm_w",), small_sum))
    conv_g_full = gsum["conv_w"]
    gsum["conv_w"] = lax.dynamic_slice_in_dim(conv_g_full, me * (CONV_CH // N_DEV), CONV_CH // N_DEV, axis=2)
    for n in BIG:
        gsum[n] = jnp.stack([big_sum[l][n] for l in range(DEPTH)])

    delta, new_m, new_v = {}, {}, {}
    for n in BIG + ("conv_w",):
        shp = wts[n].shape
        two_d = (-1, shp[-1])
        d2, m2, v2 = adamw(wts[n].reshape(two_d), gsum[n].reshape(two_d), mom[n].reshape(two_d),
                           var[n].reshape(two_d), f"adamw_{n}")
        delta[n], new_m[n], new_v[n] = d2.reshape(shp), m2.reshape(shp), v2.reshape(shp)
    rep = REPLICATED + ("final_norm_w",)
    rep_shapes = [wts[n].shape for n in rep]
    d2, m2, v2 = adamw(pack_small([wts[n] for n in rep]), pack_small([gsum[n] for n in rep]),
                       pack_small([mom[n] for n in rep]), pack_small([var[n] for n in rep]), "adamw_small")
    for n, dv, mv, vv in zip(rep, unpack_small(d2, rep_shapes), unpack_small(m2, rep_shapes),
                             unpack_small(v2, rep_shapes)):
        delta[n], new_m[n], new_v[n] = dv, mv, vv

    return (loss, grad_x, *[gsum[n] for n in WEIGHTS], *[delta[n] for n in WEIGHTS],
            *[new_m[n] for n in WEIGHTS], *[new_v[n] for n in WEIGHTS])
```

```python
import functools

import jax
import jax.numpy as jnp
from jax import lax
from jax.experimental import pallas as pl
from jax.experimental.pallas import tpu as pltpu

F32 = jnp.float32
BF16 = jnp.bfloat16

N_DEV = 8
DEPTH = 4
D = 1024
SSM_W = 2048
N_HEADS = 32
N_PAIRS = 16
N_GROUPS = 2
N_STATE = 128
CHUNK = 128
CONV_CH = 2560
CONV_K = 4
POOL_W = 1024
POOL_G = 4
POOL_GD = 256
SB_W = 1024
SB_PAIRS = 8
QB = 256
EPS = 1e-6
IN_COLS = 13856

PC = 14336
OFF_MERGE = 0
OFF_SBG = 3072
OFF_PU = 4096
OFF_PG = 5120
OFF_Z = 6144
OFF_QKV = 8192
OFF_XBC = 11264
OFF_DT = 13824

ADAM_LR = 0.001
ADAM_B1 = 0.9
ADAM_B2 = 0.999
ADAM_EPS = 1e-08
ADAM_WD = 0.01
ADAM_STEP = 10

VMEM_LIMIT = 56 * 1024 * 1024

_NN = (((1,), (0,)), ((), ()))
_NT = (((1,), (1,)), ((), ()))
_TN = (((0,), (0,)), ((), ()))


def _dot(a, b, dn=_NN):
    return lax.dot_general(a, b, dn, preferred_element_type=F32)


def _sigmoid(x):
    return 1.0 / (1.0 + jnp.exp(-x))


def _softplus(x):
    return jnp.maximum(x, 0.0) + jnp.log(1.0 + jnp.exp(-jnp.abs(x)))


def _split2(x):
    hi = x.astype(BF16)
    lo = (x - hi.astype(F32)).astype(BF16)
    return hi, lo


def _split3(x):
    hi = x.astype(BF16)
    r = x - hi.astype(F32)
    mid = r.astype(BF16)
    lo = (r - mid.astype(F32)).astype(BF16)
    return hi, mid, lo


def _params(*sem):
    return pltpu.CompilerParams(dimension_semantics=sem, vmem_limit_bytes=VMEM_LIMIT)


def matmul(a, b, mode, out_dtype, name, residual=None, tm=1024, tn=1024, tk=1024):
    if mode == "nn":
        (m, k), (k2, n) = a.shape, b.shape
    elif mode == "nt":
        (m, k), (n, k2) = a.shape, b.shape
    else:
        (k, m), (k2, n) = a.shape, b.shape
    assert k == k2
    tm, tn, tk = min(tm, m), min(tn, n), min(tk, k)
    assert m % tm == 0 and n % tn == 0 and k % tk == 0
    nk = k // tk
    dn = {"nn": _NN, "nt": _NT, "tn": _TN}[mode]
    a_spec = pl.BlockSpec((tk, tm), lambda i, j, kk: (kk, i)) if mode == "tn" else pl.BlockSpec((tm, tk), lambda i, j, kk: (i, kk))
    b_spec = pl.BlockSpec((tn, tk), lambda i, j, kk: (j, kk)) if mode == "nt" else pl.BlockSpec((tk, tn), lambda i, j, kk: (kk, j))
    in_specs = [a_spec, b_spec]
    args = [a, b]
    if residual is not None:
        in_specs.append(pl.BlockSpec((tm, tn), lambda i, j, kk: (i, j)))
        args.append(residual)

    def body(*refs):
        if residual is not None:
            a_ref, b_ref, r_ref, o_ref, acc_ref = refs
        else:
            a_ref, b_ref, o_ref, acc_ref = refs
            r_ref = None
        kk = pl.program_id(2)
        p = _dot(a_ref[...], b_ref[...], dn)

        def finish(val):
            if r_ref is not None:
                val = val + r_ref[...]
            o_ref[...] = val.astype(out_dtype)

        if nk == 1:
            finish(p)
        else:
            @pl.when(kk == 0)
            def _():
                acc_ref[...] = p

            @pl.when(kk > 0)
            def _():
                acc_ref[...] += p

            @pl.when(kk == nk - 1)
            def _():
                finish(acc_ref[...])

    return pl.pallas_call(
        body, name=name,
        grid=(m // tm, n // tn, nk),
        in_specs=in_specs,
        out_specs=pl.BlockSpec((tm, tn), lambda i, j, kk: (i, j)),
        out_shape=jax.ShapeDtypeStruct((m, n), out_dtype),
        scratch_shapes=[pltpu.VMEM((tm, tn) if nk > 1 else (8, 128), F32)],
        compiler_params=_params("parallel", "parallel", "arbitrary"),
    )(*args)


def rmsnorm_fwd(x, w, name):
    t, d = x.shape
    tr = min(512, t)

    def body(x_ref, w_ref, h_ref):
        xv = x_ref[...]
        r = lax.rsqrt(jnp.mean(xv * xv, axis=-1, keepdims=True) + EPS)
        h_ref[...] = (xv * r * w_ref[...]).astype(BF16)

    return pl.pallas_call(
        body, name=name, grid=(t // tr,),
        in_specs=[pl.BlockSpec((tr, d), lambda i: (i, 0)), pl.BlockSpec((1, d), lambda i: (0, 0))],
        out_specs=pl.BlockSpec((tr, d), lambda i: (i, 0)),
        out_shape=jax.ShapeDtypeStruct((t, d), BF16),
        compiler_params=_params("parallel"),
    )(x, w.reshape(1, d))


def rmsnorm_bwd(dh, x, w, dres, name):
    t, d = x.shape
    tr = min(512, t)

    def body(dh_ref, x_ref, w_ref, dres_ref, dx_ref, dw_ref):
        xv = x_ref[...]
        r = lax.rsqrt(jnp.mean(xv * xv, axis=-1, keepdims=True) + EPS)
        xh = xv * r
        g = dh_ref[...].astype(F32)
        dxh = g * w_ref[...]
        dx_ref[...] = dres_ref[...] + r * (dxh - xh * jnp.mean(dxh * xh, axis=-1, keepdims=True))
        part = jnp.sum(g * xh, axis=0, keepdims=True)

        @pl.when(pl.program_id(0) == 0)
        def _():
            dw_ref[...] = part

        @pl.when(pl.program_id(0) > 0)
        def _():
            dw_ref[...] += part

    return pl.pallas_call(
        body, name=name, grid=(t // tr,),
        in_specs=[pl.BlockSpec((tr, d), lambda i: (i, 0)), pl.BlockSpec((tr, d), lambda i: (i, 0)),
                  pl.BlockSpec((1, d), lambda i: (0, 0)), pl.BlockSpec((tr, d), lambda i: (i, 0))],
        out_specs=[pl.BlockSpec((tr, d), lambda i: (i, 0)), pl.BlockSpec((1, d), lambda i: (0, 0))],
        out_shape=[jax.ShapeDtypeStruct((t, d), F32), jax.ShapeDtypeStruct((1, d), F32)],
        compiler_params=_params("arbitrary"),
    )(dh, x, w.reshape(1, d), dres)


def final_loss(x, w, target, name):
    t, d = x.shape
    tr = min(512, t)

    def body(x_ref, w_ref, tg_ref, loss_ref, dx_ref, dw_ref):
        xv = x_ref[...]
        r = lax.rsqrt(jnp.mean(xv * xv, axis=-1, keepdims=True) + EPS)
        xh = xv * r
        err = xh * w_ref[...] - tg_ref[...]
        lpart = 0.5 * jnp.sum(jnp.mean(err * err, axis=-1, keepdims=True), axis=0, keepdims=True)
        dy = err * (1.0 / d)
        dxh = dy * w_ref[...]
        dx_ref[...] = r * (dxh - xh * jnp.mean(dxh * xh, axis=-1, keepdims=True))
        part = jnp.sum(dy * xh, axis=0, keepdims=True)

        @pl.when(pl.program_id(0) == 0)
        def _():
            dw_ref[...] = part
            loss_ref[...] = jnp.broadcast_to(lpart, (1, 128))

        @pl.when(pl.program_id(0) > 0)
        def _():
            dw_ref[...] += part
            loss_ref[...] += jnp.broadcast_to(lpart, (1, 128))

    return pl.pallas_call(
        body, name=name, grid=(t // tr,),
        in_specs=[pl.BlockSpec((tr, d), lambda i: (i, 0)), pl.BlockSpec((1, d), lambda i: (0, 0)),
                  pl.BlockSpec((tr, d), lambda i: (i, 0))],
        out_specs=[pl.BlockSpec((1, 128), lambda i: (0, 0)), pl.BlockSpec((tr, d), lambda i: (i, 0)),
                   pl.BlockSpec((1, d), lambda i: (0, 0))],
        out_shape=[jax.ShapeDtypeStruct((1, 128), F32), jax.ShapeDtypeStruct((t, d), F32),
                   jax.ShapeDtypeStruct((1, d), F32)],
        compiler_params=_params("arbitrary"),
    )(x, w.reshape(1, d), target)


CONV_BW = 256


def _shift_down(u, s, row):
    return jnp.where(row >= s, pltpu.roll(u, s, axis=0), 0.0)


def _shift_up(u, s, row, n):
    return jnp.where(row < n - s, pltpu.roll(u, n - s, axis=0), 0.0)


def _conv_pre(u, w, b, row):
    acc = b + w[CONV_K - 1:CONV_K, :] * u
    for k in range(CONV_K - 1):
        acc = acc + w[k:k + 1, :] * _shift_down(u, CONV_K - 1 - k, row)
    return acc


def conv_fwd(proj, conv_w, conv_b, bl, s, name):
    t = bl * s
    nb = CONV_CH // CONV_BW
    off = OFF_XBC // CONV_BW

    def body(u_ref, w_ref, b_ref, o_ref):
        u = u_ref[...].astype(F32)
        row = lax.broadcasted_iota(jnp.int32, u.shape, 0)
        xc = _conv_pre(u, w_ref[...], b_ref[...], row)
        o_ref[...] = (xc * _sigmoid(xc)).astype(BF16)

    return pl.pallas_call(
        body, name=name, grid=(bl, nb),
        in_specs=[pl.BlockSpec((s, CONV_BW), lambda b, j: (b, off + j)),
                  pl.BlockSpec((CONV_K, CONV_BW), lambda b, j: (0, j)),
                  pl.BlockSpec((1, CONV_BW), lambda b, j: (0, j))],
        out_specs=pl.BlockSpec((s, CONV_BW), lambda b, j: (b, j)),
        out_shape=jax.ShapeDtypeStruct((t, CONV_CH), BF16),
        compiler_params=_params("parallel", "parallel"),
    )(proj, conv_w, conv_b.reshape(1, CONV_CH))


def conv_bwd(dxa, proj, conv_w, conv_b, bl, s, name):
    t = bl * s
    nb = CONV_CH // CONV_BW
    off = OFF_XBC // CONV_BW

    def body(d_ref, u_ref, w_ref, b_ref, du_ref, dw_ref, db_ref):
        u = u_ref[...].astype(F32)
        w = w_ref[...]
        row = lax.broadcasted_iota(jnp.int32, u.shape, 0)
        xc = _conv_pre(u, w, b_ref[...], row)
        sg = _sigmoid(xc)
        dxc = d_ref[...].astype(F32) * sg * (1.0 + xc * (1.0 - sg))
        du = w[CONV_K - 1:CONV_K, :] * dxc
        dws = [None] * CONV_K
        dws[CONV_K - 1] = jnp.sum(dxc * u, axis=0, keepdims=True)
        for k in range(CONV_K - 1):
            sh = CONV_K - 1 - k
            du = du + w[k:k + 1, :] * _shift_up(dxc, sh, row, s)
            dws[k] = jnp.sum(dxc * _shift_down(u, sh, row), axis=0, keepdims=True)
        du_ref[...] = du.astype(BF16)
        krow = lax.broadcasted_iota(jnp.int32, (8, CONV_BW), 0)
        dwv = sum(jnp.where(krow == k, dws[k], 0.0) for k in range(CONV_K))
        dbv = jnp.sum(dxc, axis=0, keepdims=True)

        @pl.when(pl.program_id(1) == 0)
        def _():
            dw_ref[...] = dwv
            db_ref[...] = dbv

        @pl.when(pl.program_id(1) > 0)
        def _():
            dw_ref[...] += dwv
            db_ref[...] += dbv

    du, dw, db = pl.pallas_call(
        body, name=name, grid=(nb, bl),
        in_specs=[pl.BlockSpec((s, CONV_BW), lambda j, b: (b, j)),
                  pl.BlockSpec((s, CONV_BW), lambda j, b: (b, off + j)),
                  pl.BlockSpec((CONV_K, CONV_BW), lambda j, b: (0, j)),
                  pl.BlockSpec((1, CONV_BW), lambda j, b: (0, j))],
        out_specs=[pl.BlockSpec((s, CONV_BW), lambda j, b: (b, j)),
                   pl.BlockSpec((8, CONV_BW), lambda j, b: (0, j)),
                   pl.BlockSpec((1, CONV_BW), lambda j, b: (0, j))],
        out_shape=[jax.ShapeDtypeStruct((t, CONV_CH), BF16), jax.ShapeDtypeStruct((8, CONV_CH), F32),
                   jax.ShapeDtypeStruct((1, CONV_CH), F32)],
        compiler_params=_params("parallel", "arbitrary"),
    )(dxa, proj, conv_w, conv_b.reshape(1, CONV_CH))
    return du, dw[:CONV_K], db[0]


def _tri(shape, cmp):
    r = lax.broadcasted_iota(jnp.int32, shape, 0)
    c = lax.broadcasted_iota(jnp.int32, shape, 1)
    return cmp(r, c)


def dt_fwd(proj, dt_bias, a_log, t, name):
    nchunks = t // CHUNK
    bias = jnp.zeros((1, 128), F32).at[0, :N_HEADS].set(dt_bias)
    alog = jnp.zeros((1, 128), F32).at[0, :N_HEADS].set(a_log)

    def body(raw_ref, b_ref, al_ref, dt_ref, ac_ref):
        raw = raw_ref[...].astype(F32)
        dt = _softplus(raw + b_ref[...])
        adt = dt * (-jnp.exp(al_ref[...]))
        low = _tri((CHUNK, CHUNK), lambda r, c: r >= c).astype(BF16)
        acum = sum(_dot(low, part) for part in _split3(adt))
        dt_ref[...] = dt.T[:N_HEADS]
        ac_ref[...] = acum.T[:N_HEADS]

    return pl.pallas_call(
        body, name=name, grid=(nchunks,),
        in_specs=[pl.BlockSpec((CHUNK, 128), lambda i: (i, OFF_DT // 128)),
                  pl.BlockSpec((1, 128), lambda i: (0, 0)), pl.BlockSpec((1, 128), lambda i: (0, 0))],
        out_specs=[pl.BlockSpec((None, N_HEADS, CHUNK), lambda i: (i, 0, 0))] * 2,
        out_shape=[jax.ShapeDtypeStruct((nchunks, N_HEADS, CHUNK), F32)] * 2,
        compiler_params=_params("parallel"),
    )(proj, bias, alog)


def dt_bwd(ddtT, dacT, dtT, proj, dt_bias, a_log, t, name):
    nchunks = t // CHUNK
    bias = dt_bias.reshape(N_HEADS, 1)
    alog = a_log.reshape(N_HEADS, 1)

    def body(ddt_ref, dac_ref, dt_ref, raw_ref, b_ref, al_ref, draw_ref, da_ref, db_ref):
        a = -jnp.exp(al_ref[...])
        upp = _tri((CHUNK, CHUNK), lambda r, c: r >= c).astype(BF16)
        dadt = sum(_dot(part, upp) for part in _split3(dac_ref[...]))
        ddt = ddt_ref[...] + dadt * a
        rawT = raw_ref[...].astype(F32).T[:N_HEADS]
        draw = ddt * _sigmoid(rawT + b_ref[...])
        padded = jnp.concatenate([draw, jnp.zeros((128 - N_HEADS, CHUNK), F32)], axis=0)
        draw_ref[...] = padded.T.astype(BF16)
        dav = dadt * dt_ref[...]

        @pl.when(pl.program_id(0) == 0)
        def _():
            da_ref[...] = dav
            db_ref[...] = draw

        @pl.when(pl.program_id(0) > 0)
        def _():
            da_ref[...] += dav
            db_ref[...] += draw

    draw, da, db = pl.pallas_call(
        body, name=name, grid=(nchunks,),
        in_specs=[pl.BlockSpec((None, N_HEADS, CHUNK), lambda i: (i, 0, 0))] * 3
        + [pl.BlockSpec((CHUNK, 128), lambda i: (i, OFF_DT // 128)),
           pl.BlockSpec((N_HEADS, 1), lambda i: (0, 0)), pl.BlockSpec((N_HEADS, 1), lambda i: (0, 0))],
        out_specs=[pl.BlockSpec((CHUNK, 128), lambda i: (i, 0)),
                   pl.BlockSpec((N_HEADS, CHUNK), lambda i: (0, 0)), pl.BlockSpec((N_HEADS, CHUNK), lambda i: (0, 0))],
        out_shape=[jax.ShapeDtypeStruct((t, 128), BF16), jax.ShapeDtypeStruct((N_HEADS, CHUNK), F32),
                   jax.ShapeDtypeStruct((N_HEADS, CHUNK), F32)],
        compiler_params=_params("arbitrary"),
    )(ddtT, dacT, dtT, proj, bias, alog)
    return draw, jnp.sum(da, axis=1), jnp.sum(db, axis=1)


PAIRS_G = N_PAIRS // N_GROUPS
GROUP_W = PAIRS_G * 128


def _colb(r):
    return jnp.broadcast_to(r, (CHUNK, 128)).T


def _ssd_pair(x, dtr, acr, tri):
    left = lax.broadcasted_iota(jnp.int32, (CHUNK, 128), 1) < 64
    ac_cols = [_colb(acr[e:e + 1]) for e in range(2)]
    dtl = jnp.where(left, _colb(dtr[0:1]), _colb(dtr[1:2]))
    acl = jnp.where(left, ac_cols[0], ac_cols[1])
    dks = [jnp.exp(jnp.where(tri, ac_cols[e] - acr[e:e + 1], -1e30)) for e in range(2)]
    aclast = acl[CHUNK - 1:CHUNK, :]
    return left, dtl, acl, x * dtl, dks, aclast


def _ssd_specs(nc, rev):
    row = (lambda b, c, g: b * nc + (nc - 1 - c)) if rev else (lambda b, c, g: b * nc + c)
    return dict(
        wide=pl.BlockSpec((CHUNK, GROUP_W), lambda b, c, g: (row(b, c, g), g)),
        bmat=pl.BlockSpec((CHUNK, 128), lambda b, c, g: (row(b, c, g), SSM_W // 128 + g)),
        cmat=pl.BlockSpec((CHUNK, 128), lambda b, c, g: (row(b, c, g), SSM_W // 128 + N_GROUPS + g)),
        rows2=pl.BlockSpec((None, PAIRS_G, 2, CHUNK), lambda b, c, g: (row(b, c, g), g, 0, 0)),
        rows8=pl.BlockSpec((None, PAIRS_G, 8, CHUNK), lambda b, c, g: (row(b, c, g), g, 0, 0)),
        dskip=pl.BlockSpec((1, GROUP_W), lambda b, c, g: (0, g)),
        state=pl.BlockSpec((None, PAIRS_G, N_STATE, 128), lambda b, c, g: (row(b, c, g), g, 0, 0)),
        narrow=pl.BlockSpec((CHUNK, 128), lambda b, c, g: (row(b, c, g), g)))


def ssd_fwd(xa, dtT, acT, dskip_l, bl, s, name):
    t = bl * s
    nc = s // CHUNK
    dt4 = dtT.reshape(bl * nc, N_PAIRS, 2, CHUNK)
    ac4 = acT.reshape(bl * nc, N_PAIRS, 2, CHUNK)

    def body(x_ref, b_ref, c_ref, dt_ref, ac_ref, dsk_ref, y_ref, prev_ref, st_ref):
        c = pl.program_id(1)
        g = pl.program_id(2)
        bm = b_ref[...]
        cm = c_ref[...]
        cb = _dot(cm, bm, _NT)
        tri = _tri((CHUNK, CHUNK), lambda r, c: r >= c)

        @pl.when(c == 0)
        def _():
            for p in range(PAIRS_G):
                st_ref[g * PAIRS_G + p] = jnp.zeros((N_STATE, 128), F32)

        for p in range(PAIRS_G):
            hp = g * PAIRS_G + p
            cs = slice(p * 128, (p + 1) * 128)
            x = x_ref[:, cs].astype(F32)
            left, dtl, acl, xdt, dks, aclast = _ssd_pair(x, dt_ref[p], ac_ref[p], tri)
            xdt_b = xdt.astype(BF16)
            ys = [_dot((cb * dks[e]).astype(BF16), xdt_b) for e in range(2)]
            st = st_ref[hp]
            y_off = _dot(cm, st.astype(BF16)) * jnp.exp(acl)
            y_ref[:, cs] = (jnp.where(left, ys[0], ys[1]) + y_off + x * dsk_ref[:, cs]).astype(BF16)
            xw = (xdt * jnp.exp(aclast - acl)).astype(BF16)
            prev_ref[p] = st
            st_ref[hp] = st * jnp.exp(aclast) + _dot(bm, xw, _TN)

    sp = _ssd_specs(nc, False)
    return pl.pallas_call(
        body, name=name, grid=(bl, nc, N_GROUPS),
        in_specs=[sp["wide"], sp["bmat"], sp["cmat"], sp["rows2"], sp["rows2"], sp["dskip"]],
        out_specs=[sp["wide"], sp["state"]],
        out_shape=[jax.ShapeDtypeStruct((t, SSM_W), BF16),
                   jax.ShapeDtypeStruct((bl * nc, N_PAIRS, N_STATE, 128), F32)],
        scratch_shapes=[pltpu.VMEM((N_PAIRS, N_STATE, 128), F32)],
        compiler_params=_params("parallel", "arbitrary", "arbitrary"),
    )(xa, xa, xa, dt4, ac4, dskip_l)


def ssd_bwd(dy, xa, dtT, acT, dskip_l, prev, bl, s, name):
    t = bl * s
    nc = s // CHUNK
    dt4 = dtT.reshape(bl * nc, N_PAIRS, 2, CHUNK)
    ac4 = acT.reshape(bl * nc, N_PAIRS, 2, CHUNK)

    def body(dy_ref, x_ref, b_ref, c_ref, dt_ref, ac_ref, dsk_ref, prev_ref,
             dx_ref, db_ref, dc_ref, dd_ref, dsk_out_ref, dp_ref):
        b = pl.program_id(0)
        cr = pl.program_id(1)
        g = pl.program_id(2)

        @pl.when(cr == 0)
        def _():
            for p in range(PAIRS_G):
                dp_ref[g * PAIRS_G + p] = jnp.zeros((N_STATE, 128), F32)

        @pl.when((b == 0) & (cr == 0) & (g == 0))
        def _():
            dsk_out_ref[...] = jnp.zeros(dsk_out_ref.shape, F32)

        bm = b_ref[...]
        cm = c_ref[...]
        cb = _dot(cm, bm, _NT)
        tri = _tri((CHUNK, CHUNK), lambda r, c: r >= c)
        lane = lax.broadcasted_iota(jnp.int32, (CHUNK, 128), 1)
        lrow = lax.broadcasted_iota(jnp.int32, (1, CHUNK), 1)
        krow = lax.broadcasted_iota(jnp.int32, (8, CHUNK), 0)
        dcb = jnp.zeros((CHUNK, CHUNK), F32)
        dc_acc = jnp.zeros((CHUNK, N_STATE), F32)
        db_acc = jnp.zeros((CHUNK, N_STATE), F32)
        for p in range(PAIRS_G):
            hp = g * PAIRS_G + p
            cs = slice(p * 128, (p + 1) * 128)
            x = x_ref[:, cs].astype(F32)
            left, dtl, acl, xdt, dks, aclast = _ssd_pair(x, dt_ref[p], ac_ref[p], tri)
            dyv = dy_ref[:, cs].astype(F32)
            dy_b = dyv.astype(BF16)
            xdt_b = xdt.astype(BF16)
            st = prev_ref[p]
            st_b = st.astype(BF16)
            ea = jnp.exp(acl)
            ds = jnp.exp(aclast - acl)
            cdl = jnp.exp(aclast)
            xw = xdt * ds
            masks = [left, jnp.logical_not(left)]

            dsk_out_ref[hp] = dsk_out_ref[hp] + jnp.sum(dyv * x, axis=0, keepdims=True)

            yo = _dot(cm, st_b)
            dyo_b = (dyv * ea).astype(BF16)
            yoff_term = dyv * yo * ea
            dc_acc = dc_acc + _dot(dyo_b, st_b, _NT)
            dst = _dot(cm, dyo_b, _TN)
            dsv = dp_ref[hp]
            dsv_b = dsv.astype(BF16)
            dxw = _dot(bm, dsv_b)
            db_acc = db_acc + _dot(xw.astype(BF16), dsv_b, _NT)
            dxdt = dxw * ds
            qv = dxw * xw
            end_term = dsv * st * cdl
            dp_ref[hp] = dsv * cdl + dst

            cols = jnp.zeros((CHUNK, 128), F32)
            rows = []
            for e in range(2):
                m = cb * dks[e]
                dy_e = jnp.where(masks[e], dyv, 0.0).astype(BF16)
                dm = _dot(dy_e, xdt_b, _NT)
                w = dm * m
                dcb = dcb + dm * dks[e]
                dxdt = dxdt + jnp.where(masks[e], _dot(m.astype(BF16), dy_b, _TN), 0.0)
                dac_col = (jnp.sum(w, axis=1, keepdims=True)
                           + jnp.sum(jnp.where(masks[e], yoff_term - qv, 0.0), axis=1, keepdims=True))
                cols = jnp.where(lane == 2 + e, dac_col, cols)
                tail = (jnp.sum(jnp.where(masks[e], qv, 0.0)) + jnp.sum(jnp.where(masks[e], end_term, 0.0)))
                rows.append(jnp.where(lrow == CHUNK - 1, tail, 0.0) - jnp.sum(w, axis=0, keepdims=True))
            dx_ref[:, cs] = (dxdt * dtl + dyv * dsk_ref[:, cs]).astype(BF16)
            ddt_l = dxdt * x
            for e in range(2):
                cols = jnp.where(lane == e, jnp.sum(jnp.where(masks[e], ddt_l, 0.0), axis=1, keepdims=True), cols)
            dd_ref[p] = cols.T[0:8] + jnp.where(krow == 2, rows[0], 0.0) + jnp.where(krow == 3, rows[1], 0.0)
        dcb_b = dcb.astype(BF16)
        dc_ref[...] = dc_acc + _dot(dcb_b, bm)
        db_ref[...] = db_acc + _dot(dcb_b, cm, _TN)

    sp = _ssd_specs(nc, True)
    dx, db, dc, dd, dsk = pl.pallas_call(
        body, name=name, grid=(bl, nc, N_GROUPS),
        in_specs=[sp["wide"], sp["wide"], sp["bmat"], sp["cmat"], sp["rows2"], sp["rows2"], sp["dskip"], sp["state"]],
        out_specs=[sp["wide"], sp["narrow"], sp["narrow"], sp["rows8"],
                   pl.BlockSpec((N_PAIRS, 1, 128), lambda b, c, g: (0, 0, 0))],
        out_shape=[jax.ShapeDtypeStruct((t, SSM_W), BF16),
                   jax.ShapeDtypeStruct((t, N_GROUPS * N_STATE), F32),
                   jax.ShapeDtypeStruct((t, N_GROUPS * N_STATE), F32),
                   jax.ShapeDtypeStruct((bl * nc, N_PAIRS, 8, CHUNK), F32),
                   jax.ShapeDtypeStruct((N_PAIRS, 1, 128), F32)],
        scratch_shapes=[pltpu.VMEM((N_PAIRS, N_STATE, 128), F32)],
        compiler_params=_params("arbitrary", "arbitrary", "arbitrary"),
    )(dy, xa, xa, xa, dt4, ac4, dskip_l, prev)
    ddtT = dd[:, :, 0:2, :].reshape(bl * nc, N_HEADS, CHUNK)
    dacT = dd[:, :, 2:4, :].reshape(bl * nc, N_HEADS, CHUNK)
    return dx, db, dc, ddtT, dacT, dsk.reshape(N_PAIRS, 128)


def gnorm_fwd(y, proj, w, name):
    t = y.shape[0]
    tr = min(256, t)
    zb = OFF_Z // SSM_W

    def body(y_ref, z_ref, w_ref, o_ref):
        z = z_ref[...].astype(F32)
        yg = y_ref[...].astype(F32) * z * _sigmoid(z)
        r = lax.rsqrt(jnp.mean(yg * yg, axis=-1, keepdims=True) + EPS)
        o_ref[...] = (yg * r * w_ref[...]).astype(BF16)

    return pl.pallas_call(
        body, name=name, grid=(t // tr,),
        in_specs=[pl.BlockSpec((tr, SSM_W), lambda i: (i, 0)), pl.BlockSpec((tr, SSM_W), lambda i: (i, zb)),
                  pl.BlockSpec((1, SSM_W), lambda i: (0, 0))],
        out_specs=pl.BlockSpec((tr, SSM_W), lambda i: (i, 0)),
        out_shape=jax.ShapeDtypeStruct((t, SSM_W), BF16),
        compiler_params=_params("parallel"),
    )(y, proj, w.reshape(1, SSM_W))


def gnorm_bwd(ds, y, proj, w, name):
    t = y.shape[0]
    tr = min(256, t)
    zb = OFF_Z // SSM_W

    def body(ds_ref, y_ref, z_ref, w_ref, dy_ref, dz_ref, dw_ref):
        z = z_ref[...].astype(F32)
        yv = y_ref[...].astype(F32)
        sg = _sigmoid(z)
        sz = z * sg
        yg = yv * sz
        r = lax.rsqrt(jnp.mean(yg * yg, axis=-1, keepdims=True) + EPS)
        xh = yg * r
        g = ds_ref[...].astype(F32)
        dxh = g * w_ref[...]
        dyg = r * (dxh - xh * jnp.mean(dxh * xh, axis=-1, keepdims=True))
        dy_ref[...] = (dyg * sz).astype(BF16)
        dz_ref[...] = (dyg * yv * sg * (1.0 + z * (1.0 - sg))).astype(BF16)
        part = jnp.sum(g * xh, axis=0, keepdims=True)

        @pl.when(pl.program_id(0) == 0)
        def _():
            dw_ref[...] = part

        @pl.when(pl.program_id(0) > 0)
        def _():
            dw_ref[...] += part

    return pl.pallas_call(
        body, name=name, grid=(t // tr,),
        in_specs=[pl.BlockSpec((tr, SSM_W), lambda i: (i, 0)), pl.BlockSpec((tr, SSM_W), lambda i: (i, 0)),
                  pl.BlockSpec((tr, SSM_W), lambda i: (i, zb)), pl.BlockSpec((1, SSM_W), lambda i: (0, 0))],
        out_specs=[pl.BlockSpec((tr, SSM_W), lambda i: (i, 0)), pl.BlockSpec((tr, SSM_W), lambda i: (i, 0)),
                   pl.BlockSpec((1, SSM_W), lambda i: (0, 0))],
        out_shape=[jax.ShapeDtypeStruct((t, SSM_W), BF16), jax.ShapeDtypeStruct((t, SSM_W), BF16),
                   jax.ShapeDtypeStruct((1, SSM_W), F32)],
        compiler_params=_params("arbitrary"),
    )(ds, y, proj, w.reshape(1, SSM_W))


def _pool_mixed(u, g, row):
    win = 2 << g
    acc = u
    for k in range(g + 1):
        acc = acc + _shift_down(acc, 1 << k, row)
    inv = 1.0 / jnp.minimum(row + 1, win).astype(F32)
    return acc * inv - u, inv


def pool_fwd(proj, pool_w, pool_scale, bl, s, name):
    t = bl * s

    def body(u_ref, g_ref, w_ref, sc_ref, o_ref):
        row = lax.broadcasted_iota(jnp.int32, (s, POOL_GD), 0)
        for g in range(POOL_G):
            cs = slice(g * POOL_GD, (g + 1) * POOL_GD)
            u = u_ref[:, cs].astype(F32)
            mixed, _ = _pool_mixed(u, g, row)
            pm = _dot(mixed.astype(BF16), w_ref[g])
            gate = g_ref[:, cs].astype(F32)
            o_ref[:, cs] = (pm * sc_ref[:, cs] * gate * _sigmoid(gate)).astype(BF16)

    return pl.pallas_call(
        body, name=name, grid=(bl,),
        in_specs=[pl.BlockSpec((s, POOL_W), lambda b: (b, OFF_PU // POOL_W)),
                  pl.BlockSpec((s, POOL_W), lambda b: (b, OFF_PG // POOL_W)),
                  pl.BlockSpec((POOL_G, POOL_GD, POOL_GD), lambda b: (0, 0, 0)),
                  pl.BlockSpec((1, POOL_W), lambda b: (0, 0))],
        out_specs=pl.BlockSpec((s, POOL_W), lambda b: (b, 0)),
        out_shape=jax.ShapeDtypeStruct((t, POOL_W), BF16),
        compiler_params=_params("parallel"),
    )(proj, proj, pool_w, pool_scale.reshape(1, POOL_W))


def pool_bwd(dp, proj, pool_w, pool_scale, bl, s, name):
    t = bl * s

    def body(dp_ref, u_ref, g_ref, w_ref, sc_ref, du_ref, dg_ref, dw_ref, dsc_ref):
        row = lax.broadcasted_iota(jnp.int32, (s, POOL_GD), 0)
        first = pl.program_id(0) == 0
        for g in range(POOL_G):
            cs = slice(g * POOL_GD, (g + 1) * POOL_GD)
            u = u_ref[:, cs].astype(F32)
            mixed, inv = _pool_mixed(u, g, row)
            mixed_b = mixed.astype(BF16)
            wg = w_ref[g]
            pm = _dot(mixed_b, wg)
            gate = g_ref[:, cs].astype(F32)
            sg = _sigmoid(gate)
            d = dp_ref[:, cs].astype(F32)
            sc = sc_ref[:, cs]
            dpm = (d * sc * gate * sg).astype(BF16)
            dg_ref[:, cs] = (d * pm * sc * sg * (1.0 + gate * (1.0 - sg))).astype(BF16)
            dsc = jnp.sum(d * pm * gate * sg, axis=0, keepdims=True)
            dwg = _dot(mixed_b, dpm, _TN)
            dmixed = _dot(dpm, wg, _NT)
            acc = dmixed * inv
            for k in range(g + 1):
                acc = acc + _shift_up(acc, 1 << k, row, s)
            du_ref[:, cs] = (acc - dmixed).astype(BF16)

            @pl.when(first)
            def _():
                dw_ref[g] = dwg
                dsc_ref[:, cs] = dsc

            @pl.when(jnp.logical_not(first))
            def _():
                dw_ref[g] = dw_ref[g] + dwg
                dsc_ref[:, cs] = dsc_ref[:, cs] + dsc

    return pl.pallas_call(
        body, name=name, grid=(bl,),
        in_specs=[pl.BlockSpec((s, POOL_W), lambda b: (b, 0)),
                  pl.BlockSpec((s, POOL_W), lambda b: (b, OFF_PU // POOL_W)),
                  pl.BlockSpec((s, POOL_W), lambda b: (b, OFF_PG // POOL_W)),
                  pl.BlockSpec((POOL_G, POOL_GD, POOL_GD), lambda b: (0, 0, 0)),
                  pl.BlockSpec((1, POOL_W), lambda b: (0, 0))],
        out_specs=[pl.BlockSpec((s, POOL_W), lambda b: (b, 0)), pl.BlockSpec((s, POOL_W), lambda b: (b, 0)),
                   pl.BlockSpec((POOL_G, POOL_GD, POOL_GD), lambda b: (0, 0, 0)),
                   pl.BlockSpec((1, POOL_W), lambda b: (0, 0))],
        out_shape=[jax.ShapeDtypeStruct((t, POOL_W), BF16), jax.ShapeDtypeStruct((t, POOL_W), BF16),
                   jax.ShapeDtypeStruct((POOL_G, POOL_GD, POOL_GD), F32), jax.ShapeDtypeStruct((1, POOL_W), F32)],
        compiler_params=_params("arbitrary"),
    )(dp, proj, proj, pool_w, pool_scale.reshape(1, POOL_W))


SB_SCALE = 64 ** -0.5


KB = 256


def _sb_block(qe, kj, mask, rr, upper):
    z = _dot(qe, kj, _NT)
    lb = jnp.minimum(z, 0.0) - jnp.log(1.0 + jnp.exp(-jnp.abs(z)))
    lom = jnp.where(mask, lb - z, 0.0)
    later = _dot(lom.astype(BF16), upper) + rr
    return lb, lom, later


def _sb_masks(i):
    lane = lax.broadcasted_iota(jnp.int32, (QB, 128), 1)
    row = lax.broadcasted_iota(jnp.int32, (2 * QB, KB), 0) % QB
    col = lax.broadcasted_iota(jnp.int32, (2 * QB, KB), 1)
    causal = lambda jb: col + (jb * KB - i * QB) < row
    return lane, lane < 64, causal


def _stack_heads(x, left):
    zero = jnp.zeros_like(x)
    return jnp.concatenate([jnp.where(left, x, zero), jnp.where(left, zero, x)], axis=0)


def sb_fwd(proj, bl, s, name, comm=None):
    t = bl * s
    nq = s // QB
    qb0, kb0, vb0, gb0 = OFF_QKV // 128, OFF_QKV // 128 + 8, OFF_QKV // 128 + 16, OFF_SBG // 128
    grid = (bl, SB_PAIRS, nq)
    x_in, x_args, x_out, x_shape, x_scratch, x_start, x_wait = _hosted_exchange(comm, grid)

    def body(*refs):
        q_ref, k_ref, v_ref, g_ref = refs[:4]
        og_ref, o_ref, r_ref = refs[4 + len(x_in):7 + len(x_in)]
        x_refs = refs[4:4 + len(x_in)] + refs[7 + len(x_in):]
        x_start(x_refs)
        i = pl.program_id(2)
        qs = q_ref[...] * SB_SCALE
        lane, left, causal = _sb_masks(i)
        qcat = _stack_heads(qs, left)
        upper = _tri((KB, KB), lambda r, c: r > c).astype(BF16)
        zero = qcat.astype(F32) * 0.0

        def step(jj, carry):
            acc, rr, rt = carry
            jb = i - jj
            rows = pl.ds(pl.multiple_of(jb * KB, KB), KB)
            mask = causal(jb)
            lb, lom, later = _sb_block(qcat, k_ref[rows, :], mask, rr, upper)
            att = jnp.where(mask, jnp.exp(lb + later), 0.0)
            acc = acc + _dot(att.astype(BF16), v_ref[rows, :])
            rt = jnp.where(lane == jb, rr[:QB], jnp.where(lane == 8 + jb, rr[QB:], rt))
            return acc, rr + jnp.sum(lom, axis=1, keepdims=True), rt

        acc, _, rtile = lax.fori_loop(0, i + 1, step, (zero, zero[:, :1], zero[:QB]))
        o = jnp.where(left, acc[:QB], acc[QB:])
        gate = g_ref[...].astype(F32)
        o_ref[...] = o.astype(BF16)
        og_ref[...] = (o * gate * _sigmoid(gate)).astype(BF16)
        r_ref[...] = rtile
        x_wait(x_refs)

    return pl.pallas_call(
        body, name=name, grid=grid,
        in_specs=[pl.BlockSpec((QB, 128), lambda b, hp, i: (b * nq + i, qb0 + hp)),
                  pl.BlockSpec((s, 128), lambda b, hp, i: (b, kb0 + hp)),
                  pl.BlockSpec((s, 128), lambda b, hp, i: (b, vb0 + hp)),
                  pl.BlockSpec((QB, 128), lambda b, hp, i: (b * nq + i, gb0 + hp))] + x_in,
        out_specs=[pl.BlockSpec((QB, 128), lambda b, hp, i: (b * nq + i, hp)),
                   pl.BlockSpec((QB, 128), lambda b, hp, i: (b * nq + i, hp)),
                   pl.BlockSpec((None, None, QB, 128), lambda b, hp, i: (b * nq + i, hp, 0, 0))] + x_out,
        out_shape=[jax.ShapeDtypeStruct((t, SB_W), BF16), jax.ShapeDtypeStruct((t, SB_W), BF16),
                   jax.ShapeDtypeStruct((bl * nq, SB_PAIRS, QB, 128), F32)] + x_shape,
        scratch_shapes=x_scratch,
        compiler_params=_params("arbitrary", "arbitrary", "arbitrary"),
    )(proj, proj, proj, proj, *x_args)


def sb_bwd(dsb, o, rsave, proj, bl, s, name, comm=None):
    t = bl * s
    nq = s // QB
    qb0, kb0, vb0, gb0 = OFF_QKV // 128, OFF_QKV // 128 + 8, OFF_QKV // 128 + 16, OFF_SBG // 128
    grid = (bl, SB_PAIRS, nq)
    x_in, x_args, x_out, x_shape, x_scratch, x_start, x_wait = _hosted_exchange(comm, grid)

    def body(*refs):
        n = len(x_in)
        d_ref, o_ref, r_ref, q_ref, k_ref, v_ref, g_ref = refs[:7]
        dq_ref, dk_ref, dv_ref, dg_ref = refs[7 + n:11 + n]
        dk_acc, dv_acc = refs[11 + 2 * n:13 + 2 * n]
        x_refs = refs[7:7 + n] + refs[11 + n:11 + 2 * n] + refs[13 + 2 * n:]
        x_start(x_refs)
        i = pl.program_id(2)

        @pl.when(i == 0)
        def _():
            dk_acc[...] = jnp.zeros(dk_acc.shape, F32)
            dv_acc[...] = jnp.zeros(dv_acc.shape, F32)

        qs = q_ref[...] * SB_SCALE
        gate = g_ref[...].astype(F32)
        sg = _sigmoid(gate)
        d = d_ref[...].astype(F32)
        dg_ref[...] = (d * o_ref[...].astype(F32) * sg * (1.0 + gate * (1.0 - sg))).astype(BF16)
        do = (d * gate * sg).astype(BF16)
        rtile = r_ref[...]
        lane, left, causal = _sb_masks(i)
        qcat = _stack_heads(qs, left)
        docat = _stack_heads(do, left)
        upper = _tri((KB, KB), lambda r, c: r > c).astype(BF16)
        lower = _tri((KB, KB), lambda r, c: r < c).astype(BF16)
        zero = qcat.astype(F32) * 0.0

        def step(jb, carry):
            dq, gcar = carry
            rows = pl.ds(pl.multiple_of(jb * KB, KB), KB)
            kj = k_ref[rows, :]
            vj = v_ref[rows, :]
            mask = causal(jb)
            rr = jnp.concatenate(
                [jnp.sum(jnp.where(lane == jb, rtile, 0.0), axis=1, keepdims=True),
                 jnp.sum(jnp.where(lane == 8 + jb, rtile, 0.0), axis=1, keepdims=True)], axis=0)
            lb, lom, later = _sb_block(qcat, kj, mask, rr, upper)
            att = jnp.where(mask, jnp.exp(lb + later), 0.0)
            de = att * _dot(docat, vj, _NT)
            gpre = _dot(de.astype(BF16), lower) + gcar
            sig = jnp.exp(lb)
            dz = jnp.where(mask, de * (1.0 - sig) - gpre * sig, 0.0).astype(BF16)
            dk_acc[rows, :] = dk_acc[rows, :] + _dot(dz, qcat, _TN)
            dv_acc[rows, :] = dv_acc[rows, :] + _dot(att.astype(BF16), docat, _TN)
            return dq + _dot(dz, kj), gcar + jnp.sum(de, axis=1, keepdims=True)

        dq, _ = lax.fori_loop(0, i + 1, step, (zero, zero[:, :1]))
        dq_ref[...] = (jnp.where(left, dq[:QB], dq[QB:]) * SB_SCALE).astype(BF16)

        @pl.when(i == nq - 1)
        def _():
            dk_ref[...] = dk_acc[...].astype(BF16)
            dv_ref[...] = dv_acc[...].astype(BF16)

        x_wait(x_refs)

    blk = lambda b, hp, i: (b * nq + i, hp)
    return pl.pallas_call(
        body, name=name, grid=grid,
        in_specs=[pl.BlockSpec((QB, 128), blk), pl.BlockSpec((QB, 128), blk),
                  pl.BlockSpec((None, None, QB, 128), lambda b, hp, i: (b * nq + i, hp, 0, 0)),
                  pl.BlockSpec((QB, 128), lambda b, hp, i: (b * nq + i, qb0 + hp)),
                  pl.BlockSpec((s, 128), lambda b, hp, i: (b, kb0 + hp)),
                  pl.BlockSpec((s, 128), lambda b, hp, i: (b, vb0 + hp)),
                  pl.BlockSpec((QB, 128), lambda b, hp, i: (b * nq + i, gb0 + hp))] + x_in,
        out_specs=[pl.BlockSpec((QB, 128), blk),
                   pl.BlockSpec((s, 128), lambda b, hp, i: (b, hp)),
                   pl.BlockSpec((s, 128), lambda b, hp, i: (b, hp)),
                   pl.BlockSpec((QB, 128), blk)] + x_out,
        out_shape=[jax.ShapeDtypeStruct((t, SB_W), BF16)] * 4 + x_shape,
        scratch_shapes=[pltpu.VMEM((s, 128), F32), pltpu.VMEM((s, 128), F32)] + x_scratch,
        compiler_params=_params("arbitrary", "arbitrary", "arbitrary"),
    )(dsb, o, rsave, proj, proj, proj, proj, *x_args)


def merge_fwd(proj, ys, yp, yb, name):
    t = ys.shape[0]
    tr = min(512, t)

    def body(m_ref, ys_ref, yp_ref, yb_ref, o_ref):
        acc = jnp.zeros((tr, D), F32)
        for k, ref in enumerate((ys_ref, yp_ref, yb_ref)):
            acc = acc + _sigmoid(m_ref[:, k * D:(k + 1) * D].astype(F32)) * ref[...].astype(F32)
        o_ref[...] = acc.astype(BF16)

    rowblk = pl.BlockSpec((tr, D), lambda i: (i, 0))
    return pl.pallas_call(
        body, name=name, grid=(t // tr,),
        in_specs=[pl.BlockSpec((tr, 3 * D), lambda i: (i, 0)), rowblk, rowblk, rowblk],
        out_specs=rowblk,
        out_shape=jax.ShapeDtypeStruct((t, D), BF16),
        compiler_params=_params("parallel"),
    )(proj, ys, yp, yb)


def merge_bwd(dm, proj, ys, yp, yb, name):
    t = ys.shape[0]
    tr = min(512, t)

    def body(dm_ref, m_ref, ys_ref, yp_ref, yb_ref, d0_ref, d1_ref, d2_ref, dl_ref):
        dmv = dm_ref[...].astype(F32)
        for k, (ref, dref) in enumerate(((ys_ref, d0_ref), (yp_ref, d1_ref), (yb_ref, d2_ref))):
            g = _sigmoid(m_ref[:, k * D:(k + 1) * D].astype(F32))
            dref[...] = (g * dmv).astype(BF16)
            dl_ref[:, k * D:(k + 1) * D] = (dmv * ref[...].astype(F32) * g * (1.0 - g)).astype(BF16)

    rowblk = pl.BlockSpec((tr, D), lambda i: (i, 0))
    wide = pl.BlockSpec((tr, 3 * D), lambda i: (i, 0))
    return pl.pallas_call(
        body, name=name, grid=(t // tr,),
        in_specs=[rowblk, wide, rowblk, rowblk, rowblk],
        out_specs=[rowblk, rowblk, rowblk, wide],
        out_shape=[jax.ShapeDtypeStruct((t, D), BF16)] * 3 + [jax.ShapeDtypeStruct((t, 3 * D), BF16)],
        compiler_params=_params("parallel"),
    )(dm, proj, ys, yp, yb)


def layer_fwd(x, lw, bl, s, tag, comm=None):
    t = bl * s
    h = rmsnorm_fwd(x, lw["norm_w"], f"norm_fwd{tag}")
    proj = matmul(h, lw["w_in"], "nt", BF16, f"in_proj{tag}", tn=2048)
    xa = conv_fwd(proj, lw["conv_w"], lw["conv_b"], bl, s, f"conv_fwd{tag}")
    dtT, acT = dt_fwd(proj, lw["dt_bias"], lw["a_log"], t, f"dt_fwd{tag}")
    dskip_l = jnp.repeat(lw["d_skip"], 64).reshape(1, SSM_W)
    y, prev = ssd_fwd(xa, dtT, acT, dskip_l, bl, s, f"ssd_fwd{tag}")
    s_out = gnorm_fwd(y, proj, lw["ssm_norm_w"], f"gnorm_fwd{tag}")
    p_out = pool_fwd(proj, lw["pool_w"], lw["pool_scale"], bl, s, f"pool_fwd{tag}")
    sb_out, sb_o, sb_r, *carried = sb_fwd(proj, bl, s, f"sb_fwd{tag}", comm)
    ys = matmul(s_out, lw["w_proj_ssm"], "nn", BF16, f"proj_ssm{tag}")
    yp = matmul(p_out, lw["w_proj_pool"], "nn", BF16, f"proj_pool{tag}")
    yb = matmul(sb_out, lw["w_proj_sb"], "nn", BF16, f"proj_sb{tag}")
    merged = merge_fwd(proj, ys, yp, yb, f"merge_fwd{tag}")
    x_next = matmul(merged, lw["w_out"], "nn", F32, f"out_proj{tag}", residual=x)
    saved = dict(x=x, h=h, proj=proj, xa=xa, dtT=dtT, acT=acT, y=y, prev=prev, s_out=s_out, p_out=p_out,
                 sb_out=sb_out, sb_o=sb_o, sb_r=sb_r, ys=ys, yp=yp, yb=yb, merged=merged)
    return x_next, saved, (carried[0] if carried else None)


def layer_bwd(dx, lw, sv, bl, s, tag, comm=None):
    t = bl * s
    g = {}
    dx_b = dx.astype(BF16)
    dmerged = matmul(dx_b, lw["w_out"], "nt", BF16, f"d_merged{tag}")
    g["w_out"] = matmul(sv["merged"], dx_b, "tn", F32, f"dw_out{tag}")
    dys, dyp, dyb, dlogit = merge_bwd(dmerged, sv["proj"], sv["ys"], sv["yp"], sv["yb"], f"merge_bwd{tag}")
    ds_out = matmul(dys, lw["w_proj_ssm"], "nt", BF16, f"d_sout{tag}")
    g["w_proj_ssm"] = matmul(sv["s_out"], dys, "tn", F32, f"dw_proj_ssm{tag}")
    dp_out = matmul(dyp, lw["w_proj_pool"], "nt", BF16, f"d_pout{tag}")
    g["w_proj_pool"] = matmul(sv["p_out"], dyp, "tn", F32, f"dw_proj_pool{tag}")
    dsb_out = matmul(dyb, lw["w_proj_sb"], "nt", BF16, f"d_sbout{tag}")
    g["w_proj_sb"] = matmul(sv["sb_out"], dyb, "tn", F32, f"dw_proj_sb{tag}")
    dy, dz, dnw = gnorm_bwd(ds_out, sv["y"], sv["proj"], lw["ssm_norm_w"], f"gnorm_bwd{tag}")
    g["ssm_norm_w"] = dnw[0]
    dskip_l = jnp.repeat(lw["d_skip"], 64).reshape(1, SSM_W)
    dxs, db, dc, ddtT, dacT, dsk = ssd_bwd(dy, sv["xa"], sv["dtT"], sv["acT"], dskip_l, sv["prev"], bl, s, f"ssd_bwd{tag}")
    g["d_skip"] = jnp.sum(dsk.reshape(N_HEADS, 64), axis=1)
    ddt_raw, da, dbias = dt_bwd(ddtT, dacT, sv["dtT"], sv["proj"], lw["dt_bias"], lw["a_log"], t, f"dt_bwd{tag}")
    g["a_log"] = da * (-jnp.exp(lw["a_log"]))
    g["dt_bias"] = dbias
    dxa = jnp.concatenate([dxs, db.astype(BF16), dc.astype(BF16)], axis=1)
    dxbc, dcw, dcb = conv_bwd(dxa, sv["proj"], lw["conv_w"], lw["conv_b"], bl, s, f"conv_bwd{tag}")
    g["conv_w"] = dcw
    g["conv_b"] = dcb
    dpu, dpg, dpw, dpsc = pool_bwd(dp_out, sv["proj"], lw["pool_w"], lw["pool_scale"], bl, s, f"pool_bwd{tag}")
    g["pool_w"] = dpw
    g["pool_scale"] = dpsc[0]
    dq, dk, dv, dsbg, *carried = sb_bwd(dsb_out, sv["sb_o"], sv["sb_r"], sv["proj"], bl, s, f"sb_bwd{tag}", comm)
    dproj = jnp.concatenate(
        [dlogit, dsbg, dpu, dpg, dz, dq, dk, dv, dxbc, ddt_raw, jnp.zeros((t, PC - OFF_DT - 128), BF16)], axis=1)
    g["w_in"] = matmul(dproj, sv["h"], "tn", F32, f"dw_in{tag}")
    dh = matmul(dproj, lw["w_in"], "nn", F32, f"d_h{tag}")
    dx_in, dnorm = rmsnorm_bwd(dh, sv["x"], lw["norm_w"], dx, f"norm_bwd{tag}")
    g["norm_w"] = dnorm[0]
    return dx_in, g, (carried[0] if carried else None)


_PAD_PIECES = ((10784, 3072), (9760, 1024), (4640, 1024), (5664, 1024), (0, 2048), (6688, 3072), (2048, 2560), (4608, 32))
_UNPAD_PIECES = ((OFF_Z, 2048), (OFF_XBC, 2560), (OFF_DT, 32), (OFF_PU, 1024), (OFF_PG, 1024), (OFF_QKV, 3072),
                 (OFF_SBG, 1024), (OFF_MERGE, 3072))


def pad_rows(wt):
    pieces = [wt[o:o + n] for o, n in _PAD_PIECES]
    return jnp.concatenate(pieces + [jnp.zeros((PC - IN_COLS, wt.shape[1]), wt.dtype)], axis=0)


def unpad_rows(wp):
    return jnp.concatenate([wp[o:o + n] for o, n in _UNPAD_PIECES], axis=0)


MESH = pl.DeviceIdType.MESH
ANY = pl.BlockSpec(memory_space=pl.ANY)


def _coords():
    return lax.axis_index("x"), lax.axis_index("y"), lax.axis_index("c")


def _peer(p):
    x, y, c = _coords()
    return (1 - x if p & 4 else x, 1 - y if p & 2 else y, 1 - c if p & 1 else c)


def _flat(pos):
    return 4 * pos[0] + 2 * pos[1] + pos[2]


def _exchange_copies(v_ref, out_ref, send_sems, recv_sems, local_sem, gather):
    me = _flat(_coords())
    src_of = (lambda k: v_ref) if gather else (lambda k: v_ref.at[k])

    def copy(p, landing):
        peer = _peer(p)
        return pltpu.make_async_remote_copy(
            src_ref=src_of(_flat(peer)), dst_ref=out_ref.at[landing], send_sem=send_sems.at[p - 1],
            recv_sem=recv_sems.at[p - 1], device_id=peer, device_id_type=MESH)

    local = pltpu.make_async_copy(src_of(me), out_ref.at[me], local_sem)
    sends = [copy(p, me) for p in range(1, N_DEV)]
    arrivals = [copy(p, _flat(_peer(p))) for p in range(1, N_DEV)]
    return local, sends, arrivals


def _exchange_start(*refs_and_mode):
    local, sends, _ = _exchange_copies(*refs_and_mode)
    local.start()
    for cp in sends:
        cp.start()


def _exchange_wait(*refs_and_mode):
    local, sends, arrivals = _exchange_copies(*refs_and_mode)
    for cp in arrivals:
        cp.wait_recv()
    for cp in sends:
        cp.wait_send()
    local.wait()


def _exchange_shape(v, gather):
    return jax.ShapeDtypeStruct((N_DEV,) + tuple(v.shape) if gather else tuple(v.shape), v.dtype)


def _exchange_sems():
    return [pltpu.SemaphoreType.DMA((N_DEV - 1,)), pltpu.SemaphoreType.DMA((N_DEV - 1,)), pltpu.SemaphoreType.DMA]


def exchange(v, gather, name):
    def body(v_ref, out_ref, send_sems, recv_sems, local_sem):
        _exchange_start(v_ref, out_ref, send_sems, recv_sems, local_sem, gather)
        _exchange_wait(v_ref, out_ref, send_sems, recv_sems, local_sem, gather)

    return pl.pallas_call(
        body, name=name,
        in_specs=[ANY], out_specs=ANY,
        out_shape=_exchange_shape(v, gather),
        scratch_shapes=_exchange_sems(),
    )(v)


def _hosted_exchange(comm, grid):
    if comm is None:
        return [], [], [], [], [], (lambda refs: None), (lambda refs: None)
    v, gather = comm

    def at(first):
        cond = None
        for axis, n in enumerate(grid):
            term = pl.program_id(axis) == (0 if first else n - 1)
            cond = term if cond is None else jnp.logical_and(cond, term)
        return cond

    def start(refs):
        @pl.when(at(True))
        def _():
            _exchange_start(*refs, gather)

    def wait(refs):
        @pl.when(at(False))
        def _():
            _exchange_wait(*refs, gather)

    return [ANY], [v], [ANY], [_exchange_shape(v, gather)], _exchange_sems(), start, wait


def sum_slabs(v, name):
    _, r, c = v.shape
    tr = 128 if r % 128 == 0 else r

    def body(v_ref, o_ref):
        acc = v_ref[0].astype(F32)
        for k in range(1, N_DEV):
            acc = acc + v_ref[k].astype(F32)
        o_ref[...] = acc

    return pl.pallas_call(
        body, name=name, grid=(r // tr,),
        in_specs=[pl.BlockSpec((N_DEV, tr, c), lambda i: (0, i, 0))],
        out_specs=pl.BlockSpec((tr, c), lambda i: (i, 0)),
        out_shape=jax.ShapeDtypeStruct((r, c), F32),
        compiler_params=_params("parallel"),
    )(v)


def adamw(w, g, m, v, name):
    r, c = w.shape
    tr = next((cand for cand in (256, 128, 64, 32, 16, 8) if r % cand == 0), r)

    def body(w_ref, g_ref, m_ref, v_ref, d_ref, mo_ref, vo_ref):
        gv = g_ref[...]
        mn = ADAM_B1 * m_ref[...] + (1.0 - ADAM_B1) * gv
        vn = ADAM_B2 * v_ref[...] + (1.0 - ADAM_B2) * (gv * gv)
        m_hat = mn / (1.0 - ADAM_B1 ** ADAM_STEP)
        v_hat = vn / (1.0 - ADAM_B2 ** ADAM_STEP)
        d_ref[...] = -ADAM_LR * (m_hat / (jnp.sqrt(v_hat) + ADAM_EPS) + ADAM_WD * w_ref[...])
        mo_ref[...] = mn
        vo_ref[...] = vn

    blk = pl.BlockSpec((tr, c), lambda i: (i, 0))
    return pl.pallas_call(
        body, name=name, grid=(r // tr,),
        in_specs=[blk] * 4, out_specs=[blk] * 3,
        out_shape=[jax.ShapeDtypeStruct((r, c), F32)] * 3,
        compiler_params=_params("parallel"),
    )(w, g, m, v)


BIG = ("w_proj_ssm", "w_proj_pool", "w_proj_sb", "w_out", "pool_w", "w_in")
SHARD_IN = IN_COLS // N_DEV
BIG_ROWS = {"w_proj_ssm": SSM_W // N_DEV, "w_proj_pool": POOL_W // N_DEV, "w_proj_sb": SB_W // N_DEV,
            "w_out": D // N_DEV, "pool_w": POOL_G * (POOL_GD // N_DEV) * POOL_GD // D, "w_in": SHARD_IN}
PACK_C = D
PACK_R = 2432

REPLICATED = ("norm_w", "conv_b", "dt_bias", "a_log", "d_skip", "ssm_norm_w", "pool_scale")
WEIGHTS = ("norm_w", "w_in", "conv_w", "conv_b", "dt_bias", "a_log", "d_skip", "ssm_norm_w", "pool_w",
           "pool_scale", "w_proj_ssm", "w_proj_pool", "w_proj_sb", "w_out", "final_norm_w")


def _size(shape):
    n = 1
    for d in shape:
        n *= d
    return n


def _pad_flat(flat, n):
    return jnp.concatenate([flat, jnp.zeros((n - flat.shape[0],), flat.dtype)])


def _row_offsets():
    offs, off = {}, 0
    for n in BIG:
        offs[n] = off
        off += BIG_ROWS[n]
    return offs, off


def pack_shards(parts):
    rows = [parts[n].reshape(BIG_ROWS[n], PACK_C) for n in BIG]
    rows[-1] = jnp.pad(rows[-1], ((0, PACK_R - _row_offsets()[1]), (0, 0)))
    return jnp.concatenate(rows, axis=0)


def unpack_shards(packed):
    offs, _ = _row_offsets()
    out = {}
    for n in BIG:
        seg = packed[offs[n]:offs[n] + BIG_ROWS[n]]
        if n == "w_in":
            out[n] = seg.T
        elif n == "pool_w":
            out[n] = seg.reshape(POOL_G, POOL_GD // N_DEV, POOL_GD)
        else:
            out[n] = seg
    return out


def unpack_gathered(g):
    offs, _ = _row_offsets()
    out = {}
    for n in BIG:
        seg = g[:, offs[n]:offs[n] + BIG_ROWS[n], :]
        if n == "w_in":
            out[n] = pad_rows(seg.reshape(IN_COLS, D))
        elif n == "pool_w":
            out[n] = seg.reshape(N_DEV, POOL_G, POOL_GD // N_DEV, POOL_GD).transpose(1, 0, 2, 3).reshape(
                POOL_G, POOL_GD, POOL_GD)
        else:
            out[n] = seg.reshape(N_DEV * BIG_ROWS[n], D)
    return out


def pack_slabs(g):
    segs = []
    for n in BIG:
        if n == "w_in":
            w = unpad_rows(g[n])
        elif n == "pool_w":
            w = g[n].reshape(POOL_G, N_DEV, POOL_GD // N_DEV, POOL_GD).transpose(1, 0, 2, 3)
        else:
            w = g[n]
        segs.append(w.reshape(N_DEV, BIG_ROWS[n], PACK_C).astype(BF16))
    segs[-1] = jnp.pad(segs[-1], ((0, 0), (0, PACK_R - _row_offsets()[1]), (0, 0)))
    return jnp.concatenate(segs, axis=1)


SMALL_ROWS = 544


def pack_small(vals):
    flat = jnp.concatenate([v.reshape(-1) for v in vals])
    return _pad_flat(flat, SMALL_ROWS * 128).reshape(SMALL_ROWS, 128)


def unpack_small(packed, shapes):
    flat = packed.reshape(-1)
    out, off = [], 0
    for shp in shapes:
        out.append(flat[off:off + _size(shp)].reshape(shp))
        off += _size(shp)
    return out


def kernel(x, norm_w, w_in, conv_w, conv_b, dt_bias, a_log, d_skip, ssm_norm_w, pool_w, pool_scale, w_proj_ssm, w_proj_pool, w_proj_sb, w_out, final_norm_w, loss_target, m_norm_w, m_w_in, m_conv_w, m_conv_b, m_dt_bias, m_a_log, m_d_skip, m_ssm_norm_w, m_pool_w, m_pool_scale, m_w_proj_ssm, m_w_proj_pool, m_w_proj_sb, m_w_out, m_final_norm_w, v_norm_w, v_w_in, v_conv_w, v_conv_b, v_dt_bias, v_a_log, v_d_skip, v_ssm_norm_w, v_pool_w, v_pool_scale, v_w_proj_ssm, v_w_proj_pool, v_w_proj_sb, v_w_out, v_final_norm_w):
    wts = dict(norm_w=norm_w, w_in=w_in, conv_w=conv_w, conv_b=conv_b, dt_bias=dt_bias, a_log=a_log, d_skip=d_skip,
               ssm_norm_w=ssm_norm_w, pool_w=pool_w, pool_scale=pool_scale, w_proj_ssm=w_proj_ssm,
               w_proj_pool=w_proj_pool, w_proj_sb=w_proj_sb, w_out=w_out, final_norm_w=final_norm_w)
    mom = dict(norm_w=m_norm_w, w_in=m_w_in, conv_w=m_conv_w, conv_b=m_conv_b, dt_bias=m_dt_bias, a_log=m_a_log,
               d_skip=m_d_skip, ssm_norm_w=m_ssm_norm_w, pool_w=m_pool_w, pool_scale=m_pool_scale,
               w_proj_ssm=m_w_proj_ssm, w_proj_pool=m_w_proj_pool, w_proj_sb=m_w_proj_sb, w_out=m_w_out,
               final_norm_w=m_final_norm_w)
    var = dict(norm_w=v_norm_w, w_in=v_w_in, conv_w=v_conv_w, conv_b=v_conv_b, dt_bias=v_dt_bias, a_log=v_a_log,
               d_skip=v_d_skip, ssm_norm_w=v_ssm_norm_w, pool_w=v_pool_w, pool_scale=v_pool_scale,
               w_proj_ssm=v_w_proj_ssm, w_proj_pool=v_w_proj_pool, w_proj_sb=v_w_proj_sb, w_out=v_w_out,
               final_norm_w=v_final_norm_w)
    bl, s, _ = x.shape
    t = bl * s
    me = _flat(_coords())

    cw = exchange(conv_w.reshape(40, 128), True, "gather_conv_w")
    conv_w_full = cw.reshape(N_DEV, DEPTH, CONV_K, CONV_CH // N_DEV).transpose(1, 2, 0, 3).reshape(
        DEPTH, CONV_K, CONV_CH)

    xc = x.reshape(t, D)
    layer_w, saved = [], []
    packed = [pack_shards({n: (wts[n][l].T if n == "w_in" else wts[n][l]).astype(BF16) for n in BIG})
              for l in range(DEPTH)]
    gathered = exchange(packed[0], True, "gather_w0")
    for l in range(DEPTH):
        lw = unpack_gathered(gathered)
        for n in REPLICATED:
            lw[n] = wts[n][l]
        lw["conv_w"] = conv_w_full[l]
        xc, sv, gathered = layer_fwd(xc, lw, bl, s, f"_l{l}", (packed[l + 1], True) if l + 1 < DEPTH else None)
        layer_w.append(lw)
        saved.append(sv)

    loss_part, dx, dfinal = final_loss(xc, final_norm_w, loss_target.reshape(t, D), "final_loss")
    loss = lax.psum(loss_part[0, 0], ("x", "y", "c"))

    grads = [None] * DEPTH
    big_sum = [None] * DEPTH
    slabs = None
    for l in reversed(range(DEPTH)):
        dx, g, got = layer_bwd(dx, layer_w[l], saved[l], bl, s, f"_l{l}", (slabs, False) if slabs is not None else None)
        if got is not None:
            big_sum[l + 1] = unpack_shards(sum_slabs(got, f"sum_g{l + 1}"))
        grads[l] = g
        slabs = pack_slabs(g)
    big_sum[0] = unpack_shards(sum_slabs(exchange(slabs, False, "scatter_g0"), "sum_g0"))
    grad_x = dx.reshape(bl, s, D)

    small_names = REPLICATED + ("conv_w",)
    small_vals = [jnp.stack([grads[l][n] for l in range(DEPTH)]) for n in small_names] + [dfinal[0]]
    small_shapes = [v.shape for v in small_vals]
    small_all = exchange(pack_small(small_vals), True, "gather_small")
    small_sum = unpack_small(sum_slabs(small_all, "sum_small"), small_shapes)
    gsum = dict(zip(small_names + ("final_norm_w",), small_sum))
    conv_g_full = gsum["conv_w"]
    gsum["conv_w"] = lax.dynamic_slice_in_dim(conv_g_full, me * (CONV_CH // N_DEV), CONV_CH // N_DEV, axis=2)
    for n in BIG:
        gsum[n] = jnp.stack([big_sum[l][n] for l in range(DEPTH)])

    delta, new_m, new_v = {}, {}, {}
    for n in BIG + ("conv_w",):
        shp = wts[n].shape
        two_d = (-1, shp[-1])
        d2, m2, v2 = adamw(wts[n].reshape(two_d), gsum[n].reshape(two_d), mom[n].reshape(two_d),
                           var[n].reshape(two_d), f"adamw_{n}")
        delta[n], new_m[n], new_v[n] = d2.reshape(shp), m2.reshape(shp), v2.reshape(shp)
    rep = REPLICATED + ("final_norm_w",)
    rep_shapes = [wts[n].shape for n in rep]
    d2, m2, v2 = adamw(pack_small([wts[n] for n in rep]), pack_small([gsum[n] for n in rep]),
                       pack_small([mom[n] for n in rep]), pack_small([var[n] for n in rep]), "adamw_small")
    for n, dv, mv, vv in zip(rep, unpack_small(d2, rep_shapes), unpack_small(m2, rep_shapes),
                             unpack_small(v2, rep_shapes)):
        delta[n], new_m[n], new_v[n] = dv, mv, vv

    return (loss, grad_x, *[gsum[n] for n in WEIGHTS], *[delta[n] for n in WEIGHTS],
            *[new_m[n] for n in WEIGHTS], *[new_v[n] for n in WEIGHTS])
```

```python
import functools

import jax
import jax.numpy as jnp
from jax import lax
from jax.experimental import pallas as pl
from jax.experimental.pallas import tpu as pltpu

F32 = jnp.float32
BF16 = jnp.bfloat16

N_DEV = 8
DEPTH = 4
D = 1024
SSM_W = 2048
N_HEADS = 32
N_PAIRS = 16
N_GROUPS = 2
N_STATE = 128
CHUNK = 128
CONV_CH = 2560
CONV_K = 4
POOL_W = 1024
POOL_G = 4
POOL_GD = 256
SB_W = 1024
SB_PAIRS = 8
QB = 256
EPS = 1e-6
IN_COLS = 13856

PC = 14336
OFF_MERGE = 0
OFF_SBG = 3072
OFF_PU = 4096
OFF_PG = 5120
OFF_Z = 6144
OFF_QKV = 8192
OFF_XBC = 11264
OFF_DT = 13824

ADAM_LR = 0.001
ADAM_B1 = 0.9
ADAM_B2 = 0.999
ADAM_EPS = 1e-08
ADAM_WD = 0.01
ADAM_STEP = 10

VMEM_LIMIT = 56 * 1024 * 1024

_NN = (((1,), (0,)), ((), ()))
_NT = (((1,), (1,)), ((), ()))
_TN = (((0,), (0,)), ((), ()))


def _dot(a, b, dn=_NN):
    return lax.dot_general(a, b, dn, preferred_element_type=F32)


def _sigmoid(x):
    return 1.0 / (1.0 + jnp.exp(-x))


def _softplus(x):
    return jnp.maximum(x, 0.0) + jnp.log(1.0 + jnp.exp(-jnp.abs(x)))


def _split2(x):
    hi = x.astype(BF16)
    lo = (x - hi.astype(F32)).astype(BF16)
    return hi, lo


def _split3(x):
    hi = x.astype(BF16)
    r = x - hi.astype(F32)
    mid = r.astype(BF16)
    lo = (r - mid.astype(F32)).astype(BF16)
    return hi, mid, lo


def _params(*sem):
    return pltpu.CompilerParams(dimension_semantics=sem, vmem_limit_bytes=VMEM_LIMIT)


def matmul(a, b, mode, out_dtype, name, residual=None, tm=1024, tn=1024, tk=1024):
    if mode == "nn":
        (m, k), (k2, n) = a.shape, b.shape
    elif mode == "nt":
        (m, k), (n, k2) = a.shape, b.shape
    else:
        (k, m), (k2, n) = a.shape, b.shape
    assert k == k2
    tm, tn, tk = min(tm, m), min(tn, n), min(tk, k)
    assert m % tm == 0 and n % tn == 0 and k % tk == 0
    nk = k // tk
    dn = {"nn": _NN, "nt": _NT, "tn": _TN}[mode]
    a_spec = pl.BlockSpec((tk, tm), lambda i, j, kk: (kk, i)) if mode == "tn" else pl.BlockSpec((tm, tk), lambda i, j, kk: (i, kk))
    b_spec = pl.BlockSpec((tn, tk), lambda i, j, kk: (j, kk)) if mode == "nt" else pl.BlockSpec((tk, tn), lambda i, j, kk: (kk, j))
    in_specs = [a_spec, b_spec]
    args = [a, b]
    if residual is not None:
        in_specs.append(pl.BlockSpec((tm, tn), lambda i, j, kk: (i, j)))
        args.append(residual)

    def body(*refs):
        if residual is not None:
            a_ref, b_ref, r_ref, o_ref, acc_ref = refs
        else:
            a_ref, b_ref, o_ref, acc_ref = refs
            r_ref = None
        kk = pl.program_id(2)
        p = _dot(a_ref[...], b_ref[...], dn)

        def finish(val):
            if r_ref is not None:
                val = val + r_ref[...]
            o_ref[...] = val.astype(out_dtype)

        if nk == 1:
            finish(p)
        else:
            @pl.when(kk == 0)
            def _():
                acc_ref[...] = p

            @pl.when(kk > 0)
            def _():
                acc_ref[...] += p

            @pl.when(kk == nk - 1)
            def _():
                finish(acc_ref[...])

    return pl.pallas_call(
        body, name=name,
        grid=(m // tm, n // tn, nk),
        in_specs=in_specs,
        out_specs=pl.BlockSpec((tm, tn), lambda i, j, kk: (i, j)),
        out_shape=jax.ShapeDtypeStruct((m, n), out_dtype),
        scratch_shapes=[pltpu.VMEM((tm, tn) if nk > 1 else (8, 128), F32)],
        compiler_params=_params("parallel", "parallel", "arbitrary"),
    )(*args)


def rmsnorm_fwd(x, w, name):
    t, d = x.shape
    tr = min(512, t)

    def body(x_ref, w_ref, h_ref):
        xv = x_ref[...]
        r = lax.rsqrt(jnp.mean(xv * xv, axis=-1, keepdims=True) + EPS)
        h_ref[...] = (xv * r * w_ref[...]).astype(BF16)

    return pl.pallas_call(
        body, name=name, grid=(t // tr,),
        in_specs=[pl.BlockSpec((tr, d), lambda i: (i, 0)), pl.BlockSpec((1, d), lambda i: (0, 0))],
        out_specs=pl.BlockSpec((tr, d), lambda i: (i, 0)),
        out_shape=jax.ShapeDtypeStruct((t, d), BF16),
        compiler_params=_params("parallel"),
    )(x, w.reshape(1, d))


def rmsnorm_bwd(dh, x, w, dres, name):
    t, d = x.shape
    tr = min(512, t)

    def body(dh_ref, x_ref, w_ref, dres_ref, dx_ref, dxb_ref, dw_ref):
        xv = x_ref[...]
        r = lax.rsqrt(jnp.mean(xv * xv, axis=-1, keepdims=True) + EPS)
        xh = xv * r
        g = dh_ref[...].astype(F32)
        dxh = g * w_ref[...]
        dxv = dres_ref[...] + r * (dxh - xh * jnp.mean(dxh * xh, axis=-1, keepdims=True))
        dx_ref[...] = dxv
        dxb_ref[...] = dxv.astype(BF16)
        part = jnp.sum(g * xh, axis=0, keepdims=True)

        @pl.when(pl.program_id(0) == 0)
        def _():
            dw_ref[...] = part

        @pl.when(pl.program_id(0) > 0)
        def _():
            dw_ref[...] += part

    return pl.pallas_call(
        body, name=name, grid=(t // tr,),
        in_specs=[pl.BlockSpec((tr, d), lambda i: (i, 0)), pl.BlockSpec((tr, d), lambda i: (i, 0)),
                  pl.BlockSpec((1, d), lambda i: (0, 0)), pl.BlockSpec((tr, d), lambda i: (i, 0))],
        out_specs=[pl.BlockSpec((tr, d), lambda i: (i, 0)), pl.BlockSpec((tr, d), lambda i: (i, 0)),
                   pl.BlockSpec((1, d), lambda i: (0, 0))],
        out_shape=[jax.ShapeDtypeStruct((t, d), F32), jax.ShapeDtypeStruct((t, d), BF16),
                   jax.ShapeDtypeStruct((1, d), F32)],
        compiler_params=_params("arbitrary"),
    )(dh, x, w.reshape(1, d), dres)


def final_loss(x, w, target, name):
    t, d = x.shape
    tr = min(512, t)

    def body(x_ref, w_ref, tg_ref, loss_ref, dx_ref, dxb_ref, dw_ref):
        xv = x_ref[...]
        r = lax.rsqrt(jnp.mean(xv * xv, axis=-1, keepdims=True) + EPS)
        xh = xv * r
        err = xh * w_ref[...] - tg_ref[...]
        lpart = 0.5 * jnp.sum(jnp.mean(err * err, axis=-1, keepdims=True), axis=0, keepdims=True)
        dy = err * (1.0 / d)
        dxh = dy * w_ref[...]
        dxv = r * (dxh - xh * jnp.mean(dxh * xh, axis=-1, keepdims=True))
        dx_ref[...] = dxv
        dxb_ref[...] = dxv.astype(BF16)
        part = jnp.sum(dy * xh, axis=0, keepdims=True)

        @pl.when(pl.program_id(0) == 0)
        def _():
            dw_ref[...] = part
            loss_ref[...] = jnp.broadcast_to(lpart, (1, 128))

        @pl.when(pl.program_id(0) > 0)
        def _():
            dw_ref[...] += part
            loss_ref[...] += jnp.broadcast_to(lpart, (1, 128))

    return pl.pallas_call(
        body, name=name, grid=(t // tr,),
        in_specs=[pl.BlockSpec((tr, d), lambda i: (i, 0)), pl.BlockSpec((1, d), lambda i: (0, 0)),
                  pl.BlockSpec((tr, d), lambda i: (i, 0))],
        out_specs=[pl.BlockSpec((1, 128), lambda i: (0, 0)), pl.BlockSpec((tr, d), lambda i: (i, 0)),
                   pl.BlockSpec((tr, d), lambda i: (i, 0)), pl.BlockSpec((1, d), lambda i: (0, 0))],
        out_shape=[jax.ShapeDtypeStruct((1, 128), F32), jax.ShapeDtypeStruct((t, d), F32),
                   jax.ShapeDtypeStruct((t, d), BF16), jax.ShapeDtypeStruct((1, d), F32)],
        compiler_params=_params("arbitrary"),
    )(x, w.reshape(1, d), target)


CONV_BW = 256


def _shift_down(u, s, row):
    return jnp.where(row >= s, pltpu.roll(u, s, axis=0), 0.0)


def _shift_up(u, s, row, n):
    return jnp.where(row < n - s, pltpu.roll(u, n - s, axis=0), 0.0)


def _conv_pre(u, w, b, row):
    acc = b + w[CONV_K - 1:CONV_K, :] * u
    for k in range(CONV_K - 1):
        acc = acc + w[k:k + 1, :] * _shift_down(u, CONV_K - 1 - k, row)
    return acc


def conv_fwd(proj, conv_w, conv_b, bl, s, name):
    t = bl * s
    nb = CONV_CH // CONV_BW
    off = OFF_XBC // CONV_BW

    def body(u_ref, w_ref, b_ref, o_ref):
        u = u_ref[...].astype(F32)
        row = lax.broadcasted_iota(jnp.int32, u.shape, 0)
        xc = _conv_pre(u, w_ref[...], b_ref[...], row)
        o_ref[...] = (xc * _sigmoid(xc)).astype(BF16)

    return pl.pallas_call(
        body, name=name, grid=(bl, nb),
        in_specs=[pl.BlockSpec((s, CONV_BW), lambda b, j: (b, off + j)),
                  pl.BlockSpec((CONV_K, CONV_BW), lambda b, j: (0, j)),
                  pl.BlockSpec((1, CONV_BW), lambda b, j: (0, j))],
        out_specs=pl.BlockSpec((s, CONV_BW), lambda b, j: (b, j)),
        out_shape=jax.ShapeDtypeStruct((t, CONV_CH), BF16),
        compiler_params=_params("parallel", "parallel"),
    )(proj, conv_w, conv_b.reshape(1, CONV_CH))


def conv_bwd(dxa, proj, conv_w, conv_b, bl, s, name):
    t = bl * s
    nb = CONV_CH // CONV_BW
    off = OFF_XBC // CONV_BW

    def body(d_ref, u_ref, w_ref, b_ref, du_ref, dw_ref, db_ref):
        u = u_ref[...].astype(F32)
        w = w_ref[...]
        row = lax.broadcasted_iota(jnp.int32, u.shape, 0)
        xc = _conv_pre(u, w, b_ref[...], row)
        sg = _sigmoid(xc)
        dxc = d_ref[...].astype(F32) * sg * (1.0 + xc * (1.0 - sg))
        du = w[CONV_K - 1:CONV_K, :] * dxc
        dws = [None] * CONV_K
        dws[CONV_K - 1] = jnp.sum(dxc * u, axis=0, keepdims=True)
        for k in range(CONV_K - 1):
            sh = CONV_K - 1 - k
            du = du + w[k:k + 1, :] * _shift_up(dxc, sh, row, s)
            dws[k] = jnp.sum(dxc * _shift_down(u, sh, row), axis=0, keepdims=True)
        du_ref[...] = du.astype(BF16)
        krow = lax.broadcasted_iota(jnp.int32, (8, CONV_BW), 0)
        dwv = sum(jnp.where(krow == k, dws[k], 0.0) for k in range(CONV_K))
        dbv = jnp.sum(dxc, axis=0, keepdims=True)

        @pl.when(pl.program_id(1) == 0)
        def _():
            dw_ref[...] = dwv
            db_ref[...] = dbv

        @pl.when(pl.program_id(1) > 0)
        def _():
            dw_ref[...] += dwv
            db_ref[...] += dbv

    du, dw, db = pl.pallas_call(
        body, name=name, grid=(nb, bl),
        in_specs=[pl.BlockSpec((s, CONV_BW), lambda j, b: (b, j)),
                  pl.BlockSpec((s, CONV_BW), lambda j, b: (b, off + j)),
                  pl.BlockSpec((CONV_K, CONV_BW), lambda j, b: (0, j)),
                  pl.BlockSpec((1, CONV_BW), lambda j, b: (0, j))],
        out_specs=[pl.BlockSpec((s, CONV_BW), lambda j, b: (b, j)),
                   pl.BlockSpec((8, CONV_BW), lambda j, b: (0, j)),
                   pl.BlockSpec((1, CONV_BW), lambda j, b: (0, j))],
        out_shape=[jax.ShapeDtypeStruct((t, CONV_CH), BF16), jax.ShapeDtypeStruct((8, CONV_CH), F32),
                   jax.ShapeDtypeStruct((1, CONV_CH), F32)],
        compiler_params=_params("parallel", "arbitrary"),
    )(dxa, proj, conv_w, conv_b.reshape(1, CONV_CH))
    return du, dw[:CONV_K], db[0]


def _tri(shape, cmp):
    r = lax.broadcasted_iota(jnp.int32, shape, 0)
    c = lax.broadcasted_iota(jnp.int32, shape, 1)
    return cmp(r, c)


def dt_fwd(proj, dt_bias, a_log, t, name):
    nchunks = t // CHUNK
    bias = jnp.zeros((1, 128), F32).at[0, :N_HEADS].set(dt_bias)
    alog = jnp.zeros((1, 128), F32).at[0, :N_HEADS].set(a_log)

    def body(raw_ref, b_ref, al_ref, dt_ref, ac_ref):
        raw = raw_ref[...].astype(F32)
        dt = _softplus(raw + b_ref[...])
        adt = dt * (-jnp.exp(al_ref[...]))
        low = _tri((CHUNK, CHUNK), lambda r, c: r >= c).astype(BF16)
        acum = sum(_dot(low, part) for part in _split3(adt))
        dt_ref[...] = dt.T[:N_HEADS]
        ac_ref[...] = acum.T[:N_HEADS]

    return pl.pallas_call(
        body, name=name, grid=(nchunks,),
        in_specs=[pl.BlockSpec((CHUNK, 128), lambda i: (i, OFF_DT // 128)),
                  pl.BlockSpec((1, 128), lambda i: (0, 0)), pl.BlockSpec((1, 128), lambda i: (0, 0))],
        out_specs=[pl.BlockSpec((None, N_HEADS, CHUNK), lambda i: (i, 0, 0))] * 2,
        out_shape=[jax.ShapeDtypeStruct((nchunks, N_HEADS, CHUNK), F32)] * 2,
        compiler_params=_params("parallel"),
    )(proj, bias, alog)


def dt_bwd(ddtT, dacT, dtT, proj, dt_bias, a_log, t, name):
    nchunks = t // CHUNK
    bias = dt_bias.reshape(N_HEADS, 1)
    alog = a_log.reshape(N_HEADS, 1)

    def body(ddt_ref, dac_ref, dt_ref, raw_ref, b_ref, al_ref, draw_ref, da_ref, db_ref):
        a = -jnp.exp(al_ref[...])
        upp = _tri((CHUNK, CHUNK), lambda r, c: r >= c).astype(BF16)
        dadt = sum(_dot(part, upp) for part in _split3(dac_ref[...]))
        ddt = ddt_ref[...] + dadt * a
        rawT = raw_ref[...].astype(F32).T[:N_HEADS]
        draw = ddt * _sigmoid(rawT + b_ref[...])
        padded = jnp.concatenate([draw, jnp.zeros((128 - N_HEADS, CHUNK), F32)], axis=0)
        draw_ref[...] = padded.T.astype(BF16)
        dav = dadt * dt_ref[...]

        @pl.when(pl.program_id(0) == 0)
        def _():
            da_ref[...] = dav
            db_ref[...] = draw

        @pl.when(pl.program_id(0) > 0)
        def _():
            da_ref[...] += dav
            db_ref[...] += draw

    draw, da, db = pl.pallas_call(
        body, name=name, grid=(nchunks,),
        in_specs=[pl.BlockSpec((None, N_HEADS, CHUNK), lambda i: (i, 0, 0))] * 3
        + [pl.BlockSpec((CHUNK, 128), lambda i: (i, OFF_DT // 128)),
           pl.BlockSpec((N_HEADS, 1), lambda i: (0, 0)), pl.BlockSpec((N_HEADS, 1), lambda i: (0, 0))],
        out_specs=[pl.BlockSpec((CHUNK, 128), lambda i: (i, 0)),
                   pl.BlockSpec((N_HEADS, CHUNK), lambda i: (0, 0)), pl.BlockSpec((N_HEADS, CHUNK), lambda i: (0, 0))],
        out_shape=[jax.ShapeDtypeStruct((t, 128), BF16), jax.ShapeDtypeStruct((N_HEADS, CHUNK), F32),
                   jax.ShapeDtypeStruct((N_HEADS, CHUNK), F32)],
        compiler_params=_params("arbitrary"),
    )(ddtT, dacT, dtT, proj, bias, alog)
    return draw, jnp.sum(da, axis=1), jnp.sum(db, axis=1)


PAIRS_G = N_PAIRS // N_GROUPS
GROUP_W = PAIRS_G * 128


def _colb(r):
    return jnp.broadcast_to(r, (CHUNK, 128)).T


def _ssd_pair(x, dtr, acr, tri):
    left = lax.broadcasted_iota(jnp.int32, (CHUNK, 128), 1) < 64
    ac_cols = [_colb(acr[e:e + 1]) for e in range(2)]
    dtl = jnp.where(left, _colb(dtr[0:1]), _colb(dtr[1:2]))
    acl = jnp.where(left, ac_cols[0], ac_cols[1])
    dks = [jnp.exp(jnp.where(tri, ac_cols[e] - acr[e:e + 1], -1e30)) for e in range(2)]
    aclast = acl[CHUNK - 1:CHUNK, :]
    return left, dtl, acl, x * dtl, dks, aclast


def _ssd_specs(nc, rev):
    row = (lambda b, c, g: b * nc + (nc - 1 - c)) if rev else (lambda b, c, g: b * nc + c)
    return dict(
        wide=pl.BlockSpec((CHUNK, GROUP_W), lambda b, c, g: (row(b, c, g), g)),
        bmat=pl.BlockSpec((CHUNK, 128), lambda b, c, g: (row(b, c, g), SSM_W // 128 + g)),
        cmat=pl.BlockSpec((CHUNK, 128), lambda b, c, g: (row(b, c, g), SSM_W // 128 + N_GROUPS + g)),
        rows2=pl.BlockSpec((None, PAIRS_G, 2, CHUNK), lambda b, c, g: (row(b, c, g), g, 0, 0)),
        rows8=pl.BlockSpec((None, PAIRS_G, 8, CHUNK), lambda b, c, g: (row(b, c, g), g, 0, 0)),
        dskip=pl.BlockSpec((1, GROUP_W), lambda b, c, g: (0, g)),
        state=pl.BlockSpec((None, PAIRS_G, N_STATE, 128), lambda b, c, g: (row(b, c, g), g, 0, 0)),
        narrow=pl.BlockSpec((CHUNK, 128), lambda b, c, g: (row(b, c, g), g)))


def ssd_fwd(xa, dtT, acT, dskip_l, bl, s, name):
    t = bl * s
    nc = s // CHUNK
    dt4 = dtT.reshape(bl * nc, N_PAIRS, 2, CHUNK)
    ac4 = acT.reshape(bl * nc, N_PAIRS, 2, CHUNK)

    def body(x_ref, b_ref, c_ref, dt_ref, ac_ref, dsk_ref, y_ref, prev_ref, st_ref):
        c = pl.program_id(1)
        g = pl.program_id(2)
        bm = b_ref[...]
        cm = c_ref[...]
        cb = _dot(cm, bm, _NT)
        tri = _tri((CHUNK, CHUNK), lambda r, c: r >= c)

        @pl.when(c == 0)
        def _():
            for p in range(PAIRS_G):
                st_ref[g * PAIRS_G + p] = jnp.zeros((N_STATE, 128), F32)

        for p in range(PAIRS_G):
            hp = g * PAIRS_G + p
            cs = slice(p * 128, (p + 1) * 128)
            x = x_ref[:, cs].astype(F32)
            left, dtl, acl, xdt, dks, aclast = _ssd_pair(x, dt_ref[p], ac_ref[p], tri)
            xdt_b = xdt.astype(BF16)
            ys = [_dot((cb * dks[e]).astype(BF16), xdt_b) for e in range(2)]
            st = st_ref[hp]
            y_off = _dot(cm, st.astype(BF16)) * jnp.exp(acl)
            y_ref[:, cs] = (jnp.where(left, ys[0], ys[1]) + y_off + x * dsk_ref[:, cs]).astype(BF16)
            xw = (xdt * jnp.exp(aclast - acl)).astype(BF16)
            prev_ref[p] = st
            st_ref[hp] = st * jnp.exp(aclast) + _dot(bm, xw, _TN)

    sp = _ssd_specs(nc, False)
    return pl.pallas_call(
        body, name=name, grid=(bl, nc, N_GROUPS),
        in_specs=[sp["wide"], sp["bmat"], sp["cmat"], sp["rows2"], sp["rows2"], sp["dskip"]],
        out_specs=[sp["wide"], sp["state"]],
        out_shape=[jax.ShapeDtypeStruct((t, SSM_W), BF16),
                   jax.ShapeDtypeStruct((bl * nc, N_PAIRS, N_STATE, 128), F32)],
        scratch_shapes=[pltpu.VMEM((N_PAIRS, N_STATE, 128), F32)],
        compiler_params=_params("parallel", "arbitrary", "arbitrary"),
    )(xa, xa, xa, dt4, ac4, dskip_l)


def ssd_bwd(dy, xa, dtT, acT, dskip_l, prev, bl, s, name):
    t = bl * s
    nc = s // CHUNK
    dt4 = dtT.reshape(bl * nc, N_PAIRS, 2, CHUNK)
    ac4 = acT.reshape(bl * nc, N_PAIRS, 2, CHUNK)

    def body(dy_ref, x_ref, b_ref, c_ref, dt_ref, ac_ref, dsk_ref, prev_ref,
             dx_ref, db_ref, dc_ref, dd_ref, dsk_out_ref, dp_ref):
        b = pl.program_id(0)
        cr = pl.program_id(1)
        g = pl.program_id(2)

        @pl.when(cr == 0)
        def _():
            for p in range(PAIRS_G):
                dp_ref[g * PAIRS_G + p] = jnp.zeros((N_STATE, 128), F32)

        @pl.when((b == 0) & (cr == 0) & (g == 0))
        def _():
            dsk_out_ref[...] = jnp.zeros(dsk_out_ref.shape, F32)

        bm = b_ref[...]
        cm = c_ref[...]
        cb = _dot(cm, bm, _NT)
        tri = _tri((CHUNK, CHUNK), lambda r, c: r >= c)
        lane = lax.broadcasted_iota(jnp.int32, (CHUNK, 128), 1)
        lrow = lax.broadcasted_iota(jnp.int32, (1, CHUNK), 1)
        krow = lax.broadcasted_iota(jnp.int32, (8, CHUNK), 0)
        dcb = jnp.zeros((CHUNK, CHUNK), F32)
        dc_acc = jnp.zeros((CHUNK, N_STATE), F32)
        db_acc = jnp.zeros((CHUNK, N_STATE), F32)
        for p in range(PAIRS_G):
            hp = g * PAIRS_G + p
            cs = slice(p * 128, (p + 1) * 128)
            x = x_ref[:, cs].astype(F32)
            left, dtl, acl, xdt, dks, aclast = _ssd_pair(x, dt_ref[p], ac_ref[p], tri)
            dyv = dy_ref[:, cs].astype(F32)
            dy_b = dyv.astype(BF16)
            xdt_b = xdt.astype(BF16)
            st = prev_ref[p]
            st_b = st.astype(BF16)
            ea = jnp.exp(acl)
            ds = jnp.exp(aclast - acl)
            cdl = jnp.exp(aclast)
            xw = xdt * ds
            masks = [left, jnp.logical_not(left)]

            dsk_out_ref[hp] = dsk_out_ref[hp] + jnp.sum(dyv * x, axis=0, keepdims=True)

            yo = _dot(cm, st_b)
            dyo_b = (dyv * ea).astype(BF16)
            yoff_term = dyv * yo * ea
            dc_acc = dc_acc + _dot(dyo_b, st_b, _NT)
            dst = _dot(cm, dyo_b, _TN)
            dsv = dp_ref[hp]
            dsv_b = dsv.astype(BF16)
            dxw = _dot(bm, dsv_b)
            db_acc = db_acc + _dot(xw.astype(BF16), dsv_b, _NT)
            dxdt = dxw * ds
            qv = dxw * xw
            end_term = dsv * st * cdl
            dp_ref[hp] = dsv * cdl + dst

            cols = jnp.zeros((CHUNK, 128), F32)
            rows = []
            for e in range(2):
                m = cb * dks[e]
                dy_e = jnp.where(masks[e], dyv, 0.0).astype(BF16)
                dm = _dot(dy_e, xdt_b, _NT)
                w = dm * m
                dcb = dcb + dm * dks[e]
                dxdt = dxdt + jnp.where(masks[e], _dot(m.astype(BF16), dy_b, _TN), 0.0)
                dac_col = (jnp.sum(w, axis=1, keepdims=True)
                           + jnp.sum(jnp.where(masks[e], yoff_term - qv, 0.0), axis=1, keepdims=True))
                cols = jnp.where(lane == 2 + e, dac_col, cols)
                tail = (jnp.sum(jnp.where(masks[e], qv, 0.0)) + jnp.sum(jnp.where(masks[e], end_term, 0.0)))
                rows.append(jnp.where(lrow == CHUNK - 1, tail, 0.0) - jnp.sum(w, axis=0, keepdims=True))
            dx_ref[:, cs] = (dxdt * dtl + dyv * dsk_ref[:, cs]).astype(BF16)
            ddt_l = dxdt * x
            for e in range(2):
                cols = jnp.where(lane == e, jnp.sum(jnp.where(masks[e], ddt_l, 0.0), axis=1, keepdims=True), cols)
            dd_ref[p] = cols.T[0:8] + jnp.where(krow == 2, rows[0], 0.0) + jnp.where(krow == 3, rows[1], 0.0)
        dcb_b = dcb.astype(BF16)
        dc_ref[...] = dc_acc + _dot(dcb_b, bm)
        db_ref[...] = db_acc + _dot(dcb_b, cm, _TN)

    sp = _ssd_specs(nc, True)
    dx, db, dc, dd, dsk = pl.pallas_call(
        body, name=name, grid=(bl, nc, N_GROUPS),
        in_specs=[sp["wide"], sp["wide"], sp["bmat"], sp["cmat"], sp["rows2"], sp["rows2"], sp["dskip"], sp["state"]],
        out_specs=[sp["wide"], sp["narrow"], sp["narrow"], sp["rows8"],
                   pl.BlockSpec((N_PAIRS, 1, 128), lambda b, c, g: (0, 0, 0))],
        out_shape=[jax.ShapeDtypeStruct((t, SSM_W), BF16),
                   jax.ShapeDtypeStruct((t, N_GROUPS * N_STATE), F32),
                   jax.ShapeDtypeStruct((t, N_GROUPS * N_STATE), F32),
                   jax.ShapeDtypeStruct((bl * nc, N_PAIRS, 8, CHUNK), F32),
                   jax.ShapeDtypeStruct((N_PAIRS, 1, 128), F32)],
        scratch_shapes=[pltpu.VMEM((N_PAIRS, N_STATE, 128), F32)],
        compiler_params=_params("arbitrary", "arbitrary", "arbitrary"),
    )(dy, xa, xa, xa, dt4, ac4, dskip_l, prev)
    ddtT = dd[:, :, 0:2, :].reshape(bl * nc, N_HEADS, CHUNK)
    dacT = dd[:, :, 2:4, :].reshape(bl * nc, N_HEADS, CHUNK)
    return dx, db, dc, ddtT, dacT, dsk.reshape(N_PAIRS, 128)


def gnorm_fwd(y, proj, w, name):
    t = y.shape[0]
    tr = min(256, t)
    zb = OFF_Z // SSM_W

    def body(y_ref, z_ref, w_ref, o_ref):
        z = z_ref[...].astype(F32)
        yg = y_ref[...].astype(F32) * z * _sigmoid(z)
        r = lax.rsqrt(jnp.mean(yg * yg, axis=-1, keepdims=True) + EPS)
        o_ref[...] = (yg * r * w_ref[...]).astype(BF16)

    return pl.pallas_call(
        body, name=name, grid=(t // tr,),
        in_specs=[pl.BlockSpec((tr, SSM_W), lambda i: (i, 0)), pl.BlockSpec((tr, SSM_W), lambda i: (i, zb)),
                  pl.BlockSpec((1, SSM_W), lambda i: (0, 0))],
        out_specs=pl.BlockSpec((tr, SSM_W), lambda i: (i, 0)),
        out_shape=jax.ShapeDtypeStruct((t, SSM_W), BF16),
        compiler_params=_params("parallel"),
    )(y, proj, w.reshape(1, SSM_W))


def gnorm_bwd(ds, y, proj, w, name):
    t = y.shape[0]
    tr = min(256, t)
    zb = OFF_Z // SSM_W

    def body(ds_ref, y_ref, z_ref, w_ref, dy_ref, dz_ref, dw_ref):
        z = z_ref[...].astype(F32)
        yv = y_ref[...].astype(F32)
        sg = _sigmoid(z)
        sz = z * sg
        yg = yv * sz
        r = lax.rsqrt(jnp.mean(yg * yg, axis=-1, keepdims=True) + EPS)
        xh = yg * r
        g = ds_ref[...].astype(F32)
        dxh = g * w_ref[...]
        dyg = r * (dxh - xh * jnp.mean(dxh * xh, axis=-1, keepdims=True))
        dy_ref[...] = (dyg * sz).astype(BF16)
        dz_ref[...] = (dyg * yv * sg * (1.0 + z * (1.0 - sg))).astype(BF16)
        part = jnp.sum(g * xh, axis=0, keepdims=True)

        @pl.when(pl.program_id(0) == 0)
        def _():
            dw_ref[...] = part

        @pl.when(pl.program_id(0) > 0)
        def _():
            dw_ref[...] += part

    return pl.pallas_call(
        body, name=name, grid=(t // tr,),
        in_specs=[pl.BlockSpec((tr, SSM_W), lambda i: (i, 0)), pl.BlockSpec((tr, SSM_W), lambda i: (i, 0)),
                  pl.BlockSpec((tr, SSM_W), lambda i: (i, zb)), pl.BlockSpec((1, SSM_W), lambda i: (0, 0))],
        out_specs=[pl.BlockSpec((tr, SSM_W), lambda i: (i, 0)), pl.BlockSpec((tr, SSM_W), lambda i: (i, 0)),
                   pl.BlockSpec((1, SSM_W), lambda i: (0, 0))],
        out_shape=[jax.ShapeDtypeStruct((t, SSM_W), BF16), jax.ShapeDtypeStruct((t, SSM_W), BF16),
                   jax.ShapeDtypeStruct((1, SSM_W), F32)],
        compiler_params=_params("arbitrary"),
    )(ds, y, proj, w.reshape(1, SSM_W))


def _pool_mixed(u, g, row):
    win = 2 << g
    acc = u
    for k in range(g + 1):
        acc = acc + _shift_down(acc, 1 << k, row)
    inv = 1.0 / jnp.minimum(row + 1, win).astype(F32)
    return acc * inv - u, inv


def pool_fwd(proj, pool_w, pool_scale, bl, s, name):
    t = bl * s

    def body(u_ref, g_ref, w_ref, sc_ref, o_ref):
        row = lax.broadcasted_iota(jnp.int32, (s, POOL_GD), 0)
        for g in range(POOL_G):
            cs = slice(g * POOL_GD, (g + 1) * POOL_GD)
            u = u_ref[:, cs].astype(F32)
            mixed, _ = _pool_mixed(u, g, row)
            pm = _dot(mixed.astype(BF16), w_ref[g])
            gate = g_ref[:, cs].astype(F32)
            o_ref[:, cs] = (pm * sc_ref[:, cs] * gate * _sigmoid(gate)).astype(BF16)

    return pl.pallas_call(
        body, name=name, grid=(bl,),
        in_specs=[pl.BlockSpec((s, POOL_W), lambda b: (b, OFF_PU // POOL_W)),
                  pl.BlockSpec((s, POOL_W), lambda b: (b, OFF_PG // POOL_W)),
                  pl.BlockSpec((POOL_G, POOL_GD, POOL_GD), lambda b: (0, 0, 0)),
                  pl.BlockSpec((1, POOL_W), lambda b: (0, 0))],
        out_specs=pl.BlockSpec((s, POOL_W), lambda b: (b, 0)),
        out_shape=jax.ShapeDtypeStruct((t, POOL_W), BF16),
        compiler_params=_params("parallel"),
    )(proj, proj, pool_w, pool_scale.reshape(1, POOL_W))


def pool_bwd(dp, proj, pool_w, pool_scale, bl, s, name):
    t = bl * s

    def body(dp_ref, u_ref, g_ref, w_ref, sc_ref, du_ref, dg_ref, dw_ref, dsc_ref):
        row = lax.broadcasted_iota(jnp.int32, (s, POOL_GD), 0)
        first = pl.program_id(0) == 0
        for g in range(POOL_G):
            cs = slice(g * POOL_GD, (g + 1) * POOL_GD)
            u = u_ref[:, cs].astype(F32)
            mixed, inv = _pool_mixed(u, g, row)
            mixed_b = mixed.astype(BF16)
            wg = w_ref[g]
            pm = _dot(mixed_b, wg)
            gate = g_ref[:, cs].astype(F32)
            sg = _sigmoid(gate)
            d = dp_ref[:, cs].astype(F32)
            sc = sc_ref[:, cs]
            dpm = (d * sc * gate * sg).astype(BF16)
            dg_ref[:, cs] = (d * pm * sc * sg * (1.0 + gate * (1.0 - sg))).astype(BF16)
            dsc = jnp.sum(d * pm * gate * sg, axis=0, keepdims=True)
            dwg = _dot(mixed_b, dpm, _TN)
            dmixed = _dot(dpm, wg, _NT)
            acc = dmixed * inv
            for k in range(g + 1):
                acc = acc + _shift_up(acc, 1 << k, row, s)
            du_ref[:, cs] = (acc - dmixed).astype(BF16)

            @pl.when(first)
            def _():
                dw_ref[g] = dwg
                dsc_ref[:, cs] = dsc

            @pl.when(jnp.logical_not(first))
            def _():
                dw_ref[g] = dw_ref[g] + dwg
                dsc_ref[:, cs] = dsc_ref[:, cs] + dsc

    return pl.pallas_call(
        body, name=name, grid=(bl,),
        in_specs=[pl.BlockSpec((s, POOL_W), lambda b: (b, 0)),
                  pl.BlockSpec((s, POOL_W), lambda b: (b, OFF_PU // POOL_W)),
                  pl.BlockSpec((s, POOL_W), lambda b: (b, OFF_PG // POOL_W)),
                  pl.BlockSpec((POOL_G, POOL_GD, POOL_GD), lambda b: (0, 0, 0)),
                  pl.BlockSpec((1, POOL_W), lambda b: (0, 0))],
        out_specs=[pl.BlockSpec((s, POOL_W), lambda b: (b, 0)), pl.BlockSpec((s, POOL_W), lambda b: (b, 0)),
                   pl.BlockSpec((POOL_G, POOL_GD, POOL_GD), lambda b: (0, 0, 0)),
                   pl.BlockSpec((1, POOL_W), lambda b: (0, 0))],
        out_shape=[jax.ShapeDtypeStruct((t, POOL_W), BF16), jax.ShapeDtypeStruct((t, POOL_W), BF16),
                   jax.ShapeDtypeStruct((POOL_G, POOL_GD, POOL_GD), F32), jax.ShapeDtypeStruct((1, POOL_W), F32)],
        compiler_params=_params("arbitrary"),
    )(dp, proj, proj, pool_w, pool_scale.reshape(1, POOL_W))


SB_SCALE = 64 ** -0.5


KB = 256


def _sb_block(qe, kj, mask, rr, upper):
    z = _dot(qe, kj, _NT)
    lb = jnp.minimum(z, 0.0) - jnp.log(1.0 + jnp.exp(-jnp.abs(z)))
    lom = lb - z if mask is None else jnp.where(mask, lb - z, 0.0)
    later = _dot(lom.astype(BF16), upper) + rr
    return lb, lom, later


def _sb_masks(i):
    lane = lax.broadcasted_iota(jnp.int32, (QB, 128), 1)
    row = lax.broadcasted_iota(jnp.int32, (2 * QB, KB), 0) % QB
    col = lax.broadcasted_iota(jnp.int32, (2 * QB, KB), 1)
    causal = lambda jb: col + (jb * KB - i * QB) < row
    return lane, lane < 64, causal


def _stack_heads(x, left):
    zero = jnp.zeros_like(x)
    return jnp.concatenate([jnp.where(left, x, zero), jnp.where(left, zero, x)], axis=0)


SB_GROUP = 4
SB_GW = SB_GROUP * 128


def sb_fwd(proj, bl, s, name, comm=None):
    t = bl * s
    nq = s // QB
    qb0, kb0, vb0, gb0 = OFF_QKV // SB_GW, (OFF_QKV + SB_W) // SB_GW, (OFF_QKV + 2 * SB_W) // SB_GW, OFF_SBG // SB_GW
    grid = (bl, SB_PAIRS // SB_GROUP, nq)
    x_in, x_args, x_out, x_shape, x_scratch, x_start, x_wait = _hosted_exchange(comm, grid)

    def body(*refs):
        q_ref, k_ref, v_ref, g_ref = refs[:4]
        og_ref, o_ref, r_ref = refs[4 + len(x_in):7 + len(x_in)]
        x_refs = refs[4:4 + len(x_in)] + refs[7 + len(x_in):]
        x_start(x_refs)
        i = pl.program_id(2)
        lane, left, causal = _sb_masks(i)
        upper = _tri((KB, KB), lambda r, c: r > c).astype(BF16)
        cols = [slice(p * 128, (p + 1) * 128) for p in range(SB_GROUP)]
        qcats = [_stack_heads(q_ref[:, cs] * SB_SCALE, left) for cs in cols]
        zero = qcats[0].astype(F32) * 0.0

        def block(jb, carry, diagonal):
            rows = pl.ds(pl.multiple_of(jb * KB, KB), KB)
            mask = causal(jb) if diagonal else None
            out = []
            for p, cs in enumerate(cols):
                acc, rr, rt = carry[p]
                lb, lom, later = _sb_block(qcats[p], k_ref[rows, cs], mask, rr, upper)
                att = jnp.exp(lb + later)
                if diagonal:
                    att = jnp.where(mask, att, 0.0)
                acc = acc + _dot(att.astype(BF16), v_ref[rows, cs])
                rt = jnp.where(lane == jb, rr[:QB], jnp.where(lane == 8 + jb, rr[QB:], rt))
                out.append((acc, rr + jnp.sum(lom, axis=1, keepdims=True), rt))
            return tuple(out)

        carry = block(i, tuple((zero, zero[:, :1], zero[:QB]) for _ in cols), True)
        carry = lax.fori_loop(0, i, lambda jj, c: block(i - 1 - jj, c, False), carry)
        for p, cs in enumerate(cols):
            acc, _, rtile = carry[p]
            o = jnp.where(left, acc[:QB], acc[QB:])
            gate = g_ref[:, cs].astype(F32)
            o_ref[:, cs] = o.astype(BF16)
            og_ref[:, cs] = (o * gate * _sigmoid(gate)).astype(BF16)
            r_ref[p] = rtile
        x_wait(x_refs)

    rowblk = lambda b, g, i: (b * nq + i, g)
    return pl.pallas_call(
        body, name=name, grid=grid,
        in_specs=[pl.BlockSpec((QB, SB_GW), lambda b, g, i: (b * nq + i, qb0 + g)),
                  pl.BlockSpec((s, SB_GW), lambda b, g, i: (b, kb0 + g)),
                  pl.BlockSpec((s, SB_GW), lambda b, g, i: (b, vb0 + g)),
                  pl.BlockSpec((QB, SB_GW), lambda b, g, i: (b * nq + i, gb0 + g))] + x_in,
        out_specs=[pl.BlockSpec((QB, SB_GW), rowblk), pl.BlockSpec((QB, SB_GW), rowblk),
                   pl.BlockSpec((None, SB_GROUP, QB, 128), lambda b, g, i: (b * nq + i, g, 0, 0))] + x_out,
        out_shape=[jax.ShapeDtypeStruct((t, SB_W), BF16), jax.ShapeDtypeStruct((t, SB_W), BF16),
                   jax.ShapeDtypeStruct((bl * nq, SB_PAIRS, QB, 128), F32)] + x_shape,
        scratch_shapes=x_scratch,
        compiler_params=_params("arbitrary", "arbitrary", "arbitrary"),
    )(proj, proj, proj, proj, *x_args)


def sb_bwd(dsb, o, rsave, proj, bl, s, name, comm=None):
    t = bl * s
    nq = s // QB
    qb0, kb0, vb0, gb0 = OFF_QKV // SB_GW, (OFF_QKV + SB_W) // SB_GW, (OFF_QKV + 2 * SB_W) // SB_GW, OFF_SBG // SB_GW
    grid = (bl, SB_PAIRS // SB_GROUP, nq)
    x_in, x_args, x_out, x_shape, x_scratch, x_start, x_wait = _hosted_exchange(comm, grid)

    def body(*refs):
        n = len(x_in)
        d_ref, o_ref, r_ref, q_ref, k_ref, v_ref, g_ref = refs[:7]
        dq_ref, dk_ref, dv_ref, dg_ref = refs[7 + n:11 + n]
        dk_acc, dv_acc = refs[11 + 2 * n:13 + 2 * n]
        x_refs = refs[7:7 + n] + refs[11 + n:11 + 2 * n] + refs[13 + 2 * n:]
        x_start(x_refs)
        i = pl.program_id(2)

        @pl.when(i == 0)
        def _():
            dk_acc[...] = jnp.zeros(dk_acc.shape, F32)
            dv_acc[...] = jnp.zeros(dv_acc.shape, F32)

        lane, left, causal = _sb_masks(i)
        upper = _tri((KB, KB), lambda r, c: r > c).astype(BF16)
        lower = _tri((KB, KB), lambda r, c: r < c).astype(BF16)
        cols = [slice(p * 128, (p + 1) * 128) for p in range(SB_GROUP)]
        qcats, docats = [], []
        for cs in cols:
            gate = g_ref[:, cs].astype(F32)
            sg = _sigmoid(gate)
            d = d_ref[:, cs].astype(F32)
            dg_ref[:, cs] = (d * o_ref[:, cs].astype(F32) * sg * (1.0 + gate * (1.0 - sg))).astype(BF16)
            docats.append(_stack_heads((d * gate * sg).astype(BF16), left))
            qcats.append(_stack_heads(q_ref[:, cs] * SB_SCALE, left))
        qcat_ts = [qc.astype(F32).T.astype(BF16) for qc in qcats]
        docat_ts = [dc.astype(F32).T.astype(BF16) for dc in docats]
        zero = qcats[0].astype(F32) * 0.0

        def block(jb, carry, diagonal):
            rows = pl.ds(pl.multiple_of(jb * KB, KB), KB)
            mask = causal(jb) if diagonal else None
            out = []
            for p, cs in enumerate(cols):
                dq, gcar = carry[p]
                kj = k_ref[rows, cs]
                vj = v_ref[rows, cs]
                rtile = r_ref[p]
                rr = jnp.concatenate(
                    [jnp.sum(jnp.where(lane == jb, rtile, 0.0), axis=1, keepdims=True),
                     jnp.sum(jnp.where(lane == 8 + jb, rtile, 0.0), axis=1, keepdims=True)], axis=0)
                lb, lom, later = _sb_block(qcats[p], kj, mask, rr, upper)
                att = jnp.exp(lb + later)
                if diagonal:
                    att = jnp.where(mask, att, 0.0)
                de = att * _dot(docats[p], vj, _NT)
                gpre = _dot(de.astype(BF16), lower) + gcar
                sig = jnp.exp(lb)
                dz = de * (1.0 - sig) - gpre * sig
                if diagonal:
                    dz = jnp.where(mask, dz, 0.0)
                dz = dz.astype(BF16)
                dk_acc[jb, cs, :] = dk_acc[jb, cs, :] + _dot(qcat_ts[p], dz)
                dv_acc[jb, cs, :] = dv_acc[jb, cs, :] + _dot(docat_ts[p], att.astype(BF16))
                out.append((dq + _dot(dz, kj), gcar + jnp.sum(de, axis=1, keepdims=True)))
            return tuple(out)

        carry = lax.fori_loop(0, i, lambda jb, c: block(jb, c, False), tuple((zero, zero[:, :1]) for _ in cols))
        carry = block(i, carry, True)
        for p, cs in enumerate(cols):
            dq = carry[p][0]
            dq_ref[:, cs] = (jnp.where(left, dq[:QB], dq[QB:]) * SB_SCALE).astype(BF16)

        @pl.when(i == nq - 1)
        def _():
            for kb in range(s // KB):
                for cs in cols:
                    dk_ref[kb * KB:(kb + 1) * KB, cs] = dk_acc[kb, cs, :].T.astype(BF16)
                    dv_ref[kb * KB:(kb + 1) * KB, cs] = dv_acc[kb, cs, :].T.astype(BF16)

        x_wait(x_refs)

    rowblk = lambda b, g, i: (b * nq + i, g)
    seqblk = lambda b, g, i: (b, g)
    return pl.pallas_call(
        body, name=name, grid=grid,
        in_specs=[pl.BlockSpec((QB, SB_GW), rowblk), pl.BlockSpec((QB, SB_GW), rowblk),
                  pl.BlockSpec((None, SB_GROUP, QB, 128), lambda b, g, i: (b * nq + i, g, 0, 0)),
                  pl.BlockSpec((QB, SB_GW), lambda b, g, i: (b * nq + i, qb0 + g)),
                  pl.BlockSpec((s, SB_GW), lambda b, g, i: (b, kb0 + g)),
                  pl.BlockSpec((s, SB_GW), lambda b, g, i: (b, vb0 + g)),
                  pl.BlockSpec((QB, SB_GW), lambda b, g, i: (b * nq + i, gb0 + g))] + x_in,
        out_specs=[pl.BlockSpec((QB, SB_GW), rowblk), pl.BlockSpec((s, SB_GW), seqblk),
                   pl.BlockSpec((s, SB_GW), seqblk), pl.BlockSpec((QB, SB_GW), rowblk)] + x_out,
        out_shape=[jax.ShapeDtypeStruct((t, SB_W), BF16)] * 4 + x_shape,
        scratch_shapes=[pltpu.VMEM((s // KB, SB_GW, KB), F32), pltpu.VMEM((s // KB, SB_GW, KB), F32)] + x_scratch,
        compiler_params=_params("arbitrary", "arbitrary", "arbitrary"),
    )(dsb, o, rsave, proj, proj, proj, proj, *x_args)


def merge_fwd(proj, ys, yp, yb, name):
    t = ys.shape[0]
    tr = min(512, t)

    def body(m_ref, ys_ref, yp_ref, yb_ref, o_ref):
        acc = jnp.zeros((tr, D), F32)
        for k, ref in enumerate((ys_ref, yp_ref, yb_ref)):
            acc = acc + _sigmoid(m_ref[:, k * D:(k + 1) * D].astype(F32)) * ref[...].astype(F32)
        o_ref[...] = acc.astype(BF16)

    rowblk = pl.BlockSpec((tr, D), lambda i: (i, 0))
    return pl.pallas_call(
        body, name=name, grid=(t // tr,),
        in_specs=[pl.BlockSpec((tr, 3 * D), lambda i: (i, 0)), rowblk, rowblk, rowblk],
        out_specs=rowblk,
        out_shape=jax.ShapeDtypeStruct((t, D), BF16),
        compiler_params=_params("parallel"),
    )(proj, ys, yp, yb)


def merge_bwd(dm, proj, ys, yp, yb, name):
    t = ys.shape[0]
    tr = min(512, t)

    def body(dm_ref, m_ref, ys_ref, yp_ref, yb_ref, d0_ref, d1_ref, d2_ref, dl_ref):
        dmv = dm_ref[...].astype(F32)
        for k, (ref, dref) in enumerate(((ys_ref, d0_ref), (yp_ref, d1_ref), (yb_ref, d2_ref))):
            g = _sigmoid(m_ref[:, k * D:(k + 1) * D].astype(F32))
            dref[...] = (g * dmv).astype(BF16)
            dl_ref[:, k * D:(k + 1) * D] = (dmv * ref[...].astype(F32) * g * (1.0 - g)).astype(BF16)

    rowblk = pl.BlockSpec((tr, D), lambda i: (i, 0))
    wide = pl.BlockSpec((tr, 3 * D), lambda i: (i, 0))
    return pl.pallas_call(
        body, name=name, grid=(t // tr,),
        in_specs=[rowblk, wide, rowblk, rowblk, rowblk],
        out_specs=[rowblk, rowblk, rowblk, wide],
        out_shape=[jax.ShapeDtypeStruct((t, D), BF16)] * 3 + [jax.ShapeDtypeStruct((t, 3 * D), BF16)],
        compiler_params=_params("parallel"),
    )(dm, proj, ys, yp, yb)


def layer_fwd(x, lw, bl, s, tag, comm=None):
    t = bl * s
    h = rmsnorm_fwd(x, lw["norm_w"], f"norm_fwd{tag}")
    proj = matmul(h, lw["w_in"], "nt", BF16, f"in_proj{tag}", tn=2048)
    xa = conv_fwd(proj, lw["conv_w"], lw["conv_b"], bl, s, f"conv_fwd{tag}")
    dtT, acT = dt_fwd(proj, lw["dt_bias"], lw["a_log"], t, f"dt_fwd{tag}")
    dskip_l = jnp.repeat(lw["d_skip"], 64).reshape(1, SSM_W)
    y, prev = ssd_fwd(xa, dtT, acT, dskip_l, bl, s, f"ssd_fwd{tag}")
    s_out = gnorm_fwd(y, proj, lw["ssm_norm_w"], f"gnorm_fwd{tag}")
    p_out = pool_fwd(proj, lw["pool_w"], lw["pool_scale"], bl, s, f"pool_fwd{tag}")
    sb_out, sb_o, sb_r, *carried = sb_fwd(proj, bl, s, f"sb_fwd{tag}", comm)
    ys = matmul(s_out, lw["w_proj_ssm"], "nn", BF16, f"proj_ssm{tag}")
    yp = matmul(p_out, lw["w_proj_pool"], "nn", BF16, f"proj_pool{tag}")
    yb = matmul(sb_out, lw["w_proj_sb"], "nn", BF16, f"proj_sb{tag}")
    merged = merge_fwd(proj, ys, yp, yb, f"merge_fwd{tag}")
    x_next = matmul(merged, lw["w_out"], "nn", F32, f"out_proj{tag}", residual=x)
    saved = dict(x=x, h=h, proj=proj, xa=xa, dtT=dtT, acT=acT, y=y, prev=prev, s_out=s_out, p_out=p_out,
                 sb_out=sb_out, sb_o=sb_o, sb_r=sb_r, ys=ys, yp=yp, yb=yb, merged=merged)
    return x_next, saved, (carried[0] if carried else None)


def layer_bwd(dx, dx_b, lw, sv, bl, s, tag, comm=None):
    t = bl * s
    g = {}
    dmerged = matmul(dx_b, lw["w_out"], "nt", BF16, f"d_merged{tag}")
    g["w_out"] = matmul(sv["merged"], dx_b, "tn", F32, f"dw_out{tag}")
    dys, dyp, dyb, dlogit = merge_bwd(dmerged, sv["proj"], sv["ys"], sv["yp"], sv["yb"], f"merge_bwd{tag}")
    ds_out = matmul(dys, lw["w_proj_ssm"], "nt", BF16, f"d_sout{tag}")
    g["w_proj_ssm"] = matmul(sv["s_out"], dys, "tn", F32, f"dw_proj_ssm{tag}")
    dp_out = matmul(dyp, lw["w_proj_pool"], "nt", BF16, f"d_pout{tag}")
    g["w_proj_pool"] = matmul(sv["p_out"], dyp, "tn", F32, f"dw_proj_pool{tag}")
    dsb_out = matmul(dyb, lw["w_proj_sb"], "nt", BF16, f"d_sbout{tag}")
    g["w_proj_sb"] = matmul(sv["sb_out"], dyb, "tn", F32, f"dw_proj_sb{tag}")
    dy, dz, dnw = gnorm_bwd(ds_out, sv["y"], sv["proj"], lw["ssm_norm_w"], f"gnorm_bwd{tag}")
    g["ssm_norm_w"] = dnw[0]
    dskip_l = jnp.repeat(lw["d_skip"], 64).reshape(1, SSM_W)
    dxs, db, dc, ddtT, dacT, dsk = ssd_bwd(dy, sv["xa"], sv["dtT"], sv["acT"], dskip_l, sv["prev"], bl, s, f"ssd_bwd{tag}")
    g["d_skip"] = jnp.sum(dsk.reshape(N_HEADS, 64), axis=1)
    ddt_raw, da, dbias = dt_bwd(ddtT, dacT, sv["dtT"], sv["proj"], lw["dt_bias"], lw["a_log"], t, f"dt_bwd{tag}")
    g["a_log"] = da * (-jnp.exp(lw["a_log"]))
    g["dt_bias"] = dbias
    dxa = jnp.concatenate([dxs, db.astype(BF16), dc.astype(BF16)], axis=1)
    dxbc, dcw, dcb = conv_bwd(dxa, sv["proj"], lw["conv_w"], lw["conv_b"], bl, s, f"conv_bwd{tag}")
    g["conv_w"] = dcw
    g["conv_b"] = dcb
    dpu, dpg, dpw, dpsc = pool_bwd(dp_out, sv["proj"], lw["pool_w"], lw["pool_scale"], bl, s, f"pool_bwd{tag}")
    g["pool_w"] = dpw
    g["pool_scale"] = dpsc[0]
    dq, dk, dv, dsbg, *carried = sb_bwd(dsb_out, sv["sb_o"], sv["sb_r"], sv["proj"], bl, s, f"sb_bwd{tag}", comm)
    dproj = concat_columns([dlogit, dsbg, dpu, dpg, dz, dq, dk, dv, dxbc, ddt_raw], PC, f"d_proj{tag}")
    g["w_in"] = matmul(dproj, sv["h"], "tn", F32, f"dw_in{tag}")
    dh = matmul(dproj, lw["w_in"], "nn", F32, f"d_h{tag}")
    dx_in, dx_in_b, dnorm = rmsnorm_bwd(dh, sv["x"], lw["norm_w"], dx, f"norm_bwd{tag}")
    g["norm_w"] = dnorm[0]
    return dx_in, dx_in_b, g, (carried[0] if carried else None)


def concat_columns(parts, width, name):
    t = parts[0].shape[0]
    tr = min(256, t)
    widths = [p.shape[1] for p in parts]
    used = sum(widths)

    def body(*refs):
        o_ref = refs[-1]
        off = 0
        for ref, w in zip(refs[:-1], widths):
            o_ref[:, off:off + w] = ref[...]
            off += w
        if width > used:
            o_ref[:, used:] = jnp.zeros((tr, width - used), BF16)

    return pl.pallas_call(
        body, name=name, grid=(t // tr,),
        in_specs=[pl.BlockSpec((tr, w), lambda i: (i, 0)) for w in widths],
        out_specs=pl.BlockSpec((tr, width), lambda i: (i, 0)),
        out_shape=jax.ShapeDtypeStruct((t, width), BF16),
        compiler_params=_params("parallel"),
    )(*parts)


_PAD_PIECES = ((10784, 3072), (9760, 1024), (4640, 1024), (5664, 1024), (0, 2048), (6688, 3072), (2048, 2560), (4608, 32))
_UNPAD_PIECES = ((OFF_Z, 2048), (OFF_XBC, 2560), (OFF_DT, 32), (OFF_PU, 1024), (OFF_PG, 1024), (OFF_QKV, 3072),
                 (OFF_SBG, 1024), (OFF_MERGE, 3072))


def pad_rows(wt):
    pieces = [wt[o:o + n] for o, n in _PAD_PIECES]
    return jnp.concatenate(pieces + [jnp.zeros((PC - IN_COLS, wt.shape[1]), wt.dtype)], axis=0)


def unpad_rows(wp):
    return jnp.concatenate([wp[o:o + n] for o, n in _UNPAD_PIECES], axis=0)


MESH = pl.DeviceIdType.MESH
ANY = pl.BlockSpec(memory_space=pl.ANY)


def _coords():
    return lax.axis_index("x"), lax.axis_index("y"), lax.axis_index("c")


def _peer(p):
    x, y, c = _coords()
    return (1 - x if p & 4 else x, 1 - y if p & 2 else y, 1 - c if p & 1 else c)


def _flat(pos):
    return 4 * pos[0] + 2 * pos[1] + pos[2]


def _exchange_copies(v_ref, out_ref, send_sems, recv_sems, local_sem, gather):
    me = _flat(_coords())
    src_of = (lambda k: v_ref) if gather else (lambda k: v_ref.at[k])

    def copy(p, landing):
        peer = _peer(p)
        return pltpu.make_async_remote_copy(
            src_ref=src_of(_flat(peer)), dst_ref=out_ref.at[landing], send_sem=send_sems.at[p - 1],
            recv_sem=recv_sems.at[p - 1], device_id=peer, device_id_type=MESH)

    local = pltpu.make_async_copy(src_of(me), out_ref.at[me], local_sem)
    sends = [copy(p, me) for p in range(1, N_DEV)]
    arrivals = [copy(p, _flat(_peer(p))) for p in range(1, N_DEV)]
    return local, sends, arrivals


def _exchange_start(*refs_and_mode):
    local, sends, _ = _exchange_copies(*refs_and_mode)
    local.start()
    for cp in sends:
        cp.start()


def _exchange_wait(*refs_and_mode):
    local, sends, arrivals = _exchange_copies(*refs_and_mode)
    for cp in arrivals:
        cp.wait_recv()
    for cp in sends:
        cp.wait_send()
    local.wait()


def _exchange_shape(v, gather):
    return jax.ShapeDtypeStruct((N_DEV,) + tuple(v.shape) if gather else tuple(v.shape), v.dtype)


def _exchange_sems():
    return [pltpu.SemaphoreType.DMA((N_DEV - 1,)), pltpu.SemaphoreType.DMA((N_DEV - 1,)), pltpu.SemaphoreType.DMA]


def exchange(v, gather, name):
    def body(v_ref, out_ref, send_sems, recv_sems, local_sem):
        _exchange_start(v_ref, out_ref, send_sems, recv_sems, local_sem, gather)
        _exchange_wait(v_ref, out_ref, send_sems, recv_sems, local_sem, gather)

    return pl.pallas_call(
        body, name=name,
        in_specs=[ANY], out_specs=ANY,
        out_shape=_exchange_shape(v, gather),
        scratch_shapes=_exchange_sems(),
    )(v)


def _hosted_exchange(comm, grid):
    if comm is None:
        return [], [], [], [], [], (lambda refs: None), (lambda refs: None)
    v, gather = comm

    def at(first):
        cond = None
        for axis, n in enumerate(grid):
            term = pl.program_id(axis) == (0 if first else n - 1)
            cond = term if cond is None else jnp.logical_and(cond, term)
        return cond

    def start(refs):
        @pl.when(at(True))
        def _():
            _exchange_start(*refs, gather)

    def wait(refs):
        @pl.when(at(False))
        def _():
            _exchange_wait(*refs, gather)

    return [ANY], [v], [ANY], [_exchange_shape(v, gather)], _exchange_sems(), start, wait


def sum_slabs(v, name):
    _, r, c = v.shape
    tr = 128 if r % 128 == 0 else r

    def body(v_ref, o_ref):
        acc = v_ref[0].astype(F32)
        for k in range(1, N_DEV):
            acc = acc + v_ref[k].astype(F32)
        o_ref[...] = acc

    return pl.pallas_call(
        body, name=name, grid=(r // tr,),
        in_specs=[pl.BlockSpec((N_DEV, tr, c), lambda i: (0, i, 0))],
        out_specs=pl.BlockSpec((tr, c), lambda i: (i, 0)),
        out_shape=jax.ShapeDtypeStruct((r, c), F32),
        compiler_params=_params("parallel"),
    )(v)


def adamw(w, g, m, v, name):
    r, c = w.shape
    tr = next((cand for cand in (256, 128, 64, 32, 16, 8) if r % cand == 0), r)

    def body(w_ref, g_ref, m_ref, v_ref, d_ref, mo_ref, vo_ref):
        gv = g_ref[...]
        mn = ADAM_B1 * m_ref[...] + (1.0 - ADAM_B1) * gv
        vn = ADAM_B2 * v_ref[...] + (1.0 - ADAM_B2) * (gv * gv)
        m_hat = mn / (1.0 - ADAM_B1 ** ADAM_STEP)
        v_hat = vn / (1.0 - ADAM_B2 ** ADAM_STEP)
        d_ref[...] = -ADAM_LR * (m_hat / (jnp.sqrt(v_hat) + ADAM_EPS) + ADAM_WD * w_ref[...])
        mo_ref[...] = mn
        vo_ref[...] = vn

    blk = pl.BlockSpec((tr, c), lambda i: (i, 0))
    return pl.pallas_call(
        body, name=name, grid=(r // tr,),
        in_specs=[blk] * 4, out_specs=[blk] * 3,
        out_shape=[jax.ShapeDtypeStruct((r, c), F32)] * 3,
        compiler_params=_params("parallel"),
    )(w, g, m, v)


BIG = ("w_proj_ssm", "w_proj_pool", "w_proj_sb", "w_out", "pool_w", "w_in")
SHARD_IN = IN_COLS // N_DEV
BIG_ROWS = {"w_proj_ssm": SSM_W // N_DEV, "w_proj_pool": POOL_W // N_DEV, "w_proj_sb": SB_W // N_DEV,
            "w_out": D // N_DEV, "pool_w": POOL_G * (POOL_GD // N_DEV) * POOL_GD // D, "w_in": SHARD_IN}
PACK_C = D
PACK_R = 2432

REPLICATED = ("norm_w", "conv_b", "dt_bias", "a_log", "d_skip", "ssm_norm_w", "pool_scale")
WEIGHTS = ("norm_w", "w_in", "conv_w", "conv_b", "dt_bias", "a_log", "d_skip", "ssm_norm_w", "pool_w",
           "pool_scale", "w_proj_ssm", "w_proj_pool", "w_proj_sb", "w_out", "final_norm_w")


def _size(shape):
    n = 1
    for d in shape:
        n *= d
    return n


def _pad_flat(flat, n):
    return jnp.concatenate([flat, jnp.zeros((n - flat.shape[0],), flat.dtype)])


def _row_offsets():
    offs, off = {}, 0
    for n in BIG:
        offs[n] = off
        off += BIG_ROWS[n]
    return offs, off


def pack_shards(parts):
    rows = [parts[n].reshape(BIG_ROWS[n], PACK_C) for n in BIG]
    rows[-1] = jnp.pad(rows[-1], ((0, PACK_R - _row_offsets()[1]), (0, 0)))
    return jnp.concatenate(rows, axis=0)


def unpack_shards(packed):
    offs, _ = _row_offsets()
    out = {}
    for n in BIG:
        seg = packed[offs[n]:offs[n] + BIG_ROWS[n]]
        if n == "w_in":
            out[n] = seg.T
        elif n == "pool_w":
            out[n] = seg.reshape(POOL_G, POOL_GD // N_DEV, POOL_GD)
        else:
            out[n] = seg
    return out


def unpack_gathered(g):
    offs, _ = _row_offsets()
    out = {}
    for n in BIG:
        seg = g[:, offs[n]:offs[n] + BIG_ROWS[n], :]
        if n == "w_in":
            out[n] = pad_rows(seg.reshape(IN_COLS, D))
        elif n == "pool_w":
            out[n] = seg.reshape(N_DEV, POOL_G, POOL_GD // N_DEV, POOL_GD).transpose(1, 0, 2, 3).reshape(
                POOL_G, POOL_GD, POOL_GD)
        else:
            out[n] = seg.reshape(N_DEV * BIG_ROWS[n], D)
    return out


def pack_slabs(g):
    segs = []
    for n in BIG:
        if n == "w_in":
            w = unpad_rows(g[n])
        elif n == "pool_w":
            w = g[n].reshape(POOL_G, N_DEV, POOL_GD // N_DEV, POOL_GD).transpose(1, 0, 2, 3)
        else:
            w = g[n]
        segs.append(w.reshape(N_DEV, BIG_ROWS[n], PACK_C).astype(BF16))
    segs[-1] = jnp.pad(segs[-1], ((0, 0), (0, PACK_R - _row_offsets()[1]), (0, 0)))
    return jnp.concatenate(segs, axis=1)


SMALL_ROWS = 544


def pack_small(vals):
    flat = jnp.concatenate([v.reshape(-1) for v in vals])
    return _pad_flat(flat, SMALL_ROWS * 128).reshape(SMALL_ROWS, 128)


def unpack_small(packed, shapes):
    flat = packed.reshape(-1)
    out, off = [], 0
    for shp in shapes:
        out.append(flat[off:off + _size(shp)].reshape(shp))
        off += _size(shp)
    return out


def kernel(x, norm_w, w_in, conv_w, conv_b, dt_bias, a_log, d_skip, ssm_norm_w, pool_w, pool_scale, w_proj_ssm, w_proj_pool, w_proj_sb, w_out, final_norm_w, loss_target, m_norm_w, m_w_in, m_conv_w, m_conv_b, m_dt_bias, m_a_log, m_d_skip, m_ssm_norm_w, m_pool_w, m_pool_scale, m_w_proj_ssm, m_w_proj_pool, m_w_proj_sb, m_w_out, m_final_norm_w, v_norm_w, v_w_in, v_conv_w, v_conv_b, v_dt_bias, v_a_log, v_d_skip, v_ssm_norm_w, v_pool_w, v_pool_scale, v_w_proj_ssm, v_w_proj_pool, v_w_proj_sb, v_w_out, v_final_norm_w):
    wts = dict(norm_w=norm_w, w_in=w_in, conv_w=conv_w, conv_b=conv_b, dt_bias=dt_bias, a_log=a_log, d_skip=d_skip,
               ssm_norm_w=ssm_norm_w, pool_w=pool_w, pool_scale=pool_scale, w_proj_ssm=w_proj_ssm,
               w_proj_pool=w_proj_pool, w_proj_sb=w_proj_sb, w_out=w_out, final_norm_w=final_norm_w)
    mom = dict(norm_w=m_norm_w, w_in=m_w_in, conv_w=m_conv_w, conv_b=m_conv_b, dt_bias=m_dt_bias, a_log=m_a_log,
               d_skip=m_d_skip, ssm_norm_w=m_ssm_norm_w, pool_w=m_pool_w, pool_scale=m_pool_scale,
               w_proj_ssm=m_w_proj_ssm, w_proj_pool=m_w_proj_pool, w_proj_sb=m_w_proj_sb, w_out=m_w_out,
               final_norm_w=m_final_norm_w)
    var = dict(norm_w=v_norm_w, w_in=v_w_in, conv_w=v_conv_w, conv_b=v_conv_b, dt_bias=v_dt_bias, a_log=v_a_log,
               d_skip=v_d_skip, ssm_norm_w=v_ssm_norm_w, pool_w=v_pool_w, pool_scale=v_pool_scale,
               w_proj_ssm=v_w_proj_ssm, w_proj_pool=v_w_proj_pool, w_proj_sb=v_w_proj_sb, w_out=v_w_out,
               final_norm_w=v_final_norm_w)
    bl, s, _ = x.shape
    t = bl * s
    me = _flat(_coords())

    cw = exchange(conv_w.reshape(40, 128), True, "gather_conv_w")
    conv_w_full = cw.reshape(N_DEV, DEPTH, CONV_K, CONV_CH // N_DEV).transpose(1, 2, 0, 3).reshape(
        DEPTH, CONV_K, CONV_CH)

    xc = x.reshape(t, D)
    layer_w, saved = [], []
    packed = [pack_shards({n: (wts[n][l].T if n == "w_in" else wts[n][l]).astype(BF16) for n in BIG})
              for l in range(DEPTH)]
    gathered = exchange(packed[0], True, "gather_w0")
    for l in range(DEPTH):
        lw = unpack_gathered(gathered)
        for n in REPLICATED:
            lw[n] = wts[n][l]
        lw["conv_w"] = conv_w_full[l]
        xc, sv, gathered = layer_fwd(xc, lw, bl, s, f"_l{l}", (packed[l + 1], True) if l + 1 < DEPTH else None)
        layer_w.append(lw)
        saved.append(sv)

    loss_part, dx, dx_b, dfinal = final_loss(xc, final_norm_w, loss_target.reshape(t, D), "final_loss")
    loss = lax.psum(loss_part[0, 0], ("x", "y", "c"))

    grads = [None] * DEPTH
    big_sum = [None] * DEPTH
    slabs = None
    for l in reversed(range(DEPTH)):
        dx, dx_b, g, got = layer_bwd(dx, dx_b, layer_w[l], saved[l], bl, s, f"_l{l}",
                                     (slabs, False) if slabs is not None else None)
        if got is not None:
            big_sum[l + 1] = unpack_shards(sum_slabs(got, f"sum_g{l + 1}"))
        grads[l] = g
        slabs = pack_slabs(g)
    big_sum[0] = unpack_shards(sum_slabs(exchange(slabs, False, "scatter_g0"), "sum_g0"))
    grad_x = dx.reshape(bl, s, D)

    small_names = REPLICATED + ("conv_w",)
    small_vals = [jnp.stack([grads[l][n] for l in range(DEPTH)]) for n in small_names] + [dfinal[0]]
    small_shapes = [v.shape for v in small_vals]
    small_all = exchange(pack_small(small_vals), True, "gather_small")
    small_sum = unpack_small(sum_slabs(small_all, "sum_small"), small_shapes)
    gsum = dict(zip(small_names + ("final_norm_w",), small_sum))
    conv_g_full = gsum["conv_w"]
    gsum["conv_w"] = lax.dynamic_slice_in_dim(conv_g_full, me * (CONV_CH // N_DEV), CONV_CH // N_DEV, axis=2)
    for n in BIG:
        gsum[n] = jnp.stack([big_sum[l][n] for l in range(DEPTH)])

    delta, new_m, new_v = {}, {}, {}
    for n in BIG + ("conv_w",):
        shp = wts[n].shape
        two_d = (-1, shp[-1])
        d2, m2, v2 = adamw(wts[n].reshape(two_d), gsum[n].reshape(two_d), mom[n].reshape(two_d),
                           var[n].reshape(two_d), f"adamw_{n}")
        delta[n], new_m[n], new_v[n] = d2.reshape(shp), m2.reshape(shp), v2.reshape(shp)
    rep = REPLICATED + ("final_norm_w",)
    rep_shapes = [wts[n].shape for n in rep]
    d2, m2, v2 = adamw(pack_small([wts[n] for n in rep]), pack_small([gsum[n] for n in rep]),
                       pack_small([mom[n] for n in rep]), pack_small([var[n] for n in rep]), "adamw_small")
    for n, dv, mv, vv in zip(rep, unpack_small(d2, rep_shapes), unpack_small(m2, rep_shapes),
                             unpack_small(v2, rep_shapes)):
        delta[n], new_m[n], new_v[n] = dv, mv, vv

    return (loss, grad_x, *[gsum[n] for n in WEIGHTS], *[delta[n] for n in WEIGHTS],
            *[new_m[n] for n in WEIGHTS], *[new_v[n] for n in WEIGHTS])
```

```python
import functools

import jax
import jax.numpy as jnp
from jax import lax
from jax.experimental import pallas as pl
from jax.experimental.pallas import tpu as pltpu

F32 = jnp.float32
BF16 = jnp.bfloat16

N_DEV = 8
DEPTH = 4
D = 1024
SSM_W = 2048
N_HEADS = 32
N_PAIRS = 16
N_GROUPS = 2
N_STATE = 128
CHUNK = 128
CONV_CH = 2560
CONV_K = 4
POOL_W = 1024
POOL_G = 4
POOL_GD = 256
SB_W = 1024
SB_PAIRS = 8
QB = 256
EPS = 1e-6
IN_COLS = 13856

PC = 14336
OFF_MERGE = 0
OFF_SBG = 3072
OFF_PU = 4096
OFF_PG = 5120
OFF_Z = 6144
OFF_QKV = 8192
OFF_XBC = 11264
OFF_DT = 13824

ADAM_LR = 0.001
ADAM_B1 = 0.9
ADAM_B2 = 0.999
ADAM_EPS = 1e-08
ADAM_WD = 0.01
ADAM_STEP = 10

VMEM_LIMIT = 56 * 1024 * 1024

_NN = (((1,), (0,)), ((), ()))
_NT = (((1,), (1,)), ((), ()))
_TN = (((0,), (0,)), ((), ()))


def _dot(a, b, dn=_NN):
    return lax.dot_general(a, b, dn, preferred_element_type=F32)


def _sigmoid(x):
    return 1.0 / (1.0 + jnp.exp(-x))


def _softplus(x):
    return jnp.maximum(x, 0.0) + jnp.log(1.0 + jnp.exp(-jnp.abs(x)))


def _split2(x):
    hi = x.astype(BF16)
    lo = (x - hi.astype(F32)).astype(BF16)
    return hi, lo


def _split3(x):
    hi = x.astype(BF16)
    r = x - hi.astype(F32)
    mid = r.astype(BF16)
    lo = (r - mid.astype(F32)).astype(BF16)
    return hi, mid, lo


def _params(*sem):
    return pltpu.CompilerParams(dimension_semantics=sem, vmem_limit_bytes=VMEM_LIMIT)


def matmul(a, b, mode, out_dtype, name, residual=None, tm=1024, tn=1024, tk=1024):
    if mode == "nn":
        (m, k), (k2, n) = a.shape, b.shape
    elif mode == "nt":
        (m, k), (n, k2) = a.shape, b.shape
    else:
        (k, m), (k2, n) = a.shape, b.shape
    assert k == k2
    tm, tn, tk = min(tm, m), min(tn, n), min(tk, k)
    assert m % tm == 0 and n % tn == 0 and k % tk == 0
    nk = k // tk
    dn = {"nn": _NN, "nt": _NT, "tn": _TN}[mode]
    a_spec = pl.BlockSpec((tk, tm), lambda i, j, kk: (kk, i)) if mode == "tn" else pl.BlockSpec((tm, tk), lambda i, j, kk: (i, kk))
    b_spec = pl.BlockSpec((tn, tk), lambda i, j, kk: (j, kk)) if mode == "nt" else pl.BlockSpec((tk, tn), lambda i, j, kk: (kk, j))
    in_specs = [a_spec, b_spec]
    args = [a, b]
    if residual is not None:
        in_specs.append(pl.BlockSpec((tm, tn), lambda i, j, kk: (i, j)))
        args.append(residual)

    def body(*refs):
        if residual is not None:
            a_ref, b_ref, r_ref, o_ref, acc_ref = refs
        else:
            a_ref, b_ref, o_ref, acc_ref = refs
            r_ref = None
        kk = pl.program_id(2)
        p = _dot(a_ref[...], b_ref[...], dn)

        def finish(val):
            if r_ref is not None:
                val = val + r_ref[...]
            o_ref[...] = val.astype(out_dtype)

        if nk == 1:
            finish(p)
        else:
            @pl.when(kk == 0)
            def _():
                acc_ref[...] = p

            @pl.when(kk > 0)
            def _():
                acc_ref[...] += p

            @pl.when(kk == nk - 1)
            def _():
                finish(acc_ref[...])

    return pl.pallas_call(
        body, name=name,
        grid=(m // tm, n // tn, nk),
        in_specs=in_specs,
        out_specs=pl.BlockSpec((tm, tn), lambda i, j, kk: (i, j)),
        out_shape=jax.ShapeDtypeStruct((m, n), out_dtype),
        scratch_shapes=[pltpu.VMEM((tm, tn) if nk > 1 else (8, 128), F32)],
        compiler_params=_params("parallel", "parallel", "arbitrary"),
    )(*args)


def rmsnorm_fwd(x, w, name):
    t, d = x.shape
    tr = min(512, t)

    def body(x_ref, w_ref, h_ref):
        xv = x_ref[...]
        r = lax.rsqrt(jnp.mean(xv * xv, axis=-1, keepdims=True) + EPS)
        h_ref[...] = (xv * r * w_ref[...]).astype(BF16)

    return pl.pallas_call(
        body, name=name, grid=(t // tr,),
        in_specs=[pl.BlockSpec((tr, d), lambda i: (i, 0)), pl.BlockSpec((1, d), lambda i: (0, 0))],
        out_specs=pl.BlockSpec((tr, d), lambda i: (i, 0)),
        out_shape=jax.ShapeDtypeStruct((t, d), BF16),
        compiler_params=_params("parallel"),
    )(x, w.reshape(1, d))


def rmsnorm_bwd(dh, x, w, dres, name):
    t, d = x.shape
    tr = min(512, t)

    def body(dh_ref, x_ref, w_ref, dres_ref, dx_ref, dxb_ref, dw_ref):
        xv = x_ref[...]
        r = lax.rsqrt(jnp.mean(xv * xv, axis=-1, keepdims=True) + EPS)
        xh = xv * r
        g = dh_ref[...].astype(F32)
        dxh = g * w_ref[...]
        dxv = dres_ref[...] + r * (dxh - xh * jnp.mean(dxh * xh, axis=-1, keepdims=True))
        dx_ref[...] = dxv
        dxb_ref[...] = dxv.astype(BF16)
        part = jnp.sum(g * xh, axis=0, keepdims=True)

        @pl.when(pl.program_id(0) == 0)
        def _():
            dw_ref[...] = part

        @pl.when(pl.program_id(0) > 0)
        def _():
            dw_ref[...] += part

    return pl.pallas_call(
        body, name=name, grid=(t // tr,),
        in_specs=[pl.BlockSpec((tr, d), lambda i: (i, 0)), pl.BlockSpec((tr, d), lambda i: (i, 0)),
                  pl.BlockSpec((1, d), lambda i: (0, 0)), pl.BlockSpec((tr, d), lambda i: (i, 0))],
        out_specs=[pl.BlockSpec((tr, d), lambda i: (i, 0)), pl.BlockSpec((tr, d), lambda i: (i, 0)),
                   pl.BlockSpec((1, d), lambda i: (0, 0))],
        out_shape=[jax.ShapeDtypeStruct((t, d), F32), jax.ShapeDtypeStruct((t, d), BF16),
                   jax.ShapeDtypeStruct((1, d), F32)],
        compiler_params=_params("arbitrary"),
    )(dh, x, w.reshape(1, d), dres)


def final_loss(x, w, target, name):
    t, d = x.shape
    tr = min(512, t)

    def body(x_ref, w_ref, tg_ref, loss_ref, dx_ref, dxb_ref, dw_ref):
        xv = x_ref[...]
        r = lax.rsqrt(jnp.mean(xv * xv, axis=-1, keepdims=True) + EPS)
        xh = xv * r
        err = xh * w_ref[...] - tg_ref[...]
        lpart = 0.5 * jnp.sum(jnp.mean(err * err, axis=-1, keepdims=True), axis=0, keepdims=True)
        dy = err * (1.0 / d)
        dxh = dy * w_ref[...]
        dxv = r * (dxh - xh * jnp.mean(dxh * xh, axis=-1, keepdims=True))
        dx_ref[...] = dxv
        dxb_ref[...] = dxv.astype(BF16)
        part = jnp.sum(dy * xh, axis=0, keepdims=True)

        @pl.when(pl.program_id(0) == 0)
        def _():
            dw_ref[...] = part
            loss_ref[...] = jnp.broadcast_to(lpart, (1, 128))

        @pl.when(pl.program_id(0) > 0)
        def _():
            dw_ref[...] += part
            loss_ref[...] += jnp.broadcast_to(lpart, (1, 128))

    return pl.pallas_call(
        body, name=name, grid=(t // tr,),
        in_specs=[pl.BlockSpec((tr, d), lambda i: (i, 0)), pl.BlockSpec((1, d), lambda i: (0, 0)),
                  pl.BlockSpec((tr, d), lambda i: (i, 0))],
        out_specs=[pl.BlockSpec((1, 128), lambda i: (0, 0)), pl.BlockSpec((tr, d), lambda i: (i, 0)),
                   pl.BlockSpec((tr, d), lambda i: (i, 0)), pl.BlockSpec((1, d), lambda i: (0, 0))],
        out_shape=[jax.ShapeDtypeStruct((1, 128), F32), jax.ShapeDtypeStruct((t, d), F32),
                   jax.ShapeDtypeStruct((t, d), BF16), jax.ShapeDtypeStruct((1, d), F32)],
        compiler_params=_params("arbitrary"),
    )(x, w.reshape(1, d), target)


CONV_BW = 256


def _shift_down(u, s, row):
    return jnp.where(row >= s, pltpu.roll(u, s, axis=0), 0.0)


def _shift_up(u, s, row, n):
    return jnp.where(row < n - s, pltpu.roll(u, n - s, axis=0), 0.0)


def _conv_pre(u, w, b, row):
    acc = b + w[CONV_K - 1:CONV_K, :] * u
    for k in range(CONV_K - 1):
        acc = acc + w[k:k + 1, :] * _shift_down(u, CONV_K - 1 - k, row)
    return acc


def conv_fwd(proj, conv_w, conv_b, bl, s, name):
    t = bl * s
    nb = CONV_CH // CONV_BW
    off = OFF_XBC // CONV_BW

    def body(u_ref, w_ref, b_ref, o_ref):
        u = u_ref[...].astype(F32)
        row = lax.broadcasted_iota(jnp.int32, u.shape, 0)
        xc = _conv_pre(u, w_ref[...], b_ref[...], row)
        o_ref[...] = (xc * _sigmoid(xc)).astype(BF16)

    return pl.pallas_call(
        body, name=name, grid=(bl, nb),
        in_specs=[pl.BlockSpec((s, CONV_BW), lambda b, j: (b, off + j)),
                  pl.BlockSpec((CONV_K, CONV_BW), lambda b, j: (0, j)),
                  pl.BlockSpec((1, CONV_BW), lambda b, j: (0, j))],
        out_specs=pl.BlockSpec((s, CONV_BW), lambda b, j: (b, j)),
        out_shape=jax.ShapeDtypeStruct((t, CONV_CH), BF16),
        compiler_params=_params("parallel", "parallel"),
    )(proj, conv_w, conv_b.reshape(1, CONV_CH))


def conv_bwd(dxa, proj, conv_w, conv_b, bl, s, name):
    t = bl * s
    nb = CONV_CH // CONV_BW
    off = OFF_XBC // CONV_BW

    def body(d_ref, u_ref, w_ref, b_ref, du_ref, dw_ref, db_ref):
        u = u_ref[...].astype(F32)
        w = w_ref[...]
        row = lax.broadcasted_iota(jnp.int32, u.shape, 0)
        xc = _conv_pre(u, w, b_ref[...], row)
        sg = _sigmoid(xc)
        dxc = d_ref[...].astype(F32) * sg * (1.0 + xc * (1.0 - sg))
        du = w[CONV_K - 1:CONV_K, :] * dxc
        dws = [None] * CONV_K
        dws[CONV_K - 1] = jnp.sum(dxc * u, axis=0, keepdims=True)
        for k in range(CONV_K - 1):
            sh = CONV_K - 1 - k
            du = du + w[k:k + 1, :] * _shift_up(dxc, sh, row, s)
            dws[k] = jnp.sum(dxc * _shift_down(u, sh, row), axis=0, keepdims=True)
        du_ref[...] = du.astype(BF16)
        krow = lax.broadcasted_iota(jnp.int32, (8, CONV_BW), 0)
        dwv = sum(jnp.where(krow == k, dws[k], 0.0) for k in range(CONV_K))
        dbv = jnp.sum(dxc, axis=0, keepdims=True)

        @pl.when(pl.program_id(1) == 0)
        def _():
            dw_ref[...] = dwv
            db_ref[...] = dbv

        @pl.when(pl.program_id(1) > 0)
        def _():
            dw_ref[...] += dwv
            db_ref[...] += dbv

    du, dw, db = pl.pallas_call(
        body, name=name, grid=(nb, bl),
        in_specs=[pl.BlockSpec((s, CONV_BW), lambda j, b: (b, j)),
                  pl.BlockSpec((s, CONV_BW), lambda j, b: (b, off + j)),
                  pl.BlockSpec((CONV_K, CONV_BW), lambda j, b: (0, j)),
                  pl.BlockSpec((1, CONV_BW), lambda j, b: (0, j))],
        out_specs=[pl.BlockSpec((s, CONV_BW), lambda j, b: (b, j)),
                   pl.BlockSpec((8, CONV_BW), lambda j, b: (0, j)),
                   pl.BlockSpec((1, CONV_BW), lambda j, b: (0, j))],
        out_shape=[jax.ShapeDtypeStruct((t, CONV_CH), BF16), jax.ShapeDtypeStruct((8, CONV_CH), F32),
                   jax.ShapeDtypeStruct((1, CONV_CH), F32)],
        compiler_params=_params("parallel", "arbitrary"),
    )(dxa, proj, conv_w, conv_b.reshape(1, CONV_CH))
    return du, dw[:CONV_K], db[0]


def _tri(shape, cmp):
    r = lax.broadcasted_iota(jnp.int32, shape, 0)
    c = lax.broadcasted_iota(jnp.int32, shape, 1)
    return cmp(r, c)


def dt_fwd(proj, dt_bias, a_log, t, name):
    nchunks = t // CHUNK
    bias = jnp.zeros((1, 128), F32).at[0, :N_HEADS].set(dt_bias)
    alog = jnp.zeros((1, 128), F32).at[0, :N_HEADS].set(a_log)

    def body(raw_ref, b_ref, al_ref, dt_ref, ac_ref):
        raw = raw_ref[...].astype(F32)
        dt = _softplus(raw + b_ref[...])
        adt = dt * (-jnp.exp(al_ref[...]))
        low = _tri((CHUNK, CHUNK), lambda r, c: r >= c).astype(BF16)
        acum = sum(_dot(low, part) for part in _split3(adt))
        dt_ref[...] = dt.T[:N_HEADS]
        ac_ref[...] = acum.T[:N_HEADS]

    return pl.pallas_call(
        body, name=name, grid=(nchunks,),
        in_specs=[pl.BlockSpec((CHUNK, 128), lambda i: (i, OFF_DT // 128)),
                  pl.BlockSpec((1, 128), lambda i: (0, 0)), pl.BlockSpec((1, 128), lambda i: (0, 0))],
        out_specs=[pl.BlockSpec((None, N_HEADS, CHUNK), lambda i: (i, 0, 0))] * 2,
        out_shape=[jax.ShapeDtypeStruct((nchunks, N_HEADS, CHUNK), F32)] * 2,
        compiler_params=_params("parallel"),
    )(proj, bias, alog)


def dt_bwd(ddtT, dacT, dtT, proj, dt_bias, a_log, t, name):
    nchunks = t // CHUNK
    bias = dt_bias.reshape(N_HEADS, 1)
    alog = a_log.reshape(N_HEADS, 1)

    def body(ddt_ref, dac_ref, dt_ref, raw_ref, b_ref, al_ref, draw_ref, da_ref, db_ref):
        a = -jnp.exp(al_ref[...])
        upp = _tri((CHUNK, CHUNK), lambda r, c: r >= c).astype(BF16)
        dadt = sum(_dot(part, upp) for part in _split3(dac_ref[...]))
        ddt = ddt_ref[...] + dadt * a
        rawT = raw_ref[...].astype(F32).T[:N_HEADS]
        draw = ddt * _sigmoid(rawT + b_ref[...])
        padded = jnp.concatenate([draw, jnp.zeros((128 - N_HEADS, CHUNK), F32)], axis=0)
        draw_ref[...] = padded.T.astype(BF16)
        dav = dadt * dt_ref[...]

        @pl.when(pl.program_id(0) == 0)
        def _():
            da_ref[...] = dav
            db_ref[...] = draw

        @pl.when(pl.program_id(0) > 0)
        def _():
            da_ref[...] += dav
            db_ref[...] += draw

    draw, da, db = pl.pallas_call(
        body, name=name, grid=(nchunks,),
        in_specs=[pl.BlockSpec((None, N_HEADS, CHUNK), lambda i: (i, 0, 0))] * 3
        + [pl.BlockSpec((CHUNK, 128), lambda i: (i, OFF_DT // 128)),
           pl.BlockSpec((N_HEADS, 1), lambda i: (0, 0)), pl.BlockSpec((N_HEADS, 1), lambda i: (0, 0))],
        out_specs=[pl.BlockSpec((CHUNK, 128), lambda i: (i, 0)),
                   pl.BlockSpec((N_HEADS, CHUNK), lambda i: (0, 0)), pl.BlockSpec((N_HEADS, CHUNK), lambda i: (0, 0))],
        out_shape=[jax.ShapeDtypeStruct((t, 128), BF16), jax.ShapeDtypeStruct((N_HEADS, CHUNK), F32),
                   jax.ShapeDtypeStruct((N_HEADS, CHUNK), F32)],
        compiler_params=_params("arbitrary"),
    )(ddtT, dacT, dtT, proj, bias, alog)
    return draw, jnp.sum(da, axis=1), jnp.sum(db, axis=1)


PAIRS_G = N_PAIRS // N_GROUPS
GROUP_W = PAIRS_G * 128


def _colb(r):
    return jnp.broadcast_to(r, (CHUNK, 128)).T


def _ssd_pair(x, dtr, acr, tri):
    left = lax.broadcasted_iota(jnp.int32, (CHUNK, 128), 1) < 64
    ac_cols = [_colb(acr[e:e + 1]) for e in range(2)]
    dtl = jnp.where(left, _colb(dtr[0:1]), _colb(dtr[1:2]))
    acl = jnp.where(left, ac_cols[0], ac_cols[1])
    dks = [jnp.exp(jnp.where(tri, ac_cols[e] - acr[e:e + 1], -1e30)) for e in range(2)]
    aclast = acl[CHUNK - 1:CHUNK, :]
    return left, dtl, acl, x * dtl, dks, aclast


def _ssd_specs(nc, rev):
    row = (lambda b, c, g: b * nc + (nc - 1 - c)) if rev else (lambda b, c, g: b * nc + c)
    return dict(
        wide=pl.BlockSpec((CHUNK, GROUP_W), lambda b, c, g: (row(b, c, g), g)),
        bmat=pl.BlockSpec((CHUNK, 128), lambda b, c, g: (row(b, c, g), SSM_W // 128 + g)),
        cmat=pl.BlockSpec((CHUNK, 128), lambda b, c, g: (row(b, c, g), SSM_W // 128 + N_GROUPS + g)),
        rows2=pl.BlockSpec((None, PAIRS_G, 2, CHUNK), lambda b, c, g: (row(b, c, g), g, 0, 0)),
        rows8=pl.BlockSpec((None, PAIRS_G, 8, CHUNK), lambda b, c, g: (row(b, c, g), g, 0, 0)),
        dskip=pl.BlockSpec((1, GROUP_W), lambda b, c, g: (0, g)),
        state=pl.BlockSpec((None, PAIRS_G, N_STATE, 128), lambda b, c, g: (row(b, c, g), g, 0, 0)),
        narrow=pl.BlockSpec((CHUNK, 128), lambda b, c, g: (row(b, c, g), g)))


def ssd_fwd(xa, dtT, acT, dskip_l, bl, s, name):
    t = bl * s
    nc = s // CHUNK
    dt4 = dtT.reshape(bl * nc, N_PAIRS, 2, CHUNK)
    ac4 = acT.reshape(bl * nc, N_PAIRS, 2, CHUNK)

    def body(x_ref, b_ref, c_ref, dt_ref, ac_ref, dsk_ref, y_ref, prev_ref, st_ref):
        c = pl.program_id(1)
        g = pl.program_id(2)
        bm = b_ref[...]
        cm = c_ref[...]
        cb = _dot(cm, bm, _NT)
        tri = _tri((CHUNK, CHUNK), lambda r, c: r >= c)

        @pl.when(c == 0)
        def _():
            for p in range(PAIRS_G):
                st_ref[g * PAIRS_G + p] = jnp.zeros((N_STATE, 128), F32)

        for p in range(PAIRS_G):
            hp = g * PAIRS_G + p
            cs = slice(p * 128, (p + 1) * 128)
            x = x_ref[:, cs].astype(F32)
            left, dtl, acl, xdt, dks, aclast = _ssd_pair(x, dt_ref[p], ac_ref[p], tri)
            xdt_b = xdt.astype(BF16)
            ys = [_dot((cb * dks[e]).astype(BF16), xdt_b) for e in range(2)]
            st = st_ref[hp]
            y_off = _dot(cm, st.astype(BF16)) * jnp.exp(acl)
            y_ref[:, cs] = (jnp.where(left, ys[0], ys[1]) + y_off + x * dsk_ref[:, cs]).astype(BF16)
            xw = (xdt * jnp.exp(aclast - acl)).astype(BF16)
            prev_ref[p] = st
            st_ref[hp] = st * jnp.exp(aclast) + _dot(bm, xw, _TN)

    sp = _ssd_specs(nc, False)
    return pl.pallas_call(
        body, name=name, grid=(bl, nc, N_GROUPS),
        in_specs=[sp["wide"], sp["bmat"], sp["cmat"], sp["rows2"], sp["rows2"], sp["dskip"]],
        out_specs=[sp["wide"], sp["state"]],
        out_shape=[jax.ShapeDtypeStruct((t, SSM_W), BF16),
                   jax.ShapeDtypeStruct((bl * nc, N_PAIRS, N_STATE, 128), F32)],
        scratch_shapes=[pltpu.VMEM((N_PAIRS, N_STATE, 128), F32)],
        compiler_params=_params("parallel", "arbitrary", "arbitrary"),
    )(xa, xa, xa, dt4, ac4, dskip_l)


def ssd_bwd(dy, xa, dtT, acT, dskip_l, prev, bl, s, name):
    t = bl * s
    nc = s // CHUNK
    dt4 = dtT.reshape(bl * nc, N_PAIRS, 2, CHUNK)
    ac4 = acT.reshape(bl * nc, N_PAIRS, 2, CHUNK)

    def body(dy_ref, x_ref, b_ref, c_ref, dt_ref, ac_ref, dsk_ref, prev_ref,
             dx_ref, db_ref, dc_ref, dd_ref, dsk_out_ref, dp_ref):
        b = pl.program_id(0)
        cr = pl.program_id(1)
        g = pl.program_id(2)

        @pl.when(cr == 0)
        def _():
            for p in range(PAIRS_G):
                dp_ref[g * PAIRS_G + p] = jnp.zeros((N_STATE, 128), F32)

        @pl.when((b == 0) & (cr == 0) & (g == 0))
        def _():
            dsk_out_ref[...] = jnp.zeros(dsk_out_ref.shape, F32)

        bm = b_ref[...]
        cm = c_ref[...]
        cb = _dot(cm, bm, _NT)
        tri = _tri((CHUNK, CHUNK), lambda r, c: r >= c)
        lane = lax.broadcasted_iota(jnp.int32, (CHUNK, 128), 1)
        lrow = lax.broadcasted_iota(jnp.int32, (1, CHUNK), 1)
        krow = lax.broadcasted_iota(jnp.int32, (8, CHUNK), 0)
        dcb = jnp.zeros((CHUNK, CHUNK), F32)
        dc_acc = jnp.zeros((CHUNK, N_STATE), F32)
        db_acc = jnp.zeros((CHUNK, N_STATE), F32)
        for p in range(PAIRS_G):
            hp = g * PAIRS_G + p
            cs = slice(p * 128, (p + 1) * 128)
            x = x_ref[:, cs].astype(F32)
            left, dtl, acl, xdt, dks, aclast = _ssd_pair(x, dt_ref[p], ac_ref[p], tri)
            dyv = dy_ref[:, cs].astype(F32)
            dy_b = dyv.astype(BF16)
            xdt_b = xdt.astype(BF16)
            st = prev_ref[p]
            st_b = st.astype(BF16)
            ea = jnp.exp(acl)
            ds = jnp.exp(aclast - acl)
            cdl = jnp.exp(aclast)
            xw = xdt * ds
            masks = [left, jnp.logical_not(left)]

            dsk_out_ref[hp] = dsk_out_ref[hp] + jnp.sum(dyv * x, axis=0, keepdims=True)

            yo = _dot(cm, st_b)
            dyo_b = (dyv * ea).astype(BF16)
            yoff_term = dyv * yo * ea
            dc_acc = dc_acc + _dot(dyo_b, st_b, _NT)
            dst = _dot(cm, dyo_b, _TN)
            dsv = dp_ref[hp]
            dsv_b = dsv.astype(BF16)
            dxw = _dot(bm, dsv_b)
            db_acc = db_acc + _dot(xw.astype(BF16), dsv_b, _NT)
            dxdt = dxw * ds
            qv = dxw * xw
            end_term = dsv * st * cdl
            dp_ref[hp] = dsv * cdl + dst

            cols = jnp.zeros((CHUNK, 128), F32)
            rows = []
            for e in range(2):
                m = cb * dks[e]
                dy_e = jnp.where(masks[e], dyv, 0.0).astype(BF16)
                dm = _dot(dy_e, xdt_b, _NT)
                w = dm * m
                dcb = dcb + dm * dks[e]
                dxdt = dxdt + jnp.where(masks[e], _dot(m.astype(BF16), dy_b, _TN), 0.0)
                dac_col = (jnp.sum(w, axis=1, keepdims=True)
                           + jnp.sum(jnp.where(masks[e], yoff_term - qv, 0.0), axis=1, keepdims=True))
                cols = jnp.where(lane == 2 + e, dac_col, cols)
                tail = (jnp.sum(jnp.where(masks[e], qv, 0.0)) + jnp.sum(jnp.where(masks[e], end_term, 0.0)))
                rows.append(jnp.where(lrow == CHUNK - 1, tail, 0.0) - jnp.sum(w, axis=0, keepdims=True))
            dx_ref[:, cs] = (dxdt * dtl + dyv * dsk_ref[:, cs]).astype(BF16)
            ddt_l = dxdt * x
            for e in range(2):
                cols = jnp.where(lane == e, jnp.sum(jnp.where(masks[e], ddt_l, 0.0), axis=1, keepdims=True), cols)
            dd_ref[p] = cols.T[0:8] + jnp.where(krow == 2, rows[0], 0.0) + jnp.where(krow == 3, rows[1], 0.0)
        dcb_b = dcb.astype(BF16)
        dc_ref[...] = dc_acc + _dot(dcb_b, bm)
        db_ref[...] = db_acc + _dot(dcb_b, cm, _TN)

    sp = _ssd_specs(nc, True)
    dx, db, dc, dd, dsk = pl.pallas_call(
        body, name=name, grid=(bl, nc, N_GROUPS),
        in_specs=[sp["wide"], sp["wide"], sp["bmat"], sp["cmat"], sp["rows2"], sp["rows2"], sp["dskip"], sp["state"]],
        out_specs=[sp["wide"], sp["narrow"], sp["narrow"], sp["rows8"],
                   pl.BlockSpec((N_PAIRS, 1, 128), lambda b, c, g: (0, 0, 0))],
        out_shape=[jax.ShapeDtypeStruct((t, SSM_W), BF16),
                   jax.ShapeDtypeStruct((t, N_GROUPS * N_STATE), F32),
                   jax.ShapeDtypeStruct((t, N_GROUPS * N_STATE), F32),
                   jax.ShapeDtypeStruct((bl * nc, N_PAIRS, 8, CHUNK), F32),
                   jax.ShapeDtypeStruct((N_PAIRS, 1, 128), F32)],
        scratch_shapes=[pltpu.VMEM((N_PAIRS, N_STATE, 128), F32)],
        compiler_params=_params("arbitrary", "arbitrary", "arbitrary"),
    )(dy, xa, xa, xa, dt4, ac4, dskip_l, prev)
    ddtT = dd[:, :, 0:2, :].reshape(bl * nc, N_HEADS, CHUNK)
    dacT = dd[:, :, 2:4, :].reshape(bl * nc, N_HEADS, CHUNK)
    return dx, db, dc, ddtT, dacT, dsk.reshape(N_PAIRS, 128)


def gnorm_fwd(y, proj, w, name):
    t = y.shape[0]
    tr = min(256, t)
    zb = OFF_Z // SSM_W

    def body(y_ref, z_ref, w_ref, o_ref):
        z = z_ref[...].astype(F32)
        yg = y_ref[...].astype(F32) * z * _sigmoid(z)
        r = lax.rsqrt(jnp.mean(yg * yg, axis=-1, keepdims=True) + EPS)
        o_ref[...] = (yg * r * w_ref[...]).astype(BF16)

    return pl.pallas_call(
        body, name=name, grid=(t // tr,),
        in_specs=[pl.BlockSpec((tr, SSM_W), lambda i: (i, 0)), pl.BlockSpec((tr, SSM_W), lambda i: (i, zb)),
                  pl.BlockSpec((1, SSM_W), lambda i: (0, 0))],
        out_specs=pl.BlockSpec((tr, SSM_W), lambda i: (i, 0)),
        out_shape=jax.ShapeDtypeStruct((t, SSM_W), BF16),
        compiler_params=_params("parallel"),
    )(y, proj, w.reshape(1, SSM_W))


def gnorm_bwd(ds, y, proj, w, name):
    t = y.shape[0]
    tr = min(256, t)
    zb = OFF_Z // SSM_W

    def body(ds_ref, y_ref, z_ref, w_ref, dy_ref, dz_ref, dw_ref):
        z = z_ref[...].astype(F32)
        yv = y_ref[...].astype(F32)
        sg = _sigmoid(z)
        sz = z * sg
        yg = yv * sz
        r = lax.rsqrt(jnp.mean(yg * yg, axis=-1, keepdims=True) + EPS)
        xh = yg * r
        g = ds_ref[...].astype(F32)
        dxh = g * w_ref[...]
        dyg = r * (dxh - xh * jnp.mean(dxh * xh, axis=-1, keepdims=True))
        dy_ref[...] = (dyg * sz).astype(BF16)
        dz_ref[...] = (dyg * yv * sg * (1.0 + z * (1.0 - sg))).astype(BF16)
        part = jnp.sum(g * xh, axis=0, keepdims=True)

        @pl.when(pl.program_id(0) == 0)
        def _():
            dw_ref[...] = part

        @pl.when(pl.program_id(0) > 0)
        def _():
            dw_ref[...] += part

    return pl.pallas_call(
        body, name=name, grid=(t // tr,),
        in_specs=[pl.BlockSpec((tr, SSM_W), lambda i: (i, 0)), pl.BlockSpec((tr, SSM_W), lambda i: (i, 0)),
                  pl.BlockSpec((tr, SSM_W), lambda i: (i, zb)), pl.BlockSpec((1, SSM_W), lambda i: (0, 0))],
        out_specs=[pl.BlockSpec((tr, SSM_W), lambda i: (i, 0)), pl.BlockSpec((tr, SSM_W), lambda i: (i, 0)),
                   pl.BlockSpec((1, SSM_W), lambda i: (0, 0))],
        out_shape=[jax.ShapeDtypeStruct((t, SSM_W), BF16), jax.ShapeDtypeStruct((t, SSM_W), BF16),
                   jax.ShapeDtypeStruct((1, SSM_W), F32)],
        compiler_params=_params("arbitrary"),
    )(ds, y, proj, w.reshape(1, SSM_W))


def _pool_mixed(u, g, row):
    win = 2 << g
    acc = u
    for k in range(g + 1):
        acc = acc + _shift_down(acc, 1 << k, row)
    inv = 1.0 / jnp.minimum(row + 1, win).astype(F32)
    return acc * inv - u, inv


def pool_fwd(proj, pool_w, pool_scale, bl, s, name):
    t = bl * s

    def body(u_ref, g_ref, w_ref, sc_ref, o_ref):
        row = lax.broadcasted_iota(jnp.int32, (s, POOL_GD), 0)
        for g in range(POOL_G):
            cs = slice(g * POOL_GD, (g + 1) * POOL_GD)
            u = u_ref[:, cs].astype(F32)
            mixed, _ = _pool_mixed(u, g, row)
            pm = _dot(mixed.astype(BF16), w_ref[g])
            gate = g_ref[:, cs].astype(F32)
            o_ref[:, cs] = (pm * sc_ref[:, cs] * gate * _sigmoid(gate)).astype(BF16)

    return pl.pallas_call(
        body, name=name, grid=(bl,),
        in_specs=[pl.BlockSpec((s, POOL_W), lambda b: (b, OFF_PU // POOL_W)),
                  pl.BlockSpec((s, POOL_W), lambda b: (b, OFF_PG // POOL_W)),
                  pl.BlockSpec((POOL_G, POOL_GD, POOL_GD), lambda b: (0, 0, 0)),
                  pl.BlockSpec((1, POOL_W), lambda b: (0, 0))],
        out_specs=pl.BlockSpec((s, POOL_W), lambda b: (b, 0)),
        out_shape=jax.ShapeDtypeStruct((t, POOL_W), BF16),
        compiler_params=_params("parallel"),
    )(proj, proj, pool_w, pool_scale.reshape(1, POOL_W))


def pool_bwd(dp, proj, pool_w, pool_scale, bl, s, name):
    t = bl * s

    def body(dp_ref, u_ref, g_ref, w_ref, sc_ref, du_ref, dg_ref, dw_ref, dsc_ref):
        row = lax.broadcasted_iota(jnp.int32, (s, POOL_GD), 0)
        first = pl.program_id(0) == 0
        for g in range(POOL_G):
            cs = slice(g * POOL_GD, (g + 1) * POOL_GD)
            u = u_ref[:, cs].astype(F32)
            mixed, inv = _pool_mixed(u, g, row)
            mixed_b = mixed.astype(BF16)
            wg = w_ref[g]
            pm = _dot(mixed_b, wg)
            gate = g_ref[:, cs].astype(F32)
            sg = _sigmoid(gate)
            d = dp_ref[:, cs].astype(F32)
            sc = sc_ref[:, cs]
            dpm = (d * sc * gate * sg).astype(BF16)
            dg_ref[:, cs] = (d * pm * sc * sg * (1.0 + gate * (1.0 - sg))).astype(BF16)
            dsc = jnp.sum(d * pm * gate * sg, axis=0, keepdims=True)
            dwg = _dot(mixed_b, dpm, _TN)
            dmixed = _dot(dpm, wg, _NT)
            acc = dmixed * inv
            for k in range(g + 1):
                acc = acc + _shift_up(acc, 1 << k, row, s)
            du_ref[:, cs] = (acc - dmixed).astype(BF16)

            @pl.when(first)
            def _():
                dw_ref[g] = dwg
                dsc_ref[:, cs] = dsc

            @pl.when(jnp.logical_not(first))
            def _():
                dw_ref[g] = dw_ref[g] + dwg
                dsc_ref[:, cs] = dsc_ref[:, cs] + dsc

    return pl.pallas_call(
        body, name=name, grid=(bl,),
        in_specs=[pl.BlockSpec((s, POOL_W), lambda b: (b, 0)),
                  pl.BlockSpec((s, POOL_W), lambda b: (b, OFF_PU // POOL_W)),
                  pl.BlockSpec((s, POOL_W), lambda b: (b, OFF_PG // POOL_W)),
                  pl.BlockSpec((POOL_G, POOL_GD, POOL_GD), lambda b: (0, 0, 0)),
                  pl.BlockSpec((1, POOL_W), lambda b: (0, 0))],
        out_specs=[pl.BlockSpec((s, POOL_W), lambda b: (b, 0)), pl.BlockSpec((s, POOL_W), lambda b: (b, 0)),
                   pl.BlockSpec((POOL_G, POOL_GD, POOL_GD), lambda b: (0, 0, 0)),
                   pl.BlockSpec((1, POOL_W), lambda b: (0, 0))],
        out_shape=[jax.ShapeDtypeStruct((t, POOL_W), BF16), jax.ShapeDtypeStruct((t, POOL_W), BF16),
                   jax.ShapeDtypeStruct((POOL_G, POOL_GD, POOL_GD), F32), jax.ShapeDtypeStruct((1, POOL_W), F32)],
        compiler_params=_params("arbitrary"),
    )(dp, proj, proj, pool_w, pool_scale.reshape(1, POOL_W))


SB_SCALE = 64 ** -0.5


KB = 256


def _sb_block(qe, kj, mask, rr, upper):
    z = _dot(qe, kj, _NT)
    lb = jnp.minimum(z, 0.0) - jnp.log(1.0 + jnp.exp(-jnp.abs(z)))
    lom = lb - z if mask is None else jnp.where(mask, lb - z, 0.0)
    later = _dot(lom.astype(BF16), upper) + rr
    return lb, lom, later


def _sb_masks(i):
    lane = lax.broadcasted_iota(jnp.int32, (QB, 128), 1)
    row = lax.broadcasted_iota(jnp.int32, (2 * QB, KB), 0) % QB
    col = lax.broadcasted_iota(jnp.int32, (2 * QB, KB), 1)
    causal = lambda jb: col + (jb * KB - i * QB) < row
    return lane, lane < 64, causal


def _stack_heads(x, left):
    zero = jnp.zeros_like(x)
    return jnp.concatenate([jnp.where(left, x, zero), jnp.where(left, zero, x)], axis=0)


SB_GROUP = 4
SB_GW = SB_GROUP * 128


def sb_fwd(proj, bl, s, name, comm=None):
    t = bl * s
    nq = s // QB
    qb0, kb0, vb0, gb0 = OFF_QKV // SB_GW, (OFF_QKV + SB_W) // SB_GW, (OFF_QKV + 2 * SB_W) // SB_GW, OFF_SBG // SB_GW
    grid = (bl, SB_PAIRS // SB_GROUP, nq)
    x_in, x_args, x_out, x_shape, x_scratch, x_start, x_wait = _hosted_exchange(
        comm, grid, relay_at=(bl - 1, SB_PAIRS // SB_GROUP - 1, 0))

    def body(*refs):
        q_ref, k_ref, v_ref, g_ref = refs[:4]
        og_ref, o_ref, r_ref = refs[4 + len(x_in):7 + len(x_in)]
        x_refs = refs[4:4 + len(x_in)] + refs[7 + len(x_in):]
        x_start(x_refs)
        i = pl.program_id(2)
        lane, left, causal = _sb_masks(i)
        upper = _tri((KB, KB), lambda r, c: r > c).astype(BF16)
        cols = [slice(p * 128, (p + 1) * 128) for p in range(SB_GROUP)]
        qcats = [_stack_heads(q_ref[:, cs] * SB_SCALE, left) for cs in cols]
        zero = qcats[0].astype(F32) * 0.0

        def block(jb, carry, diagonal):
            rows = pl.ds(pl.multiple_of(jb * KB, KB), KB)
            mask = causal(jb) if diagonal else None
            out = []
            for p, cs in enumerate(cols):
                acc, rr, rt = carry[p]
                lb, lom, later = _sb_block(qcats[p], k_ref[rows, cs], mask, rr, upper)
                att = jnp.exp(lb + later)
                if diagonal:
                    att = jnp.where(mask, att, 0.0)
                acc = acc + _dot(att.astype(BF16), v_ref[rows, cs])
                rt = jnp.where(lane == jb, rr[:QB], jnp.where(lane == 8 + jb, rr[QB:], rt))
                out.append((acc, rr + jnp.sum(lom, axis=1, keepdims=True), rt))
            return tuple(out)

        carry = block(i, tuple((zero, zero[:, :1], zero[:QB]) for _ in cols), True)
        carry = lax.fori_loop(0, i, lambda jj, c: block(i - 1 - jj, c, False), carry)
        for p, cs in enumerate(cols):
            acc, _, rtile = carry[p]
            o = jnp.where(left, acc[:QB], acc[QB:])
            gate = g_ref[:, cs].astype(F32)
            o_ref[:, cs] = o.astype(BF16)
            og_ref[:, cs] = (o * gate * _sigmoid(gate)).astype(BF16)
            r_ref[p] = rtile
        x_wait(x_refs)

    rowblk = lambda b, g, i: (b * nq + i, g)
    return pl.pallas_call(
        body, name=name, grid=grid,
        in_specs=[pl.BlockSpec((QB, SB_GW), lambda b, g, i: (b * nq + i, qb0 + g)),
                  pl.BlockSpec((s, SB_GW), lambda b, g, i: (b, kb0 + g)),
                  pl.BlockSpec((s, SB_GW), lambda b, g, i: (b, vb0 + g)),
                  pl.BlockSpec((QB, SB_GW), lambda b, g, i: (b * nq + i, gb0 + g))] + x_in,
        out_specs=[pl.BlockSpec((QB, SB_GW), rowblk), pl.BlockSpec((QB, SB_GW), rowblk),
                   pl.BlockSpec((None, SB_GROUP, QB, 128), lambda b, g, i: (b * nq + i, g, 0, 0))] + x_out,
        out_shape=[jax.ShapeDtypeStruct((t, SB_W), BF16), jax.ShapeDtypeStruct((t, SB_W), BF16),
                   jax.ShapeDtypeStruct((bl * nq, SB_PAIRS, QB, 128), F32)] + x_shape,
        scratch_shapes=x_scratch,
        compiler_params=_params("arbitrary", "arbitrary", "arbitrary"),
    )(proj, proj, proj, proj, *x_args)


def sb_bwd(dsb, o, rsave, proj, bl, s, name, comm=None):
    t = bl * s
    nq = s // QB
    qb0, kb0, vb0, gb0 = OFF_QKV // SB_GW, (OFF_QKV + SB_W) // SB_GW, (OFF_QKV + 2 * SB_W) // SB_GW, OFF_SBG // SB_GW
    grid = (bl, SB_PAIRS // SB_GROUP, nq)
    x_in, x_args, x_out, x_shape, x_scratch, x_start, x_wait = _hosted_exchange(
        comm, grid, relay_at=(bl - 1, SB_PAIRS // SB_GROUP - 1, 0))

    def body(*refs):
        n = len(x_in)
        d_ref, o_ref, r_ref, q_ref, k_ref, v_ref, g_ref = refs[:7]
        dq_ref, dk_ref, dv_ref, dg_ref = refs[7 + n:11 + n]
        dk_acc, dv_acc = refs[11 + 2 * n:13 + 2 * n]
        x_refs = refs[7:7 + n] + refs[11 + n:11 + 2 * n] + refs[13 + 2 * n:]
        x_start(x_refs)
        i = pl.program_id(2)

        @pl.when(i == 0)
        def _():
            dk_acc[...] = jnp.zeros(dk_acc.shape, F32)
            dv_acc[...] = jnp.zeros(dv_acc.shape, F32)

        lane, left, causal = _sb_masks(i)
        upper = _tri((KB, KB), lambda r, c: r > c).astype(BF16)
        lower = _tri((KB, KB), lambda r, c: r < c).astype(BF16)
        cols = [slice(p * 128, (p + 1) * 128) for p in range(SB_GROUP)]
        qcats, docats = [], []
        for cs in cols:
            gate = g_ref[:, cs].astype(F32)
            sg = _sigmoid(gate)
            d = d_ref[:, cs].astype(F32)
            dg_ref[:, cs] = (d * o_ref[:, cs].astype(F32) * sg * (1.0 + gate * (1.0 - sg))).astype(BF16)
            docats.append(_stack_heads((d * gate * sg).astype(BF16), left))
            qcats.append(_stack_heads(q_ref[:, cs] * SB_SCALE, left))
        qcat_ts = [qc.astype(F32).T.astype(BF16) for qc in qcats]
        docat_ts = [dc.astype(F32).T.astype(BF16) for dc in docats]
        zero = qcats[0].astype(F32) * 0.0

        def block(jb, carry, diagonal):
            rows = pl.ds(pl.multiple_of(jb * KB, KB), KB)
            mask = causal(jb) if diagonal else None
            out = []
            for p, cs in enumerate(cols):
                dq, gcar = carry[p]
                kj = k_ref[rows, cs]
                vj = v_ref[rows, cs]
                rtile = r_ref[p]
                rr = jnp.concatenate(
                    [jnp.sum(jnp.where(lane == jb, rtile, 0.0), axis=1, keepdims=True),
                     jnp.sum(jnp.where(lane == 8 + jb, rtile, 0.0), axis=1, keepdims=True)], axis=0)
                lb, lom, later = _sb_block(qcats[p], kj, mask, rr, upper)
                att = jnp.exp(lb + later)
                if diagonal:
                    att = jnp.where(mask, att, 0.0)
                de = att * _dot(docats[p], vj, _NT)
                gpre = _dot(de.astype(BF16), lower) + gcar
                sig = jnp.exp(lb)
                dz = de * (1.0 - sig) - gpre * sig
                if diagonal:
                    dz = jnp.where(mask, dz, 0.0)
                dz = dz.astype(BF16)
                dk_acc[jb, cs, :] = dk_acc[jb, cs, :] + _dot(qcat_ts[p], dz)
                dv_acc[jb, cs, :] = dv_acc[jb, cs, :] + _dot(docat_ts[p], att.astype(BF16))
                out.append((dq + _dot(dz, kj), gcar + jnp.sum(de, axis=1, keepdims=True)))
            return tuple(out)

        carry = lax.fori_loop(0, i, lambda jb, c: block(jb, c, False), tuple((zero, zero[:, :1]) for _ in cols))
        carry = block(i, carry, True)
        for p, cs in enumerate(cols):
            dq = carry[p][0]
            dq_ref[:, cs] = (jnp.where(left, dq[:QB], dq[QB:]) * SB_SCALE).astype(BF16)

        @pl.when(i == nq - 1)
        def _():
            for kb in range(s // KB):
                for cs in cols:
                    dk_ref[kb * KB:(kb + 1) * KB, cs] = dk_acc[kb, cs, :].T.astype(BF16)
                    dv_ref[kb * KB:(kb + 1) * KB, cs] = dv_acc[kb, cs, :].T.astype(BF16)

        x_wait(x_refs)

    rowblk = lambda b, g, i: (b * nq + i, g)
    seqblk = lambda b, g, i: (b, g)
    return pl.pallas_call(
        body, name=name, grid=grid,
        in_specs=[pl.BlockSpec((QB, SB_GW), rowblk), pl.BlockSpec((QB, SB_GW), rowblk),
                  pl.BlockSpec((None, SB_GROUP, QB, 128), lambda b, g, i: (b * nq + i, g, 0, 0)),
                  pl.BlockSpec((QB, SB_GW), lambda b, g, i: (b * nq + i, qb0 + g)),
                  pl.BlockSpec((s, SB_GW), lambda b, g, i: (b, kb0 + g)),
                  pl.BlockSpec((s, SB_GW), lambda b, g, i: (b, vb0 + g)),
                  pl.BlockSpec((QB, SB_GW), lambda b, g, i: (b * nq + i, gb0 + g))] + x_in,
        out_specs=[pl.BlockSpec((QB, SB_GW), rowblk), pl.BlockSpec((s, SB_GW), seqblk),
                   pl.BlockSpec((s, SB_GW), seqblk), pl.BlockSpec((QB, SB_GW), rowblk)] + x_out,
        out_shape=[jax.ShapeDtypeStruct((t, SB_W), BF16)] * 4 + x_shape,
        scratch_shapes=[pltpu.VMEM((s // KB, SB_GW, KB), F32), pltpu.VMEM((s // KB, SB_GW, KB), F32)] + x_scratch,
        compiler_params=_params("arbitrary", "arbitrary", "arbitrary"),
    )(dsb, o, rsave, proj, proj, proj, proj, *x_args)


def merge_fwd(proj, ys, yp, yb, name):
    t = ys.shape[0]
    tr = min(512, t)

    def body(m_ref, ys_ref, yp_ref, yb_ref, o_ref):
        acc = jnp.zeros((tr, D), F32)
        for k, ref in enumerate((ys_ref, yp_ref, yb_ref)):
            acc = acc + _sigmoid(m_ref[:, k * D:(k + 1) * D].astype(F32)) * ref[...].astype(F32)
        o_ref[...] = acc.astype(BF16)

    rowblk = pl.BlockSpec((tr, D), lambda i: (i, 0))
    return pl.pallas_call(
        body, name=name, grid=(t // tr,),
        in_specs=[pl.BlockSpec((tr, 3 * D), lambda i: (i, 0)), rowblk, rowblk, rowblk],
        out_specs=rowblk,
        out_shape=jax.ShapeDtypeStruct((t, D), BF16),
        compiler_params=_params("parallel"),
    )(proj, ys, yp, yb)


def merge_bwd(dm, proj, ys, yp, yb, name):
    t = ys.shape[0]
    tr = min(512, t)

    def body(dm_ref, m_ref, ys_ref, yp_ref, yb_ref, d0_ref, d1_ref, d2_ref, dl_ref):
        dmv = dm_ref[...].astype(F32)
        for k, (ref, dref) in enumerate(((ys_ref, d0_ref), (yp_ref, d1_ref), (yb_ref, d2_ref))):
            g = _sigmoid(m_ref[:, k * D:(k + 1) * D].astype(F32))
            dref[...] = (g * dmv).astype(BF16)
            dl_ref[:, k * D:(k + 1) * D] = (dmv * ref[...].astype(F32) * g * (1.0 - g)).astype(BF16)

    rowblk = pl.BlockSpec((tr, D), lambda i: (i, 0))
    wide = pl.BlockSpec((tr, 3 * D), lambda i: (i, 0))
    return pl.pallas_call(
        body, name=name, grid=(t // tr,),
        in_specs=[rowblk, wide, rowblk, rowblk, rowblk],
        out_specs=[rowblk, rowblk, rowblk, wide],
        out_shape=[jax.ShapeDtypeStruct((t, D), BF16)] * 3 + [jax.ShapeDtypeStruct((t, 3 * D), BF16)],
        compiler_params=_params("parallel"),
    )(dm, proj, ys, yp, yb)


def layer_fwd(x, lw, bl, s, tag, comm=None):
    t = bl * s
    h = rmsnorm_fwd(x, lw["norm_w"], f"norm_fwd{tag}")
    proj = matmul(h, lw["w_in"], "nt", BF16, f"in_proj{tag}", tn=2048)
    xa = conv_fwd(proj, lw["conv_w"], lw["conv_b"], bl, s, f"conv_fwd{tag}")
    dtT, acT = dt_fwd(proj, lw["dt_bias"], lw["a_log"], t, f"dt_fwd{tag}")
    dskip_l = jnp.repeat(lw["d_skip"], 64).reshape(1, SSM_W)
    y, prev = ssd_fwd(xa, dtT, acT, dskip_l, bl, s, f"ssd_fwd{tag}")
    s_out = gnorm_fwd(y, proj, lw["ssm_norm_w"], f"gnorm_fwd{tag}")
    p_out = pool_fwd(proj, lw["pool_w"], lw["pool_scale"], bl, s, f"pool_fwd{tag}")
    sb_out, sb_o, sb_r, *carried = sb_fwd(proj, bl, s, f"sb_fwd{tag}", comm)
    ys = matmul(s_out, lw["w_proj_ssm"], "nn", BF16, f"proj_ssm{tag}")
    yp = matmul(p_out, lw["w_proj_pool"], "nn", BF16, f"proj_pool{tag}")
    yb = matmul(sb_out, lw["w_proj_sb"], "nn", BF16, f"proj_sb{tag}")
    merged = merge_fwd(proj, ys, yp, yb, f"merge_fwd{tag}")
    x_next = matmul(merged, lw["w_out"], "nn", F32, f"out_proj{tag}", residual=x)
    saved = dict(x=x, h=h, proj=proj, xa=xa, dtT=dtT, acT=acT, y=y, prev=prev, s_out=s_out, p_out=p_out,
                 sb_out=sb_out, sb_o=sb_o, sb_r=sb_r, ys=ys, yp=yp, yb=yb, merged=merged)
    return x_next, saved, (carried[0] if carried else None)


def layer_bwd(dx, dx_b, lw, sv, bl, s, tag, comm=None):
    t = bl * s
    g = {}
    dmerged = matmul(dx_b, lw["w_out"], "nt", BF16, f"d_merged{tag}")
    g["w_out"] = matmul(sv["merged"], dx_b, "tn", F32, f"dw_out{tag}")
    dys, dyp, dyb, dlogit = merge_bwd(dmerged, sv["proj"], sv["ys"], sv["yp"], sv["yb"], f"merge_bwd{tag}")
    ds_out = matmul(dys, lw["w_proj_ssm"], "nt", BF16, f"d_sout{tag}")
    g["w_proj_ssm"] = matmul(sv["s_out"], dys, "tn", F32, f"dw_proj_ssm{tag}")
    dp_out = matmul(dyp, lw["w_proj_pool"], "nt", BF16, f"d_pout{tag}")
    g["w_proj_pool"] = matmul(sv["p_out"], dyp, "tn", F32, f"dw_proj_pool{tag}")
    dsb_out = matmul(dyb, lw["w_proj_sb"], "nt", BF16, f"d_sbout{tag}")
    g["w_proj_sb"] = matmul(sv["sb_out"], dyb, "tn", F32, f"dw_proj_sb{tag}")
    dy, dz, dnw = gnorm_bwd(ds_out, sv["y"], sv["proj"], lw["ssm_norm_w"], f"gnorm_bwd{tag}")
    g["ssm_norm_w"] = dnw[0]
    dskip_l = jnp.repeat(lw["d_skip"], 64).reshape(1, SSM_W)
    dxs, db, dc, ddtT, dacT, dsk = ssd_bwd(dy, sv["xa"], sv["dtT"], sv["acT"], dskip_l, sv["prev"], bl, s, f"ssd_bwd{tag}")
    g["d_skip"] = jnp.sum(dsk.reshape(N_HEADS, 64), axis=1)
    ddt_raw, da, dbias = dt_bwd(ddtT, dacT, sv["dtT"], sv["proj"], lw["dt_bias"], lw["a_log"], t, f"dt_bwd{tag}")
    g["a_log"] = da * (-jnp.exp(lw["a_log"]))
    g["dt_bias"] = dbias
    dxa = jnp.concatenate([dxs, db.astype(BF16), dc.astype(BF16)], axis=1)
    dxbc, dcw, dcb = conv_bwd(dxa, sv["proj"], lw["conv_w"], lw["conv_b"], bl, s, f"conv_bwd{tag}")
    g["conv_w"] = dcw
    g["conv_b"] = dcb
    dpu, dpg, dpw, dpsc = pool_bwd(dp_out, sv["proj"], lw["pool_w"], lw["pool_scale"], bl, s, f"pool_bwd{tag}")
    g["pool_w"] = dpw
    g["pool_scale"] = dpsc[0]
    dq, dk, dv, dsbg, *carried = sb_bwd(dsb_out, sv["sb_o"], sv["sb_r"], sv["proj"], bl, s, f"sb_bwd{tag}", comm)
    dproj = concat_columns([dlogit, dsbg, dpu, dpg, dz, dq, dk, dv, dxbc, ddt_raw], PC, f"d_proj{tag}")
    g["w_in"] = matmul(dproj, sv["h"], "tn", F32, f"dw_in{tag}")
    dh = matmul(dproj, lw["w_in"], "nn", F32, f"d_h{tag}")
    dx_in, dx_in_b, dnorm = rmsnorm_bwd(dh, sv["x"], lw["norm_w"], dx, f"norm_bwd{tag}")
    g["norm_w"] = dnorm[0]
    return dx_in, dx_in_b, g, (carried[0] if carried else None)


def concat_columns(parts, width, name):
    t = parts[0].shape[0]
    tr = min(256, t)
    widths = [p.shape[1] for p in parts]
    used = sum(widths)

    def body(*refs):
        o_ref = refs[-1]
        off = 0
        for ref, w in zip(refs[:-1], widths):
            o_ref[:, off:off + w] = ref[...]
            off += w
        if width > used:
            o_ref[:, used:] = jnp.zeros((tr, width - used), BF16)

    return pl.pallas_call(
        body, name=name, grid=(t // tr,),
        in_specs=[pl.BlockSpec((tr, w), lambda i: (i, 0)) for w in widths],
        out_specs=pl.BlockSpec((tr, width), lambda i: (i, 0)),
        out_shape=jax.ShapeDtypeStruct((t, width), BF16),
        compiler_params=_params("parallel"),
    )(*parts)


_PAD_PIECES = ((10784, 3072), (9760, 1024), (4640, 1024), (5664, 1024), (0, 2048), (6688, 3072), (2048, 2560), (4608, 32))
_UNPAD_PIECES = ((OFF_Z, 2048), (OFF_XBC, 2560), (OFF_DT, 32), (OFF_PU, 1024), (OFF_PG, 1024), (OFF_QKV, 3072),
                 (OFF_SBG, 1024), (OFF_MERGE, 3072))


def pad_rows(wt):
    pieces = [wt[o:o + n] for o, n in _PAD_PIECES]
    return jnp.concatenate(pieces + [jnp.zeros((PC - IN_COLS, wt.shape[1]), wt.dtype)], axis=0)


def unpad_rows(wp):
    return jnp.concatenate([wp[o:o + n] for o, n in _UNPAD_PIECES], axis=0)


MESH = pl.DeviceIdType.MESH
ANY = pl.BlockSpec(memory_space=pl.ANY)


def _coords():
    return lax.axis_index("x"), lax.axis_index("y"), lax.axis_index("c")


def _peer(p):
    x, y, c = _coords()
    return (1 - x if p & 4 else x, 1 - y if p & 2 else y, 1 - c if p & 1 else c)


def _flat(pos):
    return 4 * pos[0] + 2 * pos[1] + pos[2]


def _chip(pos):
    return 2 * pos[0] + pos[1]


def _exchange_copies(v_ref, out_ref, send_sems, recv_sems, local_sem, mode):
    x, y, c = _coords()
    me = _flat((x, y, c))
    sibling = (x, y, 1 - c)
    chips = [(1 - x if j & 2 else x, 1 - y if j & 1 else y) for j in range(1, 4)]

    def copy(k, src, landing, to):
        return pltpu.make_async_remote_copy(src_ref=src, dst_ref=out_ref.at[landing], send_sem=send_sems.at[k],
                                            recv_sem=recv_sems.at[k], device_id=to, device_id_type=MESH)

    if mode == "direct":
        local = pltpu.make_async_copy(v_ref, out_ref.at[me], local_sem)
        first = [copy(p - 1, v_ref, me, _peer(p)) for p in range(1, N_DEV)]
        last = [copy(p - 1, v_ref, _flat(_peer(p)), _peer(p)) for p in range(1, N_DEV)]
        return local, first, [], last
    if mode == "gather":
        local = pltpu.make_async_copy(v_ref, out_ref.at[me], local_sem)
        first = [copy(0, v_ref, me, sibling)] + [copy(1 + j, v_ref, me, (*ch, c)) for j, ch in enumerate(chips)]
        relay = [(copy(1 + j, v_ref, _flat((*ch, c)), (*ch, c)),
                  copy(4 + j, out_ref.at[_flat((*ch, c))], _flat((*ch, c)), sibling)) for j, ch in enumerate(chips)]
        last = [copy(0, v_ref, _flat(sibling), sibling)] + [
            copy(4 + j, v_ref, _flat((*ch, 1 - c)), sibling) for j, ch in enumerate(chips)]
        return local, first, relay, last
    assert mode == "chips"
    mine = _chip((x, y))
    local = pltpu.make_async_copy(v_ref.at[mine], out_ref.at[mine], local_sem)
    first = [copy(j, v_ref.at[_chip(ch)], mine, (*ch, c)) for j, ch in enumerate(chips)]
    last = [copy(j, v_ref.at[mine], _chip(ch), (*ch, c)) for j, ch in enumerate(chips)]
    return local, first, [], last


def _exchange_start(*refs_and_mode):
    local, first, _, _ = _exchange_copies(*refs_and_mode)
    local.start()
    for cp in first:
        cp.start()


def _exchange_relay(*refs_and_mode):
    for arrival, onward in _exchange_copies(*refs_and_mode)[2]:
        arrival.wait_recv()
        onward.start()


def _exchange_finish(*refs_and_mode):
    local, first, relay, last = _exchange_copies(*refs_and_mode)
    for cp in last:
        cp.wait_recv()
    for cp in first + [onward for _, onward in relay]:
        cp.wait_send()
    local.wait()


def _exchange_shape(v, mode):
    return jax.ShapeDtypeStruct(tuple(v.shape) if mode == "chips" else (N_DEV,) + tuple(v.shape), v.dtype)


def _exchange_sems():
    return [pltpu.SemaphoreType.DMA((N_DEV - 1,)), pltpu.SemaphoreType.DMA((N_DEV - 1,)), pltpu.SemaphoreType.DMA]


def exchange(v, mode, name):
    def body(*refs):
        _exchange_start(*refs, mode)
        _exchange_relay(*refs, mode)
        _exchange_finish(*refs, mode)

    return pl.pallas_call(
        body, name=name,
        in_specs=[ANY], out_specs=ANY,
        out_shape=_exchange_shape(v, mode),
        scratch_shapes=_exchange_sems(),
    )(v)


def _hosted_exchange(comm, grid, relay_at):
    if comm is None:
        return [], [], [], [], [], (lambda refs: None), (lambda refs: None)
    v, mode = comm

    def at(step):
        cond = None
        for axis, want in enumerate(step):
            term = pl.program_id(axis) == want
            cond = term if cond is None else jnp.logical_and(cond, term)
        return cond

    def start(refs):
        @pl.when(at([0] * len(grid)))
        def _():
            _exchange_start(*refs, mode)

        if mode == "gather":
            @pl.when(at(relay_at))
            def _():
                _exchange_relay(*refs, mode)

    def wait(refs):
        @pl.when(at([n - 1 for n in grid]))
        def _():
            _exchange_finish(*refs, mode)

    return [ANY], [v], [ANY], [_exchange_shape(v, mode)], _exchange_sems(), start, wait


def pair_swap(s0, s1, name):
    def body(s0_ref, s1_ref, out_ref, send_sems, recv_sems):
        x, y, c = _coords()

        def copies(src_ref):
            return [pltpu.make_async_remote_copy(src_ref=src_ref.at[j], dst_ref=out_ref.at[j], send_sem=send_sems.at[j],
                                                 recv_sem=recv_sems.at[j], device_id=(x, y, 1 - c), device_id_type=MESH)
                    for j in range(4)]

        @pl.when(c == 0)
        def _():
            for cp in copies(s1_ref):
                cp.start()

        @pl.when(c == 1)
        def _():
            for cp in copies(s0_ref):
                cp.start()

        for cp in copies(s0_ref):
            cp.wait_recv()
        for cp in copies(s0_ref):
            cp.wait_send()

    return pl.pallas_call(
        body, name=name,
        in_specs=[ANY, ANY], out_specs=ANY,
        out_shape=jax.ShapeDtypeStruct(s0.shape, s0.dtype),
        scratch_shapes=[pltpu.SemaphoreType.DMA((4,)), pltpu.SemaphoreType.DMA((4,))],
    )(s0, s1)


def pair_sum(s0, s1, got, name):
    _, r, c = s0.shape
    tr = 128 if r % 128 == 0 else r

    def body(s0_ref, s1_ref, got_ref, o_ref):
        mine = jnp.where(lax.axis_index("c") == 0, s0_ref[...].astype(F32), s1_ref[...].astype(F32))
        o_ref[...] = (mine + got_ref[...].astype(F32)).astype(BF16)

    blk = pl.BlockSpec((None, tr, c), lambda j, i: (j, i, 0))
    return pl.pallas_call(
        body, name=name, grid=(4, r // tr),
        in_specs=[blk, blk, blk], out_specs=blk,
        out_shape=jax.ShapeDtypeStruct(s0.shape, BF16),
        compiler_params=_params("parallel", "parallel"),
    )(s0, s1, got)


def sum_slabs(v, name):
    n, r, c = v.shape
    tr = 128 if r % 128 == 0 else r

    def body(v_ref, o_ref):
        acc = v_ref[0].astype(F32)
        for k in range(1, n):
            acc = acc + v_ref[k].astype(F32)
        o_ref[...] = acc

    return pl.pallas_call(
        body, name=name, grid=(r // tr,),
        in_specs=[pl.BlockSpec((n, tr, c), lambda i: (0, i, 0))],
        out_specs=pl.BlockSpec((tr, c), lambda i: (i, 0)),
        out_shape=jax.ShapeDtypeStruct((r, c), F32),
        compiler_params=_params("parallel"),
    )(v)


def adamw(w, g, m, v, name):
    r, c = w.shape
    tr = next((cand for cand in (256, 128, 64, 32, 16, 8) if r % cand == 0), r)

    def body(w_ref, g_ref, m_ref, v_ref, d_ref, mo_ref, vo_ref):
        gv = g_ref[...]
        mn = ADAM_B1 * m_ref[...] + (1.0 - ADAM_B1) * gv
        vn = ADAM_B2 * v_ref[...] + (1.0 - ADAM_B2) * (gv * gv)
        m_hat = mn / (1.0 - ADAM_B1 ** ADAM_STEP)
        v_hat = vn / (1.0 - ADAM_B2 ** ADAM_STEP)
        d_ref[...] = -ADAM_LR * (m_hat / (jnp.sqrt(v_hat) + ADAM_EPS) + ADAM_WD * w_ref[...])
        mo_ref[...] = mn
        vo_ref[...] = vn

    blk = pl.BlockSpec((tr, c), lambda i: (i, 0))
    return pl.pallas_call(
        body, name=name, grid=(r // tr,),
        in_specs=[blk] * 4, out_specs=[blk] * 3,
        out_shape=[jax.ShapeDtypeStruct((r, c), F32)] * 3,
        compiler_params=_params("parallel"),
    )(w, g, m, v)


BIG = ("w_proj_ssm", "w_proj_pool", "w_proj_sb", "w_out", "pool_w", "w_in")
SHARD_IN = IN_COLS // N_DEV
BIG_ROWS = {"w_proj_ssm": SSM_W // N_DEV, "w_proj_pool": POOL_W // N_DEV, "w_proj_sb": SB_W // N_DEV,
            "w_out": D // N_DEV, "pool_w": POOL_G * (POOL_GD // N_DEV) * POOL_GD // D, "w_in": SHARD_IN}
PACK_C = D
PACK_R = 2432

REPLICATED = ("norm_w", "conv_b", "dt_bias", "a_log", "d_skip", "ssm_norm_w", "pool_scale")
WEIGHTS = ("norm_w", "w_in", "conv_w", "conv_b", "dt_bias", "a_log", "d_skip", "ssm_norm_w", "pool_w",
           "pool_scale", "w_proj_ssm", "w_proj_pool", "w_proj_sb", "w_out", "final_norm_w")


def _size(shape):
    n = 1
    for d in shape:
        n *= d
    return n


def _pad_flat(flat, n):
    return jnp.concatenate([flat, jnp.zeros((n - flat.shape[0],), flat.dtype)])


def _row_offsets():
    offs, off = {}, 0
    for n in BIG:
        offs[n] = off
        off += BIG_ROWS[n]
    return offs, off


def pack_shards(parts):
    rows = [parts[n].reshape(BIG_ROWS[n], PACK_C) for n in BIG]
    rows[-1] = jnp.pad(rows[-1], ((0, PACK_R - _row_offsets()[1]), (0, 0)))
    return jnp.concatenate(rows, axis=0)


def unpack_shards(packed):
    offs, _ = _row_offsets()
    out = {}
    for n in BIG:
        seg = packed[offs[n]:offs[n] + BIG_ROWS[n]]
        if n == "w_in":
            out[n] = seg.T
        elif n == "pool_w":
            out[n] = seg.reshape(POOL_G, POOL_GD // N_DEV, POOL_GD)
        else:
            out[n] = seg
    return out


def unpack_gathered(g):
    offs, _ = _row_offsets()
    out = {}
    for n in BIG:
        seg = g[:, offs[n]:offs[n] + BIG_ROWS[n], :]
        if n == "w_in":
            out[n] = pad_rows(seg.reshape(IN_COLS, D))
        elif n == "pool_w":
            out[n] = seg.reshape(N_DEV, POOL_G, POOL_GD // N_DEV, POOL_GD).transpose(1, 0, 2, 3).reshape(
                POOL_G, POOL_GD, POOL_GD)
        else:
            out[n] = seg.reshape(N_DEV * BIG_ROWS[n], D)
    return out


def pack_slabs(g):
    segs = []
    for n in BIG:
        if n == "w_in":
            w = unpad_rows(g[n])
        elif n == "pool_w":
            w = g[n].reshape(POOL_G, N_DEV, POOL_GD // N_DEV, POOL_GD).transpose(1, 0, 2, 3)
        else:
            w = g[n]
        segs.append(w.reshape(N_DEV // 2, 2, BIG_ROWS[n], PACK_C).astype(BF16))
    segs[-1] = jnp.pad(segs[-1], ((0, 0), (0, 0), (0, PACK_R - _row_offsets()[1]), (0, 0)))
    return tuple(jnp.concatenate([seg[:, core] for seg in segs], axis=1) for core in range(2))


SMALL_ROWS = 544


def pack_small(vals):
    flat = jnp.concatenate([v.reshape(-1) for v in vals])
    return _pad_flat(flat, SMALL_ROWS * 128).reshape(SMALL_ROWS, 128)


def unpack_small(packed, shapes):
    flat = packed.reshape(-1)
    out, off = [], 0
    for shp in shapes:
        out.append(flat[off:off + _size(shp)].reshape(shp))
        off += _size(shp)
    return out


def kernel(x, norm_w, w_in, conv_w, conv_b, dt_bias, a_log, d_skip, ssm_norm_w, pool_w, pool_scale, w_proj_ssm, w_proj_pool, w_proj_sb, w_out, final_norm_w, loss_target, m_norm_w, m_w_in, m_conv_w, m_conv_b, m_dt_bias, m_a_log, m_d_skip, m_ssm_norm_w, m_pool_w, m_pool_scale, m_w_proj_ssm, m_w_proj_pool, m_w_proj_sb, m_w_out, m_final_norm_w, v_norm_w, v_w_in, v_conv_w, v_conv_b, v_dt_bias, v_a_log, v_d_skip, v_ssm_norm_w, v_pool_w, v_pool_scale, v_w_proj_ssm, v_w_proj_pool, v_w_proj_sb, v_w_out, v_final_norm_w):
    wts = dict(norm_w=norm_w, w_in=w_in, conv_w=conv_w, conv_b=conv_b, dt_bias=dt_bias, a_log=a_log, d_skip=d_skip,
               ssm_norm_w=ssm_norm_w, pool_w=pool_w, pool_scale=pool_scale, w_proj_ssm=w_proj_ssm,
               w_proj_pool=w_proj_pool, w_proj_sb=w_proj_sb, w_out=w_out, final_norm_w=final_norm_w)
    mom = dict(norm_w=m_norm_w, w_in=m_w_in, conv_w=m_conv_w, conv_b=m_conv_b, dt_bias=m_dt_bias, a_log=m_a_log,
               d_skip=m_d_skip, ssm_norm_w=m_ssm_norm_w, pool_w=m_pool_w, pool_scale=m_pool_scale,
               w_proj_ssm=m_w_proj_ssm, w_proj_pool=m_w_proj_pool, w_proj_sb=m_w_proj_sb, w_out=m_w_out,
               final_norm_w=m_final_norm_w)
    var = dict(norm_w=v_norm_w, w_in=v_w_in, conv_w=v_conv_w, conv_b=v_conv_b, dt_bias=v_dt_bias, a_log=v_a_log,
               d_skip=v_d_skip, ssm_norm_w=v_ssm_norm_w, pool_w=v_pool_w, pool_scale=v_pool_scale,
               w_proj_ssm=v_w_proj_ssm, w_proj_pool=v_w_proj_pool, w_proj_sb=v_w_proj_sb, w_out=v_w_out,
               final_norm_w=v_final_norm_w)
    bl, s, _ = x.shape
    t = bl * s
    me = _flat(_coords())

    cw = exchange(conv_w.reshape(40, 128), "direct", "gather_conv_w")
    conv_w_full = cw.reshape(N_DEV, DEPTH, CONV_K, CONV_CH // N_DEV).transpose(1, 2, 0, 3).reshape(
        DEPTH, CONV_K, CONV_CH)

    xc = x.reshape(t, D)
    layer_w, saved = [], []
    packed = [pack_shards({n: (wts[n][l].T if n == "w_in" else wts[n][l]).astype(BF16) for n in BIG})
              for l in range(DEPTH)]
    gathered = exchange(packed[0], "gather", "gather_w0")
    for l in range(DEPTH):
        lw = unpack_gathered(gathered)
        for n in REPLICATED:
            lw[n] = wts[n][l]
        lw["conv_w"] = conv_w_full[l]
        xc, sv, gathered = layer_fwd(xc, lw, bl, s, f"_l{l}", (packed[l + 1], "gather") if l + 1 < DEPTH else None)
        layer_w.append(lw)
        saved.append(sv)

    loss_part, dx, dx_b, dfinal = final_loss(xc, final_norm_w, loss_target.reshape(t, D), "final_loss")
    loss = lax.psum(loss_part[0, 0], ("x", "y", "c"))

    grads = [None] * DEPTH
    big_sum = [None] * DEPTH
    chip_sums = None
    for l in reversed(range(DEPTH)):
        dx, dx_b, g, got = layer_bwd(dx, dx_b, layer_w[l], saved[l], bl, s, f"_l{l}",
                                     (chip_sums, "chips") if chip_sums is not None else None)
        if got is not None:
            big_sum[l + 1] = unpack_shards(sum_slabs(got, f"sum_g{l + 1}"))
        grads[l] = g
        s0, s1 = pack_slabs(g)
        chip_sums = pair_sum(s0, s1, pair_swap(s0, s1, f"pair_swap{l}"), f"pair_sum{l}")
    big_sum[0] = unpack_shards(sum_slabs(exchange(chip_sums, "chips", "scatter_g0"), "sum_g0"))
    grad_x = dx.reshape(bl, s, D)

    small_names = REPLICATED + ("conv_w",)
    small_vals = [jnp.stack([grads[l][n] for l in range(DEPTH)]) for n in small_names] + [dfinal[0]]
    small_shapes = [v.shape for v in small_vals]
    small_all = exchange(pack_small(small_vals), "direct", "gather_small")
    small_sum = unpack_small(sum_slabs(small_all, "sum_small"), small_shapes)
    gsum = dict(zip(small_names + ("final_norm_w",), small_sum))
    conv_g_full = gsum["conv_w"]
    gsum["conv_w"] = lax.dynamic_slice_in_dim(conv_g_full, me * (CONV_CH // N_DEV), CONV_CH // N_DEV, axis=2)
    for n in BIG:
        gsum[n] = jnp.stack([big_sum[l][n] for l in range(DEPTH)])

    delta, new_m, new_v = {}, {}, {}
    for n in BIG + ("conv_w",):
        shp = wts[n].shape
        two_d = (-1, shp[-1])
        d2, m2, v2 = adamw(wts[n].reshape(two_d), gsum[n].reshape(two_d), mom[n].reshape(two_d),
                           var[n].reshape(two_d), f"adamw_{n}")
        delta[n], new_m[n], new_v[n] = d2.reshape(shp), m2.reshape(shp), v2.reshape(shp)
    rep = REPLICATED + ("final_norm_w",)
    rep_shapes = [wts[n].shape for n in rep]
    d2, m2, v2 = adamw(pack_small([wts[n] for n in rep]), pack_small([gsum[n] for n in rep]),
                       pack_small([mom[n] for n in rep]), pack_small([var[n] for n in rep]), "adamw_small")
    for n, dv, mv, vv in zip(rep, unpack_small(d2, rep_shapes), unpack_small(m2, rep_shapes),
                             unpack_small(v2, rep_shapes)):
        delta[n], new_m[n], new_v[n] = dv, mv, vv

    return (loss, grad_x, *[gsum[n] for n in WEIGHTS], *[delta[n] for n in WEIGHTS],
            *[new_m[n] for n in WEIGHTS], *[new_v[n] for n in WEIGHTS])
```

```python
import functools

import jax
import jax.numpy as jnp
from jax import lax
from jax.experimental import pallas as pl
from jax.experimental.pallas import tpu as pltpu

F32 = jnp.float32
BF16 = jnp.bfloat16

N_DEV = 8
DEPTH = 4
D = 1024
SSM_W = 2048
N_HEADS = 32
N_PAIRS = 16
N_GROUPS = 2
N_STATE = 128
CHUNK = 128
CONV_CH = 2560
CONV_K = 4
POOL_W = 1024
POOL_G = 4
POOL_GD = 256
SB_W = 1024
SB_PAIRS = 8
QB = 256
EPS = 1e-6
IN_COLS = 13856

PC = 14336
OFF_MERGE = 0
OFF_SBG = 3072
OFF_PU = 4096
OFF_PG = 5120
OFF_Z = 6144
OFF_QKV = 8192
OFF_XBC = 11264
OFF_DT = 13824

ADAM_LR = 0.001
ADAM_B1 = 0.9
ADAM_B2 = 0.999
ADAM_EPS = 1e-08
ADAM_WD = 0.01
ADAM_STEP = 10

VMEM_LIMIT = 56 * 1024 * 1024

_NN = (((1,), (0,)), ((), ()))
_NT = (((1,), (1,)), ((), ()))
_TN = (((0,), (0,)), ((), ()))


def _dot(a, b, dn=_NN):
    return lax.dot_general(a, b, dn, preferred_element_type=F32)


def _sigmoid(x):
    return 1.0 / (1.0 + jnp.exp(-x))


def _softplus(x):
    return jnp.maximum(x, 0.0) + jnp.log(1.0 + jnp.exp(-jnp.abs(x)))


def _split2(x):
    hi = x.astype(BF16)
    lo = (x - hi.astype(F32)).astype(BF16)
    return hi, lo


def _split3(x):
    hi = x.astype(BF16)
    r = x - hi.astype(F32)
    mid = r.astype(BF16)
    lo = (r - mid.astype(F32)).astype(BF16)
    return hi, mid, lo


def _params(*sem):
    return pltpu.CompilerParams(dimension_semantics=sem, vmem_limit_bytes=VMEM_LIMIT)


def matmul(a, b, mode, out_dtype, name, residual=None, tm=1024, tn=1024, tk=1024):
    if mode == "nn":
        (m, k), (k2, n) = a.shape, b.shape
    elif mode == "nt":
        (m, k), (n, k2) = a.shape, b.shape
    else:
        (k, m), (k2, n) = a.shape, b.shape
    assert k == k2
    tm, tn, tk = min(tm, m), min(tn, n), min(tk, k)
    assert m % tm == 0 and n % tn == 0 and k % tk == 0
    nk = k // tk
    dn = {"nn": _NN, "nt": _NT, "tn": _TN}[mode]
    a_spec = pl.BlockSpec((tk, tm), lambda i, j, kk: (kk, i)) if mode == "tn" else pl.BlockSpec((tm, tk), lambda i, j, kk: (i, kk))
    b_spec = pl.BlockSpec((tn, tk), lambda i, j, kk: (j, kk)) if mode == "nt" else pl.BlockSpec((tk, tn), lambda i, j, kk: (kk, j))
    in_specs = [a_spec, b_spec]
    args = [a, b]
    if residual is not None:
        in_specs.append(pl.BlockSpec((tm, tn), lambda i, j, kk: (i, j)))
        args.append(residual)

    def body(*refs):
        if residual is not None:
            a_ref, b_ref, r_ref, o_ref, acc_ref = refs
        else:
            a_ref, b_ref, o_ref, acc_ref = refs
            r_ref = None
        kk = pl.program_id(2)
        p = _dot(a_ref[...], b_ref[...], dn)

        def finish(val):
            if r_ref is not None:
                val = val + r_ref[...]
            o_ref[...] = val.astype(out_dtype)

        if nk == 1:
            finish(p)
        else:
            @pl.when(kk == 0)
            def _():
                acc_ref[...] = p

            @pl.when(kk > 0)
            def _():
                acc_ref[...] += p

            @pl.when(kk == nk - 1)
            def _():
                finish(acc_ref[...])

    return pl.pallas_call(
        body, name=name,
        grid=(m // tm, n // tn, nk),
        in_specs=in_specs,
        out_specs=pl.BlockSpec((tm, tn), lambda i, j, kk: (i, j)),
        out_shape=jax.ShapeDtypeStruct((m, n), out_dtype),
        scratch_shapes=[pltpu.VMEM((tm, tn) if nk > 1 else (8, 128), F32)],
        compiler_params=_params("parallel", "parallel", "arbitrary"),
    )(*args)


def rmsnorm_fwd(x, w, name):
    t, d = x.shape
    tr = min(512, t)

    def body(x_ref, w_ref, h_ref):
        xv = x_ref[...]
        r = lax.rsqrt(jnp.mean(xv * xv, axis=-1, keepdims=True) + EPS)
        h_ref[...] = (xv * r * w_ref[...]).astype(BF16)

    return pl.pallas_call(
        body, name=name, grid=(t // tr,),
        in_specs=[pl.BlockSpec((tr, d), lambda i: (i, 0)), pl.BlockSpec((1, d), lambda i: (0, 0))],
        out_specs=pl.BlockSpec((tr, d), lambda i: (i, 0)),
        out_shape=jax.ShapeDtypeStruct((t, d), BF16),
        compiler_params=_params("parallel"),
    )(x, w.reshape(1, d))


def rmsnorm_bwd(dh, x, w, dres, name):
    t, d = x.shape
    tr = min(512, t)

    def body(dh_ref, x_ref, w_ref, dres_ref, dx_ref, dxb_ref, dw_ref):
        xv = x_ref[...]
        r = lax.rsqrt(jnp.mean(xv * xv, axis=-1, keepdims=True) + EPS)
        xh = xv * r
        g = dh_ref[...].astype(F32)
        dxh = g * w_ref[...]
        dxv = dres_ref[...] + r * (dxh - xh * jnp.mean(dxh * xh, axis=-1, keepdims=True))
        dx_ref[...] = dxv
        dxb_ref[...] = dxv.astype(BF16)
        part = jnp.sum(g * xh, axis=0, keepdims=True)

        @pl.when(pl.program_id(0) == 0)
        def _():
            dw_ref[...] = part

        @pl.when(pl.program_id(0) > 0)
        def _():
            dw_ref[...] += part

    return pl.pallas_call(
        body, name=name, grid=(t // tr,),
        in_specs=[pl.BlockSpec((tr, d), lambda i: (i, 0)), pl.BlockSpec((tr, d), lambda i: (i, 0)),
                  pl.BlockSpec((1, d), lambda i: (0, 0)), pl.BlockSpec((tr, d), lambda i: (i, 0))],
        out_specs=[pl.BlockSpec((tr, d), lambda i: (i, 0)), pl.BlockSpec((tr, d), lambda i: (i, 0)),
                   pl.BlockSpec((1, d), lambda i: (0, 0))],
        out_shape=[jax.ShapeDtypeStruct((t, d), F32), jax.ShapeDtypeStruct((t, d), BF16),
                   jax.ShapeDtypeStruct((1, d), F32)],
        compiler_params=_params("arbitrary"),
    )(dh, x, w.reshape(1, d), dres)


def final_loss(x, w, target, name):
    t, d = x.shape
    tr = min(512, t)

    def body(x_ref, w_ref, tg_ref, loss_ref, dx_ref, dxb_ref, dw_ref):
        xv = x_ref[...]
        r = lax.rsqrt(jnp.mean(xv * xv, axis=-1, keepdims=True) + EPS)
        xh = xv * r
        err = xh * w_ref[...] - tg_ref[...]
        lpart = 0.5 * jnp.sum(jnp.mean(err * err, axis=-1, keepdims=True), axis=0, keepdims=True)
        dy = err * (1.0 / d)
        dxh = dy * w_ref[...]
        dxv = r * (dxh - xh * jnp.mean(dxh * xh, axis=-1, keepdims=True))
        dx_ref[...] = dxv
        dxb_ref[...] = dxv.astype(BF16)
        part = jnp.sum(dy * xh, axis=0, keepdims=True)

        @pl.when(pl.program_id(0) == 0)
        def _():
            dw_ref[...] = part
            loss_ref[...] = jnp.broadcast_to(lpart, (1, 128))

        @pl.when(pl.program_id(0) > 0)
        def _():
            dw_ref[...] += part
            loss_ref[...] += jnp.broadcast_to(lpart, (1, 128))

    return pl.pallas_call(
        body, name=name, grid=(t // tr,),
        in_specs=[pl.BlockSpec((tr, d), lambda i: (i, 0)), pl.BlockSpec((1, d), lambda i: (0, 0)),
                  pl.BlockSpec((tr, d), lambda i: (i, 0))],
        out_specs=[pl.BlockSpec((1, 128), lambda i: (0, 0)), pl.BlockSpec((tr, d), lambda i: (i, 0)),
                   pl.BlockSpec((tr, d), lambda i: (i, 0)), pl.BlockSpec((1, d), lambda i: (0, 0))],
        out_shape=[jax.ShapeDtypeStruct((1, 128), F32), jax.ShapeDtypeStruct((t, d), F32),
                   jax.ShapeDtypeStruct((t, d), BF16), jax.ShapeDtypeStruct((1, d), F32)],
        compiler_params=_params("arbitrary"),
    )(x, w.reshape(1, d), target)


CONV_BW = 256


def _shift_down(u, s, row):
    return jnp.where(row >= s, pltpu.roll(u, s, axis=0), 0.0)


def _shift_up(u, s, row, n):
    return jnp.where(row < n - s, pltpu.roll(u, n - s, axis=0), 0.0)


def _conv_pre(u, w, b, row):
    acc = b + w[CONV_K - 1:CONV_K, :] * u
    for k in range(CONV_K - 1):
        acc = acc + w[k:k + 1, :] * _shift_down(u, CONV_K - 1 - k, row)
    return acc


def conv_fwd(proj, conv_w, conv_b, bl, s, name):
    t = bl * s
    nb = CONV_CH // CONV_BW
    off = OFF_XBC // CONV_BW

    def body(u_ref, w_ref, b_ref, o_ref):
        u = u_ref[...].astype(F32)
        row = lax.broadcasted_iota(jnp.int32, u.shape, 0)
        xc = _conv_pre(u, w_ref[...], b_ref[...], row)
        o_ref[...] = (xc * _sigmoid(xc)).astype(BF16)

    return pl.pallas_call(
        body, name=name, grid=(bl, nb),
        in_specs=[pl.BlockSpec((s, CONV_BW), lambda b, j: (b, off + j)),
                  pl.BlockSpec((CONV_K, CONV_BW), lambda b, j: (0, j)),
                  pl.BlockSpec((1, CONV_BW), lambda b, j: (0, j))],
        out_specs=pl.BlockSpec((s, CONV_BW), lambda b, j: (b, j)),
        out_shape=jax.ShapeDtypeStruct((t, CONV_CH), BF16),
        compiler_params=_params("parallel", "parallel"),
    )(proj, conv_w, conv_b.reshape(1, CONV_CH))


def conv_bwd(dxa, proj, conv_w, conv_b, bl, s, name):
    t = bl * s
    nb = CONV_CH // CONV_BW
    off = OFF_XBC // CONV_BW

    def body(d_ref, u_ref, w_ref, b_ref, du_ref, dw_ref, db_ref):
        u = u_ref[...].astype(F32)
        w = w_ref[...]
        row = lax.broadcasted_iota(jnp.int32, u.shape, 0)
        xc = _conv_pre(u, w, b_ref[...], row)
        sg = _sigmoid(xc)
        dxc = d_ref[...].astype(F32) * sg * (1.0 + xc * (1.0 - sg))
        du = w[CONV_K - 1:CONV_K, :] * dxc
        dws = [None] * CONV_K
        dws[CONV_K - 1] = jnp.sum(dxc * u, axis=0, keepdims=True)
        for k in range(CONV_K - 1):
            sh = CONV_K - 1 - k
            du = du + w[k:k + 1, :] * _shift_up(dxc, sh, row, s)
            dws[k] = jnp.sum(dxc * _shift_down(u, sh, row), axis=0, keepdims=True)
        du_ref[...] = du.astype(BF16)
        krow = lax.broadcasted_iota(jnp.int32, (8, CONV_BW), 0)
        dwv = sum(jnp.where(krow == k, dws[k], 0.0) for k in range(CONV_K))
        dbv = jnp.sum(dxc, axis=0, keepdims=True)

        @pl.when(pl.program_id(1) == 0)
        def _():
            dw_ref[...] = dwv
            db_ref[...] = dbv

        @pl.when(pl.program_id(1) > 0)
        def _():
            dw_ref[...] += dwv
            db_ref[...] += dbv

    du, dw, db = pl.pallas_call(
        body, name=name, grid=(nb, bl),
        in_specs=[pl.BlockSpec((s, CONV_BW), lambda j, b: (b, j)),
                  pl.BlockSpec((s, CONV_BW), lambda j, b: (b, off + j)),
                  pl.BlockSpec((CONV_K, CONV_BW), lambda j, b: (0, j)),
                  pl.BlockSpec((1, CONV_BW), lambda j, b: (0, j))],
        out_specs=[pl.BlockSpec((s, CONV_BW), lambda j, b: (b, j)),
                   pl.BlockSpec((8, CONV_BW), lambda j, b: (0, j)),
                   pl.BlockSpec((1, CONV_BW), lambda j, b: (0, j))],
        out_shape=[jax.ShapeDtypeStruct((t, CONV_CH), BF16), jax.ShapeDtypeStruct((8, CONV_CH), F32),
                   jax.ShapeDtypeStruct((1, CONV_CH), F32)],
        compiler_params=_params("parallel", "arbitrary"),
    )(dxa, proj, conv_w, conv_b.reshape(1, CONV_CH))
    return du, dw[:CONV_K], db[0]


def _tri(shape, cmp):
    r = lax.broadcasted_iota(jnp.int32, shape, 0)
    c = lax.broadcasted_iota(jnp.int32, shape, 1)
    return cmp(r, c)


def dt_fwd(proj, dt_bias, a_log, t, name):
    nchunks = t // CHUNK
    bias = jnp.zeros((1, 128), F32).at[0, :N_HEADS].set(dt_bias)
    alog = jnp.zeros((1, 128), F32).at[0, :N_HEADS].set(a_log)

    def body(raw_ref, b_ref, al_ref, dt_ref, ac_ref, dtl_ref, acl_ref):
        raw = raw_ref[...].astype(F32)
        dt = _softplus(raw + b_ref[...])
        adt = dt * (-jnp.exp(al_ref[...]))
        low = _tri((CHUNK, CHUNK), lambda r, c: r >= c).astype(BF16)
        acum = sum(_dot(low, part) for part in _split3(adt))
        dt_ref[...] = dt.T[:N_HEADS]
        ac_ref[...] = acum.T[:N_HEADS]
        spread = _tri((128, SSM_W), lambda h, lane: lane // 64 == h).astype(BF16)
        dtl_ref[...] = sum(_dot(part, spread) for part in _split3(dt))
        acl_ref[...] = sum(_dot(part, spread) for part in _split3(acum))

    return pl.pallas_call(
        body, name=name, grid=(nchunks,),
        in_specs=[pl.BlockSpec((CHUNK, 128), lambda i: (i, OFF_DT // 128)),
                  pl.BlockSpec((1, 128), lambda i: (0, 0)), pl.BlockSpec((1, 128), lambda i: (0, 0))],
        out_specs=[pl.BlockSpec((None, N_HEADS, CHUNK), lambda i: (i, 0, 0))] * 2
        + [pl.BlockSpec((CHUNK, SSM_W), lambda i: (i, 0))] * 2,
        out_shape=[jax.ShapeDtypeStruct((nchunks, N_HEADS, CHUNK), F32)] * 2
        + [jax.ShapeDtypeStruct((t, SSM_W), F32)] * 2,
        compiler_params=_params("parallel"),
    )(proj, bias, alog)


def dt_bwd(ddtT, dacT, dtT, proj, dt_bias, a_log, t, name):
    nchunks = t // CHUNK
    bias = dt_bias.reshape(N_HEADS, 1)
    alog = a_log.reshape(N_HEADS, 1)

    def body(ddt_ref, dac_ref, dt_ref, raw_ref, b_ref, al_ref, draw_ref, da_ref, db_ref):
        a = -jnp.exp(al_ref[...])
        upp = _tri((CHUNK, CHUNK), lambda r, c: r >= c).astype(BF16)
        dadt = sum(_dot(part, upp) for part in _split3(dac_ref[...]))
        ddt = ddt_ref[...] + dadt * a
        rawT = raw_ref[...].astype(F32).T[:N_HEADS]
        draw = ddt * _sigmoid(rawT + b_ref[...])
        padded = jnp.concatenate([draw, jnp.zeros((128 - N_HEADS, CHUNK), F32)], axis=0)
        draw_ref[...] = padded.T.astype(BF16)
        dav = dadt * dt_ref[...]

        @pl.when(pl.program_id(0) == 0)
        def _():
            da_ref[...] = dav
            db_ref[...] = draw

        @pl.when(pl.program_id(0) > 0)
        def _():
            da_ref[...] += dav
            db_ref[...] += draw

    draw, da, db = pl.pallas_call(
        body, name=name, grid=(nchunks,),
        in_specs=[pl.BlockSpec((None, N_HEADS, CHUNK), lambda i: (i, 0, 0))] * 3
        + [pl.BlockSpec((CHUNK, 128), lambda i: (i, OFF_DT // 128)),
           pl.BlockSpec((N_HEADS, 1), lambda i: (0, 0)), pl.BlockSpec((N_HEADS, 1), lambda i: (0, 0))],
        out_specs=[pl.BlockSpec((CHUNK, 128), lambda i: (i, 0)),
                   pl.BlockSpec((N_HEADS, CHUNK), lambda i: (0, 0)), pl.BlockSpec((N_HEADS, CHUNK), lambda i: (0, 0))],
        out_shape=[jax.ShapeDtypeStruct((t, 128), BF16), jax.ShapeDtypeStruct((N_HEADS, CHUNK), F32),
                   jax.ShapeDtypeStruct((N_HEADS, CHUNK), F32)],
        compiler_params=_params("arbitrary"),
    )(ddtT, dacT, dtT, proj, bias, alog)
    return draw, jnp.sum(da, axis=1), jnp.sum(db, axis=1)


PAIRS_G = N_PAIRS // N_GROUPS
GROUP_W = PAIRS_G * 128


def _ssd_pair(x, dtl, acl, acr, tri):
    left = lax.broadcasted_iota(jnp.int32, (CHUNK, 128), 1) < 64
    swapped = pltpu.roll(acl, 64, axis=1)
    ac_cols = [jnp.where(left, acl, swapped), jnp.where(left, swapped, acl)]
    dks = [jnp.exp(jnp.where(tri, ac_cols[e] - acr[e:e + 1], -1e30)) for e in range(2)]
    aclast = acl[CHUNK - 1:CHUNK, :]
    return left, dtl, acl, x * dtl, dks, aclast


def _ssd_specs(nc, rev):
    row = (lambda b, c, g: b * nc + (nc - 1 - c)) if rev else (lambda b, c, g: b * nc + c)
    return dict(
        wide=pl.BlockSpec((CHUNK, GROUP_W), lambda b, c, g: (row(b, c, g), g)),
        bmat=pl.BlockSpec((CHUNK, 128), lambda b, c, g: (row(b, c, g), SSM_W // 128 + g)),
        cmat=pl.BlockSpec((CHUNK, 128), lambda b, c, g: (row(b, c, g), SSM_W // 128 + N_GROUPS + g)),
        rows2=pl.BlockSpec((None, PAIRS_G, 2, CHUNK), lambda b, c, g: (row(b, c, g), g, 0, 0)),
        cols=pl.BlockSpec((CHUNK, GROUP_W), lambda b, c, g: (row(b, c, g), g)),
        rows8=pl.BlockSpec((None, PAIRS_G, 8, CHUNK), lambda b, c, g: (row(b, c, g), g, 0, 0)),
        dskip=pl.BlockSpec((1, GROUP_W), lambda b, c, g: (0, g)),
        state=pl.BlockSpec((None, PAIRS_G, N_STATE, 128), lambda b, c, g: (row(b, c, g), g, 0, 0)),
        narrow=pl.BlockSpec((CHUNK, 128), lambda b, c, g: (row(b, c, g), g)))


def ssd_fwd(xa, acT, dtC, acC, dskip_l, bl, s, name):
    t = bl * s
    nc = s // CHUNK
    ac4 = acT.reshape(bl * nc, N_PAIRS, 2, CHUNK)

    def body(x_ref, b_ref, c_ref, ac_ref, dtc_ref, acc_ref, dsk_ref, y_ref, prev_ref, st_ref):
        c = pl.program_id(1)
        g = pl.program_id(2)
        bm = b_ref[...]
        cm = c_ref[...]
        cb = _dot(cm, bm, _NT)
        tri = _tri((CHUNK, CHUNK), lambda r, c: r >= c)

        @pl.when(c == 0)
        def _():
            for p in range(PAIRS_G):
                st_ref[g * PAIRS_G + p] = jnp.zeros((N_STATE, 128), F32)

        for p in range(PAIRS_G):
            hp = g * PAIRS_G + p
            cs = slice(p * 128, (p + 1) * 128)
            x = x_ref[:, cs].astype(F32)
            left, dtl, acl, xdt, dks, aclast = _ssd_pair(x, dtc_ref[:, cs], acc_ref[:, cs], ac_ref[p], tri)
            xdt_b = xdt.astype(BF16)
            ys = [_dot((cb * dks[e]).astype(BF16), xdt_b) for e in range(2)]
            st = st_ref[hp]
            y_off = _dot(cm, st.astype(BF16)) * jnp.exp(acl)
            y_ref[:, cs] = (jnp.where(left, ys[0], ys[1]) + y_off + x * dsk_ref[:, cs]).astype(BF16)
            xw = (xdt * jnp.exp(aclast - acl)).astype(BF16)
            prev_ref[p] = st
            st_ref[hp] = st * jnp.exp(aclast) + _dot(bm, xw, _TN)

    sp = _ssd_specs(nc, False)
    return pl.pallas_call(
        body, name=name, grid=(bl, nc, N_GROUPS),
        in_specs=[sp["wide"], sp["bmat"], sp["cmat"], sp["rows2"], sp["cols"], sp["cols"], sp["dskip"]],
        out_specs=[sp["wide"], sp["state"]],
        out_shape=[jax.ShapeDtypeStruct((t, SSM_W), BF16),
                   jax.ShapeDtypeStruct((bl * nc, N_PAIRS, N_STATE, 128), F32)],
        scratch_shapes=[pltpu.VMEM((N_PAIRS, N_STATE, 128), F32)],
        compiler_params=_params("parallel", "arbitrary", "arbitrary"),
    )(xa, xa, xa, ac4, dtC, acC, dskip_l)


def ssd_bwd(dy, xa, acT, dtC, acC, dskip_l, prev, bl, s, name):
    t = bl * s
    nc = s // CHUNK
    ac4 = acT.reshape(bl * nc, N_PAIRS, 2, CHUNK)

    def body(dy_ref, x_ref, b_ref, c_ref, ac_ref, dtc_ref, acc_ref, dsk_ref, prev_ref,
             dx_ref, db_ref, dc_ref, dd_ref, dsk_out_ref, dp_ref):
        b = pl.program_id(0)
        cr = pl.program_id(1)
        g = pl.program_id(2)

        @pl.when(cr == 0)
        def _():
            for p in range(PAIRS_G):
                dp_ref[g * PAIRS_G + p] = jnp.zeros((N_STATE, 128), F32)

        @pl.when((b == 0) & (cr == 0) & (g == 0))
        def _():
            dsk_out_ref[...] = jnp.zeros(dsk_out_ref.shape, F32)

        bm = b_ref[...]
        cm = c_ref[...]
        cb = _dot(cm, bm, _NT)
        tri = _tri((CHUNK, CHUNK), lambda r, c: r >= c)
        lane = lax.broadcasted_iota(jnp.int32, (CHUNK, 128), 1)
        lrow = lax.broadcasted_iota(jnp.int32, (1, CHUNK), 1)
        krow = lax.broadcasted_iota(jnp.int32, (8, CHUNK), 0)
        dcb = jnp.zeros((CHUNK, CHUNK), F32)
        dc_acc = jnp.zeros((CHUNK, N_STATE), F32)
        db_acc = jnp.zeros((CHUNK, N_STATE), F32)
        for p in range(PAIRS_G):
            hp = g * PAIRS_G + p
            cs = slice(p * 128, (p + 1) * 128)
            x = x_ref[:, cs].astype(F32)
            left, dtl, acl, xdt, dks, aclast = _ssd_pair(x, dtc_ref[:, cs], acc_ref[:, cs], ac_ref[p], tri)
            dyv = dy_ref[:, cs].astype(F32)
            dy_b = dyv.astype(BF16)
            xdt_b = xdt.astype(BF16)
            st = prev_ref[p]
            st_b = st.astype(BF16)
            ea = jnp.exp(acl)
            ds = jnp.exp(aclast - acl)
            cdl = jnp.exp(aclast)
            xw = xdt * ds
            masks = [left, jnp.logical_not(left)]

            dsk_out_ref[hp] = dsk_out_ref[hp] + jnp.sum(dyv * x, axis=0, keepdims=True)

            yo = _dot(cm, st_b)
            dyo_b = (dyv * ea).astype(BF16)
            yoff_term = dyv * yo * ea
            dc_acc = dc_acc + _dot(dyo_b, st_b, _NT)
            dst = _dot(cm, dyo_b, _TN)
            dsv = dp_ref[hp]
            dsv_b = dsv.astype(BF16)
            dxw = _dot(bm, dsv_b)
            db_acc = db_acc + _dot(xw.astype(BF16), dsv_b, _NT)
            dxdt = dxw * ds
            qv = dxw * xw
            end_term = dsv * st * cdl
            dp_ref[hp] = dsv * cdl + dst

            cols = jnp.zeros((CHUNK, 128), F32)
            rows = []
            for e in range(2):
                m = cb * dks[e]
                dy_e = jnp.where(masks[e], dyv, 0.0).astype(BF16)
                dm = _dot(dy_e, xdt_b, _NT)
                w = dm * m
                dcb = dcb + dm * dks[e]
                dxdt = dxdt + jnp.where(masks[e], _dot(m.astype(BF16), dy_b, _TN), 0.0)
                dac_col = jnp.sum(w + jnp.where(masks[e], yoff_term - qv, 0.0), axis=1, keepdims=True)
                cols = jnp.where(lane == 2 + e, dac_col, cols)
                tail = jnp.sum(jnp.where(masks[e], qv + end_term, 0.0))
                rows.append(jnp.where(lrow == CHUNK - 1, tail, 0.0) - jnp.sum(w, axis=0, keepdims=True))
            dx_ref[:, cs] = (dxdt * dtl + dyv * dsk_ref[:, cs]).astype(BF16)
            ddt_l = dxdt * x
            for e in range(2):
                cols = jnp.where(lane == e, jnp.sum(jnp.where(masks[e], ddt_l, 0.0), axis=1, keepdims=True), cols)
            dd_ref[p] = cols.T[0:8] + jnp.where(krow == 2, rows[0], 0.0) + jnp.where(krow == 3, rows[1], 0.0)
        dcb_b = dcb.astype(BF16)
        dc_ref[...] = dc_acc + _dot(dcb_b, bm)
        db_ref[...] = db_acc + _dot(dcb_b, cm, _TN)

    sp = _ssd_specs(nc, True)
    dx, db, dc, dd, dsk = pl.pallas_call(
        body, name=name, grid=(bl, nc, N_GROUPS),
        in_specs=[sp["wide"], sp["wide"], sp["bmat"], sp["cmat"], sp["rows2"], sp["cols"], sp["cols"], sp["dskip"],
                  sp["state"]],
        out_specs=[sp["wide"], sp["narrow"], sp["narrow"], sp["rows8"],
                   pl.BlockSpec((N_PAIRS, 1, 128), lambda b, c, g: (0, 0, 0))],
        out_shape=[jax.ShapeDtypeStruct((t, SSM_W), BF16),
                   jax.ShapeDtypeStruct((t, N_GROUPS * N_STATE), F32),
                   jax.ShapeDtypeStruct((t, N_GROUPS * N_STATE), F32),
                   jax.ShapeDtypeStruct((bl * nc, N_PAIRS, 8, CHUNK), F32),
                   jax.ShapeDtypeStruct((N_PAIRS, 1, 128), F32)],
        scratch_shapes=[pltpu.VMEM((N_PAIRS, N_STATE, 128), F32)],
        compiler_params=_params("arbitrary", "arbitrary", "arbitrary"),
    )(dy, xa, xa, xa, ac4, dtC, acC, dskip_l, prev)
    ddtT = dd[:, :, 0:2, :].reshape(bl * nc, N_HEADS, CHUNK)
    dacT = dd[:, :, 2:4, :].reshape(bl * nc, N_HEADS, CHUNK)
    return dx, db, dc, ddtT, dacT, dsk.reshape(N_PAIRS, 128)


def gnorm_fwd(y, proj, w, name):
    t = y.shape[0]
    tr = min(256, t)
    zb = OFF_Z // SSM_W

    def body(y_ref, z_ref, w_ref, o_ref):
        z = z_ref[...].astype(F32)
        yg = y_ref[...].astype(F32) * z * _sigmoid(z)
        r = lax.rsqrt(jnp.mean(yg * yg, axis=-1, keepdims=True) + EPS)
        o_ref[...] = (yg * r * w_ref[...]).astype(BF16)

    return pl.pallas_call(
        body, name=name, grid=(t // tr,),
        in_specs=[pl.BlockSpec((tr, SSM_W), lambda i: (i, 0)), pl.BlockSpec((tr, SSM_W), lambda i: (i, zb)),
                  pl.BlockSpec((1, SSM_W), lambda i: (0, 0))],
        out_specs=pl.BlockSpec((tr, SSM_W), lambda i: (i, 0)),
        out_shape=jax.ShapeDtypeStruct((t, SSM_W), BF16),
        compiler_params=_params("parallel"),
    )(y, proj, w.reshape(1, SSM_W))


def gnorm_bwd(ds, y, proj, w, name):
    t = y.shape[0]
    tr = min(256, t)
    zb = OFF_Z // SSM_W

    def body(ds_ref, y_ref, z_ref, w_ref, dy_ref, dz_ref, dw_ref):
        z = z_ref[...].astype(F32)
        yv = y_ref[...].astype(F32)
        sg = _sigmoid(z)
        sz = z * sg
        yg = yv * sz
        r = lax.rsqrt(jnp.mean(yg * yg, axis=-1, keepdims=True) + EPS)
        xh = yg * r
        g = ds_ref[...].astype(F32)
        dxh = g * w_ref[...]
        dyg = r * (dxh - xh * jnp.mean(dxh * xh, axis=-1, keepdims=True))
        dy_ref[...] = (dyg * sz).astype(BF16)
        dz_ref[...] = (dyg * yv * sg * (1.0 + z * (1.0 - sg))).astype(BF16)
        part = jnp.sum(g * xh, axis=0, keepdims=True)

        @pl.when(pl.program_id(0) == 0)
        def _():
            dw_ref[...] = part

        @pl.when(pl.program_id(0) > 0)
        def _():
            dw_ref[...] += part

    return pl.pallas_call(
        body, name=name, grid=(t // tr,),
        in_specs=[pl.BlockSpec((tr, SSM_W), lambda i: (i, 0)), pl.BlockSpec((tr, SSM_W), lambda i: (i, 0)),
                  pl.BlockSpec((tr, SSM_W), lambda i: (i, zb)), pl.BlockSpec((1, SSM_W), lambda i: (0, 0))],
        out_specs=[pl.BlockSpec((tr, SSM_W), lambda i: (i, 0)), pl.BlockSpec((tr, SSM_W), lambda i: (i, 0)),
                   pl.BlockSpec((1, SSM_W), lambda i: (0, 0))],
        out_shape=[jax.ShapeDtypeStruct((t, SSM_W), BF16), jax.ShapeDtypeStruct((t, SSM_W), BF16),
                   jax.ShapeDtypeStruct((1, SSM_W), F32)],
        compiler_params=_params("arbitrary"),
    )(ds, y, proj, w.reshape(1, SSM_W))


def _pool_mixed(u, g, row):
    win = 2 << g
    acc = u
    for k in range(g + 1):
        acc = acc + _shift_down(acc, 1 << k, row)
    inv = 1.0 / jnp.minimum(row + 1, win).astype(F32)
    return acc * inv - u, inv


def pool_fwd(proj, pool_w, pool_scale, bl, s, name):
    t = bl * s

    def body(u_ref, g_ref, w_ref, sc_ref, o_ref):
        row = lax.broadcasted_iota(jnp.int32, (s, POOL_GD), 0)
        for g in range(POOL_G):
            cs = slice(g * POOL_GD, (g + 1) * POOL_GD)
            u = u_ref[:, cs].astype(F32)
            mixed, _ = _pool_mixed(u, g, row)
            pm = _dot(mixed.astype(BF16), w_ref[g])
            gate = g_ref[:, cs].astype(F32)
            o_ref[:, cs] = (pm * sc_ref[:, cs] * gate * _sigmoid(gate)).astype(BF16)

    return pl.pallas_call(
        body, name=name, grid=(bl,),
        in_specs=[pl.BlockSpec((s, POOL_W), lambda b: (b, OFF_PU // POOL_W)),
                  pl.BlockSpec((s, POOL_W), lambda b: (b, OFF_PG // POOL_W)),
                  pl.BlockSpec((POOL_G, POOL_GD, POOL_GD), lambda b: (0, 0, 0)),
                  pl.BlockSpec((1, POOL_W), lambda b: (0, 0))],
        out_specs=pl.BlockSpec((s, POOL_W), lambda b: (b, 0)),
        out_shape=jax.ShapeDtypeStruct((t, POOL_W), BF16),
        compiler_params=_params("parallel"),
    )(proj, proj, pool_w, pool_scale.reshape(1, POOL_W))


def pool_bwd(dp, proj, pool_w, pool_scale, bl, s, name):
    t = bl * s

    def body(dp_ref, u_ref, g_ref, w_ref, sc_ref, du_ref, dg_ref, dw_ref, dsc_ref):
        row = lax.broadcasted_iota(jnp.int32, (s, POOL_GD), 0)
        first = pl.program_id(0) == 0
        for g in range(POOL_G):
            cs = slice(g * POOL_GD, (g + 1) * POOL_GD)
            u = u_ref[:, cs].astype(F32)
            mixed, inv = _pool_mixed(u, g, row)
            mixed_b = mixed.astype(BF16)
            wg = w_ref[g]
            pm = _dot(mixed_b, wg)
            gate = g_ref[:, cs].astype(F32)
            sg = _sigmoid(gate)
            d = dp_ref[:, cs].astype(F32)
            sc = sc_ref[:, cs]
            dpm = (d * sc * gate * sg).astype(BF16)
            dg_ref[:, cs] = (d * pm * sc * sg * (1.0 + gate * (1.0 - sg))).astype(BF16)
            dsc = jnp.sum(d * pm * gate * sg, axis=0, keepdims=True)
            dwg = _dot(mixed_b, dpm, _TN)
            dmixed = _dot(dpm, wg, _NT)
            acc = dmixed * inv
            for k in range(g + 1):
                acc = acc + _shift_up(acc, 1 << k, row, s)
            du_ref[:, cs] = (acc - dmixed).astype(BF16)

            @pl.when(first)
            def _():
                dw_ref[g] = dwg
                dsc_ref[:, cs] = dsc

            @pl.when(jnp.logical_not(first))
            def _():
                dw_ref[g] = dw_ref[g] + dwg
                dsc_ref[:, cs] = dsc_ref[:, cs] + dsc

    return pl.pallas_call(
        body, name=name, grid=(bl,),
        in_specs=[pl.BlockSpec((s, POOL_W), lambda b: (b, 0)),
                  pl.BlockSpec((s, POOL_W), lambda b: (b, OFF_PU // POOL_W)),
                  pl.BlockSpec((s, POOL_W), lambda b: (b, OFF_PG // POOL_W)),
                  pl.BlockSpec((POOL_G, POOL_GD, POOL_GD), lambda b: (0, 0, 0)),
                  pl.BlockSpec((1, POOL_W), lambda b: (0, 0))],
        out_specs=[pl.BlockSpec((s, POOL_W), lambda b: (b, 0)), pl.BlockSpec((s, POOL_W), lambda b: (b, 0)),
                   pl.BlockSpec((POOL_G, POOL_GD, POOL_GD), lambda b: (0, 0, 0)),
                   pl.BlockSpec((1, POOL_W), lambda b: (0, 0))],
        out_shape=[jax.ShapeDtypeStruct((t, POOL_W), BF16), jax.ShapeDtypeStruct((t, POOL_W), BF16),
                   jax.ShapeDtypeStruct((POOL_G, POOL_GD, POOL_GD), F32), jax.ShapeDtypeStruct((1, POOL_W), F32)],
        compiler_params=_params("arbitrary"),
    )(dp, proj, proj, pool_w, pool_scale.reshape(1, POOL_W))


SB_SCALE = 64 ** -0.5


KB = 256


def _sb_block(qe, kj, mask, rr, upper):
    z = _dot(qe, kj, _NT)
    lb = jnp.minimum(z, 0.0) - jnp.log(1.0 + jnp.exp(-jnp.abs(z)))
    lom = lb - z if mask is None else jnp.where(mask, lb - z, 0.0)
    later = _dot(lom.astype(BF16), upper) + rr
    return lb, lom, later


def _sb_masks(i):
    lane = lax.broadcasted_iota(jnp.int32, (QB, 128), 1)
    row = lax.broadcasted_iota(jnp.int32, (2 * QB, KB), 0) % QB
    col = lax.broadcasted_iota(jnp.int32, (2 * QB, KB), 1)
    causal = lambda jb: col + (jb * KB - i * QB) < row
    return lane, lane < 64, causal


def _stack_heads(x, left):
    zero = jnp.zeros_like(x)
    return jnp.concatenate([jnp.where(left, x, zero), jnp.where(left, zero, x)], axis=0)


SB_GROUP = 4
SB_GW = SB_GROUP * 128


def sb_fwd(proj, bl, s, name, comm=None):
    t = bl * s
    nq = s // QB
    qb0, kb0, vb0, gb0 = OFF_QKV // SB_GW, (OFF_QKV + SB_W) // SB_GW, (OFF_QKV + 2 * SB_W) // SB_GW, OFF_SBG // SB_GW
    grid = (bl, SB_PAIRS // SB_GROUP, nq)
    x_in, x_args, x_out, x_shape, x_scratch, x_start, x_wait = _hosted_exchange(
        comm, grid, relay_at=(bl - 1, SB_PAIRS // SB_GROUP - 1, 0))

    def body(*refs):
        q_ref, k_ref, v_ref, g_ref = refs[:4]
        og_ref, o_ref, r_ref = refs[4 + len(x_in):7 + len(x_in)]
        x_refs = refs[4:4 + len(x_in)] + refs[7 + len(x_in):]
        x_start(x_refs)
        i = pl.program_id(2)
        lane, left, causal = _sb_masks(i)
        upper = _tri((KB, KB), lambda r, c: r > c).astype(BF16)
        cols = [slice(p * 128, (p + 1) * 128) for p in range(SB_GROUP)]
        qcats = [_stack_heads(q_ref[:, cs] * SB_SCALE, left) for cs in cols]
        zero = qcats[0].astype(F32) * 0.0

        def block(jb, carry, diagonal):
            rows = pl.ds(pl.multiple_of(jb * KB, KB), KB)
            mask = causal(jb) if diagonal else None
            out = []
            for p, cs in enumerate(cols):
                acc, rr, rt = carry[p]
                lb, lom, later = _sb_block(qcats[p], k_ref[rows, cs], mask, rr, upper)
                att = jnp.exp(lb + later)
                if diagonal:
                    att = jnp.where(mask, att, 0.0)
                acc = acc + _dot(att.astype(BF16), v_ref[rows, cs])
                rt = jnp.where(lane == jb, rr[:QB], jnp.where(lane == 8 + jb, rr[QB:], rt))
                out.append((acc, rr + jnp.sum(lom, axis=1, keepdims=True), rt))
            return tuple(out)

        carry = block(i, tuple((zero, zero[:, :1], zero[:QB]) for _ in cols), True)
        carry = lax.fori_loop(0, i, lambda jj, c: block(i - 1 - jj, c, False), carry)
        for p, cs in enumerate(cols):
            acc, _, rtile = carry[p]
            o = jnp.where(left, acc[:QB], acc[QB:])
            gate = g_ref[:, cs].astype(F32)
            o_ref[:, cs] = o.astype(BF16)
            og_ref[:, cs] = (o * gate * _sigmoid(gate)).astype(BF16)
            r_ref[p] = rtile
        x_wait(x_refs)

    rowblk = lambda b, g, i: (b * nq + i, g)
    return pl.pallas_call(
        body, name=name, grid=grid,
        in_specs=[pl.BlockSpec((QB, SB_GW), lambda b, g, i: (b * nq + i, qb0 + g)),
                  pl.BlockSpec((s, SB_GW), lambda b, g, i: (b, kb0 + g)),
                  pl.BlockSpec((s, SB_GW), lambda b, g, i: (b, vb0 + g)),
                  pl.BlockSpec((QB, SB_GW), lambda b, g, i: (b * nq + i, gb0 + g))] + x_in,
        out_specs=[pl.BlockSpec((QB, SB_GW), rowblk), pl.BlockSpec((QB, SB_GW), rowblk),
                   pl.BlockSpec((None, SB_GROUP, QB, 128), lambda b, g, i: (b * nq + i, g, 0, 0))] + x_out,
        out_shape=[jax.ShapeDtypeStruct((t, SB_W), BF16), jax.ShapeDtypeStruct((t, SB_W), BF16),
                   jax.ShapeDtypeStruct((bl * nq, SB_PAIRS, QB, 128), F32)] + x_shape,
        scratch_shapes=x_scratch,
        compiler_params=_params("arbitrary", "arbitrary", "arbitrary"),
    )(proj, proj, proj, proj, *x_args)


def sb_bwd(dsb, o, rsave, proj, bl, s, name, comm=None):
    t = bl * s
    nq = s // QB
    qb0, kb0, vb0, gb0 = OFF_QKV // SB_GW, (OFF_QKV + SB_W) // SB_GW, (OFF_QKV + 2 * SB_W) // SB_GW, OFF_SBG // SB_GW
    grid = (bl, SB_PAIRS // SB_GROUP, nq)
    x_in, x_args, x_out, x_shape, x_scratch, x_start, x_wait = _hosted_exchange(
        comm, grid, relay_at=(bl - 1, SB_PAIRS // SB_GROUP - 1, 0))

    def body(*refs):
        n = len(x_in)
        d_ref, o_ref, r_ref, q_ref, k_ref, v_ref, g_ref = refs[:7]
        dq_ref, dk_ref, dv_ref, dg_ref = refs[7 + n:11 + n]
        dk_acc, dv_acc = refs[11 + 2 * n:13 + 2 * n]
        x_refs = refs[7:7 + n] + refs[11 + n:11 + 2 * n] + refs[13 + 2 * n:]
        x_start(x_refs)
        i = pl.program_id(2)

        @pl.when(i == 0)
        def _():
            dk_acc[...] = jnp.zeros(dk_acc.shape, F32)
            dv_acc[...] = jnp.zeros(dv_acc.shape, F32)

        lane, left, causal = _sb_masks(i)
        upper = _tri((KB, KB), lambda r, c: r > c).astype(BF16)
        lower = _tri((KB, KB), lambda r, c: r < c).astype(BF16)
        cols = [slice(p * 128, (p + 1) * 128) for p in range(SB_GROUP)]
        qcats, docats = [], []
        for cs in cols:
            gate = g_ref[:, cs].astype(F32)
            sg = _sigmoid(gate)
            d = d_ref[:, cs].astype(F32)
            dg_ref[:, cs] = (d * o_ref[:, cs].astype(F32) * sg * (1.0 + gate * (1.0 - sg))).astype(BF16)
            docats.append(_stack_heads((d * gate * sg).astype(BF16), left))
            qcats.append(_stack_heads(q_ref[:, cs] * SB_SCALE, left))
        qcat_ts = [qc.astype(F32).T.astype(BF16) for qc in qcats]
        docat_ts = [dc.astype(F32).T.astype(BF16) for dc in docats]
        zero = qcats[0].astype(F32) * 0.0

        def block(jb, carry, diagonal):
            rows = pl.ds(pl.multiple_of(jb * KB, KB), KB)
            mask = causal(jb) if diagonal else None
            out = []
            for p, cs in enumerate(cols):
                dq, gcar = carry[p]
                kj = k_ref[rows, cs]
                vj = v_ref[rows, cs]
                rtile = r_ref[p]
                rr = jnp.concatenate(
                    [jnp.sum(jnp.where(lane == jb, rtile, 0.0), axis=1, keepdims=True),
                     jnp.sum(jnp.where(lane == 8 + jb, rtile, 0.0), axis=1, keepdims=True)], axis=0)
                lb, lom, later = _sb_block(qcats[p], kj, mask, rr, upper)
                att = jnp.exp(lb + later)
                if diagonal:
                    att = jnp.where(mask, att, 0.0)
                de = att * _dot(docats[p], vj, _NT)
                gpre = _dot(de.astype(BF16), lower) + gcar
                sig = jnp.exp(lb)
                dz = de * (1.0 - sig) - gpre * sig
                if diagonal:
                    dz = jnp.where(mask, dz, 0.0)
                dz = dz.astype(BF16)
                dk_acc[jb, cs, :] = dk_acc[jb, cs, :] + _dot(qcat_ts[p], dz)
                dv_acc[jb, cs, :] = dv_acc[jb, cs, :] + _dot(docat_ts[p], att.astype(BF16))
                out.append((dq + _dot(dz, kj), gcar + jnp.sum(de, axis=1, keepdims=True)))
            return tuple(out)

        carry = lax.fori_loop(0, i, lambda jb, c: block(jb, c, False), tuple((zero, zero[:, :1]) for _ in cols))
        carry = block(i, carry, True)
        for p, cs in enumerate(cols):
            dq = carry[p][0]
            dq_ref[:, cs] = (jnp.where(left, dq[:QB], dq[QB:]) * SB_SCALE).astype(BF16)

        @pl.when(i == nq - 1)
        def _():
            for kb in range(s // KB):
                for cs in cols:
                    dk_ref[kb * KB:(kb + 1) * KB, cs] = dk_acc[kb, cs, :].T.astype(BF16)
                    dv_ref[kb * KB:(kb + 1) * KB, cs] = dv_acc[kb, cs, :].T.astype(BF16)

        x_wait(x_refs)

    rowblk = lambda b, g, i: (b * nq + i, g)
    seqblk = lambda b, g, i: (b, g)
    return pl.pallas_call(
        body, name=name, grid=grid,
        in_specs=[pl.BlockSpec((QB, SB_GW), rowblk), pl.BlockSpec((QB, SB_GW), rowblk),
                  pl.BlockSpec((None, SB_GROUP, QB, 128), lambda b, g, i: (b * nq + i, g, 0, 0)),
                  pl.BlockSpec((QB, SB_GW), lambda b, g, i: (b * nq + i, qb0 + g)),
                  pl.BlockSpec((s, SB_GW), lambda b, g, i: (b, kb0 + g)),
                  pl.BlockSpec((s, SB_GW), lambda b, g, i: (b, vb0 + g)),
                  pl.BlockSpec((QB, SB_GW), lambda b, g, i: (b * nq + i, gb0 + g))] + x_in,
        out_specs=[pl.BlockSpec((QB, SB_GW), rowblk), pl.BlockSpec((s, SB_GW), seqblk),
                   pl.BlockSpec((s, SB_GW), seqblk), pl.BlockSpec((QB, SB_GW), rowblk)] + x_out,
        out_shape=[jax.ShapeDtypeStruct((t, SB_W), BF16)] * 4 + x_shape,
        scratch_shapes=[pltpu.VMEM((s // KB, SB_GW, KB), F32), pltpu.VMEM((s // KB, SB_GW, KB), F32)] + x_scratch,
        compiler_params=_params("arbitrary", "arbitrary", "arbitrary"),
    )(dsb, o, rsave, proj, proj, proj, proj, *x_args)


def merge_fwd(proj, ys, yp, yb, name):
    t = ys.shape[0]
    tr = min(512, t)

    def body(m_ref, ys_ref, yp_ref, yb_ref, o_ref):
        acc = jnp.zeros((tr, D), F32)
        for k, ref in enumerate((ys_ref, yp_ref, yb_ref)):
            acc = acc + _sigmoid(m_ref[:, k * D:(k + 1) * D].astype(F32)) * ref[...].astype(F32)
        o_ref[...] = acc.astype(BF16)

    rowblk = pl.BlockSpec((tr, D), lambda i: (i, 0))
    return pl.pallas_call(
        body, name=name, grid=(t // tr,),
        in_specs=[pl.BlockSpec((tr, 3 * D), lambda i: (i, 0)), rowblk, rowblk, rowblk],
        out_specs=rowblk,
        out_shape=jax.ShapeDtypeStruct((t, D), BF16),
        compiler_params=_params("parallel"),
    )(proj, ys, yp, yb)


def merge_bwd(dm, proj, ys, yp, yb, name):
    t = ys.shape[0]
    tr = min(512, t)

    def body(dm_ref, m_ref, ys_ref, yp_ref, yb_ref, d0_ref, d1_ref, d2_ref, dl_ref):
        dmv = dm_ref[...].astype(F32)
        for k, (ref, dref) in enumerate(((ys_ref, d0_ref), (yp_ref, d1_ref), (yb_ref, d2_ref))):
            g = _sigmoid(m_ref[:, k * D:(k + 1) * D].astype(F32))
            dref[...] = (g * dmv).astype(BF16)
            dl_ref[:, k * D:(k + 1) * D] = (dmv * ref[...].astype(F32) * g * (1.0 - g)).astype(BF16)

    rowblk = pl.BlockSpec((tr, D), lambda i: (i, 0))
    wide = pl.BlockSpec((tr, 3 * D), lambda i: (i, 0))
    return pl.pallas_call(
        body, name=name, grid=(t // tr,),
        in_specs=[rowblk, wide, rowblk, rowblk, rowblk],
        out_specs=[rowblk, rowblk, rowblk, wide],
        out_shape=[jax.ShapeDtypeStruct((t, D), BF16)] * 3 + [jax.ShapeDtypeStruct((t, 3 * D), BF16)],
        compiler_params=_params("parallel"),
    )(dm, proj, ys, yp, yb)


def layer_fwd(x, lw, bl, s, tag, comm=None):
    t = bl * s
    h = rmsnorm_fwd(x, lw["norm_w"], f"norm_fwd{tag}")
    proj = matmul(h, lw["w_in"], "nt", BF16, f"in_proj{tag}", tn=2048)
    xa = conv_fwd(proj, lw["conv_w"], lw["conv_b"], bl, s, f"conv_fwd{tag}")
    dtT, acT, dtC, acC = dt_fwd(proj, lw["dt_bias"], lw["a_log"], t, f"dt_fwd{tag}")
    dskip_l = jnp.repeat(lw["d_skip"], 64).reshape(1, SSM_W)
    y, prev = ssd_fwd(xa, acT, dtC, acC, dskip_l, bl, s, f"ssd_fwd{tag}")
    s_out = gnorm_fwd(y, proj, lw["ssm_norm_w"], f"gnorm_fwd{tag}")
    p_out = pool_fwd(proj, lw["pool_w"], lw["pool_scale"], bl, s, f"pool_fwd{tag}")
    sb_out, sb_o, sb_r, *carried = sb_fwd(proj, bl, s, f"sb_fwd{tag}", comm)
    ys = matmul(s_out, lw["w_proj_ssm"], "nn", BF16, f"proj_ssm{tag}")
    yp = matmul(p_out, lw["w_proj_pool"], "nn", BF16, f"proj_pool{tag}")
    yb = matmul(sb_out, lw["w_proj_sb"], "nn", BF16, f"proj_sb{tag}")
    merged = merge_fwd(proj, ys, yp, yb, f"merge_fwd{tag}")
    x_next = matmul(merged, lw["w_out"], "nn", F32, f"out_proj{tag}", residual=x)
    saved = dict(x=x, h=h, proj=proj, xa=xa, dtT=dtT, acT=acT, dtC=dtC, acC=acC, y=y, prev=prev, s_out=s_out, p_out=p_out,
                 sb_out=sb_out, sb_o=sb_o, sb_r=sb_r, ys=ys, yp=yp, yb=yb, merged=merged)
    return x_next, saved, (carried[0] if carried else None)


def layer_bwd(dx, dx_b, lw, sv, bl, s, tag, comm=None):
    t = bl * s
    g = {}
    dmerged = matmul(dx_b, lw["w_out"], "nt", BF16, f"d_merged{tag}")
    g["w_out"] = matmul(sv["merged"], dx_b, "tn", BF16, f"dw_out{tag}")
    dys, dyp, dyb, dlogit = merge_bwd(dmerged, sv["proj"], sv["ys"], sv["yp"], sv["yb"], f"merge_bwd{tag}")
    ds_out = matmul(dys, lw["w_proj_ssm"], "nt", BF16, f"d_sout{tag}")
    g["w_proj_ssm"] = matmul(sv["s_out"], dys, "tn", BF16, f"dw_proj_ssm{tag}")
    dp_out = matmul(dyp, lw["w_proj_pool"], "nt", BF16, f"d_pout{tag}")
    g["w_proj_pool"] = matmul(sv["p_out"], dyp, "tn", BF16, f"dw_proj_pool{tag}")
    dsb_out = matmul(dyb, lw["w_proj_sb"], "nt", BF16, f"d_sbout{tag}")
    g["w_proj_sb"] = matmul(sv["sb_out"], dyb, "tn", BF16, f"dw_proj_sb{tag}")
    dy, dz, dnw = gnorm_bwd(ds_out, sv["y"], sv["proj"], lw["ssm_norm_w"], f"gnorm_bwd{tag}")
    g["ssm_norm_w"] = dnw[0]
    dskip_l = jnp.repeat(lw["d_skip"], 64).reshape(1, SSM_W)
    dxs, db, dc, ddtT, dacT, dsk = ssd_bwd(dy, sv["xa"], sv["acT"], sv["dtC"], sv["acC"], dskip_l, sv["prev"], bl, s,
                                           f"ssd_bwd{tag}")
    g["d_skip"] = jnp.sum(dsk.reshape(N_HEADS, 64), axis=1)
    ddt_raw, da, dbias = dt_bwd(ddtT, dacT, sv["dtT"], sv["proj"], lw["dt_bias"], lw["a_log"], t, f"dt_bwd{tag}")
    g["a_log"] = da * (-jnp.exp(lw["a_log"]))
    g["dt_bias"] = dbias
    dxa = jnp.concatenate([dxs, db.astype(BF16), dc.astype(BF16)], axis=1)
    dxbc, dcw, dcb = conv_bwd(dxa, sv["proj"], lw["conv_w"], lw["conv_b"], bl, s, f"conv_bwd{tag}")
    g["conv_w"] = dcw
    g["conv_b"] = dcb
    dpu, dpg, dpw, dpsc = pool_bwd(dp_out, sv["proj"], lw["pool_w"], lw["pool_scale"], bl, s, f"pool_bwd{tag}")
    g["pool_w"] = dpw
    g["pool_scale"] = dpsc[0]
    dq, dk, dv, dsbg, *carried = sb_bwd(dsb_out, sv["sb_o"], sv["sb_r"], sv["proj"], bl, s, f"sb_bwd{tag}", comm)
    dproj = concat_columns([dlogit, dsbg, dpu, dpg, dz, dq, dk, dv, dxbc, ddt_raw], PC, f"d_proj{tag}")
    g["w_in"] = matmul(dproj, sv["h"], "tn", BF16, f"dw_in{tag}", tk=2048)
    dh = matmul(dproj, lw["w_in"], "nn", F32, f"d_h{tag}", tk=2048)
    dx_in, dx_in_b, dnorm = rmsnorm_bwd(dh, sv["x"], lw["norm_w"], dx, f"norm_bwd{tag}")
    g["norm_w"] = dnorm[0]
    return dx_in, dx_in_b, g, (carried[0] if carried else None)


def concat_columns(parts, width, name):
    t = parts[0].shape[0]
    tr = min(256, t)
    widths = [p.shape[1] for p in parts]
    used = sum(widths)

    def body(*refs):
        o_ref = refs[-1]
        off = 0
        for ref, w in zip(refs[:-1], widths):
            o_ref[:, off:off + w] = ref[...]
            off += w
        if width > used:
            o_ref[:, used:] = jnp.zeros((tr, width - used), BF16)

    return pl.pallas_call(
        body, name=name, grid=(t // tr,),
        in_specs=[pl.BlockSpec((tr, w), lambda i: (i, 0)) for w in widths],
        out_specs=pl.BlockSpec((tr, width), lambda i: (i, 0)),
        out_shape=jax.ShapeDtypeStruct((t, width), BF16),
        compiler_params=_params("parallel"),
    )(*parts)


_PAD_PIECES = ((10784, 3072), (9760, 1024), (4640, 1024), (5664, 1024), (0, 2048), (6688, 3072), (2048, 2560), (4608, 32))
_UNPAD_PIECES = ((OFF_Z, 2048), (OFF_XBC, 2560), (OFF_DT, 32), (OFF_PU, 1024), (OFF_PG, 1024), (OFF_QKV, 3072),
                 (OFF_SBG, 1024), (OFF_MERGE, 3072))


def pad_rows(wt):
    pieces = [wt[o:o + n] for o, n in _PAD_PIECES]
    return jnp.concatenate(pieces + [jnp.zeros((PC - IN_COLS, wt.shape[1]), wt.dtype)], axis=0)


def unpad_rows(wp):
    return jnp.concatenate([wp[o:o + n] for o, n in _UNPAD_PIECES], axis=0)


MESH = pl.DeviceIdType.MESH
ANY = pl.BlockSpec(memory_space=pl.ANY)


def _coords():
    return lax.axis_index("x"), lax.axis_index("y"), lax.axis_index("c")


def _peer(p):
    x, y, c = _coords()
    return (1 - x if p & 4 else x, 1 - y if p & 2 else y, 1 - c if p & 1 else c)


def _flat(pos):
    return 4 * pos[0] + 2 * pos[1] + pos[2]


def _chip(pos):
    return 2 * pos[0] + pos[1]


def _exchange_copies(v_ref, out_ref, send_sems, recv_sems, local_sem, mode):
    x, y, c = _coords()
    me = _flat((x, y, c))
    sibling = (x, y, 1 - c)
    chips = [(1 - x if j & 2 else x, 1 - y if j & 1 else y) for j in range(1, 4)]

    def copy(k, src, landing, to):
        return pltpu.make_async_remote_copy(src_ref=src, dst_ref=out_ref.at[landing], send_sem=send_sems.at[k],
                                            recv_sem=recv_sems.at[k], device_id=to, device_id_type=MESH)

    if mode == "direct":
        local = pltpu.make_async_copy(v_ref, out_ref.at[me], local_sem)
        first = [copy(p - 1, v_ref, me, _peer(p)) for p in range(1, N_DEV)]
        last = [copy(p - 1, v_ref, _flat(_peer(p)), _peer(p)) for p in range(1, N_DEV)]
        return local, first, [], last
    if mode == "gather":
        local = pltpu.make_async_copy(v_ref, out_ref.at[me], local_sem)
        first = [copy(0, v_ref, me, sibling)] + [copy(1 + j, v_ref, me, (*ch, c)) for j, ch in enumerate(chips)]
        relay = [(copy(1 + j, v_ref, _flat((*ch, c)), (*ch, c)),
                  copy(4 + j, out_ref.at[_flat((*ch, c))], _flat((*ch, c)), sibling)) for j, ch in enumerate(chips)]
        last = [copy(0, v_ref, _flat(sibling), sibling)] + [
            copy(4 + j, v_ref, _flat((*ch, 1 - c)), sibling) for j, ch in enumerate(chips)]
        return local, first, relay, last
    assert mode == "chips"
    mine = _chip((x, y))
    local = pltpu.make_async_copy(v_ref.at[mine], out_ref.at[mine], local_sem)
    first = [copy(j, v_ref.at[_chip(ch)], mine, (*ch, c)) for j, ch in enumerate(chips)]
    last = [copy(j, v_ref.at[mine], _chip(ch), (*ch, c)) for j, ch in enumerate(chips)]
    return local, first, [], last


def _exchange_start(*refs_and_mode):
    local, first, _, _ = _exchange_copies(*refs_and_mode)
    local.start()
    for cp in first:
        cp.start()


def _exchange_relay(*refs_and_mode):
    for arrival, onward in _exchange_copies(*refs_and_mode)[2]:
        arrival.wait_recv()
        onward.start()


def _exchange_finish(*refs_and_mode):
    local, first, relay, last = _exchange_copies(*refs_and_mode)
    for cp in last:
        cp.wait_recv()
    for cp in first + [onward for _, onward in relay]:
        cp.wait_send()
    local.wait()


def _exchange_shape(v, mode):
    return jax.ShapeDtypeStruct(tuple(v.shape) if mode == "chips" else (N_DEV,) + tuple(v.shape), v.dtype)


def _exchange_sems():
    return [pltpu.SemaphoreType.DMA((N_DEV - 1,)), pltpu.SemaphoreType.DMA((N_DEV - 1,)), pltpu.SemaphoreType.DMA]


def exchange(v, mode, name):
    def body(*refs):
        _exchange_start(*refs, mode)
        _exchange_relay(*refs, mode)
        _exchange_finish(*refs, mode)

    return pl.pallas_call(
        body, name=name,
        in_specs=[ANY], out_specs=ANY,
        out_shape=_exchange_shape(v, mode),
        scratch_shapes=_exchange_sems(),
    )(v)


def _hosted_exchange(comm, grid, relay_at):
    if comm is None:
        return [], [], [], [], [], (lambda refs: None), (lambda refs: None)
    v, mode = comm

    def at(step):
        cond = None
        for axis, want in enumerate(step):
            term = pl.program_id(axis) == want
            cond = term if cond is None else jnp.logical_and(cond, term)
        return cond

    def start(refs):
        @pl.when(at([0] * len(grid)))
        def _():
            _exchange_start(*refs, mode)

        if mode == "gather":
            @pl.when(at(relay_at))
            def _():
                _exchange_relay(*refs, mode)

    def wait(refs):
        @pl.when(at([n - 1 for n in grid]))
        def _():
            _exchange_finish(*refs, mode)

    return [ANY], [v], [ANY], [_exchange_shape(v, mode)], _exchange_sems(), start, wait


def pair_swap(s0, s1, name):
    def body(s0_ref, s1_ref, out_ref, send_sems, recv_sems):
        x, y, c = _coords()

        def copies(src_ref):
            return [pltpu.make_async_remote_copy(src_ref=src_ref.at[j], dst_ref=out_ref.at[j], send_sem=send_sems.at[j],
                                                 recv_sem=recv_sems.at[j], device_id=(x, y, 1 - c), device_id_type=MESH)
                    for j in range(4)]

        @pl.when(c == 0)
        def _():
            for cp in copies(s1_ref):
                cp.start()

        @pl.when(c == 1)
        def _():
            for cp in copies(s0_ref):
                cp.start()

        for cp in copies(s0_ref):
            cp.wait_recv()
        for cp in copies(s0_ref):
            cp.wait_send()

    return pl.pallas_call(
        body, name=name,
        in_specs=[ANY, ANY], out_specs=ANY,
        out_shape=jax.ShapeDtypeStruct(s0.shape, s0.dtype),
        scratch_shapes=[pltpu.SemaphoreType.DMA((4,)), pltpu.SemaphoreType.DMA((4,))],
    )(s0, s1)


def pair_sum(s0, s1, got, name):
    _, r, c = s0.shape
    tr = r // 4 if r % 64 == 0 else r

    def body(s0_ref, s1_ref, got_ref, o_ref):
        mine = jnp.where(lax.axis_index("c") == 0, s0_ref[...].astype(F32), s1_ref[...].astype(F32))
        o_ref[...] = (mine + got_ref[...].astype(F32)).astype(BF16)

    blk = pl.BlockSpec((None, tr, c), lambda j, i: (j, i, 0))
    return pl.pallas_call(
        body, name=name, grid=(4, r // tr),
        in_specs=[blk, blk, blk], out_specs=blk,
        out_shape=jax.ShapeDtypeStruct(s0.shape, BF16),
        compiler_params=_params("parallel", "parallel"),
    )(s0, s1, got)


def sum_slabs(v, name):
    n, r, c = v.shape
    tr = 128 if r % 128 == 0 else r

    def body(v_ref, o_ref):
        acc = v_ref[0].astype(F32)
        for k in range(1, n):
            acc = acc + v_ref[k].astype(F32)
        o_ref[...] = acc

    return pl.pallas_call(
        body, name=name, grid=(r // tr,),
        in_specs=[pl.BlockSpec((n, tr, c), lambda i: (0, i, 0))],
        out_specs=pl.BlockSpec((tr, c), lambda i: (i, 0)),
        out_shape=jax.ShapeDtypeStruct((r, c), F32),
        compiler_params=_params("parallel"),
    )(v)


def adamw(w, g, m, v, name):
    r, c = w.shape
    tr = next((cand for cand in (256, 128, 64, 32, 16, 8) if r % cand == 0), r)

    def body(w_ref, g_ref, m_ref, v_ref, d_ref, mo_ref, vo_ref):
        gv = g_ref[...]
        mn = ADAM_B1 * m_ref[...] + (1.0 - ADAM_B1) * gv
        vn = ADAM_B2 * v_ref[...] + (1.0 - ADAM_B2) * (gv * gv)
        m_hat = mn / (1.0 - ADAM_B1 ** ADAM_STEP)
        v_hat = vn / (1.0 - ADAM_B2 ** ADAM_STEP)
        d_ref[...] = -ADAM_LR * (m_hat / (jnp.sqrt(v_hat) + ADAM_EPS) + ADAM_WD * w_ref[...])
        mo_ref[...] = mn
        vo_ref[...] = vn

    blk = pl.BlockSpec((tr, c), lambda i: (i, 0))
    return pl.pallas_call(
        body, name=name, grid=(r // tr,),
        in_specs=[blk] * 4, out_specs=[blk] * 3,
        out_shape=[jax.ShapeDtypeStruct((r, c), F32)] * 3,
        compiler_params=_params("parallel"),
    )(w, g, m, v)


BIG = ("w_proj_ssm", "w_proj_pool", "w_proj_sb", "w_out", "pool_w", "w_in")
SHARD_IN = IN_COLS // N_DEV
BIG_ROWS = {"w_proj_ssm": SSM_W // N_DEV, "w_proj_pool": POOL_W // N_DEV, "w_proj_sb": SB_W // N_DEV,
            "w_out": D // N_DEV, "pool_w": POOL_G * (POOL_GD // N_DEV) * POOL_GD // D, "w_in": SHARD_IN}
PACK_C = D
PACK_R = 2432

REPLICATED = ("norm_w", "conv_b", "dt_bias", "a_log", "d_skip", "ssm_norm_w", "pool_scale")
WEIGHTS = ("norm_w", "w_in", "conv_w", "conv_b", "dt_bias", "a_log", "d_skip", "ssm_norm_w", "pool_w",
           "pool_scale", "w_proj_ssm", "w_proj_pool", "w_proj_sb", "w_out", "final_norm_w")


def _size(shape):
    n = 1
    for d in shape:
        n *= d
    return n


def _pad_flat(flat, n):
    return jnp.concatenate([flat, jnp.zeros((n - flat.shape[0],), flat.dtype)])


def _row_offsets():
    offs, off = {}, 0
    for n in BIG:
        offs[n] = off
        off += BIG_ROWS[n]
    return offs, off


def pack_shards(parts):
    rows = [parts[n].reshape(BIG_ROWS[n], PACK_C) for n in BIG]
    rows[-1] = jnp.pad(rows[-1], ((0, PACK_R - _row_offsets()[1]), (0, 0)))
    return jnp.concatenate(rows, axis=0)


def unpack_shards(packed):
    offs, _ = _row_offsets()
    out = {}
    for n in BIG:
        seg = packed[offs[n]:offs[n] + BIG_ROWS[n]]
        if n == "w_in":
            out[n] = seg.T
        elif n == "pool_w":
            out[n] = seg.reshape(POOL_G, POOL_GD // N_DEV, POOL_GD)
        else:
            out[n] = seg
    return out


def unpack_gathered(g):
    offs, _ = _row_offsets()
    out = {}
    for n in BIG:
        seg = g[:, offs[n]:offs[n] + BIG_ROWS[n], :]
        if n == "w_in":
            out[n] = pad_rows(seg.reshape(IN_COLS, D))
        elif n == "pool_w":
            out[n] = seg.reshape(N_DEV, POOL_G, POOL_GD // N_DEV, POOL_GD).transpose(1, 0, 2, 3).reshape(
                POOL_G, POOL_GD, POOL_GD)
        else:
            out[n] = seg.reshape(N_DEV * BIG_ROWS[n], D)
    return out


def pack_slabs(g):
    segs = []
    for n in BIG:
        if n == "w_in":
            w = unpad_rows(g[n])
        elif n == "pool_w":
            w = g[n].reshape(POOL_G, N_DEV, POOL_GD // N_DEV, POOL_GD).transpose(1, 0, 2, 3)
        else:
            w = g[n]
        segs.append(w.reshape(N_DEV // 2, 2, BIG_ROWS[n], PACK_C).astype(BF16))
    segs[-1] = jnp.pad(segs[-1], ((0, 0), (0, 0), (0, PACK_R - _row_offsets()[1]), (0, 0)))
    return tuple(jnp.concatenate([seg[:, core] for seg in segs], axis=1) for core in range(2))


SMALL_ROWS = 544


def pack_small(vals):
    flat = jnp.concatenate([v.reshape(-1) for v in vals])
    return _pad_flat(flat, SMALL_ROWS * 128).reshape(SMALL_ROWS, 128)


def unpack_small(packed, shapes):
    flat = packed.reshape(-1)
    out, off = [], 0
    for shp in shapes:
        out.append(flat[off:off + _size(shp)].reshape(shp))
        off += _size(shp)
    return out


def kernel(x, norm_w, w_in, conv_w, conv_b, dt_bias, a_log, d_skip, ssm_norm_w, pool_w, pool_scale, w_proj_ssm, w_proj_pool, w_proj_sb, w_out, final_norm_w, loss_target, m_norm_w, m_w_in, m_conv_w, m_conv_b, m_dt_bias, m_a_log, m_d_skip, m_ssm_norm_w, m_pool_w, m_pool_scale, m_w_proj_ssm, m_w_proj_pool, m_w_proj_sb, m_w_out, m_final_norm_w, v_norm_w, v_w_in, v_conv_w, v_conv_b, v_dt_bias, v_a_log, v_d_skip, v_ssm_norm_w, v_pool_w, v_pool_scale, v_w_proj_ssm, v_w_proj_pool, v_w_proj_sb, v_w_out, v_final_norm_w):
    wts = dict(norm_w=norm_w, w_in=w_in, conv_w=conv_w, conv_b=conv_b, dt_bias=dt_bias, a_log=a_log, d_skip=d_skip,
               ssm_norm_w=ssm_norm_w, pool_w=pool_w, pool_scale=pool_scale, w_proj_ssm=w_proj_ssm,
               w_proj_pool=w_proj_pool, w_proj_sb=w_proj_sb, w_out=w_out, final_norm_w=final_norm_w)
    mom = dict(norm_w=m_norm_w, w_in=m_w_in, conv_w=m_conv_w, conv_b=m_conv_b, dt_bias=m_dt_bias, a_log=m_a_log,
               d_skip=m_d_skip, ssm_norm_w=m_ssm_norm_w, pool_w=m_pool_w, pool_scale=m_pool_scale,
               w_proj_ssm=m_w_proj_ssm, w_proj_pool=m_w_proj_pool, w_proj_sb=m_w_proj_sb, w_out=m_w_out,
               final_norm_w=m_final_norm_w)
    var = dict(norm_w=v_norm_w, w_in=v_w_in, conv_w=v_conv_w, conv_b=v_conv_b, dt_bias=v_dt_bias, a_log=v_a_log,
               d_skip=v_d_skip, ssm_norm_w=v_ssm_norm_w, pool_w=v_pool_w, pool_scale=v_pool_scale,
               w_proj_ssm=v_w_proj_ssm, w_proj_pool=v_w_proj_pool, w_proj_sb=v_w_proj_sb, w_out=v_w_out,
               final_norm_w=v_final_norm_w)
    bl, s, _ = x.shape
    t = bl * s
    me = _flat(_coords())

    cw = exchange(conv_w.reshape(40, 128), "direct", "gather_conv_w")
    conv_w_full = cw.reshape(N_DEV, DEPTH, CONV_K, CONV_CH // N_DEV).transpose(1, 2, 0, 3).reshape(
        DEPTH, CONV_K, CONV_CH)

    xc = x.reshape(t, D)
    layer_w, saved = [], []
    packed = [pack_shards({n: (wts[n][l].T if n == "w_in" else wts[n][l]).astype(BF16) for n in BIG})
              for l in range(DEPTH)]
    gathered = exchange(packed[0], "gather", "gather_w0")
    for l in range(DEPTH):
        lw = unpack_gathered(gathered)
        for n in REPLICATED:
            lw[n] = wts[n][l]
        lw["conv_w"] = conv_w_full[l]
        xc, sv, gathered = layer_fwd(xc, lw, bl, s, f"_l{l}", (packed[l + 1], "gather") if l + 1 < DEPTH else None)
        layer_w.append(lw)
        saved.append(sv)

    loss_part, dx, dx_b, dfinal = final_loss(xc, final_norm_w, loss_target.reshape(t, D), "final_loss")
    loss = lax.psum(loss_part[0, 0], ("x", "y", "c"))

    grads = [None] * DEPTH
    big_sum = [None] * DEPTH
    chip_sums = None
    for l in reversed(range(DEPTH)):
        dx, dx_b, g, got = layer_bwd(dx, dx_b, layer_w[l], saved[l], bl, s, f"_l{l}",
                                     (chip_sums, "chips") if chip_sums is not None else None)
        if got is not None:
            big_sum[l + 1] = unpack_shards(sum_slabs(got, f"sum_g{l + 1}"))
        grads[l] = g
        s0, s1 = pack_slabs(g)
        chip_sums = pair_sum(s0, s1, pair_swap(s0, s1, f"pair_swap{l}"), f"pair_sum{l}")
    big_sum[0] = unpack_shards(sum_slabs(exchange(chip_sums, "chips", "scatter_g0"), "sum_g0"))
    grad_x = dx.reshape(bl, s, D)

    small_names = REPLICATED + ("conv_w",)
    small_vals = [jnp.stack([grads[l][n] for l in range(DEPTH)]) for n in small_names] + [dfinal[0]]
    small_shapes = [v.shape for v in small_vals]
    small_all = exchange(pack_small(small_vals), "direct", "gather_small")
    small_sum = unpack_small(sum_slabs(small_all, "sum_small"), small_shapes)
    gsum = dict(zip(small_names + ("final_norm_w",), small_sum))
    conv_g_full = gsum["conv_w"]
    gsum["conv_w"] = lax.dynamic_slice_in_dim(conv_g_full, me * (CONV_CH // N_DEV), CONV_CH // N_DEV, axis=2)
    for n in BIG:
        gsum[n] = jnp.stack([big_sum[l][n] for l in range(DEPTH)])

    delta, new_m, new_v = {}, {}, {}
    for n in BIG + ("conv_w",):
        shp = wts[n].shape
        two_d = (-1, shp[-1])
        d2, m2, v2 = adamw(wts[n].reshape(two_d), gsum[n].reshape(two_d), mom[n].reshape(two_d),
                           var[n].reshape(two_d), f"adamw_{n}")
        delta[n], new_m[n], new_v[n] = d2.reshape(shp), m2.reshape(shp), v2.reshape(shp)
    rep = REPLICATED + ("final_norm_w",)
    rep_shapes = [wts[n].shape for n in rep]
    d2, m2, v2 = adamw(pack_small([wts[n] for n in rep]), pack_small([gsum[n] for n in rep]),
                       pack_small([mom[n] for n in rep]), pack_small([var[n] for n in rep]), "adamw_small")
    for n, dv, mv, vv in zip(rep, unpack_small(d2, rep_shapes), unpack_small(m2, rep_shapes),
                             unpack_small(v2, rep_shapes)):
        delta[n], new_m[n], new_v[n] = dv, mv, vv

    return (loss, grad_x, *[gsum[n] for n in WEIGHTS], *[delta[n] for n in WEIGHTS],
            *[new_m[n] for n in WEIGHTS], *[new_v[n] for n in WEIGHTS])
```

```python
import functools

import jax
import jax.numpy as jnp
from jax import lax
from jax.experimental import pallas as pl
from jax.experimental.pallas import tpu as pltpu

F32 = jnp.float32
BF16 = jnp.bfloat16

N_DEV = 8
DEPTH = 4
D = 1024
SSM_W = 2048
N_HEADS = 32
N_PAIRS = 16
N_GROUPS = 2
N_STATE = 128
CHUNK = 128
CONV_CH = 2560
CONV_K = 4
POOL_W = 1024
POOL_G = 4
POOL_GD = 256
SB_W = 1024
SB_PAIRS = 8
QB = 256
EPS = 1e-6
IN_COLS = 13856

PC = 14336
OFF_MERGE = 0
OFF_SBG = 3072
OFF_PU = 4096
OFF_PG = 5120
OFF_Z = 6144
OFF_QKV = 8192
OFF_XBC = 11264
OFF_DT = 13824

ADAM_LR = 0.001
ADAM_B1 = 0.9
ADAM_B2 = 0.999
ADAM_EPS = 1e-08
ADAM_WD = 0.01
ADAM_STEP = 10

VMEM_LIMIT = 56 * 1024 * 1024

_NN = (((1,), (0,)), ((), ()))
_NT = (((1,), (1,)), ((), ()))
_TN = (((0,), (0,)), ((), ()))


def _dot(a, b, dn=_NN):
    return lax.dot_general(a, b, dn, preferred_element_type=F32)


def _sigmoid(x):
    return 1.0 / (1.0 + jnp.exp(-x))


def _softplus(x):
    return jnp.maximum(x, 0.0) + jnp.log(1.0 + jnp.exp(-jnp.abs(x)))


def _split2(x):
    hi = x.astype(BF16)
    lo = (x - hi.astype(F32)).astype(BF16)
    return hi, lo


def _split3(x):
    hi = x.astype(BF16)
    r = x - hi.astype(F32)
    mid = r.astype(BF16)
    lo = (r - mid.astype(F32)).astype(BF16)
    return hi, mid, lo


def _params(*sem):
    return pltpu.CompilerParams(dimension_semantics=sem, vmem_limit_bytes=VMEM_LIMIT)


def matmul(a, b, mode, out_dtype, name, residual=None, tm=1024, tn=1024, tk=1024, comm=None):
    if mode == "nn":
        (m, k), (k2, n) = a.shape, b.shape
    elif mode == "nt":
        (m, k), (n, k2) = a.shape, b.shape
    else:
        (k, m), (k2, n) = a.shape, b.shape
    assert k == k2
    tm, tn, tk = min(tm, m), min(tn, n), min(tk, k)
    assert m % tm == 0 and n % tn == 0 and k % tk == 0
    nk = k // tk
    dn = {"nn": _NN, "nt": _NT, "tn": _TN}[mode]
    a_spec = pl.BlockSpec((tk, tm), lambda i, j, kk: (kk, i)) if mode == "tn" else pl.BlockSpec((tm, tk), lambda i, j, kk: (i, kk))
    b_spec = pl.BlockSpec((tn, tk), lambda i, j, kk: (j, kk)) if mode == "nt" else pl.BlockSpec((tk, tn), lambda i, j, kk: (kk, j))
    in_specs = [a_spec, b_spec]
    args = [a, b]
    if residual is not None:
        in_specs.append(pl.BlockSpec((tm, tn), lambda i, j, kk: (i, j)))
        args.append(residual)
    grid = (m // tm, n // tn, nk)
    n_in = len(args)
    x_in, x_args, x_out, x_shape, x_scratch, x_start, x_wait = _hosted_exchange(comm, grid, relay_at=None)

    def body(*refs):
        n_x = len(x_in)
        a_ref, b_ref = refs[:2]
        r_ref = refs[2] if residual is not None else None
        o_ref = refs[n_in + n_x]
        acc_ref = refs[n_in + 2 * n_x + 1]
        x_refs = refs[n_in:n_in + n_x] + refs[n_in + n_x + 1:n_in + 2 * n_x + 1] + refs[n_in + 2 * n_x + 2:]
        x_start(x_refs)
        kk = pl.program_id(2)
        p = _dot(a_ref[...], b_ref[...], dn)

        def finish(val):
            if r_ref is not None:
                val = val + r_ref[...]
            o_ref[...] = val.astype(out_dtype)

        if nk == 1:
            finish(p)
        else:
            @pl.when(kk == 0)
            def _():
                acc_ref[...] = p

            @pl.when(kk > 0)
            def _():
                acc_ref[...] += p

            @pl.when(kk == nk - 1)
            def _():
                finish(acc_ref[...])

        x_wait(x_refs)

    out = pl.pallas_call(
        body, name=name,
        grid=grid,
        in_specs=in_specs + x_in,
        out_specs=[pl.BlockSpec((tm, tn), lambda i, j, kk: (i, j))] + x_out,
        out_shape=[jax.ShapeDtypeStruct((m, n), out_dtype)] + x_shape,
        scratch_shapes=[pltpu.VMEM((tm, tn) if nk > 1 else (8, 128), F32)] + x_scratch,
        compiler_params=_params(*(("arbitrary",) * 3 if comm is not None else ("parallel", "parallel", "arbitrary"))),
    )(*args, *x_args)
    return tuple(out) if comm is not None else out[0]


def rmsnorm_fwd(x, w, name):
    t, d = x.shape
    tr = min(512, t)

    def body(x_ref, w_ref, h_ref):
        xv = x_ref[...]
        r = lax.rsqrt(jnp.mean(xv * xv, axis=-1, keepdims=True) + EPS)
        h_ref[...] = (xv * r * w_ref[...]).astype(BF16)

    return pl.pallas_call(
        body, name=name, grid=(t // tr,),
        in_specs=[pl.BlockSpec((tr, d), lambda i: (i, 0)), pl.BlockSpec((1, d), lambda i: (0, 0))],
        out_specs=pl.BlockSpec((tr, d), lambda i: (i, 0)),
        out_shape=jax.ShapeDtypeStruct((t, d), BF16),
        compiler_params=_params("parallel"),
    )(x, w.reshape(1, d))


def rmsnorm_bwd(dh, x, w, dres, name):
    t, d = x.shape
    tr = min(512, t)

    def body(dh_ref, x_ref, w_ref, dres_ref, dx_ref, dxb_ref, dw_ref):
        xv = x_ref[...]
        r = lax.rsqrt(jnp.mean(xv * xv, axis=-1, keepdims=True) + EPS)
        xh = xv * r
        g = dh_ref[...].astype(F32)
        dxh = g * w_ref[...]
        dxv = dres_ref[...] + r * (dxh - xh * jnp.mean(dxh * xh, axis=-1, keepdims=True))
        dx_ref[...] = dxv
        dxb_ref[...] = dxv.astype(BF16)
        part = jnp.sum(g * xh, axis=0, keepdims=True)

        @pl.when(pl.program_id(0) == 0)
        def _():
            dw_ref[...] = part

        @pl.when(pl.program_id(0) > 0)
        def _():
            dw_ref[...] += part

    return pl.pallas_call(
        body, name=name, grid=(t // tr,),
        in_specs=[pl.BlockSpec((tr, d), lambda i: (i, 0)), pl.BlockSpec((tr, d), lambda i: (i, 0)),
                  pl.BlockSpec((1, d), lambda i: (0, 0)), pl.BlockSpec((tr, d), lambda i: (i, 0))],
        out_specs=[pl.BlockSpec((tr, d), lambda i: (i, 0)), pl.BlockSpec((tr, d), lambda i: (i, 0)),
                   pl.BlockSpec((1, d), lambda i: (0, 0))],
        out_shape=[jax.ShapeDtypeStruct((t, d), F32), jax.ShapeDtypeStruct((t, d), BF16),
                   jax.ShapeDtypeStruct((1, d), F32)],
        compiler_params=_params("arbitrary"),
    )(dh, x, w.reshape(1, d), dres)


def final_loss(x, w, target, name):
    t, d = x.shape
    tr = min(512, t)

    def body(x_ref, w_ref, tg_ref, loss_ref, dx_ref, dxb_ref, dw_ref):
        xv = x_ref[...]
        r = lax.rsqrt(jnp.mean(xv * xv, axis=-1, keepdims=True) + EPS)
        xh = xv * r
        err = xh * w_ref[...] - tg_ref[...]
        lpart = 0.5 * jnp.sum(jnp.mean(err * err, axis=-1, keepdims=True), axis=0, keepdims=True)
        dy = err * (1.0 / d)
        dxh = dy * w_ref[...]
        dxv = r * (dxh - xh * jnp.mean(dxh * xh, axis=-1, keepdims=True))
        dx_ref[...] = dxv
        dxb_ref[...] = dxv.astype(BF16)
        part = jnp.sum(dy * xh, axis=0, keepdims=True)

        @pl.when(pl.program_id(0) == 0)
        def _():
            dw_ref[...] = part
            loss_ref[...] = jnp.broadcast_to(lpart, (1, 128))

        @pl.when(pl.program_id(0) > 0)
        def _():
            dw_ref[...] += part
            loss_ref[...] += jnp.broadcast_to(lpart, (1, 128))

    return pl.pallas_call(
        body, name=name, grid=(t // tr,),
        in_specs=[pl.BlockSpec((tr, d), lambda i: (i, 0)), pl.BlockSpec((1, d), lambda i: (0, 0)),
                  pl.BlockSpec((tr, d), lambda i: (i, 0))],
        out_specs=[pl.BlockSpec((1, 128), lambda i: (0, 0)), pl.BlockSpec((tr, d), lambda i: (i, 0)),
                   pl.BlockSpec((tr, d), lambda i: (i, 0)), pl.BlockSpec((1, d), lambda i: (0, 0))],
        out_shape=[jax.ShapeDtypeStruct((1, 128), F32), jax.ShapeDtypeStruct((t, d), F32),
                   jax.ShapeDtypeStruct((t, d), BF16), jax.ShapeDtypeStruct((1, d), F32)],
        compiler_params=_params("arbitrary"),
    )(x, w.reshape(1, d), target)


CONV_BW = 256


def _shift_down(u, s, row):
    return jnp.where(row >= s, pltpu.roll(u, s, axis=0), 0.0)


def _shift_up(u, s, row, n):
    return jnp.where(row < n - s, pltpu.roll(u, n - s, axis=0), 0.0)


def _conv_pre(u, w, b, row):
    acc = b + w[CONV_K - 1:CONV_K, :] * u
    for k in range(CONV_K - 1):
        acc = acc + w[k:k + 1, :] * _shift_down(u, CONV_K - 1 - k, row)
    return acc


def conv_fwd(proj, conv_w, conv_b, bl, s, name):
    t = bl * s
    nb = CONV_CH // CONV_BW
    off = OFF_XBC // CONV_BW

    def body(u_ref, w_ref, b_ref, o_ref):
        u = u_ref[...].astype(F32)
        row = lax.broadcasted_iota(jnp.int32, u.shape, 0)
        xc = _conv_pre(u, w_ref[...], b_ref[...], row)
        o_ref[...] = (xc * _sigmoid(xc)).astype(BF16)

    return pl.pallas_call(
        body, name=name, grid=(bl, nb),
        in_specs=[pl.BlockSpec((s, CONV_BW), lambda b, j: (b, off + j)),
                  pl.BlockSpec((CONV_K, CONV_BW), lambda b, j: (0, j)),
                  pl.BlockSpec((1, CONV_BW), lambda b, j: (0, j))],
        out_specs=pl.BlockSpec((s, CONV_BW), lambda b, j: (b, j)),
        out_shape=jax.ShapeDtypeStruct((t, CONV_CH), BF16),
        compiler_params=_params("parallel", "parallel"),
    )(proj, conv_w, conv_b.reshape(1, CONV_CH))


def conv_bwd(dxa, proj, conv_w, conv_b, bl, s, name):
    t = bl * s
    nb = CONV_CH // CONV_BW
    off = OFF_XBC // CONV_BW

    def body(d_ref, u_ref, w_ref, b_ref, du_ref, dw_ref, db_ref):
        u = u_ref[...].astype(F32)
        w = w_ref[...]
        row = lax.broadcasted_iota(jnp.int32, u.shape, 0)
        xc = _conv_pre(u, w, b_ref[...], row)
        sg = _sigmoid(xc)
        dxc = d_ref[...].astype(F32) * sg * (1.0 + xc * (1.0 - sg))
        du = w[CONV_K - 1:CONV_K, :] * dxc
        dws = [None] * CONV_K
        dws[CONV_K - 1] = jnp.sum(dxc * u, axis=0, keepdims=True)
        for k in range(CONV_K - 1):
            sh = CONV_K - 1 - k
            du = du + w[k:k + 1, :] * _shift_up(dxc, sh, row, s)
            dws[k] = jnp.sum(dxc * _shift_down(u, sh, row), axis=0, keepdims=True)
        du_ref[...] = du.astype(BF16)
        krow = lax.broadcasted_iota(jnp.int32, (8, CONV_BW), 0)
        dwv = sum(jnp.where(krow == k, dws[k], 0.0) for k in range(CONV_K))
        dbv = jnp.sum(dxc, axis=0, keepdims=True)

        @pl.when(pl.program_id(1) == 0)
        def _():
            dw_ref[...] = dwv
            db_ref[...] = dbv

        @pl.when(pl.program_id(1) > 0)
        def _():
            dw_ref[...] += dwv
            db_ref[...] += dbv

    du, dw, db = pl.pallas_call(
        body, name=name, grid=(nb, bl),
        in_specs=[pl.BlockSpec((s, CONV_BW), lambda j, b: (b, j)),
                  pl.BlockSpec((s, CONV_BW), lambda j, b: (b, off + j)),
                  pl.BlockSpec((CONV_K, CONV_BW), lambda j, b: (0, j)),
                  pl.BlockSpec((1, CONV_BW), lambda j, b: (0, j))],
        out_specs=[pl.BlockSpec((s, CONV_BW), lambda j, b: (b, j)),
                   pl.BlockSpec((8, CONV_BW), lambda j, b: (0, j)),
                   pl.BlockSpec((1, CONV_BW), lambda j, b: (0, j))],
        out_shape=[jax.ShapeDtypeStruct((t, CONV_CH), BF16), jax.ShapeDtypeStruct((8, CONV_CH), F32),
                   jax.ShapeDtypeStruct((1, CONV_CH), F32)],
        compiler_params=_params("parallel", "arbitrary"),
    )(dxa, proj, conv_w, conv_b.reshape(1, CONV_CH))
    return du, dw[:CONV_K], db[0]


def _tri(shape, cmp):
    r = lax.broadcasted_iota(jnp.int32, shape, 0)
    c = lax.broadcasted_iota(jnp.int32, shape, 1)
    return cmp(r, c)


def dt_fwd(proj, dt_bias, a_log, t, name):
    nchunks = t // CHUNK
    bias = jnp.zeros((1, 128), F32).at[0, :N_HEADS].set(dt_bias)
    alog = jnp.zeros((1, 128), F32).at[0, :N_HEADS].set(a_log)

    def body(raw_ref, b_ref, al_ref, dt_ref, ac_ref, dtl_ref, acl_ref):
        raw = raw_ref[...].astype(F32)
        dt = _softplus(raw + b_ref[...])
        adt = dt * (-jnp.exp(al_ref[...]))
        low = _tri((CHUNK, CHUNK), lambda r, c: r >= c).astype(BF16)
        acum = sum(_dot(low, part) for part in _split3(adt))
        dt_ref[...] = dt.T[:N_HEADS]
        ac_ref[...] = acum.T[:N_HEADS]
        spread = _tri((128, SSM_W), lambda h, lane: lane // 64 == h).astype(BF16)
        dtl_ref[...] = sum(_dot(part, spread) for part in _split3(dt))
        acl_ref[...] = sum(_dot(part, spread) for part in _split3(acum))

    return pl.pallas_call(
        body, name=name, grid=(nchunks,),
        in_specs=[pl.BlockSpec((CHUNK, 128), lambda i: (i, OFF_DT // 128)),
                  pl.BlockSpec((1, 128), lambda i: (0, 0)), pl.BlockSpec((1, 128), lambda i: (0, 0))],
        out_specs=[pl.BlockSpec((None, N_HEADS, CHUNK), lambda i: (i, 0, 0))] * 2
        + [pl.BlockSpec((CHUNK, SSM_W), lambda i: (i, 0))] * 2,
        out_shape=[jax.ShapeDtypeStruct((nchunks, N_HEADS, CHUNK), F32)] * 2
        + [jax.ShapeDtypeStruct((t, SSM_W), F32)] * 2,
        compiler_params=_params("parallel"),
    )(proj, bias, alog)


def dt_bwd(ddtT, dacT, dtT, proj, dt_bias, a_log, t, name):
    nchunks = t // CHUNK
    bias = dt_bias.reshape(N_HEADS, 1)
    alog = a_log.reshape(N_HEADS, 1)

    def body(ddt_ref, dac_ref, dt_ref, raw_ref, b_ref, al_ref, draw_ref, da_ref, db_ref):
        a = -jnp.exp(al_ref[...])
        upp = _tri((CHUNK, CHUNK), lambda r, c: r >= c).astype(BF16)
        dadt = sum(_dot(part, upp) for part in _split3(dac_ref[...]))
        ddt = ddt_ref[...] + dadt * a
        rawT = raw_ref[...].astype(F32).T[:N_HEADS]
        draw = ddt * _sigmoid(rawT + b_ref[...])
        padded = jnp.concatenate([draw, jnp.zeros((128 - N_HEADS, CHUNK), F32)], axis=0)
        draw_ref[...] = padded.T.astype(BF16)
        dav = dadt * dt_ref[...]

        @pl.when(pl.program_id(0) == 0)
        def _():
            da_ref[...] = dav
            db_ref[...] = draw

        @pl.when(pl.program_id(0) > 0)
        def _():
            da_ref[...] += dav
            db_ref[...] += draw

    draw, da, db = pl.pallas_call(
        body, name=name, grid=(nchunks,),
        in_specs=[pl.BlockSpec((None, N_HEADS, CHUNK), lambda i: (i, 0, 0))] * 3
        + [pl.BlockSpec((CHUNK, 128), lambda i: (i, OFF_DT // 128)),
           pl.BlockSpec((N_HEADS, 1), lambda i: (0, 0)), pl.BlockSpec((N_HEADS, 1), lambda i: (0, 0))],
        out_specs=[pl.BlockSpec((CHUNK, 128), lambda i: (i, 0)),
                   pl.BlockSpec((N_HEADS, CHUNK), lambda i: (0, 0)), pl.BlockSpec((N_HEADS, CHUNK), lambda i: (0, 0))],
        out_shape=[jax.ShapeDtypeStruct((t, 128), BF16), jax.ShapeDtypeStruct((N_HEADS, CHUNK), F32),
                   jax.ShapeDtypeStruct((N_HEADS, CHUNK), F32)],
        compiler_params=_params("arbitrary"),
    )(ddtT, dacT, dtT, proj, bias, alog)
    return draw, jnp.sum(da, axis=1), jnp.sum(db, axis=1)


PAIRS_G = N_PAIRS // N_GROUPS
GROUP_W = PAIRS_G * 128


def _ssd_pair(x, dtl, acl, acr, tri):
    left = lax.broadcasted_iota(jnp.int32, (CHUNK, 128), 1) < 64
    swapped = pltpu.roll(acl, 64, axis=1)
    ac_cols = [jnp.where(left, acl, swapped), jnp.where(left, swapped, acl)]
    dks = [jnp.exp(jnp.where(tri, ac_cols[e] - acr[e:e + 1], -1e30)) for e in range(2)]
    aclast = acl[CHUNK - 1:CHUNK, :]
    return left, dtl, acl, x * dtl, dks, aclast


def _ssd_specs(nc, rev):
    row = (lambda b, c, g: b * nc + (nc - 1 - c)) if rev else (lambda b, c, g: b * nc + c)
    return dict(
        wide=pl.BlockSpec((CHUNK, GROUP_W), lambda b, c, g: (row(b, c, g), g)),
        bmat=pl.BlockSpec((CHUNK, 128), lambda b, c, g: (row(b, c, g), SSM_W // 128 + g)),
        cmat=pl.BlockSpec((CHUNK, 128), lambda b, c, g: (row(b, c, g), SSM_W // 128 + N_GROUPS + g)),
        rows2=pl.BlockSpec((None, PAIRS_G, 2, CHUNK), lambda b, c, g: (row(b, c, g), g, 0, 0)),
        cols=pl.BlockSpec((CHUNK, GROUP_W), lambda b, c, g: (row(b, c, g), g)),
        rows8=pl.BlockSpec((None, PAIRS_G, 8, CHUNK), lambda b, c, g: (row(b, c, g), g, 0, 0)),
        dskip=pl.BlockSpec((1, GROUP_W), lambda b, c, g: (0, g)),
        state=pl.BlockSpec((None, PAIRS_G, N_STATE, 128), lambda b, c, g: (row(b, c, g), g, 0, 0)),
        narrow=pl.BlockSpec((CHUNK, 128), lambda b, c, g: (row(b, c, g), g)))


def ssd_fwd(xa, acT, dtC, acC, dskip_l, bl, s, name):
    t = bl * s
    nc = s // CHUNK
    ac4 = acT.reshape(bl * nc, N_PAIRS, 2, CHUNK)

    def body(x_ref, b_ref, c_ref, ac_ref, dtc_ref, acc_ref, dsk_ref, y_ref, prev_ref, st_ref):
        c = pl.program_id(1)
        g = pl.program_id(2)
        bm = b_ref[...]
        cm = c_ref[...]
        cb = _dot(cm, bm, _NT)
        tri = _tri((CHUNK, CHUNK), lambda r, c: r >= c)

        @pl.when(c == 0)
        def _():
            for p in range(PAIRS_G):
                st_ref[g * PAIRS_G + p] = jnp.zeros((N_STATE, 128), F32)

        for p in range(PAIRS_G):
            hp = g * PAIRS_G + p
            cs = slice(p * 128, (p + 1) * 128)
            x = x_ref[:, cs].astype(F32)
            left, dtl, acl, xdt, dks, aclast = _ssd_pair(x, dtc_ref[:, cs], acc_ref[:, cs], ac_ref[p], tri)
            xdt_b = xdt.astype(BF16)
            ys = [_dot((cb * dks[e]).astype(BF16), xdt_b) for e in range(2)]
            st = st_ref[hp]
            y_off = _dot(cm, st.astype(BF16)) * jnp.exp(acl)
            y_ref[:, cs] = (jnp.where(left, ys[0], ys[1]) + y_off + x * dsk_ref[:, cs]).astype(BF16)
            xw = (xdt * jnp.exp(aclast - acl)).astype(BF16)
            prev_ref[p] = st
            st_ref[hp] = st * jnp.exp(aclast) + _dot(bm, xw, _TN)

    sp = _ssd_specs(nc, False)
    return pl.pallas_call(
        body, name=name, grid=(bl, nc, N_GROUPS),
        in_specs=[sp["wide"], sp["bmat"], sp["cmat"], sp["rows2"], sp["cols"], sp["cols"], sp["dskip"]],
        out_specs=[sp["wide"], sp["state"]],
        out_shape=[jax.ShapeDtypeStruct((t, SSM_W), BF16),
                   jax.ShapeDtypeStruct((bl * nc, N_PAIRS, N_STATE, 128), F32)],
        scratch_shapes=[pltpu.VMEM((N_PAIRS, N_STATE, 128), F32)],
        compiler_params=_params("parallel", "arbitrary", "arbitrary"),
    )(xa, xa, xa, ac4, dtC, acC, dskip_l)


def ssd_bwd(dy, xa, acT, dtC, acC, dskip_l, prev, bl, s, name):
    t = bl * s
    nc = s // CHUNK
    ac4 = acT.reshape(bl * nc, N_PAIRS, 2, CHUNK)

    def body(dy_ref, x_ref, b_ref, c_ref, ac_ref, dtc_ref, acc_ref, dsk_ref, prev_ref,
             dx_ref, db_ref, dc_ref, dd_ref, dsk_out_ref, dp_ref):
        b = pl.program_id(0)
        cr = pl.program_id(1)
        g = pl.program_id(2)

        @pl.when(cr == 0)
        def _():
            for p in range(PAIRS_G):
                dp_ref[g * PAIRS_G + p] = jnp.zeros((N_STATE, 128), F32)

        @pl.when((b == 0) & (cr == 0) & (g == 0))
        def _():
            dsk_out_ref[...] = jnp.zeros(dsk_out_ref.shape, F32)

        bm = b_ref[...]
        cm = c_ref[...]
        cb = _dot(cm, bm, _NT)
        tri = _tri((CHUNK, CHUNK), lambda r, c: r >= c)
        lane = lax.broadcasted_iota(jnp.int32, (CHUNK, 128), 1)
        lrow = lax.broadcasted_iota(jnp.int32, (1, CHUNK), 1)
        krow = lax.broadcasted_iota(jnp.int32, (8, CHUNK), 0)
        dcb = jnp.zeros((CHUNK, CHUNK), F32)
        dc_acc = jnp.zeros((CHUNK, N_STATE), F32)
        db_acc = jnp.zeros((CHUNK, N_STATE), F32)
        for p in range(PAIRS_G):
            hp = g * PAIRS_G + p
            cs = slice(p * 128, (p + 1) * 128)
            x = x_ref[:, cs].astype(F32)
            left, dtl, acl, xdt, dks, aclast = _ssd_pair(x, dtc_ref[:, cs], acc_ref[:, cs], ac_ref[p], tri)
            dyv = dy_ref[:, cs].astype(F32)
            dy_b = dyv.astype(BF16)
            xdt_b = xdt.astype(BF16)
            st = prev_ref[p]
            st_b = st.astype(BF16)
            ea = jnp.exp(acl)
            ds = jnp.exp(aclast - acl)
            cdl = jnp.exp(aclast)
            xw = xdt * ds
            masks = [left, jnp.logical_not(left)]

            dsk_out_ref[hp] = dsk_out_ref[hp] + jnp.sum(dyv * x, axis=0, keepdims=True)

            yo = _dot(cm, st_b)
            dyo_b = (dyv * ea).astype(BF16)
            yoff_term = dyv * yo * ea
            dc_acc = dc_acc + _dot(dyo_b, st_b, _NT)
            dst = _dot(cm, dyo_b, _TN)
            dsv = dp_ref[hp]
            dsv_b = dsv.astype(BF16)
            dxw = _dot(bm, dsv_b)
            db_acc = db_acc + _dot(xw.astype(BF16), dsv_b, _NT)
            dxdt = dxw * ds
            qv = dxw * xw
            end_term = dsv * st * cdl
            dp_ref[hp] = dsv * cdl + dst

            cols = jnp.zeros((CHUNK, 128), F32)
            rows = []
            for e in range(2):
                m = cb * dks[e]
                dy_e = jnp.where(masks[e], dyv, 0.0).astype(BF16)
                dm = _dot(dy_e, xdt_b, _NT)
                w = dm * m
                dcb = dcb + dm * dks[e]
                dxdt = dxdt + jnp.where(masks[e], _dot(m.astype(BF16), dy_b, _TN), 0.0)
                dac_col = jnp.sum(w + jnp.where(masks[e], yoff_term - qv, 0.0), axis=1, keepdims=True)
                cols = jnp.where(lane == 2 + e, dac_col, cols)
                tail = jnp.sum(jnp.where(masks[e], qv + end_term, 0.0))
                rows.append(jnp.where(lrow == CHUNK - 1, tail, 0.0) - jnp.sum(w, axis=0, keepdims=True))
            dx_ref[:, cs] = (dxdt * dtl + dyv * dsk_ref[:, cs]).astype(BF16)
            ddt_l = dxdt * x
            for e in range(2):
                cols = jnp.where(lane == e, jnp.sum(jnp.where(masks[e], ddt_l, 0.0), axis=1, keepdims=True), cols)
            dd_ref[p] = cols.T[0:8] + jnp.where(krow == 2, rows[0], 0.0) + jnp.where(krow == 3, rows[1], 0.0)
        dcb_b = dcb.astype(BF16)
        dc_ref[...] = dc_acc + _dot(dcb_b, bm)
        db_ref[...] = db_acc + _dot(dcb_b, cm, _TN)

    sp = _ssd_specs(nc, True)
    dx, db, dc, dd, dsk = pl.pallas_call(
        body, name=name, grid=(bl, nc, N_GROUPS),
        in_specs=[sp["wide"], sp["wide"], sp["bmat"], sp["cmat"], sp["rows2"], sp["cols"], sp["cols"], sp["dskip"],
                  sp["state"]],
        out_specs=[sp["wide"], sp["narrow"], sp["narrow"], sp["rows8"],
                   pl.BlockSpec((N_PAIRS, 1, 128), lambda b, c, g: (0, 0, 0))],
        out_shape=[jax.ShapeDtypeStruct((t, SSM_W), BF16),
                   jax.ShapeDtypeStruct((t, N_GROUPS * N_STATE), F32),
                   jax.ShapeDtypeStruct((t, N_GROUPS * N_STATE), F32),
                   jax.ShapeDtypeStruct((bl * nc, N_PAIRS, 8, CHUNK), F32),
                   jax.ShapeDtypeStruct((N_PAIRS, 1, 128), F32)],
        scratch_shapes=[pltpu.VMEM((N_PAIRS, N_STATE, 128), F32)],
        compiler_params=_params("arbitrary", "arbitrary", "arbitrary"),
    )(dy, xa, xa, xa, ac4, dtC, acC, dskip_l, prev)
    ddtT = dd[:, :, 0:2, :].reshape(bl * nc, N_HEADS, CHUNK)
    dacT = dd[:, :, 2:4, :].reshape(bl * nc, N_HEADS, CHUNK)
    return dx, db, dc, ddtT, dacT, dsk.reshape(N_PAIRS, 128)


def gnorm_fwd(y, proj, w, name):
    t = y.shape[0]
    tr = min(256, t)
    zb = OFF_Z // SSM_W

    def body(y_ref, z_ref, w_ref, o_ref):
        z = z_ref[...].astype(F32)
        yg = y_ref[...].astype(F32) * z * _sigmoid(z)
        r = lax.rsqrt(jnp.mean(yg * yg, axis=-1, keepdims=True) + EPS)
        o_ref[...] = (yg * r * w_ref[...]).astype(BF16)

    return pl.pallas_call(
        body, name=name, grid=(t // tr,),
        in_specs=[pl.BlockSpec((tr, SSM_W), lambda i: (i, 0)), pl.BlockSpec((tr, SSM_W), lambda i: (i, zb)),
                  pl.BlockSpec((1, SSM_W), lambda i: (0, 0))],
        out_specs=pl.BlockSpec((tr, SSM_W), lambda i: (i, 0)),
        out_shape=jax.ShapeDtypeStruct((t, SSM_W), BF16),
        compiler_params=_params("parallel"),
    )(y, proj, w.reshape(1, SSM_W))


def gnorm_bwd(ds, y, proj, w, name):
    t = y.shape[0]
    tr = min(256, t)
    zb = OFF_Z // SSM_W

    def body(ds_ref, y_ref, z_ref, w_ref, dy_ref, dz_ref, dw_ref):
        z = z_ref[...].astype(F32)
        yv = y_ref[...].astype(F32)
        sg = _sigmoid(z)
        sz = z * sg
        yg = yv * sz
        r = lax.rsqrt(jnp.mean(yg * yg, axis=-1, keepdims=True) + EPS)
        xh = yg * r
        g = ds_ref[...].astype(F32)
        dxh = g * w_ref[...]
        dyg = r * (dxh - xh * jnp.mean(dxh * xh, axis=-1, keepdims=True))
        dy_ref[...] = (dyg * sz).astype(BF16)
        dz_ref[...] = (dyg * yv * sg * (1.0 + z * (1.0 - sg))).astype(BF16)
        part = jnp.sum(g * xh, axis=0, keepdims=True)

        @pl.when(pl.program_id(0) == 0)
        def _():
            dw_ref[...] = part

        @pl.when(pl.program_id(0) > 0)
        def _():
            dw_ref[...] += part

    return pl.pallas_call(
        body, name=name, grid=(t // tr,),
        in_specs=[pl.BlockSpec((tr, SSM_W), lambda i: (i, 0)), pl.BlockSpec((tr, SSM_W), lambda i: (i, 0)),
                  pl.BlockSpec((tr, SSM_W), lambda i: (i, zb)), pl.BlockSpec((1, SSM_W), lambda i: (0, 0))],
        out_specs=[pl.BlockSpec((tr, SSM_W), lambda i: (i, 0)), pl.BlockSpec((tr, SSM_W), lambda i: (i, 0)),
                   pl.BlockSpec((1, SSM_W), lambda i: (0, 0))],
        out_shape=[jax.ShapeDtypeStruct((t, SSM_W), BF16), jax.ShapeDtypeStruct((t, SSM_W), BF16),
                   jax.ShapeDtypeStruct((1, SSM_W), F32)],
        compiler_params=_params("arbitrary"),
    )(ds, y, proj, w.reshape(1, SSM_W))


def _pool_mixed(u, g, row):
    win = 2 << g
    acc = u
    for k in range(g + 1):
        acc = acc + _shift_down(acc, 1 << k, row)
    inv = 1.0 / jnp.minimum(row + 1, win).astype(F32)
    return acc * inv - u, inv


def pool_fwd(proj, pool_w, pool_scale, bl, s, name):
    t = bl * s

    def body(u_ref, g_ref, w_ref, sc_ref, o_ref):
        row = lax.broadcasted_iota(jnp.int32, (s, POOL_GD), 0)
        for g in range(POOL_G):
            cs = slice(g * POOL_GD, (g + 1) * POOL_GD)
            u = u_ref[:, cs].astype(F32)
            mixed, _ = _pool_mixed(u, g, row)
            pm = _dot(mixed.astype(BF16), w_ref[g])
            gate = g_ref[:, cs].astype(F32)
            o_ref[:, cs] = (pm * sc_ref[:, cs] * gate * _sigmoid(gate)).astype(BF16)

    return pl.pallas_call(
        body, name=name, grid=(bl,),
        in_specs=[pl.BlockSpec((s, POOL_W), lambda b: (b, OFF_PU // POOL_W)),
                  pl.BlockSpec((s, POOL_W), lambda b: (b, OFF_PG // POOL_W)),
                  pl.BlockSpec((POOL_G, POOL_GD, POOL_GD), lambda b: (0, 0, 0)),
                  pl.BlockSpec((1, POOL_W), lambda b: (0, 0))],
        out_specs=pl.BlockSpec((s, POOL_W), lambda b: (b, 0)),
        out_shape=jax.ShapeDtypeStruct((t, POOL_W), BF16),
        compiler_params=_params("parallel"),
    )(proj, proj, pool_w, pool_scale.reshape(1, POOL_W))


def pool_bwd(dp, proj, pool_w, pool_scale, bl, s, name):
    t = bl * s

    def body(dp_ref, u_ref, g_ref, w_ref, sc_ref, du_ref, dg_ref, dw_ref, dsc_ref):
        row = lax.broadcasted_iota(jnp.int32, (s, POOL_GD), 0)
        first = pl.program_id(0) == 0
        for g in range(POOL_G):
            cs = slice(g * POOL_GD, (g + 1) * POOL_GD)
            u = u_ref[:, cs].astype(F32)
            mixed, inv = _pool_mixed(u, g, row)
            mixed_b = mixed.astype(BF16)
            wg = w_ref[g]
            pm = _dot(mixed_b, wg)
            gate = g_ref[:, cs].astype(F32)
            sg = _sigmoid(gate)
            d = dp_ref[:, cs].astype(F32)
            sc = sc_ref[:, cs]
            dpm = (d * sc * gate * sg).astype(BF16)
            dg_ref[:, cs] = (d * pm * sc * sg * (1.0 + gate * (1.0 - sg))).astype(BF16)
            dsc = jnp.sum(d * pm * gate * sg, axis=0, keepdims=True)
            dwg = _dot(mixed_b, dpm, _TN)
            dmixed = _dot(dpm, wg, _NT)
            acc = dmixed * inv
            for k in range(g + 1):
                acc = acc + _shift_up(acc, 1 << k, row, s)
            du_ref[:, cs] = (acc - dmixed).astype(BF16)

            @pl.when(first)
            def _():
                dw_ref[g] = dwg
                dsc_ref[:, cs] = dsc

            @pl.when(jnp.logical_not(first))
            def _():
                dw_ref[g] = dw_ref[g] + dwg
                dsc_ref[:, cs] = dsc_ref[:, cs] + dsc

    return pl.pallas_call(
        body, name=name, grid=(bl,),
        in_specs=[pl.BlockSpec((s, POOL_W), lambda b: (b, 0)),
                  pl.BlockSpec((s, POOL_W), lambda b: (b, OFF_PU // POOL_W)),
                  pl.BlockSpec((s, POOL_W), lambda b: (b, OFF_PG // POOL_W)),
                  pl.BlockSpec((POOL_G, POOL_GD, POOL_GD), lambda b: (0, 0, 0)),
                  pl.BlockSpec((1, POOL_W), lambda b: (0, 0))],
        out_specs=[pl.BlockSpec((s, POOL_W), lambda b: (b, 0)), pl.BlockSpec((s, POOL_W), lambda b: (b, 0)),
                   pl.BlockSpec((POOL_G, POOL_GD, POOL_GD), lambda b: (0, 0, 0)),
                   pl.BlockSpec((1, POOL_W), lambda b: (0, 0))],
        out_shape=[jax.ShapeDtypeStruct((t, POOL_W), BF16), jax.ShapeDtypeStruct((t, POOL_W), BF16),
                   jax.ShapeDtypeStruct((POOL_G, POOL_GD, POOL_GD), F32), jax.ShapeDtypeStruct((1, POOL_W), F32)],
        compiler_params=_params("arbitrary"),
    )(dp, proj, proj, pool_w, pool_scale.reshape(1, POOL_W))


SB_SCALE = 64 ** -0.5


KB = 256


def _sb_block(qe, kj, mask, rr, upper):
    z = _dot(qe, kj, _NT)
    lb = jnp.minimum(z, 0.0) - jnp.log(1.0 + jnp.exp(-jnp.abs(z)))
    lom = lb - z if mask is None else jnp.where(mask, lb - z, 0.0)
    later = _dot(lom.astype(BF16), upper) + rr
    return lb, lom, later


def _sb_masks(i):
    lane = lax.broadcasted_iota(jnp.int32, (QB, 128), 1)
    row = lax.broadcasted_iota(jnp.int32, (2 * QB, KB), 0) % QB
    col = lax.broadcasted_iota(jnp.int32, (2 * QB, KB), 1)
    causal = lambda jb: col + (jb * KB - i * QB) < row
    return lane, lane < 64, causal


def _stack_heads(x, left):
    zero = jnp.zeros_like(x)
    return jnp.concatenate([jnp.where(left, x, zero), jnp.where(left, zero, x)], axis=0)


SB_GROUP = 4
SB_GW = SB_GROUP * 128


def sb_fwd(proj, bl, s, name, comm=None):
    t = bl * s
    nq = s // QB
    qb0, kb0, vb0, gb0 = OFF_QKV // SB_GW, (OFF_QKV + SB_W) // SB_GW, (OFF_QKV + 2 * SB_W) // SB_GW, OFF_SBG // SB_GW
    grid = (bl, SB_PAIRS // SB_GROUP, nq)
    x_in, x_args, x_out, x_shape, x_scratch, x_start, x_wait = _hosted_exchange(
        comm, grid, relay_at=(bl - 1, SB_PAIRS // SB_GROUP - 1, 0))

    def body(*refs):
        q_ref, k_ref, v_ref, g_ref = refs[:4]
        og_ref, o_ref, r_ref = refs[4 + len(x_in):7 + len(x_in)]
        x_refs = refs[4:4 + len(x_in)] + refs[7 + len(x_in):]
        x_start(x_refs)
        i = pl.program_id(2)
        lane, left, causal = _sb_masks(i)
        upper = _tri((KB, KB), lambda r, c: r > c).astype(BF16)
        cols = [slice(p * 128, (p + 1) * 128) for p in range(SB_GROUP)]
        qcats = [_stack_heads(q_ref[:, cs] * SB_SCALE, left) for cs in cols]
        zero = qcats[0].astype(F32) * 0.0

        def block(jb, carry, diagonal):
            rows = pl.ds(pl.multiple_of(jb * KB, KB), KB)
            mask = causal(jb) if diagonal else None
            out = []
            for p, cs in enumerate(cols):
                acc, rr, rt = carry[p]
                lb, lom, later = _sb_block(qcats[p], k_ref[rows, cs], mask, rr, upper)
                att = jnp.exp(lb + later)
                if diagonal:
                    att = jnp.where(mask, att, 0.0)
                acc = acc + _dot(att.astype(BF16), v_ref[rows, cs])
                rt = jnp.where(lane == jb, rr[:QB], jnp.where(lane == 8 + jb, rr[QB:], rt))
                out.append((acc, rr + jnp.sum(lom, axis=1, keepdims=True), rt))
            return tuple(out)

        carry = block(i, tuple((zero, zero[:, :1], zero[:QB]) for _ in cols), True)
        carry = lax.fori_loop(0, i, lambda jj, c: block(i - 1 - jj, c, False), carry)
        for p, cs in enumerate(cols):
            acc, _, rtile = carry[p]
            o = jnp.where(left, acc[:QB], acc[QB:])
            gate = g_ref[:, cs].astype(F32)
            o_ref[:, cs] = o.astype(BF16)
            og_ref[:, cs] = (o * gate * _sigmoid(gate)).astype(BF16)
            r_ref[p] = rtile
        x_wait(x_refs)

    rowblk = lambda b, g, i: (b * nq + i, g)
    return pl.pallas_call(
        body, name=name, grid=grid,
        in_specs=[pl.BlockSpec((QB, SB_GW), lambda b, g, i: (b * nq + i, qb0 + g)),
                  pl.BlockSpec((s, SB_GW), lambda b, g, i: (b, kb0 + g)),
                  pl.BlockSpec((s, SB_GW), lambda b, g, i: (b, vb0 + g)),
                  pl.BlockSpec((QB, SB_GW), lambda b, g, i: (b * nq + i, gb0 + g))] + x_in,
        out_specs=[pl.BlockSpec((QB, SB_GW), rowblk), pl.BlockSpec((QB, SB_GW), rowblk),
                   pl.BlockSpec((None, SB_GROUP, QB, 128), lambda b, g, i: (b * nq + i, g, 0, 0))] + x_out,
        out_shape=[jax.ShapeDtypeStruct((t, SB_W), BF16), jax.ShapeDtypeStruct((t, SB_W), BF16),
                   jax.ShapeDtypeStruct((bl * nq, SB_PAIRS, QB, 128), F32)] + x_shape,
        scratch_shapes=x_scratch,
        compiler_params=_params("arbitrary", "arbitrary", "arbitrary"),
    )(proj, proj, proj, proj, *x_args)


def sb_bwd(dsb, o, rsave, proj, bl, s, name, comm=None):
    t = bl * s
    nq = s // QB
    qb0, kb0, vb0, gb0 = OFF_QKV // SB_GW, (OFF_QKV + SB_W) // SB_GW, (OFF_QKV + 2 * SB_W) // SB_GW, OFF_SBG // SB_GW
    grid = (bl, SB_PAIRS // SB_GROUP, nq)
    x_in, x_args, x_out, x_shape, x_scratch, x_start, x_wait = _hosted_exchange(
        comm, grid, relay_at=(bl - 1, SB_PAIRS // SB_GROUP - 1, 0))

    def body(*refs):
        n = len(x_in)
        d_ref, o_ref, r_ref, q_ref, k_ref, v_ref, g_ref = refs[:7]
        dq_ref, dk_ref, dv_ref, dg_ref = refs[7 + n:11 + n]
        dk_acc, dv_acc = refs[11 + 2 * n:13 + 2 * n]
        x_refs = refs[7:7 + n] + refs[11 + n:11 + 2 * n] + refs[13 + 2 * n:]
        x_start(x_refs)
        i = pl.program_id(2)

        @pl.when(i == 0)
        def _():
            dk_acc[...] = jnp.zeros(dk_acc.shape, F32)
            dv_acc[...] = jnp.zeros(dv_acc.shape, F32)

        lane, left, causal = _sb_masks(i)
        upper = _tri((KB, KB), lambda r, c: r > c).astype(BF16)
        lower = _tri((KB, KB), lambda r, c: r < c).astype(BF16)
        cols = [slice(p * 128, (p + 1) * 128) for p in range(SB_GROUP)]
        qcats, docats = [], []
        for cs in cols:
            gate = g_ref[:, cs].astype(F32)
            sg = _sigmoid(gate)
            d = d_ref[:, cs].astype(F32)
            dg_ref[:, cs] = (d * o_ref[:, cs].astype(F32) * sg * (1.0 + gate * (1.0 - sg))).astype(BF16)
            docats.append(_stack_heads((d * gate * sg).astype(BF16), left))
            qcats.append(_stack_heads(q_ref[:, cs] * SB_SCALE, left))
        qcat_ts = [qc.astype(F32).T.astype(BF16) for qc in qcats]
        docat_ts = [dc.astype(F32).T.astype(BF16) for dc in docats]
        zero = qcats[0].astype(F32) * 0.0

        def block(jb, carry, diagonal):
            rows = pl.ds(pl.multiple_of(jb * KB, KB), KB)
            mask = causal(jb) if diagonal else None
            out = []
            for p, cs in enumerate(cols):
                dq, gcar = carry[p]
                kj = k_ref[rows, cs]
                vj = v_ref[rows, cs]
                rtile = r_ref[p]
                rr = jnp.concatenate(
                    [jnp.sum(jnp.where(lane == jb, rtile, 0.0), axis=1, keepdims=True),
                     jnp.sum(jnp.where(lane == 8 + jb, rtile, 0.0), axis=1, keepdims=True)], axis=0)
                lb, lom, later = _sb_block(qcats[p], kj, mask, rr, upper)
                att = jnp.exp(lb + later)
                if diagonal:
                    att = jnp.where(mask, att, 0.0)
                de = att * _dot(docats[p], vj, _NT)
                gpre = _dot(de.astype(BF16), lower) + gcar
                sig = jnp.exp(lb)
                dz = de * (1.0 - sig) - gpre * sig
                if diagonal:
                    dz = jnp.where(mask, dz, 0.0)
                dz = dz.astype(BF16)
                dk_acc[jb, cs, :] = dk_acc[jb, cs, :] + _dot(qcat_ts[p], dz)
                dv_acc[jb, cs, :] = dv_acc[jb, cs, :] + _dot(docat_ts[p], att.astype(BF16))
                out.append((dq + _dot(dz, kj), gcar + jnp.sum(de, axis=1, keepdims=True)))
            return tuple(out)

        carry = lax.fori_loop(0, i, lambda jb, c: block(jb, c, False), tuple((zero, zero[:, :1]) for _ in cols))
        carry = block(i, carry, True)
        for p, cs in enumerate(cols):
            dq = carry[p][0]
            dq_ref[:, cs] = (jnp.where(left, dq[:QB], dq[QB:]) * SB_SCALE).astype(BF16)

        @pl.when(i == nq - 1)
        def _():
            for kb in range(s // KB):
                for cs in cols:
                    dk_ref[kb * KB:(kb + 1) * KB, cs] = dk_acc[kb, cs, :].T.astype(BF16)
                    dv_ref[kb * KB:(kb + 1) * KB, cs] = dv_acc[kb, cs, :].T.astype(BF16)

        x_wait(x_refs)

    rowblk = lambda b, g, i: (b * nq + i, g)
    seqblk = lambda b, g, i: (b, g)
    return pl.pallas_call(
        body, name=name, grid=grid,
        in_specs=[pl.BlockSpec((QB, SB_GW), rowblk), pl.BlockSpec((QB, SB_GW), rowblk),
                  pl.BlockSpec((None, SB_GROUP, QB, 128), lambda b, g, i: (b * nq + i, g, 0, 0)),
                  pl.BlockSpec((QB, SB_GW), lambda b, g, i: (b * nq + i, qb0 + g)),
                  pl.BlockSpec((s, SB_GW), lambda b, g, i: (b, kb0 + g)),
                  pl.BlockSpec((s, SB_GW), lambda b, g, i: (b, vb0 + g)),
                  pl.BlockSpec((QB, SB_GW), lambda b, g, i: (b * nq + i, gb0 + g))] + x_in,
        out_specs=[pl.BlockSpec((QB, SB_GW), rowblk), pl.BlockSpec((s, SB_GW), seqblk),
                   pl.BlockSpec((s, SB_GW), seqblk), pl.BlockSpec((QB, SB_GW), rowblk)] + x_out,
        out_shape=[jax.ShapeDtypeStruct((t, SB_W), BF16)] * 4 + x_shape,
        scratch_shapes=[pltpu.VMEM((s // KB, SB_GW, KB), F32), pltpu.VMEM((s // KB, SB_GW, KB), F32)] + x_scratch,
        compiler_params=_params("arbitrary", "arbitrary", "arbitrary"),
    )(dsb, o, rsave, proj, proj, proj, proj, *x_args)


def merge_fwd(proj, ys, yp, yb, name):
    t = ys.shape[0]
    tr = min(512, t)

    def body(m_ref, ys_ref, yp_ref, yb_ref, o_ref):
        acc = jnp.zeros((tr, D), F32)
        for k, ref in enumerate((ys_ref, yp_ref, yb_ref)):
            acc = acc + _sigmoid(m_ref[:, k * D:(k + 1) * D].astype(F32)) * ref[...].astype(F32)
        o_ref[...] = acc.astype(BF16)

    rowblk = pl.BlockSpec((tr, D), lambda i: (i, 0))
    return pl.pallas_call(
        body, name=name, grid=(t // tr,),
        in_specs=[pl.BlockSpec((tr, 3 * D), lambda i: (i, 0)), rowblk, rowblk, rowblk],
        out_specs=rowblk,
        out_shape=jax.ShapeDtypeStruct((t, D), BF16),
        compiler_params=_params("parallel"),
    )(proj, ys, yp, yb)


def merge_bwd(dm, proj, ys, yp, yb, name):
    t = ys.shape[0]
    tr = min(512, t)

    def body(dm_ref, m_ref, ys_ref, yp_ref, yb_ref, d0_ref, d1_ref, d2_ref, dl_ref):
        dmv = dm_ref[...].astype(F32)
        for k, (ref, dref) in enumerate(((ys_ref, d0_ref), (yp_ref, d1_ref), (yb_ref, d2_ref))):
            g = _sigmoid(m_ref[:, k * D:(k + 1) * D].astype(F32))
            dref[...] = (g * dmv).astype(BF16)
            dl_ref[:, k * D:(k + 1) * D] = (dmv * ref[...].astype(F32) * g * (1.0 - g)).astype(BF16)

    rowblk = pl.BlockSpec((tr, D), lambda i: (i, 0))
    wide = pl.BlockSpec((tr, 3 * D), lambda i: (i, 0))
    return pl.pallas_call(
        body, name=name, grid=(t // tr,),
        in_specs=[rowblk, wide, rowblk, rowblk, rowblk],
        out_specs=[rowblk, rowblk, rowblk, wide],
        out_shape=[jax.ShapeDtypeStruct((t, D), BF16)] * 3 + [jax.ShapeDtypeStruct((t, 3 * D), BF16)],
        compiler_params=_params("parallel"),
    )(dm, proj, ys, yp, yb)


def layer_fwd(x, lw, bl, s, tag, comm=None):
    t = bl * s
    h = rmsnorm_fwd(x, lw["norm_w"], f"norm_fwd{tag}")
    proj = matmul(h, lw["w_in"], "nt", BF16, f"in_proj{tag}", tn=2048)
    xa = conv_fwd(proj, lw["conv_w"], lw["conv_b"], bl, s, f"conv_fwd{tag}")
    dtT, acT, dtC, acC = dt_fwd(proj, lw["dt_bias"], lw["a_log"], t, f"dt_fwd{tag}")
    dskip_l = jnp.repeat(lw["d_skip"], 64).reshape(1, SSM_W)
    y, prev = ssd_fwd(xa, acT, dtC, acC, dskip_l, bl, s, f"ssd_fwd{tag}")
    s_out = gnorm_fwd(y, proj, lw["ssm_norm_w"], f"gnorm_fwd{tag}")
    p_out = pool_fwd(proj, lw["pool_w"], lw["pool_scale"], bl, s, f"pool_fwd{tag}")
    sb_out, sb_o, sb_r, *carried = sb_fwd(proj, bl, s, f"sb_fwd{tag}", comm)
    ys = matmul(s_out, lw["w_proj_ssm"], "nn", BF16, f"proj_ssm{tag}")
    yp = matmul(p_out, lw["w_proj_pool"], "nn", BF16, f"proj_pool{tag}")
    yb = matmul(sb_out, lw["w_proj_sb"], "nn", BF16, f"proj_sb{tag}")
    merged = merge_fwd(proj, ys, yp, yb, f"merge_fwd{tag}")
    x_next = matmul(merged, lw["w_out"], "nn", F32, f"out_proj{tag}", residual=x)
    saved = dict(x=x, h=h, proj=proj, xa=xa, dtT=dtT, acT=acT, dtC=dtC, acC=acC, y=y, prev=prev, s_out=s_out, p_out=p_out,
                 sb_out=sb_out, sb_o=sb_o, sb_r=sb_r, ys=ys, yp=yp, yb=yb, merged=merged)
    return x_next, saved, (carried[0] if carried else None)


def layer_bwd(dx, dx_b, lw, sv, bl, s, tag, comm=None, own_slabs=None):
    t = bl * s
    g = {}
    dmerged = matmul(dx_b, lw["w_out"], "nt", BF16, f"d_merged{tag}")
    g["w_out"] = matmul(sv["merged"], dx_b, "tn", BF16, f"dw_out{tag}")
    dys, dyp, dyb, dlogit = merge_bwd(dmerged, sv["proj"], sv["ys"], sv["yp"], sv["yb"], f"merge_bwd{tag}")
    ds_out = matmul(dys, lw["w_proj_ssm"], "nt", BF16, f"d_sout{tag}")
    g["w_proj_ssm"] = matmul(sv["s_out"], dys, "tn", BF16, f"dw_proj_ssm{tag}")
    dp_out = matmul(dyp, lw["w_proj_pool"], "nt", BF16, f"d_pout{tag}")
    g["w_proj_pool"] = matmul(sv["p_out"], dyp, "tn", BF16, f"dw_proj_pool{tag}")
    dsb_out = matmul(dyb, lw["w_proj_sb"], "nt", BF16, f"d_sbout{tag}")
    g["w_proj_sb"] = matmul(sv["sb_out"], dyb, "tn", BF16, f"dw_proj_sb{tag}")
    dy, dz, dnw = gnorm_bwd(ds_out, sv["y"], sv["proj"], lw["ssm_norm_w"], f"gnorm_bwd{tag}")
    g["ssm_norm_w"] = dnw[0]
    dskip_l = jnp.repeat(lw["d_skip"], 64).reshape(1, SSM_W)
    dxs, db, dc, ddtT, dacT, dsk = ssd_bwd(dy, sv["xa"], sv["acT"], sv["dtC"], sv["acC"], dskip_l, sv["prev"], bl, s,
                                           f"ssd_bwd{tag}")
    g["d_skip"] = jnp.sum(dsk.reshape(N_HEADS, 64), axis=1)
    ddt_raw, da, dbias = dt_bwd(ddtT, dacT, sv["dtT"], sv["proj"], lw["dt_bias"], lw["a_log"], t, f"dt_bwd{tag}")
    g["a_log"] = da * (-jnp.exp(lw["a_log"]))
    g["dt_bias"] = dbias
    dxa = jnp.concatenate([dxs, db.astype(BF16), dc.astype(BF16)], axis=1)
    dxbc, dcw, dcb = conv_bwd(dxa, sv["proj"], lw["conv_w"], lw["conv_b"], bl, s, f"conv_bwd{tag}")
    g["conv_w"] = dcw
    g["conv_b"] = dcb
    dpu, dpg, dpw, dpsc = pool_bwd(dp_out, sv["proj"], lw["pool_w"], lw["pool_scale"], bl, s, f"pool_bwd{tag}")
    g["pool_w"] = dpw
    g["pool_scale"] = dpsc[0]
    dq, dk, dv, dsbg, *carried = sb_bwd(dsb_out, sv["sb_o"], sv["sb_r"], sv["proj"], bl, s, f"sb_bwd{tag}", comm)
    dproj = concat_columns([dlogit, dsbg, dpu, dpg, dz, dq, dk, dv, dxbc, ddt_raw], PC, f"d_proj{tag}")
    g["w_in"] = matmul(dproj, sv["h"], "tn", BF16, f"dw_in{tag}", tk=2048)
    own_got = None
    if own_slabs is None:
        dh = matmul(dproj, lw["w_in"], "nn", F32, f"d_h{tag}", tk=2048)
    else:
        dh, own_got = matmul(dproj, lw["w_in"], "nn", F32, f"d_h{tag}", tk=2048, comm=(own_slabs(g), "chips"))
    dx_in, dx_in_b, dnorm = rmsnorm_bwd(dh, sv["x"], lw["norm_w"], dx, f"norm_bwd{tag}")
    g["norm_w"] = dnorm[0]
    return dx_in, dx_in_b, g, (carried[0] if carried else None), own_got


def concat_columns(parts, width, name):
    t = parts[0].shape[0]
    tr = min(256, t)
    widths = [p.shape[1] for p in parts]
    used = sum(widths)

    def body(*refs):
        o_ref = refs[-1]
        off = 0
        for ref, w in zip(refs[:-1], widths):
            o_ref[:, off:off + w] = ref[...]
            off += w
        if width > used:
            o_ref[:, used:] = jnp.zeros((tr, width - used), BF16)

    return pl.pallas_call(
        body, name=name, grid=(t // tr,),
        in_specs=[pl.BlockSpec((tr, w), lambda i: (i, 0)) for w in widths],
        out_specs=pl.BlockSpec((tr, width), lambda i: (i, 0)),
        out_shape=jax.ShapeDtypeStruct((t, width), BF16),
        compiler_params=_params("parallel"),
    )(*parts)


_PAD_PIECES = ((10784, 3072), (9760, 1024), (4640, 1024), (5664, 1024), (0, 2048), (6688, 3072), (2048, 2560), (4608, 32))
_UNPAD_PIECES = ((OFF_Z, 2048), (OFF_XBC, 2560), (OFF_DT, 32), (OFF_PU, 1024), (OFF_PG, 1024), (OFF_QKV, 3072),
                 (OFF_SBG, 1024), (OFF_MERGE, 3072))


def pad_rows(wt):
    pieces = [wt[o:o + n] for o, n in _PAD_PIECES]
    return jnp.concatenate(pieces + [jnp.zeros((PC - IN_COLS, wt.shape[1]), wt.dtype)], axis=0)


def unpad_rows(wp):
    return jnp.concatenate([wp[o:o + n] for o, n in _UNPAD_PIECES], axis=0)


MESH = pl.DeviceIdType.MESH
ANY = pl.BlockSpec(memory_space=pl.ANY)


def _coords():
    return lax.axis_index("x"), lax.axis_index("y"), lax.axis_index("c")


def _peer(p):
    x, y, c = _coords()
    return (1 - x if p & 4 else x, 1 - y if p & 2 else y, 1 - c if p & 1 else c)


def _flat(pos):
    return 4 * pos[0] + 2 * pos[1] + pos[2]


def _chip(pos):
    return 2 * pos[0] + pos[1]


def _exchange_copies(v_ref, out_ref, send_sems, recv_sems, local_sem, mode):
    x, y, c = _coords()
    me = _flat((x, y, c))
    sibling = (x, y, 1 - c)
    chips = [(1 - x if j & 2 else x, 1 - y if j & 1 else y) for j in range(1, 4)]

    def copy(k, src, landing, to):
        return pltpu.make_async_remote_copy(src_ref=src, dst_ref=out_ref.at[landing], send_sem=send_sems.at[k],
                                            recv_sem=recv_sems.at[k], device_id=to, device_id_type=MESH)

    if mode == "direct":
        local = pltpu.make_async_copy(v_ref, out_ref.at[me], local_sem)
        first = [copy(p - 1, v_ref, me, _peer(p)) for p in range(1, N_DEV)]
        last = [copy(p - 1, v_ref, _flat(_peer(p)), _peer(p)) for p in range(1, N_DEV)]
        return local, first, [], last
    if mode == "gather":
        local = pltpu.make_async_copy(v_ref, out_ref.at[me], local_sem)
        first = [copy(0, v_ref, me, sibling)] + [copy(1 + j, v_ref, me, (*ch, c)) for j, ch in enumerate(chips)]
        relay = [(copy(1 + j, v_ref, _flat((*ch, c)), (*ch, c)),
                  copy(4 + j, out_ref.at[_flat((*ch, c))], _flat((*ch, c)), sibling)) for j, ch in enumerate(chips)]
        last = [copy(0, v_ref, _flat(sibling), sibling)] + [
            copy(4 + j, v_ref, _flat((*ch, 1 - c)), sibling) for j, ch in enumerate(chips)]
        return local, first, relay, last
    assert mode == "chips"
    mine = _chip((x, y))
    local = pltpu.make_async_copy(v_ref.at[mine], out_ref.at[mine], local_sem)
    first = [copy(j, v_ref.at[_chip(ch)], mine, (*ch, c)) for j, ch in enumerate(chips)]
    last = [copy(j, v_ref.at[mine], _chip(ch), (*ch, c)) for j, ch in enumerate(chips)]
    return local, first, [], last


def _exchange_start(*refs_and_mode):
    local, first, _, _ = _exchange_copies(*refs_and_mode)
    local.start()
    for cp in first:
        cp.start()


def _exchange_relay(*refs_and_mode):
    for arrival, onward in _exchange_copies(*refs_and_mode)[2]:
        arrival.wait_recv()
        onward.start()


def _exchange_finish(*refs_and_mode):
    local, first, relay, last = _exchange_copies(*refs_and_mode)
    for cp in last:
        cp.wait_recv()
    for cp in first + [onward for _, onward in relay]:
        cp.wait_send()
    local.wait()


def _exchange_shape(v, mode):
    return jax.ShapeDtypeStruct(tuple(v.shape) if mode == "chips" else (N_DEV,) + tuple(v.shape), v.dtype)


def _exchange_sems():
    return [pltpu.SemaphoreType.DMA((N_DEV - 1,)), pltpu.SemaphoreType.DMA((N_DEV - 1,)), pltpu.SemaphoreType.DMA]


def exchange(v, mode, name):
    def body(*refs):
        _exchange_start(*refs, mode)
        _exchange_relay(*refs, mode)
        _exchange_finish(*refs, mode)

    return pl.pallas_call(
        body, name=name,
        in_specs=[ANY], out_specs=ANY,
        out_shape=_exchange_shape(v, mode),
        scratch_shapes=_exchange_sems(),
    )(v)


def _hosted_exchange(comm, grid, relay_at):
    if comm is None:
        return [], [], [], [], [], (lambda refs: None), (lambda refs: None)
    v, mode = comm

    def at(step):
        cond = None
        for axis, want in enumerate(step):
            term = pl.program_id(axis) == want
            cond = term if cond is None else jnp.logical_and(cond, term)
        return cond

    def start(refs):
        @pl.when(at([0] * len(grid)))
        def _():
            _exchange_start(*refs, mode)

        if mode == "gather":
            @pl.when(at(relay_at))
            def _():
                _exchange_relay(*refs, mode)

    def wait(refs):
        @pl.when(at([n - 1 for n in grid]))
        def _():
            _exchange_finish(*refs, mode)

    return [ANY], [v], [ANY], [_exchange_shape(v, mode)], _exchange_sems(), start, wait


def pair_swap(s0, s1, name):
    def body(s0_ref, s1_ref, out_ref, send_sems, recv_sems):
        x, y, c = _coords()

        def copies(src_ref):
            return [pltpu.make_async_remote_copy(src_ref=src_ref.at[j], dst_ref=out_ref.at[j], send_sem=send_sems.at[j],
                                                 recv_sem=recv_sems.at[j], device_id=(x, y, 1 - c), device_id_type=MESH)
                    for j in range(4)]

        @pl.when(c == 0)
        def _():
            for cp in copies(s1_ref):
                cp.start()

        @pl.when(c == 1)
        def _():
            for cp in copies(s0_ref):
                cp.start()

        for cp in copies(s0_ref):
            cp.wait_recv()
        for cp in copies(s0_ref):
            cp.wait_send()

    return pl.pallas_call(
        body, name=name,
        in_specs=[ANY, ANY], out_specs=ANY,
        out_shape=jax.ShapeDtypeStruct(s0.shape, s0.dtype),
        scratch_shapes=[pltpu.SemaphoreType.DMA((4,)), pltpu.SemaphoreType.DMA((4,))],
    )(s0, s1)


def pair_sum(s0, s1, got, name):
    _, r, c = s0.shape
    tr = r // 4 if r % 64 == 0 else r

    def body(s0_ref, s1_ref, got_ref, o_ref):
        mine = jnp.where(lax.axis_index("c") == 0, s0_ref[...].astype(F32), s1_ref[...].astype(F32))
        o_ref[...] = (mine + got_ref[...].astype(F32)).astype(BF16)

    blk = pl.BlockSpec((None, tr, c), lambda j, i: (j, i, 0))
    return pl.pallas_call(
        body, name=name, grid=(4, r // tr),
        in_specs=[blk, blk, blk], out_specs=blk,
        out_shape=jax.ShapeDtypeStruct(s0.shape, BF16),
        compiler_params=_params("parallel", "parallel"),
    )(s0, s1, got)


def sum_slabs(v, name):
    n, r, c = v.shape
    tr = 128 if r % 128 == 0 else r

    def body(v_ref, o_ref):
        acc = v_ref[0].astype(F32)
        for k in range(1, n):
            acc = acc + v_ref[k].astype(F32)
        o_ref[...] = acc

    return pl.pallas_call(
        body, name=name, grid=(r // tr,),
        in_specs=[pl.BlockSpec((n, tr, c), lambda i: (0, i, 0))],
        out_specs=pl.BlockSpec((tr, c), lambda i: (i, 0)),
        out_shape=jax.ShapeDtypeStruct((r, c), F32),
        compiler_params=_params("parallel"),
    )(v)


def adamw(w, g, m, v, name):
    r, c = w.shape
    tr = next((cand for cand in (256, 128, 64, 32, 16, 8) if r % cand == 0), r)

    def body(w_ref, g_ref, m_ref, v_ref, d_ref, mo_ref, vo_ref):
        gv = g_ref[...]
        mn = ADAM_B1 * m_ref[...] + (1.0 - ADAM_B1) * gv
        vn = ADAM_B2 * v_ref[...] + (1.0 - ADAM_B2) * (gv * gv)
        m_hat = mn / (1.0 - ADAM_B1 ** ADAM_STEP)
        v_hat = vn / (1.0 - ADAM_B2 ** ADAM_STEP)
        d_ref[...] = -ADAM_LR * (m_hat / (jnp.sqrt(v_hat) + ADAM_EPS) + ADAM_WD * w_ref[...])
        mo_ref[...] = mn
        vo_ref[...] = vn

    blk = pl.BlockSpec((tr, c), lambda i: (i, 0))
    return pl.pallas_call(
        body, name=name, grid=(r // tr,),
        in_specs=[blk] * 4, out_specs=[blk] * 3,
        out_shape=[jax.ShapeDtypeStruct((r, c), F32)] * 3,
        compiler_params=_params("parallel"),
    )(w, g, m, v)


BIG = ("w_proj_ssm", "w_proj_pool", "w_proj_sb", "w_out", "pool_w", "w_in")
SHARD_IN = IN_COLS // N_DEV
BIG_ROWS = {"w_proj_ssm": SSM_W // N_DEV, "w_proj_pool": POOL_W // N_DEV, "w_proj_sb": SB_W // N_DEV,
            "w_out": D // N_DEV, "pool_w": POOL_G * (POOL_GD // N_DEV) * POOL_GD // D, "w_in": SHARD_IN}
PACK_C = D
PACK_R = 2432

REPLICATED = ("norm_w", "conv_b", "dt_bias", "a_log", "d_skip", "ssm_norm_w", "pool_scale")
WEIGHTS = ("norm_w", "w_in", "conv_w", "conv_b", "dt_bias", "a_log", "d_skip", "ssm_norm_w", "pool_w",
           "pool_scale", "w_proj_ssm", "w_proj_pool", "w_proj_sb", "w_out", "final_norm_w")


def _size(shape):
    n = 1
    for d in shape:
        n *= d
    return n


def _pad_flat(flat, n):
    return jnp.concatenate([flat, jnp.zeros((n - flat.shape[0],), flat.dtype)])


def _row_offsets():
    offs, off = {}, 0
    for n in BIG:
        offs[n] = off
        off += BIG_ROWS[n]
    return offs, off


def pack_shards(parts):
    rows = [parts[n].reshape(BIG_ROWS[n], PACK_C) for n in BIG]
    rows[-1] = jnp.pad(rows[-1], ((0, PACK_R - _row_offsets()[1]), (0, 0)))
    return jnp.concatenate(rows, axis=0)


def unpack_shards(packed):
    offs, _ = _row_offsets()
    out = {}
    for n in BIG:
        seg = packed[offs[n]:offs[n] + BIG_ROWS[n]]
        if n == "w_in":
            out[n] = seg.T
        elif n == "pool_w":
            out[n] = seg.reshape(POOL_G, POOL_GD // N_DEV, POOL_GD)
        else:
            out[n] = seg
    return out


def unpack_gathered(g):
    offs, _ = _row_offsets()
    out = {}
    for n in BIG:
        seg = g[:, offs[n]:offs[n] + BIG_ROWS[n], :]
        if n == "w_in":
            out[n] = pad_rows(seg.reshape(IN_COLS, D))
        elif n == "pool_w":
            out[n] = seg.reshape(N_DEV, POOL_G, POOL_GD // N_DEV, POOL_GD).transpose(1, 0, 2, 3).reshape(
                POOL_G, POOL_GD, POOL_GD)
        else:
            out[n] = seg.reshape(N_DEV * BIG_ROWS[n], D)
    return out


def pack_slabs(g):
    segs = []
    for n in BIG:
        if n == "w_in":
            w = unpad_rows(g[n])
        elif n == "pool_w":
            w = g[n].reshape(POOL_G, N_DEV, POOL_GD // N_DEV, POOL_GD).transpose(1, 0, 2, 3)
        else:
            w = g[n]
        segs.append(w.reshape(N_DEV // 2, 2, BIG_ROWS[n], PACK_C).astype(BF16))
    segs[-1] = jnp.pad(segs[-1], ((0, 0), (0, 0), (0, PACK_R - _row_offsets()[1]), (0, 0)))
    return tuple(jnp.concatenate([seg[:, core] for seg in segs], axis=1) for core in range(2))


SMALL_ROWS = 544


def pack_small(vals):
    flat = jnp.concatenate([v.reshape(-1) for v in vals])
    return _pad_flat(flat, SMALL_ROWS * 128).reshape(SMALL_ROWS, 128)


def unpack_small(packed, shapes):
    flat = packed.reshape(-1)
    out, off = [], 0
    for shp in shapes:
        out.append(flat[off:off + _size(shp)].reshape(shp))
        off += _size(shp)
    return out


def kernel(x, norm_w, w_in, conv_w, conv_b, dt_bias, a_log, d_skip, ssm_norm_w, pool_w, pool_scale, w_proj_ssm, w_proj_pool, w_proj_sb, w_out, final_norm_w, loss_target, m_norm_w, m_w_in, m_conv_w, m_conv_b, m_dt_bias, m_a_log, m_d_skip, m_ssm_norm_w, m_pool_w, m_pool_scale, m_w_proj_ssm, m_w_proj_pool, m_w_proj_sb, m_w_out, m_final_norm_w, v_norm_w, v_w_in, v_conv_w, v_conv_b, v_dt_bias, v_a_log, v_d_skip, v_ssm_norm_w, v_pool_w, v_pool_scale, v_w_proj_ssm, v_w_proj_pool, v_w_proj_sb, v_w_out, v_final_norm_w):
    wts = dict(norm_w=norm_w, w_in=w_in, conv_w=conv_w, conv_b=conv_b, dt_bias=dt_bias, a_log=a_log, d_skip=d_skip,
               ssm_norm_w=ssm_norm_w, pool_w=pool_w, pool_scale=pool_scale, w_proj_ssm=w_proj_ssm,
               w_proj_pool=w_proj_pool, w_proj_sb=w_proj_sb, w_out=w_out, final_norm_w=final_norm_w)
    mom = dict(norm_w=m_norm_w, w_in=m_w_in, conv_w=m_conv_w, conv_b=m_conv_b, dt_bias=m_dt_bias, a_log=m_a_log,
               d_skip=m_d_skip, ssm_norm_w=m_ssm_norm_w, pool_w=m_pool_w, pool_scale=m_pool_scale,
               w_proj_ssm=m_w_proj_ssm, w_proj_pool=m_w_proj_pool, w_proj_sb=m_w_proj_sb, w_out=m_w_out,
               final_norm_w=m_final_norm_w)
    var = dict(norm_w=v_norm_w, w_in=v_w_in, conv_w=v_conv_w, conv_b=v_conv_b, dt_bias=v_dt_bias, a_log=v_a_log,
               d_skip=v_d_skip, ssm_norm_w=v_ssm_norm_w, pool_w=v_pool_w, pool_scale=v_pool_scale,
               w_proj_ssm=v_w_proj_ssm, w_proj_pool=v_w_proj_pool, w_proj_sb=v_w_proj_sb, w_out=v_w_out,
               final_norm_w=v_final_norm_w)
    bl, s, _ = x.shape
    t = bl * s
    me = _flat(_coords())

    cw = exchange(conv_w.reshape(40, 128), "direct", "gather_conv_w")
    conv_w_full = cw.reshape(N_DEV, DEPTH, CONV_K, CONV_CH // N_DEV).transpose(1, 2, 0, 3).reshape(
        DEPTH, CONV_K, CONV_CH)

    xc = x.reshape(t, D)
    layer_w, saved = [], []
    packed = [pack_shards({n: (wts[n][l].T if n == "w_in" else wts[n][l]).astype(BF16) for n in BIG})
              for l in range(DEPTH)]
    gathered = exchange(packed[0], "gather", "gather_w0")
    for l in range(DEPTH):
        lw = unpack_gathered(gathered)
        for n in REPLICATED:
            lw[n] = wts[n][l]
        lw["conv_w"] = conv_w_full[l]
        xc, sv, gathered = layer_fwd(xc, lw, bl, s, f"_l{l}", (packed[l + 1], "gather") if l + 1 < DEPTH else None)
        layer_w.append(lw)
        saved.append(sv)

    loss_part, dx, dx_b, dfinal = final_loss(xc, final_norm_w, loss_target.reshape(t, D), "final_loss")
    loss = lax.psum(loss_part[0, 0], ("x", "y", "c"))

    grads = [None] * DEPTH
    big_sum = [None] * DEPTH
    def chip_sums_of(g, l):
        s0, s1 = pack_slabs(g)
        return pair_sum(s0, s1, pair_swap(s0, s1, f"pair_swap{l}"), f"pair_sum{l}")

    chip_sums = None
    for l in reversed(range(DEPTH)):
        dx, dx_b, g, got, own_got = layer_bwd(
            dx, dx_b, layer_w[l], saved[l], bl, s, f"_l{l}", (chip_sums, "chips") if chip_sums is not None else None,
            (lambda g: chip_sums_of(g, 0)) if l == 0 else None)
        if got is not None:
            big_sum[l + 1] = unpack_shards(sum_slabs(got, f"sum_g{l + 1}"))
        grads[l] = g
        if l > 0:
            chip_sums = chip_sums_of(g, l)
    big_sum[0] = unpack_shards(sum_slabs(own_got, "sum_g0"))
    grad_x = dx.reshape(bl, s, D)

    small_names = REPLICATED + ("conv_w",)
    small_vals = [jnp.stack([grads[l][n] for l in range(DEPTH)]) for n in small_names] + [dfinal[0]]
    small_shapes = [v.shape for v in small_vals]
    small_all = exchange(pack_small(small_vals), "direct", "gather_small")
    small_sum = unpack_small(sum_slabs(small_all, "sum_small"), small_shapes)
    gsum = dict(zip(small_names + ("final_norm_w",), small_sum))
    conv_g_full = gsum["conv_w"]
    gsum["conv_w"] = lax.dynamic_slice_in_dim(conv_g_full, me * (CONV_CH // N_DEV), CONV_CH // N_DEV, axis=2)
    for n in BIG:
        gsum[n] = jnp.stack([big_sum[l][n] for l in range(DEPTH)])

    delta, new_m, new_v = {}, {}, {}
    for n in BIG + ("conv_w",):
        shp = wts[n].shape
        two_d = (-1, shp[-1])
        d2, m2, v2 = adamw(wts[n].reshape(two_d), gsum[n].reshape(two_d), mom[n].reshape(two_d),
                           var[n].reshape(two_d), f"adamw_{n}")
        delta[n], new_m[n], new_v[n] = d2.reshape(shp), m2.reshape(shp), v2.reshape(shp)
    rep = REPLICATED + ("final_norm_w",)
    rep_shapes = [wts[n].shape for n in rep]
    d2, m2, v2 = adamw(pack_small([wts[n] for n in rep]), pack_small([gsum[n] for n in rep]),
                       pack_small([mom[n] for n in rep]), pack_small([var[n] for n in rep]), "adamw_small")
    for n, dv, mv, vv in zip(rep, unpack_small(d2, rep_shapes), unpack_small(m2, rep_shapes),
                             unpack_small(v2, rep_shapes)):
        delta[n], new_m[n], new_v[n] = dv, mv, vv

    return (loss, grad_x, *[gsum[n] for n in WEIGHTS], *[delta[n] for n in WEIGHTS],
            *[new_m[n] for n in WEIGHTS], *[new_v[n] for n in WEIGHTS])
```

```python
import functools

import jax
import jax.numpy as jnp
from jax import lax
from jax.experimental import pallas as pl
from jax.experimental.pallas import tpu as pltpu

F32 = jnp.float32
BF16 = jnp.bfloat16

N_DEV = 8
DEPTH = 4
D = 1024
SSM_W = 2048
N_HEADS = 32
N_PAIRS = 16
N_GROUPS = 2
N_STATE = 128
CHUNK = 128
CONV_CH = 2560
CONV_K = 4
POOL_W = 1024
POOL_G = 4
POOL_GD = 256
SB_W = 1024
SB_PAIRS = 8
QB = 256
EPS = 1e-6
IN_COLS = 13856

PC = 14336
OFF_MERGE = 0
OFF_SBG = 3072
OFF_PU = 4096
OFF_PG = 5120
OFF_Z = 6144
OFF_QKV = 8192
OFF_XBC = 11264
OFF_DT = 13824

ADAM_LR = 0.001
ADAM_B1 = 0.9
ADAM_B2 = 0.999
ADAM_EPS = 1e-08
ADAM_WD = 0.01
ADAM_STEP = 10

VMEM_LIMIT = 56 * 1024 * 1024

_NN = (((1,), (0,)), ((), ()))
_NT = (((1,), (1,)), ((), ()))
_TN = (((0,), (0,)), ((), ()))


def _dot(a, b, dn=_NN):
    return lax.dot_general(a, b, dn, preferred_element_type=F32)


def _sigmoid(x):
    return 1.0 / (1.0 + jnp.exp(-x))


def _softplus(x):
    return jnp.maximum(x, 0.0) + jnp.log(1.0 + jnp.exp(-jnp.abs(x)))


def _split2(x):
    hi = x.astype(BF16)
    lo = (x - hi.astype(F32)).astype(BF16)
    return hi, lo


def _split3(x):
    hi = x.astype(BF16)
    r = x - hi.astype(F32)
    mid = r.astype(BF16)
    lo = (r - mid.astype(F32)).astype(BF16)
    return hi, mid, lo


def _params(*sem):
    return pltpu.CompilerParams(dimension_semantics=sem, vmem_limit_bytes=VMEM_LIMIT)


def matmul(a, b, mode, out_dtype, name, residual=None, tm=1024, tn=1024, tk=1024, comm=None):
    if mode == "nn":
        (m, k), (k2, n) = a.shape, b.shape
    elif mode == "nt":
        (m, k), (n, k2) = a.shape, b.shape
    else:
        (k, m), (k2, n) = a.shape, b.shape
    assert k == k2
    tm, tn, tk = min(tm, m), min(tn, n), min(tk, k)
    assert m % tm == 0 and n % tn == 0 and k % tk == 0
    nk = k // tk
    dn = {"nn": _NN, "nt": _NT, "tn": _TN}[mode]
    a_spec = pl.BlockSpec((tk, tm), lambda i, j, kk: (kk, i)) if mode == "tn" else pl.BlockSpec((tm, tk), lambda i, j, kk: (i, kk))
    b_spec = pl.BlockSpec((tn, tk), lambda i, j, kk: (j, kk)) if mode == "nt" else pl.BlockSpec((tk, tn), lambda i, j, kk: (kk, j))
    in_specs = [a_spec, b_spec]
    args = [a, b]
    if residual is not None:
        in_specs.append(pl.BlockSpec((tm, tn), lambda i, j, kk: (i, j)))
        args.append(residual)
    grid = (m // tm, n // tn, nk)
    n_in = len(args)
    x_in, x_args, x_out, x_shape, x_scratch, x_start, x_wait = _hosted_exchange(comm, grid, relay_at=None)

    def body(*refs):
        n_x = len(x_in)
        a_ref, b_ref = refs[:2]
        r_ref = refs[2] if residual is not None else None
        o_ref = refs[n_in + n_x]
        acc_ref = refs[n_in + 2 * n_x + 1]
        x_refs = refs[n_in:n_in + n_x] + refs[n_in + n_x + 1:n_in + 2 * n_x + 1] + refs[n_in + 2 * n_x + 2:]
        x_start(x_refs)
        kk = pl.program_id(2)
        p = _dot(a_ref[...], b_ref[...], dn)

        def finish(val):
            if r_ref is not None:
                val = val + r_ref[...]
            o_ref[...] = val.astype(out_dtype)

        if nk == 1:
            finish(p)
        else:
            @pl.when(kk == 0)
            def _():
                acc_ref[...] = p

            @pl.when(kk > 0)
            def _():
                acc_ref[...] += p

            @pl.when(kk == nk - 1)
            def _():
                finish(acc_ref[...])

        x_wait(x_refs)

    out = pl.pallas_call(
        body, name=name,
        grid=grid,
        in_specs=in_specs + x_in,
        out_specs=[pl.BlockSpec((tm, tn), lambda i, j, kk: (i, j))] + x_out,
        out_shape=[jax.ShapeDtypeStruct((m, n), out_dtype)] + x_shape,
        scratch_shapes=[pltpu.VMEM((tm, tn) if nk > 1 else (8, 128), F32)] + x_scratch,
        compiler_params=_params(*(("arbitrary",) * 3 if comm is not None else ("parallel", "parallel", "arbitrary"))),
    )(*args, *x_args)
    return tuple(out) if comm is not None else out[0]


def rmsnorm_fwd(x, w, name):
    t, d = x.shape
    tr = min(512, t)

    def body(x_ref, w_ref, h_ref):
        xv = x_ref[...]
        r = lax.rsqrt(jnp.mean(xv * xv, axis=-1, keepdims=True) + EPS)
        h_ref[...] = (xv * r * w_ref[...]).astype(BF16)

    return pl.pallas_call(
        body, name=name, grid=(t // tr,),
        in_specs=[pl.BlockSpec((tr, d), lambda i: (i, 0)), pl.BlockSpec((1, d), lambda i: (0, 0))],
        out_specs=pl.BlockSpec((tr, d), lambda i: (i, 0)),
        out_shape=jax.ShapeDtypeStruct((t, d), BF16),
        compiler_params=_params("parallel"),
    )(x, w.reshape(1, d))


def rmsnorm_bwd(dh, x, w, dres, name):
    t, d = x.shape
    tr = min(512, t)

    def body(dh_ref, x_ref, w_ref, dres_ref, dx_ref, dxb_ref, dw_ref):
        xv = x_ref[...]
        r = lax.rsqrt(jnp.mean(xv * xv, axis=-1, keepdims=True) + EPS)
        xh = xv * r
        g = dh_ref[...].astype(F32)
        dxh = g * w_ref[...]
        dxv = dres_ref[...] + r * (dxh - xh * jnp.mean(dxh * xh, axis=-1, keepdims=True))
        dx_ref[...] = dxv
        dxb_ref[...] = dxv.astype(BF16)
        part = jnp.sum(g * xh, axis=0, keepdims=True)

        @pl.when(pl.program_id(0) == 0)
        def _():
            dw_ref[...] = part

        @pl.when(pl.program_id(0) > 0)
        def _():
            dw_ref[...] += part

    return pl.pallas_call(
        body, name=name, grid=(t // tr,),
        in_specs=[pl.BlockSpec((tr, d), lambda i: (i, 0)), pl.BlockSpec((tr, d), lambda i: (i, 0)),
                  pl.BlockSpec((1, d), lambda i: (0, 0)), pl.BlockSpec((tr, d), lambda i: (i, 0))],
        out_specs=[pl.BlockSpec((tr, d), lambda i: (i, 0)), pl.BlockSpec((tr, d), lambda i: (i, 0)),
                   pl.BlockSpec((1, d), lambda i: (0, 0))],
        out_shape=[jax.ShapeDtypeStruct((t, d), F32), jax.ShapeDtypeStruct((t, d), BF16),
                   jax.ShapeDtypeStruct((1, d), F32)],
        compiler_params=_params("arbitrary"),
    )(dh, x, w.reshape(1, d), dres)


def final_loss(x, w, target, name):
    t, d = x.shape
    tr = min(512, t)

    def body(x_ref, w_ref, tg_ref, loss_ref, dx_ref, dxb_ref, dw_ref):
        xv = x_ref[...]
        r = lax.rsqrt(jnp.mean(xv * xv, axis=-1, keepdims=True) + EPS)
        xh = xv * r
        err = xh * w_ref[...] - tg_ref[...]
        lpart = 0.5 * jnp.sum(jnp.mean(err * err, axis=-1, keepdims=True), axis=0, keepdims=True)
        dy = err * (1.0 / d)
        dxh = dy * w_ref[...]
        dxv = r * (dxh - xh * jnp.mean(dxh * xh, axis=-1, keepdims=True))
        dx_ref[...] = dxv
        dxb_ref[...] = dxv.astype(BF16)
        part = jnp.sum(dy * xh, axis=0, keepdims=True)

        @pl.when(pl.program_id(0) == 0)
        def _():
            dw_ref[...] = part
            loss_ref[...] = jnp.broadcast_to(lpart, (1, 128))

        @pl.when(pl.program_id(0) > 0)
        def _():
            dw_ref[...] += part
            loss_ref[...] += jnp.broadcast_to(lpart, (1, 128))

    return pl.pallas_call(
        body, name=name, grid=(t // tr,),
        in_specs=[pl.BlockSpec((tr, d), lambda i: (i, 0)), pl.BlockSpec((1, d), lambda i: (0, 0)),
                  pl.BlockSpec((tr, d), lambda i: (i, 0))],
        out_specs=[pl.BlockSpec((1, 128), lambda i: (0, 0)), pl.BlockSpec((tr, d), lambda i: (i, 0)),
                   pl.BlockSpec((tr, d), lambda i: (i, 0)), pl.BlockSpec((1, d), lambda i: (0, 0))],
        out_shape=[jax.ShapeDtypeStruct((1, 128), F32), jax.ShapeDtypeStruct((t, d), F32),
                   jax.ShapeDtypeStruct((t, d), BF16), jax.ShapeDtypeStruct((1, d), F32)],
        compiler_params=_params("arbitrary"),
    )(x, w.reshape(1, d), target)


CONV_BW = 256


def _shift_down(u, s, row):
    return jnp.where(row >= s, pltpu.roll(u, s, axis=0), 0.0)


def _shift_up(u, s, row, n):
    return jnp.where(row < n - s, pltpu.roll(u, n - s, axis=0), 0.0)


def _conv_pre(u, w, b, row):
    acc = b + w[CONV_K - 1:CONV_K, :] * u
    for k in range(CONV_K - 1):
        acc = acc + w[k:k + 1, :] * _shift_down(u, CONV_K - 1 - k, row)
    return acc


def conv_fwd(proj, conv_w, conv_b, bl, s, name):
    t = bl * s
    nb = CONV_CH // CONV_BW
    off = OFF_XBC // CONV_BW

    def body(u_ref, w_ref, b_ref, o_ref):
        u = u_ref[...].astype(F32)
        row = lax.broadcasted_iota(jnp.int32, u.shape, 0)
        xc = _conv_pre(u, w_ref[...], b_ref[...], row)
        o_ref[...] = (xc * _sigmoid(xc)).astype(BF16)

    return pl.pallas_call(
        body, name=name, grid=(bl, nb),
        in_specs=[pl.BlockSpec((s, CONV_BW), lambda b, j: (b, off + j)),
                  pl.BlockSpec((CONV_K, CONV_BW), lambda b, j: (0, j)),
                  pl.BlockSpec((1, CONV_BW), lambda b, j: (0, j))],
        out_specs=pl.BlockSpec((s, CONV_BW), lambda b, j: (b, j)),
        out_shape=jax.ShapeDtypeStruct((t, CONV_CH), BF16),
        compiler_params=_params("parallel", "parallel"),
    )(proj, conv_w, conv_b.reshape(1, CONV_CH))


def conv_bwd(dxa, proj, conv_w, conv_b, bl, s, name):
    t = bl * s
    nb = CONV_CH // CONV_BW
    off = OFF_XBC // CONV_BW

    def body(d_ref, u_ref, w_ref, b_ref, du_ref, dw_ref, db_ref):
        u = u_ref[...].astype(F32)
        w = w_ref[...]
        row = lax.broadcasted_iota(jnp.int32, u.shape, 0)
        xc = _conv_pre(u, w, b_ref[...], row)
        sg = _sigmoid(xc)
        dxc = d_ref[...].astype(F32) * sg * (1.0 + xc * (1.0 - sg))
        du = w[CONV_K - 1:CONV_K, :] * dxc
        dws = [None] * CONV_K
        dws[CONV_K - 1] = jnp.sum(dxc * u, axis=0, keepdims=True)
        for k in range(CONV_K - 1):
            up = _shift_up(dxc, CONV_K - 1 - k, row, s)
            du = du + w[k:k + 1, :] * up
            dws[k] = jnp.sum(up * u, axis=0, keepdims=True)
        du_ref[...] = du.astype(BF16)
        krow = lax.broadcasted_iota(jnp.int32, (8, CONV_BW), 0)
        dwv = sum(jnp.where(krow == k, dws[k], 0.0) for k in range(CONV_K))
        dbv = jnp.sum(dxc, axis=0, keepdims=True)

        @pl.when(pl.program_id(1) == 0)
        def _():
            dw_ref[...] = dwv
            db_ref[...] = dbv

        @pl.when(pl.program_id(1) > 0)
        def _():
            dw_ref[...] += dwv
            db_ref[...] += dbv

    du, dw, db = pl.pallas_call(
        body, name=name, grid=(nb, bl),
        in_specs=[pl.BlockSpec((s, CONV_BW), lambda j, b: (b, j)),
                  pl.BlockSpec((s, CONV_BW), lambda j, b: (b, off + j)),
                  pl.BlockSpec((CONV_K, CONV_BW), lambda j, b: (0, j)),
                  pl.BlockSpec((1, CONV_BW), lambda j, b: (0, j))],
        out_specs=[pl.BlockSpec((s, CONV_BW), lambda j, b: (b, j)),
                   pl.BlockSpec((8, CONV_BW), lambda j, b: (0, j)),
                   pl.BlockSpec((1, CONV_BW), lambda j, b: (0, j))],
        out_shape=[jax.ShapeDtypeStruct((t, CONV_CH), BF16), jax.ShapeDtypeStruct((8, CONV_CH), F32),
                   jax.ShapeDtypeStruct((1, CONV_CH), F32)],
        compiler_params=_params("parallel", "arbitrary"),
    )(dxa, proj, conv_w, conv_b.reshape(1, CONV_CH))
    return du, dw[:CONV_K], db[0]


def _tri(shape, cmp):
    r = lax.broadcasted_iota(jnp.int32, shape, 0)
    c = lax.broadcasted_iota(jnp.int32, shape, 1)
    return cmp(r, c)


def dt_fwd(proj, dt_bias, a_log, t, name):
    nchunks = t // CHUNK
    bias = jnp.zeros((1, 128), F32).at[0, :N_HEADS].set(dt_bias)
    alog = jnp.zeros((1, 128), F32).at[0, :N_HEADS].set(a_log)

    def body(raw_ref, b_ref, al_ref, dt_ref, ac_ref, dtl_ref, acl_ref):
        raw = raw_ref[...].astype(F32)
        dt = _softplus(raw + b_ref[...])
        adt = dt * (-jnp.exp(al_ref[...]))
        low = _tri((CHUNK, CHUNK), lambda r, c: r >= c).astype(BF16)
        acum = sum(_dot(low, part) for part in _split3(adt))
        dt_ref[...] = dt.T[:N_HEADS]
        ac_ref[...] = acum.T[:N_HEADS]
        spread = _tri((128, SSM_W), lambda h, lane: lane // 64 == h).astype(BF16)
        dtl_ref[...] = sum(_dot(part, spread) for part in _split2(dt))
        acl_ref[...] = sum(_dot(part, spread) for part in _split3(acum))

    return pl.pallas_call(
        body, name=name, grid=(nchunks,),
        in_specs=[pl.BlockSpec((CHUNK, 128), lambda i: (i, OFF_DT // 128)),
                  pl.BlockSpec((1, 128), lambda i: (0, 0)), pl.BlockSpec((1, 128), lambda i: (0, 0))],
        out_specs=[pl.BlockSpec((None, N_HEADS, CHUNK), lambda i: (i, 0, 0))] * 2
        + [pl.BlockSpec((CHUNK, SSM_W), lambda i: (i, 0))] * 2,
        out_shape=[jax.ShapeDtypeStruct((nchunks, N_HEADS, CHUNK), F32)] * 2
        + [jax.ShapeDtypeStruct((t, SSM_W), F32)] * 2,
        compiler_params=_params("parallel"),
    )(proj, bias, alog)


def dt_bwd(ddtT, dacT, dtT, proj, dt_bias, a_log, t, name):
    nchunks = t // CHUNK
    bias = dt_bias.reshape(N_HEADS, 1)
    alog = a_log.reshape(N_HEADS, 1)

    def body(ddt_ref, dac_ref, dt_ref, raw_ref, b_ref, al_ref, draw_ref, da_ref, db_ref):
        a = -jnp.exp(al_ref[...])
        upp = _tri((CHUNK, CHUNK), lambda r, c: r >= c).astype(BF16)
        dadt = sum(_dot(part, upp) for part in _split3(dac_ref[...]))
        ddt = ddt_ref[...] + dadt * a
        rawT = raw_ref[...].astype(F32).T[:N_HEADS]
        draw = ddt * _sigmoid(rawT + b_ref[...])
        padded = jnp.concatenate([draw, jnp.zeros((128 - N_HEADS, CHUNK), F32)], axis=0)
        draw_ref[...] = padded.T.astype(BF16)
        dav = dadt * dt_ref[...]

        @pl.when(pl.program_id(0) == 0)
        def _():
            da_ref[...] = dav
            db_ref[...] = draw

        @pl.when(pl.program_id(0) > 0)
        def _():
            da_ref[...] += dav
            db_ref[...] += draw

    draw, da, db = pl.pallas_call(
        body, name=name, grid=(nchunks,),
        in_specs=[pl.BlockSpec((None, N_HEADS, CHUNK), lambda i: (i, 0, 0))] * 3
        + [pl.BlockSpec((CHUNK, 128), lambda i: (i, OFF_DT // 128)),
           pl.BlockSpec((N_HEADS, 1), lambda i: (0, 0)), pl.BlockSpec((N_HEADS, 1), lambda i: (0, 0))],
        out_specs=[pl.BlockSpec((CHUNK, 128), lambda i: (i, 0)),
                   pl.BlockSpec((N_HEADS, CHUNK), lambda i: (0, 0)), pl.BlockSpec((N_HEADS, CHUNK), lambda i: (0, 0))],
        out_shape=[jax.ShapeDtypeStruct((t, 128), BF16), jax.ShapeDtypeStruct((N_HEADS, CHUNK), F32),
                   jax.ShapeDtypeStruct((N_HEADS, CHUNK), F32)],
        compiler_params=_params("arbitrary"),
    )(ddtT, dacT, dtT, proj, bias, alog)
    return draw, jnp.sum(da, axis=1), jnp.sum(db, axis=1)


PAIRS_G = N_PAIRS // N_GROUPS
GROUP_W = PAIRS_G * 128


def _ssd_pair(x, dtl, acl, acr, tri):
    left = lax.broadcasted_iota(jnp.int32, (CHUNK, 128), 1) < 64
    swapped = pltpu.roll(acl, 64, axis=1)
    ac_cols = [jnp.where(left, acl, swapped), jnp.where(left, swapped, acl)]
    dks = [jnp.exp(jnp.where(tri, ac_cols[e] - acr[e:e + 1], -1e30)) for e in range(2)]
    aclast = acl[CHUNK - 1:CHUNK, :]
    return left, dtl, acl, x * dtl, dks, aclast


def _ssd_specs(nc, rev):
    row = (lambda b, c, g: b * nc + (nc - 1 - c)) if rev else (lambda b, c, g: b * nc + c)
    return dict(
        wide=pl.BlockSpec((CHUNK, GROUP_W), lambda b, c, g: (row(b, c, g), g)),
        bmat=pl.BlockSpec((CHUNK, 128), lambda b, c, g: (row(b, c, g), SSM_W // 128 + g)),
        cmat=pl.BlockSpec((CHUNK, 128), lambda b, c, g: (row(b, c, g), SSM_W // 128 + N_GROUPS + g)),
        rows2=pl.BlockSpec((None, PAIRS_G, 2, CHUNK), lambda b, c, g: (row(b, c, g), g, 0, 0)),
        cols=pl.BlockSpec((CHUNK, GROUP_W), lambda b, c, g: (row(b, c, g), g)),
        rows8=pl.BlockSpec((None, PAIRS_G, 8, CHUNK), lambda b, c, g: (row(b, c, g), g, 0, 0)),
        dskip=pl.BlockSpec((1, GROUP_W), lambda b, c, g: (0, g)),
        state=pl.BlockSpec((None, PAIRS_G, N_STATE, 128), lambda b, c, g: (row(b, c, g), g, 0, 0)),
        narrow=pl.BlockSpec((CHUNK, 128), lambda b, c, g: (row(b, c, g), g)))


def ssd_fwd(xa, acT, dtC, acC, dskip_l, bl, s, name):
    t = bl * s
    nc = s // CHUNK
    ac4 = acT.reshape(bl * nc, N_PAIRS, 2, CHUNK)

    def body(x_ref, b_ref, c_ref, ac_ref, dtc_ref, acc_ref, dsk_ref, y_ref, prev_ref, st_ref):
        c = pl.program_id(1)
        g = pl.program_id(2)
        bm = b_ref[...]
        cm = c_ref[...]
        cb = _dot(cm, bm, _NT)
        tri = _tri((CHUNK, CHUNK), lambda r, c: r >= c)

        @pl.when(c == 0)
        def _():
            for p in range(PAIRS_G):
                st_ref[g * PAIRS_G + p] = jnp.zeros((N_STATE, 128), F32)

        for p in range(PAIRS_G):
            hp = g * PAIRS_G + p
            cs = slice(p * 128, (p + 1) * 128)
            x = x_ref[:, cs].astype(F32)
            left, dtl, acl, xdt, dks, aclast = _ssd_pair(x, dtc_ref[:, cs], acc_ref[:, cs], ac_ref[p], tri)
            xdt_b = xdt.astype(BF16)
            ys = [_dot((cb * dks[e]).astype(BF16), xdt_b) for e in range(2)]
            st = st_ref[hp]
            y_off = _dot(cm, st.astype(BF16)) * jnp.exp(acl)
            y_ref[:, cs] = (jnp.where(left, ys[0], ys[1]) + y_off + x * dsk_ref[:, cs]).astype(BF16)
            xw = (xdt * jnp.exp(aclast - acl)).astype(BF16)
            prev_ref[p] = st
            st_ref[hp] = st * jnp.exp(aclast) + _dot(bm, xw, _TN)

    sp = _ssd_specs(nc, False)
    return pl.pallas_call(
        body, name=name, grid=(bl, nc, N_GROUPS),
        in_specs=[sp["wide"], sp["bmat"], sp["cmat"], sp["rows2"], sp["cols"], sp["cols"], sp["dskip"]],
        out_specs=[sp["wide"], sp["state"]],
        out_shape=[jax.ShapeDtypeStruct((t, SSM_W), BF16),
                   jax.ShapeDtypeStruct((bl * nc, N_PAIRS, N_STATE, 128), F32)],
        scratch_shapes=[pltpu.VMEM((N_PAIRS, N_STATE, 128), F32)],
        compiler_params=_params("parallel", "arbitrary", "arbitrary"),
    )(xa, xa, xa, ac4, dtC, acC, dskip_l)


def ssd_bwd(dy, xa, acT, dtC, acC, dskip_l, prev, bl, s, name):
    t = bl * s
    nc = s // CHUNK
    ac4 = acT.reshape(bl * nc, N_PAIRS, 2, CHUNK)

    def body(dy_ref, x_ref, b_ref, c_ref, ac_ref, dtc_ref, acc_ref, dsk_ref, prev_ref,
             dx_ref, db_ref, dc_ref, dd_ref, dsk_out_ref, dp_ref):
        b = pl.program_id(0)
        cr = pl.program_id(1)
        g = pl.program_id(2)

        @pl.when(cr == 0)
        def _():
            for p in range(PAIRS_G):
                dp_ref[g * PAIRS_G + p] = jnp.zeros((N_STATE, 128), F32)

        @pl.when((b == 0) & (cr == 0) & (g == 0))
        def _():
            dsk_out_ref[...] = jnp.zeros(dsk_out_ref.shape, F32)

        bm = b_ref[...]
        cm = c_ref[...]
        cb = _dot(cm, bm, _NT)
        tri = _tri((CHUNK, CHUNK), lambda r, c: r >= c)
        lane = lax.broadcasted_iota(jnp.int32, (CHUNK, 128), 1)
        lrow = lax.broadcasted_iota(jnp.int32, (1, CHUNK), 1)
        krow = lax.broadcasted_iota(jnp.int32, (8, CHUNK), 0)
        dcb = jnp.zeros((CHUNK, CHUNK), F32)
        dc_acc = jnp.zeros((CHUNK, N_STATE), F32)
        db_acc = jnp.zeros((CHUNK, N_STATE), F32)
        for p in range(PAIRS_G):
            hp = g * PAIRS_G + p
            cs = slice(p * 128, (p + 1) * 128)
            x = x_ref[:, cs].astype(F32)
            left, dtl, acl, xdt, dks, aclast = _ssd_pair(x, dtc_ref[:, cs], acc_ref[:, cs], ac_ref[p], tri)
            dyv = dy_ref[:, cs].astype(F32)
            dy_b = dyv.astype(BF16)
            xdt_b = xdt.astype(BF16)
            st = prev_ref[p]
            st_b = st.astype(BF16)
            ea = jnp.exp(acl)
            ds = jnp.exp(aclast - acl)
            cdl = jnp.exp(aclast)
            xw = xdt * ds
            masks = [left, jnp.logical_not(left)]

            dsk_out_ref[hp] = dsk_out_ref[hp] + jnp.sum(dyv * x, axis=0, keepdims=True)

            yo = _dot(cm, st_b)
            dyo_b = (dyv * ea).astype(BF16)
            yoff_term = dyv * yo * ea
            dc_acc = dc_acc + _dot(dyo_b, st_b, _NT)
            dst = _dot(cm, dyo_b, _TN)
            dsv = dp_ref[hp]
            dsv_b = dsv.astype(BF16)
            dxw = _dot(bm, dsv_b)
            db_acc = db_acc + _dot(xw.astype(BF16), dsv_b, _NT)
            dxdt = dxw * ds
            qv = dxw * xw
            end_term = dsv * st * cdl
            dp_ref[hp] = dsv * cdl + dst

            cols = jnp.zeros((CHUNK, 128), F32)
            rows = []
            for e in range(2):
                m = cb * dks[e]
                dy_e = jnp.where(masks[e], dyv, 0.0).astype(BF16)
                dm = _dot(dy_e, xdt_b, _NT)
                w = dm * m
                dcb = dcb + dm * dks[e]
                dxdt = dxdt + jnp.where(masks[e], _dot(m.astype(BF16), dy_b, _TN), 0.0)
                dac_col = jnp.sum(w + jnp.where(masks[e], yoff_term - qv, 0.0), axis=1, keepdims=True)
                cols = jnp.where(lane == 2 + e, dac_col, cols)
                tail = jnp.sum(jnp.where(masks[e], qv + end_term, 0.0))
                rows.append(jnp.where(lrow == CHUNK - 1, tail, 0.0) - jnp.sum(w, axis=0, keepdims=True))
            dx_ref[:, cs] = (dxdt * dtl + dyv * dsk_ref[:, cs]).astype(BF16)
            ddt_l = dxdt * x
            for e in range(2):
                cols = jnp.where(lane == e, jnp.sum(jnp.where(masks[e], ddt_l, 0.0), axis=1, keepdims=True), cols)
            dd_ref[p] = cols.T[0:8] + jnp.where(krow == 2, rows[0], 0.0) + jnp.where(krow == 3, rows[1], 0.0)
        dcb_b = dcb.astype(BF16)
        dc_ref[...] = dc_acc + _dot(dcb_b, bm)
        db_ref[...] = db_acc + _dot(dcb_b, cm, _TN)

    sp = _ssd_specs(nc, True)
    dx, db, dc, dd, dsk = pl.pallas_call(
        body, name=name, grid=(bl, nc, N_GROUPS),
        in_specs=[sp["wide"], sp["wide"], sp["bmat"], sp["cmat"], sp["rows2"], sp["cols"], sp["cols"], sp["dskip"],
                  sp["state"]],
        out_specs=[sp["wide"], sp["narrow"], sp["narrow"], sp["rows8"],
                   pl.BlockSpec((N_PAIRS, 1, 128), lambda b, c, g: (0, 0, 0))],
        out_shape=[jax.ShapeDtypeStruct((t, SSM_W), BF16),
                   jax.ShapeDtypeStruct((t, N_GROUPS * N_STATE), F32),
                   jax.ShapeDtypeStruct((t, N_GROUPS * N_STATE), F32),
                   jax.ShapeDtypeStruct((bl * nc, N_PAIRS, 8, CHUNK), F32),
                   jax.ShapeDtypeStruct((N_PAIRS, 1, 128), F32)],
        scratch_shapes=[pltpu.VMEM((N_PAIRS, N_STATE, 128), F32)],
        compiler_params=_params("arbitrary", "arbitrary", "arbitrary"),
    )(dy, xa, xa, xa, ac4, dtC, acC, dskip_l, prev)
    ddtT = dd[:, :, 0:2, :].reshape(bl * nc, N_HEADS, CHUNK)
    dacT = dd[:, :, 2:4, :].reshape(bl * nc, N_HEADS, CHUNK)
    return dx, db, dc, ddtT, dacT, dsk.reshape(N_PAIRS, 128)


def gnorm_fwd(y, proj, w, name):
    t = y.shape[0]
    tr = min(256, t)
    zb = OFF_Z // SSM_W

    def body(y_ref, z_ref, w_ref, o_ref):
        z = z_ref[...].astype(F32)
        yg = y_ref[...].astype(F32) * z * _sigmoid(z)
        r = lax.rsqrt(jnp.mean(yg * yg, axis=-1, keepdims=True) + EPS)
        o_ref[...] = (yg * r * w_ref[...]).astype(BF16)

    return pl.pallas_call(
        body, name=name, grid=(t // tr,),
        in_specs=[pl.BlockSpec((tr, SSM_W), lambda i: (i, 0)), pl.BlockSpec((tr, SSM_W), lambda i: (i, zb)),
                  pl.BlockSpec((1, SSM_W), lambda i: (0, 0))],
        out_specs=pl.BlockSpec((tr, SSM_W), lambda i: (i, 0)),
        out_shape=jax.ShapeDtypeStruct((t, SSM_W), BF16),
        compiler_params=_params("parallel"),
    )(y, proj, w.reshape(1, SSM_W))


def gnorm_bwd(ds, y, proj, w, name):
    t = y.shape[0]
    tr = min(256, t)
    zb = OFF_Z // SSM_W

    def body(ds_ref, y_ref, z_ref, w_ref, dy_ref, dz_ref, dw_ref):
        z = z_ref[...].astype(F32)
        yv = y_ref[...].astype(F32)
        sg = _sigmoid(z)
        sz = z * sg
        yg = yv * sz
        r = lax.rsqrt(jnp.mean(yg * yg, axis=-1, keepdims=True) + EPS)
        xh = yg * r
        g = ds_ref[...].astype(F32)
        dxh = g * w_ref[...]
        dyg = r * (dxh - xh * jnp.mean(dxh * xh, axis=-1, keepdims=True))
        dy_ref[...] = (dyg * sz).astype(BF16)
        dz_ref[...] = (dyg * yv * sg * (1.0 + z * (1.0 - sg))).astype(BF16)
        part = jnp.sum(g * xh, axis=0, keepdims=True)

        @pl.when(pl.program_id(0) == 0)
        def _():
            dw_ref[...] = part

        @pl.when(pl.program_id(0) > 0)
        def _():
            dw_ref[...] += part

    return pl.pallas_call(
        body, name=name, grid=(t // tr,),
        in_specs=[pl.BlockSpec((tr, SSM_W), lambda i: (i, 0)), pl.BlockSpec((tr, SSM_W), lambda i: (i, 0)),
                  pl.BlockSpec((tr, SSM_W), lambda i: (i, zb)), pl.BlockSpec((1, SSM_W), lambda i: (0, 0))],
        out_specs=[pl.BlockSpec((tr, SSM_W), lambda i: (i, 0)), pl.BlockSpec((tr, SSM_W), lambda i: (i, 0)),
                   pl.BlockSpec((1, SSM_W), lambda i: (0, 0))],
        out_shape=[jax.ShapeDtypeStruct((t, SSM_W), BF16), jax.ShapeDtypeStruct((t, SSM_W), BF16),
                   jax.ShapeDtypeStruct((1, SSM_W), F32)],
        compiler_params=_params("arbitrary"),
    )(ds, y, proj, w.reshape(1, SSM_W))


def _pool_mixed(u, g, row):
    win = 2 << g
    acc = u
    for k in range(g + 1):
        acc = acc + _shift_down(acc, 1 << k, row)
    inv = 1.0 / jnp.minimum(row + 1, win).astype(F32)
    return acc * inv - u, inv


def pool_fwd(proj, pool_w, pool_scale, bl, s, name):
    t = bl * s

    def body(u_ref, g_ref, w_ref, sc_ref, o_ref):
        row = lax.broadcasted_iota(jnp.int32, (s, POOL_GD), 0)
        for g in range(POOL_G):
            cs = slice(g * POOL_GD, (g + 1) * POOL_GD)
            u = u_ref[:, cs].astype(F32)
            mixed, _ = _pool_mixed(u, g, row)
            pm = _dot(mixed.astype(BF16), w_ref[g])
            gate = g_ref[:, cs].astype(F32)
            o_ref[:, cs] = (pm * sc_ref[:, cs] * gate * _sigmoid(gate)).astype(BF16)

    return pl.pallas_call(
        body, name=name, grid=(bl,),
        in_specs=[pl.BlockSpec((s, POOL_W), lambda b: (b, OFF_PU // POOL_W)),
                  pl.BlockSpec((s, POOL_W), lambda b: (b, OFF_PG // POOL_W)),
                  pl.BlockSpec((POOL_G, POOL_GD, POOL_GD), lambda b: (0, 0, 0)),
                  pl.BlockSpec((1, POOL_W), lambda b: (0, 0))],
        out_specs=pl.BlockSpec((s, POOL_W), lambda b: (b, 0)),
        out_shape=jax.ShapeDtypeStruct((t, POOL_W), BF16),
        compiler_params=_params("parallel"),
    )(proj, proj, pool_w, pool_scale.reshape(1, POOL_W))


def pool_bwd(dp, proj, pool_w, pool_scale, bl, s, name):
    t = bl * s

    def body(dp_ref, u_ref, g_ref, w_ref, sc_ref, du_ref, dg_ref, dw_ref, dsc_ref):
        row = lax.broadcasted_iota(jnp.int32, (s, POOL_GD), 0)
        first = pl.program_id(0) == 0
        for g in range(POOL_G):
            cs = slice(g * POOL_GD, (g + 1) * POOL_GD)
            u = u_ref[:, cs].astype(F32)
            mixed, inv = _pool_mixed(u, g, row)
            mixed_b = mixed.astype(BF16)
            wg = w_ref[g]
            pm = _dot(mixed_b, wg)
            gate = g_ref[:, cs].astype(F32)
            sg = _sigmoid(gate)
            d = dp_ref[:, cs].astype(F32)
            sc = sc_ref[:, cs]
            dpm = (d * sc * gate * sg).astype(BF16)
            dg_ref[:, cs] = (d * pm * sc * sg * (1.0 + gate * (1.0 - sg))).astype(BF16)
            dsc = jnp.sum(d * pm * gate * sg, axis=0, keepdims=True)
            dwg = _dot(mixed_b, dpm, _TN)
            dmixed = _dot(dpm, wg, _NT)
            acc = dmixed * inv
            for k in range(g + 1):
                acc = acc + _shift_up(acc, 1 << k, row, s)
            du_ref[:, cs] = (acc - dmixed).astype(BF16)

            @pl.when(first)
            def _():
                dw_ref[g] = dwg
                dsc_ref[:, cs] = dsc

            @pl.when(jnp.logical_not(first))
            def _():
                dw_ref[g] = dw_ref[g] + dwg
                dsc_ref[:, cs] = dsc_ref[:, cs] + dsc

    return pl.pallas_call(
        body, name=name, grid=(bl,),
        in_specs=[pl.BlockSpec((s, POOL_W), lambda b: (b, 0)),
                  pl.BlockSpec((s, POOL_W), lambda b: (b, OFF_PU // POOL_W)),
                  pl.BlockSpec((s, POOL_W), lambda b: (b, OFF_PG // POOL_W)),
                  pl.BlockSpec((POOL_G, POOL_GD, POOL_GD), lambda b: (0, 0, 0)),
                  pl.BlockSpec((1, POOL_W), lambda b: (0, 0))],
        out_specs=[pl.BlockSpec((s, POOL_W), lambda b: (b, 0)), pl.BlockSpec((s, POOL_W), lambda b: (b, 0)),
                   pl.BlockSpec((POOL_G, POOL_GD, POOL_GD), lambda b: (0, 0, 0)),
                   pl.BlockSpec((1, POOL_W), lambda b: (0, 0))],
        out_shape=[jax.ShapeDtypeStruct((t, POOL_W), BF16), jax.ShapeDtypeStruct((t, POOL_W), BF16),
                   jax.ShapeDtypeStruct((POOL_G, POOL_GD, POOL_GD), F32), jax.ShapeDtypeStruct((1, POOL_W), F32)],
        compiler_params=_params("arbitrary"),
    )(dp, proj, proj, pool_w, pool_scale.reshape(1, POOL_W))


SB_SCALE = 64 ** -0.5


KB = 256


def _sb_block(qe, kj, mask, rr, upper):
    z = _dot(qe, kj, _NT)
    lb = jnp.minimum(z, 0.0) - jnp.log(1.0 + jnp.exp(-jnp.abs(z)))
    lom = lb - z if mask is None else jnp.where(mask, lb - z, 0.0)
    later = _dot(lom.astype(BF16), upper) + rr
    return lb, lom, later


def _sb_masks(i):
    lane = lax.broadcasted_iota(jnp.int32, (QB, 128), 1)
    row = lax.broadcasted_iota(jnp.int32, (2 * QB, KB), 0) % QB
    col = lax.broadcasted_iota(jnp.int32, (2 * QB, KB), 1)
    causal = lambda jb: col + (jb * KB - i * QB) < row
    return lane, lane < 64, causal


def _stack_heads(x, left):
    zero = jnp.zeros_like(x)
    return jnp.concatenate([jnp.where(left, x, zero), jnp.where(left, zero, x)], axis=0)


SB_GROUP = 4
SB_GW = SB_GROUP * 128


def sb_fwd(proj, bl, s, name, comm=None):
    t = bl * s
    nq = s // QB
    qb0, kb0, vb0, gb0 = OFF_QKV // SB_GW, (OFF_QKV + SB_W) // SB_GW, (OFF_QKV + 2 * SB_W) // SB_GW, OFF_SBG // SB_GW
    grid = (bl, SB_PAIRS // SB_GROUP, nq)
    x_in, x_args, x_out, x_shape, x_scratch, x_start, x_wait = _hosted_exchange(
        comm, grid, relay_at=(bl - 1, SB_PAIRS // SB_GROUP - 1, 0))

    def body(*refs):
        q_ref, k_ref, v_ref, g_ref = refs[:4]
        og_ref, o_ref, r_ref = refs[4 + len(x_in):7 + len(x_in)]
        x_refs = refs[4:4 + len(x_in)] + refs[7 + len(x_in):]
        x_start(x_refs)
        i = pl.program_id(2)
        lane, left, causal = _sb_masks(i)
        upper = _tri((KB, KB), lambda r, c: r > c).astype(BF16)
        cols = [slice(p * 128, (p + 1) * 128) for p in range(SB_GROUP)]
        qcats = [_stack_heads(q_ref[:, cs] * SB_SCALE, left) for cs in cols]
        zero = qcats[0].astype(F32) * 0.0

        def block(jb, carry, diagonal):
            rows = pl.ds(pl.multiple_of(jb * KB, KB), KB)
            mask = causal(jb) if diagonal else None
            out = []
            for p, cs in enumerate(cols):
                acc, rr, rt = carry[p]
                lb, lom, later = _sb_block(qcats[p], k_ref[rows, cs], mask, rr, upper)
                att = jnp.exp(lb + later)
                if diagonal:
                    att = jnp.where(mask, att, 0.0)
                acc = acc + _dot(att.astype(BF16), v_ref[rows, cs])
                rt = jnp.where(lane == jb, rr[:QB], jnp.where(lane == 8 + jb, rr[QB:], rt))
                out.append((acc, rr + jnp.sum(lom, axis=1, keepdims=True), rt))
            return tuple(out)

        carry = block(i, tuple((zero, zero[:, :1], zero[:QB]) for _ in cols), True)
        carry = lax.fori_loop(0, i, lambda jj, c: block(i - 1 - jj, c, False), carry)
        for p, cs in enumerate(cols):
            acc, _, rtile = carry[p]
            o = jnp.where(left, acc[:QB], acc[QB:])
            gate = g_ref[:, cs].astype(F32)
            o_ref[:, cs] = o.astype(BF16)
            og_ref[:, cs] = (o * gate * _sigmoid(gate)).astype(BF16)
            r_ref[p] = rtile
        x_wait(x_refs)

    rowblk = lambda b, g, i: (b * nq + i, g)
    return pl.pallas_call(
        body, name=name, grid=grid,
        in_specs=[pl.BlockSpec((QB, SB_GW), lambda b, g, i: (b * nq + i, qb0 + g)),
                  pl.BlockSpec((s, SB_GW), lambda b, g, i: (b, kb0 + g)),
                  pl.BlockSpec((s, SB_GW), lambda b, g, i: (b, vb0 + g)),
                  pl.BlockSpec((QB, SB_GW), lambda b, g, i: (b * nq + i, gb0 + g))] + x_in,
        out_specs=[pl.BlockSpec((QB, SB_GW), rowblk), pl.BlockSpec((QB, SB_GW), rowblk),
                   pl.BlockSpec((None, SB_GROUP, QB, 128), lambda b, g, i: (b * nq + i, g, 0, 0))] + x_out,
        out_shape=[jax.ShapeDtypeStruct((t, SB_W), BF16), jax.ShapeDtypeStruct((t, SB_W), BF16),
                   jax.ShapeDtypeStruct((bl * nq, SB_PAIRS, QB, 128), F32)] + x_shape,
        scratch_shapes=x_scratch,
        compiler_params=_params("arbitrary", "arbitrary", "arbitrary"),
    )(proj, proj, proj, proj, *x_args)


def sb_bwd(dsb, o, rsave, proj, bl, s, name, comm=None):
    t = bl * s
    nq = s // QB
    qb0, kb0, vb0, gb0 = OFF_QKV // SB_GW, (OFF_QKV + SB_W) // SB_GW, (OFF_QKV + 2 * SB_W) // SB_GW, OFF_SBG // SB_GW
    grid = (bl, SB_PAIRS // SB_GROUP, nq)
    x_in, x_args, x_out, x_shape, x_scratch, x_start, x_wait = _hosted_exchange(
        comm, grid, relay_at=(bl - 1, SB_PAIRS // SB_GROUP - 1, 0))

    def body(*refs):
        n = len(x_in)
        d_ref, o_ref, r_ref, q_ref, k_ref, v_ref, g_ref = refs[:7]
        dq_ref, dk_ref, dv_ref, dg_ref = refs[7 + n:11 + n]
        dk_acc, dv_acc = refs[11 + 2 * n:13 + 2 * n]
        x_refs = refs[7:7 + n] + refs[11 + n:11 + 2 * n] + refs[13 + 2 * n:]
        x_start(x_refs)
        i = pl.program_id(2)

        @pl.when(i == 0)
        def _():
            dk_acc[...] = jnp.zeros(dk_acc.shape, F32)
            dv_acc[...] = jnp.zeros(dv_acc.shape, F32)

        lane, left, causal = _sb_masks(i)
        upper = _tri((KB, KB), lambda r, c: r > c).astype(BF16)
        lower = _tri((KB, KB), lambda r, c: r < c).astype(BF16)
        cols = [slice(p * 128, (p + 1) * 128) for p in range(SB_GROUP)]
        qcats, docats = [], []
        for cs in cols:
            gate = g_ref[:, cs].astype(F32)
            sg = _sigmoid(gate)
            d = d_ref[:, cs].astype(F32)
            dg_ref[:, cs] = (d * o_ref[:, cs].astype(F32) * sg * (1.0 + gate * (1.0 - sg))).astype(BF16)
            docats.append(_stack_heads((d * gate * sg).astype(BF16), left))
            qcats.append(_stack_heads(q_ref[:, cs] * SB_SCALE, left))
        qcat_ts = [qc.astype(F32).T.astype(BF16) for qc in qcats]
        docat_ts = [dc.astype(F32).T.astype(BF16) for dc in docats]
        zero = qcats[0].astype(F32) * 0.0

        def block(jb, carry, diagonal):
            rows = pl.ds(pl.multiple_of(jb * KB, KB), KB)
            mask = causal(jb) if diagonal else None
            out = []
            for p, cs in enumerate(cols):
                dq, gcar = carry[p]
                kj = k_ref[rows, cs]
                vj = v_ref[rows, cs]
                rtile = r_ref[p]
                rr = jnp.concatenate(
                    [jnp.sum(jnp.where(lane == jb, rtile, 0.0), axis=1, keepdims=True),
                     jnp.sum(jnp.where(lane == 8 + jb, rtile, 0.0), axis=1, keepdims=True)], axis=0)
                lb, lom, later = _sb_block(qcats[p], kj, mask, rr, upper)
                att = jnp.exp(lb + later)
                if diagonal:
                    att = jnp.where(mask, att, 0.0)
                de = att * _dot(docats[p], vj, _NT)
                gpre = _dot(de.astype(BF16), lower) + gcar
                sig = jnp.exp(lb)
                dz = de * (1.0 - sig) - gpre * sig
                if diagonal:
                    dz = jnp.where(mask, dz, 0.0)
                dz = dz.astype(BF16)
                dk_acc[jb, cs, :] = dk_acc[jb, cs, :] + _dot(qcat_ts[p], dz)
                dv_acc[jb, cs, :] = dv_acc[jb, cs, :] + _dot(docat_ts[p], att.astype(BF16))
                out.append((dq + _dot(dz, kj), gcar + jnp.sum(de, axis=1, keepdims=True)))
            return tuple(out)

        carry = lax.fori_loop(0, i, lambda jb, c: block(jb, c, False), tuple((zero, zero[:, :1]) for _ in cols))
        carry = block(i, carry, True)
        for p, cs in enumerate(cols):
            dq = carry[p][0]
            dq_ref[:, cs] = (jnp.where(left, dq[:QB], dq[QB:]) * SB_SCALE).astype(BF16)

        @pl.when(i == nq - 1)
        def _():
            for kb in range(s // KB):
                for cs in cols:
                    dk_ref[kb * KB:(kb + 1) * KB, cs] = dk_acc[kb, cs, :].T.astype(BF16)
                    dv_ref[kb * KB:(kb + 1) * KB, cs] = dv_acc[kb, cs, :].T.astype(BF16)

        x_wait(x_refs)

    rowblk = lambda b, g, i: (b * nq + i, g)
    seqblk = lambda b, g, i: (b, g)
    return pl.pallas_call(
        body, name=name, grid=grid,
        in_specs=[pl.BlockSpec((QB, SB_GW), rowblk), pl.BlockSpec((QB, SB_GW), rowblk),
                  pl.BlockSpec((None, SB_GROUP, QB, 128), lambda b, g, i: (b * nq + i, g, 0, 0)),
                  pl.BlockSpec((QB, SB_GW), lambda b, g, i: (b * nq + i, qb0 + g)),
                  pl.BlockSpec((s, SB_GW), lambda b, g, i: (b, kb0 + g)),
                  pl.BlockSpec((s, SB_GW), lambda b, g, i: (b, vb0 + g)),
                  pl.BlockSpec((QB, SB_GW), lambda b, g, i: (b * nq + i, gb0 + g))] + x_in,
        out_specs=[pl.BlockSpec((QB, SB_GW), rowblk), pl.BlockSpec((s, SB_GW), seqblk),
                   pl.BlockSpec((s, SB_GW), seqblk), pl.BlockSpec((QB, SB_GW), rowblk)] + x_out,
        out_shape=[jax.ShapeDtypeStruct((t, SB_W), BF16)] * 4 + x_shape,
        scratch_shapes=[pltpu.VMEM((s // KB, SB_GW, KB), F32), pltpu.VMEM((s // KB, SB_GW, KB), F32)] + x_scratch,
        compiler_params=_params("arbitrary", "arbitrary", "arbitrary"),
    )(dsb, o, rsave, proj, proj, proj, proj, *x_args)


def merge_fwd(proj, ys, yp, yb, name):
    t = ys.shape[0]
    tr = min(512, t)

    def body(m_ref, ys_ref, yp_ref, yb_ref, o_ref):
        acc = jnp.zeros((tr, D), F32)
        for k, ref in enumerate((ys_ref, yp_ref, yb_ref)):
            acc = acc + _sigmoid(m_ref[:, k * D:(k + 1) * D].astype(F32)) * ref[...].astype(F32)
        o_ref[...] = acc.astype(BF16)

    rowblk = pl.BlockSpec((tr, D), lambda i: (i, 0))
    return pl.pallas_call(
        body, name=name, grid=(t // tr,),
        in_specs=[pl.BlockSpec((tr, 3 * D), lambda i: (i, 0)), rowblk, rowblk, rowblk],
        out_specs=rowblk,
        out_shape=jax.ShapeDtypeStruct((t, D), BF16),
        compiler_params=_params("parallel"),
    )(proj, ys, yp, yb)


def merge_bwd(dm, proj, ys, yp, yb, name):
    t = ys.shape[0]
    tr = min(512, t)

    def body(dm_ref, m_ref, ys_ref, yp_ref, yb_ref, d0_ref, d1_ref, d2_ref, dl_ref):
        dmv = dm_ref[...].astype(F32)
        for k, (ref, dref) in enumerate(((ys_ref, d0_ref), (yp_ref, d1_ref), (yb_ref, d2_ref))):
            g = _sigmoid(m_ref[:, k * D:(k + 1) * D].astype(F32))
            dref[...] = (g * dmv).astype(BF16)
            dl_ref[:, k * D:(k + 1) * D] = (dmv * ref[...].astype(F32) * g * (1.0 - g)).astype(BF16)

    rowblk = pl.BlockSpec((tr, D), lambda i: (i, 0))
    wide = pl.BlockSpec((tr, 3 * D), lambda i: (i, 0))
    return pl.pallas_call(
        body, name=name, grid=(t // tr,),
        in_specs=[rowblk, wide, rowblk, rowblk, rowblk],
        out_specs=[rowblk, rowblk, rowblk, wide],
        out_shape=[jax.ShapeDtypeStruct((t, D), BF16)] * 3 + [jax.ShapeDtypeStruct((t, 3 * D), BF16)],
        compiler_params=_params("parallel"),
    )(dm, proj, ys, yp, yb)


def layer_fwd(x, lw, bl, s, tag, comm=None):
    t = bl * s
    h = rmsnorm_fwd(x, lw["norm_w"], f"norm_fwd{tag}")
    proj = matmul(h, lw["w_in"], "nt", BF16, f"in_proj{tag}", tn=2048)
    xa = conv_fwd(proj, lw["conv_w"], lw["conv_b"], bl, s, f"conv_fwd{tag}")
    dtT, acT, dtC, acC = dt_fwd(proj, lw["dt_bias"], lw["a_log"], t, f"dt_fwd{tag}")
    dskip_l = jnp.repeat(lw["d_skip"], 64).reshape(1, SSM_W)
    y, prev = ssd_fwd(xa, acT, dtC, acC, dskip_l, bl, s, f"ssd_fwd{tag}")
    s_out = gnorm_fwd(y, proj, lw["ssm_norm_w"], f"gnorm_fwd{tag}")
    p_out = pool_fwd(proj, lw["pool_w"], lw["pool_scale"], bl, s, f"pool_fwd{tag}")
    sb_out, sb_o, sb_r, *carried = sb_fwd(proj, bl, s, f"sb_fwd{tag}", comm)
    ys = matmul(s_out, lw["w_proj_ssm"], "nn", BF16, f"proj_ssm{tag}")
    yp = matmul(p_out, lw["w_proj_pool"], "nn", BF16, f"proj_pool{tag}")
    yb = matmul(sb_out, lw["w_proj_sb"], "nn", BF16, f"proj_sb{tag}")
    merged = merge_fwd(proj, ys, yp, yb, f"merge_fwd{tag}")
    x_next = matmul(merged, lw["w_out"], "nn", F32, f"out_proj{tag}", residual=x)
    saved = dict(x=x, h=h, proj=proj, xa=xa, dtT=dtT, acT=acT, dtC=dtC, acC=acC, y=y, prev=prev, s_out=s_out, p_out=p_out,
                 sb_out=sb_out, sb_o=sb_o, sb_r=sb_r, ys=ys, yp=yp, yb=yb, merged=merged)
    return x_next, saved, (carried[0] if carried else None)


def layer_bwd(dx, dx_b, lw, sv, bl, s, tag, comm, own_slabs):
    t = bl * s
    g = {}
    dmerged = matmul(dx_b, lw["w_out"], "nt", BF16, f"d_merged{tag}")
    g["w_out"] = matmul(sv["merged"], dx_b, "tn", BF16, f"dw_out{tag}")
    dys, dyp, dyb, dlogit = merge_bwd(dmerged, sv["proj"], sv["ys"], sv["yp"], sv["yb"], f"merge_bwd{tag}")
    ds_out = matmul(dys, lw["w_proj_ssm"], "nt", BF16, f"d_sout{tag}")
    g["w_proj_ssm"] = matmul(sv["s_out"], dys, "tn", BF16, f"dw_proj_ssm{tag}")
    dp_out = matmul(dyp, lw["w_proj_pool"], "nt", BF16, f"d_pout{tag}")
    g["w_proj_pool"] = matmul(sv["p_out"], dyp, "tn", BF16, f"dw_proj_pool{tag}")
    dsb_out = matmul(dyb, lw["w_proj_sb"], "nt", BF16, f"d_sbout{tag}")
    g["w_proj_sb"] = matmul(sv["sb_out"], dyb, "tn", BF16, f"dw_proj_sb{tag}")
    dy, dz, dnw = gnorm_bwd(ds_out, sv["y"], sv["proj"], lw["ssm_norm_w"], f"gnorm_bwd{tag}")
    g["ssm_norm_w"] = dnw[0]
    dskip_l = jnp.repeat(lw["d_skip"], 64).reshape(1, SSM_W)
    dxs, db, dc, ddtT, dacT, dsk = ssd_bwd(dy, sv["xa"], sv["acT"], sv["dtC"], sv["acC"], dskip_l, sv["prev"], bl, s,
                                           f"ssd_bwd{tag}")
    g["d_skip"] = jnp.sum(dsk.reshape(N_HEADS, 64), axis=1)
    ddt_raw, da, dbias = dt_bwd(ddtT, dacT, sv["dtT"], sv["proj"], lw["dt_bias"], lw["a_log"], t, f"dt_bwd{tag}")
    g["a_log"] = da * (-jnp.exp(lw["a_log"]))
    g["dt_bias"] = dbias
    dxa = jnp.concatenate([dxs, db.astype(BF16), dc.astype(BF16)], axis=1)
    dxbc, dcw, dcb = conv_bwd(dxa, sv["proj"], lw["conv_w"], lw["conv_b"], bl, s, f"conv_bwd{tag}")
    g["conv_w"] = dcw
    g["conv_b"] = dcb
    dpu, dpg, dpw, dpsc = pool_bwd(dp_out, sv["proj"], lw["pool_w"], lw["pool_scale"], bl, s, f"pool_bwd{tag}")
    g["pool_w"] = dpw
    g["pool_scale"] = dpsc[0]
    dq, dk, dv, dsbg, *carried = sb_bwd(dsb_out, sv["sb_o"], sv["sb_r"], sv["proj"], bl, s, f"sb_bwd{tag}", comm)
    dproj = concat_columns([dlogit, dsbg, dpu, dpg, dz, dq, dk, dv, dxbc, ddt_raw], PC, f"d_proj{tag}")
    g["w_in"] = matmul(dproj, sv["h"], "tn", BF16, f"dw_in{tag}", tk=2048)
    own_comm = own_slabs(g) if own_slabs is not None else None
    dh = matmul(dproj, lw["w_in"], "nn", F32, f"d_h{tag}", tk=2048, comm=own_comm)
    dh, own_got = dh if own_comm is not None else (dh, None)
    dx_in, dx_in_b, dnorm = rmsnorm_bwd(dh, sv["x"], lw["norm_w"], dx, f"norm_bwd{tag}")
    g["norm_w"] = dnorm[0]
    return dx_in, dx_in_b, g, (carried[0] if carried else None), (own_comm[0] if own_comm else None), own_got


def concat_columns(parts, width, name):
    t = parts[0].shape[0]
    tr = min(256, t)
    widths = [p.shape[1] for p in parts]
    used = sum(widths)

    def body(*refs):
        o_ref = refs[-1]
        off = 0
        for ref, w in zip(refs[:-1], widths):
            o_ref[:, off:off + w] = ref[...]
            off += w
        if width > used:
            o_ref[:, used:] = jnp.zeros((tr, width - used), BF16)

    return pl.pallas_call(
        body, name=name, grid=(t // tr,),
        in_specs=[pl.BlockSpec((tr, w), lambda i: (i, 0)) for w in widths],
        out_specs=pl.BlockSpec((tr, width), lambda i: (i, 0)),
        out_shape=jax.ShapeDtypeStruct((t, width), BF16),
        compiler_params=_params("parallel"),
    )(*parts)


_PAD_PIECES = ((10784, 3072), (9760, 1024), (4640, 1024), (5664, 1024), (0, 2048), (6688, 3072), (2048, 2560), (4608, 32))
_UNPAD_PIECES = ((OFF_Z, 2048), (OFF_XBC, 2560), (OFF_DT, 32), (OFF_PU, 1024), (OFF_PG, 1024), (OFF_QKV, 3072),
                 (OFF_SBG, 1024), (OFF_MERGE, 3072))


def pad_rows(wt):
    pieces = [wt[o:o + n] for o, n in _PAD_PIECES]
    return jnp.concatenate(pieces + [jnp.zeros((PC - IN_COLS, wt.shape[1]), wt.dtype)], axis=0)


def unpad_rows(wp):
    return jnp.concatenate([wp[o:o + n] for o, n in _UNPAD_PIECES], axis=0)


MESH = pl.DeviceIdType.MESH
ANY = pl.BlockSpec(memory_space=pl.ANY)


def _coords():
    return lax.axis_index("x"), lax.axis_index("y"), lax.axis_index("c")


def _peer(p):
    x, y, c = _coords()
    return (1 - x if p & 4 else x, 1 - y if p & 2 else y, 1 - c if p & 1 else c)


def _flat(pos):
    return 4 * pos[0] + 2 * pos[1] + pos[2]


def _chip(pos):
    return 2 * pos[0] + pos[1]


def _exchange_copies(v_ref, out_ref, send_sems, recv_sems, local_sem, mode):
    x, y, c = _coords()
    me = _flat((x, y, c))
    sibling = (x, y, 1 - c)
    chips = [(1 - x if j & 2 else x, 1 - y if j & 1 else y) for j in range(1, 4)]

    def copy(k, src, landing, to):
        return pltpu.make_async_remote_copy(src_ref=src, dst_ref=out_ref.at[landing], send_sem=send_sems.at[k],
                                            recv_sem=recv_sems.at[k], device_id=to, device_id_type=MESH)

    if mode == "direct":
        local = pltpu.make_async_copy(v_ref, out_ref.at[me], local_sem)
        first = [copy(p - 1, v_ref, me, _peer(p)) for p in range(1, N_DEV)]
        last = [copy(p - 1, v_ref, _flat(_peer(p)), _peer(p)) for p in range(1, N_DEV)]
        return local, first, [], last
    if mode == "gather":
        local = pltpu.make_async_copy(v_ref, out_ref.at[me], local_sem)
        first = [copy(0, v_ref, me, sibling)] + [copy(1 + j, v_ref, me, (*ch, c)) for j, ch in enumerate(chips)]
        relay = [(copy(1 + j, v_ref, _flat((*ch, c)), (*ch, c)),
                  copy(4 + j, out_ref.at[_flat((*ch, c))], _flat((*ch, c)), sibling)) for j, ch in enumerate(chips)]
        last = [copy(0, v_ref, _flat(sibling), sibling)] + [
            copy(4 + j, v_ref, _flat((*ch, 1 - c)), sibling) for j, ch in enumerate(chips)]
        return local, first, relay, last
    if mode == "swap":
        first = [copy(j, v_ref.at[1 - c, j], j, sibling) for j in range(4)]
        return None, first, [], first
    assert mode == "chips"
    mine = _chip((x, y))
    local = pltpu.make_async_copy(v_ref.at[mine], out_ref.at[mine], local_sem)
    first = [copy(j, v_ref.at[_chip(ch)], mine, (*ch, c)) for j, ch in enumerate(chips)]
    last = [copy(j, v_ref.at[mine], _chip(ch), (*ch, c)) for j, ch in enumerate(chips)]
    return local, first, [], last


def _exchange_start(*refs_and_mode):
    local, first, _, _ = _exchange_copies(*refs_and_mode)
    if local is not None:
        local.start()
    for cp in first:
        cp.start()


def _exchange_relay(*refs_and_mode):
    for arrival, onward in _exchange_copies(*refs_and_mode)[2]:
        arrival.wait_recv()
        onward.start()


def _exchange_finish(*refs_and_mode):
    local, first, relay, last = _exchange_copies(*refs_and_mode)
    for cp in last:
        cp.wait_recv()
    for cp in first + [onward for _, onward in relay]:
        cp.wait_send()
    if local is not None:
        local.wait()


def _exchange_shape(v, mode):
    shape = {"chips": tuple(v.shape), "swap": tuple(v.shape[1:])}.get(mode, (N_DEV,) + tuple(v.shape))
    return jax.ShapeDtypeStruct(shape, v.dtype)


def _exchange_sems():
    return [pltpu.SemaphoreType.DMA((N_DEV - 1,)), pltpu.SemaphoreType.DMA((N_DEV - 1,)), pltpu.SemaphoreType.DMA]


def exchange(v, mode, name):
    def body(*refs):
        _exchange_start(*refs, mode)
        _exchange_relay(*refs, mode)
        _exchange_finish(*refs, mode)

    return pl.pallas_call(
        body, name=name,
        in_specs=[ANY], out_specs=ANY,
        out_shape=_exchange_shape(v, mode),
        scratch_shapes=_exchange_sems(),
    )(v)


def _hosted_exchange(comm, grid, relay_at):
    if comm is None:
        return [], [], [], [], [], (lambda refs: None), (lambda refs: None)
    v, mode = comm

    def at(step):
        cond = None
        for axis, want in enumerate(step):
            term = pl.program_id(axis) == want
            cond = term if cond is None else jnp.logical_and(cond, term)
        return cond

    def start(refs):
        @pl.when(at([0] * len(grid)))
        def _():
            _exchange_start(*refs, mode)

        if mode == "gather":
            @pl.when(at(relay_at))
            def _():
                _exchange_relay(*refs, mode)

    def wait(refs):
        @pl.when(at([n - 1 for n in grid]))
        def _():
            _exchange_finish(*refs, mode)

    return [ANY], [v], [ANY], [_exchange_shape(v, mode)], _exchange_sems(), start, wait


def pair_sum(slabs, got, name):
    _, _, r, c = slabs.shape
    tr = r // 4 if r % 64 == 0 else r

    def body(s_ref, got_ref, o_ref):
        mine = jnp.where(lax.axis_index("c") == 0, s_ref[0].astype(F32), s_ref[1].astype(F32))
        o_ref[...] = (mine + got_ref[...].astype(F32)).astype(BF16)

    blk = pl.BlockSpec((None, tr, c), lambda j, i: (j, i, 0))
    return pl.pallas_call(
        body, name=name, grid=(4, r // tr),
        in_specs=[pl.BlockSpec((2, None, tr, c), lambda j, i: (0, j, i, 0)), blk], out_specs=blk,
        out_shape=jax.ShapeDtypeStruct(got.shape, BF16),
        compiler_params=_params("parallel", "parallel"),
    )(slabs, got)


def sum_slabs(v, name):
    n, r, c = v.shape
    tr = 128 if r % 128 == 0 else r

    def body(v_ref, o_ref):
        acc = v_ref[0].astype(F32)
        for k in range(1, n):
            acc = acc + v_ref[k].astype(F32)
        o_ref[...] = acc

    return pl.pallas_call(
        body, name=name, grid=(r // tr,),
        in_specs=[pl.BlockSpec((n, tr, c), lambda i: (0, i, 0))],
        out_specs=pl.BlockSpec((tr, c), lambda i: (i, 0)),
        out_shape=jax.ShapeDtypeStruct((r, c), F32),
        compiler_params=_params("parallel"),
    )(v)


def adamw(w, g, m, v, name):
    r, c = w.shape
    tr = next((cand for cand in (256, 128, 64, 32, 16, 8) if r % cand == 0), r)

    def body(w_ref, g_ref, m_ref, v_ref, d_ref, mo_ref, vo_ref):
        gv = g_ref[...]
        mn = ADAM_B1 * m_ref[...] + (1.0 - ADAM_B1) * gv
        vn = ADAM_B2 * v_ref[...] + (1.0 - ADAM_B2) * (gv * gv)
        m_hat = mn / (1.0 - ADAM_B1 ** ADAM_STEP)
        v_hat = vn / (1.0 - ADAM_B2 ** ADAM_STEP)
        d_ref[...] = -ADAM_LR * (m_hat / (jnp.sqrt(v_hat) + ADAM_EPS) + ADAM_WD * w_ref[...])
        mo_ref[...] = mn
        vo_ref[...] = vn

    blk = pl.BlockSpec((tr, c), lambda i: (i, 0))
    return pl.pallas_call(
        body, name=name, grid=(r // tr,),
        in_specs=[blk] * 4, out_specs=[blk] * 3,
        out_shape=[jax.ShapeDtypeStruct((r, c), F32)] * 3,
        compiler_params=_params("parallel"),
    )(w, g, m, v)


BIG = ("w_proj_ssm", "w_proj_pool", "w_proj_sb", "w_out", "pool_w", "w_in")
SHARD_IN = IN_COLS // N_DEV
BIG_ROWS = {"w_proj_ssm": SSM_W // N_DEV, "w_proj_pool": POOL_W // N_DEV, "w_proj_sb": SB_W // N_DEV,
            "w_out": D // N_DEV, "pool_w": POOL_G * (POOL_GD // N_DEV) * POOL_GD // D, "w_in": SHARD_IN}
PACK_C = D
PACK_R = 2432

REPLICATED = ("norm_w", "conv_b", "dt_bias", "a_log", "d_skip", "ssm_norm_w", "pool_scale")
WEIGHTS = ("norm_w", "w_in", "conv_w", "conv_b", "dt_bias", "a_log", "d_skip", "ssm_norm_w", "pool_w",
           "pool_scale", "w_proj_ssm", "w_proj_pool", "w_proj_sb", "w_out", "final_norm_w")


def _size(shape):
    n = 1
    for d in shape:
        n *= d
    return n


def _pad_flat(flat, n):
    return jnp.concatenate([flat, jnp.zeros((n - flat.shape[0],), flat.dtype)])


def _row_offsets():
    offs, off = {}, 0
    for n in BIG:
        offs[n] = off
        off += BIG_ROWS[n]
    return offs, off


def pack_shards(parts):
    rows = [parts[n].reshape(BIG_ROWS[n], PACK_C) for n in BIG]
    rows[-1] = jnp.pad(rows[-1], ((0, PACK_R - _row_offsets()[1]), (0, 0)))
    return jnp.concatenate(rows, axis=0)


def unpack_shards(packed):
    offs, _ = _row_offsets()
    out = {}
    for n in BIG:
        seg = packed[offs[n]:offs[n] + BIG_ROWS[n]]
        if n == "w_in":
            out[n] = seg.T
        elif n == "pool_w":
            out[n] = seg.reshape(POOL_G, POOL_GD // N_DEV, POOL_GD)
        else:
            out[n] = seg
    return out


def unpack_gathered(g):
    offs, _ = _row_offsets()
    out = {}
    for n in BIG:
        seg = g[:, offs[n]:offs[n] + BIG_ROWS[n], :]
        if n == "w_in":
            out[n] = pad_rows(seg.reshape(IN_COLS, D))
        elif n == "pool_w":
            out[n] = seg.reshape(N_DEV, POOL_G, POOL_GD // N_DEV, POOL_GD).transpose(1, 0, 2, 3).reshape(
                POOL_G, POOL_GD, POOL_GD)
        else:
            out[n] = seg.reshape(N_DEV * BIG_ROWS[n], D)
    return out


def pack_slabs(g):
    segs = []
    for n in BIG:
        if n == "w_in":
            w = unpad_rows(g[n])
        elif n == "pool_w":
            w = g[n].reshape(POOL_G, N_DEV, POOL_GD // N_DEV, POOL_GD).transpose(1, 0, 2, 3)
        else:
            w = g[n]
        segs.append(w.reshape(N_DEV // 2, 2, BIG_ROWS[n], PACK_C).astype(BF16).transpose(1, 0, 2, 3))
    segs[-1] = jnp.pad(segs[-1], ((0, 0), (0, 0), (0, PACK_R - _row_offsets()[1]), (0, 0)))
    return jnp.concatenate(segs, axis=2)


SMALL_ROWS = 544


def pack_small(vals):
    flat = jnp.concatenate([v.reshape(-1) for v in vals])
    return _pad_flat(flat, SMALL_ROWS * 128).reshape(SMALL_ROWS, 128)


def unpack_small(packed, shapes):
    flat = packed.reshape(-1)
    out, off = [], 0
    for shp in shapes:
        out.append(flat[off:off + _size(shp)].reshape(shp))
        off += _size(shp)
    return out


def kernel(x, norm_w, w_in, conv_w, conv_b, dt_bias, a_log, d_skip, ssm_norm_w, pool_w, pool_scale, w_proj_ssm, w_proj_pool, w_proj_sb, w_out, final_norm_w, loss_target, m_norm_w, m_w_in, m_conv_w, m_conv_b, m_dt_bias, m_a_log, m_d_skip, m_ssm_norm_w, m_pool_w, m_pool_scale, m_w_proj_ssm, m_w_proj_pool, m_w_proj_sb, m_w_out, m_final_norm_w, v_norm_w, v_w_in, v_conv_w, v_conv_b, v_dt_bias, v_a_log, v_d_skip, v_ssm_norm_w, v_pool_w, v_pool_scale, v_w_proj_ssm, v_w_proj_pool, v_w_proj_sb, v_w_out, v_final_norm_w):
    wts = dict(norm_w=norm_w, w_in=w_in, conv_w=conv_w, conv_b=conv_b, dt_bias=dt_bias, a_log=a_log, d_skip=d_skip,
               ssm_norm_w=ssm_norm_w, pool_w=pool_w, pool_scale=pool_scale, w_proj_ssm=w_proj_ssm,
               w_proj_pool=w_proj_pool, w_proj_sb=w_proj_sb, w_out=w_out, final_norm_w=final_norm_w)
    mom = dict(norm_w=m_norm_w, w_in=m_w_in, conv_w=m_conv_w, conv_b=m_conv_b, dt_bias=m_dt_bias, a_log=m_a_log,
               d_skip=m_d_skip, ssm_norm_w=m_ssm_norm_w, pool_w=m_pool_w, pool_scale=m_pool_scale,
               w_proj_ssm=m_w_proj_ssm, w_proj_pool=m_w_proj_pool, w_proj_sb=m_w_proj_sb, w_out=m_w_out,
               final_norm_w=m_final_norm_w)
    var = dict(norm_w=v_norm_w, w_in=v_w_in, conv_w=v_conv_w, conv_b=v_conv_b, dt_bias=v_dt_bias, a_log=v_a_log,
               d_skip=v_d_skip, ssm_norm_w=v_ssm_norm_w, pool_w=v_pool_w, pool_scale=v_pool_scale,
               w_proj_ssm=v_w_proj_ssm, w_proj_pool=v_w_proj_pool, w_proj_sb=v_w_proj_sb, w_out=v_w_out,
               final_norm_w=v_final_norm_w)
    bl, s, _ = x.shape
    t = bl * s
    me = _flat(_coords())

    cw = exchange(conv_w.reshape(40, 128), "direct", "gather_conv_w")
    conv_w_full = cw.reshape(N_DEV, DEPTH, CONV_K, CONV_CH // N_DEV).transpose(1, 2, 0, 3).reshape(
        DEPTH, CONV_K, CONV_CH)

    xc = x.reshape(t, D)
    layer_w, saved = [], []
    packed = [pack_shards({n: (wts[n][l].T if n == "w_in" else wts[n][l]).astype(BF16) for n in BIG})
              for l in range(DEPTH)]
    gathered = exchange(packed[0], "gather", "gather_w0")
    for l in range(DEPTH):
        lw = unpack_gathered(gathered)
        for n in REPLICATED:
            lw[n] = wts[n][l]
        lw["conv_w"] = conv_w_full[l]
        xc, sv, gathered = layer_fwd(xc, lw, bl, s, f"_l{l}", (packed[l + 1], "gather") if l + 1 < DEPTH else None)
        layer_w.append(lw)
        saved.append(sv)

    loss_part, dx, dx_b, dfinal = final_loss(xc, final_norm_w, loss_target.reshape(t, D), "final_loss")
    loss = lax.psum(loss_part[0, 0], ("x", "y", "c"))

    grads = [None] * DEPTH
    big_sum = [None] * DEPTH
    def last_layer_slabs(g):
        slabs = pack_slabs(g)
        return pair_sum(slabs, exchange(slabs, "swap", "pair_swap0"), "pair_sum0"), "chips"

    chip_sums = None
    for l in reversed(range(DEPTH)):
        dx, dx_b, g, got, own_sent, own_got = layer_bwd(
            dx, dx_b, layer_w[l], saved[l], bl, s, f"_l{l}", (chip_sums, "chips") if chip_sums is not None else None,
            last_layer_slabs if l == 0 else (lambda g: (pack_slabs(g), "swap")))
        if got is not None:
            big_sum[l + 1] = unpack_shards(sum_slabs(got, f"sum_g{l + 1}"))
        grads[l] = g
        if l > 0:
            chip_sums = pair_sum(own_sent, own_got, f"pair_sum{l}")
    big_sum[0] = unpack_shards(sum_slabs(own_got, "sum_g0"))
    grad_x = dx.reshape(bl, s, D)

    small_names = REPLICATED + ("conv_w",)
    small_vals = [jnp.stack([grads[l][n] for l in range(DEPTH)]) for n in small_names] + [dfinal[0]]
    small_shapes = [v.shape for v in small_vals]
    small_all = exchange(pack_small(small_vals), "direct", "gather_small")
    small_sum = unpack_small(sum_slabs(small_all, "sum_small"), small_shapes)
    gsum = dict(zip(small_names + ("final_norm_w",), small_sum))
    conv_g_full = gsum["conv_w"]
    gsum["conv_w"] = lax.dynamic_slice_in_dim(conv_g_full, me * (CONV_CH // N_DEV), CONV_CH // N_DEV, axis=2)
    for n in BIG:
        gsum[n] = jnp.stack([big_sum[l][n] for l in range(DEPTH)])

    delta, new_m, new_v = {}, {}, {}
    for n in BIG + ("conv_w",):
        shp = wts[n].shape
        two_d = (-1, shp[-1])
        d2, m2, v2 = adamw(wts[n].reshape(two_d), gsum[n].reshape(two_d), mom[n].reshape(two_d),
                           var[n].reshape(two_d), f"adamw_{n}")
        delta[n], new_m[n], new_v[n] = d2.reshape(shp), m2.reshape(shp), v2.reshape(shp)
    rep = REPLICATED + ("final_norm_w",)
    rep_shapes = [wts[n].shape for n in rep]
    d2, m2, v2 = adamw(pack_small([wts[n] for n in rep]), pack_small([gsum[n] for n in rep]),
                       pack_small([mom[n] for n in rep]), pack_small([var[n] for n in rep]), "adamw_small")
    for n, dv, mv, vv in zip(rep, unpack_small(d2, rep_shapes), unpack_small(m2, rep_shapes),
                             unpack_small(v2, rep_shapes)):
        delta[n], new_m[n], new_v[n] = dv, mv, vv

    return (loss, grad_x, *[gsum[n] for n in WEIGHTS], *[delta[n] for n in WEIGHTS],
            *[new_m[n] for n in WEIGHTS], *[new_v[n] for n in WEIGHTS])
```

```python
import functools

import jax
import jax.numpy as jnp
from jax import lax
from jax.experimental import pallas as pl
from jax.experimental.pallas import tpu as pltpu

F32 = jnp.float32
BF16 = jnp.bfloat16

N_DEV = 8
DEPTH = 4
D = 1024
SSM_W = 2048
N_HEADS = 32
N_PAIRS = 16
N_GROUPS = 2
N_STATE = 128
CHUNK = 128
CONV_CH = 2560
CONV_K = 4
POOL_W = 1024
POOL_G = 4
POOL_GD = 256
SB_W = 1024
SB_PAIRS = 8
QB = 256
EPS = 1e-6
IN_COLS = 13856

PC = 14336
OFF_MERGE = 0
OFF_SBG = 3072
OFF_PU = 4096
OFF_PG = 5120
OFF_Z = 6144
OFF_QKV = 8192
OFF_XBC = 11264
OFF_DT = 13824

ADAM_LR = 0.001
ADAM_B1 = 0.9
ADAM_B2 = 0.999
ADAM_EPS = 1e-08
ADAM_WD = 0.01
ADAM_STEP = 10

VMEM_LIMIT = 56 * 1024 * 1024

_NN = (((1,), (0,)), ((), ()))
_NT = (((1,), (1,)), ((), ()))
_TN = (((0,), (0,)), ((), ()))


def _dot(a, b, dn=_NN):
    return lax.dot_general(a, b, dn, preferred_element_type=F32)


def _sigmoid(x):
    return 1.0 / (1.0 + jnp.exp(-x))


def _softplus(x):
    return jnp.maximum(x, 0.0) + jnp.log(1.0 + jnp.exp(-jnp.abs(x)))


def _split2(x):
    hi = x.astype(BF16)
    lo = (x - hi.astype(F32)).astype(BF16)
    return hi, lo


def _split3(x):
    hi = x.astype(BF16)
    r = x - hi.astype(F32)
    mid = r.astype(BF16)
    lo = (r - mid.astype(F32)).astype(BF16)
    return hi, mid, lo


def _params(*sem):
    return pltpu.CompilerParams(dimension_semantics=sem, vmem_limit_bytes=VMEM_LIMIT)


def matmul(a, b, mode, out_dtype, name, residual=None, tm=1024, tn=1024, tk=1024, comm=None):
    if mode == "nn":
        (m, k), (k2, n) = a.shape, b.shape
    elif mode == "nt":
        (m, k), (n, k2) = a.shape, b.shape
    else:
        (k, m), (k2, n) = a.shape, b.shape
    assert k == k2
    tm, tn, tk = min(tm, m), min(tn, n), min(tk, k)
    assert m % tm == 0 and n % tn == 0 and k % tk == 0
    nk = k // tk
    dn = {"nn": _NN, "nt": _NT, "tn": _TN}[mode]
    a_spec = pl.BlockSpec((tk, tm), lambda i, j, kk: (kk, i)) if mode == "tn" else pl.BlockSpec((tm, tk), lambda i, j, kk: (i, kk))
    b_spec = pl.BlockSpec((tn, tk), lambda i, j, kk: (j, kk)) if mode == "nt" else pl.BlockSpec((tk, tn), lambda i, j, kk: (kk, j))
    in_specs = [a_spec, b_spec]
    args = [a, b]
    if residual is not None:
        in_specs.append(pl.BlockSpec((tm, tn), lambda i, j, kk: (i, j)))
        args.append(residual)
    grid = (m // tm, n // tn, nk)
    n_in = len(args)
    x_in, x_args, x_out, x_shape, x_scratch, x_start, x_wait = _hosted_exchange(comm, grid, relay_at=None)

    def body(*refs):
        n_xi, n_xo = len(x_in), len(x_out)
        a_ref, b_ref = refs[:2]
        r_ref = refs[2] if residual is not None else None
        o_ref = refs[n_in + n_xi]
        acc_ref = refs[n_in + n_xi + n_xo + 1]
        x_refs = refs[n_in:n_in + n_xi] + refs[n_in + n_xi + 1:n_in + n_xi + n_xo + 1] + refs[n_in + n_xi + n_xo + 2:]
        x_start(x_refs)
        kk = pl.program_id(2)
        p = _dot(a_ref[...], b_ref[...], dn)

        def finish(val):
            if r_ref is not None:
                val = val + r_ref[...]
            o_ref[...] = val.astype(out_dtype)

        if nk == 1:
            finish(p)
        else:
            @pl.when(kk == 0)
            def _():
                acc_ref[...] = p

            @pl.when(kk > 0)
            def _():
                acc_ref[...] += p

            @pl.when(kk == nk - 1)
            def _():
                finish(acc_ref[...])

        x_wait(x_refs)

    out = pl.pallas_call(
        body, name=name,
        grid=grid,
        in_specs=in_specs + x_in,
        out_specs=[pl.BlockSpec((tm, tn), lambda i, j, kk: (i, j))] + x_out,
        out_shape=[jax.ShapeDtypeStruct((m, n), out_dtype)] + x_shape,
        scratch_shapes=[pltpu.VMEM((tm, tn) if nk > 1 else (8, 128), F32)] + x_scratch,
        compiler_params=_params(*(("arbitrary",) * 3 if comm is not None else ("parallel", "parallel", "arbitrary"))),
    )(*args, *x_args)
    return tuple(out) if comm is not None else out[0]


def rmsnorm_fwd(x, w, name):
    t, d = x.shape
    tr = min(512, t)

    def body(x_ref, w_ref, h_ref):
        xv = x_ref[...]
        r = lax.rsqrt(jnp.mean(xv * xv, axis=-1, keepdims=True) + EPS)
        h_ref[...] = (xv * r * w_ref[...]).astype(BF16)

    return pl.pallas_call(
        body, name=name, grid=(t // tr,),
        in_specs=[pl.BlockSpec((tr, d), lambda i: (i, 0)), pl.BlockSpec((1, d), lambda i: (0, 0))],
        out_specs=pl.BlockSpec((tr, d), lambda i: (i, 0)),
        out_shape=jax.ShapeDtypeStruct((t, d), BF16),
        compiler_params=_params("parallel"),
    )(x, w.reshape(1, d))


def rmsnorm_bwd(dh, x, w, dres, name):
    t, d = x.shape
    tr = min(512, t)

    def body(dh_ref, x_ref, w_ref, dres_ref, dx_ref, dxb_ref, dw_ref):
        xv = x_ref[...]
        r = lax.rsqrt(jnp.mean(xv * xv, axis=-1, keepdims=True) + EPS)
        xh = xv * r
        g = dh_ref[...].astype(F32)
        dxh = g * w_ref[...]
        dxv = dres_ref[...] + r * (dxh - xh * jnp.mean(dxh * xh, axis=-1, keepdims=True))
        dx_ref[...] = dxv
        dxb_ref[...] = dxv.astype(BF16)
        part = jnp.sum(g * xh, axis=0, keepdims=True)

        @pl.when(pl.program_id(0) == 0)
        def _():
            dw_ref[...] = part

        @pl.when(pl.program_id(0) > 0)
        def _():
            dw_ref[...] += part

    return pl.pallas_call(
        body, name=name, grid=(t // tr,),
        in_specs=[pl.BlockSpec((tr, d), lambda i: (i, 0)), pl.BlockSpec((tr, d), lambda i: (i, 0)),
                  pl.BlockSpec((1, d), lambda i: (0, 0)), pl.BlockSpec((tr, d), lambda i: (i, 0))],
        out_specs=[pl.BlockSpec((tr, d), lambda i: (i, 0)), pl.BlockSpec((tr, d), lambda i: (i, 0)),
                   pl.BlockSpec((1, d), lambda i: (0, 0))],
        out_shape=[jax.ShapeDtypeStruct((t, d), F32), jax.ShapeDtypeStruct((t, d), BF16),
                   jax.ShapeDtypeStruct((1, d), F32)],
        compiler_params=_params("arbitrary"),
    )(dh, x, w.reshape(1, d), dres)


def final_loss(x, w, target, name):
    t, d = x.shape
    tr = min(512, t)

    def body(x_ref, w_ref, tg_ref, loss_ref, dx_ref, dxb_ref, dw_ref):
        xv = x_ref[...]
        r = lax.rsqrt(jnp.mean(xv * xv, axis=-1, keepdims=True) + EPS)
        xh = xv * r
        err = xh * w_ref[...] - tg_ref[...]
        lpart = 0.5 * jnp.sum(jnp.mean(err * err, axis=-1, keepdims=True), axis=0, keepdims=True)
        dy = err * (1.0 / d)
        dxh = dy * w_ref[...]
        dxv = r * (dxh - xh * jnp.mean(dxh * xh, axis=-1, keepdims=True))
        dx_ref[...] = dxv
        dxb_ref[...] = dxv.astype(BF16)
        part = jnp.sum(dy * xh, axis=0, keepdims=True)

        @pl.when(pl.program_id(0) == 0)
        def _():
            dw_ref[...] = part
            loss_ref[...] = jnp.broadcast_to(lpart, (1, 128))

        @pl.when(pl.program_id(0) > 0)
        def _():
            dw_ref[...] += part
            loss_ref[...] += jnp.broadcast_to(lpart, (1, 128))

    return pl.pallas_call(
        body, name=name, grid=(t // tr,),
        in_specs=[pl.BlockSpec((tr, d), lambda i: (i, 0)), pl.BlockSpec((1, d), lambda i: (0, 0)),
                  pl.BlockSpec((tr, d), lambda i: (i, 0))],
        out_specs=[pl.BlockSpec((1, 128), lambda i: (0, 0)), pl.BlockSpec((tr, d), lambda i: (i, 0)),
                   pl.BlockSpec((tr, d), lambda i: (i, 0)), pl.BlockSpec((1, d), lambda i: (0, 0))],
        out_shape=[jax.ShapeDtypeStruct((1, 128), F32), jax.ShapeDtypeStruct((t, d), F32),
                   jax.ShapeDtypeStruct((t, d), BF16), jax.ShapeDtypeStruct((1, d), F32)],
        compiler_params=_params("arbitrary"),
    )(x, w.reshape(1, d), target)


CONV_BW = 256


def _shift_down(u, s, row):
    return jnp.where(row >= s, pltpu.roll(u, s, axis=0), 0.0)


def _shift_up(u, s, row, n):
    return jnp.where(row < n - s, pltpu.roll(u, n - s, axis=0), 0.0)


def _conv_pre(u, w, b, row):
    acc = b + w[CONV_K - 1:CONV_K, :] * u
    for k in range(CONV_K - 1):
        acc = acc + w[k:k + 1, :] * _shift_down(u, CONV_K - 1 - k, row)
    return acc


def conv_fwd(proj, conv_w, conv_b, bl, s, name):
    t = bl * s
    nb = CONV_CH // CONV_BW
    off = OFF_XBC // CONV_BW

    def body(u_ref, w_ref, b_ref, o_ref):
        u = u_ref[...].astype(F32)
        row = lax.broadcasted_iota(jnp.int32, u.shape, 0)
        xc = _conv_pre(u, w_ref[...], b_ref[...], row)
        o_ref[...] = (xc * _sigmoid(xc)).astype(BF16)

    return pl.pallas_call(
        body, name=name, grid=(bl, nb),
        in_specs=[pl.BlockSpec((s, CONV_BW), lambda b, j: (b, off + j)),
                  pl.BlockSpec((CONV_K, CONV_BW), lambda b, j: (0, j)),
                  pl.BlockSpec((1, CONV_BW), lambda b, j: (0, j))],
        out_specs=pl.BlockSpec((s, CONV_BW), lambda b, j: (b, j)),
        out_shape=jax.ShapeDtypeStruct((t, CONV_CH), BF16),
        compiler_params=_params("parallel", "parallel"),
    )(proj, conv_w, conv_b.reshape(1, CONV_CH))


def conv_bwd(dxa, proj, conv_w, conv_b, bl, s, name):
    t = bl * s
    nb = CONV_CH // CONV_BW
    off = OFF_XBC // CONV_BW

    def body(d_ref, u_ref, w_ref, b_ref, du_ref, dw_ref, db_ref):
        u = u_ref[...].astype(F32)
        w = w_ref[...]
        row = lax.broadcasted_iota(jnp.int32, u.shape, 0)
        xc = _conv_pre(u, w, b_ref[...], row)
        sg = _sigmoid(xc)
        dxc = d_ref[...].astype(F32) * sg * (1.0 + xc * (1.0 - sg))
        du = w[CONV_K - 1:CONV_K, :] * dxc
        dws = [None] * CONV_K
        dws[CONV_K - 1] = jnp.sum(dxc * u, axis=0, keepdims=True)
        for k in range(CONV_K - 1):
            up = _shift_up(dxc, CONV_K - 1 - k, row, s)
            du = du + w[k:k + 1, :] * up
            dws[k] = jnp.sum(up * u, axis=0, keepdims=True)
        du_ref[...] = du.astype(BF16)
        krow = lax.broadcasted_iota(jnp.int32, (8, CONV_BW), 0)
        dwv = sum(jnp.where(krow == k, dws[k], 0.0) for k in range(CONV_K))
        dbv = jnp.sum(dxc, axis=0, keepdims=True)

        @pl.when(pl.program_id(1) == 0)
        def _():
            dw_ref[...] = dwv
            db_ref[...] = dbv

        @pl.when(pl.program_id(1) > 0)
        def _():
            dw_ref[...] += dwv
            db_ref[...] += dbv

    du, dw, db = pl.pallas_call(
        body, name=name, grid=(nb, bl),
        in_specs=[pl.BlockSpec((s, CONV_BW), lambda j, b: (b, j)),
                  pl.BlockSpec((s, CONV_BW), lambda j, b: (b, off + j)),
                  pl.BlockSpec((CONV_K, CONV_BW), lambda j, b: (0, j)),
                  pl.BlockSpec((1, CONV_BW), lambda j, b: (0, j))],
        out_specs=[pl.BlockSpec((s, CONV_BW), lambda j, b: (b, j)),
                   pl.BlockSpec((8, CONV_BW), lambda j, b: (0, j)),
                   pl.BlockSpec((1, CONV_BW), lambda j, b: (0, j))],
        out_shape=[jax.ShapeDtypeStruct((t, CONV_CH), BF16), jax.ShapeDtypeStruct((8, CONV_CH), F32),
                   jax.ShapeDtypeStruct((1, CONV_CH), F32)],
        compiler_params=_params("parallel", "arbitrary"),
    )(dxa, proj, conv_w, conv_b.reshape(1, CONV_CH))
    return du, dw[:CONV_K], db[0]


def _tri(shape, cmp):
    r = lax.broadcasted_iota(jnp.int32, shape, 0)
    c = lax.broadcasted_iota(jnp.int32, shape, 1)
    return cmp(r, c)


def dt_fwd(proj, dt_bias, a_log, t, name):
    nchunks = t // CHUNK
    bias = jnp.zeros((1, 128), F32).at[0, :N_HEADS].set(dt_bias)
    alog = jnp.zeros((1, 128), F32).at[0, :N_HEADS].set(a_log)

    def body(raw_ref, b_ref, al_ref, dt_ref, ac_ref, dtl_ref, acl_ref):
        raw = raw_ref[...].astype(F32)
        dt = _softplus(raw + b_ref[...])
        adt = dt * (-jnp.exp(al_ref[...]))
        low = _tri((CHUNK, CHUNK), lambda r, c: r >= c).astype(BF16)
        acum = sum(_dot(low, part) for part in _split3(adt))
        dt_ref[...] = dt.T[:N_HEADS]
        ac_ref[...] = acum.T[:N_HEADS]
        spread = _tri((128, SSM_W), lambda h, lane: lane // 64 == h).astype(BF16)
        dtl_ref[...] = sum(_dot(part, spread) for part in _split2(dt))
        acl_ref[...] = sum(_dot(part, spread) for part in _split3(acum))

    return pl.pallas_call(
        body, name=name, grid=(nchunks,),
        in_specs=[pl.BlockSpec((CHUNK, 128), lambda i: (i, OFF_DT // 128)),
                  pl.BlockSpec((1, 128), lambda i: (0, 0)), pl.BlockSpec((1, 128), lambda i: (0, 0))],
        out_specs=[pl.BlockSpec((None, N_HEADS, CHUNK), lambda i: (i, 0, 0))] * 2
        + [pl.BlockSpec((CHUNK, SSM_W), lambda i: (i, 0))] * 2,
        out_shape=[jax.ShapeDtypeStruct((nchunks, N_HEADS, CHUNK), F32)] * 2
        + [jax.ShapeDtypeStruct((t, SSM_W), F32)] * 2,
        compiler_params=_params("parallel"),
    )(proj, bias, alog)


def dt_bwd(ddtT, dacT, dtT, proj, dt_bias, a_log, t, name):
    nchunks = t // CHUNK
    bias = dt_bias.reshape(N_HEADS, 1)
    alog = a_log.reshape(N_HEADS, 1)

    def body(ddt_ref, dac_ref, dt_ref, raw_ref, b_ref, al_ref, draw_ref, da_ref, db_ref):
        a = -jnp.exp(al_ref[...])
        upp = _tri((CHUNK, CHUNK), lambda r, c: r >= c).astype(BF16)
        dadt = sum(_dot(part, upp) for part in _split3(dac_ref[...]))
        ddt = ddt_ref[...] + dadt * a
        rawT = raw_ref[...].astype(F32).T[:N_HEADS]
        draw = ddt * _sigmoid(rawT + b_ref[...])
        padded = jnp.concatenate([draw, jnp.zeros((128 - N_HEADS, CHUNK), F32)], axis=0)
        draw_ref[...] = padded.T.astype(BF16)
        dav = dadt * dt_ref[...]

        @pl.when(pl.program_id(0) == 0)
        def _():
            da_ref[...] = dav
            db_ref[...] = draw

        @pl.when(pl.program_id(0) > 0)
        def _():
            da_ref[...] += dav
            db_ref[...] += draw

    draw, da, db = pl.pallas_call(
        body, name=name, grid=(nchunks,),
        in_specs=[pl.BlockSpec((None, N_HEADS, CHUNK), lambda i: (i, 0, 0))] * 3
        + [pl.BlockSpec((CHUNK, 128), lambda i: (i, OFF_DT // 128)),
           pl.BlockSpec((N_HEADS, 1), lambda i: (0, 0)), pl.BlockSpec((N_HEADS, 1), lambda i: (0, 0))],
        out_specs=[pl.BlockSpec((CHUNK, 128), lambda i: (i, 0)),
                   pl.BlockSpec((N_HEADS, CHUNK), lambda i: (0, 0)), pl.BlockSpec((N_HEADS, CHUNK), lambda i: (0, 0))],
        out_shape=[jax.ShapeDtypeStruct((t, 128), BF16), jax.ShapeDtypeStruct((N_HEADS, CHUNK), F32),
                   jax.ShapeDtypeStruct((N_HEADS, CHUNK), F32)],
        compiler_params=_params("arbitrary"),
    )(ddtT, dacT, dtT, proj, bias, alog)
    return draw, jnp.sum(da, axis=1), jnp.sum(db, axis=1)


PAIRS_G = N_PAIRS // N_GROUPS
GROUP_W = PAIRS_G * 128


def _ssd_pair(x, dtl, acl, acr, tri):
    left = lax.broadcasted_iota(jnp.int32, (CHUNK, 128), 1) < 64
    swapped = pltpu.roll(acl, 64, axis=1)
    ac_cols = [jnp.where(left, acl, swapped), jnp.where(left, swapped, acl)]
    dks = [jnp.exp(jnp.where(tri, ac_cols[e] - acr[e:e + 1], -1e30)) for e in range(2)]
    aclast = acl[CHUNK - 1:CHUNK, :]
    return left, dtl, acl, x * dtl, dks, aclast


def _ssd_specs(nc, rev):
    row = (lambda b, c, g: b * nc + (nc - 1 - c)) if rev else (lambda b, c, g: b * nc + c)
    return dict(
        wide=pl.BlockSpec((CHUNK, GROUP_W), lambda b, c, g: (row(b, c, g), g)),
        bmat=pl.BlockSpec((CHUNK, 128), lambda b, c, g: (row(b, c, g), SSM_W // 128 + g)),
        cmat=pl.BlockSpec((CHUNK, 128), lambda b, c, g: (row(b, c, g), SSM_W // 128 + N_GROUPS + g)),
        rows2=pl.BlockSpec((None, PAIRS_G, 2, CHUNK), lambda b, c, g: (row(b, c, g), g, 0, 0)),
        cols=pl.BlockSpec((CHUNK, GROUP_W), lambda b, c, g: (row(b, c, g), g)),
        rows8=pl.BlockSpec((None, PAIRS_G, 8, CHUNK), lambda b, c, g: (row(b, c, g), g, 0, 0)),
        dskip=pl.BlockSpec((1, GROUP_W), lambda b, c, g: (0, g)),
        state=pl.BlockSpec((None, PAIRS_G, N_STATE, 128), lambda b, c, g: (row(b, c, g), g, 0, 0)),
        narrow=pl.BlockSpec((CHUNK, 128), lambda b, c, g: (row(b, c, g), g)))


def ssd_fwd(xa, acT, dtC, acC, dskip_l, bl, s, name):
    t = bl * s
    nc = s // CHUNK
    ac4 = acT.reshape(bl * nc, N_PAIRS, 2, CHUNK)

    def body(x_ref, b_ref, c_ref, ac_ref, dtc_ref, acc_ref, dsk_ref, y_ref, prev_ref, st_ref):
        c = pl.program_id(1)
        g = pl.program_id(2)
        bm = b_ref[...]
        cm = c_ref[...]
        cb = _dot(cm, bm, _NT)
        tri = _tri((CHUNK, CHUNK), lambda r, c: r >= c)

        @pl.when(c == 0)
        def _():
            for p in range(PAIRS_G):
                st_ref[g * PAIRS_G + p] = jnp.zeros((N_STATE, 128), F32)

        for p in range(PAIRS_G):
            hp = g * PAIRS_G + p
            cs = slice(p * 128, (p + 1) * 128)
            x = x_ref[:, cs].astype(F32)
            left, dtl, acl, xdt, dks, aclast = _ssd_pair(x, dtc_ref[:, cs], acc_ref[:, cs], ac_ref[p], tri)
            xdt_b = xdt.astype(BF16)
            ys = [_dot((cb * dks[e]).astype(BF16), xdt_b) for e in range(2)]
            st = st_ref[hp]
            y_off = _dot(cm, st.astype(BF16)) * jnp.exp(acl)
            y_ref[:, cs] = (jnp.where(left, ys[0], ys[1]) + y_off + x * dsk_ref[:, cs]).astype(BF16)
            xw = (xdt * jnp.exp(aclast - acl)).astype(BF16)
            prev_ref[p] = st
            st_ref[hp] = st * jnp.exp(aclast) + _dot(bm, xw, _TN)

    sp = _ssd_specs(nc, False)
    return pl.pallas_call(
        body, name=name, grid=(bl, nc, N_GROUPS),
        in_specs=[sp["wide"], sp["bmat"], sp["cmat"], sp["rows2"], sp["cols"], sp["cols"], sp["dskip"]],
        out_specs=[sp["wide"], sp["state"]],
        out_shape=[jax.ShapeDtypeStruct((t, SSM_W), BF16),
                   jax.ShapeDtypeStruct((bl * nc, N_PAIRS, N_STATE, 128), F32)],
        scratch_shapes=[pltpu.VMEM((N_PAIRS, N_STATE, 128), F32)],
        compiler_params=_params("parallel", "arbitrary", "arbitrary"),
    )(xa, xa, xa, ac4, dtC, acC, dskip_l)


def ssd_bwd(dy, xa, acT, dtC, acC, dskip_l, prev, bl, s, name):
    t = bl * s
    nc = s // CHUNK
    ac4 = acT.reshape(bl * nc, N_PAIRS, 2, CHUNK)

    def body(dy_ref, x_ref, b_ref, c_ref, ac_ref, dtc_ref, acc_ref, dsk_ref, prev_ref,
             dx_ref, db_ref, dc_ref, dd_ref, dsk_out_ref, dp_ref):
        b = pl.program_id(0)
        cr = pl.program_id(1)
        g = pl.program_id(2)

        @pl.when(cr == 0)
        def _():
            for p in range(PAIRS_G):
                dp_ref[g * PAIRS_G + p] = jnp.zeros((N_STATE, 128), F32)

        @pl.when((b == 0) & (cr == 0) & (g == 0))
        def _():
            dsk_out_ref[...] = jnp.zeros(dsk_out_ref.shape, F32)

        bm = b_ref[...]
        cm = c_ref[...]
        cb = _dot(cm, bm, _NT)
        tri = _tri((CHUNK, CHUNK), lambda r, c: r >= c)
        lane = lax.broadcasted_iota(jnp.int32, (CHUNK, 128), 1)
        lrow = lax.broadcasted_iota(jnp.int32, (1, CHUNK), 1)
        krow = lax.broadcasted_iota(jnp.int32, (8, CHUNK), 0)
        dcb = jnp.zeros((CHUNK, CHUNK), F32)
        dc_acc = jnp.zeros((CHUNK, N_STATE), F32)
        db_acc = jnp.zeros((CHUNK, N_STATE), F32)
        for p in range(PAIRS_G):
            hp = g * PAIRS_G + p
            cs = slice(p * 128, (p + 1) * 128)
            x = x_ref[:, cs].astype(F32)
            left, dtl, acl, xdt, dks, aclast = _ssd_pair(x, dtc_ref[:, cs], acc_ref[:, cs], ac_ref[p], tri)
            dyv = dy_ref[:, cs].astype(F32)
            dy_b = dyv.astype(BF16)
            xdt_b = xdt.astype(BF16)
            st = prev_ref[p]
            st_b = st.astype(BF16)
            ea = jnp.exp(acl)
            ds = jnp.exp(aclast - acl)
            cdl = jnp.exp(aclast)
            xw = xdt * ds
            masks = [left, jnp.logical_not(left)]

            dsk_out_ref[hp] = dsk_out_ref[hp] + jnp.sum(dyv * x, axis=0, keepdims=True)

            yo = _dot(cm, st_b)
            dyo_b = (dyv * ea).astype(BF16)
            yoff_term = dyv * yo * ea
            dc_acc = dc_acc + _dot(dyo_b, st_b, _NT)
            dst = _dot(cm, dyo_b, _TN)
            dsv = dp_ref[hp]
            dsv_b = dsv.astype(BF16)
            dxw = _dot(bm, dsv_b)
            db_acc = db_acc + _dot(xw.astype(BF16), dsv_b, _NT)
            dxdt = dxw * ds
            qv = dxw * xw
            end_term = dsv * st * cdl
            dp_ref[hp] = dsv * cdl + dst

            cols = jnp.zeros((CHUNK, 128), F32)
            rows = []
            for e in range(2):
                m = cb * dks[e]
                dy_e = jnp.where(masks[e], dyv, 0.0).astype(BF16)
                dm = _dot(dy_e, xdt_b, _NT)
                w = dm * m
                dcb = dcb + dm * dks[e]
                dxdt = dxdt + jnp.where(masks[e], _dot(m.astype(BF16), dy_b, _TN), 0.0)
                dac_col = jnp.sum(w + jnp.where(masks[e], yoff_term - qv, 0.0), axis=1, keepdims=True)
                cols = jnp.where(lane == 2 + e, dac_col, cols)
                tail = jnp.sum(jnp.where(masks[e], qv + end_term, 0.0))
                rows.append(jnp.where(lrow == CHUNK - 1, tail, 0.0) - jnp.sum(w, axis=0, keepdims=True))
            dx_ref[:, cs] = (dxdt * dtl + dyv * dsk_ref[:, cs]).astype(BF16)
            ddt_l = dxdt * x
            for e in range(2):
                cols = jnp.where(lane == e, jnp.sum(jnp.where(masks[e], ddt_l, 0.0), axis=1, keepdims=True), cols)
            dd_ref[p] = cols.T[0:8] + jnp.where(krow == 2, rows[0], 0.0) + jnp.where(krow == 3, rows[1], 0.0)
        dcb_b = dcb.astype(BF16)
        dc_ref[...] = dc_acc + _dot(dcb_b, bm)
        db_ref[...] = db_acc + _dot(dcb_b, cm, _TN)

    sp = _ssd_specs(nc, True)
    dx, db, dc, dd, dsk = pl.pallas_call(
        body, name=name, grid=(bl, nc, N_GROUPS),
        in_specs=[sp["wide"], sp["wide"], sp["bmat"], sp["cmat"], sp["rows2"], sp["cols"], sp["cols"], sp["dskip"],
                  sp["state"]],
        out_specs=[sp["wide"], sp["narrow"], sp["narrow"], sp["rows8"],
                   pl.BlockSpec((N_PAIRS, 1, 128), lambda b, c, g: (0, 0, 0))],
        out_shape=[jax.ShapeDtypeStruct((t, SSM_W), BF16),
                   jax.ShapeDtypeStruct((t, N_GROUPS * N_STATE), F32),
                   jax.ShapeDtypeStruct((t, N_GROUPS * N_STATE), F32),
                   jax.ShapeDtypeStruct((bl * nc, N_PAIRS, 8, CHUNK), F32),
                   jax.ShapeDtypeStruct((N_PAIRS, 1, 128), F32)],
        scratch_shapes=[pltpu.VMEM((N_PAIRS, N_STATE, 128), F32)],
        compiler_params=_params("arbitrary", "arbitrary", "arbitrary"),
    )(dy, xa, xa, xa, ac4, dtC, acC, dskip_l, prev)
    ddtT = dd[:, :, 0:2, :].reshape(bl * nc, N_HEADS, CHUNK)
    dacT = dd[:, :, 2:4, :].reshape(bl * nc, N_HEADS, CHUNK)
    return dx, db, dc, ddtT, dacT, dsk.reshape(N_PAIRS, 128)


def gnorm_fwd(y, proj, w, name):
    t = y.shape[0]
    tr = min(256, t)
    zb = OFF_Z // SSM_W

    def body(y_ref, z_ref, w_ref, o_ref):
        z = z_ref[...].astype(F32)
        yg = y_ref[...].astype(F32) * z * _sigmoid(z)
        r = lax.rsqrt(jnp.mean(yg * yg, axis=-1, keepdims=True) + EPS)
        o_ref[...] = (yg * r * w_ref[...]).astype(BF16)

    return pl.pallas_call(
        body, name=name, grid=(t // tr,),
        in_specs=[pl.BlockSpec((tr, SSM_W), lambda i: (i, 0)), pl.BlockSpec((tr, SSM_W), lambda i: (i, zb)),
                  pl.BlockSpec((1, SSM_W), lambda i: (0, 0))],
        out_specs=pl.BlockSpec((tr, SSM_W), lambda i: (i, 0)),
        out_shape=jax.ShapeDtypeStruct((t, SSM_W), BF16),
        compiler_params=_params("parallel"),
    )(y, proj, w.reshape(1, SSM_W))


def gnorm_bwd(ds, y, proj, w, name):
    t = y.shape[0]
    tr = min(256, t)
    zb = OFF_Z // SSM_W

    def body(ds_ref, y_ref, z_ref, w_ref, dy_ref, dz_ref, dw_ref):
        z = z_ref[...].astype(F32)
        yv = y_ref[...].astype(F32)
        sg = _sigmoid(z)
        sz = z * sg
        yg = yv * sz
        r = lax.rsqrt(jnp.mean(yg * yg, axis=-1, keepdims=True) + EPS)
        xh = yg * r
        g = ds_ref[...].astype(F32)
        dxh = g * w_ref[...]
        dyg = r * (dxh - xh * jnp.mean(dxh * xh, axis=-1, keepdims=True))
        dy_ref[...] = (dyg * sz).astype(BF16)
        dz_ref[...] = (dyg * yv * sg * (1.0 + z * (1.0 - sg))).astype(BF16)
        part = jnp.sum(g * xh, axis=0, keepdims=True)

        @pl.when(pl.program_id(0) == 0)
        def _():
            dw_ref[...] = part

        @pl.when(pl.program_id(0) > 0)
        def _():
            dw_ref[...] += part

    return pl.pallas_call(
        body, name=name, grid=(t // tr,),
        in_specs=[pl.BlockSpec((tr, SSM_W), lambda i: (i, 0)), pl.BlockSpec((tr, SSM_W), lambda i: (i, 0)),
                  pl.BlockSpec((tr, SSM_W), lambda i: (i, zb)), pl.BlockSpec((1, SSM_W), lambda i: (0, 0))],
        out_specs=[pl.BlockSpec((tr, SSM_W), lambda i: (i, 0)), pl.BlockSpec((tr, SSM_W), lambda i: (i, 0)),
                   pl.BlockSpec((1, SSM_W), lambda i: (0, 0))],
        out_shape=[jax.ShapeDtypeStruct((t, SSM_W), BF16), jax.ShapeDtypeStruct((t, SSM_W), BF16),
                   jax.ShapeDtypeStruct((1, SSM_W), F32)],
        compiler_params=_params("arbitrary"),
    )(ds, y, proj, w.reshape(1, SSM_W))


def _pool_mixed(u, g, row):
    win = 2 << g
    acc = u
    for k in range(g + 1):
        acc = acc + _shift_down(acc, 1 << k, row)
    inv = 1.0 / jnp.minimum(row + 1, win).astype(F32)
    return acc * inv - u, inv


def pool_fwd(proj, pool_w, pool_scale, bl, s, name):
    t = bl * s

    def body(u_ref, g_ref, w_ref, sc_ref, o_ref):
        row = lax.broadcasted_iota(jnp.int32, (s, POOL_GD), 0)
        for g in range(POOL_G):
            cs = slice(g * POOL_GD, (g + 1) * POOL_GD)
            u = u_ref[:, cs].astype(F32)
            mixed, _ = _pool_mixed(u, g, row)
            pm = _dot(mixed.astype(BF16), w_ref[g])
            gate = g_ref[:, cs].astype(F32)
            o_ref[:, cs] = (pm * sc_ref[:, cs] * gate * _sigmoid(gate)).astype(BF16)

    return pl.pallas_call(
        body, name=name, grid=(bl,),
        in_specs=[pl.BlockSpec((s, POOL_W), lambda b: (b, OFF_PU // POOL_W)),
                  pl.BlockSpec((s, POOL_W), lambda b: (b, OFF_PG // POOL_W)),
                  pl.BlockSpec((POOL_G, POOL_GD, POOL_GD), lambda b: (0, 0, 0)),
                  pl.BlockSpec((1, POOL_W), lambda b: (0, 0))],
        out_specs=pl.BlockSpec((s, POOL_W), lambda b: (b, 0)),
        out_shape=jax.ShapeDtypeStruct((t, POOL_W), BF16),
        compiler_params=_params("parallel"),
    )(proj, proj, pool_w, pool_scale.reshape(1, POOL_W))


def pool_bwd(dp, proj, pool_w, pool_scale, bl, s, name):
    t = bl * s

    def body(dp_ref, u_ref, g_ref, w_ref, sc_ref, du_ref, dg_ref, dw_ref, dsc_ref):
        row = lax.broadcasted_iota(jnp.int32, (s, POOL_GD), 0)
        first = pl.program_id(0) == 0
        for g in range(POOL_G):
            cs = slice(g * POOL_GD, (g + 1) * POOL_GD)
            u = u_ref[:, cs].astype(F32)
            mixed, inv = _pool_mixed(u, g, row)
            mixed_b = mixed.astype(BF16)
            wg = w_ref[g]
            pm = _dot(mixed_b, wg)
            gate = g_ref[:, cs].astype(F32)
            sg = _sigmoid(gate)
            d = dp_ref[:, cs].astype(F32)
            sc = sc_ref[:, cs]
            dpm = (d * sc * gate * sg).astype(BF16)
            dg_ref[:, cs] = (d * pm * sc * sg * (1.0 + gate * (1.0 - sg))).astype(BF16)
            dsc = jnp.sum(d * pm * gate * sg, axis=0, keepdims=True)
            dwg = _dot(mixed_b, dpm, _TN)
            dmixed = _dot(dpm, wg, _NT)
            acc = dmixed * inv
            for k in range(g + 1):
                acc = acc + _shift_up(acc, 1 << k, row, s)
            du_ref[:, cs] = (acc - dmixed).astype(BF16)

            @pl.when(first)
            def _():
                dw_ref[g] = dwg
                dsc_ref[:, cs] = dsc

            @pl.when(jnp.logical_not(first))
            def _():
                dw_ref[g] = dw_ref[g] + dwg
                dsc_ref[:, cs] = dsc_ref[:, cs] + dsc

    return pl.pallas_call(
        body, name=name, grid=(bl,),
        in_specs=[pl.BlockSpec((s, POOL_W), lambda b: (b, 0)),
                  pl.BlockSpec((s, POOL_W), lambda b: (b, OFF_PU // POOL_W)),
                  pl.BlockSpec((s, POOL_W), lambda b: (b, OFF_PG // POOL_W)),
                  pl.BlockSpec((POOL_G, POOL_GD, POOL_GD), lambda b: (0, 0, 0)),
                  pl.BlockSpec((1, POOL_W), lambda b: (0, 0))],
        out_specs=[pl.BlockSpec((s, POOL_W), lambda b: (b, 0)), pl.BlockSpec((s, POOL_W), lambda b: (b, 0)),
                   pl.BlockSpec((POOL_G, POOL_GD, POOL_GD), lambda b: (0, 0, 0)),
                   pl.BlockSpec((1, POOL_W), lambda b: (0, 0))],
        out_shape=[jax.ShapeDtypeStruct((t, POOL_W), BF16), jax.ShapeDtypeStruct((t, POOL_W), BF16),
                   jax.ShapeDtypeStruct((POOL_G, POOL_GD, POOL_GD), F32), jax.ShapeDtypeStruct((1, POOL_W), F32)],
        compiler_params=_params("arbitrary"),
    )(dp, proj, proj, pool_w, pool_scale.reshape(1, POOL_W))


SB_SCALE = 64 ** -0.5


KB = 256


def _sb_block(qe, kj, mask, rr, upper):
    z = _dot(qe, kj, _NT)
    lb = jnp.minimum(z, 0.0) - jnp.log(1.0 + jnp.exp(-jnp.abs(z)))
    lom = lb - z if mask is None else jnp.where(mask, lb - z, 0.0)
    later = _dot(lom.astype(BF16), upper) + rr
    return lb, lom, later


def _sb_masks(i):
    lane = lax.broadcasted_iota(jnp.int32, (QB, 128), 1)
    row = lax.broadcasted_iota(jnp.int32, (2 * QB, KB), 0) % QB
    col = lax.broadcasted_iota(jnp.int32, (2 * QB, KB), 1)
    causal = lambda jb: col + (jb * KB - i * QB) < row
    return lane, lane < 64, causal


def _stack_heads(x, left):
    zero = jnp.zeros_like(x)
    return jnp.concatenate([jnp.where(left, x, zero), jnp.where(left, zero, x)], axis=0)


SB_GROUP = 4
SB_GW = SB_GROUP * 128


def sb_fwd(proj, bl, s, name, comm=None):
    t = bl * s
    nq = s // QB
    qb0, kb0, vb0, gb0 = OFF_QKV // SB_GW, (OFF_QKV + SB_W) // SB_GW, (OFF_QKV + 2 * SB_W) // SB_GW, OFF_SBG // SB_GW
    grid = (bl, SB_PAIRS // SB_GROUP, nq)
    x_in, x_args, x_out, x_shape, x_scratch, x_start, x_wait = _hosted_exchange(
        comm, grid, relay_at=(bl - 1, SB_PAIRS // SB_GROUP - 1, 0))

    def body(*refs):
        q_ref, k_ref, v_ref, g_ref = refs[:4]
        og_ref, o_ref, r_ref = refs[4 + len(x_in):7 + len(x_in)]
        x_refs = refs[4:4 + len(x_in)] + refs[7 + len(x_in):]
        x_start(x_refs)
        i = pl.program_id(2)
        lane, left, causal = _sb_masks(i)
        upper = _tri((KB, KB), lambda r, c: r > c).astype(BF16)
        cols = [slice(p * 128, (p + 1) * 128) for p in range(SB_GROUP)]
        qcats = [_stack_heads(q_ref[:, cs] * SB_SCALE, left) for cs in cols]
        zero = qcats[0].astype(F32) * 0.0

        def block(jb, carry, diagonal):
            rows = pl.ds(pl.multiple_of(jb * KB, KB), KB)
            mask = causal(jb) if diagonal else None
            out = []
            for p, cs in enumerate(cols):
                acc, rr, rt = carry[p]
                lb, lom, later = _sb_block(qcats[p], k_ref[rows, cs], mask, rr, upper)
                att = jnp.exp(lb + later)
                if diagonal:
                    att = jnp.where(mask, att, 0.0)
                acc = acc + _dot(att.astype(BF16), v_ref[rows, cs])
                rt = jnp.where(lane == jb, rr[:QB], jnp.where(lane == 8 + jb, rr[QB:], rt))
                out.append((acc, rr + jnp.sum(lom, axis=1, keepdims=True), rt))
            return tuple(out)

        carry = block(i, tuple((zero, zero[:, :1], zero[:QB]) for _ in cols), True)
        carry = lax.fori_loop(0, i, lambda jj, c: block(i - 1 - jj, c, False), carry)
        for p, cs in enumerate(cols):
            acc, _, rtile = carry[p]
            o = jnp.where(left, acc[:QB], acc[QB:])
            gate = g_ref[:, cs].astype(F32)
            o_ref[:, cs] = o.astype(BF16)
            og_ref[:, cs] = (o * gate * _sigmoid(gate)).astype(BF16)
            r_ref[p] = rtile
        x_wait(x_refs)

    rowblk = lambda b, g, i: (b * nq + i, g)
    return pl.pallas_call(
        body, name=name, grid=grid,
        in_specs=[pl.BlockSpec((QB, SB_GW), lambda b, g, i: (b * nq + i, qb0 + g)),
                  pl.BlockSpec((s, SB_GW), lambda b, g, i: (b, kb0 + g)),
                  pl.BlockSpec((s, SB_GW), lambda b, g, i: (b, vb0 + g)),
                  pl.BlockSpec((QB, SB_GW), lambda b, g, i: (b * nq + i, gb0 + g))] + x_in,
        out_specs=[pl.BlockSpec((QB, SB_GW), rowblk), pl.BlockSpec((QB, SB_GW), rowblk),
                   pl.BlockSpec((None, SB_GROUP, QB, 128), lambda b, g, i: (b * nq + i, g, 0, 0))] + x_out,
        out_shape=[jax.ShapeDtypeStruct((t, SB_W), BF16), jax.ShapeDtypeStruct((t, SB_W), BF16),
                   jax.ShapeDtypeStruct((bl * nq, SB_PAIRS, QB, 128), F32)] + x_shape,
        scratch_shapes=x_scratch,
        compiler_params=_params("arbitrary", "arbitrary", "arbitrary"),
    )(proj, proj, proj, proj, *x_args)


def sb_bwd(dsb, o, rsave, proj, bl, s, name, comm=None):
    t = bl * s
    nq = s // QB
    qb0, kb0, vb0, gb0 = OFF_QKV // SB_GW, (OFF_QKV + SB_W) // SB_GW, (OFF_QKV + 2 * SB_W) // SB_GW, OFF_SBG // SB_GW
    grid = (bl, SB_PAIRS // SB_GROUP, nq)
    x_in, x_args, x_out, x_shape, x_scratch, x_start, x_wait = _hosted_exchange(
        comm, grid, relay_at=(bl - 1, SB_PAIRS // SB_GROUP - 1, 0))

    def body(*refs):
        n = len(x_in)
        d_ref, o_ref, r_ref, q_ref, k_ref, v_ref, g_ref = refs[:7]
        dq_ref, dk_ref, dv_ref, dg_ref = refs[7 + n:11 + n]
        dk_acc, dv_acc = refs[11 + 2 * n:13 + 2 * n]
        x_refs = refs[7:7 + n] + refs[11 + n:11 + 2 * n] + refs[13 + 2 * n:]
        x_start(x_refs)
        i = pl.program_id(2)

        @pl.when(i == 0)
        def _():
            dk_acc[...] = jnp.zeros(dk_acc.shape, F32)
            dv_acc[...] = jnp.zeros(dv_acc.shape, F32)

        lane, left, causal = _sb_masks(i)
        upper = _tri((KB, KB), lambda r, c: r > c).astype(BF16)
        lower = _tri((KB, KB), lambda r, c: r < c).astype(BF16)
        cols = [slice(p * 128, (p + 1) * 128) for p in range(SB_GROUP)]
        qcats, docats = [], []
        for cs in cols:
            gate = g_ref[:, cs].astype(F32)
            sg = _sigmoid(gate)
            d = d_ref[:, cs].astype(F32)
            dg_ref[:, cs] = (d * o_ref[:, cs].astype(F32) * sg * (1.0 + gate * (1.0 - sg))).astype(BF16)
            docats.append(_stack_heads((d * gate * sg).astype(BF16), left))
            qcats.append(_stack_heads(q_ref[:, cs] * SB_SCALE, left))
        qcat_ts = [qc.astype(F32).T.astype(BF16) for qc in qcats]
        docat_ts = [dc.astype(F32).T.astype(BF16) for dc in docats]
        zero = qcats[0].astype(F32) * 0.0

        def block(jb, carry, diagonal):
            rows = pl.ds(pl.multiple_of(jb * KB, KB), KB)
            mask = causal(jb) if diagonal else None
            out = []
            for p, cs in enumerate(cols):
                dq, gcar = carry[p]
                kj = k_ref[rows, cs]
                vj = v_ref[rows, cs]
                rtile = r_ref[p]
                rr = jnp.concatenate(
                    [jnp.sum(jnp.where(lane == jb, rtile, 0.0), axis=1, keepdims=True),
                     jnp.sum(jnp.where(lane == 8 + jb, rtile, 0.0), axis=1, keepdims=True)], axis=0)
                lb, lom, later = _sb_block(qcats[p], kj, mask, rr, upper)
                att = jnp.exp(lb + later)
                if diagonal:
                    att = jnp.where(mask, att, 0.0)
                de = att * _dot(docats[p], vj, _NT)
                gpre = _dot(de.astype(BF16), lower) + gcar
                sig = jnp.exp(lb)
                dz = de * (1.0 - sig) - gpre * sig
                if diagonal:
                    dz = jnp.where(mask, dz, 0.0)
                dz = dz.astype(BF16)
                dk_acc[jb, cs, :] = dk_acc[jb, cs, :] + _dot(qcat_ts[p], dz)
                dv_acc[jb, cs, :] = dv_acc[jb, cs, :] + _dot(docat_ts[p], att.astype(BF16))
                out.append((dq + _dot(dz, kj), gcar + jnp.sum(de, axis=1, keepdims=True)))
            return tuple(out)

        carry = lax.fori_loop(0, i, lambda jb, c: block(jb, c, False), tuple((zero, zero[:, :1]) for _ in cols))
        carry = block(i, carry, True)
        for p, cs in enumerate(cols):
            dq = carry[p][0]
            dq_ref[:, cs] = (jnp.where(left, dq[:QB], dq[QB:]) * SB_SCALE).astype(BF16)

        @pl.when(i == nq - 1)
        def _():
            for kb in range(s // KB):
                for cs in cols:
                    dk_ref[kb * KB:(kb + 1) * KB, cs] = dk_acc[kb, cs, :].T.astype(BF16)
                    dv_ref[kb * KB:(kb + 1) * KB, cs] = dv_acc[kb, cs, :].T.astype(BF16)

        x_wait(x_refs)

    rowblk = lambda b, g, i: (b * nq + i, g)
    seqblk = lambda b, g, i: (b, g)
    return pl.pallas_call(
        body, name=name, grid=grid,
        in_specs=[pl.BlockSpec((QB, SB_GW), rowblk), pl.BlockSpec((QB, SB_GW), rowblk),
                  pl.BlockSpec((None, SB_GROUP, QB, 128), lambda b, g, i: (b * nq + i, g, 0, 0)),
                  pl.BlockSpec((QB, SB_GW), lambda b, g, i: (b * nq + i, qb0 + g)),
                  pl.BlockSpec((s, SB_GW), lambda b, g, i: (b, kb0 + g)),
                  pl.BlockSpec((s, SB_GW), lambda b, g, i: (b, vb0 + g)),
                  pl.BlockSpec((QB, SB_GW), lambda b, g, i: (b * nq + i, gb0 + g))] + x_in,
        out_specs=[pl.BlockSpec((QB, SB_GW), rowblk), pl.BlockSpec((s, SB_GW), seqblk),
                   pl.BlockSpec((s, SB_GW), seqblk), pl.BlockSpec((QB, SB_GW), rowblk)] + x_out,
        out_shape=[jax.ShapeDtypeStruct((t, SB_W), BF16)] * 4 + x_shape,
        scratch_shapes=[pltpu.VMEM((s // KB, SB_GW, KB), F32), pltpu.VMEM((s // KB, SB_GW, KB), F32)] + x_scratch,
        compiler_params=_params("arbitrary", "arbitrary", "arbitrary"),
    )(dsb, o, rsave, proj, proj, proj, proj, *x_args)


def merge_fwd(proj, ys, yp, yb, name):
    t = ys.shape[0]
    tr = min(512, t)

    def body(m_ref, ys_ref, yp_ref, yb_ref, o_ref):
        acc = jnp.zeros((tr, D), F32)
        for k, ref in enumerate((ys_ref, yp_ref, yb_ref)):
            acc = acc + _sigmoid(m_ref[:, k * D:(k + 1) * D].astype(F32)) * ref[...].astype(F32)
        o_ref[...] = acc.astype(BF16)

    rowblk = pl.BlockSpec((tr, D), lambda i: (i, 0))
    return pl.pallas_call(
        body, name=name, grid=(t // tr,),
        in_specs=[pl.BlockSpec((tr, 3 * D), lambda i: (i, 0)), rowblk, rowblk, rowblk],
        out_specs=rowblk,
        out_shape=jax.ShapeDtypeStruct((t, D), BF16),
        compiler_params=_params("parallel"),
    )(proj, ys, yp, yb)


def merge_bwd(dm, proj, ys, yp, yb, name):
    t = ys.shape[0]
    tr = min(512, t)

    def body(dm_ref, m_ref, ys_ref, yp_ref, yb_ref, d0_ref, d1_ref, d2_ref, dl_ref):
        dmv = dm_ref[...].astype(F32)
        for k, (ref, dref) in enumerate(((ys_ref, d0_ref), (yp_ref, d1_ref), (yb_ref, d2_ref))):
            g = _sigmoid(m_ref[:, k * D:(k + 1) * D].astype(F32))
            dref[...] = (g * dmv).astype(BF16)
            dl_ref[:, k * D:(k + 1) * D] = (dmv * ref[...].astype(F32) * g * (1.0 - g)).astype(BF16)

    rowblk = pl.BlockSpec((tr, D), lambda i: (i, 0))
    wide = pl.BlockSpec((tr, 3 * D), lambda i: (i, 0))
    return pl.pallas_call(
        body, name=name, grid=(t // tr,),
        in_specs=[rowblk, wide, rowblk, rowblk, rowblk],
        out_specs=[rowblk, rowblk, rowblk, wide],
        out_shape=[jax.ShapeDtypeStruct((t, D), BF16)] * 3 + [jax.ShapeDtypeStruct((t, 3 * D), BF16)],
        compiler_params=_params("parallel"),
    )(dm, proj, ys, yp, yb)


def layer_fwd(x, lw, bl, s, tag, comm=None):
    t = bl * s
    h = rmsnorm_fwd(x, lw["norm_w"], f"norm_fwd{tag}")
    proj = matmul(h, lw["w_in"], "nt", BF16, f"in_proj{tag}", tn=2048)
    xa = conv_fwd(proj, lw["conv_w"], lw["conv_b"], bl, s, f"conv_fwd{tag}")
    dtT, acT, dtC, acC = dt_fwd(proj, lw["dt_bias"], lw["a_log"], t, f"dt_fwd{tag}")
    dskip_l = jnp.repeat(lw["d_skip"], 64).reshape(1, SSM_W)
    y, prev = ssd_fwd(xa, acT, dtC, acC, dskip_l, bl, s, f"ssd_fwd{tag}")
    s_out = gnorm_fwd(y, proj, lw["ssm_norm_w"], f"gnorm_fwd{tag}")
    p_out = pool_fwd(proj, lw["pool_w"], lw["pool_scale"], bl, s, f"pool_fwd{tag}")
    sb_out, sb_o, sb_r, *carried = sb_fwd(proj, bl, s, f"sb_fwd{tag}", comm)
    ys = matmul(s_out, lw["w_proj_ssm"], "nn", BF16, f"proj_ssm{tag}")
    yp = matmul(p_out, lw["w_proj_pool"], "nn", BF16, f"proj_pool{tag}")
    yb = matmul(sb_out, lw["w_proj_sb"], "nn", BF16, f"proj_sb{tag}")
    merged = merge_fwd(proj, ys, yp, yb, f"merge_fwd{tag}")
    x_next = matmul(merged, lw["w_out"], "nn", F32, f"out_proj{tag}", residual=x)
    saved = dict(x=x, h=h, proj=proj, xa=xa, dtT=dtT, acT=acT, dtC=dtC, acC=acC, y=y, prev=prev, s_out=s_out, p_out=p_out,
                 sb_out=sb_out, sb_o=sb_o, sb_r=sb_r, ys=ys, yp=yp, yb=yb, merged=merged)
    return x_next, saved, (carried[0] if carried else None)


def layer_bwd(dx, dx_b, lw, sv, bl, s, tag, comm, own_slabs):
    t = bl * s
    g = {}
    dmerged = matmul(dx_b, lw["w_out"], "nt", BF16, f"d_merged{tag}")
    g["w_out"] = matmul(sv["merged"], dx_b, "tn", BF16, f"dw_out{tag}")
    dys, dyp, dyb, dlogit = merge_bwd(dmerged, sv["proj"], sv["ys"], sv["yp"], sv["yb"], f"merge_bwd{tag}")
    ds_out = matmul(dys, lw["w_proj_ssm"], "nt", BF16, f"d_sout{tag}")
    g["w_proj_ssm"] = matmul(sv["s_out"], dys, "tn", BF16, f"dw_proj_ssm{tag}")
    dp_out = matmul(dyp, lw["w_proj_pool"], "nt", BF16, f"d_pout{tag}")
    g["w_proj_pool"] = matmul(sv["p_out"], dyp, "tn", BF16, f"dw_proj_pool{tag}")
    dsb_out = matmul(dyb, lw["w_proj_sb"], "nt", BF16, f"d_sbout{tag}")
    g["w_proj_sb"] = matmul(sv["sb_out"], dyb, "tn", BF16, f"dw_proj_sb{tag}")
    dy, dz, dnw = gnorm_bwd(ds_out, sv["y"], sv["proj"], lw["ssm_norm_w"], f"gnorm_bwd{tag}")
    g["ssm_norm_w"] = dnw[0]
    dskip_l = jnp.repeat(lw["d_skip"], 64).reshape(1, SSM_W)
    dxs, db, dc, ddtT, dacT, dsk = ssd_bwd(dy, sv["xa"], sv["acT"], sv["dtC"], sv["acC"], dskip_l, sv["prev"], bl, s,
                                           f"ssd_bwd{tag}")
    g["d_skip"] = jnp.sum(dsk.reshape(N_HEADS, 64), axis=1)
    ddt_raw, da, dbias = dt_bwd(ddtT, dacT, sv["dtT"], sv["proj"], lw["dt_bias"], lw["a_log"], t, f"dt_bwd{tag}")
    g["a_log"] = da * (-jnp.exp(lw["a_log"]))
    g["dt_bias"] = dbias
    dxa = jnp.concatenate([dxs, db.astype(BF16), dc.astype(BF16)], axis=1)
    dxbc, dcw, dcb = conv_bwd(dxa, sv["proj"], lw["conv_w"], lw["conv_b"], bl, s, f"conv_bwd{tag}")
    g["conv_w"] = dcw
    g["conv_b"] = dcb
    dpu, dpg, dpw, dpsc = pool_bwd(dp_out, sv["proj"], lw["pool_w"], lw["pool_scale"], bl, s, f"pool_bwd{tag}")
    g["pool_w"] = dpw
    g["pool_scale"] = dpsc[0]
    dq, dk, dv, dsbg, *carried = sb_bwd(dsb_out, sv["sb_o"], sv["sb_r"], sv["proj"], bl, s, f"sb_bwd{tag}", comm)
    dproj = concat_columns([dlogit, dsbg, dpu, dpg, dz, dq, dk, dv, dxbc, ddt_raw], PC, f"d_proj{tag}")
    g["w_in"] = matmul(dproj, sv["h"], "tn", BF16, f"dw_in{tag}", tk=2048)
    own_comm = own_slabs(g) if own_slabs is not None else None
    dh = matmul(dproj, lw["w_in"], "nn", F32, f"d_h{tag}", tk=2048, comm=own_comm)
    dh, own_got = dh if own_comm is not None else (dh, None)
    dx_in, dx_in_b, dnorm = rmsnorm_bwd(dh, sv["x"], lw["norm_w"], dx, f"norm_bwd{tag}")
    g["norm_w"] = dnorm[0]
    return dx_in, dx_in_b, g, (carried[0] if carried else None), (own_comm[0] if own_comm else None), own_got


def concat_columns(parts, width, name):
    t = parts[0].shape[0]
    tr = min(256, t)
    widths = [p.shape[1] for p in parts]
    used = sum(widths)

    def body(*refs):
        o_ref = refs[-1]
        off = 0
        for ref, w in zip(refs[:-1], widths):
            o_ref[:, off:off + w] = ref[...]
            off += w
        if width > used:
            o_ref[:, used:] = jnp.zeros((tr, width - used), BF16)

    return pl.pallas_call(
        body, name=name, grid=(t // tr,),
        in_specs=[pl.BlockSpec((tr, w), lambda i: (i, 0)) for w in widths],
        out_specs=pl.BlockSpec((tr, width), lambda i: (i, 0)),
        out_shape=jax.ShapeDtypeStruct((t, width), BF16),
        compiler_params=_params("parallel"),
    )(*parts)


_PAD_PIECES = ((10784, 3072), (9760, 1024), (4640, 1024), (5664, 1024), (0, 2048), (6688, 3072), (2048, 2560), (4608, 32))
_UNPAD_PIECES = ((OFF_Z, 2048), (OFF_XBC, 2560), (OFF_DT, 32), (OFF_PU, 1024), (OFF_PG, 1024), (OFF_QKV, 3072),
                 (OFF_SBG, 1024), (OFF_MERGE, 3072))


def pad_rows(wt):
    pieces = [wt[o:o + n] for o, n in _PAD_PIECES]
    return jnp.concatenate(pieces + [jnp.zeros((PC - IN_COLS, wt.shape[1]), wt.dtype)], axis=0)


def unpad_rows(wp):
    return jnp.concatenate([wp[o:o + n] for o, n in _UNPAD_PIECES], axis=0)


MESH = pl.DeviceIdType.MESH
ANY = pl.BlockSpec(memory_space=pl.ANY)


def _coords():
    return lax.axis_index("x"), lax.axis_index("y"), lax.axis_index("c")


def _peer(p):
    x, y, c = _coords()
    return (1 - x if p & 4 else x, 1 - y if p & 2 else y, 1 - c if p & 1 else c)


def _flat(pos):
    return 4 * pos[0] + 2 * pos[1] + pos[2]


def _chip(pos):
    return 2 * pos[0] + pos[1]


def _exchange_copies(v_ref, out_ref, send_sems, recv_sems, local_sem, mode):
    x, y, c = _coords()
    me = _flat((x, y, c))
    sibling = (x, y, 1 - c)
    chips = [(1 - x if j & 2 else x, 1 - y if j & 1 else y) for j in range(1, 4)]

    def copy(k, src, landing, to):
        return pltpu.make_async_remote_copy(src_ref=src, dst_ref=out_ref.at[landing], send_sem=send_sems.at[k],
                                            recv_sem=recv_sems.at[k], device_id=to, device_id_type=MESH)

    if mode == "direct":
        local = pltpu.make_async_copy(v_ref, out_ref.at[me], local_sem)
        first = [copy(p - 1, v_ref, me, _peer(p)) for p in range(1, N_DEV)]
        last = [copy(p - 1, v_ref, _flat(_peer(p)), _peer(p)) for p in range(1, N_DEV)]
        return local, first, [], last
    if mode == "gather":
        local = pltpu.make_async_copy(v_ref, out_ref.at[me], local_sem)
        first = [copy(0, v_ref, me, sibling)] + [copy(1 + j, v_ref, me, (*ch, c)) for j, ch in enumerate(chips)]
        relay = [(copy(1 + j, v_ref, _flat((*ch, c)), (*ch, c)),
                  copy(4 + j, out_ref.at[_flat((*ch, c))], _flat((*ch, c)), sibling)) for j, ch in enumerate(chips)]
        last = [copy(0, v_ref, _flat(sibling), sibling)] + [
            copy(4 + j, v_ref, _flat((*ch, 1 - c)), sibling) for j, ch in enumerate(chips)]
        return local, first, relay, last
    assert mode == "chips"
    mine = _chip((x, y))
    local = pltpu.make_async_copy(v_ref.at[mine], out_ref.at[mine], local_sem)
    first = [copy(j, v_ref.at[_chip(ch)], mine, (*ch, c)) for j, ch in enumerate(chips)]
    last = [copy(j, v_ref.at[mine], _chip(ch), (*ch, c)) for j, ch in enumerate(chips)]
    return local, first, [], last


def _swap_copies(s0_ref, s1_ref, out_ref, send_sems, recv_sems, local_sem):
    x, y, c = _coords()

    def four(src_ref):
        return [pltpu.make_async_remote_copy(src_ref=src_ref.at[j], dst_ref=out_ref.at[j], send_sem=send_sems.at[j],
                                             recv_sem=recv_sems.at[j], device_id=(x, y, 1 - c), device_id_type=MESH)
                for j in range(4)]

    return c, four(s1_ref), four(s0_ref)


def _exchange_start(*refs_and_mode):
    if refs_and_mode[-1] == "swap":
        c, from_core0, from_core1 = _swap_copies(*refs_and_mode[:-1])
        for core, copies in ((0, from_core0), (1, from_core1)):
            @pl.when(c == core)
            def _():
                for cp in copies:
                    cp.start()
        return
    local, first, _, _ = _exchange_copies(*refs_and_mode)
    local.start()
    for cp in first:
        cp.start()


def _exchange_relay(*refs_and_mode):
    if refs_and_mode[-1] == "swap":
        return
    for arrival, onward in _exchange_copies(*refs_and_mode)[2]:
        arrival.wait_recv()
        onward.start()


def _exchange_finish(*refs_and_mode):
    if refs_and_mode[-1] == "swap":
        _, four, _ = _swap_copies(*refs_and_mode[:-1])
        for cp in four:
            cp.wait_recv()
        for cp in four:
            cp.wait_send()
        return
    local, first, relay, last = _exchange_copies(*refs_and_mode)
    for cp in last:
        cp.wait_recv()
    for cp in first + [onward for _, onward in relay]:
        cp.wait_send()
    local.wait()


def _exchange_shape(v, mode):
    if mode == "swap":
        return jax.ShapeDtypeStruct(tuple(v[0].shape), v[0].dtype)
    return jax.ShapeDtypeStruct(tuple(v.shape) if mode == "chips" else (N_DEV,) + tuple(v.shape), v.dtype)


def _exchange_sems():
    return [pltpu.SemaphoreType.DMA((N_DEV - 1,)), pltpu.SemaphoreType.DMA((N_DEV - 1,)), pltpu.SemaphoreType.DMA]


def exchange(v, mode, name):
    def body(*refs):
        _exchange_start(*refs, mode)
        _exchange_relay(*refs, mode)
        _exchange_finish(*refs, mode)

    args = list(v) if mode == "swap" else [v]
    return pl.pallas_call(
        body, name=name,
        in_specs=[ANY] * len(args), out_specs=ANY,
        out_shape=_exchange_shape(v, mode),
        scratch_shapes=_exchange_sems(),
    )(*args)


def _hosted_exchange(comm, grid, relay_at):
    if comm is None:
        return [], [], [], [], [], (lambda refs: None), (lambda refs: None)
    v, mode = comm

    def at(step):
        cond = None
        for axis, want in enumerate(step):
            term = pl.program_id(axis) == want
            cond = term if cond is None else jnp.logical_and(cond, term)
        return cond

    def start(refs):
        @pl.when(at([0] * len(grid)))
        def _():
            _exchange_start(*refs, mode)

        if mode == "gather":
            @pl.when(at(relay_at))
            def _():
                _exchange_relay(*refs, mode)

    def wait(refs):
        @pl.when(at([n - 1 for n in grid]))
        def _():
            _exchange_finish(*refs, mode)

    args = list(v) if mode == "swap" else [v]
    return [ANY] * len(args), args, [ANY], [_exchange_shape(v, mode)], _exchange_sems(), start, wait


def pair_sum(slabs, got, name):
    _, r, c = got.shape
    tr = r // 4 if r % 64 == 0 else r

    def body(s0_ref, s1_ref, got_ref, o_ref):
        mine = jnp.where(lax.axis_index("c") == 0, s0_ref[...].astype(F32), s1_ref[...].astype(F32))
        o_ref[...] = (mine + got_ref[...].astype(F32)).astype(BF16)

    blk = pl.BlockSpec((None, tr, c), lambda j, i: (j, i, 0))
    return pl.pallas_call(
        body, name=name, grid=(4, r // tr),
        in_specs=[blk, blk, blk], out_specs=blk,
        out_shape=jax.ShapeDtypeStruct(got.shape, BF16),
        compiler_params=_params("parallel", "parallel"),
    )(*slabs, got)


def sum_slabs(v, name):
    n, r, c = v.shape
    tr = 128 if r % 128 == 0 else r

    def body(v_ref, o_ref):
        acc = v_ref[0].astype(F32)
        for k in range(1, n):
            acc = acc + v_ref[k].astype(F32)
        o_ref[...] = acc

    return pl.pallas_call(
        body, name=name, grid=(r // tr,),
        in_specs=[pl.BlockSpec((n, tr, c), lambda i: (0, i, 0))],
        out_specs=pl.BlockSpec((tr, c), lambda i: (i, 0)),
        out_shape=jax.ShapeDtypeStruct((r, c), F32),
        compiler_params=_params("parallel"),
    )(v)


def adamw(w, g, m, v, name):
    r, c = w.shape
    tr = next((cand for cand in (256, 128, 64, 32, 16, 8) if r % cand == 0), r)

    def body(w_ref, g_ref, m_ref, v_ref, d_ref, mo_ref, vo_ref):
        gv = g_ref[...]
        mn = ADAM_B1 * m_ref[...] + (1.0 - ADAM_B1) * gv
        vn = ADAM_B2 * v_ref[...] + (1.0 - ADAM_B2) * (gv * gv)
        m_hat = mn / (1.0 - ADAM_B1 ** ADAM_STEP)
        v_hat = vn / (1.0 - ADAM_B2 ** ADAM_STEP)
        d_ref[...] = -ADAM_LR * (m_hat / (jnp.sqrt(v_hat) + ADAM_EPS) + ADAM_WD * w_ref[...])
        mo_ref[...] = mn
        vo_ref[...] = vn

    blk = pl.BlockSpec((tr, c), lambda i: (i, 0))
    return pl.pallas_call(
        body, name=name, grid=(r // tr,),
        in_specs=[blk] * 4, out_specs=[blk] * 3,
        out_shape=[jax.ShapeDtypeStruct((r, c), F32)] * 3,
        compiler_params=_params("parallel"),
    )(w, g, m, v)


BIG = ("w_proj_ssm", "w_proj_pool", "w_proj_sb", "w_out", "pool_w", "w_in")
SHARD_IN = IN_COLS // N_DEV
BIG_ROWS = {"w_proj_ssm": SSM_W // N_DEV, "w_proj_pool": POOL_W // N_DEV, "w_proj_sb": SB_W // N_DEV,
            "w_out": D // N_DEV, "pool_w": POOL_G * (POOL_GD // N_DEV) * POOL_GD // D, "w_in": SHARD_IN}
PACK_C = D
PACK_R = 2432

REPLICATED = ("norm_w", "conv_b", "dt_bias", "a_log", "d_skip", "ssm_norm_w", "pool_scale")
WEIGHTS = ("norm_w", "w_in", "conv_w", "conv_b", "dt_bias", "a_log", "d_skip", "ssm_norm_w", "pool_w",
           "pool_scale", "w_proj_ssm", "w_proj_pool", "w_proj_sb", "w_out", "final_norm_w")


def _size(shape):
    n = 1
    for d in shape:
        n *= d
    return n


def _pad_flat(flat, n):
    return jnp.concatenate([flat, jnp.zeros((n - flat.shape[0],), flat.dtype)])


def _row_offsets():
    offs, off = {}, 0
    for n in BIG:
        offs[n] = off
        off += BIG_ROWS[n]
    return offs, off


def pack_shards(parts):
    rows = [parts[n].reshape(BIG_ROWS[n], PACK_C) for n in BIG]
    rows[-1] = jnp.pad(rows[-1], ((0, PACK_R - _row_offsets()[1]), (0, 0)))
    return jnp.concatenate(rows, axis=0)


def unpack_shards(packed):
    offs, _ = _row_offsets()
    out = {}
    for n in BIG:
        seg = packed[offs[n]:offs[n] + BIG_ROWS[n]]
        if n == "w_in":
            out[n] = seg.T
        elif n == "pool_w":
            out[n] = seg.reshape(POOL_G, POOL_GD // N_DEV, POOL_GD)
        else:
            out[n] = seg
    return out


def unpack_gathered(g):
    offs, _ = _row_offsets()
    out = {}
    for n in BIG:
        seg = g[:, offs[n]:offs[n] + BIG_ROWS[n], :]
        if n == "w_in":
            out[n] = pad_rows(seg.reshape(IN_COLS, D))
        elif n == "pool_w":
            out[n] = seg.reshape(N_DEV, POOL_G, POOL_GD // N_DEV, POOL_GD).transpose(1, 0, 2, 3).reshape(
                POOL_G, POOL_GD, POOL_GD)
        else:
            out[n] = seg.reshape(N_DEV * BIG_ROWS[n], D)
    return out


def pack_slabs(g):
    segs = []
    for n in BIG:
        if n == "w_in":
            w = unpad_rows(g[n])
        elif n == "pool_w":
            w = g[n].reshape(POOL_G, N_DEV, POOL_GD // N_DEV, POOL_GD).transpose(1, 0, 2, 3)
        else:
            w = g[n]
        segs.append(w.reshape(N_DEV // 2, 2, BIG_ROWS[n], PACK_C).astype(BF16))
    segs[-1] = jnp.pad(segs[-1], ((0, 0), (0, 0), (0, PACK_R - _row_offsets()[1]), (0, 0)))
    return tuple(jnp.concatenate([seg[:, core] for seg in segs], axis=1) for core in range(2))


SMALL_ROWS = 544


def pack_small(vals):
    flat = jnp.concatenate([v.reshape(-1) for v in vals])
    return _pad_flat(flat, SMALL_ROWS * 128).reshape(SMALL_ROWS, 128)


def unpack_small(packed, shapes):
    flat = packed.reshape(-1)
    out, off = [], 0
    for shp in shapes:
        out.append(flat[off:off + _size(shp)].reshape(shp))
        off += _size(shp)
    return out


def kernel(x, norm_w, w_in, conv_w, conv_b, dt_bias, a_log, d_skip, ssm_norm_w, pool_w, pool_scale, w_proj_ssm, w_proj_pool, w_proj_sb, w_out, final_norm_w, loss_target, m_norm_w, m_w_in, m_conv_w, m_conv_b, m_dt_bias, m_a_log, m_d_skip, m_ssm_norm_w, m_pool_w, m_pool_scale, m_w_proj_ssm, m_w_proj_pool, m_w_proj_sb, m_w_out, m_final_norm_w, v_norm_w, v_w_in, v_conv_w, v_conv_b, v_dt_bias, v_a_log, v_d_skip, v_ssm_norm_w, v_pool_w, v_pool_scale, v_w_proj_ssm, v_w_proj_pool, v_w_proj_sb, v_w_out, v_final_norm_w):
    wts = dict(norm_w=norm_w, w_in=w_in, conv_w=conv_w, conv_b=conv_b, dt_bias=dt_bias, a_log=a_log, d_skip=d_skip,
               ssm_norm_w=ssm_norm_w, pool_w=pool_w, pool_scale=pool_scale, w_proj_ssm=w_proj_ssm,
               w_proj_pool=w_proj_pool, w_proj_sb=w_proj_sb, w_out=w_out, final_norm_w=final_norm_w)
    mom = dict(norm_w=m_norm_w, w_in=m_w_in, conv_w=m_conv_w, conv_b=m_conv_b, dt_bias=m_dt_bias, a_log=m_a_log,
               d_skip=m_d_skip, ssm_norm_w=m_ssm_norm_w, pool_w=m_pool_w, pool_scale=m_pool_scale,
               w_proj_ssm=m_w_proj_ssm, w_proj_pool=m_w_proj_pool, w_proj_sb=m_w_proj_sb, w_out=m_w_out,
               final_norm_w=m_final_norm_w)
    var = dict(norm_w=v_norm_w, w_in=v_w_in, conv_w=v_conv_w, conv_b=v_conv_b, dt_bias=v_dt_bias, a_log=v_a_log,
               d_skip=v_d_skip, ssm_norm_w=v_ssm_norm_w, pool_w=v_pool_w, pool_scale=v_pool_scale,
               w_proj_ssm=v_w_proj_ssm, w_proj_pool=v_w_proj_pool, w_proj_sb=v_w_proj_sb, w_out=v_w_out,
               final_norm_w=v_final_norm_w)
    bl, s, _ = x.shape
    t = bl * s
    me = _flat(_coords())

    cw = exchange(conv_w.reshape(40, 128), "direct", "gather_conv_w")
    conv_w_full = cw.reshape(N_DEV, DEPTH, CONV_K, CONV_CH // N_DEV).transpose(1, 2, 0, 3).reshape(
        DEPTH, CONV_K, CONV_CH)

    xc = x.reshape(t, D)
    layer_w, saved = [], []
    packed = [pack_shards({n: (wts[n][l].T if n == "w_in" else wts[n][l]).astype(BF16) for n in BIG})
              for l in range(DEPTH)]
    gathered = exchange(packed[0], "gather", "gather_w0")
    for l in range(DEPTH):
        lw = unpack_gathered(gathered)
        for n in REPLICATED:
            lw[n] = wts[n][l]
        lw["conv_w"] = conv_w_full[l]
        xc, sv, gathered = layer_fwd(xc, lw, bl, s, f"_l{l}", (packed[l + 1], "gather") if l + 1 < DEPTH else None)
        layer_w.append(lw)
        saved.append(sv)

    loss_part, dx, dx_b, dfinal = final_loss(xc, final_norm_w, loss_target.reshape(t, D), "final_loss")
    loss = lax.psum(loss_part[0, 0], ("x", "y", "c"))

    grads = [None] * DEPTH
    big_sum = [None] * DEPTH
    def last_layer_slabs(g):
        slabs = pack_slabs(g)
        return pair_sum(slabs, exchange(slabs, "swap", "pair_swap0"), "pair_sum0"), "chips"

    chip_sums = None
    for l in reversed(range(DEPTH)):
        dx, dx_b, g, got, own_sent, own_got = layer_bwd(
            dx, dx_b, layer_w[l], saved[l], bl, s, f"_l{l}", (chip_sums, "chips") if chip_sums is not None else None,
            last_layer_slabs if l == 0 else (lambda g: (pack_slabs(g), "swap")))
        if got is not None:
            big_sum[l + 1] = unpack_shards(sum_slabs(got, f"sum_g{l + 1}"))
        grads[l] = g
        if l > 0:
            chip_sums = pair_sum(own_sent, own_got, f"pair_sum{l}")
    big_sum[0] = unpack_shards(sum_slabs(own_got, "sum_g0"))
    grad_x = dx.reshape(bl, s, D)

    small_names = REPLICATED + ("conv_w",)
    small_vals = [jnp.stack([grads[l][n] for l in range(DEPTH)]) for n in small_names] + [dfinal[0]]
    small_shapes = [v.shape for v in small_vals]
    small_all = exchange(pack_small(small_vals), "direct", "gather_small")
    small_sum = unpack_small(sum_slabs(small_all, "sum_small"), small_shapes)
    gsum = dict(zip(small_names + ("final_norm_w",), small_sum))
    conv_g_full = gsum["conv_w"]
    gsum["conv_w"] = lax.dynamic_slice_in_dim(conv_g_full, me * (CONV_CH // N_DEV), CONV_CH // N_DEV, axis=2)
    for n in BIG:
        gsum[n] = jnp.stack([big_sum[l][n] for l in range(DEPTH)])

    delta, new_m, new_v = {}, {}, {}
    for n in BIG + ("conv_w",):
        shp = wts[n].shape
        two_d = (-1, shp[-1])
        d2, m2, v2 = adamw(wts[n].reshape(two_d), gsum[n].reshape(two_d), mom[n].reshape(two_d),
                           var[n].reshape(two_d), f"adamw_{n}")
        delta[n], new_m[n], new_v[n] = d2.reshape(shp), m2.reshape(shp), v2.reshape(shp)
    rep = REPLICATED + ("final_norm_w",)
    rep_shapes = [wts[n].shape for n in rep]
    d2, m2, v2 = adamw(pack_small([wts[n] for n in rep]), pack_small([gsum[n] for n in rep]),
                       pack_small([mom[n] for n in rep]), pack_small([var[n] for n in rep]), "adamw_small")
    for n, dv, mv, vv in zip(rep, unpack_small(d2, rep_shapes), unpack_small(m2, rep_shapes),
                             unpack_small(v2, rep_shapes)):
        delta[n], new_m[n], new_v[n] = dv, mv, vv

    return (loss, grad_x, *[gsum[n] for n in WEIGHTS], *[delta[n] for n in WEIGHTS],
            *[new_m[n] for n in WEIGHTS], *[new_v[n] for n in WEIGHTS])
```

```python
import functools

import jax
import jax.numpy as jnp
from jax import lax
from jax.experimental import pallas as pl
from jax.experimental.pallas import tpu as pltpu

F32 = jnp.float32
BF16 = jnp.bfloat16

N_DEV = 8
DEPTH = 4
D = 1024
SSM_W = 2048
N_HEADS = 32
N_PAIRS = 16
N_GROUPS = 2
N_STATE = 128
CHUNK = 128
CONV_CH = 2560
CONV_K = 4
POOL_W = 1024
POOL_G = 4
POOL_GD = 256
SB_W = 1024
SB_PAIRS = 8
QB = 256
EPS = 1e-6
IN_COLS = 13856

PC = 14336
OFF_MERGE = 0
OFF_SBG = 3072
OFF_PU = 4096
OFF_PG = 5120
OFF_Z = 6144
OFF_QKV = 8192
OFF_XBC = 11264
OFF_DT = 13824

ADAM_LR = 0.001
ADAM_B1 = 0.9
ADAM_B2 = 0.999
ADAM_EPS = 1e-08
ADAM_WD = 0.01
ADAM_STEP = 10

VMEM_LIMIT = 56 * 1024 * 1024

_NN = (((1,), (0,)), ((), ()))
_NT = (((1,), (1,)), ((), ()))
_TN = (((0,), (0,)), ((), ()))


def _dot(a, b, dn=_NN):
    return lax.dot_general(a, b, dn, preferred_element_type=F32)


def _sigmoid(x):
    return 1.0 / (1.0 + jnp.exp(-x))


def _softplus(x):
    return jnp.maximum(x, 0.0) + jnp.log(1.0 + jnp.exp(-jnp.abs(x)))


def _split2(x):
    hi = x.astype(BF16)
    lo = (x - hi.astype(F32)).astype(BF16)
    return hi, lo


def _split3(x):
    hi = x.astype(BF16)
    r = x - hi.astype(F32)
    mid = r.astype(BF16)
    lo = (r - mid.astype(F32)).astype(BF16)
    return hi, mid, lo


def _params(*sem):
    return pltpu.CompilerParams(dimension_semantics=sem, vmem_limit_bytes=VMEM_LIMIT)


def matmul(a, b, mode, out_dtype, name, residual=None, tm=1024, tn=1024, tk=1024, comm=None):
    if mode == "nn":
        (m, k), (k2, n) = a.shape, b.shape
    elif mode == "nt":
        (m, k), (n, k2) = a.shape, b.shape
    else:
        (k, m), (k2, n) = a.shape, b.shape
    assert k == k2
    tm, tn, tk = min(tm, m), min(tn, n), min(tk, k)
    assert m % tm == 0 and n % tn == 0 and k % tk == 0
    nk = k // tk
    dn = {"nn": _NN, "nt": _NT, "tn": _TN}[mode]
    a_spec = pl.BlockSpec((tk, tm), lambda i, j, kk: (kk, i)) if mode == "tn" else pl.BlockSpec((tm, tk), lambda i, j, kk: (i, kk))
    b_spec = pl.BlockSpec((tn, tk), lambda i, j, kk: (j, kk)) if mode == "nt" else pl.BlockSpec((tk, tn), lambda i, j, kk: (kk, j))
    in_specs = [a_spec, b_spec]
    args = [a, b]
    if residual is not None:
        in_specs.append(pl.BlockSpec((tm, tn), lambda i, j, kk: (i, j)))
        args.append(residual)
    grid = (m // tm, n // tn, nk)
    n_in = len(args)
    x_in, x_args, x_out, x_shape, x_scratch, x_start, x_wait = _hosted_exchange(comm, grid, relay_at=None)

    def body(*refs):
        n_xi, n_xo = len(x_in), len(x_out)
        a_ref, b_ref = refs[:2]
        r_ref = refs[2] if residual is not None else None
        o_ref = refs[n_in + n_xi]
        acc_ref = refs[n_in + n_xi + n_xo + 1]
        x_refs = refs[n_in:n_in + n_xi] + refs[n_in + n_xi + 1:n_in + n_xi + n_xo + 1] + refs[n_in + n_xi + n_xo + 2:]
        x_start(x_refs)
        kk = pl.program_id(2)
        p = _dot(a_ref[...], b_ref[...], dn)

        def finish(val):
            if r_ref is not None:
                val = val + r_ref[...]
            o_ref[...] = val.astype(out_dtype)

        if nk == 1:
            finish(p)
        else:
            @pl.when(kk == 0)
            def _():
                acc_ref[...] = p

            @pl.when(kk > 0)
            def _():
                acc_ref[...] += p

            @pl.when(kk == nk - 1)
            def _():
                finish(acc_ref[...])

        x_wait(x_refs)

    out = pl.pallas_call(
        body, name=name,
        grid=grid,
        in_specs=in_specs + x_in,
        out_specs=[pl.BlockSpec((tm, tn), lambda i, j, kk: (i, j))] + x_out,
        out_shape=[jax.ShapeDtypeStruct((m, n), out_dtype)] + x_shape,
        scratch_shapes=[pltpu.VMEM((tm, tn) if nk > 1 else (8, 128), F32)] + x_scratch,
        compiler_params=_params(*(("arbitrary",) * 3 if comm is not None else ("parallel", "parallel", "arbitrary"))),
    )(*args, *x_args)
    return tuple(out) if comm is not None else out[0]


def rmsnorm_fwd(x, w, name):
    t, d = x.shape
    tr = min(512, t)

    def body(x_ref, w_ref, h_ref):
        xv = x_ref[...]
        r = lax.rsqrt(jnp.mean(xv * xv, axis=-1, keepdims=True) + EPS)
        h_ref[...] = (xv * r * w_ref[...]).astype(BF16)

    return pl.pallas_call(
        body, name=name, grid=(t // tr,),
        in_specs=[pl.BlockSpec((tr, d), lambda i: (i, 0)), pl.BlockSpec((1, d), lambda i: (0, 0))],
        out_specs=pl.BlockSpec((tr, d), lambda i: (i, 0)),
        out_shape=jax.ShapeDtypeStruct((t, d), BF16),
        compiler_params=_params("parallel"),
    )(x, w.reshape(1, d))


def rmsnorm_bwd(dh, x, w, dres, name):
    t, d = x.shape
    tr = min(512, t)

    def body(dh_ref, x_ref, w_ref, dres_ref, dx_ref, dxb_ref, dw_ref):
        xv = x_ref[...]
        r = lax.rsqrt(jnp.mean(xv * xv, axis=-1, keepdims=True) + EPS)
        xh = xv * r
        g = dh_ref[...].astype(F32)
        dxh = g * w_ref[...]
        dxv = dres_ref[...] + r * (dxh - xh * jnp.mean(dxh * xh, axis=-1, keepdims=True))
        dx_ref[...] = dxv
        dxb_ref[...] = dxv.astype(BF16)
        part = jnp.sum(g * xh, axis=0, keepdims=True)

        @pl.when(pl.program_id(0) == 0)
        def _():
            dw_ref[...] = part

        @pl.when(pl.program_id(0) > 0)
        def _():
            dw_ref[...] += part

    return pl.pallas_call(
        body, name=name, grid=(t // tr,),
        in_specs=[pl.BlockSpec((tr, d), lambda i: (i, 0)), pl.BlockSpec((tr, d), lambda i: (i, 0)),
                  pl.BlockSpec((1, d), lambda i: (0, 0)), pl.BlockSpec((tr, d), lambda i: (i, 0))],
        out_specs=[pl.BlockSpec((tr, d), lambda i: (i, 0)), pl.BlockSpec((tr, d), lambda i: (i, 0)),
                   pl.BlockSpec((1, d), lambda i: (0, 0))],
        out_shape=[jax.ShapeDtypeStruct((t, d), F32), jax.ShapeDtypeStruct((t, d), BF16),
                   jax.ShapeDtypeStruct((1, d), F32)],
        compiler_params=_params("arbitrary"),
    )(dh, x, w.reshape(1, d), dres)


def final_loss(x, w, target, name):
    t, d = x.shape
    tr = min(512, t)

    def body(x_ref, w_ref, tg_ref, loss_ref, dx_ref, dxb_ref, dw_ref):
        xv = x_ref[...]
        r = lax.rsqrt(jnp.mean(xv * xv, axis=-1, keepdims=True) + EPS)
        xh = xv * r
        err = xh * w_ref[...] - tg_ref[...]
        lpart = 0.5 * jnp.sum(jnp.mean(err * err, axis=-1, keepdims=True), axis=0, keepdims=True)
        dy = err * (1.0 / d)
        dxh = dy * w_ref[...]
        dxv = r * (dxh - xh * jnp.mean(dxh * xh, axis=-1, keepdims=True))
        dx_ref[...] = dxv
        dxb_ref[...] = dxv.astype(BF16)
        part = jnp.sum(dy * xh, axis=0, keepdims=True)

        @pl.when(pl.program_id(0) == 0)
        def _():
            dw_ref[...] = part
            loss_ref[...] = jnp.broadcast_to(lpart, (1, 128))

        @pl.when(pl.program_id(0) > 0)
        def _():
            dw_ref[...] += part
            loss_ref[...] += jnp.broadcast_to(lpart, (1, 128))

    return pl.pallas_call(
        body, name=name, grid=(t // tr,),
        in_specs=[pl.BlockSpec((tr, d), lambda i: (i, 0)), pl.BlockSpec((1, d), lambda i: (0, 0)),
                  pl.BlockSpec((tr, d), lambda i: (i, 0))],
        out_specs=[pl.BlockSpec((1, 128), lambda i: (0, 0)), pl.BlockSpec((tr, d), lambda i: (i, 0)),
                   pl.BlockSpec((tr, d), lambda i: (i, 0)), pl.BlockSpec((1, d), lambda i: (0, 0))],
        out_shape=[jax.ShapeDtypeStruct((1, 128), F32), jax.ShapeDtypeStruct((t, d), F32),
                   jax.ShapeDtypeStruct((t, d), BF16), jax.ShapeDtypeStruct((1, d), F32)],
        compiler_params=_params("arbitrary"),
    )(x, w.reshape(1, d), target)


CONV_BW = 256


def _shift_down(u, s, row):
    return jnp.where(row >= s, pltpu.roll(u, s, axis=0), 0.0)


def _shift_up(u, s, row, n):
    return jnp.where(row < n - s, pltpu.roll(u, n - s, axis=0), 0.0)


def _conv_pre(u, w, b, row):
    acc = b + w[CONV_K - 1:CONV_K, :] * u
    for k in range(CONV_K - 1):
        acc = acc + w[k:k + 1, :] * _shift_down(u, CONV_K - 1 - k, row)
    return acc


def conv_fwd(proj, conv_w, conv_b, bl, s, name):
    t = bl * s
    nb = CONV_CH // CONV_BW
    off = OFF_XBC // CONV_BW

    def body(u_ref, w_ref, b_ref, o_ref):
        u = u_ref[...].astype(F32)
        row = lax.broadcasted_iota(jnp.int32, u.shape, 0)
        xc = _conv_pre(u, w_ref[...], b_ref[...], row)
        o_ref[...] = (xc * _sigmoid(xc)).astype(BF16)

    return pl.pallas_call(
        body, name=name, grid=(bl, nb),
        in_specs=[pl.BlockSpec((s, CONV_BW), lambda b, j: (b, off + j)),
                  pl.BlockSpec((CONV_K, CONV_BW), lambda b, j: (0, j)),
                  pl.BlockSpec((1, CONV_BW), lambda b, j: (0, j))],
        out_specs=pl.BlockSpec((s, CONV_BW), lambda b, j: (b, j)),
        out_shape=jax.ShapeDtypeStruct((t, CONV_CH), BF16),
        compiler_params=_params("parallel", "parallel"),
    )(proj, conv_w, conv_b.reshape(1, CONV_CH))


def conv_bwd(dxa, proj, conv_w, conv_b, bl, s, name):
    t = bl * s
    nb = CONV_CH // CONV_BW
    off = OFF_XBC // CONV_BW

    def body(d_ref, u_ref, w_ref, b_ref, du_ref, dw_ref, db_ref):
        u = u_ref[...].astype(F32)
        w = w_ref[...]
        row = lax.broadcasted_iota(jnp.int32, u.shape, 0)
        xc = _conv_pre(u, w, b_ref[...], row)
        sg = _sigmoid(xc)
        dxc = d_ref[...].astype(F32) * sg * (1.0 + xc * (1.0 - sg))
        du = w[CONV_K - 1:CONV_K, :] * dxc
        dws = [None] * CONV_K
        dws[CONV_K - 1] = jnp.sum(dxc * u, axis=0, keepdims=True)
        for k in range(CONV_K - 1):
            up = _shift_up(dxc, CONV_K - 1 - k, row, s)
            du = du + w[k:k + 1, :] * up
            dws[k] = jnp.sum(up * u, axis=0, keepdims=True)
        du_ref[...] = du.astype(BF16)
        krow = lax.broadcasted_iota(jnp.int32, (8, CONV_BW), 0)
        dwv = sum(jnp.where(krow == k, dws[k], 0.0) for k in range(CONV_K))
        dbv = jnp.sum(dxc, axis=0, keepdims=True)

        @pl.when(pl.program_id(1) == 0)
        def _():
            dw_ref[...] = dwv
            db_ref[...] = dbv

        @pl.when(pl.program_id(1) > 0)
        def _():
            dw_ref[...] += dwv
            db_ref[...] += dbv

    du, dw, db = pl.pallas_call(
        body, name=name, grid=(nb, bl),
        in_specs=[pl.BlockSpec((s, CONV_BW), lambda j, b: (b, j)),
                  pl.BlockSpec((s, CONV_BW), lambda j, b: (b, off + j)),
                  pl.BlockSpec((CONV_K, CONV_BW), lambda j, b: (0, j)),
                  pl.BlockSpec((1, CONV_BW), lambda j, b: (0, j))],
        out_specs=[pl.BlockSpec((s, CONV_BW), lambda j, b: (b, j)),
                   pl.BlockSpec((8, CONV_BW), lambda j, b: (0, j)),
                   pl.BlockSpec((1, CONV_BW), lambda j, b: (0, j))],
        out_shape=[jax.ShapeDtypeStruct((t, CONV_CH), BF16), jax.ShapeDtypeStruct((8, CONV_CH), F32),
                   jax.ShapeDtypeStruct((1, CONV_CH), F32)],
        compiler_params=_params("parallel", "arbitrary"),
    )(dxa, proj, conv_w, conv_b.reshape(1, CONV_CH))
    return du, dw[:CONV_K], db[0]


def _tri(shape, cmp):
    r = lax.broadcasted_iota(jnp.int32, shape, 0)
    c = lax.broadcasted_iota(jnp.int32, shape, 1)
    return cmp(r, c)


def dt_fwd(proj, dt_bias, a_log, t, name):
    nchunks = t // CHUNK
    bias = jnp.zeros((1, 128), F32).at[0, :N_HEADS].set(dt_bias)
    alog = jnp.zeros((1, 128), F32).at[0, :N_HEADS].set(a_log)

    def body(raw_ref, b_ref, al_ref, dt_ref, ac_ref, dtl_ref, acl_ref):
        raw = raw_ref[...].astype(F32)
        dt = _softplus(raw + b_ref[...])
        adt = dt * (-jnp.exp(al_ref[...]))
        low = _tri((CHUNK, CHUNK), lambda r, c: r >= c).astype(BF16)
        acum = sum(_dot(low, part) for part in _split3(adt))
        dt_ref[...] = dt.T[:N_HEADS]
        ac_ref[...] = acum.T[:N_HEADS]
        spread = _tri((128, SSM_W), lambda h, lane: lane // 64 == h).astype(BF16)
        dtl_ref[...] = sum(_dot(part, spread) for part in _split2(dt))
        acl_ref[...] = sum(_dot(part, spread) for part in _split3(acum))

    return pl.pallas_call(
        body, name=name, grid=(nchunks,),
        in_specs=[pl.BlockSpec((CHUNK, 128), lambda i: (i, OFF_DT // 128)),
                  pl.BlockSpec((1, 128), lambda i: (0, 0)), pl.BlockSpec((1, 128), lambda i: (0, 0))],
        out_specs=[pl.BlockSpec((None, N_HEADS, CHUNK), lambda i: (i, 0, 0))] * 2
        + [pl.BlockSpec((CHUNK, SSM_W), lambda i: (i, 0))] * 2,
        out_shape=[jax.ShapeDtypeStruct((nchunks, N_HEADS, CHUNK), F32)] * 2
        + [jax.ShapeDtypeStruct((t, SSM_W), F32)] * 2,
        compiler_params=_params("parallel"),
    )(proj, bias, alog)


def dt_bwd(ddtT, dacT, dtT, proj, dt_bias, a_log, t, name):
    nchunks = t // CHUNK
    bias = dt_bias.reshape(N_HEADS, 1)
    alog = a_log.reshape(N_HEADS, 1)

    def body(ddt_ref, dac_ref, dt_ref, raw_ref, b_ref, al_ref, draw_ref, da_ref, db_ref):
        a = -jnp.exp(al_ref[...])
        upp = _tri((CHUNK, CHUNK), lambda r, c: r >= c).astype(BF16)
        dadt = sum(_dot(part, upp) for part in _split3(dac_ref[...]))
        ddt = ddt_ref[...] + dadt * a
        rawT = raw_ref[...].astype(F32).T[:N_HEADS]
        draw = ddt * _sigmoid(rawT + b_ref[...])
        padded = jnp.concatenate([draw, jnp.zeros((128 - N_HEADS, CHUNK), F32)], axis=0)
        draw_ref[...] = padded.T.astype(BF16)
        dav = dadt * dt_ref[...]

        @pl.when(pl.program_id(0) == 0)
        def _():
            da_ref[...] = dav
            db_ref[...] = draw

        @pl.when(pl.program_id(0) > 0)
        def _():
            da_ref[...] += dav
            db_ref[...] += draw

    draw, da, db = pl.pallas_call(
        body, name=name, grid=(nchunks,),
        in_specs=[pl.BlockSpec((None, N_HEADS, CHUNK), lambda i: (i, 0, 0))] * 3
        + [pl.BlockSpec((CHUNK, 128), lambda i: (i, OFF_DT // 128)),
           pl.BlockSpec((N_HEADS, 1), lambda i: (0, 0)), pl.BlockSpec((N_HEADS, 1), lambda i: (0, 0))],
        out_specs=[pl.BlockSpec((CHUNK, 128), lambda i: (i, 0)),
                   pl.BlockSpec((N_HEADS, CHUNK), lambda i: (0, 0)), pl.BlockSpec((N_HEADS, CHUNK), lambda i: (0, 0))],
        out_shape=[jax.ShapeDtypeStruct((t, 128), BF16), jax.ShapeDtypeStruct((N_HEADS, CHUNK), F32),
                   jax.ShapeDtypeStruct((N_HEADS, CHUNK), F32)],
        compiler_params=_params("arbitrary"),
    )(ddtT, dacT, dtT, proj, bias, alog)
    return draw, jnp.sum(da, axis=1), jnp.sum(db, axis=1)


PAIRS_G = N_PAIRS // N_GROUPS
GROUP_W = PAIRS_G * 128


def _ssd_pair(x, dtl, acl, acr, tri):
    left = lax.broadcasted_iota(jnp.int32, (CHUNK, 128), 1) < 64
    swapped = pltpu.roll(acl, 64, axis=1)
    ac_cols = [jnp.where(left, acl, swapped), jnp.where(left, swapped, acl)]
    dks = [jnp.exp(jnp.where(tri, ac_cols[e] - acr[e:e + 1], -1e30)) for e in range(2)]
    aclast = acl[CHUNK - 1:CHUNK, :]
    return left, dtl, acl, x * dtl, dks, aclast


def _ssd_specs(nc, rev):
    row = (lambda b, c, g: b * nc + (nc - 1 - c)) if rev else (lambda b, c, g: b * nc + c)
    return dict(
        wide=pl.BlockSpec((CHUNK, GROUP_W), lambda b, c, g: (row(b, c, g), g)),
        bmat=pl.BlockSpec((CHUNK, 128), lambda b, c, g: (row(b, c, g), SSM_W // 128 + g)),
        cmat=pl.BlockSpec((CHUNK, 128), lambda b, c, g: (row(b, c, g), SSM_W // 128 + N_GROUPS + g)),
        rows2=pl.BlockSpec((None, PAIRS_G, 2, CHUNK), lambda b, c, g: (row(b, c, g), g, 0, 0)),
        cols=pl.BlockSpec((CHUNK, GROUP_W), lambda b, c, g: (row(b, c, g), g)),
        rows8=pl.BlockSpec((None, PAIRS_G, 8, CHUNK), lambda b, c, g: (row(b, c, g), g, 0, 0)),
        dskip=pl.BlockSpec((1, GROUP_W), lambda b, c, g: (0, g)),
        state=pl.BlockSpec((None, PAIRS_G, N_STATE, 128), lambda b, c, g: (row(b, c, g), g, 0, 0)),
        narrow=pl.BlockSpec((CHUNK, 128), lambda b, c, g: (row(b, c, g), g)))


def ssd_fwd(xa, acT, dtC, acC, dskip_l, bl, s, name):
    t = bl * s
    nc = s // CHUNK
    ac4 = acT.reshape(bl * nc, N_PAIRS, 2, CHUNK)

    def body(x_ref, b_ref, c_ref, ac_ref, dtc_ref, acc_ref, dsk_ref, y_ref, prev_ref, st_ref):
        c = pl.program_id(1)
        g = pl.program_id(2)
        bm = b_ref[...]
        cm = c_ref[...]
        cb = _dot(cm, bm, _NT)
        tri = _tri((CHUNK, CHUNK), lambda r, c: r >= c)

        @pl.when(c == 0)
        def _():
            for p in range(PAIRS_G):
                st_ref[g * PAIRS_G + p] = jnp.zeros((N_STATE, 128), F32)

        for p in range(PAIRS_G):
            hp = g * PAIRS_G + p
            cs = slice(p * 128, (p + 1) * 128)
            x = x_ref[:, cs].astype(F32)
            left, dtl, acl, xdt, dks, aclast = _ssd_pair(x, dtc_ref[:, cs], acc_ref[:, cs], ac_ref[p], tri)
            xdt_b = xdt.astype(BF16)
            ys = [_dot((cb * dks[e]).astype(BF16), xdt_b) for e in range(2)]
            st = st_ref[hp]
            y_off = _dot(cm, st.astype(BF16)) * jnp.exp(acl)
            y_ref[:, cs] = (jnp.where(left, ys[0], ys[1]) + y_off + x * dsk_ref[:, cs]).astype(BF16)
            xw = (xdt * jnp.exp(aclast - acl)).astype(BF16)
            prev_ref[p] = st
            st_ref[hp] = st * jnp.exp(aclast) + _dot(bm, xw, _TN)

    sp = _ssd_specs(nc, False)
    return pl.pallas_call(
        body, name=name, grid=(bl, nc, N_GROUPS),
        in_specs=[sp["wide"], sp["bmat"], sp["cmat"], sp["rows2"], sp["cols"], sp["cols"], sp["dskip"]],
        out_specs=[sp["wide"], sp["state"]],
        out_shape=[jax.ShapeDtypeStruct((t, SSM_W), BF16),
                   jax.ShapeDtypeStruct((bl * nc, N_PAIRS, N_STATE, 128), F32)],
        scratch_shapes=[pltpu.VMEM((N_PAIRS, N_STATE, 128), F32)],
        compiler_params=_params("parallel", "arbitrary", "arbitrary"),
    )(xa, xa, xa, ac4, dtC, acC, dskip_l)


def ssd_bwd(dy, xa, acT, dtC, acC, dskip_l, prev, bl, s, name):
    t = bl * s
    nc = s // CHUNK
    ac4 = acT.reshape(bl * nc, N_PAIRS, 2, CHUNK)

    def body(dy_ref, x_ref, b_ref, c_ref, ac_ref, dtc_ref, acc_ref, dsk_ref, prev_ref,
             dx_ref, db_ref, dc_ref, dd_ref, dsk_out_ref, dp_ref):
        b = pl.program_id(0)
        cr = pl.program_id(1)
        g = pl.program_id(2)

        @pl.when(cr == 0)
        def _():
            for p in range(PAIRS_G):
                dp_ref[g * PAIRS_G + p] = jnp.zeros((N_STATE, 128), F32)

        @pl.when((b == 0) & (cr == 0) & (g == 0))
        def _():
            dsk_out_ref[...] = jnp.zeros(dsk_out_ref.shape, F32)

        bm = b_ref[...]
        cm = c_ref[...]
        cb = _dot(cm, bm, _NT)
        tri = _tri((CHUNK, CHUNK), lambda r, c: r >= c)
        lane = lax.broadcasted_iota(jnp.int32, (CHUNK, 128), 1)
        lrow = lax.broadcasted_iota(jnp.int32, (1, CHUNK), 1)
        krow = lax.broadcasted_iota(jnp.int32, (8, CHUNK), 0)
        dcb = jnp.zeros((CHUNK, CHUNK), F32)
        dc_acc = jnp.zeros((CHUNK, N_STATE), F32)
        db_acc = jnp.zeros((CHUNK, N_STATE), F32)
        for p in range(PAIRS_G):
            hp = g * PAIRS_G + p
            cs = slice(p * 128, (p + 1) * 128)
            x = x_ref[:, cs].astype(F32)
            left, dtl, acl, xdt, dks, aclast = _ssd_pair(x, dtc_ref[:, cs], acc_ref[:, cs], ac_ref[p], tri)
            dyv = dy_ref[:, cs].astype(F32)
            dy_b = dyv.astype(BF16)
            xdt_b = xdt.astype(BF16)
            st = prev_ref[p]
            st_b = st.astype(BF16)
            ea = jnp.exp(acl)
            ds = jnp.exp(aclast - acl)
            cdl = jnp.exp(aclast)
            xw = xdt * ds
            masks = [left, jnp.logical_not(left)]

            dsk_out_ref[hp] = dsk_out_ref[hp] + jnp.sum(dyv * x, axis=0, keepdims=True)

            yo = _dot(cm, st_b)
            dyo_b = (dyv * ea).astype(BF16)
            yoff_term = dyv * yo * ea
            dc_acc = dc_acc + _dot(dyo_b, st_b, _NT)
            dst = _dot(cm, dyo_b, _TN)
            dsv = dp_ref[hp]
            dsv_b = dsv.astype(BF16)
            dxw = _dot(bm, dsv_b)
            db_acc = db_acc + _dot(xw.astype(BF16), dsv_b, _NT)
            dxdt = dxw * ds
            qv = dxw * xw
            end_term = dsv * st * cdl
            dp_ref[hp] = dsv * cdl + dst

            cols = jnp.zeros((CHUNK, 128), F32)
            rows = []
            for e in range(2):
                m = cb * dks[e]
                dy_e = jnp.where(masks[e], dyv, 0.0).astype(BF16)
                dm = _dot(dy_e, xdt_b, _NT)
                w = dm * m
                dcb = dcb + dm * dks[e]
                dxdt = dxdt + jnp.where(masks[e], _dot(m.astype(BF16), dy_b, _TN), 0.0)
                dac_col = jnp.sum(w + jnp.where(masks[e], yoff_term - qv, 0.0), axis=1, keepdims=True)
                cols = jnp.where(lane == 2 + e, dac_col, cols)
                tail = jnp.sum(jnp.where(masks[e], qv + end_term, 0.0))
                rows.append(jnp.where(lrow == CHUNK - 1, tail, 0.0) - jnp.sum(w, axis=0, keepdims=True))
            dx_ref[:, cs] = (dxdt * dtl + dyv * dsk_ref[:, cs]).astype(BF16)
            ddt_l = dxdt * x
            for e in range(2):
                cols = jnp.where(lane == e, jnp.sum(jnp.where(masks[e], ddt_l, 0.0), axis=1, keepdims=True), cols)
            dd_ref[p] = cols.T[0:8] + jnp.where(krow == 2, rows[0], 0.0) + jnp.where(krow == 3, rows[1], 0.0)
        dcb_b = dcb.astype(BF16)
        dc_ref[...] = dc_acc + _dot(dcb_b, bm)
        db_ref[...] = db_acc + _dot(dcb_b, cm, _TN)

    sp = _ssd_specs(nc, True)
    dx, db, dc, dd, dsk = pl.pallas_call(
        body, name=name, grid=(bl, nc, N_GROUPS),
        in_specs=[sp["wide"], sp["wide"], sp["bmat"], sp["cmat"], sp["rows2"], sp["cols"], sp["cols"], sp["dskip"],
                  sp["state"]],
        out_specs=[sp["wide"], sp["narrow"], sp["narrow"], sp["rows8"],
                   pl.BlockSpec((N_PAIRS, 1, 128), lambda b, c, g: (0, 0, 0))],
        out_shape=[jax.ShapeDtypeStruct((t, SSM_W), BF16),
                   jax.ShapeDtypeStruct((t, N_GROUPS * N_STATE), F32),
                   jax.ShapeDtypeStruct((t, N_GROUPS * N_STATE), F32),
                   jax.ShapeDtypeStruct((bl * nc, N_PAIRS, 8, CHUNK), F32),
                   jax.ShapeDtypeStruct((N_PAIRS, 1, 128), F32)],
        scratch_shapes=[pltpu.VMEM((N_PAIRS, N_STATE, 128), F32)],
        compiler_params=_params("arbitrary", "arbitrary", "arbitrary"),
    )(dy, xa, xa, xa, ac4, dtC, acC, dskip_l, prev)
    ddtT = dd[:, :, 0:2, :].reshape(bl * nc, N_HEADS, CHUNK)
    dacT = dd[:, :, 2:4, :].reshape(bl * nc, N_HEADS, CHUNK)
    return dx, db, dc, ddtT, dacT, dsk.reshape(N_PAIRS, 128)


def gnorm_fwd(y, proj, w, name):
    t = y.shape[0]
    tr = min(256, t)
    zb = OFF_Z // SSM_W

    def body(y_ref, z_ref, w_ref, o_ref):
        z = z_ref[...].astype(F32)
        yg = y_ref[...].astype(F32) * z * _sigmoid(z)
        r = lax.rsqrt(jnp.mean(yg * yg, axis=-1, keepdims=True) + EPS)
        o_ref[...] = (yg * r * w_ref[...]).astype(BF16)

    return pl.pallas_call(
        body, name=name, grid=(t // tr,),
        in_specs=[pl.BlockSpec((tr, SSM_W), lambda i: (i, 0)), pl.BlockSpec((tr, SSM_W), lambda i: (i, zb)),
                  pl.BlockSpec((1, SSM_W), lambda i: (0, 0))],
        out_specs=pl.BlockSpec((tr, SSM_W), lambda i: (i, 0)),
        out_shape=jax.ShapeDtypeStruct((t, SSM_W), BF16),
        compiler_params=_params("parallel"),
    )(y, proj, w.reshape(1, SSM_W))


def gnorm_bwd(ds, y, proj, w, name):
    t = y.shape[0]
    tr = min(256, t)
    zb = OFF_Z // SSM_W

    def body(ds_ref, y_ref, z_ref, w_ref, dy_ref, dz_ref, dw_ref):
        z = z_ref[...].astype(F32)
        yv = y_ref[...].astype(F32)
        sg = _sigmoid(z)
        sz = z * sg
        yg = yv * sz
        r = lax.rsqrt(jnp.mean(yg * yg, axis=-1, keepdims=True) + EPS)
        xh = yg * r
        g = ds_ref[...].astype(F32)
        dxh = g * w_ref[...]
        dyg = r * (dxh - xh * jnp.mean(dxh * xh, axis=-1, keepdims=True))
        dy_ref[...] = (dyg * sz).astype(BF16)
        dz_ref[...] = (dyg * yv * sg * (1.0 + z * (1.0 - sg))).astype(BF16)
        part = jnp.sum(g * xh, axis=0, keepdims=True)

        @pl.when(pl.program_id(0) == 0)
        def _():
            dw_ref[...] = part

        @pl.when(pl.program_id(0) > 0)
        def _():
            dw_ref[...] += part

    return pl.pallas_call(
        body, name=name, grid=(t // tr,),
        in_specs=[pl.BlockSpec((tr, SSM_W), lambda i: (i, 0)), pl.BlockSpec((tr, SSM_W), lambda i: (i, 0)),
                  pl.BlockSpec((tr, SSM_W), lambda i: (i, zb)), pl.BlockSpec((1, SSM_W), lambda i: (0, 0))],
        out_specs=[pl.BlockSpec((tr, SSM_W), lambda i: (i, 0)), pl.BlockSpec((tr, SSM_W), lambda i: (i, 0)),
                   pl.BlockSpec((1, SSM_W), lambda i: (0, 0))],
        out_shape=[jax.ShapeDtypeStruct((t, SSM_W), BF16), jax.ShapeDtypeStruct((t, SSM_W), BF16),
                   jax.ShapeDtypeStruct((1, SSM_W), F32)],
        compiler_params=_params("arbitrary"),
    )(ds, y, proj, w.reshape(1, SSM_W))


def _pool_mixed(u, g, row):
    win = 2 << g
    acc = u
    for k in range(g + 1):
        acc = acc + _shift_down(acc, 1 << k, row)
    inv = 1.0 / jnp.minimum(row + 1, win).astype(F32)
    return acc * inv - u, inv


def pool_fwd(proj, pool_w, pool_scale, bl, s, name):
    t = bl * s

    def body(u_ref, g_ref, w_ref, sc_ref, o_ref):
        row = lax.broadcasted_iota(jnp.int32, (s, POOL_GD), 0)
        for g in range(POOL_G):
            cs = slice(g * POOL_GD, (g + 1) * POOL_GD)
            u = u_ref[:, cs].astype(F32)
            mixed, _ = _pool_mixed(u, g, row)
            pm = _dot(mixed.astype(BF16), w_ref[g])
            gate = g_ref[:, cs].astype(F32)
            o_ref[:, cs] = (pm * sc_ref[:, cs] * gate * _sigmoid(gate)).astype(BF16)

    return pl.pallas_call(
        body, name=name, grid=(bl,),
        in_specs=[pl.BlockSpec((s, POOL_W), lambda b: (b, OFF_PU // POOL_W)),
                  pl.BlockSpec((s, POOL_W), lambda b: (b, OFF_PG // POOL_W)),
                  pl.BlockSpec((POOL_G, POOL_GD, POOL_GD), lambda b: (0, 0, 0)),
                  pl.BlockSpec((1, POOL_W), lambda b: (0, 0))],
        out_specs=pl.BlockSpec((s, POOL_W), lambda b: (b, 0)),
        out_shape=jax.ShapeDtypeStruct((t, POOL_W), BF16),
        compiler_params=_params("parallel"),
    )(proj, proj, pool_w, pool_scale.reshape(1, POOL_W))


def pool_bwd(dp, proj, pool_w, pool_scale, bl, s, name):
    t = bl * s

    def body(dp_ref, u_ref, g_ref, w_ref, sc_ref, du_ref, dg_ref, dw_ref, dsc_ref):
        row = lax.broadcasted_iota(jnp.int32, (s, POOL_GD), 0)
        first = pl.program_id(0) == 0
        for g in range(POOL_G):
            cs = slice(g * POOL_GD, (g + 1) * POOL_GD)
            u = u_ref[:, cs].astype(F32)
            mixed, inv = _pool_mixed(u, g, row)
            mixed_b = mixed.astype(BF16)
            wg = w_ref[g]
            pm = _dot(mixed_b, wg)
            gate = g_ref[:, cs].astype(F32)
            sg = _sigmoid(gate)
            d = dp_ref[:, cs].astype(F32)
            sc = sc_ref[:, cs]
            dpm = (d * sc * gate * sg).astype(BF16)
            dg_ref[:, cs] = (d * pm * sc * sg * (1.0 + gate * (1.0 - sg))).astype(BF16)
            dsc = jnp.sum(d * pm * gate * sg, axis=0, keepdims=True)
            dwg = _dot(mixed_b, dpm, _TN)
            dmixed = _dot(dpm, wg, _NT)
            acc = dmixed * inv
            for k in range(g + 1):
                acc = acc + _shift_up(acc, 1 << k, row, s)
            du_ref[:, cs] = (acc - dmixed).astype(BF16)

            @pl.when(first)
            def _():
                dw_ref[g] = dwg
                dsc_ref[:, cs] = dsc

            @pl.when(jnp.logical_not(first))
            def _():
                dw_ref[g] = dw_ref[g] + dwg
                dsc_ref[:, cs] = dsc_ref[:, cs] + dsc

    return pl.pallas_call(
        body, name=name, grid=(bl,),
        in_specs=[pl.BlockSpec((s, POOL_W), lambda b: (b, 0)),
                  pl.BlockSpec((s, POOL_W), lambda b: (b, OFF_PU // POOL_W)),
                  pl.BlockSpec((s, POOL_W), lambda b: (b, OFF_PG // POOL_W)),
                  pl.BlockSpec((POOL_G, POOL_GD, POOL_GD), lambda b: (0, 0, 0)),
                  pl.BlockSpec((1, POOL_W), lambda b: (0, 0))],
        out_specs=[pl.BlockSpec((s, POOL_W), lambda b: (b, 0)), pl.BlockSpec((s, POOL_W), lambda b: (b, 0)),
                   pl.BlockSpec((POOL_G, POOL_GD, POOL_GD), lambda b: (0, 0, 0)),
                   pl.BlockSpec((1, POOL_W), lambda b: (0, 0))],
        out_shape=[jax.ShapeDtypeStruct((t, POOL_W), BF16), jax.ShapeDtypeStruct((t, POOL_W), BF16),
                   jax.ShapeDtypeStruct((POOL_G, POOL_GD, POOL_GD), F32), jax.ShapeDtypeStruct((1, POOL_W), F32)],
        compiler_params=_params("arbitrary"),
    )(dp, proj, proj, pool_w, pool_scale.reshape(1, POOL_W))


SB_SCALE = 64 ** -0.5


KB = 256


def _sb_block(qe, kj, mask, rr, upper):
    z = _dot(qe, kj, _NT).astype(BF16)
    lb = jnp.minimum(z, 0.0) - jnp.log(1.0 + jnp.exp(-jnp.abs(z)))
    lom = lb - z if mask is None else jnp.where(mask, lb - z, 0.0)
    later = _dot(lom, upper) + rr
    return lb, lom, later


def _sb_masks(i):
    lane = lax.broadcasted_iota(jnp.int32, (QB, 128), 1)
    row = lax.broadcasted_iota(jnp.int32, (2 * QB, KB), 0) % QB
    col = lax.broadcasted_iota(jnp.int32, (2 * QB, KB), 1)
    causal = lambda jb: col + (jb * KB - i * QB) < row
    return lane, lane < 64, causal


def _stack_heads(x, left):
    zero = jnp.zeros_like(x)
    return jnp.concatenate([jnp.where(left, x, zero), jnp.where(left, zero, x)], axis=0)


SB_GROUP = 4
SB_GW = SB_GROUP * 128


def sb_fwd(proj, bl, s, name, comm=None):
    t = bl * s
    nq = s // QB
    qb0, kb0, vb0, gb0 = OFF_QKV // SB_GW, (OFF_QKV + SB_W) // SB_GW, (OFF_QKV + 2 * SB_W) // SB_GW, OFF_SBG // SB_GW
    grid = (bl, SB_PAIRS // SB_GROUP, nq)
    x_in, x_args, x_out, x_shape, x_scratch, x_start, x_wait = _hosted_exchange(
        comm, grid, relay_at=(bl - 1, SB_PAIRS // SB_GROUP - 1, 0))

    def body(*refs):
        q_ref, k_ref, v_ref, g_ref = refs[:4]
        og_ref, o_ref, r_ref = refs[4 + len(x_in):7 + len(x_in)]
        x_refs = refs[4:4 + len(x_in)] + refs[7 + len(x_in):]
        x_start(x_refs)
        i = pl.program_id(2)
        lane, left, causal = _sb_masks(i)
        upper = _tri((KB, KB), lambda r, c: r > c).astype(BF16)
        cols = [slice(p * 128, (p + 1) * 128) for p in range(SB_GROUP)]
        qcats = [_stack_heads(q_ref[:, cs] * SB_SCALE, left) for cs in cols]
        zero = qcats[0].astype(F32) * 0.0

        def block(jb, carry, diagonal):
            rows = pl.ds(pl.multiple_of(jb * KB, KB), KB)
            mask = causal(jb) if diagonal else None
            out = []
            for p, cs in enumerate(cols):
                acc, rr, rt = carry[p]
                lb, lom, later = _sb_block(qcats[p], k_ref[rows, cs], mask, rr, upper)
                att = jnp.exp(lb + later)
                if diagonal:
                    att = jnp.where(mask, att, 0.0)
                acc = acc + _dot(att.astype(BF16), v_ref[rows, cs])
                rt = jnp.where(lane == jb, rr[:QB], jnp.where(lane == 8 + jb, rr[QB:], rt))
                out.append((acc, rr + jnp.sum(lom, axis=1, keepdims=True, dtype=F32), rt))
            return tuple(out)

        carry = block(i, tuple((zero, zero[:, :1], zero[:QB]) for _ in cols), True)
        carry = lax.fori_loop(0, i, lambda jj, c: block(i - 1 - jj, c, False), carry)
        for p, cs in enumerate(cols):
            acc, _, rtile = carry[p]
            o = jnp.where(left, acc[:QB], acc[QB:])
            gate = g_ref[:, cs].astype(F32)
            o_ref[:, cs] = o.astype(BF16)
            og_ref[:, cs] = (o * gate * _sigmoid(gate)).astype(BF16)
            r_ref[p] = rtile
        x_wait(x_refs)

    rowblk = lambda b, g, i: (b * nq + i, g)
    return pl.pallas_call(
        body, name=name, grid=grid,
        in_specs=[pl.BlockSpec((QB, SB_GW), lambda b, g, i: (b * nq + i, qb0 + g)),
                  pl.BlockSpec((s, SB_GW), lambda b, g, i: (b, kb0 + g)),
                  pl.BlockSpec((s, SB_GW), lambda b, g, i: (b, vb0 + g)),
                  pl.BlockSpec((QB, SB_GW), lambda b, g, i: (b * nq + i, gb0 + g))] + x_in,
        out_specs=[pl.BlockSpec((QB, SB_GW), rowblk), pl.BlockSpec((QB, SB_GW), rowblk),
                   pl.BlockSpec((None, SB_GROUP, QB, 128), lambda b, g, i: (b * nq + i, g, 0, 0))] + x_out,
        out_shape=[jax.ShapeDtypeStruct((t, SB_W), BF16), jax.ShapeDtypeStruct((t, SB_W), BF16),
                   jax.ShapeDtypeStruct((bl * nq, SB_PAIRS, QB, 128), F32)] + x_shape,
        scratch_shapes=x_scratch,
        compiler_params=_params("arbitrary", "arbitrary", "arbitrary"),
    )(proj, proj, proj, proj, *x_args)


def sb_bwd(dsb, o, rsave, proj, bl, s, name, comm=None):
    t = bl * s
    nq = s // QB
    qb0, kb0, vb0, gb0 = OFF_QKV // SB_GW, (OFF_QKV + SB_W) // SB_GW, (OFF_QKV + 2 * SB_W) // SB_GW, OFF_SBG // SB_GW
    grid = (bl, SB_PAIRS // SB_GROUP, nq)
    x_in, x_args, x_out, x_shape, x_scratch, x_start, x_wait = _hosted_exchange(
        comm, grid, relay_at=(bl - 1, SB_PAIRS // SB_GROUP - 1, 0))

    def body(*refs):
        n = len(x_in)
        d_ref, o_ref, r_ref, q_ref, k_ref, v_ref, g_ref = refs[:7]
        dq_ref, dk_ref, dv_ref, dg_ref = refs[7 + n:11 + n]
        dk_acc, dv_acc = refs[11 + 2 * n:13 + 2 * n]
        x_refs = refs[7:7 + n] + refs[11 + n:11 + 2 * n] + refs[13 + 2 * n:]
        x_start(x_refs)
        i = pl.program_id(2)

        @pl.when(i == 0)
        def _():
            dk_acc[...] = jnp.zeros(dk_acc.shape, F32)
            dv_acc[...] = jnp.zeros(dv_acc.shape, F32)

        lane, left, causal = _sb_masks(i)
        upper = _tri((KB, KB), lambda r, c: r > c).astype(BF16)
        lower = _tri((KB, KB), lambda r, c: r < c).astype(BF16)
        cols = [slice(p * 128, (p + 1) * 128) for p in range(SB_GROUP)]
        qcats, docats = [], []
        for cs in cols:
            gate = g_ref[:, cs].astype(F32)
            sg = _sigmoid(gate)
            d = d_ref[:, cs].astype(F32)
            dg_ref[:, cs] = (d * o_ref[:, cs].astype(F32) * sg * (1.0 + gate * (1.0 - sg))).astype(BF16)
            docats.append(_stack_heads((d * gate * sg).astype(BF16), left))
            qcats.append(_stack_heads(q_ref[:, cs] * SB_SCALE, left))
        qcat_ts = [qc.astype(F32).T.astype(BF16) for qc in qcats]
        docat_ts = [dc.astype(F32).T.astype(BF16) for dc in docats]
        zero = qcats[0].astype(F32) * 0.0

        def block(jb, carry, diagonal):
            rows = pl.ds(pl.multiple_of(jb * KB, KB), KB)
            mask = causal(jb) if diagonal else None
            out = []
            for p, cs in enumerate(cols):
                dq, gcar = carry[p]
                kj = k_ref[rows, cs]
                vj = v_ref[rows, cs]
                rtile = r_ref[p]
                rr = jnp.concatenate(
                    [jnp.sum(jnp.where(lane == jb, rtile, 0.0), axis=1, keepdims=True),
                     jnp.sum(jnp.where(lane == 8 + jb, rtile, 0.0), axis=1, keepdims=True)], axis=0)
                lb, lom, later = _sb_block(qcats[p], kj, mask, rr, upper)
                att = jnp.exp(lb + later)
                if diagonal:
                    att = jnp.where(mask, att, 0.0)
                de = att * _dot(docats[p], vj, _NT)
                gpre = _dot(de.astype(BF16), lower) + gcar
                sig = jnp.exp(lb)
                dz = de * (1.0 - sig) - gpre * sig
                if diagonal:
                    dz = jnp.where(mask, dz, 0.0)
                dz = dz.astype(BF16)
                dk_acc[jb, cs, :] = dk_acc[jb, cs, :] + _dot(qcat_ts[p], dz)
                dv_acc[jb, cs, :] = dv_acc[jb, cs, :] + _dot(docat_ts[p], att.astype(BF16))
                out.append((dq + _dot(dz, kj), gcar + jnp.sum(de, axis=1, keepdims=True)))
            return tuple(out)

        carry = lax.fori_loop(0, i, lambda jb, c: block(jb, c, False), tuple((zero, zero[:, :1]) for _ in cols))
        carry = block(i, carry, True)
        for p, cs in enumerate(cols):
            dq = carry[p][0]
            dq_ref[:, cs] = (jnp.where(left, dq[:QB], dq[QB:]) * SB_SCALE).astype(BF16)

        @pl.when(i == nq - 1)
        def _():
            for kb in range(s // KB):
                for cs in cols:
                    dk_ref[kb * KB:(kb + 1) * KB, cs] = dk_acc[kb, cs, :].T.astype(BF16)
                    dv_ref[kb * KB:(kb + 1) * KB, cs] = dv_acc[kb, cs, :].T.astype(BF16)

        x_wait(x_refs)

    rowblk = lambda b, g, i: (b * nq + i, g)
    seqblk = lambda b, g, i: (b, g)
    return pl.pallas_call(
        body, name=name, grid=grid,
        in_specs=[pl.BlockSpec((QB, SB_GW), rowblk), pl.BlockSpec((QB, SB_GW), rowblk),
                  pl.BlockSpec((None, SB_GROUP, QB, 128), lambda b, g, i: (b * nq + i, g, 0, 0)),
                  pl.BlockSpec((QB, SB_GW), lambda b, g, i: (b * nq + i, qb0 + g)),
                  pl.BlockSpec((s, SB_GW), lambda b, g, i: (b, kb0 + g)),
                  pl.BlockSpec((s, SB_GW), lambda b, g, i: (b, vb0 + g)),
                  pl.BlockSpec((QB, SB_GW), lambda b, g, i: (b * nq + i, gb0 + g))] + x_in,
        out_specs=[pl.BlockSpec((QB, SB_GW), rowblk), pl.BlockSpec((s, SB_GW), seqblk),
                   pl.BlockSpec((s, SB_GW), seqblk), pl.BlockSpec((QB, SB_GW), rowblk)] + x_out,
        out_shape=[jax.ShapeDtypeStruct((t, SB_W), BF16)] * 4 + x_shape,
        scratch_shapes=[pltpu.VMEM((s // KB, SB_GW, KB), F32), pltpu.VMEM((s // KB, SB_GW, KB), F32)] + x_scratch,
        compiler_params=_params("arbitrary", "arbitrary", "arbitrary"),
    )(dsb, o, rsave, proj, proj, proj, proj, *x_args)


def merge_fwd(proj, ys, yp, yb, name):
    t = ys.shape[0]
    tr = min(512, t)

    def body(m_ref, ys_ref, yp_ref, yb_ref, o_ref):
        acc = jnp.zeros((tr, D), F32)
        for k, ref in enumerate((ys_ref, yp_ref, yb_ref)):
            acc = acc + _sigmoid(m_ref[:, k * D:(k + 1) * D].astype(F32)) * ref[...].astype(F32)
        o_ref[...] = acc.astype(BF16)

    rowblk = pl.BlockSpec((tr, D), lambda i: (i, 0))
    return pl.pallas_call(
        body, name=name, grid=(t // tr,),
        in_specs=[pl.BlockSpec((tr, 3 * D), lambda i: (i, 0)), rowblk, rowblk, rowblk],
        out_specs=rowblk,
        out_shape=jax.ShapeDtypeStruct((t, D), BF16),
        compiler_params=_params("parallel"),
    )(proj, ys, yp, yb)


def merge_bwd(dm, proj, ys, yp, yb, name):
    t = ys.shape[0]
    tr = min(512, t)

    def body(dm_ref, m_ref, ys_ref, yp_ref, yb_ref, d0_ref, d1_ref, d2_ref, dl_ref):
        dmv = dm_ref[...].astype(F32)
        for k, (ref, dref) in enumerate(((ys_ref, d0_ref), (yp_ref, d1_ref), (yb_ref, d2_ref))):
            g = _sigmoid(m_ref[:, k * D:(k + 1) * D].astype(F32))
            dref[...] = (g * dmv).astype(BF16)
            dl_ref[:, k * D:(k + 1) * D] = (dmv * ref[...].astype(F32) * g * (1.0 - g)).astype(BF16)

    rowblk = pl.BlockSpec((tr, D), lambda i: (i, 0))
    wide = pl.BlockSpec((tr, 3 * D), lambda i: (i, 0))
    return pl.pallas_call(
        body, name=name, grid=(t // tr,),
        in_specs=[rowblk, wide, rowblk, rowblk, rowblk],
        out_specs=[rowblk, rowblk, rowblk, wide],
        out_shape=[jax.ShapeDtypeStruct((t, D), BF16)] * 3 + [jax.ShapeDtypeStruct((t, 3 * D), BF16)],
        compiler_params=_params("parallel"),
    )(dm, proj, ys, yp, yb)


def layer_fwd(x, lw, bl, s, tag, comm=None):
    t = bl * s
    h = rmsnorm_fwd(x, lw["norm_w"], f"norm_fwd{tag}")
    proj = matmul(h, lw["w_in"], "nt", BF16, f"in_proj{tag}", tn=2048)
    xa = conv_fwd(proj, lw["conv_w"], lw["conv_b"], bl, s, f"conv_fwd{tag}")
    dtT, acT, dtC, acC = dt_fwd(proj, lw["dt_bias"], lw["a_log"], t, f"dt_fwd{tag}")
    dskip_l = jnp.repeat(lw["d_skip"], 64).reshape(1, SSM_W)
    y, prev = ssd_fwd(xa, acT, dtC, acC, dskip_l, bl, s, f"ssd_fwd{tag}")
    s_out = gnorm_fwd(y, proj, lw["ssm_norm_w"], f"gnorm_fwd{tag}")
    p_out = pool_fwd(proj, lw["pool_w"], lw["pool_scale"], bl, s, f"pool_fwd{tag}")
    sb_out, sb_o, sb_r, *carried = sb_fwd(proj, bl, s, f"sb_fwd{tag}", comm)
    ys = matmul(s_out, lw["w_proj_ssm"], "nn", BF16, f"proj_ssm{tag}")
    yp = matmul(p_out, lw["w_proj_pool"], "nn", BF16, f"proj_pool{tag}")
    yb = matmul(sb_out, lw["w_proj_sb"], "nn", BF16, f"proj_sb{tag}")
    merged = merge_fwd(proj, ys, yp, yb, f"merge_fwd{tag}")
    x_next = matmul(merged, lw["w_out"], "nn", F32, f"out_proj{tag}", residual=x)
    saved = dict(x=x, h=h, proj=proj, xa=xa, dtT=dtT, acT=acT, dtC=dtC, acC=acC, y=y, prev=prev, s_out=s_out, p_out=p_out,
                 sb_out=sb_out, sb_o=sb_o, sb_r=sb_r, ys=ys, yp=yp, yb=yb, merged=merged)
    return x_next, saved, (carried[0] if carried else None)


def layer_bwd(dx, dx_b, lw, sv, bl, s, tag, comm, own_slabs):
    t = bl * s
    g = {}
    dmerged = matmul(dx_b, lw["w_out"], "nt", BF16, f"d_merged{tag}")
    g["w_out"] = matmul(sv["merged"], dx_b, "tn", BF16, f"dw_out{tag}")
    dys, dyp, dyb, dlogit = merge_bwd(dmerged, sv["proj"], sv["ys"], sv["yp"], sv["yb"], f"merge_bwd{tag}")
    ds_out = matmul(dys, lw["w_proj_ssm"], "nt", BF16, f"d_sout{tag}")
    g["w_proj_ssm"] = matmul(sv["s_out"], dys, "tn", BF16, f"dw_proj_ssm{tag}")
    dp_out = matmul(dyp, lw["w_proj_pool"], "nt", BF16, f"d_pout{tag}")
    g["w_proj_pool"] = matmul(sv["p_out"], dyp, "tn", BF16, f"dw_proj_pool{tag}")
    dsb_out = matmul(dyb, lw["w_proj_sb"], "nt", BF16, f"d_sbout{tag}")
    g["w_proj_sb"] = matmul(sv["sb_out"], dyb, "tn", BF16, f"dw_proj_sb{tag}")
    dy, dz, dnw = gnorm_bwd(ds_out, sv["y"], sv["proj"], lw["ssm_norm_w"], f"gnorm_bwd{tag}")
    g["ssm_norm_w"] = dnw[0]
    dskip_l = jnp.repeat(lw["d_skip"], 64).reshape(1, SSM_W)
    dxs, db, dc, ddtT, dacT, dsk = ssd_bwd(dy, sv["xa"], sv["acT"], sv["dtC"], sv["acC"], dskip_l, sv["prev"], bl, s,
                                           f"ssd_bwd{tag}")
    g["d_skip"] = jnp.sum(dsk.reshape(N_HEADS, 64), axis=1)
    ddt_raw, da, dbias = dt_bwd(ddtT, dacT, sv["dtT"], sv["proj"], lw["dt_bias"], lw["a_log"], t, f"dt_bwd{tag}")
    g["a_log"] = da * (-jnp.exp(lw["a_log"]))
    g["dt_bias"] = dbias
    dxa = jnp.concatenate([dxs, db.astype(BF16), dc.astype(BF16)], axis=1)
    dxbc, dcw, dcb = conv_bwd(dxa, sv["proj"], lw["conv_w"], lw["conv_b"], bl, s, f"conv_bwd{tag}")
    g["conv_w"] = dcw
    g["conv_b"] = dcb
    dpu, dpg, dpw, dpsc = pool_bwd(dp_out, sv["proj"], lw["pool_w"], lw["pool_scale"], bl, s, f"pool_bwd{tag}")
    g["pool_w"] = dpw
    g["pool_scale"] = dpsc[0]
    dq, dk, dv, dsbg, *carried = sb_bwd(dsb_out, sv["sb_o"], sv["sb_r"], sv["proj"], bl, s, f"sb_bwd{tag}", comm)
    dproj = concat_columns([dlogit, dsbg, dpu, dpg, dz, dq, dk, dv, dxbc, ddt_raw], PC, f"d_proj{tag}")
    g["w_in"] = matmul(dproj, sv["h"], "tn", BF16, f"dw_in{tag}", tk=2048)
    own_comm = own_slabs(g) if own_slabs is not None else None
    dh = matmul(dproj, lw["w_in"], "nn", F32, f"d_h{tag}", tk=2048, comm=own_comm)
    dh, own_got = dh if own_comm is not None else (dh, None)
    dx_in, dx_in_b, dnorm = rmsnorm_bwd(dh, sv["x"], lw["norm_w"], dx, f"norm_bwd{tag}")
    g["norm_w"] = dnorm[0]
    return dx_in, dx_in_b, g, (carried[0] if carried else None), (own_comm[0] if own_comm else None), own_got


def concat_columns(parts, width, name):
    t = parts[0].shape[0]
    tr = min(256, t)
    widths = [p.shape[1] for p in parts]
    used = sum(widths)

    def body(*refs):
        o_ref = refs[-1]
        off = 0
        for ref, w in zip(refs[:-1], widths):
            o_ref[:, off:off + w] = ref[...]
            off += w
        if width > used:
            o_ref[:, used:] = jnp.zeros((tr, width - used), BF16)

    return pl.pallas_call(
        body, name=name, grid=(t // tr,),
        in_specs=[pl.BlockSpec((tr, w), lambda i: (i, 0)) for w in widths],
        out_specs=pl.BlockSpec((tr, width), lambda i: (i, 0)),
        out_shape=jax.ShapeDtypeStruct((t, width), BF16),
        compiler_params=_params("parallel"),
    )(*parts)


_PAD_PIECES = ((10784, 3072), (9760, 1024), (4640, 1024), (5664, 1024), (0, 2048), (6688, 3072), (2048, 2560), (4608, 32))
_UNPAD_PIECES = ((OFF_Z, 2048), (OFF_XBC, 2560), (OFF_DT, 32), (OFF_PU, 1024), (OFF_PG, 1024), (OFF_QKV, 3072),
                 (OFF_SBG, 1024), (OFF_MERGE, 3072))


def pad_rows(wt):
    pieces = [wt[o:o + n] for o, n in _PAD_PIECES]
    return jnp.concatenate(pieces + [jnp.zeros((PC - IN_COLS, wt.shape[1]), wt.dtype)], axis=0)


def unpad_rows(wp):
    return jnp.concatenate([wp[o:o + n] for o, n in _UNPAD_PIECES], axis=0)


MESH = pl.DeviceIdType.MESH
ANY = pl.BlockSpec(memory_space=pl.ANY)


def _coords():
    return lax.axis_index("x"), lax.axis_index("y"), lax.axis_index("c")


def _peer(p):
    x, y, c = _coords()
    return (1 - x if p & 4 else x, 1 - y if p & 2 else y, 1 - c if p & 1 else c)


def _flat(pos):
    return 4 * pos[0] + 2 * pos[1] + pos[2]


def _chip(pos):
    return 2 * pos[0] + pos[1]


def _exchange_copies(v_ref, out_ref, send_sems, recv_sems, local_sem, mode):
    x, y, c = _coords()
    me = _flat((x, y, c))
    sibling = (x, y, 1 - c)
    chips = [(1 - x if j & 2 else x, 1 - y if j & 1 else y) for j in range(1, 4)]

    def copy(k, src, landing, to):
        return pltpu.make_async_remote_copy(src_ref=src, dst_ref=out_ref.at[landing], send_sem=send_sems.at[k],
                                            recv_sem=recv_sems.at[k], device_id=to, device_id_type=MESH)

    if mode == "direct":
        local = pltpu.make_async_copy(v_ref, out_ref.at[me], local_sem)
        first = [copy(p - 1, v_ref, me, _peer(p)) for p in range(1, N_DEV)]
        last = [copy(p - 1, v_ref, _flat(_peer(p)), _peer(p)) for p in range(1, N_DEV)]
        return local, first, [], last
    if mode == "gather":
        local = pltpu.make_async_copy(v_ref, out_ref.at[me], local_sem)
        first = [copy(0, v_ref, me, sibling)] + [copy(1 + j, v_ref, me, (*ch, c)) for j, ch in enumerate(chips)]
        relay = [(copy(1 + j, v_ref, _flat((*ch, c)), (*ch, c)),
                  copy(4 + j, out_ref.at[_flat((*ch, c))], _flat((*ch, c)), sibling)) for j, ch in enumerate(chips)]
        last = [copy(0, v_ref, _flat(sibling), sibling)] + [
            copy(4 + j, v_ref, _flat((*ch, 1 - c)), sibling) for j, ch in enumerate(chips)]
        return local, first, relay, last
    assert mode == "chips"
    mine = _chip((x, y))
    local = pltpu.make_async_copy(v_ref.at[mine], out_ref.at[mine], local_sem)
    first = [copy(j, v_ref.at[_chip(ch)], mine, (*ch, c)) for j, ch in enumerate(chips)]
    last = [copy(j, v_ref.at[mine], _chip(ch), (*ch, c)) for j, ch in enumerate(chips)]
    return local, first, [], last


def _swap_copies(s0_ref, s1_ref, out_ref, send_sems, recv_sems, local_sem):
    x, y, c = _coords()

    def four(src_ref):
        return [pltpu.make_async_remote_copy(src_ref=src_ref.at[j], dst_ref=out_ref.at[j], send_sem=send_sems.at[j],
                                             recv_sem=recv_sems.at[j], device_id=(x, y, 1 - c), device_id_type=MESH)
                for j in range(4)]

    return c, four(s1_ref), four(s0_ref)


def _exchange_start(*refs_and_mode):
    if refs_and_mode[-1] == "swap":
        c, from_core0, from_core1 = _swap_copies(*refs_and_mode[:-1])
        for core, copies in ((0, from_core0), (1, from_core1)):
            @pl.when(c == core)
            def _():
                for cp in copies:
                    cp.start()
        return
    local, first, _, _ = _exchange_copies(*refs_and_mode)
    local.start()
    for cp in first:
        cp.start()


def _exchange_relay(*refs_and_mode):
    if refs_and_mode[-1] == "swap":
        return
    for arrival, onward in _exchange_copies(*refs_and_mode)[2]:
        arrival.wait_recv()
        onward.start()


def _exchange_finish(*refs_and_mode):
    if refs_and_mode[-1] == "swap":
        _, four, _ = _swap_copies(*refs_and_mode[:-1])
        for cp in four:
            cp.wait_recv()
        for cp in four:
            cp.wait_send()
        return
    local, first, relay, last = _exchange_copies(*refs_and_mode)
    for cp in last:
        cp.wait_recv()
    for cp in first + [onward for _, onward in relay]:
        cp.wait_send()
    local.wait()


def _exchange_shape(v, mode):
    if mode == "swap":
        return jax.ShapeDtypeStruct(tuple(v[0].shape), v[0].dtype)
    return jax.ShapeDtypeStruct(tuple(v.shape) if mode == "chips" else (N_DEV,) + tuple(v.shape), v.dtype)


def _exchange_sems():
    return [pltpu.SemaphoreType.DMA((N_DEV - 1,)), pltpu.SemaphoreType.DMA((N_DEV - 1,)), pltpu.SemaphoreType.DMA]


def exchange(v, mode, name):
    def body(*refs):
        _exchange_start(*refs, mode)
        _exchange_relay(*refs, mode)
        _exchange_finish(*refs, mode)

    args = list(v) if mode == "swap" else [v]
    return pl.pallas_call(
        body, name=name,
        in_specs=[ANY] * len(args), out_specs=ANY,
        out_shape=_exchange_shape(v, mode),
        scratch_shapes=_exchange_sems(),
    )(*args)


def _hosted_exchange(comm, grid, relay_at):
    if comm is None:
        return [], [], [], [], [], (lambda refs: None), (lambda refs: None)
    v, mode = comm

    def at(step):
        cond = None
        for axis, want in enumerate(step):
            term = pl.program_id(axis) == want
            cond = term if cond is None else jnp.logical_and(cond, term)
        return cond

    def start(refs):
        @pl.when(at([0] * len(grid)))
        def _():
            _exchange_start(*refs, mode)

        if mode == "gather":
            @pl.when(at(relay_at))
            def _():
                _exchange_relay(*refs, mode)

    def wait(refs):
        @pl.when(at([n - 1 for n in grid]))
        def _():
            _exchange_finish(*refs, mode)

    args = list(v) if mode == "swap" else [v]
    return [ANY] * len(args), args, [ANY], [_exchange_shape(v, mode)], _exchange_sems(), start, wait


def pair_sum(slabs, got, name):
    _, r, c = got.shape
    tr = r // 4 if r % 64 == 0 else r

    def body(s0_ref, s1_ref, got_ref, o_ref):
        mine = jnp.where(lax.axis_index("c") == 0, s0_ref[...].astype(F32), s1_ref[...].astype(F32))
        o_ref[...] = (mine + got_ref[...].astype(F32)).astype(BF16)

    blk = pl.BlockSpec((None, tr, c), lambda j, i: (j, i, 0))
    return pl.pallas_call(
        body, name=name, grid=(4, r // tr),
        in_specs=[blk, blk, blk], out_specs=blk,
        out_shape=jax.ShapeDtypeStruct(got.shape, BF16),
        compiler_params=_params("parallel", "parallel"),
    )(*slabs, got)


def sum_slabs(v, name):
    n, r, c = v.shape
    tr = 128 if r % 128 == 0 else r

    def body(v_ref, o_ref):
        acc = v_ref[0].astype(F32)
        for k in range(1, n):
            acc = acc + v_ref[k].astype(F32)
        o_ref[...] = acc

    return pl.pallas_call(
        body, name=name, grid=(r // tr,),
        in_specs=[pl.BlockSpec((n, tr, c), lambda i: (0, i, 0))],
        out_specs=pl.BlockSpec((tr, c), lambda i: (i, 0)),
        out_shape=jax.ShapeDtypeStruct((r, c), F32),
        compiler_params=_params("parallel"),
    )(v)


def adamw(w, g, m, v, name):
    r, c = w.shape
    tr = next((cand for cand in (256, 128, 64, 32, 16, 8) if r % cand == 0), r)

    def body(w_ref, g_ref, m_ref, v_ref, d_ref, mo_ref, vo_ref):
        gv = g_ref[...]
        mn = ADAM_B1 * m_ref[...] + (1.0 - ADAM_B1) * gv
        vn = ADAM_B2 * v_ref[...] + (1.0 - ADAM_B2) * (gv * gv)
        m_hat = mn / (1.0 - ADAM_B1 ** ADAM_STEP)
        v_hat = vn / (1.0 - ADAM_B2 ** ADAM_STEP)
        d_ref[...] = -ADAM_LR * (m_hat / (jnp.sqrt(v_hat) + ADAM_EPS) + ADAM_WD * w_ref[...])
        mo_ref[...] = mn
        vo_ref[...] = vn

    blk = pl.BlockSpec((tr, c), lambda i: (i, 0))
    return pl.pallas_call(
        body, name=name, grid=(r // tr,),
        in_specs=[blk] * 4, out_specs=[blk] * 3,
        out_shape=[jax.ShapeDtypeStruct((r, c), F32)] * 3,
        compiler_params=_params("parallel"),
    )(w, g, m, v)


BIG = ("w_proj_ssm", "w_proj_pool", "w_proj_sb", "w_out", "pool_w", "w_in")
SHARD_IN = IN_COLS // N_DEV
BIG_ROWS = {"w_proj_ssm": SSM_W // N_DEV, "w_proj_pool": POOL_W // N_DEV, "w_proj_sb": SB_W // N_DEV,
            "w_out": D // N_DEV, "pool_w": POOL_G * (POOL_GD // N_DEV) * POOL_GD // D, "w_in": SHARD_IN}
PACK_C = D
PACK_R = 2432

REPLICATED = ("norm_w", "conv_b", "dt_bias", "a_log", "d_skip", "ssm_norm_w", "pool_scale")
WEIGHTS = ("norm_w", "w_in", "conv_w", "conv_b", "dt_bias", "a_log", "d_skip", "ssm_norm_w", "pool_w",
           "pool_scale", "w_proj_ssm", "w_proj_pool", "w_proj_sb", "w_out", "final_norm_w")


def _size(shape):
    n = 1
    for d in shape:
        n *= d
    return n


def _pad_flat(flat, n):
    return jnp.concatenate([flat, jnp.zeros((n - flat.shape[0],), flat.dtype)])


def _row_offsets():
    offs, off = {}, 0
    for n in BIG:
        offs[n] = off
        off += BIG_ROWS[n]
    return offs, off


def pack_shards(parts):
    rows = [parts[n].reshape(BIG_ROWS[n], PACK_C) for n in BIG]
    rows[-1] = jnp.pad(rows[-1], ((0, PACK_R - _row_offsets()[1]), (0, 0)))
    return jnp.concatenate(rows, axis=0)


def unpack_shards(packed):
    offs, _ = _row_offsets()
    out = {}
    for n in BIG:
        seg = packed[offs[n]:offs[n] + BIG_ROWS[n]]
        if n == "w_in":
            out[n] = seg.T
        elif n == "pool_w":
            out[n] = seg.reshape(POOL_G, POOL_GD // N_DEV, POOL_GD)
        else:
            out[n] = seg
    return out


def unpack_gathered(g):
    offs, _ = _row_offsets()
    out = {}
    for n in BIG:
        seg = g[:, offs[n]:offs[n] + BIG_ROWS[n], :]
        if n == "w_in":
            out[n] = pad_rows(seg.reshape(IN_COLS, D))
        elif n == "pool_w":
            out[n] = seg.reshape(N_DEV, POOL_G, POOL_GD // N_DEV, POOL_GD).transpose(1, 0, 2, 3).reshape(
                POOL_G, POOL_GD, POOL_GD)
        else:
            out[n] = seg.reshape(N_DEV * BIG_ROWS[n], D)
    return out


def pack_slabs(g):
    segs = []
    for n in BIG:
        if n == "w_in":
            w = unpad_rows(g[n])
        elif n == "pool_w":
            w = g[n].reshape(POOL_G, N_DEV, POOL_GD // N_DEV, POOL_GD).transpose(1, 0, 2, 3)
        else:
            w = g[n]
        segs.append(w.reshape(N_DEV // 2, 2, BIG_ROWS[n], PACK_C).astype(BF16))
    segs[-1] = jnp.pad(segs[-1], ((0, 0), (0, 0), (0, PACK_R - _row_offsets()[1]), (0, 0)))
    return tuple(jnp.concatenate([seg[:, core] for seg in segs], axis=1) for core in range(2))


SMALL_ROWS = 544


def pack_small(vals):
    flat = jnp.concatenate([v.reshape(-1) for v in vals])
    return _pad_flat(flat, SMALL_ROWS * 128).reshape(SMALL_ROWS, 128)


def unpack_small(packed, shapes):
    flat = packed.reshape(-1)
    out, off = [], 0
    for shp in shapes:
        out.append(flat[off:off + _size(shp)].reshape(shp))
        off += _size(shp)
    return out


def kernel(x, norm_w, w_in, conv_w, conv_b, dt_bias, a_log, d_skip, ssm_norm_w, pool_w, pool_scale, w_proj_ssm, w_proj_pool, w_proj_sb, w_out, final_norm_w, loss_target, m_norm_w, m_w_in, m_conv_w, m_conv_b, m_dt_bias, m_a_log, m_d_skip, m_ssm_norm_w, m_pool_w, m_pool_scale, m_w_proj_ssm, m_w_proj_pool, m_w_proj_sb, m_w_out, m_final_norm_w, v_norm_w, v_w_in, v_conv_w, v_conv_b, v_dt_bias, v_a_log, v_d_skip, v_ssm_norm_w, v_pool_w, v_pool_scale, v_w_proj_ssm, v_w_proj_pool, v_w_proj_sb, v_w_out, v_final_norm_w):
    wts = dict(norm_w=norm_w, w_in=w_in, conv_w=conv_w, conv_b=conv_b, dt_bias=dt_bias, a_log=a_log, d_skip=d_skip,
               ssm_norm_w=ssm_norm_w, pool_w=pool_w, pool_scale=pool_scale, w_proj_ssm=w_proj_ssm,
               w_proj_pool=w_proj_pool, w_proj_sb=w_proj_sb, w_out=w_out, final_norm_w=final_norm_w)
    mom = dict(norm_w=m_norm_w, w_in=m_w_in, conv_w=m_conv_w, conv_b=m_conv_b, dt_bias=m_dt_bias, a_log=m_a_log,
               d_skip=m_d_skip, ssm_norm_w=m_ssm_norm_w, pool_w=m_pool_w, pool_scale=m_pool_scale,
               w_proj_ssm=m_w_proj_ssm, w_proj_pool=m_w_proj_pool, w_proj_sb=m_w_proj_sb, w_out=m_w_out,
               final_norm_w=m_final_norm_w)
    var = dict(norm_w=v_norm_w, w_in=v_w_in, conv_w=v_conv_w, conv_b=v_conv_b, dt_bias=v_dt_bias, a_log=v_a_log,
               d_skip=v_d_skip, ssm_norm_w=v_ssm_norm_w, pool_w=v_pool_w, pool_scale=v_pool_scale,
               w_proj_ssm=v_w_proj_ssm, w_proj_pool=v_w_proj_pool, w_proj_sb=v_w_proj_sb, w_out=v_w_out,
               final_norm_w=v_final_norm_w)
    bl, s, _ = x.shape
    t = bl * s
    me = _flat(_coords())

    cw = exchange(conv_w.reshape(40, 128), "direct", "gather_conv_w")
    conv_w_full = cw.reshape(N_DEV, DEPTH, CONV_K, CONV_CH // N_DEV).transpose(1, 2, 0, 3).reshape(
        DEPTH, CONV_K, CONV_CH)

    xc = x.reshape(t, D)
    layer_w, saved = [], []
    packed = [pack_shards({n: (wts[n][l].T if n == "w_in" else wts[n][l]).astype(BF16) for n in BIG})
              for l in range(DEPTH)]
    gathered = exchange(packed[0], "gather", "gather_w0")
    for l in range(DEPTH):
        lw = unpack_gathered(gathered)
        for n in REPLICATED:
            lw[n] = wts[n][l]
        lw["conv_w"] = conv_w_full[l]
        xc, sv, gathered = layer_fwd(xc, lw, bl, s, f"_l{l}", (packed[l + 1], "gather") if l + 1 < DEPTH else None)
        layer_w.append(lw)
        saved.append(sv)

    loss_part, dx, dx_b, dfinal = final_loss(xc, final_norm_w, loss_target.reshape(t, D), "final_loss")
    loss = lax.psum(loss_part[0, 0], ("x", "y", "c"))

    grads = [None] * DEPTH
    big_sum = [None] * DEPTH
    def last_layer_slabs(g):
        slabs = pack_slabs(g)
        return pair_sum(slabs, exchange(slabs, "swap", "pair_swap0"), "pair_sum0"), "chips"

    chip_sums = None
    for l in reversed(range(DEPTH)):
        dx, dx_b, g, got, own_sent, own_got = layer_bwd(
            dx, dx_b, layer_w[l], saved[l], bl, s, f"_l{l}", (chip_sums, "chips") if chip_sums is not None else None,
            last_layer_slabs if l == 0 else (lambda g: (pack_slabs(g), "swap")))
        if got is not None:
            big_sum[l + 1] = unpack_shards(sum_slabs(got, f"sum_g{l + 1}"))
        grads[l] = g
        if l > 0:
            chip_sums = pair_sum(own_sent, own_got, f"pair_sum{l}")
    big_sum[0] = unpack_shards(sum_slabs(own_got, "sum_g0"))
    grad_x = dx.reshape(bl, s, D)

    small_names = REPLICATED + ("conv_w",)
    small_vals = [jnp.stack([grads[l][n] for l in range(DEPTH)]) for n in small_names] + [dfinal[0]]
    small_shapes = [v.shape for v in small_vals]
    small_all = exchange(pack_small(small_vals), "direct", "gather_small")
    small_sum = unpack_small(sum_slabs(small_all, "sum_small"), small_shapes)
    gsum = dict(zip(small_names + ("final_norm_w",), small_sum))
    conv_g_full = gsum["conv_w"]
    gsum["conv_w"] = lax.dynamic_slice_in_dim(conv_g_full, me * (CONV_CH // N_DEV), CONV_CH // N_DEV, axis=2)
    for n in BIG:
        gsum[n] = jnp.stack([big_sum[l][n] for l in range(DEPTH)])

    delta, new_m, new_v = {}, {}, {}
    for n in BIG + ("conv_w",):
        shp = wts[n].shape
        two_d = (-1, shp[-1])
        d2, m2, v2 = adamw(wts[n].reshape(two_d), gsum[n].reshape(two_d), mom[n].reshape(two_d),
                           var[n].reshape(two_d), f"adamw_{n}")
        delta[n], new_m[n], new_v[n] = d2.reshape(shp), m2.reshape(shp), v2.reshape(shp)
    rep = REPLICATED + ("final_norm_w",)
    rep_shapes = [wts[n].shape for n in rep]
    d2, m2, v2 = adamw(pack_small([wts[n] for n in rep]), pack_small([gsum[n] for n in rep]),
                       pack_small([mom[n] for n in rep]), pack_small([var[n] for n in rep]), "adamw_small")
    for n, dv, mv, vv in zip(rep, unpack_small(d2, rep_shapes), unpack_small(m2, rep_shapes),
                             unpack_small(v2, rep_shapes)):
        delta[n], new_m[n], new_v[n] = dv, mv, vv

    return (loss, grad_x, *[gsum[n] for n in WEIGHTS], *[delta[n] for n in WEIGHTS],
            *[new_m[n] for n in WEIGHTS], *[new_v[n] for n in WEIGHTS])
```

```python
import functools

import jax
import jax.numpy as jnp
from jax import lax
from jax.experimental import pallas as pl
from jax.experimental.pallas import tpu as pltpu

F32 = jnp.float32
BF16 = jnp.bfloat16

N_DEV = 8
DEPTH = 4
D = 1024
SSM_W = 2048
N_HEADS = 32
N_PAIRS = 16
N_GROUPS = 2
N_STATE = 128
CHUNK = 128
CONV_CH = 2560
CONV_K = 4
POOL_W = 1024
POOL_G = 4
POOL_GD = 256
SB_W = 1024
SB_PAIRS = 8
QB = 256
EPS = 1e-6
IN_COLS = 13856

PC = 14336
OFF_MERGE = 0
OFF_SBG = 3072
OFF_PU = 4096
OFF_PG = 5120
OFF_Z = 6144
OFF_QKV = 8192
OFF_XBC = 11264
OFF_DT = 13824

ADAM_LR = 0.001
ADAM_B1 = 0.9
ADAM_B2 = 0.999
ADAM_EPS = 1e-08
ADAM_WD = 0.01
ADAM_STEP = 10

VMEM_LIMIT = 56 * 1024 * 1024

_NN = (((1,), (0,)), ((), ()))
_NT = (((1,), (1,)), ((), ()))
_TN = (((0,), (0,)), ((), ()))


def _dot(a, b, dn=_NN):
    return lax.dot_general(a, b, dn, preferred_element_type=F32)


def _sigmoid(x):
    return 1.0 / (1.0 + jnp.exp(-x))


def _softplus(x):
    return jnp.maximum(x, 0.0) + jnp.log(1.0 + jnp.exp(-jnp.abs(x)))


def _split2(x):
    hi = x.astype(BF16)
    lo = (x - hi.astype(F32)).astype(BF16)
    return hi, lo


def _split3(x):
    hi = x.astype(BF16)
    r = x - hi.astype(F32)
    mid = r.astype(BF16)
    lo = (r - mid.astype(F32)).astype(BF16)
    return hi, mid, lo


def _params(*sem):
    return pltpu.CompilerParams(dimension_semantics=sem, vmem_limit_bytes=VMEM_LIMIT)


def matmul(a, b, mode, out_dtype, name, residual=None, tm=1024, tn=1024, tk=1024, comm=None):
    if mode == "nn":
        (m, k), (k2, n) = a.shape, b.shape
    elif mode == "nt":
        (m, k), (n, k2) = a.shape, b.shape
    else:
        (k, m), (k2, n) = a.shape, b.shape
    assert k == k2
    tm, tn, tk = min(tm, m), min(tn, n), min(tk, k)
    assert m % tm == 0 and n % tn == 0 and k % tk == 0
    nk = k // tk
    dn = {"nn": _NN, "nt": _NT, "tn": _TN}[mode]
    a_spec = pl.BlockSpec((tk, tm), lambda i, j, kk: (kk, i)) if mode == "tn" else pl.BlockSpec((tm, tk), lambda i, j, kk: (i, kk))
    b_spec = pl.BlockSpec((tn, tk), lambda i, j, kk: (j, kk)) if mode == "nt" else pl.BlockSpec((tk, tn), lambda i, j, kk: (kk, j))
    in_specs = [a_spec, b_spec]
    args = [a, b]
    if residual is not None:
        in_specs.append(pl.BlockSpec((tm, tn), lambda i, j, kk: (i, j)))
        args.append(residual)
    grid = (m // tm, n // tn, nk)
    n_in = len(args)
    x_in, x_args, x_out, x_shape, x_scratch, x_start, x_wait = _hosted_exchange(comm, grid, relay_at=None)

    def body(*refs):
        n_xi, n_xo = len(x_in), len(x_out)
        a_ref, b_ref = refs[:2]
        r_ref = refs[2] if residual is not None else None
        o_ref = refs[n_in + n_xi]
        acc_ref = refs[n_in + n_xi + n_xo + 1]
        x_refs = refs[n_in:n_in + n_xi] + refs[n_in + n_xi + 1:n_in + n_xi + n_xo + 1] + refs[n_in + n_xi + n_xo + 2:]
        x_start(x_refs)
        kk = pl.program_id(2)
        p = _dot(a_ref[...], b_ref[...], dn)

        def finish(val):
            if r_ref is not None:
                val = val + r_ref[...]
            o_ref[...] = val.astype(out_dtype)

        if nk == 1:
            finish(p)
        else:
            @pl.when(kk == 0)
            def _():
                acc_ref[...] = p

            @pl.when(kk > 0)
            def _():
                acc_ref[...] += p

            @pl.when(kk == nk - 1)
            def _():
                finish(acc_ref[...])

        x_wait(x_refs)

    out = pl.pallas_call(
        body, name=name,
        grid=grid,
        in_specs=in_specs + x_in,
        out_specs=[pl.BlockSpec((tm, tn), lambda i, j, kk: (i, j))] + x_out,
        out_shape=[jax.ShapeDtypeStruct((m, n), out_dtype)] + x_shape,
        scratch_shapes=[pltpu.VMEM((tm, tn) if nk > 1 else (8, 128), F32)] + x_scratch,
        compiler_params=_params(*(("arbitrary",) * 3 if comm is not None else ("parallel", "parallel", "arbitrary"))),
    )(*args, *x_args)
    return tuple(out) if comm is not None else out[0]


def rmsnorm_fwd(x, w, name):
    t, d = x.shape
    tr = min(512, t)

    def body(x_ref, w_ref, h_ref):
        xv = x_ref[...]
        r = lax.rsqrt(jnp.mean(xv * xv, axis=-1, keepdims=True) + EPS)
        h_ref[...] = (xv * r * w_ref[...]).astype(BF16)

    return pl.pallas_call(
        body, name=name, grid=(t // tr,),
        in_specs=[pl.BlockSpec((tr, d), lambda i: (i, 0)), pl.BlockSpec((1, d), lambda i: (0, 0))],
        out_specs=pl.BlockSpec((tr, d), lambda i: (i, 0)),
        out_shape=jax.ShapeDtypeStruct((t, d), BF16),
        compiler_params=_params("parallel"),
    )(x, w.reshape(1, d))


def rmsnorm_bwd(dh, x, w, dres, name):
    t, d = x.shape
    tr = min(512, t)

    def body(dh_ref, x_ref, w_ref, dres_ref, dx_ref, dxb_ref, dw_ref):
        xv = x_ref[...]
        r = lax.rsqrt(jnp.mean(xv * xv, axis=-1, keepdims=True) + EPS)
        xh = xv * r
        g = dh_ref[...].astype(F32)
        dxh = g * w_ref[...]
        dxv = dres_ref[...] + r * (dxh - xh * jnp.mean(dxh * xh, axis=-1, keepdims=True))
        dx_ref[...] = dxv
        dxb_ref[...] = dxv.astype(BF16)
        part = jnp.sum(g * xh, axis=0, keepdims=True)

        @pl.when(pl.program_id(0) == 0)
        def _():
            dw_ref[...] = part

        @pl.when(pl.program_id(0) > 0)
        def _():
            dw_ref[...] += part

    return pl.pallas_call(
        body, name=name, grid=(t // tr,),
        in_specs=[pl.BlockSpec((tr, d), lambda i: (i, 0)), pl.BlockSpec((tr, d), lambda i: (i, 0)),
                  pl.BlockSpec((1, d), lambda i: (0, 0)), pl.BlockSpec((tr, d), lambda i: (i, 0))],
        out_specs=[pl.BlockSpec((tr, d), lambda i: (i, 0)), pl.BlockSpec((tr, d), lambda i: (i, 0)),
                   pl.BlockSpec((1, d), lambda i: (0, 0))],
        out_shape=[jax.ShapeDtypeStruct((t, d), F32), jax.ShapeDtypeStruct((t, d), BF16),
                   jax.ShapeDtypeStruct((1, d), F32)],
        compiler_params=_params("arbitrary"),
    )(dh, x, w.reshape(1, d), dres)


def final_loss(x, w, target, name):
    t, d = x.shape
    tr = min(512, t)

    def body(x_ref, w_ref, tg_ref, loss_ref, dx_ref, dxb_ref, dw_ref):
        xv = x_ref[...]
        r = lax.rsqrt(jnp.mean(xv * xv, axis=-1, keepdims=True) + EPS)
        xh = xv * r
        err = xh * w_ref[...] - tg_ref[...]
        lpart = 0.5 * jnp.sum(jnp.mean(err * err, axis=-1, keepdims=True), axis=0, keepdims=True)
        dy = err * (1.0 / d)
        dxh = dy * w_ref[...]
        dxv = r * (dxh - xh * jnp.mean(dxh * xh, axis=-1, keepdims=True))
        dx_ref[...] = dxv
        dxb_ref[...] = dxv.astype(BF16)
        part = jnp.sum(dy * xh, axis=0, keepdims=True)

        @pl.when(pl.program_id(0) == 0)
        def _():
            dw_ref[...] = part
            loss_ref[...] = jnp.broadcast_to(lpart, (1, 128))

        @pl.when(pl.program_id(0) > 0)
        def _():
            dw_ref[...] += part
            loss_ref[...] += jnp.broadcast_to(lpart, (1, 128))

    return pl.pallas_call(
        body, name=name, grid=(t // tr,),
        in_specs=[pl.BlockSpec((tr, d), lambda i: (i, 0)), pl.BlockSpec((1, d), lambda i: (0, 0)),
                  pl.BlockSpec((tr, d), lambda i: (i, 0))],
        out_specs=[pl.BlockSpec((1, 128), lambda i: (0, 0)), pl.BlockSpec((tr, d), lambda i: (i, 0)),
                   pl.BlockSpec((tr, d), lambda i: (i, 0)), pl.BlockSpec((1, d), lambda i: (0, 0))],
        out_shape=[jax.ShapeDtypeStruct((1, 128), F32), jax.ShapeDtypeStruct((t, d), F32),
                   jax.ShapeDtypeStruct((t, d), BF16), jax.ShapeDtypeStruct((1, d), F32)],
        compiler_params=_params("arbitrary"),
    )(x, w.reshape(1, d), target)


CONV_BW = 256


def _shift_down(u, s, row):
    return jnp.where(row >= s, pltpu.roll(u, s, axis=0), 0.0)


def _shift_up(u, s, row, n):
    return jnp.where(row < n - s, pltpu.roll(u, n - s, axis=0), 0.0)


def _conv_pre(u, w, b, row):
    acc = b + w[CONV_K - 1:CONV_K, :] * u
    for k in range(CONV_K - 1):
        acc = acc + w[k:k + 1, :] * _shift_down(u, CONV_K - 1 - k, row)
    return acc


def conv_fwd(proj, conv_w, conv_b, bl, s, name):
    t = bl * s
    nb = CONV_CH // CONV_BW
    off = OFF_XBC // CONV_BW

    def body(u_ref, w_ref, b_ref, o_ref):
        u = u_ref[...].astype(F32)
        row = lax.broadcasted_iota(jnp.int32, u.shape, 0)
        xc = _conv_pre(u, w_ref[...], b_ref[...], row)
        o_ref[...] = (xc * _sigmoid(xc)).astype(BF16)

    return pl.pallas_call(
        body, name=name, grid=(bl, nb),
        in_specs=[pl.BlockSpec((s, CONV_BW), lambda b, j: (b, off + j)),
                  pl.BlockSpec((CONV_K, CONV_BW), lambda b, j: (0, j)),
                  pl.BlockSpec((1, CONV_BW), lambda b, j: (0, j))],
        out_specs=pl.BlockSpec((s, CONV_BW), lambda b, j: (b, j)),
        out_shape=jax.ShapeDtypeStruct((t, CONV_CH), BF16),
        compiler_params=_params("parallel", "parallel"),
    )(proj, conv_w, conv_b.reshape(1, CONV_CH))


def conv_bwd(dxa, proj, conv_w, conv_b, bl, s, name):
    t = bl * s
    nb = CONV_CH // CONV_BW
    off = OFF_XBC // CONV_BW

    def body(d_ref, u_ref, w_ref, b_ref, du_ref, dw_ref, db_ref):
        u = u_ref[...].astype(F32)
        w = w_ref[...]
        row = lax.broadcasted_iota(jnp.int32, u.shape, 0)
        xc = _conv_pre(u, w, b_ref[...], row)
        sg = _sigmoid(xc)
        dxc = d_ref[...].astype(F32) * sg * (1.0 + xc * (1.0 - sg))
        du = w[CONV_K - 1:CONV_K, :] * dxc
        dws = [None] * CONV_K
        dws[CONV_K - 1] = jnp.sum(dxc * u, axis=0, keepdims=True)
        for k in range(CONV_K - 1):
            up = _shift_up(dxc, CONV_K - 1 - k, row, s)
            du = du + w[k:k + 1, :] * up
            dws[k] = jnp.sum(up * u, axis=0, keepdims=True)
        du_ref[...] = du.astype(BF16)
        krow = lax.broadcasted_iota(jnp.int32, (8, CONV_BW), 0)
        dwv = sum(jnp.where(krow == k, dws[k], 0.0) for k in range(CONV_K))
        dbv = jnp.sum(dxc, axis=0, keepdims=True)

        @pl.when(pl.program_id(1) == 0)
        def _():
            dw_ref[...] = dwv
            db_ref[...] = dbv

        @pl.when(pl.program_id(1) > 0)
        def _():
            dw_ref[...] += dwv
            db_ref[...] += dbv

    du, dw, db = pl.pallas_call(
        body, name=name, grid=(nb, bl),
        in_specs=[pl.BlockSpec((s, CONV_BW), lambda j, b: (b, j)),
                  pl.BlockSpec((s, CONV_BW), lambda j, b: (b, off + j)),
                  pl.BlockSpec((CONV_K, CONV_BW), lambda j, b: (0, j)),
                  pl.BlockSpec((1, CONV_BW), lambda j, b: (0, j))],
        out_specs=[pl.BlockSpec((s, CONV_BW), lambda j, b: (b, j)),
                   pl.BlockSpec((8, CONV_BW), lambda j, b: (0, j)),
                   pl.BlockSpec((1, CONV_BW), lambda j, b: (0, j))],
        out_shape=[jax.ShapeDtypeStruct((t, CONV_CH), BF16), jax.ShapeDtypeStruct((8, CONV_CH), F32),
                   jax.ShapeDtypeStruct((1, CONV_CH), F32)],
        compiler_params=_params("parallel", "arbitrary"),
    )(dxa, proj, conv_w, conv_b.reshape(1, CONV_CH))
    return du, dw[:CONV_K], db[0]


def _tri(shape, cmp):
    r = lax.broadcasted_iota(jnp.int32, shape, 0)
    c = lax.broadcasted_iota(jnp.int32, shape, 1)
    return cmp(r, c)


def dt_fwd(proj, dt_bias, a_log, t, name):
    nchunks = t // CHUNK
    bias = jnp.zeros((1, 128), F32).at[0, :N_HEADS].set(dt_bias)
    alog = jnp.zeros((1, 128), F32).at[0, :N_HEADS].set(a_log)

    def body(raw_ref, b_ref, al_ref, dt_ref, ac_ref, dtl_ref, acl_ref):
        raw = raw_ref[...].astype(F32)
        dt = _softplus(raw + b_ref[...])
        adt = dt * (-jnp.exp(al_ref[...]))
        low = _tri((CHUNK, CHUNK), lambda r, c: r >= c).astype(BF16)
        acum = sum(_dot(low, part) for part in _split3(adt))
        dt_ref[...] = dt.T[:N_HEADS]
        ac_ref[...] = acum.T[:N_HEADS]
        spread = _tri((128, SSM_W), lambda h, lane: lane // 64 == h).astype(BF16)
        dtl_ref[...] = sum(_dot(part, spread) for part in _split2(dt))
        acl_ref[...] = sum(_dot(part, spread) for part in _split3(acum))

    return pl.pallas_call(
        body, name=name, grid=(nchunks,),
        in_specs=[pl.BlockSpec((CHUNK, 128), lambda i: (i, OFF_DT // 128)),
                  pl.BlockSpec((1, 128), lambda i: (0, 0)), pl.BlockSpec((1, 128), lambda i: (0, 0))],
        out_specs=[pl.BlockSpec((None, N_HEADS, CHUNK), lambda i: (i, 0, 0))] * 2
        + [pl.BlockSpec((CHUNK, SSM_W), lambda i: (i, 0))] * 2,
        out_shape=[jax.ShapeDtypeStruct((nchunks, N_HEADS, CHUNK), F32)] * 2
        + [jax.ShapeDtypeStruct((t, SSM_W), F32)] * 2,
        compiler_params=_params("parallel"),
    )(proj, bias, alog)


def dt_bwd(ddtT, dacT, dtT, proj, dt_bias, a_log, t, name):
    nchunks = t // CHUNK
    bias = dt_bias.reshape(N_HEADS, 1)
    alog = a_log.reshape(N_HEADS, 1)

    def body(ddt_ref, dac_ref, dt_ref, raw_ref, b_ref, al_ref, draw_ref, da_ref, db_ref):
        a = -jnp.exp(al_ref[...])
        upp = _tri((CHUNK, CHUNK), lambda r, c: r >= c).astype(BF16)
        dadt = sum(_dot(part, upp) for part in _split3(dac_ref[...]))
        ddt = ddt_ref[...] + dadt * a
        rawT = raw_ref[...].astype(F32).T[:N_HEADS]
        draw = ddt * _sigmoid(rawT + b_ref[...])
        padded = jnp.concatenate([draw, jnp.zeros((128 - N_HEADS, CHUNK), F32)], axis=0)
        draw_ref[...] = padded.T.astype(BF16)
        dav = dadt * dt_ref[...]

        @pl.when(pl.program_id(0) == 0)
        def _():
            da_ref[...] = dav
            db_ref[...] = draw

        @pl.when(pl.program_id(0) > 0)
        def _():
            da_ref[...] += dav
            db_ref[...] += draw

    draw, da, db = pl.pallas_call(
        body, name=name, grid=(nchunks,),
        in_specs=[pl.BlockSpec((None, N_HEADS, CHUNK), lambda i: (i, 0, 0))] * 3
        + [pl.BlockSpec((CHUNK, 128), lambda i: (i, OFF_DT // 128)),
           pl.BlockSpec((N_HEADS, 1), lambda i: (0, 0)), pl.BlockSpec((N_HEADS, 1), lambda i: (0, 0))],
        out_specs=[pl.BlockSpec((CHUNK, 128), lambda i: (i, 0)),
                   pl.BlockSpec((N_HEADS, CHUNK), lambda i: (0, 0)), pl.BlockSpec((N_HEADS, CHUNK), lambda i: (0, 0))],
        out_shape=[jax.ShapeDtypeStruct((t, 128), BF16), jax.ShapeDtypeStruct((N_HEADS, CHUNK), F32),
                   jax.ShapeDtypeStruct((N_HEADS, CHUNK), F32)],
        compiler_params=_params("arbitrary"),
    )(ddtT, dacT, dtT, proj, bias, alog)
    return draw, jnp.sum(da, axis=1), jnp.sum(db, axis=1)


PAIRS_G = N_PAIRS // N_GROUPS
GROUP_W = PAIRS_G * 128


def _ssd_pair(x, dtl, acl, acr, tri):
    left = lax.broadcasted_iota(jnp.int32, (CHUNK, 128), 1) < 64
    swapped = pltpu.roll(acl, 64, axis=1)
    ac_cols = [jnp.where(left, acl, swapped), jnp.where(left, swapped, acl)]
    dks = [jnp.exp(jnp.where(tri, ac_cols[e] - acr[e:e + 1], -1e30)) for e in range(2)]
    aclast = acl[CHUNK - 1:CHUNK, :]
    return left, dtl, acl, x * dtl, dks, aclast


def _ssd_specs(nc, rev):
    row = (lambda b, c, g: b * nc + (nc - 1 - c)) if rev else (lambda b, c, g: b * nc + c)
    return dict(
        wide=pl.BlockSpec((CHUNK, GROUP_W), lambda b, c, g: (row(b, c, g), g)),
        bmat=pl.BlockSpec((CHUNK, 128), lambda b, c, g: (row(b, c, g), SSM_W // 128 + g)),
        cmat=pl.BlockSpec((CHUNK, 128), lambda b, c, g: (row(b, c, g), SSM_W // 128 + N_GROUPS + g)),
        rows2=pl.BlockSpec((None, PAIRS_G, 2, CHUNK), lambda b, c, g: (row(b, c, g), g, 0, 0)),
        cols=pl.BlockSpec((CHUNK, GROUP_W), lambda b, c, g: (row(b, c, g), g)),
        rows8=pl.BlockSpec((None, PAIRS_G, 8, CHUNK), lambda b, c, g: (row(b, c, g), g, 0, 0)),
        dskip=pl.BlockSpec((1, GROUP_W), lambda b, c, g: (0, g)),
        state=pl.BlockSpec((None, PAIRS_G, N_STATE, 128), lambda b, c, g: (row(b, c, g), g, 0, 0)),
        narrow=pl.BlockSpec((CHUNK, 128), lambda b, c, g: (row(b, c, g), g)))


def ssd_fwd(xa, acT, dtC, acC, dskip_l, bl, s, name):
    t = bl * s
    nc = s // CHUNK
    ac4 = acT.reshape(bl * nc, N_PAIRS, 2, CHUNK)

    def body(x_ref, b_ref, c_ref, ac_ref, dtc_ref, acc_ref, dsk_ref, y_ref, prev_ref, st_ref):
        c = pl.program_id(1)
        g = pl.program_id(2)
        bm = b_ref[...]
        cm = c_ref[...]
        cb = _dot(cm, bm, _NT)
        tri = _tri((CHUNK, CHUNK), lambda r, c: r >= c)

        @pl.when(c == 0)
        def _():
            for p in range(PAIRS_G):
                st_ref[g * PAIRS_G + p] = jnp.zeros((N_STATE, 128), F32)

        for p in range(PAIRS_G):
            hp = g * PAIRS_G + p
            cs = slice(p * 128, (p + 1) * 128)
            x = x_ref[:, cs].astype(F32)
            left, dtl, acl, xdt, dks, aclast = _ssd_pair(x, dtc_ref[:, cs], acc_ref[:, cs], ac_ref[p], tri)
            xdt_b = xdt.astype(BF16)
            ys = [_dot((cb * dks[e]).astype(BF16), xdt_b) for e in range(2)]
            st = st_ref[hp]
            y_off = _dot(cm, st.astype(BF16)) * jnp.exp(acl)
            y_ref[:, cs] = (jnp.where(left, ys[0], ys[1]) + y_off + x * dsk_ref[:, cs]).astype(BF16)
            xw = (xdt * jnp.exp(aclast - acl)).astype(BF16)
            prev_ref[p] = st
            st_ref[hp] = st * jnp.exp(aclast) + _dot(bm, xw, _TN)

    sp = _ssd_specs(nc, False)
    return pl.pallas_call(
        body, name=name, grid=(bl, nc, N_GROUPS),
        in_specs=[sp["wide"], sp["bmat"], sp["cmat"], sp["rows2"], sp["cols"], sp["cols"], sp["dskip"]],
        out_specs=[sp["wide"], sp["state"]],
        out_shape=[jax.ShapeDtypeStruct((t, SSM_W), BF16),
                   jax.ShapeDtypeStruct((bl * nc, N_PAIRS, N_STATE, 128), F32)],
        scratch_shapes=[pltpu.VMEM((N_PAIRS, N_STATE, 128), F32)],
        compiler_params=_params("parallel", "arbitrary", "arbitrary"),
    )(xa, xa, xa, ac4, dtC, acC, dskip_l)


def ssd_bwd(dy, xa, acT, dtC, acC, dskip_l, prev, bl, s, name):
    t = bl * s
    nc = s // CHUNK
    ac4 = acT.reshape(bl * nc, N_PAIRS, 2, CHUNK)

    def body(dy_ref, x_ref, b_ref, c_ref, ac_ref, dtc_ref, acc_ref, dsk_ref, prev_ref,
             dx_ref, db_ref, dc_ref, dd_ref, dsk_out_ref, dp_ref):
        b = pl.program_id(0)
        cr = pl.program_id(1)
        g = pl.program_id(2)

        @pl.when(cr == 0)
        def _():
            for p in range(PAIRS_G):
                dp_ref[g * PAIRS_G + p] = jnp.zeros((N_STATE, 128), F32)

        @pl.when((b == 0) & (cr == 0) & (g == 0))
        def _():
            dsk_out_ref[...] = jnp.zeros(dsk_out_ref.shape, F32)

        bm = b_ref[...]
        cm = c_ref[...]
        cb = _dot(cm, bm, _NT)
        tri = _tri((CHUNK, CHUNK), lambda r, c: r >= c)
        lane = lax.broadcasted_iota(jnp.int32, (CHUNK, 128), 1)
        lrow = lax.broadcasted_iota(jnp.int32, (1, CHUNK), 1)
        krow = lax.broadcasted_iota(jnp.int32, (8, CHUNK), 0)
        dcb = jnp.zeros((CHUNK, CHUNK), F32)
        dc_acc = jnp.zeros((CHUNK, N_STATE), F32)
        db_acc = jnp.zeros((CHUNK, N_STATE), F32)
        for p in range(PAIRS_G):
            hp = g * PAIRS_G + p
            cs = slice(p * 128, (p + 1) * 128)
            x = x_ref[:, cs].astype(F32)
            left, dtl, acl, xdt, dks, aclast = _ssd_pair(x, dtc_ref[:, cs], acc_ref[:, cs], ac_ref[p], tri)
            dyv = dy_ref[:, cs].astype(F32)
            dy_b = dyv.astype(BF16)
            xdt_b = xdt.astype(BF16)
            st = prev_ref[p]
            st_b = st.astype(BF16)
            ea = jnp.exp(acl)
            ds = jnp.exp(aclast - acl)
            cdl = jnp.exp(aclast)
            xw = xdt * ds
            masks = [left, jnp.logical_not(left)]

            dsk_out_ref[hp] = dsk_out_ref[hp] + jnp.sum(dyv * x, axis=0, keepdims=True)

            yo = _dot(cm, st_b)
            dyo_b = (dyv * ea).astype(BF16)
            yoff_term = dyv * yo * ea
            dc_acc = dc_acc + _dot(dyo_b, st_b, _NT)
            dst = _dot(cm, dyo_b, _TN)
            dsv = dp_ref[hp]
            dsv_b = dsv.astype(BF16)
            dxw = _dot(bm, dsv_b)
            db_acc = db_acc + _dot(xw.astype(BF16), dsv_b, _NT)
            dxdt = dxw * ds
            qv = dxw * xw
            end_term = dsv * st * cdl
            dp_ref[hp] = dsv * cdl + dst

            cols = jnp.zeros((CHUNK, 128), F32)
            rows = []
            for e in range(2):
                m = cb * dks[e]
                dy_e = jnp.where(masks[e], dyv, 0.0).astype(BF16)
                dm = _dot(dy_e, xdt_b, _NT)
                w = dm * m
                dcb = dcb + dm * dks[e]
                dxdt = dxdt + jnp.where(masks[e], _dot(m.astype(BF16), dy_b, _TN), 0.0)
                dac_col = jnp.sum(w + jnp.where(masks[e], yoff_term - qv, 0.0), axis=1, keepdims=True)
                cols = jnp.where(lane == 2 + e, dac_col, cols)
                tail = jnp.sum(jnp.where(masks[e], qv + end_term, 0.0))
                rows.append(jnp.where(lrow == CHUNK - 1, tail, 0.0) - jnp.sum(w, axis=0, keepdims=True))
            dx_ref[:, cs] = (dxdt * dtl + dyv * dsk_ref[:, cs]).astype(BF16)
            ddt_l = dxdt * x
            for e in range(2):
                cols = jnp.where(lane == e, jnp.sum(jnp.where(masks[e], ddt_l, 0.0), axis=1, keepdims=True), cols)
            dd_ref[p] = cols.T[0:8] + jnp.where(krow == 2, rows[0], 0.0) + jnp.where(krow == 3, rows[1], 0.0)
        dcb_b = dcb.astype(BF16)
        dc_ref[...] = dc_acc + _dot(dcb_b, bm)
        db_ref[...] = db_acc + _dot(dcb_b, cm, _TN)

    sp = _ssd_specs(nc, True)
    dx, db, dc, dd, dsk = pl.pallas_call(
        body, name=name, grid=(bl, nc, N_GROUPS),
        in_specs=[sp["wide"], sp["wide"], sp["bmat"], sp["cmat"], sp["rows2"], sp["cols"], sp["cols"], sp["dskip"],
                  sp["state"]],
        out_specs=[sp["wide"], sp["narrow"], sp["narrow"], sp["rows8"],
                   pl.BlockSpec((N_PAIRS, 1, 128), lambda b, c, g: (0, 0, 0))],
        out_shape=[jax.ShapeDtypeStruct((t, SSM_W), BF16),
                   jax.ShapeDtypeStruct((t, N_GROUPS * N_STATE), F32),
                   jax.ShapeDtypeStruct((t, N_GROUPS * N_STATE), F32),
                   jax.ShapeDtypeStruct((bl * nc, N_PAIRS, 8, CHUNK), F32),
                   jax.ShapeDtypeStruct((N_PAIRS, 1, 128), F32)],
        scratch_shapes=[pltpu.VMEM((N_PAIRS, N_STATE, 128), F32)],
        compiler_params=_params("arbitrary", "arbitrary", "arbitrary"),
    )(dy, xa, xa, xa, ac4, dtC, acC, dskip_l, prev)
    ddtT = dd[:, :, 0:2, :].reshape(bl * nc, N_HEADS, CHUNK)
    dacT = dd[:, :, 2:4, :].reshape(bl * nc, N_HEADS, CHUNK)
    return dx, db, dc, ddtT, dacT, dsk.reshape(N_PAIRS, 128)


def gnorm_fwd(y, proj, w, name):
    t = y.shape[0]
    tr = min(256, t)
    zb = OFF_Z // SSM_W

    def body(y_ref, z_ref, w_ref, o_ref):
        z = z_ref[...].astype(F32)
        yg = y_ref[...].astype(F32) * z * _sigmoid(z)
        r = lax.rsqrt(jnp.mean(yg * yg, axis=-1, keepdims=True) + EPS)
        o_ref[...] = (yg * r * w_ref[...]).astype(BF16)

    return pl.pallas_call(
        body, name=name, grid=(t // tr,),
        in_specs=[pl.BlockSpec((tr, SSM_W), lambda i: (i, 0)), pl.BlockSpec((tr, SSM_W), lambda i: (i, zb)),
                  pl.BlockSpec((1, SSM_W), lambda i: (0, 0))],
        out_specs=pl.BlockSpec((tr, SSM_W), lambda i: (i, 0)),
        out_shape=jax.ShapeDtypeStruct((t, SSM_W), BF16),
        compiler_params=_params("parallel"),
    )(y, proj, w.reshape(1, SSM_W))


def gnorm_bwd(ds, y, proj, w, name):
    t = y.shape[0]
    tr = min(256, t)
    zb = OFF_Z // SSM_W

    def body(ds_ref, y_ref, z_ref, w_ref, dy_ref, dz_ref, dw_ref):
        z = z_ref[...].astype(F32)
        yv = y_ref[...].astype(F32)
        sg = _sigmoid(z)
        sz = z * sg
        yg = yv * sz
        r = lax.rsqrt(jnp.mean(yg * yg, axis=-1, keepdims=True) + EPS)
        xh = yg * r
        g = ds_ref[...].astype(F32)
        dxh = g * w_ref[...]
        dyg = r * (dxh - xh * jnp.mean(dxh * xh, axis=-1, keepdims=True))
        dy_ref[...] = (dyg * sz).astype(BF16)
        dz_ref[...] = (dyg * yv * sg * (1.0 + z * (1.0 - sg))).astype(BF16)
        part = jnp.sum(g * xh, axis=0, keepdims=True)

        @pl.when(pl.program_id(0) == 0)
        def _():
            dw_ref[...] = part

        @pl.when(pl.program_id(0) > 0)
        def _():
            dw_ref[...] += part

    return pl.pallas_call(
        body, name=name, grid=(t // tr,),
        in_specs=[pl.BlockSpec((tr, SSM_W), lambda i: (i, 0)), pl.BlockSpec((tr, SSM_W), lambda i: (i, 0)),
                  pl.BlockSpec((tr, SSM_W), lambda i: (i, zb)), pl.BlockSpec((1, SSM_W), lambda i: (0, 0))],
        out_specs=[pl.BlockSpec((tr, SSM_W), lambda i: (i, 0)), pl.BlockSpec((tr, SSM_W), lambda i: (i, 0)),
                   pl.BlockSpec((1, SSM_W), lambda i: (0, 0))],
        out_shape=[jax.ShapeDtypeStruct((t, SSM_W), BF16), jax.ShapeDtypeStruct((t, SSM_W), BF16),
                   jax.ShapeDtypeStruct((1, SSM_W), F32)],
        compiler_params=_params("arbitrary"),
    )(ds, y, proj, w.reshape(1, SSM_W))


def _pool_mixed(u, g, row):
    win = 2 << g
    acc = u
    for k in range(g + 1):
        acc = acc + _shift_down(acc, 1 << k, row)
    inv = 1.0 / jnp.minimum(row + 1, win).astype(F32)
    return acc * inv - u, inv


def pool_fwd(proj, pool_w, pool_scale, bl, s, name):
    t = bl * s

    def body(u_ref, g_ref, w_ref, sc_ref, o_ref):
        row = lax.broadcasted_iota(jnp.int32, (s, POOL_GD), 0)
        for g in range(POOL_G):
            cs = slice(g * POOL_GD, (g + 1) * POOL_GD)
            u = u_ref[:, cs].astype(F32)
            mixed, _ = _pool_mixed(u, g, row)
            pm = _dot(mixed.astype(BF16), w_ref[g])
            gate = g_ref[:, cs].astype(F32)
            o_ref[:, cs] = (pm * sc_ref[:, cs] * gate * _sigmoid(gate)).astype(BF16)

    return pl.pallas_call(
        body, name=name, grid=(bl,),
        in_specs=[pl.BlockSpec((s, POOL_W), lambda b: (b, OFF_PU // POOL_W)),
                  pl.BlockSpec((s, POOL_W), lambda b: (b, OFF_PG // POOL_W)),
                  pl.BlockSpec((POOL_G, POOL_GD, POOL_GD), lambda b: (0, 0, 0)),
                  pl.BlockSpec((1, POOL_W), lambda b: (0, 0))],
        out_specs=pl.BlockSpec((s, POOL_W), lambda b: (b, 0)),
        out_shape=jax.ShapeDtypeStruct((t, POOL_W), BF16),
        compiler_params=_params("parallel"),
    )(proj, proj, pool_w, pool_scale.reshape(1, POOL_W))


def pool_bwd(dp, proj, pool_w, pool_scale, bl, s, name):
    t = bl * s

    def body(dp_ref, u_ref, g_ref, w_ref, sc_ref, du_ref, dg_ref, dw_ref, dsc_ref):
        row = lax.broadcasted_iota(jnp.int32, (s, POOL_GD), 0)
        first = pl.program_id(0) == 0
        for g in range(POOL_G):
            cs = slice(g * POOL_GD, (g + 1) * POOL_GD)
            u = u_ref[:, cs].astype(F32)
            mixed, inv = _pool_mixed(u, g, row)
            mixed_b = mixed.astype(BF16)
            wg = w_ref[g]
            pm = _dot(mixed_b, wg)
            gate = g_ref[:, cs].astype(F32)
            sg = _sigmoid(gate)
            d = dp_ref[:, cs].astype(F32)
            sc = sc_ref[:, cs]
            dpm = (d * sc * gate * sg).astype(BF16)
            dg_ref[:, cs] = (d * pm * sc * sg * (1.0 + gate * (1.0 - sg))).astype(BF16)
            dsc = jnp.sum(d * pm * gate * sg, axis=0, keepdims=True)
            dwg = _dot(mixed_b, dpm, _TN)
            dmixed = _dot(dpm, wg, _NT)
            acc = dmixed * inv
            for k in range(g + 1):
                acc = acc + _shift_up(acc, 1 << k, row, s)
            du_ref[:, cs] = (acc - dmixed).astype(BF16)

            @pl.when(first)
            def _():
                dw_ref[g] = dwg
                dsc_ref[:, cs] = dsc

            @pl.when(jnp.logical_not(first))
            def _():
                dw_ref[g] = dw_ref[g] + dwg
                dsc_ref[:, cs] = dsc_ref[:, cs] + dsc

    return pl.pallas_call(
        body, name=name, grid=(bl,),
        in_specs=[pl.BlockSpec((s, POOL_W), lambda b: (b, 0)),
                  pl.BlockSpec((s, POOL_W), lambda b: (b, OFF_PU // POOL_W)),
                  pl.BlockSpec((s, POOL_W), lambda b: (b, OFF_PG // POOL_W)),
                  pl.BlockSpec((POOL_G, POOL_GD, POOL_GD), lambda b: (0, 0, 0)),
                  pl.BlockSpec((1, POOL_W), lambda b: (0, 0))],
        out_specs=[pl.BlockSpec((s, POOL_W), lambda b: (b, 0)), pl.BlockSpec((s, POOL_W), lambda b: (b, 0)),
                   pl.BlockSpec((POOL_G, POOL_GD, POOL_GD), lambda b: (0, 0, 0)),
                   pl.BlockSpec((1, POOL_W), lambda b: (0, 0))],
        out_shape=[jax.ShapeDtypeStruct((t, POOL_W), BF16), jax.ShapeDtypeStruct((t, POOL_W), BF16),
                   jax.ShapeDtypeStruct((POOL_G, POOL_GD, POOL_GD), F32), jax.ShapeDtypeStruct((1, POOL_W), F32)],
        compiler_params=_params("arbitrary"),
    )(dp, proj, proj, pool_w, pool_scale.reshape(1, POOL_W))


SB_SCALE = 64 ** -0.5


KB = 256


def _sb_block(qe, kj, mask, rr, upper):
    z = _dot(qe, kj, _NT).astype(BF16)
    lb = jnp.minimum(z, 0.0) - jnp.log(1.0 + jnp.exp(-jnp.abs(z)))
    lom = lb - z if mask is None else jnp.where(mask, lb - z, 0.0)
    later = _dot(lom, upper) + rr
    return lb, lom, later


def _sb_masks(i):
    lane = lax.broadcasted_iota(jnp.int32, (QB, 128), 1)
    row = lax.broadcasted_iota(jnp.int32, (2 * QB, KB), 0) % QB
    col = lax.broadcasted_iota(jnp.int32, (2 * QB, KB), 1)
    causal = lambda jb: col + (jb * KB - i * QB) < row
    return lane, lane < 64, causal


def _stack_heads(x, left):
    zero = jnp.zeros_like(x)
    return jnp.concatenate([jnp.where(left, x, zero), jnp.where(left, zero, x)], axis=0)


SB_GROUP = 4
SB_GW = SB_GROUP * 128


def sb_fwd(proj, bl, s, name, comm=None):
    t = bl * s
    nq = s // QB
    group, gw = SB_PAIRS, SB_W
    qb0, kb0, vb0, gb0 = OFF_QKV // gw, (OFF_QKV + SB_W) // gw, (OFF_QKV + 2 * SB_W) // gw, OFF_SBG // gw
    grid = (bl, SB_PAIRS // group, nq)
    x_in, x_args, x_out, x_shape, x_scratch, x_start, x_wait = _hosted_exchange(
        comm, grid, relay_at=(bl - 1, 0, (5 * nq) // 8))

    def body(*refs):
        q_ref, k_ref, v_ref, g_ref = refs[:4]
        og_ref, o_ref, r_ref = refs[4 + len(x_in):7 + len(x_in)]
        x_refs = refs[4:4 + len(x_in)] + refs[7 + len(x_in):]
        x_start(x_refs)
        i = pl.program_id(2)
        lane, left, causal = _sb_masks(i)
        upper = _tri((KB, KB), lambda r, c: r > c).astype(BF16)
        cols = [slice(p * 128, (p + 1) * 128) for p in range(group)]
        qcats = [_stack_heads(q_ref[:, cs] * SB_SCALE, left) for cs in cols]
        zero = qcats[0].astype(F32) * 0.0

        def block(jb, carry, diagonal):
            rows = pl.ds(pl.multiple_of(jb * KB, KB), KB)
            mask = causal(jb) if diagonal else None
            out = []
            for p, cs in enumerate(cols):
                acc, rr, rt = carry[p]
                lb, lom, later = _sb_block(qcats[p], k_ref[rows, cs], mask, rr, upper)
                att = jnp.exp(lb + later)
                if diagonal:
                    att = jnp.where(mask, att, 0.0)
                acc = acc + _dot(att.astype(BF16), v_ref[rows, cs])
                rt = jnp.where(lane == jb, rr[:QB], jnp.where(lane == 8 + jb, rr[QB:], rt))
                out.append((acc, rr + jnp.sum(lom, axis=1, keepdims=True, dtype=F32), rt))
            return tuple(out)

        carry = block(i, tuple((zero, zero[:, :1], zero[:QB]) for _ in cols), True)
        carry = lax.fori_loop(0, i, lambda jj, c: block(i - 1 - jj, c, False), carry)
        for p, cs in enumerate(cols):
            acc, _, rtile = carry[p]
            o = jnp.where(left, acc[:QB], acc[QB:])
            gate = g_ref[:, cs].astype(F32)
            o_ref[:, cs] = o.astype(BF16)
            og_ref[:, cs] = (o * gate * _sigmoid(gate)).astype(BF16)
            r_ref[p] = rtile
        x_wait(x_refs)

    rowblk = lambda b, g, i: (b * nq + i, g)
    return pl.pallas_call(
        body, name=name, grid=grid,
        in_specs=[pl.BlockSpec((QB, gw), lambda b, g, i: (b * nq + i, qb0 + g)),
                  pl.BlockSpec((s, gw), lambda b, g, i: (b, kb0 + g)),
                  pl.BlockSpec((s, gw), lambda b, g, i: (b, vb0 + g)),
                  pl.BlockSpec((QB, gw), lambda b, g, i: (b * nq + i, gb0 + g))] + x_in,
        out_specs=[pl.BlockSpec((QB, gw), rowblk), pl.BlockSpec((QB, gw), rowblk),
                   pl.BlockSpec((None, group, QB, 128), lambda b, g, i: (b * nq + i, g, 0, 0))] + x_out,
        out_shape=[jax.ShapeDtypeStruct((t, SB_W), BF16), jax.ShapeDtypeStruct((t, SB_W), BF16),
                   jax.ShapeDtypeStruct((bl * nq, SB_PAIRS, QB, 128), F32)] + x_shape,
        scratch_shapes=x_scratch,
        compiler_params=_params("arbitrary", "arbitrary", "arbitrary"),
    )(proj, proj, proj, proj, *x_args)


def sb_bwd(dsb, o, rsave, proj, bl, s, name, comm=None):
    t = bl * s
    nq = s // QB
    qb0, kb0, vb0, gb0 = OFF_QKV // SB_GW, (OFF_QKV + SB_W) // SB_GW, (OFF_QKV + 2 * SB_W) // SB_GW, OFF_SBG // SB_GW
    grid = (bl, SB_PAIRS // SB_GROUP, nq)
    x_in, x_args, x_out, x_shape, x_scratch, x_start, x_wait = _hosted_exchange(
        comm, grid, relay_at=(bl - 1, SB_PAIRS // SB_GROUP - 1, 0))

    def body(*refs):
        n = len(x_in)
        d_ref, o_ref, r_ref, q_ref, k_ref, v_ref, g_ref = refs[:7]
        dq_ref, dk_ref, dv_ref, dg_ref = refs[7 + n:11 + n]
        dk_acc, dv_acc = refs[11 + 2 * n:13 + 2 * n]
        x_refs = refs[7:7 + n] + refs[11 + n:11 + 2 * n] + refs[13 + 2 * n:]
        x_start(x_refs)
        i = pl.program_id(2)

        @pl.when(i == 0)
        def _():
            dk_acc[...] = jnp.zeros(dk_acc.shape, F32)
            dv_acc[...] = jnp.zeros(dv_acc.shape, F32)

        lane, left, causal = _sb_masks(i)
        upper = _tri((KB, KB), lambda r, c: r > c).astype(BF16)
        lower = _tri((KB, KB), lambda r, c: r < c).astype(BF16)
        cols = [slice(p * 128, (p + 1) * 128) for p in range(SB_GROUP)]
        qcats, docats = [], []
        for cs in cols:
            gate = g_ref[:, cs].astype(F32)
            sg = _sigmoid(gate)
            d = d_ref[:, cs].astype(F32)
            dg_ref[:, cs] = (d * o_ref[:, cs].astype(F32) * sg * (1.0 + gate * (1.0 - sg))).astype(BF16)
            docats.append(_stack_heads((d * gate * sg).astype(BF16), left))
            qcats.append(_stack_heads(q_ref[:, cs] * SB_SCALE, left))
        qcat_ts = [qc.astype(F32).T.astype(BF16) for qc in qcats]
        docat_ts = [dc.astype(F32).T.astype(BF16) for dc in docats]
        zero = qcats[0].astype(F32) * 0.0

        def block(jb, carry, diagonal):
            rows = pl.ds(pl.multiple_of(jb * KB, KB), KB)
            mask = causal(jb) if diagonal else None
            out = []
            for p, cs in enumerate(cols):
                dq, gcar = carry[p]
                kj = k_ref[rows, cs]
                vj = v_ref[rows, cs]
                rtile = r_ref[p]
                rr = jnp.concatenate(
                    [jnp.sum(jnp.where(lane == jb, rtile, 0.0), axis=1, keepdims=True),
                     jnp.sum(jnp.where(lane == 8 + jb, rtile, 0.0), axis=1, keepdims=True)], axis=0)
                lb, lom, later = _sb_block(qcats[p], kj, mask, rr, upper)
                att = jnp.exp(lb + later)
                if diagonal:
                    att = jnp.where(mask, att, 0.0)
                de = att * _dot(docats[p], vj, _NT)
                gpre = _dot(de.astype(BF16), lower) + gcar
                sig = jnp.exp(lb)
                dz = de * (1.0 - sig) - gpre * sig
                if diagonal:
                    dz = jnp.where(mask, dz, 0.0)
                dz = dz.astype(BF16)
                dk_acc[jb, cs, :] = dk_acc[jb, cs, :] + _dot(qcat_ts[p], dz)
                dv_acc[jb, cs, :] = dv_acc[jb, cs, :] + _dot(docat_ts[p], att.astype(BF16))
                out.append((dq + _dot(dz, kj), gcar + jnp.sum(de, axis=1, keepdims=True)))
            return tuple(out)

        carry = lax.fori_loop(0, i, lambda jb, c: block(jb, c, False), tuple((zero, zero[:, :1]) for _ in cols))
        carry = block(i, carry, True)
        for p, cs in enumerate(cols):
            dq = carry[p][0]
            dq_ref[:, cs] = (jnp.where(left, dq[:QB], dq[QB:]) * SB_SCALE).astype(BF16)

        @pl.when(i == nq - 1)
        def _():
            for kb in range(s // KB):
                for cs in cols:
                    dk_ref[kb * KB:(kb + 1) * KB, cs] = dk_acc[kb, cs, :].T.astype(BF16)
                    dv_ref[kb * KB:(kb + 1) * KB, cs] = dv_acc[kb, cs, :].T.astype(BF16)

        x_wait(x_refs)

    rowblk = lambda b, g, i: (b * nq + i, g)
    seqblk = lambda b, g, i: (b, g)
    return pl.pallas_call(
        body, name=name, grid=grid,
        in_specs=[pl.BlockSpec((QB, SB_GW), rowblk), pl.BlockSpec((QB, SB_GW), rowblk),
                  pl.BlockSpec((None, SB_GROUP, QB, 128), lambda b, g, i: (b * nq + i, g, 0, 0)),
                  pl.BlockSpec((QB, SB_GW), lambda b, g, i: (b * nq + i, qb0 + g)),
                  pl.BlockSpec((s, SB_GW), lambda b, g, i: (b, kb0 + g)),
                  pl.BlockSpec((s, SB_GW), lambda b, g, i: (b, vb0 + g)),
                  pl.BlockSpec((QB, SB_GW), lambda b, g, i: (b * nq + i, gb0 + g))] + x_in,
        out_specs=[pl.BlockSpec((QB, SB_GW), rowblk), pl.BlockSpec((s, SB_GW), seqblk),
                   pl.BlockSpec((s, SB_GW), seqblk), pl.BlockSpec((QB, SB_GW), rowblk)] + x_out,
        out_shape=[jax.ShapeDtypeStruct((t, SB_W), BF16)] * 4 + x_shape,
        scratch_shapes=[pltpu.VMEM((s // KB, SB_GW, KB), F32), pltpu.VMEM((s // KB, SB_GW, KB), F32)] + x_scratch,
        compiler_params=_params("arbitrary", "arbitrary", "arbitrary"),
    )(dsb, o, rsave, proj, proj, proj, proj, *x_args)


def merge_fwd(proj, ys, yp, yb, name):
    t = ys.shape[0]
    tr = min(512, t)

    def body(m_ref, ys_ref, yp_ref, yb_ref, o_ref):
        acc = jnp.zeros((tr, D), F32)
        for k, ref in enumerate((ys_ref, yp_ref, yb_ref)):
            acc = acc + _sigmoid(m_ref[:, k * D:(k + 1) * D].astype(F32)) * ref[...].astype(F32)
        o_ref[...] = acc.astype(BF16)

    rowblk = pl.BlockSpec((tr, D), lambda i: (i, 0))
    return pl.pallas_call(
        body, name=name, grid=(t // tr,),
        in_specs=[pl.BlockSpec((tr, 3 * D), lambda i: (i, 0)), rowblk, rowblk, rowblk],
        out_specs=rowblk,
        out_shape=jax.ShapeDtypeStruct((t, D), BF16),
        compiler_params=_params("parallel"),
    )(proj, ys, yp, yb)


def merge_bwd(dm, proj, ys, yp, yb, name):
    t = ys.shape[0]
    tr = min(512, t)

    def body(dm_ref, m_ref, ys_ref, yp_ref, yb_ref, d0_ref, d1_ref, d2_ref, dl_ref):
        dmv = dm_ref[...].astype(F32)
        for k, (ref, dref) in enumerate(((ys_ref, d0_ref), (yp_ref, d1_ref), (yb_ref, d2_ref))):
            g = _sigmoid(m_ref[:, k * D:(k + 1) * D].astype(F32))
            dref[...] = (g * dmv).astype(BF16)
            dl_ref[:, k * D:(k + 1) * D] = (dmv * ref[...].astype(F32) * g * (1.0 - g)).astype(BF16)

    rowblk = pl.BlockSpec((tr, D), lambda i: (i, 0))
    wide = pl.BlockSpec((tr, 3 * D), lambda i: (i, 0))
    return pl.pallas_call(
        body, name=name, grid=(t // tr,),
        in_specs=[rowblk, wide, rowblk, rowblk, rowblk],
        out_specs=[rowblk, rowblk, rowblk, wide],
        out_shape=[jax.ShapeDtypeStruct((t, D), BF16)] * 3 + [jax.ShapeDtypeStruct((t, 3 * D), BF16)],
        compiler_params=_params("parallel"),
    )(dm, proj, ys, yp, yb)


def layer_fwd(x, lw, bl, s, tag, comm=None):
    t = bl * s
    h = rmsnorm_fwd(x, lw["norm_w"], f"norm_fwd{tag}")
    proj = matmul(h, lw["w_in"], "nt", BF16, f"in_proj{tag}", tn=2048)
    xa = conv_fwd(proj, lw["conv_w"], lw["conv_b"], bl, s, f"conv_fwd{tag}")
    dtT, acT, dtC, acC = dt_fwd(proj, lw["dt_bias"], lw["a_log"], t, f"dt_fwd{tag}")
    dskip_l = jnp.repeat(lw["d_skip"], 64).reshape(1, SSM_W)
    y, prev = ssd_fwd(xa, acT, dtC, acC, dskip_l, bl, s, f"ssd_fwd{tag}")
    s_out = gnorm_fwd(y, proj, lw["ssm_norm_w"], f"gnorm_fwd{tag}")
    p_out = pool_fwd(proj, lw["pool_w"], lw["pool_scale"], bl, s, f"pool_fwd{tag}")
    sb_out, sb_o, sb_r, *carried = sb_fwd(proj, bl, s, f"sb_fwd{tag}", comm)
    ys = matmul(s_out, lw["w_proj_ssm"], "nn", BF16, f"proj_ssm{tag}")
    yp = matmul(p_out, lw["w_proj_pool"], "nn", BF16, f"proj_pool{tag}")
    yb = matmul(sb_out, lw["w_proj_sb"], "nn", BF16, f"proj_sb{tag}")
    merged = merge_fwd(proj, ys, yp, yb, f"merge_fwd{tag}")
    x_next = matmul(merged, lw["w_out"], "nn", F32, f"out_proj{tag}", residual=x)
    saved = dict(x=x, h=h, proj=proj, xa=xa, dtT=dtT, acT=acT, dtC=dtC, acC=acC, y=y, prev=prev, s_out=s_out, p_out=p_out,
                 sb_out=sb_out, sb_o=sb_o, sb_r=sb_r, ys=ys, yp=yp, yb=yb, merged=merged)
    return x_next, saved, (carried[0] if carried else None)


def layer_bwd(dx, dx_b, lw, sv, bl, s, tag, comm, own_slabs):
    t = bl * s
    g = {}
    dmerged = matmul(dx_b, lw["w_out"], "nt", BF16, f"d_merged{tag}")
    g["w_out"] = matmul(sv["merged"], dx_b, "tn", BF16, f"dw_out{tag}")
    dys, dyp, dyb, dlogit = merge_bwd(dmerged, sv["proj"], sv["ys"], sv["yp"], sv["yb"], f"merge_bwd{tag}")
    ds_out = matmul(dys, lw["w_proj_ssm"], "nt", BF16, f"d_sout{tag}")
    g["w_proj_ssm"] = matmul(sv["s_out"], dys, "tn", BF16, f"dw_proj_ssm{tag}")
    dp_out = matmul(dyp, lw["w_proj_pool"], "nt", BF16, f"d_pout{tag}")
    g["w_proj_pool"] = matmul(sv["p_out"], dyp, "tn", BF16, f"dw_proj_pool{tag}")
    dsb_out = matmul(dyb, lw["w_proj_sb"], "nt", BF16, f"d_sbout{tag}")
    g["w_proj_sb"] = matmul(sv["sb_out"], dyb, "tn", BF16, f"dw_proj_sb{tag}")
    dy, dz, dnw = gnorm_bwd(ds_out, sv["y"], sv["proj"], lw["ssm_norm_w"], f"gnorm_bwd{tag}")
    g["ssm_norm_w"] = dnw[0]
    dskip_l = jnp.repeat(lw["d_skip"], 64).reshape(1, SSM_W)
    dxs, db, dc, ddtT, dacT, dsk = ssd_bwd(dy, sv["xa"], sv["acT"], sv["dtC"], sv["acC"], dskip_l, sv["prev"], bl, s,
                                           f"ssd_bwd{tag}")
    g["d_skip"] = jnp.sum(dsk.reshape(N_HEADS, 64), axis=1)
    ddt_raw, da, dbias = dt_bwd(ddtT, dacT, sv["dtT"], sv["proj"], lw["dt_bias"], lw["a_log"], t, f"dt_bwd{tag}")
    g["a_log"] = da * (-jnp.exp(lw["a_log"]))
    g["dt_bias"] = dbias
    dxa = jnp.concatenate([dxs, db.astype(BF16), dc.astype(BF16)], axis=1)
    dxbc, dcw, dcb = conv_bwd(dxa, sv["proj"], lw["conv_w"], lw["conv_b"], bl, s, f"conv_bwd{tag}")
    g["conv_w"] = dcw
    g["conv_b"] = dcb
    dpu, dpg, dpw, dpsc = pool_bwd(dp_out, sv["proj"], lw["pool_w"], lw["pool_scale"], bl, s, f"pool_bwd{tag}")
    g["pool_w"] = dpw
    g["pool_scale"] = dpsc[0]
    dq, dk, dv, dsbg, *carried = sb_bwd(dsb_out, sv["sb_o"], sv["sb_r"], sv["proj"], bl, s, f"sb_bwd{tag}", comm)
    dproj = concat_columns([dlogit, dsbg, dpu, dpg, dz, dq, dk, dv, dxbc, ddt_raw], PC, f"d_proj{tag}")
    g["w_in"] = matmul(dproj, sv["h"], "tn", BF16, f"dw_in{tag}", tk=2048)
    own_comm = own_slabs(g) if own_slabs is not None else None
    dh = matmul(dproj, lw["w_in"], "nn", F32, f"d_h{tag}", tk=2048, comm=own_comm)
    dh, own_got = dh if own_comm is not None else (dh, None)
    dx_in, dx_in_b, dnorm = rmsnorm_bwd(dh, sv["x"], lw["norm_w"], dx, f"norm_bwd{tag}")
    g["norm_w"] = dnorm[0]
    return dx_in, dx_in_b, g, (carried[0] if carried else None), (own_comm[0] if own_comm else None), own_got


def concat_columns(parts, width, name):
    t = parts[0].shape[0]
    tr = min(256, t)
    widths = [p.shape[1] for p in parts]
    used = sum(widths)

    def body(*refs):
        o_ref = refs[-1]
        off = 0
        for ref, w in zip(refs[:-1], widths):
            o_ref[:, off:off + w] = ref[...]
            off += w
        if width > used:
            o_ref[:, used:] = jnp.zeros((tr, width - used), BF16)

    return pl.pallas_call(
        body, name=name, grid=(t // tr,),
        in_specs=[pl.BlockSpec((tr, w), lambda i: (i, 0)) for w in widths],
        out_specs=pl.BlockSpec((tr, width), lambda i: (i, 0)),
        out_shape=jax.ShapeDtypeStruct((t, width), BF16),
        compiler_params=_params("parallel"),
    )(*parts)


_PAD_PIECES = ((10784, 3072), (9760, 1024), (4640, 1024), (5664, 1024), (0, 2048), (6688, 3072), (2048, 2560), (4608, 32))
_UNPAD_PIECES = ((OFF_Z, 2048), (OFF_XBC, 2560), (OFF_DT, 32), (OFF_PU, 1024), (OFF_PG, 1024), (OFF_QKV, 3072),
                 (OFF_SBG, 1024), (OFF_MERGE, 3072))


def pad_rows(wt):
    pieces = [wt[o:o + n] for o, n in _PAD_PIECES]
    return jnp.concatenate(pieces + [jnp.zeros((PC - IN_COLS, wt.shape[1]), wt.dtype)], axis=0)


def unpad_rows(wp):
    return jnp.concatenate([wp[o:o + n] for o, n in _UNPAD_PIECES], axis=0)


MESH = pl.DeviceIdType.MESH
ANY = pl.BlockSpec(memory_space=pl.ANY)


def _coords():
    return lax.axis_index("x"), lax.axis_index("y"), lax.axis_index("c")


def _peer(p):
    x, y, c = _coords()
    return (1 - x if p & 4 else x, 1 - y if p & 2 else y, 1 - c if p & 1 else c)


def _flat(pos):
    return 4 * pos[0] + 2 * pos[1] + pos[2]


def _chip(pos):
    return 2 * pos[0] + pos[1]


def _exchange_copies(v_ref, out_ref, send_sems, recv_sems, local_sem, mode):
    x, y, c = _coords()
    me = _flat((x, y, c))
    sibling = (x, y, 1 - c)
    chips = [(1 - x if j & 2 else x, 1 - y if j & 1 else y) for j in range(1, 4)]

    def copy(k, src, landing, to):
        return pltpu.make_async_remote_copy(src_ref=src, dst_ref=out_ref.at[landing], send_sem=send_sems.at[k],
                                            recv_sem=recv_sems.at[k], device_id=to, device_id_type=MESH)

    if mode == "direct":
        local = pltpu.make_async_copy(v_ref, out_ref.at[me], local_sem)
        first = [copy(p - 1, v_ref, me, _peer(p)) for p in range(1, N_DEV)]
        last = [copy(p - 1, v_ref, _flat(_peer(p)), _peer(p)) for p in range(1, N_DEV)]
        return local, first, [], last
    if mode == "gather":
        local = pltpu.make_async_copy(v_ref, out_ref.at[me], local_sem)
        first = [copy(0, v_ref, me, sibling)] + [copy(1 + j, v_ref, me, (*ch, c)) for j, ch in enumerate(chips)]
        relay = [(copy(1 + j, v_ref, _flat((*ch, c)), (*ch, c)),
                  copy(4 + j, out_ref.at[_flat((*ch, c))], _flat((*ch, c)), sibling)) for j, ch in enumerate(chips)]
        last = [copy(0, v_ref, _flat(sibling), sibling)] + [
            copy(4 + j, v_ref, _flat((*ch, 1 - c)), sibling) for j, ch in enumerate(chips)]
        return local, first, relay, last
    assert mode == "chips"
    mine = _chip((x, y))
    local = pltpu.make_async_copy(v_ref.at[mine], out_ref.at[mine], local_sem)
    first = [copy(j, v_ref.at[_chip(ch)], mine, (*ch, c)) for j, ch in enumerate(chips)]
    last = [copy(j, v_ref.at[mine], _chip(ch), (*ch, c)) for j, ch in enumerate(chips)]
    return local, first, [], last


def _swap_copies(s0_ref, s1_ref, out_ref, send_sems, recv_sems, local_sem):
    x, y, c = _coords()

    def four(src_ref):
        return [pltpu.make_async_remote_copy(src_ref=src_ref.at[j], dst_ref=out_ref.at[j], send_sem=send_sems.at[j],
                                             recv_sem=recv_sems.at[j], device_id=(x, y, 1 - c), device_id_type=MESH)
                for j in range(4)]

    return c, four(s1_ref), four(s0_ref)


def _exchange_start(*refs_and_mode):
    if refs_and_mode[-1] == "swap":
        c, from_core0, from_core1 = _swap_copies(*refs_and_mode[:-1])
        for core, copies in ((0, from_core0), (1, from_core1)):
            @pl.when(c == core)
            def _():
                for cp in copies:
                    cp.start()
        return
    local, first, _, _ = _exchange_copies(*refs_and_mode)
    local.start()
    for cp in first:
        cp.start()


def _exchange_relay(*refs_and_mode):
    if refs_and_mode[-1] == "swap":
        return
    for arrival, onward in _exchange_copies(*refs_and_mode)[2]:
        arrival.wait_recv()
        onward.start()


def _exchange_finish(*refs_and_mode):
    if refs_and_mode[-1] == "swap":
        _, four, _ = _swap_copies(*refs_and_mode[:-1])
        for cp in four:
            cp.wait_recv()
        for cp in four:
            cp.wait_send()
        return
    local, first, relay, last = _exchange_copies(*refs_and_mode)
    for cp in last:
        cp.wait_recv()
    for cp in first + [onward for _, onward in relay]:
        cp.wait_send()
    local.wait()


def _exchange_shape(v, mode):
    if mode == "swap":
        return jax.ShapeDtypeStruct(tuple(v[0].shape), v[0].dtype)
    return jax.ShapeDtypeStruct(tuple(v.shape) if mode == "chips" else (N_DEV,) + tuple(v.shape), v.dtype)


def _exchange_sems():
    return [pltpu.SemaphoreType.DMA((N_DEV - 1,)), pltpu.SemaphoreType.DMA((N_DEV - 1,)), pltpu.SemaphoreType.DMA]


def exchange(v, mode, name):
    def body(*refs):
        _exchange_start(*refs, mode)
        _exchange_relay(*refs, mode)
        _exchange_finish(*refs, mode)

    args = list(v) if mode == "swap" else [v]
    return pl.pallas_call(
        body, name=name,
        in_specs=[ANY] * len(args), out_specs=ANY,
        out_shape=_exchange_shape(v, mode),
        scratch_shapes=_exchange_sems(),
    )(*args)


def _hosted_exchange(comm, grid, relay_at):
    if comm is None:
        return [], [], [], [], [], (lambda refs: None), (lambda refs: None)
    v, mode = comm

    def at(step):
        cond = None
        for axis, want in enumerate(step):
            term = pl.program_id(axis) == want
            cond = term if cond is None else jnp.logical_and(cond, term)
        return cond

    def start(refs):
        @pl.when(at([0] * len(grid)))
        def _():
            _exchange_start(*refs, mode)

        if mode == "gather":
            @pl.when(at(relay_at))
            def _():
                _exchange_relay(*refs, mode)

    def wait(refs):
        @pl.when(at([n - 1 for n in grid]))
        def _():
            _exchange_finish(*refs, mode)

    args = list(v) if mode == "swap" else [v]
    return [ANY] * len(args), args, [ANY], [_exchange_shape(v, mode)], _exchange_sems(), start, wait


def pair_sum(slabs, got, name):
    _, r, c = got.shape
    tr = r // 4 if r % 64 == 0 else r

    def body(s0_ref, s1_ref, got_ref, o_ref):
        mine = jnp.where(lax.axis_index("c") == 0, s0_ref[...].astype(F32), s1_ref[...].astype(F32))
        o_ref[...] = (mine + got_ref[...].astype(F32)).astype(BF16)

    blk = pl.BlockSpec((None, tr, c), lambda j, i: (j, i, 0))
    return pl.pallas_call(
        body, name=name, grid=(4, r // tr),
        in_specs=[blk, blk, blk], out_specs=blk,
        out_shape=jax.ShapeDtypeStruct(got.shape, BF16),
        compiler_params=_params("parallel", "parallel"),
    )(*slabs, got)


def sum_slabs(v, name):
    n, r, c = v.shape
    tr = 128 if r % 128 == 0 else r

    def body(v_ref, o_ref):
        acc = v_ref[0].astype(F32)
        for k in range(1, n):
            acc = acc + v_ref[k].astype(F32)
        o_ref[...] = acc

    return pl.pallas_call(
        body, name=name, grid=(r // tr,),
        in_specs=[pl.BlockSpec((n, tr, c), lambda i: (0, i, 0))],
        out_specs=pl.BlockSpec((tr, c), lambda i: (i, 0)),
        out_shape=jax.ShapeDtypeStruct((r, c), F32),
        compiler_params=_params("parallel"),
    )(v)


def adamw(w, g, m, v, name):
    r, c = w.shape
    tr = next((cand for cand in (256, 128, 64, 32, 16, 8) if r % cand == 0), r)

    def body(w_ref, g_ref, m_ref, v_ref, d_ref, mo_ref, vo_ref):
        gv = g_ref[...]
        mn = ADAM_B1 * m_ref[...] + (1.0 - ADAM_B1) * gv
        vn = ADAM_B2 * v_ref[...] + (1.0 - ADAM_B2) * (gv * gv)
        m_hat = mn / (1.0 - ADAM_B1 ** ADAM_STEP)
        v_hat = vn / (1.0 - ADAM_B2 ** ADAM_STEP)
        d_ref[...] = -ADAM_LR * (m_hat / (jnp.sqrt(v_hat) + ADAM_EPS) + ADAM_WD * w_ref[...])
        mo_ref[...] = mn
        vo_ref[...] = vn

    blk = pl.BlockSpec((tr, c), lambda i: (i, 0))
    return pl.pallas_call(
        body, name=name, grid=(r // tr,),
        in_specs=[blk] * 4, out_specs=[blk] * 3,
        out_shape=[jax.ShapeDtypeStruct((r, c), F32)] * 3,
        compiler_params=_params("parallel"),
    )(w, g, m, v)


BIG = ("w_proj_ssm", "w_proj_pool", "w_proj_sb", "w_out", "pool_w", "w_in")
SHARD_IN = IN_COLS // N_DEV
BIG_ROWS = {"w_proj_ssm": SSM_W // N_DEV, "w_proj_pool": POOL_W // N_DEV, "w_proj_sb": SB_W // N_DEV,
            "w_out": D // N_DEV, "pool_w": POOL_G * (POOL_GD // N_DEV) * POOL_GD // D, "w_in": SHARD_IN}
PACK_C = D
PACK_R = 2432

REPLICATED = ("norm_w", "conv_b", "dt_bias", "a_log", "d_skip", "ssm_norm_w", "pool_scale")
WEIGHTS = ("norm_w", "w_in", "conv_w", "conv_b", "dt_bias", "a_log", "d_skip", "ssm_norm_w", "pool_w",
           "pool_scale", "w_proj_ssm", "w_proj_pool", "w_proj_sb", "w_out", "final_norm_w")


def _size(shape):
    n = 1
    for d in shape:
        n *= d
    return n


def _pad_flat(flat, n):
    return jnp.concatenate([flat, jnp.zeros((n - flat.shape[0],), flat.dtype)])


def _row_offsets():
    offs, off = {}, 0
    for n in BIG:
        offs[n] = off
        off += BIG_ROWS[n]
    return offs, off


def pack_shards(parts):
    rows = [parts[n].reshape(BIG_ROWS[n], PACK_C) for n in BIG]
    rows[-1] = jnp.pad(rows[-1], ((0, PACK_R - _row_offsets()[1]), (0, 0)))
    return jnp.concatenate(rows, axis=0)


def unpack_shards(packed):
    offs, _ = _row_offsets()
    out = {}
    for n in BIG:
        seg = packed[offs[n]:offs[n] + BIG_ROWS[n]]
        if n == "w_in":
            out[n] = seg.T
        elif n == "pool_w":
            out[n] = seg.reshape(POOL_G, POOL_GD // N_DEV, POOL_GD)
        else:
            out[n] = seg
    return out


def unpack_gathered(g):
    offs, _ = _row_offsets()
    out = {}
    for n in BIG:
        seg = g[:, offs[n]:offs[n] + BIG_ROWS[n], :]
        if n == "w_in":
            out[n] = pad_rows(seg.reshape(IN_COLS, D))
        elif n == "pool_w":
            out[n] = seg.reshape(N_DEV, POOL_G, POOL_GD // N_DEV, POOL_GD).transpose(1, 0, 2, 3).reshape(
                POOL_G, POOL_GD, POOL_GD)
        else:
            out[n] = seg.reshape(N_DEV * BIG_ROWS[n], D)
    return out


def pack_slabs(g):
    segs = []
    for n in BIG:
        if n == "w_in":
            w = unpad_rows(g[n])
        elif n == "pool_w":
            w = g[n].reshape(POOL_G, N_DEV, POOL_GD // N_DEV, POOL_GD).transpose(1, 0, 2, 3)
        else:
            w = g[n]
        segs.append(w.reshape(N_DEV // 2, 2, BIG_ROWS[n], PACK_C).astype(BF16))
    segs[-1] = jnp.pad(segs[-1], ((0, 0), (0, 0), (0, PACK_R - _row_offsets()[1]), (0, 0)))
    return tuple(jnp.concatenate([seg[:, core] for seg in segs], axis=1) for core in range(2))


SMALL_ROWS = 544


def pack_small(vals):
    flat = jnp.concatenate([v.reshape(-1) for v in vals])
    return _pad_flat(flat, SMALL_ROWS * 128).reshape(SMALL_ROWS, 128)


def unpack_small(packed, shapes):
    flat = packed.reshape(-1)
    out, off = [], 0
    for shp in shapes:
        out.append(flat[off:off + _size(shp)].reshape(shp))
        off += _size(shp)
    return out


def kernel(x, norm_w, w_in, conv_w, conv_b, dt_bias, a_log, d_skip, ssm_norm_w, pool_w, pool_scale, w_proj_ssm, w_proj_pool, w_proj_sb, w_out, final_norm_w, loss_target, m_norm_w, m_w_in, m_conv_w, m_conv_b, m_dt_bias, m_a_log, m_d_skip, m_ssm_norm_w, m_pool_w, m_pool_scale, m_w_proj_ssm, m_w_proj_pool, m_w_proj_sb, m_w_out, m_final_norm_w, v_norm_w, v_w_in, v_conv_w, v_conv_b, v_dt_bias, v_a_log, v_d_skip, v_ssm_norm_w, v_pool_w, v_pool_scale, v_w_proj_ssm, v_w_proj_pool, v_w_proj_sb, v_w_out, v_final_norm_w):
    wts = dict(norm_w=norm_w, w_in=w_in, conv_w=conv_w, conv_b=conv_b, dt_bias=dt_bias, a_log=a_log, d_skip=d_skip,
               ssm_norm_w=ssm_norm_w, pool_w=pool_w, pool_scale=pool_scale, w_proj_ssm=w_proj_ssm,
               w_proj_pool=w_proj_pool, w_proj_sb=w_proj_sb, w_out=w_out, final_norm_w=final_norm_w)
    mom = dict(norm_w=m_norm_w, w_in=m_w_in, conv_w=m_conv_w, conv_b=m_conv_b, dt_bias=m_dt_bias, a_log=m_a_log,
               d_skip=m_d_skip, ssm_norm_w=m_ssm_norm_w, pool_w=m_pool_w, pool_scale=m_pool_scale,
               w_proj_ssm=m_w_proj_ssm, w_proj_pool=m_w_proj_pool, w_proj_sb=m_w_proj_sb, w_out=m_w_out,
               final_norm_w=m_final_norm_w)
    var = dict(norm_w=v_norm_w, w_in=v_w_in, conv_w=v_conv_w, conv_b=v_conv_b, dt_bias=v_dt_bias, a_log=v_a_log,
               d_skip=v_d_skip, ssm_norm_w=v_ssm_norm_w, pool_w=v_pool_w, pool_scale=v_pool_scale,
               w_proj_ssm=v_w_proj_ssm, w_proj_pool=v_w_proj_pool, w_proj_sb=v_w_proj_sb, w_out=v_w_out,
               final_norm_w=v_final_norm_w)
    bl, s, _ = x.shape
    t = bl * s
    me = _flat(_coords())

    cw = exchange(conv_w.reshape(40, 128), "direct", "gather_conv_w")
    conv_w_full = cw.reshape(N_DEV, DEPTH, CONV_K, CONV_CH // N_DEV).transpose(1, 2, 0, 3).reshape(
        DEPTH, CONV_K, CONV_CH)

    xc = x.reshape(t, D)
    layer_w, saved = [], []
    packed = [pack_shards({n: (wts[n][l].T if n == "w_in" else wts[n][l]).astype(BF16) for n in BIG})
              for l in range(DEPTH)]
    gathered = exchange(packed[0], "gather", "gather_w0")
    for l in range(DEPTH):
        lw = unpack_gathered(gathered)
        for n in REPLICATED:
            lw[n] = wts[n][l]
        lw["conv_w"] = conv_w_full[l]
        xc, sv, gathered = layer_fwd(xc, lw, bl, s, f"_l{l}", (packed[l + 1], "gather") if l + 1 < DEPTH else None)
        layer_w.append(lw)
        saved.append(sv)

    loss_part, dx, dx_b, dfinal = final_loss(xc, final_norm_w, loss_target.reshape(t, D), "final_loss")
    loss = lax.psum(loss_part[0, 0], ("x", "y", "c"))

    grads = [None] * DEPTH
    big_sum = [None] * DEPTH
    def last_layer_slabs(g):
        slabs = pack_slabs(g)
        return pair_sum(slabs, exchange(slabs, "swap", "pair_swap0"), "pair_sum0"), "chips"

    chip_sums = None
    for l in reversed(range(DEPTH)):
        dx, dx_b, g, got, own_sent, own_got = layer_bwd(
            dx, dx_b, layer_w[l], saved[l], bl, s, f"_l{l}", (chip_sums, "chips") if chip_sums is not None else None,
            last_layer_slabs if l == 0 else (lambda g: (pack_slabs(g), "swap")))
        if got is not None:
            big_sum[l + 1] = unpack_shards(sum_slabs(got, f"sum_g{l + 1}"))
        grads[l] = g
        if l > 0:
            chip_sums = pair_sum(own_sent, own_got, f"pair_sum{l}")
    big_sum[0] = unpack_shards(sum_slabs(own_got, "sum_g0"))
    grad_x = dx.reshape(bl, s, D)

    small_names = REPLICATED + ("conv_w",)
    small_vals = [jnp.stack([grads[l][n] for l in range(DEPTH)]) for n in small_names] + [dfinal[0]]
    small_shapes = [v.shape for v in small_vals]
    small_all = exchange(pack_small(small_vals), "direct", "gather_small")
    small_sum = unpack_small(sum_slabs(small_all, "sum_small"), small_shapes)
    gsum = dict(zip(small_names + ("final_norm_w",), small_sum))
    conv_g_full = gsum["conv_w"]
    gsum["conv_w"] = lax.dynamic_slice_in_dim(conv_g_full, me * (CONV_CH // N_DEV), CONV_CH // N_DEV, axis=2)
    for n in BIG:
        gsum[n] = jnp.stack([big_sum[l][n] for l in range(DEPTH)])

    delta, new_m, new_v = {}, {}, {}
    for n in BIG + ("conv_w",):
        shp = wts[n].shape
        two_d = (-1, shp[-1])
        d2, m2, v2 = adamw(wts[n].reshape(two_d), gsum[n].reshape(two_d), mom[n].reshape(two_d),
                           var[n].reshape(two_d), f"adamw_{n}")
        delta[n], new_m[n], new_v[n] = d2.reshape(shp), m2.reshape(shp), v2.reshape(shp)
    rep = REPLICATED + ("final_norm_w",)
    rep_shapes = [wts[n].shape for n in rep]
    d2, m2, v2 = adamw(pack_small([wts[n] for n in rep]), pack_small([gsum[n] for n in rep]),
                       pack_small([mom[n] for n in rep]), pack_small([var[n] for n in rep]), "adamw_small")
    for n, dv, mv, vv in zip(rep, unpack_small(d2, rep_shapes), unpack_small(m2, rep_shapes),
                             unpack_small(v2, rep_shapes)):
        delta[n], new_m[n], new_v[n] = dv, mv, vv

    return (loss, grad_x, *[gsum[n] for n in WEIGHTS], *[delta[n] for n in WEIGHTS],
            *[new_m[n] for n in WEIGHTS], *[new_v[n] for n in WEIGHTS])
```

```python
import functools

import jax
import jax.numpy as jnp
from jax import lax
from jax.experimental import pallas as pl
from jax.experimental.pallas import tpu as pltpu

F32 = jnp.float32
BF16 = jnp.bfloat16

N_DEV = 8
DEPTH = 4
D = 1024
SSM_W = 2048
N_HEADS = 32
N_PAIRS = 16
N_GROUPS = 2
N_STATE = 128
CHUNK = 128
CONV_CH = 2560
CONV_K = 4
POOL_W = 1024
POOL_G = 4
POOL_GD = 256
SB_W = 1024
SB_PAIRS = 8
QB = 256
EPS = 1e-6
IN_COLS = 13856

PC = 14336
OFF_MERGE = 0
OFF_SBG = 3072
OFF_PU = 4096
OFF_PG = 5120
OFF_Z = 6144
OFF_QKV = 8192
OFF_XBC = 11264
OFF_DT = 13824

ADAM_LR = 0.001
ADAM_B1 = 0.9
ADAM_B2 = 0.999
ADAM_EPS = 1e-08
ADAM_WD = 0.01
ADAM_STEP = 10

VMEM_LIMIT = 56 * 1024 * 1024

_NN = (((1,), (0,)), ((), ()))
_NT = (((1,), (1,)), ((), ()))
_TN = (((0,), (0,)), ((), ()))


def _dot(a, b, dn=_NN):
    return lax.dot_general(a, b, dn, preferred_element_type=F32)


def _sigmoid(x):
    return 1.0 / (1.0 + jnp.exp(-x))


def _softplus(x):
    return jnp.maximum(x, 0.0) + jnp.log(1.0 + jnp.exp(-jnp.abs(x)))


def _split2(x):
    hi = x.astype(BF16)
    lo = (x - hi.astype(F32)).astype(BF16)
    return hi, lo


def _split3(x):
    hi = x.astype(BF16)
    r = x - hi.astype(F32)
    mid = r.astype(BF16)
    lo = (r - mid.astype(F32)).astype(BF16)
    return hi, mid, lo


def _params(*sem):
    return pltpu.CompilerParams(dimension_semantics=sem, vmem_limit_bytes=VMEM_LIMIT)


def matmul(a, b, mode, out_dtype, name, residual=None, tm=1024, tn=1024, tk=1024, comm=None):
    if mode == "nn":
        (m, k), (k2, n) = a.shape, b.shape
    elif mode == "nt":
        (m, k), (n, k2) = a.shape, b.shape
    else:
        (k, m), (k2, n) = a.shape, b.shape
    assert k == k2
    tm, tn, tk = min(tm, m), min(tn, n), min(tk, k)
    assert m % tm == 0 and n % tn == 0 and k % tk == 0
    nk = k // tk
    dn = {"nn": _NN, "nt": _NT, "tn": _TN}[mode]
    a_spec = pl.BlockSpec((tk, tm), lambda i, j, kk: (kk, i)) if mode == "tn" else pl.BlockSpec((tm, tk), lambda i, j, kk: (i, kk))
    b_spec = pl.BlockSpec((tn, tk), lambda i, j, kk: (j, kk)) if mode == "nt" else pl.BlockSpec((tk, tn), lambda i, j, kk: (kk, j))
    in_specs = [a_spec, b_spec]
    args = [a, b]
    if residual is not None:
        in_specs.append(pl.BlockSpec((tm, tn), lambda i, j, kk: (i, j)))
        args.append(residual)
    grid = (m // tm, n // tn, nk)
    n_in = len(args)
    x_in, x_args, x_out, x_shape, x_scratch, x_start, x_wait = _hosted_exchange(comm, grid, relay_at=None)

    def body(*refs):
        n_xi, n_xo = len(x_in), len(x_out)
        a_ref, b_ref = refs[:2]
        r_ref = refs[2] if residual is not None else None
        o_ref = refs[n_in + n_xi]
        acc_ref = refs[n_in + n_xi + n_xo + 1]
        x_refs = refs[n_in:n_in + n_xi] + refs[n_in + n_xi + 1:n_in + n_xi + n_xo + 1] + refs[n_in + n_xi + n_xo + 2:]
        x_start(x_refs)
        kk = pl.program_id(2)
        p = _dot(a_ref[...], b_ref[...], dn)

        def finish(val):
            if r_ref is not None:
                val = val + r_ref[...]
            o_ref[...] = val.astype(out_dtype)

        if nk == 1:
            finish(p)
        else:
            @pl.when(kk == 0)
            def _():
                acc_ref[...] = p

            @pl.when(kk > 0)
            def _():
                acc_ref[...] += p

            @pl.when(kk == nk - 1)
            def _():
                finish(acc_ref[...])

        x_wait(x_refs)

    out = pl.pallas_call(
        body, name=name,
        grid=grid,
        in_specs=in_specs + x_in,
        out_specs=[pl.BlockSpec((tm, tn), lambda i, j, kk: (i, j))] + x_out,
        out_shape=[jax.ShapeDtypeStruct((m, n), out_dtype)] + x_shape,
        scratch_shapes=[pltpu.VMEM((tm, tn) if nk > 1 else (8, 128), F32)] + x_scratch,
        compiler_params=_params(*(("arbitrary",) * 3 if comm is not None else ("parallel", "parallel", "arbitrary"))),
    )(*args, *x_args)
    return tuple(out) if comm is not None else out[0]


def rmsnorm_fwd(x, w, name):
    t, d = x.shape
    tr = min(512, t)

    def body(x_ref, w_ref, h_ref):
        xv = x_ref[...]
        r = lax.rsqrt(jnp.mean(xv * xv, axis=-1, keepdims=True) + EPS)
        h_ref[...] = (xv * r * w_ref[...]).astype(BF16)

    return pl.pallas_call(
        body, name=name, grid=(t // tr,),
        in_specs=[pl.BlockSpec((tr, d), lambda i: (i, 0)), pl.BlockSpec((1, d), lambda i: (0, 0))],
        out_specs=pl.BlockSpec((tr, d), lambda i: (i, 0)),
        out_shape=jax.ShapeDtypeStruct((t, d), BF16),
        compiler_params=_params("parallel"),
    )(x, w.reshape(1, d))


def rmsnorm_bwd(dh, x, w, dres, name):
    t, d = x.shape
    tr = min(512, t)

    def body(dh_ref, x_ref, w_ref, dres_ref, dx_ref, dxb_ref, dw_ref):
        xv = x_ref[...]
        r = lax.rsqrt(jnp.mean(xv * xv, axis=-1, keepdims=True) + EPS)
        xh = xv * r
        g = dh_ref[...].astype(F32)
        dxh = g * w_ref[...]
        dxv = dres_ref[...] + r * (dxh - xh * jnp.mean(dxh * xh, axis=-1, keepdims=True))
        dx_ref[...] = dxv
        dxb_ref[...] = dxv.astype(BF16)
        part = jnp.sum(g * xh, axis=0, keepdims=True)

        @pl.when(pl.program_id(0) == 0)
        def _():
            dw_ref[...] = part

        @pl.when(pl.program_id(0) > 0)
        def _():
            dw_ref[...] += part

    return pl.pallas_call(
        body, name=name, grid=(t // tr,),
        in_specs=[pl.BlockSpec((tr, d), lambda i: (i, 0)), pl.BlockSpec((tr, d), lambda i: (i, 0)),
                  pl.BlockSpec((1, d), lambda i: (0, 0)), pl.BlockSpec((tr, d), lambda i: (i, 0))],
        out_specs=[pl.BlockSpec((tr, d), lambda i: (i, 0)), pl.BlockSpec((tr, d), lambda i: (i, 0)),
                   pl.BlockSpec((1, d), lambda i: (0, 0))],
        out_shape=[jax.ShapeDtypeStruct((t, d), F32), jax.ShapeDtypeStruct((t, d), BF16),
                   jax.ShapeDtypeStruct((1, d), F32)],
        compiler_params=_params("arbitrary"),
    )(dh, x, w.reshape(1, d), dres)


def final_loss(x, w, target, name):
    t, d = x.shape
    tr = min(512, t)

    def body(x_ref, w_ref, tg_ref, loss_ref, dx_ref, dxb_ref, dw_ref):
        xv = x_ref[...]
        r = lax.rsqrt(jnp.mean(xv * xv, axis=-1, keepdims=True) + EPS)
        xh = xv * r
        err = xh * w_ref[...] - tg_ref[...]
        lpart = 0.5 * jnp.sum(jnp.mean(err * err, axis=-1, keepdims=True), axis=0, keepdims=True)
        dy = err * (1.0 / d)
        dxh = dy * w_ref[...]
        dxv = r * (dxh - xh * jnp.mean(dxh * xh, axis=-1, keepdims=True))
        dx_ref[...] = dxv
        dxb_ref[...] = dxv.astype(BF16)
        part = jnp.sum(dy * xh, axis=0, keepdims=True)

        @pl.when(pl.program_id(0) == 0)
        def _():
            dw_ref[...] = part
            loss_ref[...] = jnp.broadcast_to(lpart, (1, 128))

        @pl.when(pl.program_id(0) > 0)
        def _():
            dw_ref[...] += part
            loss_ref[...] += jnp.broadcast_to(lpart, (1, 128))

    return pl.pallas_call(
        body, name=name, grid=(t // tr,),
        in_specs=[pl.BlockSpec((tr, d), lambda i: (i, 0)), pl.BlockSpec((1, d), lambda i: (0, 0)),
                  pl.BlockSpec((tr, d), lambda i: (i, 0))],
        out_specs=[pl.BlockSpec((1, 128), lambda i: (0, 0)), pl.BlockSpec((tr, d), lambda i: (i, 0)),
                   pl.BlockSpec((tr, d), lambda i: (i, 0)), pl.BlockSpec((1, d), lambda i: (0, 0))],
        out_shape=[jax.ShapeDtypeStruct((1, 128), F32), jax.ShapeDtypeStruct((t, d), F32),
                   jax.ShapeDtypeStruct((t, d), BF16), jax.ShapeDtypeStruct((1, d), F32)],
        compiler_params=_params("arbitrary"),
    )(x, w.reshape(1, d), target)


CONV_BW = 256


def _shift_down(u, s, row):
    return jnp.where(row >= s, pltpu.roll(u, s, axis=0), 0.0)


def _shift_up(u, s, row, n):
    return jnp.where(row < n - s, pltpu.roll(u, n - s, axis=0), 0.0)


def _conv_pre(u, w, b, row):
    acc = b + w[CONV_K - 1:CONV_K, :] * u
    for k in range(CONV_K - 1):
        acc = acc + w[k:k + 1, :] * _shift_down(u, CONV_K - 1 - k, row)
    return acc


def conv_fwd(proj, conv_w, conv_b, bl, s, name):
    t = bl * s
    nb = CONV_CH // CONV_BW
    off = OFF_XBC // CONV_BW

    def body(u_ref, w_ref, b_ref, o_ref):
        u = u_ref[...].astype(F32)
        row = lax.broadcasted_iota(jnp.int32, u.shape, 0)
        xc = _conv_pre(u, w_ref[...], b_ref[...], row)
        o_ref[...] = (xc * _sigmoid(xc)).astype(BF16)

    return pl.pallas_call(
        body, name=name, grid=(bl, nb),
        in_specs=[pl.BlockSpec((s, CONV_BW), lambda b, j: (b, off + j)),
                  pl.BlockSpec((CONV_K, CONV_BW), lambda b, j: (0, j)),
                  pl.BlockSpec((1, CONV_BW), lambda b, j: (0, j))],
        out_specs=pl.BlockSpec((s, CONV_BW), lambda b, j: (b, j)),
        out_shape=jax.ShapeDtypeStruct((t, CONV_CH), BF16),
        compiler_params=_params("parallel", "parallel"),
    )(proj, conv_w, conv_b.reshape(1, CONV_CH))


def conv_bwd(dxa, proj, conv_w, conv_b, bl, s, name):
    t = bl * s
    nb = CONV_CH // CONV_BW
    off = OFF_XBC // CONV_BW

    def body(d_ref, u_ref, w_ref, b_ref, du_ref, dw_ref, db_ref):
        u = u_ref[...].astype(F32)
        w = w_ref[...]
        row = lax.broadcasted_iota(jnp.int32, u.shape, 0)
        xc = _conv_pre(u, w, b_ref[...], row)
        sg = _sigmoid(xc)
        dxc = d_ref[...].astype(F32) * sg * (1.0 + xc * (1.0 - sg))
        du = w[CONV_K - 1:CONV_K, :] * dxc
        dws = [None] * CONV_K
        dws[CONV_K - 1] = jnp.sum(dxc * u, axis=0, keepdims=True)
        for k in range(CONV_K - 1):
            up = _shift_up(dxc, CONV_K - 1 - k, row, s)
            du = du + w[k:k + 1, :] * up
            dws[k] = jnp.sum(up * u, axis=0, keepdims=True)
        du_ref[...] = du.astype(BF16)
        krow = lax.broadcasted_iota(jnp.int32, (8, CONV_BW), 0)
        dwv = sum(jnp.where(krow == k, dws[k], 0.0) for k in range(CONV_K))
        dbv = jnp.sum(dxc, axis=0, keepdims=True)

        @pl.when(pl.program_id(1) == 0)
        def _():
            dw_ref[...] = dwv
            db_ref[...] = dbv

        @pl.when(pl.program_id(1) > 0)
        def _():
            dw_ref[...] += dwv
            db_ref[...] += dbv

    du, dw, db = pl.pallas_call(
        body, name=name, grid=(nb, bl),
        in_specs=[pl.BlockSpec((s, CONV_BW), lambda j, b: (b, j)),
                  pl.BlockSpec((s, CONV_BW), lambda j, b: (b, off + j)),
                  pl.BlockSpec((CONV_K, CONV_BW), lambda j, b: (0, j)),
                  pl.BlockSpec((1, CONV_BW), lambda j, b: (0, j))],
        out_specs=[pl.BlockSpec((s, CONV_BW), lambda j, b: (b, j)),
                   pl.BlockSpec((8, CONV_BW), lambda j, b: (0, j)),
                   pl.BlockSpec((1, CONV_BW), lambda j, b: (0, j))],
        out_shape=[jax.ShapeDtypeStruct((t, CONV_CH), BF16), jax.ShapeDtypeStruct((8, CONV_CH), F32),
                   jax.ShapeDtypeStruct((1, CONV_CH), F32)],
        compiler_params=_params("parallel", "arbitrary"),
    )(dxa, proj, conv_w, conv_b.reshape(1, CONV_CH))
    return du, dw[:CONV_K], db[0]


def _tri(shape, cmp):
    r = lax.broadcasted_iota(jnp.int32, shape, 0)
    c = lax.broadcasted_iota(jnp.int32, shape, 1)
    return cmp(r, c)


def dt_fwd(proj, dt_bias, a_log, t, name):
    nchunks = t // CHUNK
    bias = jnp.zeros((1, 128), F32).at[0, :N_HEADS].set(dt_bias)
    alog = jnp.zeros((1, 128), F32).at[0, :N_HEADS].set(a_log)

    def body(raw_ref, b_ref, al_ref, dt_ref, ac_ref, dtl_ref, acl_ref):
        raw = raw_ref[...].astype(F32)
        dt = _softplus(raw + b_ref[...])
        adt = dt * (-jnp.exp(al_ref[...]))
        low = _tri((CHUNK, CHUNK), lambda r, c: r >= c).astype(BF16)
        acum = sum(_dot(low, part) for part in _split3(adt))
        dt_ref[...] = dt.T[:N_HEADS]
        ac_ref[...] = acum.T[:N_HEADS]
        spread = _tri((128, SSM_W), lambda h, lane: lane // 64 == h).astype(BF16)
        dtl_ref[...] = sum(_dot(part, spread) for part in _split2(dt))
        acl_ref[...] = sum(_dot(part, spread) for part in _split3(acum))

    return pl.pallas_call(
        body, name=name, grid=(nchunks,),
        in_specs=[pl.BlockSpec((CHUNK, 128), lambda i: (i, OFF_DT // 128)),
                  pl.BlockSpec((1, 128), lambda i: (0, 0)), pl.BlockSpec((1, 128), lambda i: (0, 0))],
        out_specs=[pl.BlockSpec((None, N_HEADS, CHUNK), lambda i: (i, 0, 0))] * 2
        + [pl.BlockSpec((CHUNK, SSM_W), lambda i: (i, 0))] * 2,
        out_shape=[jax.ShapeDtypeStruct((nchunks, N_HEADS, CHUNK), F32)] * 2
        + [jax.ShapeDtypeStruct((t, SSM_W), F32)] * 2,
        compiler_params=_params("parallel"),
    )(proj, bias, alog)


def dt_bwd(ddtT, dacT, dtT, proj, dt_bias, a_log, t, name):
    nchunks = t // CHUNK
    bias = dt_bias.reshape(N_HEADS, 1)
    alog = a_log.reshape(N_HEADS, 1)

    def body(ddt_ref, dac_ref, dt_ref, raw_ref, b_ref, al_ref, draw_ref, da_ref, db_ref):
        a = -jnp.exp(al_ref[...])
        upp = _tri((CHUNK, CHUNK), lambda r, c: r >= c).astype(BF16)
        dadt = sum(_dot(part, upp) for part in _split3(dac_ref[...]))
        ddt = ddt_ref[...] + dadt * a
        rawT = raw_ref[...].astype(F32).T[:N_HEADS]
        draw = ddt * _sigmoid(rawT + b_ref[...])
        padded = jnp.concatenate([draw, jnp.zeros((128 - N_HEADS, CHUNK), F32)], axis=0)
        draw_ref[...] = padded.T.astype(BF16)
        dav = dadt * dt_ref[...]

        @pl.when(pl.program_id(0) == 0)
        def _():
            da_ref[...] = dav
            db_ref[...] = draw

        @pl.when(pl.program_id(0) > 0)
        def _():
            da_ref[...] += dav
            db_ref[...] += draw

    draw, da, db = pl.pallas_call(
        body, name=name, grid=(nchunks,),
        in_specs=[pl.BlockSpec((None, N_HEADS, CHUNK), lambda i: (i, 0, 0))] * 3
        + [pl.BlockSpec((CHUNK, 128), lambda i: (i, OFF_DT // 128)),
           pl.BlockSpec((N_HEADS, 1), lambda i: (0, 0)), pl.BlockSpec((N_HEADS, 1), lambda i: (0, 0))],
        out_specs=[pl.BlockSpec((CHUNK, 128), lambda i: (i, 0)),
                   pl.BlockSpec((N_HEADS, CHUNK), lambda i: (0, 0)), pl.BlockSpec((N_HEADS, CHUNK), lambda i: (0, 0))],
        out_shape=[jax.ShapeDtypeStruct((t, 128), BF16), jax.ShapeDtypeStruct((N_HEADS, CHUNK), F32),
                   jax.ShapeDtypeStruct((N_HEADS, CHUNK), F32)],
        compiler_params=_params("arbitrary"),
    )(ddtT, dacT, dtT, proj, bias, alog)
    return draw, jnp.sum(da, axis=1), jnp.sum(db, axis=1)


PAIRS_G = N_PAIRS // N_GROUPS
GROUP_W = PAIRS_G * 128


def _ssd_pair(x, dtl, acl, acr, tri):
    left = lax.broadcasted_iota(jnp.int32, (CHUNK, 128), 1) < 64
    swapped = pltpu.roll(acl, 64, axis=1)
    ac_cols = [jnp.where(left, acl, swapped), jnp.where(left, swapped, acl)]
    dks = [jnp.exp(jnp.where(tri, ac_cols[e] - acr[e:e + 1], -1e30)) for e in range(2)]
    aclast = acl[CHUNK - 1:CHUNK, :]
    return left, dtl, acl, x * dtl, dks, aclast


def _ssd_specs(nc, rev):
    row = (lambda b, c, g: b * nc + (nc - 1 - c)) if rev else (lambda b, c, g: b * nc + c)
    return dict(
        wide=pl.BlockSpec((CHUNK, GROUP_W), lambda b, c, g: (row(b, c, g), g)),
        bmat=pl.BlockSpec((CHUNK, 128), lambda b, c, g: (row(b, c, g), SSM_W // 128 + g)),
        cmat=pl.BlockSpec((CHUNK, 128), lambda b, c, g: (row(b, c, g), SSM_W // 128 + N_GROUPS + g)),
        rows2=pl.BlockSpec((None, PAIRS_G, 2, CHUNK), lambda b, c, g: (row(b, c, g), g, 0, 0)),
        cols=pl.BlockSpec((CHUNK, GROUP_W), lambda b, c, g: (row(b, c, g), g)),
        rows8=pl.BlockSpec((None, PAIRS_G, 8, CHUNK), lambda b, c, g: (row(b, c, g), g, 0, 0)),
        dskip=pl.BlockSpec((1, GROUP_W), lambda b, c, g: (0, g)),
        state=pl.BlockSpec((None, PAIRS_G, N_STATE, 128), lambda b, c, g: (row(b, c, g), g, 0, 0)),
        narrow=pl.BlockSpec((CHUNK, 128), lambda b, c, g: (row(b, c, g), g)))


def ssd_fwd(xa, acT, dtC, acC, dskip_l, bl, s, name):
    t = bl * s
    nc = s // CHUNK
    ac4 = acT.reshape(bl * nc, N_PAIRS, 2, CHUNK)

    def body(x_ref, b_ref, c_ref, ac_ref, dtc_ref, acc_ref, dsk_ref, y_ref, prev_ref, st_ref):
        c = pl.program_id(1)
        g = pl.program_id(2)
        bm = b_ref[...]
        cm = c_ref[...]
        cb = _dot(cm, bm, _NT)
        tri = _tri((CHUNK, CHUNK), lambda r, c: r >= c)

        @pl.when(c == 0)
        def _():
            for p in range(PAIRS_G):
                st_ref[g * PAIRS_G + p] = jnp.zeros((N_STATE, 128), F32)

        for p in range(PAIRS_G):
            hp = g * PAIRS_G + p
            cs = slice(p * 128, (p + 1) * 128)
            x = x_ref[:, cs].astype(F32)
            left, dtl, acl, xdt, dks, aclast = _ssd_pair(x, dtc_ref[:, cs], acc_ref[:, cs], ac_ref[p], tri)
            xdt_b = xdt.astype(BF16)
            ys = [_dot((cb * dks[e]).astype(BF16), xdt_b) for e in range(2)]
            st = st_ref[hp]
            y_off = _dot(cm, st.astype(BF16)) * jnp.exp(acl)
            y_ref[:, cs] = (jnp.where(left, ys[0], ys[1]) + y_off + x * dsk_ref[:, cs]).astype(BF16)
            xw = (xdt * jnp.exp(aclast - acl)).astype(BF16)
            prev_ref[p] = st
            st_ref[hp] = st * jnp.exp(aclast) + _dot(bm, xw, _TN)

    sp = _ssd_specs(nc, False)
    return pl.pallas_call(
        body, name=name, grid=(bl, nc, N_GROUPS),
        in_specs=[sp["wide"], sp["bmat"], sp["cmat"], sp["rows2"], sp["cols"], sp["cols"], sp["dskip"]],
        out_specs=[sp["wide"], sp["state"]],
        out_shape=[jax.ShapeDtypeStruct((t, SSM_W), BF16),
                   jax.ShapeDtypeStruct((bl * nc, N_PAIRS, N_STATE, 128), F32)],
        scratch_shapes=[pltpu.VMEM((N_PAIRS, N_STATE, 128), F32)],
        compiler_params=_params("parallel", "arbitrary", "arbitrary"),
    )(xa, xa, xa, ac4, dtC, acC, dskip_l)


def ssd_bwd(dy, xa, acT, dtC, acC, dskip_l, prev, bl, s, name):
    t = bl * s
    nc = s // CHUNK
    ac4 = acT.reshape(bl * nc, N_PAIRS, 2, CHUNK)

    def body(dy_ref, x_ref, b_ref, c_ref, ac_ref, dtc_ref, acc_ref, dsk_ref, prev_ref,
             dx_ref, db_ref, dc_ref, dd_ref, dsk_out_ref, dp_ref):
        b = pl.program_id(0)
        cr = pl.program_id(1)
        g = pl.program_id(2)

        @pl.when(cr == 0)
        def _():
            for p in range(PAIRS_G):
                dp_ref[g * PAIRS_G + p] = jnp.zeros((N_STATE, 128), F32)

        @pl.when((b == 0) & (cr == 0) & (g == 0))
        def _():
            dsk_out_ref[...] = jnp.zeros(dsk_out_ref.shape, F32)

        bm = b_ref[...]
        cm = c_ref[...]
        cb = _dot(cm, bm, _NT)
        tri = _tri((CHUNK, CHUNK), lambda r, c: r >= c)
        lane = lax.broadcasted_iota(jnp.int32, (CHUNK, 128), 1)
        lrow = lax.broadcasted_iota(jnp.int32, (1, CHUNK), 1)
        krow = lax.broadcasted_iota(jnp.int32, (8, CHUNK), 0)
        dcb = jnp.zeros((CHUNK, CHUNK), F32)
        dc_acc = jnp.zeros((CHUNK, N_STATE), F32)
        db_acc = jnp.zeros((CHUNK, N_STATE), F32)
        for p in range(PAIRS_G):
            hp = g * PAIRS_G + p
            cs = slice(p * 128, (p + 1) * 128)
            x = x_ref[:, cs].astype(F32)
            left, dtl, acl, xdt, dks, aclast = _ssd_pair(x, dtc_ref[:, cs], acc_ref[:, cs], ac_ref[p], tri)
            dyv = dy_ref[:, cs].astype(F32)
            dy_b = dyv.astype(BF16)
            xdt_b = xdt.astype(BF16)
            st = prev_ref[p]
            st_b = st.astype(BF16)
            ea = jnp.exp(acl)
            ds = jnp.exp(aclast - acl)
            cdl = jnp.exp(aclast)
            xw = xdt * ds
            masks = [left, jnp.logical_not(left)]

            dsk_out_ref[hp] = dsk_out_ref[hp] + jnp.sum(dyv * x, axis=0, keepdims=True)

            yo = _dot(cm, st_b)
            dyo_b = (dyv * ea).astype(BF16)
            yoff_term = dyv * yo * ea
            dc_acc = dc_acc + _dot(dyo_b, st_b, _NT)
            dst = _dot(cm, dyo_b, _TN)
            dsv = dp_ref[hp]
            dsv_b = dsv.astype(BF16)
            dxw = _dot(bm, dsv_b)
            db_acc = db_acc + _dot(xw.astype(BF16), dsv_b, _NT)
            dxdt = dxw * ds
            qv = dxw * xw
            end_term = dsv * st * cdl
            dp_ref[hp] = dsv * cdl + dst

            cols = jnp.zeros((CHUNK, 128), F32)
            rows = []
            for e in range(2):
                m = cb * dks[e]
                dy_e = jnp.where(masks[e], dyv, 0.0).astype(BF16)
                dm = _dot(dy_e, xdt_b, _NT)
                w = dm * m
                dcb = dcb + dm * dks[e]
                dxdt = dxdt + jnp.where(masks[e], _dot(m.astype(BF16), dy_b, _TN), 0.0)
                dac_col = jnp.sum(w + jnp.where(masks[e], yoff_term - qv, 0.0), axis=1, keepdims=True)
                cols = jnp.where(lane == 2 + e, dac_col, cols)
                tail = jnp.sum(jnp.where(masks[e], qv + end_term, 0.0))
                rows.append(jnp.where(lrow == CHUNK - 1, tail, 0.0) - jnp.sum(w, axis=0, keepdims=True))
            dx_ref[:, cs] = (dxdt * dtl + dyv * dsk_ref[:, cs]).astype(BF16)
            ddt_l = dxdt * x
            for e in range(2):
                cols = jnp.where(lane == e, jnp.sum(jnp.where(masks[e], ddt_l, 0.0), axis=1, keepdims=True), cols)
            dd_ref[p] = cols.T[0:8] + jnp.where(krow == 2, rows[0], 0.0) + jnp.where(krow == 3, rows[1], 0.0)
        dcb_b = dcb.astype(BF16)
        dc_ref[...] = dc_acc + _dot(dcb_b, bm)
        db_ref[...] = db_acc + _dot(dcb_b, cm, _TN)

    sp = _ssd_specs(nc, True)
    dx, db, dc, dd, dsk = pl.pallas_call(
        body, name=name, grid=(bl, nc, N_GROUPS),
        in_specs=[sp["wide"], sp["wide"], sp["bmat"], sp["cmat"], sp["rows2"], sp["cols"], sp["cols"], sp["dskip"],
                  sp["state"]],
        out_specs=[sp["wide"], sp["narrow"], sp["narrow"], sp["rows8"],
                   pl.BlockSpec((N_PAIRS, 1, 128), lambda b, c, g: (0, 0, 0))],
        out_shape=[jax.ShapeDtypeStruct((t, SSM_W), BF16),
                   jax.ShapeDtypeStruct((t, N_GROUPS * N_STATE), F32),
                   jax.ShapeDtypeStruct((t, N_GROUPS * N_STATE), F32),
                   jax.ShapeDtypeStruct((bl * nc, N_PAIRS, 8, CHUNK), F32),
                   jax.ShapeDtypeStruct((N_PAIRS, 1, 128), F32)],
        scratch_shapes=[pltpu.VMEM((N_PAIRS, N_STATE, 128), F32)],
        compiler_params=_params("arbitrary", "arbitrary", "arbitrary"),
    )(dy, xa, xa, xa, ac4, dtC, acC, dskip_l, prev)
    ddtT = dd[:, :, 0:2, :].reshape(bl * nc, N_HEADS, CHUNK)
    dacT = dd[:, :, 2:4, :].reshape(bl * nc, N_HEADS, CHUNK)
    return dx, db, dc, ddtT, dacT, dsk.reshape(N_PAIRS, 128)


def gnorm_fwd(y, proj, w, name):
    t = y.shape[0]
    tr = min(256, t)
    zb = OFF_Z // SSM_W

    def body(y_ref, z_ref, w_ref, o_ref):
        z = z_ref[...].astype(F32)
        yg = y_ref[...].astype(F32) * z * _sigmoid(z)
        r = lax.rsqrt(jnp.mean(yg * yg, axis=-1, keepdims=True) + EPS)
        o_ref[...] = (yg * r * w_ref[...]).astype(BF16)

    return pl.pallas_call(
        body, name=name, grid=(t // tr,),
        in_specs=[pl.BlockSpec((tr, SSM_W), lambda i: (i, 0)), pl.BlockSpec((tr, SSM_W), lambda i: (i, zb)),
                  pl.BlockSpec((1, SSM_W), lambda i: (0, 0))],
        out_specs=pl.BlockSpec((tr, SSM_W), lambda i: (i, 0)),
        out_shape=jax.ShapeDtypeStruct((t, SSM_W), BF16),
        compiler_params=_params("parallel"),
    )(y, proj, w.reshape(1, SSM_W))


def gnorm_bwd(ds, y, proj, w, name):
    t = y.shape[0]
    tr = min(256, t)
    zb = OFF_Z // SSM_W

    def body(ds_ref, y_ref, z_ref, w_ref, dy_ref, dz_ref, dw_ref):
        z = z_ref[...].astype(F32)
        yv = y_ref[...].astype(F32)
        sg = _sigmoid(z)
        sz = z * sg
        yg = yv * sz
        r = lax.rsqrt(jnp.mean(yg * yg, axis=-1, keepdims=True) + EPS)
        xh = yg * r
        g = ds_ref[...].astype(F32)
        dxh = g * w_ref[...]
        dyg = r * (dxh - xh * jnp.mean(dxh * xh, axis=-1, keepdims=True))
        dy_ref[...] = (dyg * sz).astype(BF16)
        dz_ref[...] = (dyg * yv * sg * (1.0 + z * (1.0 - sg))).astype(BF16)
        part = jnp.sum(g * xh, axis=0, keepdims=True)

        @pl.when(pl.program_id(0) == 0)
        def _():
            dw_ref[...] = part

        @pl.when(pl.program_id(0) > 0)
        def _():
            dw_ref[...] += part

    return pl.pallas_call(
        body, name=name, grid=(t // tr,),
        in_specs=[pl.BlockSpec((tr, SSM_W), lambda i: (i, 0)), pl.BlockSpec((tr, SSM_W), lambda i: (i, 0)),
                  pl.BlockSpec((tr, SSM_W), lambda i: (i, zb)), pl.BlockSpec((1, SSM_W), lambda i: (0, 0))],
        out_specs=[pl.BlockSpec((tr, SSM_W), lambda i: (i, 0)), pl.BlockSpec((tr, SSM_W), lambda i: (i, 0)),
                   pl.BlockSpec((1, SSM_W), lambda i: (0, 0))],
        out_shape=[jax.ShapeDtypeStruct((t, SSM_W), BF16), jax.ShapeDtypeStruct((t, SSM_W), BF16),
                   jax.ShapeDtypeStruct((1, SSM_W), F32)],
        compiler_params=_params("arbitrary"),
    )(ds, y, proj, w.reshape(1, SSM_W))


def _pool_mixed(u, g, row):
    win = 2 << g
    acc = u
    for k in range(g + 1):
        acc = acc + _shift_down(acc, 1 << k, row)
    inv = 1.0 / jnp.minimum(row + 1, win).astype(F32)
    return acc * inv - u, inv


def pool_fwd(proj, pool_w, pool_scale, bl, s, name):
    t = bl * s

    def body(u_ref, g_ref, w_ref, sc_ref, o_ref):
        row = lax.broadcasted_iota(jnp.int32, (s, POOL_GD), 0)
        for g in range(POOL_G):
            cs = slice(g * POOL_GD, (g + 1) * POOL_GD)
            u = u_ref[:, cs].astype(F32)
            mixed, _ = _pool_mixed(u, g, row)
            pm = _dot(mixed.astype(BF16), w_ref[g])
            gate = g_ref[:, cs].astype(F32)
            o_ref[:, cs] = (pm * sc_ref[:, cs] * gate * _sigmoid(gate)).astype(BF16)

    return pl.pallas_call(
        body, name=name, grid=(bl,),
        in_specs=[pl.BlockSpec((s, POOL_W), lambda b: (b, OFF_PU // POOL_W)),
                  pl.BlockSpec((s, POOL_W), lambda b: (b, OFF_PG // POOL_W)),
                  pl.BlockSpec((POOL_G, POOL_GD, POOL_GD), lambda b: (0, 0, 0)),
                  pl.BlockSpec((1, POOL_W), lambda b: (0, 0))],
        out_specs=pl.BlockSpec((s, POOL_W), lambda b: (b, 0)),
        out_shape=jax.ShapeDtypeStruct((t, POOL_W), BF16),
        compiler_params=_params("parallel"),
    )(proj, proj, pool_w, pool_scale.reshape(1, POOL_W))


def pool_bwd(dp, proj, pool_w, pool_scale, bl, s, name):
    t = bl * s

    def body(dp_ref, u_ref, g_ref, w_ref, sc_ref, du_ref, dg_ref, dw_ref, dsc_ref):
        row = lax.broadcasted_iota(jnp.int32, (s, POOL_GD), 0)
        first = pl.program_id(0) == 0
        for g in range(POOL_G):
            cs = slice(g * POOL_GD, (g + 1) * POOL_GD)
            u = u_ref[:, cs].astype(F32)
            mixed, inv = _pool_mixed(u, g, row)
            mixed_b = mixed.astype(BF16)
            wg = w_ref[g]
            pm = _dot(mixed_b, wg)
            gate = g_ref[:, cs].astype(F32)
            sg = _sigmoid(gate)
            d = dp_ref[:, cs].astype(F32)
            sc = sc_ref[:, cs]
            dpm = (d * sc * gate * sg).astype(BF16)
            dg_ref[:, cs] = (d * pm * sc * sg * (1.0 + gate * (1.0 - sg))).astype(BF16)
            dsc = jnp.sum(d * pm * gate * sg, axis=0, keepdims=True)
            dwg = _dot(mixed_b, dpm, _TN)
            dmixed = _dot(dpm, wg, _NT)
            acc = dmixed * inv
            for k in range(g + 1):
                acc = acc + _shift_up(acc, 1 << k, row, s)
            du_ref[:, cs] = (acc - dmixed).astype(BF16)

            @pl.when(first)
            def _():
                dw_ref[g] = dwg
                dsc_ref[:, cs] = dsc

            @pl.when(jnp.logical_not(first))
            def _():
                dw_ref[g] = dw_ref[g] + dwg
                dsc_ref[:, cs] = dsc_ref[:, cs] + dsc

    return pl.pallas_call(
        body, name=name, grid=(bl,),
        in_specs=[pl.BlockSpec((s, POOL_W), lambda b: (b, 0)),
                  pl.BlockSpec((s, POOL_W), lambda b: (b, OFF_PU // POOL_W)),
                  pl.BlockSpec((s, POOL_W), lambda b: (b, OFF_PG // POOL_W)),
                  pl.BlockSpec((POOL_G, POOL_GD, POOL_GD), lambda b: (0, 0, 0)),
                  pl.BlockSpec((1, POOL_W), lambda b: (0, 0))],
        out_specs=[pl.BlockSpec((s, POOL_W), lambda b: (b, 0)), pl.BlockSpec((s, POOL_W), lambda b: (b, 0)),
                   pl.BlockSpec((POOL_G, POOL_GD, POOL_GD), lambda b: (0, 0, 0)),
                   pl.BlockSpec((1, POOL_W), lambda b: (0, 0))],
        out_shape=[jax.ShapeDtypeStruct((t, POOL_W), BF16), jax.ShapeDtypeStruct((t, POOL_W), BF16),
                   jax.ShapeDtypeStruct((POOL_G, POOL_GD, POOL_GD), F32), jax.ShapeDtypeStruct((1, POOL_W), F32)],
        compiler_params=_params("arbitrary"),
    )(dp, proj, proj, pool_w, pool_scale.reshape(1, POOL_W))


SB_SCALE = 64 ** -0.5


KB = 256


def _sb_block(qe, kj, mask, rr, upper):
    z = _dot(qe, kj, _NT).astype(BF16)
    lb = jnp.minimum(z, 0.0) - jnp.log(1.0 + jnp.exp(-jnp.abs(z)))
    lom = lb - z if mask is None else jnp.where(mask, lb - z, 0.0)
    later = _dot(lom, upper) + rr
    return lb, lom, later


def _sb_masks(i):
    lane = lax.broadcasted_iota(jnp.int32, (QB, 128), 1)
    row = lax.broadcasted_iota(jnp.int32, (2 * QB, KB), 0) % QB
    col = lax.broadcasted_iota(jnp.int32, (2 * QB, KB), 1)
    causal = lambda jb: col + (jb * KB - i * QB) < row
    return lane, lane < 64, causal


def _stack_heads(x, left):
    zero = jnp.zeros_like(x)
    return jnp.concatenate([jnp.where(left, x, zero), jnp.where(left, zero, x)], axis=0)


SB_GROUP = 4
SB_GW = SB_GROUP * 128


def sb_fwd(proj, bl, s, name, comm=None):
    t = bl * s
    nq = s // QB
    group, gw = SB_PAIRS, SB_W
    qb0, kb0, vb0, gb0 = OFF_QKV // gw, (OFF_QKV + SB_W) // gw, (OFF_QKV + 2 * SB_W) // gw, OFF_SBG // gw
    grid = (bl, SB_PAIRS // group, nq)
    x_in, x_args, x_out, x_shape, x_scratch, x_start, x_wait = _hosted_exchange(
        comm, grid, relay_at=(bl - 1, 0, (3 * nq) // 4))

    def body(*refs):
        q_ref, k_ref, v_ref, g_ref = refs[:4]
        og_ref, o_ref, r_ref = refs[4 + len(x_in):7 + len(x_in)]
        x_refs = refs[4:4 + len(x_in)] + refs[7 + len(x_in):]
        x_start(x_refs)
        i = pl.program_id(2)
        lane, left, causal = _sb_masks(i)
        upper = _tri((KB, KB), lambda r, c: r > c).astype(BF16)
        cols = [slice(p * 128, (p + 1) * 128) for p in range(group)]
        qcats = [_stack_heads(q_ref[:, cs] * SB_SCALE, left) for cs in cols]
        zero = qcats[0].astype(F32) * 0.0

        def block(jb, carry, diagonal):
            rows = pl.ds(pl.multiple_of(jb * KB, KB), KB)
            mask = causal(jb) if diagonal else None
            out = []
            for p, cs in enumerate(cols):
                acc, rr, rt = carry[p]
                lb, lom, later = _sb_block(qcats[p], k_ref[rows, cs], mask, rr, upper)
                att = jnp.exp(lb + later)
                if diagonal:
                    att = jnp.where(mask, att, 0.0)
                acc = acc + _dot(att.astype(BF16), v_ref[rows, cs])
                rt = jnp.where(lane == jb, rr[:QB], jnp.where(lane == 8 + jb, rr[QB:], rt))
                out.append((acc, rr + jnp.sum(lom, axis=1, keepdims=True, dtype=F32), rt))
            return tuple(out)

        carry = block(i, tuple((zero, zero[:, :1], zero[:QB]) for _ in cols), True)
        carry = lax.fori_loop(0, i, lambda jj, c: block(i - 1 - jj, c, False), carry)
        for p, cs in enumerate(cols):
            acc, _, rtile = carry[p]
            o = jnp.where(left, acc[:QB], acc[QB:])
            gate = g_ref[:, cs].astype(F32)
            o_ref[:, cs] = o.astype(BF16)
            og_ref[:, cs] = (o * gate * _sigmoid(gate)).astype(BF16)
            r_ref[p] = rtile
        x_wait(x_refs)

    rowblk = lambda b, g, i: (b * nq + i, g)
    return pl.pallas_call(
        body, name=name, grid=grid,
        in_specs=[pl.BlockSpec((QB, gw), lambda b, g, i: (b * nq + i, qb0 + g)),
                  pl.BlockSpec((s, gw), lambda b, g, i: (b, kb0 + g)),
                  pl.BlockSpec((s, gw), lambda b, g, i: (b, vb0 + g)),
                  pl.BlockSpec((QB, gw), lambda b, g, i: (b * nq + i, gb0 + g))] + x_in,
        out_specs=[pl.BlockSpec((QB, gw), rowblk), pl.BlockSpec((QB, gw), rowblk),
                   pl.BlockSpec((None, group, QB, 128), lambda b, g, i: (b * nq + i, g, 0, 0))] + x_out,
        out_shape=[jax.ShapeDtypeStruct((t, SB_W), BF16), jax.ShapeDtypeStruct((t, SB_W), BF16),
                   jax.ShapeDtypeStruct((bl * nq, SB_PAIRS, QB, 128), F32)] + x_shape,
        scratch_shapes=x_scratch,
        compiler_params=_params("arbitrary", "arbitrary", "arbitrary"),
    )(proj, proj, proj, proj, *x_args)


def sb_bwd(dsb, o, rsave, proj, bl, s, name, comm=None):
    t = bl * s
    nq = s // QB
    qb0, kb0, vb0, gb0 = OFF_QKV // SB_GW, (OFF_QKV + SB_W) // SB_GW, (OFF_QKV + 2 * SB_W) // SB_GW, OFF_SBG // SB_GW
    grid = (bl, SB_PAIRS // SB_GROUP, nq)
    x_in, x_args, x_out, x_shape, x_scratch, x_start, x_wait = _hosted_exchange(
        comm, grid, relay_at=(bl - 1, SB_PAIRS // SB_GROUP - 1, 0))

    def body(*refs):
        n = len(x_in)
        d_ref, o_ref, r_ref, q_ref, k_ref, v_ref, g_ref = refs[:7]
        dq_ref, dk_ref, dv_ref, dg_ref = refs[7 + n:11 + n]
        dk_acc, dv_acc = refs[11 + 2 * n:13 + 2 * n]
        x_refs = refs[7:7 + n] + refs[11 + n:11 + 2 * n] + refs[13 + 2 * n:]
        x_start(x_refs)
        i = pl.program_id(2)

        @pl.when(i == 0)
        def _():
            dk_acc[...] = jnp.zeros(dk_acc.shape, F32)
            dv_acc[...] = jnp.zeros(dv_acc.shape, F32)

        lane, left, causal = _sb_masks(i)
        upper = _tri((KB, KB), lambda r, c: r > c).astype(BF16)
        lower = _tri((KB, KB), lambda r, c: r < c).astype(BF16)
        cols = [slice(p * 128, (p + 1) * 128) for p in range(SB_GROUP)]
        qcats, docats = [], []
        for cs in cols:
            gate = g_ref[:, cs].astype(F32)
            sg = _sigmoid(gate)
            d = d_ref[:, cs].astype(F32)
            dg_ref[:, cs] = (d * o_ref[:, cs].astype(F32) * sg * (1.0 + gate * (1.0 - sg))).astype(BF16)
            docats.append(_stack_heads((d * gate * sg).astype(BF16), left))
            qcats.append(_stack_heads(q_ref[:, cs] * SB_SCALE, left))
        qcat_ts = [qc.astype(F32).T.astype(BF16) for qc in qcats]
        docat_ts = [dc.astype(F32).T.astype(BF16) for dc in docats]
        zero = qcats[0].astype(F32) * 0.0

        def block(jb, carry, diagonal):
            rows = pl.ds(pl.multiple_of(jb * KB, KB), KB)
            mask = causal(jb) if diagonal else None
            out = []
            for p, cs in enumerate(cols):
                dq, gcar = carry[p]
                kj = k_ref[rows, cs]
                vj = v_ref[rows, cs]
                rtile = r_ref[p]
                rr = jnp.concatenate(
                    [jnp.sum(jnp.where(lane == jb, rtile, 0.0), axis=1, keepdims=True),
                     jnp.sum(jnp.where(lane == 8 + jb, rtile, 0.0), axis=1, keepdims=True)], axis=0)
                lb, lom, later = _sb_block(qcats[p], kj, mask, rr, upper)
                att = jnp.exp(lb + later)
                if diagonal:
                    att = jnp.where(mask, att, 0.0)
                de = att * _dot(docats[p], vj, _NT)
                gpre = _dot(de.astype(BF16), lower) + gcar
                sig = jnp.exp(lb)
                dz = de * (1.0 - sig) - gpre * sig
                if diagonal:
                    dz = jnp.where(mask, dz, 0.0)
                dz = dz.astype(BF16)
                dk_acc[jb, cs, :] = dk_acc[jb, cs, :] + _dot(qcat_ts[p], dz)
                dv_acc[jb, cs, :] = dv_acc[jb, cs, :] + _dot(docat_ts[p], att.astype(BF16))
                out.append((dq + _dot(dz, kj), gcar + jnp.sum(de, axis=1, keepdims=True)))
            return tuple(out)

        carry = lax.fori_loop(0, i, lambda jb, c: block(jb, c, False), tuple((zero, zero[:, :1]) for _ in cols))
        carry = block(i, carry, True)
        for p, cs in enumerate(cols):
            dq = carry[p][0]
            dq_ref[:, cs] = (jnp.where(left, dq[:QB], dq[QB:]) * SB_SCALE).astype(BF16)

        @pl.when(i == nq - 1)
        def _():
            for kb in range(s // KB):
                for cs in cols:
                    dk_ref[kb * KB:(kb + 1) * KB, cs] = dk_acc[kb, cs, :].T.astype(BF16)
                    dv_ref[kb * KB:(kb + 1) * KB, cs] = dv_acc[kb, cs, :].T.astype(BF16)

        x_wait(x_refs)

    rowblk = lambda b, g, i: (b * nq + i, g)
    seqblk = lambda b, g, i: (b, g)
    return pl.pallas_call(
        body, name=name, grid=grid,
        in_specs=[pl.BlockSpec((QB, SB_GW), rowblk), pl.BlockSpec((QB, SB_GW), rowblk),
                  pl.BlockSpec((None, SB_GROUP, QB, 128), lambda b, g, i: (b * nq + i, g, 0, 0)),
                  pl.BlockSpec((QB, SB_GW), lambda b, g, i: (b * nq + i, qb0 + g)),
                  pl.BlockSpec((s, SB_GW), lambda b, g, i: (b, kb0 + g)),
                  pl.BlockSpec((s, SB_GW), lambda b, g, i: (b, vb0 + g)),
                  pl.BlockSpec((QB, SB_GW), lambda b, g, i: (b * nq + i, gb0 + g))] + x_in,
        out_specs=[pl.BlockSpec((QB, SB_GW), rowblk), pl.BlockSpec((s, SB_GW), seqblk),
                   pl.BlockSpec((s, SB_GW), seqblk), pl.BlockSpec((QB, SB_GW), rowblk)] + x_out,
        out_shape=[jax.ShapeDtypeStruct((t, SB_W), BF16)] * 4 + x_shape,
        scratch_shapes=[pltpu.VMEM((s // KB, SB_GW, KB), F32), pltpu.VMEM((s // KB, SB_GW, KB), F32)] + x_scratch,
        compiler_params=_params("arbitrary", "arbitrary", "arbitrary"),
    )(dsb, o, rsave, proj, proj, proj, proj, *x_args)


def merge_fwd(proj, ys, yp, yb, name):
    t = ys.shape[0]
    tr = min(512, t)

    def body(m_ref, ys_ref, yp_ref, yb_ref, o_ref):
        acc = jnp.zeros((tr, D), F32)
        for k, ref in enumerate((ys_ref, yp_ref, yb_ref)):
            acc = acc + _sigmoid(m_ref[:, k * D:(k + 1) * D].astype(F32)) * ref[...].astype(F32)
        o_ref[...] = acc.astype(BF16)

    rowblk = pl.BlockSpec((tr, D), lambda i: (i, 0))
    return pl.pallas_call(
        body, name=name, grid=(t // tr,),
        in_specs=[pl.BlockSpec((tr, 3 * D), lambda i: (i, 0)), rowblk, rowblk, rowblk],
        out_specs=rowblk,
        out_shape=jax.ShapeDtypeStruct((t, D), BF16),
        compiler_params=_params("parallel"),
    )(proj, ys, yp, yb)


def merge_bwd(dm, proj, ys, yp, yb, name):
    t = ys.shape[0]
    tr = min(512, t)

    def body(dm_ref, m_ref, ys_ref, yp_ref, yb_ref, d0_ref, d1_ref, d2_ref, dl_ref):
        dmv = dm_ref[...].astype(F32)
        for k, (ref, dref) in enumerate(((ys_ref, d0_ref), (yp_ref, d1_ref), (yb_ref, d2_ref))):
            g = _sigmoid(m_ref[:, k * D:(k + 1) * D].astype(F32))
            dref[...] = (g * dmv).astype(BF16)
            dl_ref[:, k * D:(k + 1) * D] = (dmv * ref[...].astype(F32) * g * (1.0 - g)).astype(BF16)

    rowblk = pl.BlockSpec((tr, D), lambda i: (i, 0))
    wide = pl.BlockSpec((tr, 3 * D), lambda i: (i, 0))
    return pl.pallas_call(
        body, name=name, grid=(t // tr,),
        in_specs=[rowblk, wide, rowblk, rowblk, rowblk],
        out_specs=[rowblk, rowblk, rowblk, wide],
        out_shape=[jax.ShapeDtypeStruct((t, D), BF16)] * 3 + [jax.ShapeDtypeStruct((t, 3 * D), BF16)],
        compiler_params=_params("parallel"),
    )(dm, proj, ys, yp, yb)


def layer_fwd(x, lw, bl, s, tag, comm=None):
    t = bl * s
    h = rmsnorm_fwd(x, lw["norm_w"], f"norm_fwd{tag}")
    proj = matmul(h, lw["w_in"], "nt", BF16, f"in_proj{tag}", tn=2048)
    xa = conv_fwd(proj, lw["conv_w"], lw["conv_b"], bl, s, f"conv_fwd{tag}")
    dtT, acT, dtC, acC = dt_fwd(proj, lw["dt_bias"], lw["a_log"], t, f"dt_fwd{tag}")
    dskip_l = jnp.repeat(lw["d_skip"], 64).reshape(1, SSM_W)
    y, prev = ssd_fwd(xa, acT, dtC, acC, dskip_l, bl, s, f"ssd_fwd{tag}")
    s_out = gnorm_fwd(y, proj, lw["ssm_norm_w"], f"gnorm_fwd{tag}")
    p_out = pool_fwd(proj, lw["pool_w"], lw["pool_scale"], bl, s, f"pool_fwd{tag}")
    sb_out, sb_o, sb_r, *carried = sb_fwd(proj, bl, s, f"sb_fwd{tag}", comm)
    ys = matmul(s_out, lw["w_proj_ssm"], "nn", BF16, f"proj_ssm{tag}")
    yp = matmul(p_out, lw["w_proj_pool"], "nn", BF16, f"proj_pool{tag}")
    yb = matmul(sb_out, lw["w_proj_sb"], "nn", BF16, f"proj_sb{tag}")
    merged = merge_fwd(proj, ys, yp, yb, f"merge_fwd{tag}")
    x_next = matmul(merged, lw["w_out"], "nn", F32, f"out_proj{tag}", residual=x)
    saved = dict(x=x, h=h, proj=proj, xa=xa, dtT=dtT, acT=acT, dtC=dtC, acC=acC, y=y, prev=prev, s_out=s_out, p_out=p_out,
                 sb_out=sb_out, sb_o=sb_o, sb_r=sb_r, ys=ys, yp=yp, yb=yb, merged=merged)
    return x_next, saved, (carried[0] if carried else None)


def layer_bwd(dx, dx_b, lw, sv, bl, s, tag, comm, own_slabs):
    t = bl * s
    g = {}
    dmerged = matmul(dx_b, lw["w_out"], "nt", BF16, f"d_merged{tag}")
    g["w_out"] = matmul(sv["merged"], dx_b, "tn", BF16, f"dw_out{tag}")
    dys, dyp, dyb, dlogit = merge_bwd(dmerged, sv["proj"], sv["ys"], sv["yp"], sv["yb"], f"merge_bwd{tag}")
    ds_out = matmul(dys, lw["w_proj_ssm"], "nt", BF16, f"d_sout{tag}")
    g["w_proj_ssm"] = matmul(sv["s_out"], dys, "tn", BF16, f"dw_proj_ssm{tag}")
    dp_out = matmul(dyp, lw["w_proj_pool"], "nt", BF16, f"d_pout{tag}")
    g["w_proj_pool"] = matmul(sv["p_out"], dyp, "tn", BF16, f"dw_proj_pool{tag}")
    dsb_out = matmul(dyb, lw["w_proj_sb"], "nt", BF16, f"d_sbout{tag}")
    g["w_proj_sb"] = matmul(sv["sb_out"], dyb, "tn", BF16, f"dw_proj_sb{tag}")
    dy, dz, dnw = gnorm_bwd(ds_out, sv["y"], sv["proj"], lw["ssm_norm_w"], f"gnorm_bwd{tag}")
    g["ssm_norm_w"] = dnw[0]
    dskip_l = jnp.repeat(lw["d_skip"], 64).reshape(1, SSM_W)
    dxs, db, dc, ddtT, dacT, dsk = ssd_bwd(dy, sv["xa"], sv["acT"], sv["dtC"], sv["acC"], dskip_l, sv["prev"], bl, s,
                                           f"ssd_bwd{tag}")
    g["d_skip"] = jnp.sum(dsk.reshape(N_HEADS, 64), axis=1)
    ddt_raw, da, dbias = dt_bwd(ddtT, dacT, sv["dtT"], sv["proj"], lw["dt_bias"], lw["a_log"], t, f"dt_bwd{tag}")
    g["a_log"] = da * (-jnp.exp(lw["a_log"]))
    g["dt_bias"] = dbias
    dxa = jnp.concatenate([dxs, db.astype(BF16), dc.astype(BF16)], axis=1)
    dxbc, dcw, dcb = conv_bwd(dxa, sv["proj"], lw["conv_w"], lw["conv_b"], bl, s, f"conv_bwd{tag}")
    g["conv_w"] = dcw
    g["conv_b"] = dcb
    dpu, dpg, dpw, dpsc = pool_bwd(dp_out, sv["proj"], lw["pool_w"], lw["pool_scale"], bl, s, f"pool_bwd{tag}")
    g["pool_w"] = dpw
    g["pool_scale"] = dpsc[0]
    dq, dk, dv, dsbg, *carried = sb_bwd(dsb_out, sv["sb_o"], sv["sb_r"], sv["proj"], bl, s, f"sb_bwd{tag}", comm)
    dproj = concat_columns([dlogit, dsbg, dpu, dpg, dz, dq, dk, dv, dxbc, ddt_raw], PC, f"d_proj{tag}")
    g["w_in"] = matmul(dproj, sv["h"], "tn", BF16, f"dw_in{tag}", tk=2048)
    own_comm = own_slabs(g) if own_slabs is not None else None
    dh = matmul(dproj, lw["w_in"], "nn", F32, f"d_h{tag}", tk=2048, comm=own_comm)
    dh, own_got = dh if own_comm is not None else (dh, None)
    dx_in, dx_in_b, dnorm = rmsnorm_bwd(dh, sv["x"], lw["norm_w"], dx, f"norm_bwd{tag}")
    g["norm_w"] = dnorm[0]
    return dx_in, dx_in_b, g, (carried[0] if carried else None), (own_comm[0] if own_comm else None), own_got


def concat_columns(parts, width, name):
    t = parts[0].shape[0]
    tr = min(256, t)
    widths = [p.shape[1] for p in parts]
    used = sum(widths)

    def body(*refs):
        o_ref = refs[-1]
        off = 0
        for ref, w in zip(refs[:-1], widths):
            o_ref[:, off:off + w] = ref[...]
            off += w
        if width > used:
            o_ref[:, used:] = jnp.zeros((tr, width - used), BF16)

    return pl.pallas_call(
        body, name=name, grid=(t // tr,),
        in_specs=[pl.BlockSpec((tr, w), lambda i: (i, 0)) for w in widths],
        out_specs=pl.BlockSpec((tr, width), lambda i: (i, 0)),
        out_shape=jax.ShapeDtypeStruct((t, width), BF16),
        compiler_params=_params("parallel"),
    )(*parts)


_PAD_PIECES = ((10784, 3072), (9760, 1024), (4640, 1024), (5664, 1024), (0, 2048), (6688, 3072), (2048, 2560), (4608, 32))
_UNPAD_PIECES = ((OFF_Z, 2048), (OFF_XBC, 2560), (OFF_DT, 32), (OFF_PU, 1024), (OFF_PG, 1024), (OFF_QKV, 3072),
                 (OFF_SBG, 1024), (OFF_MERGE, 3072))


def pad_rows(wt):
    pieces = [wt[o:o + n] for o, n in _PAD_PIECES]
    return jnp.concatenate(pieces + [jnp.zeros((PC - IN_COLS, wt.shape[1]), wt.dtype)], axis=0)


def unpad_rows(wp):
    return jnp.concatenate([wp[o:o + n] for o, n in _UNPAD_PIECES], axis=0)


MESH = pl.DeviceIdType.MESH
ANY = pl.BlockSpec(memory_space=pl.ANY)


def _coords():
    return lax.axis_index("x"), lax.axis_index("y"), lax.axis_index("c")


def _peer(p):
    x, y, c = _coords()
    return (1 - x if p & 4 else x, 1 - y if p & 2 else y, 1 - c if p & 1 else c)


def _flat(pos):
    return 4 * pos[0] + 2 * pos[1] + pos[2]


def _chip(pos):
    return 2 * pos[0] + pos[1]


def _exchange_copies(v_ref, out_ref, send_sems, recv_sems, local_sem, mode):
    x, y, c = _coords()
    me = _flat((x, y, c))
    sibling = (x, y, 1 - c)
    chips = [(1 - x if j & 2 else x, 1 - y if j & 1 else y) for j in range(1, 4)]

    def copy(k, src, landing, to):
        return pltpu.make_async_remote_copy(src_ref=src, dst_ref=out_ref.at[landing], send_sem=send_sems.at[k],
                                            recv_sem=recv_sems.at[k], device_id=to, device_id_type=MESH)

    if mode == "direct":
        local = pltpu.make_async_copy(v_ref, out_ref.at[me], local_sem)
        first = [copy(p - 1, v_ref, me, _peer(p)) for p in range(1, N_DEV)]
        last = [copy(p - 1, v_ref, _flat(_peer(p)), _peer(p)) for p in range(1, N_DEV)]
        return local, first, [], last
    if mode == "gather":
        local = pltpu.make_async_copy(v_ref, out_ref.at[me], local_sem)
        first = [copy(0, v_ref, me, sibling)] + [copy(1 + j, v_ref, me, (*ch, c)) for j, ch in enumerate(chips)]
        relay = [(copy(1 + j, v_ref, _flat((*ch, c)), (*ch, c)),
                  copy(4 + j, out_ref.at[_flat((*ch, c))], _flat((*ch, c)), sibling)) for j, ch in enumerate(chips)]
        last = [copy(0, v_ref, _flat(sibling), sibling)] + [
            copy(4 + j, v_ref, _flat((*ch, 1 - c)), sibling) for j, ch in enumerate(chips)]
        return local, first, relay, last
    assert mode == "chips"
    mine = _chip((x, y))
    local = pltpu.make_async_copy(v_ref.at[mine], out_ref.at[mine], local_sem)
    first = [copy(j, v_ref.at[_chip(ch)], mine, (*ch, c)) for j, ch in enumerate(chips)]
    last = [copy(j, v_ref.at[mine], _chip(ch), (*ch, c)) for j, ch in enumerate(chips)]
    return local, first, [], last


def _swap_copies(s0_ref, s1_ref, out_ref, send_sems, recv_sems, local_sem):
    x, y, c = _coords()

    def four(src_ref):
        return [pltpu.make_async_remote_copy(src_ref=src_ref.at[j], dst_ref=out_ref.at[j], send_sem=send_sems.at[j],
                                             recv_sem=recv_sems.at[j], device_id=(x, y, 1 - c), device_id_type=MESH)
                for j in range(4)]

    return c, four(s1_ref), four(s0_ref)


def _exchange_start(*refs_and_mode):
    if refs_and_mode[-1] == "swap":
        c, from_core0, from_core1 = _swap_copies(*refs_and_mode[:-1])
        for core, copies in ((0, from_core0), (1, from_core1)):
            @pl.when(c == core)
            def _():
                for cp in copies:
                    cp.start()
        return
    local, first, _, _ = _exchange_copies(*refs_and_mode)
    local.start()
    for cp in first:
        cp.start()


def _exchange_relay(*refs_and_mode):
    if refs_and_mode[-1] == "swap":
        return
    for arrival, onward in _exchange_copies(*refs_and_mode)[2]:
        arrival.wait_recv()
        onward.start()


def _exchange_finish(*refs_and_mode):
    if refs_and_mode[-1] == "swap":
        _, four, _ = _swap_copies(*refs_and_mode[:-1])
        for cp in four:
            cp.wait_recv()
        for cp in four:
            cp.wait_send()
        return
    local, first, relay, last = _exchange_copies(*refs_and_mode)
    for cp in last:
        cp.wait_recv()
    for cp in first + [onward for _, onward in relay]:
        cp.wait_send()
    local.wait()


def _exchange_shape(v, mode):
    if mode == "swap":
        return jax.ShapeDtypeStruct(tuple(v[0].shape), v[0].dtype)
    return jax.ShapeDtypeStruct(tuple(v.shape) if mode == "chips" else (N_DEV,) + tuple(v.shape), v.dtype)


def _exchange_sems():
    return [pltpu.SemaphoreType.DMA((N_DEV - 1,)), pltpu.SemaphoreType.DMA((N_DEV - 1,)), pltpu.SemaphoreType.DMA]


def exchange(v, mode, name):
    def body(*refs):
        _exchange_start(*refs, mode)
        _exchange_relay(*refs, mode)
        _exchange_finish(*refs, mode)

    args = list(v) if mode == "swap" else [v]
    return pl.pallas_call(
        body, name=name,
        in_specs=[ANY] * len(args), out_specs=ANY,
        out_shape=_exchange_shape(v, mode),
        scratch_shapes=_exchange_sems(),
    )(*args)


def _hosted_exchange(comm, grid, relay_at):
    if comm is None:
        return [], [], [], [], [], (lambda refs: None), (lambda refs: None)
    v, mode = comm

    def at(step):
        cond = None
        for axis, want in enumerate(step):
            term = pl.program_id(axis) == want
            cond = term if cond is None else jnp.logical_and(cond, term)
        return cond

    def start(refs):
        @pl.when(at([0] * len(grid)))
        def _():
            _exchange_start(*refs, mode)

        if mode == "gather":
            @pl.when(at(relay_at))
            def _():
                _exchange_relay(*refs, mode)

    def wait(refs):
        @pl.when(at([n - 1 for n in grid]))
        def _():
            _exchange_finish(*refs, mode)

    args = list(v) if mode == "swap" else [v]
    return [ANY] * len(args), args, [ANY], [_exchange_shape(v, mode)], _exchange_sems(), start, wait


def pair_sum(slabs, got, name):
    _, r, c = got.shape
    tr = r // 4 if r % 64 == 0 else r

    def body(s0_ref, s1_ref, got_ref, o_ref):
        mine = jnp.where(lax.axis_index("c") == 0, s0_ref[...].astype(F32), s1_ref[...].astype(F32))
        o_ref[...] = (mine + got_ref[...].astype(F32)).astype(BF16)

    blk = pl.BlockSpec((None, tr, c), lambda j, i: (j, i, 0))
    return pl.pallas_call(
        body, name=name, grid=(4, r // tr),
        in_specs=[blk, blk, blk], out_specs=blk,
        out_shape=jax.ShapeDtypeStruct(got.shape, BF16),
        compiler_params=_params("parallel", "parallel"),
    )(*slabs, got)


def sum_slabs(v, name):
    n, r, c = v.shape
    tr = 128 if r % 128 == 0 else r

    def body(v_ref, o_ref):
        acc = v_ref[0].astype(F32)
        for k in range(1, n):
            acc = acc + v_ref[k].astype(F32)
        o_ref[...] = acc

    return pl.pallas_call(
        body, name=name, grid=(r // tr,),
        in_specs=[pl.BlockSpec((n, tr, c), lambda i: (0, i, 0))],
        out_specs=pl.BlockSpec((tr, c), lambda i: (i, 0)),
        out_shape=jax.ShapeDtypeStruct((r, c), F32),
        compiler_params=_params("parallel"),
    )(v)


def adamw(w, g, m, v, name):
    r, c = w.shape
    tr = next((cand for cand in (256, 128, 64, 32, 16, 8) if r % cand == 0), r)

    def body(w_ref, g_ref, m_ref, v_ref, d_ref, mo_ref, vo_ref):
        gv = g_ref[...]
        mn = ADAM_B1 * m_ref[...] + (1.0 - ADAM_B1) * gv
        vn = ADAM_B2 * v_ref[...] + (1.0 - ADAM_B2) * (gv * gv)
        m_hat = mn / (1.0 - ADAM_B1 ** ADAM_STEP)
        v_hat = vn / (1.0 - ADAM_B2 ** ADAM_STEP)
        d_ref[...] = -ADAM_LR * (m_hat / (jnp.sqrt(v_hat) + ADAM_EPS) + ADAM_WD * w_ref[...])
        mo_ref[...] = mn
        vo_ref[...] = vn

    blk = pl.BlockSpec((tr, c), lambda i: (i, 0))
    return pl.pallas_call(
        body, name=name, grid=(r // tr,),
        in_specs=[blk] * 4, out_specs=[blk] * 3,
        out_shape=[jax.ShapeDtypeStruct((r, c), F32)] * 3,
        compiler_params=_params("parallel"),
    )(w, g, m, v)


BIG = ("w_proj_ssm", "w_proj_pool", "w_proj_sb", "w_out", "pool_w", "w_in")
SHARD_IN = IN_COLS // N_DEV
BIG_ROWS = {"w_proj_ssm": SSM_W // N_DEV, "w_proj_pool": POOL_W // N_DEV, "w_proj_sb": SB_W // N_DEV,
            "w_out": D // N_DEV, "pool_w": POOL_G * (POOL_GD // N_DEV) * POOL_GD // D, "w_in": SHARD_IN}
PACK_C = D
PACK_R = 2432

REPLICATED = ("norm_w", "conv_b", "dt_bias", "a_log", "d_skip", "ssm_norm_w", "pool_scale")
WEIGHTS = ("norm_w", "w_in", "conv_w", "conv_b", "dt_bias", "a_log", "d_skip", "ssm_norm_w", "pool_w",
           "pool_scale", "w_proj_ssm", "w_proj_pool", "w_proj_sb", "w_out", "final_norm_w")


def _size(shape):
    n = 1
    for d in shape:
        n *= d
    return n


def _pad_flat(flat, n):
    return jnp.concatenate([flat, jnp.zeros((n - flat.shape[0],), flat.dtype)])


def _row_offsets():
    offs, off = {}, 0
    for n in BIG:
        offs[n] = off
        off += BIG_ROWS[n]
    return offs, off


def pack_shards(parts):
    rows = [parts[n].reshape(BIG_ROWS[n], PACK_C) for n in BIG]
    rows[-1] = jnp.pad(rows[-1], ((0, PACK_R - _row_offsets()[1]), (0, 0)))
    return jnp.concatenate(rows, axis=0)


def unpack_shards(packed):
    offs, _ = _row_offsets()
    out = {}
    for n in BIG:
        seg = packed[offs[n]:offs[n] + BIG_ROWS[n]]
        if n == "w_in":
            out[n] = seg.T
        elif n == "pool_w":
            out[n] = seg.reshape(POOL_G, POOL_GD // N_DEV, POOL_GD)
        else:
            out[n] = seg
    return out


def unpack_gathered(g):
    offs, _ = _row_offsets()
    out = {}
    for n in BIG:
        seg = g[:, offs[n]:offs[n] + BIG_ROWS[n], :]
        if n == "w_in":
            out[n] = pad_rows(seg.reshape(IN_COLS, D))
        elif n == "pool_w":
            out[n] = seg.reshape(N_DEV, POOL_G, POOL_GD // N_DEV, POOL_GD).transpose(1, 0, 2, 3).reshape(
                POOL_G, POOL_GD, POOL_GD)
        else:
            out[n] = seg.reshape(N_DEV * BIG_ROWS[n], D)
    return out


def pack_slabs(g):
    segs = []
    for n in BIG:
        if n == "w_in":
            w = unpad_rows(g[n])
        elif n == "pool_w":
            w = g[n].reshape(POOL_G, N_DEV, POOL_GD // N_DEV, POOL_GD).transpose(1, 0, 2, 3)
        else:
            w = g[n]
        segs.append(w.reshape(N_DEV // 2, 2, BIG_ROWS[n], PACK_C).astype(BF16))
    segs[-1] = jnp.pad(segs[-1], ((0, 0), (0, 0), (0, PACK_R - _row_offsets()[1]), (0, 0)))
    return tuple(jnp.concatenate([seg[:, core] for seg in segs], axis=1) for core in range(2))


SMALL_ROWS = 544


def pack_small(vals):
    flat = jnp.concatenate([v.reshape(-1) for v in vals])
    return _pad_flat(flat, SMALL_ROWS * 128).reshape(SMALL_ROWS, 128)


def unpack_small(packed, shapes):
    flat = packed.reshape(-1)
    out, off = [], 0
    for shp in shapes:
        out.append(flat[off:off + _size(shp)].reshape(shp))
        off += _size(shp)
    return out


def kernel(x, norm_w, w_in, conv_w, conv_b, dt_bias, a_log, d_skip, ssm_norm_w, pool_w, pool_scale, w_proj_ssm, w_proj_pool, w_proj_sb, w_out, final_norm_w, loss_target, m_norm_w, m_w_in, m_conv_w, m_conv_b, m_dt_bias, m_a_log, m_d_skip, m_ssm_norm_w, m_pool_w, m_pool_scale, m_w_proj_ssm, m_w_proj_pool, m_w_proj_sb, m_w_out, m_final_norm_w, v_norm_w, v_w_in, v_conv_w, v_conv_b, v_dt_bias, v_a_log, v_d_skip, v_ssm_norm_w, v_pool_w, v_pool_scale, v_w_proj_ssm, v_w_proj_pool, v_w_proj_sb, v_w_out, v_final_norm_w):
    wts = dict(norm_w=norm_w, w_in=w_in, conv_w=conv_w, conv_b=conv_b, dt_bias=dt_bias, a_log=a_log, d_skip=d_skip,
               ssm_norm_w=ssm_norm_w, pool_w=pool_w, pool_scale=pool_scale, w_proj_ssm=w_proj_ssm,
               w_proj_pool=w_proj_pool, w_proj_sb=w_proj_sb, w_out=w_out, final_norm_w=final_norm_w)
    mom = dict(norm_w=m_norm_w, w_in=m_w_in, conv_w=m_conv_w, conv_b=m_conv_b, dt_bias=m_dt_bias, a_log=m_a_log,
               d_skip=m_d_skip, ssm_norm_w=m_ssm_norm_w, pool_w=m_pool_w, pool_scale=m_pool_scale,
               w_proj_ssm=m_w_proj_ssm, w_proj_pool=m_w_proj_pool, w_proj_sb=m_w_proj_sb, w_out=m_w_out,
               final_norm_w=m_final_norm_w)
    var = dict(norm_w=v_norm_w, w_in=v_w_in, conv_w=v_conv_w, conv_b=v_conv_b, dt_bias=v_dt_bias, a_log=v_a_log,
               d_skip=v_d_skip, ssm_norm_w=v_ssm_norm_w, pool_w=v_pool_w, pool_scale=v_pool_scale,
               w_proj_ssm=v_w_proj_ssm, w_proj_pool=v_w_proj_pool, w_proj_sb=v_w_proj_sb, w_out=v_w_out,
               final_norm_w=v_final_norm_w)
    bl, s, _ = x.shape
    t = bl * s
    me = _flat(_coords())

    cw = exchange(conv_w.reshape(40, 128), "direct", "gather_conv_w")
    conv_w_full = cw.reshape(N_DEV, DEPTH, CONV_K, CONV_CH // N_DEV).transpose(1, 2, 0, 3).reshape(
        DEPTH, CONV_K, CONV_CH)

    xc = x.reshape(t, D)
    layer_w, saved = [], []
    packed = [pack_shards({n: (wts[n][l].T if n == "w_in" else wts[n][l]).astype(BF16) for n in BIG})
              for l in range(DEPTH)]
    gathered = exchange(packed[0], "gather", "gather_w0")
    for l in range(DEPTH):
        lw = unpack_gathered(gathered)
        for n in REPLICATED:
            lw[n] = wts[n][l]
        lw["conv_w"] = conv_w_full[l]
        xc, sv, gathered = layer_fwd(xc, lw, bl, s, f"_l{l}", (packed[l + 1], "gather") if l + 1 < DEPTH else None)
        layer_w.append(lw)
        saved.append(sv)

    loss_part, dx, dx_b, dfinal = final_loss(xc, final_norm_w, loss_target.reshape(t, D), "final_loss")
    loss = lax.psum(loss_part[0, 0], ("x", "y", "c"))

    grads = [None] * DEPTH
    big_sum = [None] * DEPTH
    def last_layer_slabs(g):
        slabs = pack_slabs(g)
        return pair_sum(slabs, exchange(slabs, "swap", "pair_swap0"), "pair_sum0"), "chips"

    chip_sums = None
    for l in reversed(range(DEPTH)):
        dx, dx_b, g, got, own_sent, own_got = layer_bwd(
            dx, dx_b, layer_w[l], saved[l], bl, s, f"_l{l}", (chip_sums, "chips") if chip_sums is not None else None,
            last_layer_slabs if l == 0 else (lambda g: (pack_slabs(g), "swap")))
        if got is not None:
            big_sum[l + 1] = unpack_shards(sum_slabs(got, f"sum_g{l + 1}"))
        grads[l] = g
        if l > 0:
            chip_sums = pair_sum(own_sent, own_got, f"pair_sum{l}")
    big_sum[0] = unpack_shards(sum_slabs(own_got, "sum_g0"))
    grad_x = dx.reshape(bl, s, D)

    small_names = REPLICATED + ("conv_w",)
    small_vals = [jnp.stack([grads[l][n] for l in range(DEPTH)]) for n in small_names] + [dfinal[0]]
    small_shapes = [v.shape for v in small_vals]
    small_all = exchange(pack_small(small_vals), "direct", "gather_small")
    small_sum = unpack_small(sum_slabs(small_all, "sum_small"), small_shapes)
    gsum = dict(zip(small_names + ("final_norm_w",), small_sum))
    conv_g_full = gsum["conv_w"]
    gsum["conv_w"] = lax.dynamic_slice_in_dim(conv_g_full, me * (CONV_CH // N_DEV), CONV_CH // N_DEV, axis=2)
    for n in BIG:
        gsum[n] = jnp.stack([big_sum[l][n] for l in range(DEPTH)])

    delta, new_m, new_v = {}, {}, {}
    for n in BIG + ("conv_w",):
        shp = wts[n].shape
        two_d = (-1, shp[-1])
        d2, m2, v2 = adamw(wts[n].reshape(two_d), gsum[n].reshape(two_d), mom[n].reshape(two_d),
                           var[n].reshape(two_d), f"adamw_{n}")
        delta[n], new_m[n], new_v[n] = d2.reshape(shp), m2.reshape(shp), v2.reshape(shp)
    rep = REPLICATED + ("final_norm_w",)
    rep_shapes = [wts[n].shape for n in rep]
    d2, m2, v2 = adamw(pack_small([wts[n] for n in rep]), pack_small([gsum[n] for n in rep]),
                       pack_small([mom[n] for n in rep]), pack_small([var[n] for n in rep]), "adamw_small")
    for n, dv, mv, vv in zip(rep, unpack_small(d2, rep_shapes), unpack_small(m2, rep_shapes),
                             unpack_small(v2, rep_shapes)):
        delta[n], new_m[n], new_v[n] = dv, mv, vv

    return (loss, grad_x, *[gsum[n] for n in WEIGHTS], *[delta[n] for n in WEIGHTS],
            *[new_m[n] for n in WEIGHTS], *[new_v[n] for n in WEIGHTS])
```

```python
import functools

import jax
import jax.numpy as jnp
from jax import lax
from jax.experimental import pallas as pl
from jax.experimental.pallas import tpu as pltpu

F32 = jnp.float32
BF16 = jnp.bfloat16

N_DEV = 8
DEPTH = 4
D = 1024
SSM_W = 2048
N_HEADS = 32
N_PAIRS = 16
N_GROUPS = 2
N_STATE = 128
CHUNK = 128
CONV_CH = 2560
CONV_K = 4
POOL_W = 1024
POOL_G = 4
POOL_GD = 256
SB_W = 1024
SB_PAIRS = 8
QB = 256
EPS = 1e-6
IN_COLS = 13856

PC = 14336
OFF_MERGE = 0
OFF_SBG = 3072
OFF_PU = 4096
OFF_PG = 5120
OFF_Z = 6144
OFF_QKV = 8192
OFF_XBC = 11264
OFF_DT = 13824

ADAM_LR = 0.001
ADAM_B1 = 0.9
ADAM_B2 = 0.999
ADAM_EPS = 1e-08
ADAM_WD = 0.01
ADAM_STEP = 10

VMEM_LIMIT = 56 * 1024 * 1024

_NN = (((1,), (0,)), ((), ()))
_NT = (((1,), (1,)), ((), ()))
_TN = (((0,), (0,)), ((), ()))


def _dot(a, b, dn=_NN):
    return lax.dot_general(a, b, dn, preferred_element_type=F32)


def _sigmoid(x):
    return 1.0 / (1.0 + jnp.exp(-x))


def _softplus(x):
    return jnp.maximum(x, 0.0) + jnp.log(1.0 + jnp.exp(-jnp.abs(x)))


def _split2(x):
    hi = x.astype(BF16)
    lo = (x - hi.astype(F32)).astype(BF16)
    return hi, lo


def _split3(x):
    hi = x.astype(BF16)
    r = x - hi.astype(F32)
    mid = r.astype(BF16)
    lo = (r - mid.astype(F32)).astype(BF16)
    return hi, mid, lo


def _params(*sem):
    return pltpu.CompilerParams(dimension_semantics=sem, vmem_limit_bytes=VMEM_LIMIT)


def matmul(a, b, mode, out_dtype, name, residual=None, tm=1024, tn=1024, tk=1024, comm=None):
    if mode == "nn":
        (m, k), (k2, n) = a.shape, b.shape
    elif mode == "nt":
        (m, k), (n, k2) = a.shape, b.shape
    else:
        (k, m), (k2, n) = a.shape, b.shape
    assert k == k2
    tm, tn, tk = min(tm, m), min(tn, n), min(tk, k)
    assert m % tm == 0 and n % tn == 0 and k % tk == 0
    nk = k // tk
    dn = {"nn": _NN, "nt": _NT, "tn": _TN}[mode]
    a_spec = pl.BlockSpec((tk, tm), lambda i, j, kk: (kk, i)) if mode == "tn" else pl.BlockSpec((tm, tk), lambda i, j, kk: (i, kk))
    b_spec = pl.BlockSpec((tn, tk), lambda i, j, kk: (j, kk)) if mode == "nt" else pl.BlockSpec((tk, tn), lambda i, j, kk: (kk, j))
    in_specs = [a_spec, b_spec]
    args = [a, b]
    if residual is not None:
        in_specs.append(pl.BlockSpec((tm, tn), lambda i, j, kk: (i, j)))
        args.append(residual)
    grid = (m // tm, n // tn, nk)
    n_in = len(args)
    x_in, x_args, x_out, x_shape, x_scratch, x_start, x_wait = _hosted_exchange(comm, grid, relay_at=None)

    def body(*refs):
        n_xi, n_xo = len(x_in), len(x_out)
        a_ref, b_ref = refs[:2]
        r_ref = refs[2] if residual is not None else None
        o_ref = refs[n_in + n_xi]
        acc_ref = refs[n_in + n_xi + n_xo + 1]
        x_refs = refs[n_in:n_in + n_xi] + refs[n_in + n_xi + 1:n_in + n_xi + n_xo + 1] + refs[n_in + n_xi + n_xo + 2:]
        x_start(x_refs)
        kk = pl.program_id(2)
        p = _dot(a_ref[...], b_ref[...], dn)

        def finish(val):
            if r_ref is not None:
                val = val + r_ref[...]
            o_ref[...] = val.astype(out_dtype)

        if nk == 1:
            finish(p)
        else:
            @pl.when(kk == 0)
            def _():
                acc_ref[...] = p

            @pl.when(kk > 0)
            def _():
                acc_ref[...] += p

            @pl.when(kk == nk - 1)
            def _():
                finish(acc_ref[...])

        x_wait(x_refs)

    out = pl.pallas_call(
        body, name=name,
        grid=grid,
        in_specs=in_specs + x_in,
        out_specs=[pl.BlockSpec((tm, tn), lambda i, j, kk: (i, j))] + x_out,
        out_shape=[jax.ShapeDtypeStruct((m, n), out_dtype)] + x_shape,
        scratch_shapes=[pltpu.VMEM((tm, tn) if nk > 1 else (8, 128), F32)] + x_scratch,
        compiler_params=_params(*(("arbitrary",) * 3 if comm is not None else ("parallel", "parallel", "arbitrary"))),
    )(*args, *x_args)
    return tuple(out) if comm is not None else out[0]


def rmsnorm_fwd(x, w, name):
    t, d = x.shape
    tr = min(512, t)

    def body(x_ref, w_ref, h_ref):
        xv = x_ref[...]
        r = lax.rsqrt(jnp.mean(xv * xv, axis=-1, keepdims=True) + EPS)
        h_ref[...] = (xv * r * w_ref[...]).astype(BF16)

    return pl.pallas_call(
        body, name=name, grid=(t // tr,),
        in_specs=[pl.BlockSpec((tr, d), lambda i: (i, 0)), pl.BlockSpec((1, d), lambda i: (0, 0))],
        out_specs=pl.BlockSpec((tr, d), lambda i: (i, 0)),
        out_shape=jax.ShapeDtypeStruct((t, d), BF16),
        compiler_params=_params("parallel"),
    )(x, w.reshape(1, d))


def rmsnorm_bwd(dh, x, w, dres, name):
    t, d = x.shape
    tr = min(512, t)

    def body(dh_ref, x_ref, w_ref, dres_ref, dx_ref, dxb_ref, dw_ref):
        xv = x_ref[...]
        r = lax.rsqrt(jnp.mean(xv * xv, axis=-1, keepdims=True) + EPS)
        xh = xv * r
        g = dh_ref[...].astype(F32)
        dxh = g * w_ref[...]
        dxv = dres_ref[...] + r * (dxh - xh * jnp.mean(dxh * xh, axis=-1, keepdims=True))
        dx_ref[...] = dxv
        dxb_ref[...] = dxv.astype(BF16)
        part = jnp.sum(g * xh, axis=0, keepdims=True)

        @pl.when(pl.program_id(0) == 0)
        def _():
            dw_ref[...] = part

        @pl.when(pl.program_id(0) > 0)
        def _():
            dw_ref[...] += part

    return pl.pallas_call(
        body, name=name, grid=(t // tr,),
        in_specs=[pl.BlockSpec((tr, d), lambda i: (i, 0)), pl.BlockSpec((tr, d), lambda i: (i, 0)),
                  pl.BlockSpec((1, d), lambda i: (0, 0)), pl.BlockSpec((tr, d), lambda i: (i, 0))],
        out_specs=[pl.BlockSpec((tr, d), lambda i: (i, 0)), pl.BlockSpec((tr, d), lambda i: (i, 0)),
                   pl.BlockSpec((1, d), lambda i: (0, 0))],
        out_shape=[jax.ShapeDtypeStruct((t, d), F32), jax.ShapeDtypeStruct((t, d), BF16),
                   jax.ShapeDtypeStruct((1, d), F32)],
        compiler_params=_params("arbitrary"),
    )(dh, x, w.reshape(1, d), dres)


def final_loss(x, w, target, name):
    t, d = x.shape
    tr = min(512, t)

    def body(x_ref, w_ref, tg_ref, loss_ref, dx_ref, dxb_ref, dw_ref):
        xv = x_ref[...]
        r = lax.rsqrt(jnp.mean(xv * xv, axis=-1, keepdims=True) + EPS)
        xh = xv * r
        err = xh * w_ref[...] - tg_ref[...]
        lpart = 0.5 * jnp.sum(jnp.mean(err * err, axis=-1, keepdims=True), axis=0, keepdims=True)
        dy = err * (1.0 / d)
        dxh = dy * w_ref[...]
        dxv = r * (dxh - xh * jnp.mean(dxh * xh, axis=-1, keepdims=True))
        dx_ref[...] = dxv
        dxb_ref[...] = dxv.astype(BF16)
        part = jnp.sum(dy * xh, axis=0, keepdims=True)

        @pl.when(pl.program_id(0) == 0)
        def _():
            dw_ref[...] = part
            loss_ref[...] = jnp.broadcast_to(lpart, (1, 128))

        @pl.when(pl.program_id(0) > 0)
        def _():
            dw_ref[...] += part
            loss_ref[...] += jnp.broadcast_to(lpart, (1, 128))

    return pl.pallas_call(
        body, name=name, grid=(t // tr,),
        in_specs=[pl.BlockSpec((tr, d), lambda i: (i, 0)), pl.BlockSpec((1, d), lambda i: (0, 0)),
                  pl.BlockSpec((tr, d), lambda i: (i, 0))],
        out_specs=[pl.BlockSpec((1, 128), lambda i: (0, 0)), pl.BlockSpec((tr, d), lambda i: (i, 0)),
                   pl.BlockSpec((tr, d), lambda i: (i, 0)), pl.BlockSpec((1, d), lambda i: (0, 0))],
        out_shape=[jax.ShapeDtypeStruct((1, 128), F32), jax.ShapeDtypeStruct((t, d), F32),
                   jax.ShapeDtypeStruct((t, d), BF16), jax.ShapeDtypeStruct((1, d), F32)],
        compiler_params=_params("arbitrary"),
    )(x, w.reshape(1, d), target)


CONV_BW = 256


def _shift_down(u, s, row):
    return jnp.where(row >= s, pltpu.roll(u, s, axis=0), 0.0)


def _shift_up(u, s, row, n):
    return jnp.where(row < n - s, pltpu.roll(u, n - s, axis=0), 0.0)


def _conv_pre(u, w, b, row):
    acc = b + w[CONV_K - 1:CONV_K, :] * u
    for k in range(CONV_K - 1):
        acc = acc + w[k:k + 1, :] * _shift_down(u, CONV_K - 1 - k, row)
    return acc


def conv_fwd(proj, conv_w, conv_b, bl, s, name):
    t = bl * s
    nb = CONV_CH // CONV_BW
    off = OFF_XBC // CONV_BW

    def body(u_ref, w_ref, b_ref, o_ref):
        u = u_ref[...].astype(F32)
        row = lax.broadcasted_iota(jnp.int32, u.shape, 0)
        xc = _conv_pre(u, w_ref[...], b_ref[...], row)
        o_ref[...] = (xc * _sigmoid(xc)).astype(BF16)

    return pl.pallas_call(
        body, name=name, grid=(bl, nb),
        in_specs=[pl.BlockSpec((s, CONV_BW), lambda b, j: (b, off + j)),
                  pl.BlockSpec((CONV_K, CONV_BW), lambda b, j: (0, j)),
                  pl.BlockSpec((1, CONV_BW), lambda b, j: (0, j))],
        out_specs=pl.BlockSpec((s, CONV_BW), lambda b, j: (b, j)),
        out_shape=jax.ShapeDtypeStruct((t, CONV_CH), BF16),
        compiler_params=_params("parallel", "parallel"),
    )(proj, conv_w, conv_b.reshape(1, CONV_CH))


def conv_bwd(dxa, proj, conv_w, conv_b, bl, s, name):
    t = bl * s
    nb = CONV_CH // CONV_BW
    off = OFF_XBC // CONV_BW

    def body(d_ref, u_ref, w_ref, b_ref, du_ref, dw_ref, db_ref):
        u = u_ref[...].astype(F32)
        w = w_ref[...]
        row = lax.broadcasted_iota(jnp.int32, u.shape, 0)
        xc = _conv_pre(u, w, b_ref[...], row)
        sg = _sigmoid(xc)
        dxc = d_ref[...].astype(F32) * sg * (1.0 + xc * (1.0 - sg))
        du = w[CONV_K - 1:CONV_K, :] * dxc
        dws = [None] * CONV_K
        dws[CONV_K - 1] = jnp.sum(dxc * u, axis=0, keepdims=True)
        for k in range(CONV_K - 1):
            up = _shift_up(dxc, CONV_K - 1 - k, row, s)
            du = du + w[k:k + 1, :] * up
            dws[k] = jnp.sum(up * u, axis=0, keepdims=True)
        du_ref[...] = du.astype(BF16)
        krow = lax.broadcasted_iota(jnp.int32, (8, CONV_BW), 0)
        dwv = sum(jnp.where(krow == k, dws[k], 0.0) for k in range(CONV_K))
        dbv = jnp.sum(dxc, axis=0, keepdims=True)

        @pl.when(pl.program_id(1) == 0)
        def _():
            dw_ref[...] = dwv
            db_ref[...] = dbv

        @pl.when(pl.program_id(1) > 0)
        def _():
            dw_ref[...] += dwv
            db_ref[...] += dbv

    du, dw, db = pl.pallas_call(
        body, name=name, grid=(nb, bl),
        in_specs=[pl.BlockSpec((s, CONV_BW), lambda j, b: (b, j)),
                  pl.BlockSpec((s, CONV_BW), lambda j, b: (b, off + j)),
                  pl.BlockSpec((CONV_K, CONV_BW), lambda j, b: (0, j)),
                  pl.BlockSpec((1, CONV_BW), lambda j, b: (0, j))],
        out_specs=[pl.BlockSpec((s, CONV_BW), lambda j, b: (b, j)),
                   pl.BlockSpec((8, CONV_BW), lambda j, b: (0, j)),
                   pl.BlockSpec((1, CONV_BW), lambda j, b: (0, j))],
        out_shape=[jax.ShapeDtypeStruct((t, CONV_CH), BF16), jax.ShapeDtypeStruct((8, CONV_CH), F32),
                   jax.ShapeDtypeStruct((1, CONV_CH), F32)],
        compiler_params=_params("parallel", "arbitrary"),
    )(dxa, proj, conv_w, conv_b.reshape(1, CONV_CH))
    return du, dw[:CONV_K], db[0]


def _tri(shape, cmp):
    r = lax.broadcasted_iota(jnp.int32, shape, 0)
    c = lax.broadcasted_iota(jnp.int32, shape, 1)
    return cmp(r, c)


def dt_fwd(proj, dt_bias, a_log, t, name):
    nchunks = t // CHUNK
    bias = jnp.zeros((1, 128), F32).at[0, :N_HEADS].set(dt_bias)
    alog = jnp.zeros((1, 128), F32).at[0, :N_HEADS].set(a_log)

    def body(raw_ref, b_ref, al_ref, dt_ref, ac_ref, dtl_ref, acl_ref):
        raw = raw_ref[...].astype(F32)
        dt = _softplus(raw + b_ref[...])
        adt = dt * (-jnp.exp(al_ref[...]))
        low = _tri((CHUNK, CHUNK), lambda r, c: r >= c).astype(BF16)
        acum = sum(_dot(low, part) for part in _split3(adt))
        dt_ref[...] = dt.T[:N_HEADS]
        ac_ref[...] = acum.T[:N_HEADS]
        spread = _tri((128, SSM_W), lambda h, lane: lane // 64 == h).astype(BF16)
        dtl_ref[...] = sum(_dot(part, spread) for part in _split2(dt))
        acl_ref[...] = sum(_dot(part, spread) for part in _split3(acum))

    return pl.pallas_call(
        body, name=name, grid=(nchunks,),
        in_specs=[pl.BlockSpec((CHUNK, 128), lambda i: (i, OFF_DT // 128)),
                  pl.BlockSpec((1, 128), lambda i: (0, 0)), pl.BlockSpec((1, 128), lambda i: (0, 0))],
        out_specs=[pl.BlockSpec((None, N_HEADS, CHUNK), lambda i: (i, 0, 0))] * 2
        + [pl.BlockSpec((CHUNK, SSM_W), lambda i: (i, 0))] * 2,
        out_shape=[jax.ShapeDtypeStruct((nchunks, N_HEADS, CHUNK), F32)] * 2
        + [jax.ShapeDtypeStruct((t, SSM_W), F32)] * 2,
        compiler_params=_params("parallel"),
    )(proj, bias, alog)


def dt_bwd(ddtT, dacT, dtT, proj, dt_bias, a_log, t, name):
    nchunks = t // CHUNK
    bias = dt_bias.reshape(N_HEADS, 1)
    alog = a_log.reshape(N_HEADS, 1)

    def body(ddt_ref, dac_ref, dt_ref, raw_ref, b_ref, al_ref, draw_ref, da_ref, db_ref):
        a = -jnp.exp(al_ref[...])
        upp = _tri((CHUNK, CHUNK), lambda r, c: r >= c).astype(BF16)
        dadt = sum(_dot(part, upp) for part in _split3(dac_ref[...]))
        ddt = ddt_ref[...] + dadt * a
        rawT = raw_ref[...].astype(F32).T[:N_HEADS]
        draw = ddt * _sigmoid(rawT + b_ref[...])
        padded = jnp.concatenate([draw, jnp.zeros((128 - N_HEADS, CHUNK), F32)], axis=0)
        draw_ref[...] = padded.T.astype(BF16)
        dav = dadt * dt_ref[...]

        @pl.when(pl.program_id(0) == 0)
        def _():
            da_ref[...] = dav
            db_ref[...] = draw

        @pl.when(pl.program_id(0) > 0)
        def _():
            da_ref[...] += dav
            db_ref[...] += draw

    draw, da, db = pl.pallas_call(
        body, name=name, grid=(nchunks,),
        in_specs=[pl.BlockSpec((None, N_HEADS, CHUNK), lambda i: (i, 0, 0))] * 3
        + [pl.BlockSpec((CHUNK, 128), lambda i: (i, OFF_DT // 128)),
           pl.BlockSpec((N_HEADS, 1), lambda i: (0, 0)), pl.BlockSpec((N_HEADS, 1), lambda i: (0, 0))],
        out_specs=[pl.BlockSpec((CHUNK, 128), lambda i: (i, 0)),
                   pl.BlockSpec((N_HEADS, CHUNK), lambda i: (0, 0)), pl.BlockSpec((N_HEADS, CHUNK), lambda i: (0, 0))],
        out_shape=[jax.ShapeDtypeStruct((t, 128), BF16), jax.ShapeDtypeStruct((N_HEADS, CHUNK), F32),
                   jax.ShapeDtypeStruct((N_HEADS, CHUNK), F32)],
        compiler_params=_params("arbitrary"),
    )(ddtT, dacT, dtT, proj, bias, alog)
    return draw, jnp.sum(da, axis=1), jnp.sum(db, axis=1)


PAIRS_G = N_PAIRS // N_GROUPS
GROUP_W = PAIRS_G * 128


def _ssd_pair(x, dtl, acl, acr, tri):
    left = lax.broadcasted_iota(jnp.int32, (CHUNK, 128), 1) < 64
    swapped = pltpu.roll(acl, 64, axis=1)
    ac_cols = [jnp.where(left, acl, swapped), jnp.where(left, swapped, acl)]
    dks = [jnp.exp(jnp.where(tri, ac_cols[e] - acr[e:e + 1], -1e30)) for e in range(2)]
    aclast = acl[CHUNK - 1:CHUNK, :]
    return left, dtl, acl, x * dtl, dks, aclast


def _ssd_specs(nc, rev):
    row = (lambda b, c, g: b * nc + (nc - 1 - c)) if rev else (lambda b, c, g: b * nc + c)
    return dict(
        wide=pl.BlockSpec((CHUNK, GROUP_W), lambda b, c, g: (row(b, c, g), g)),
        bmat=pl.BlockSpec((CHUNK, 128), lambda b, c, g: (row(b, c, g), SSM_W // 128 + g)),
        cmat=pl.BlockSpec((CHUNK, 128), lambda b, c, g: (row(b, c, g), SSM_W // 128 + N_GROUPS + g)),
        rows2=pl.BlockSpec((None, PAIRS_G, 2, CHUNK), lambda b, c, g: (row(b, c, g), g, 0, 0)),
        cols=pl.BlockSpec((CHUNK, GROUP_W), lambda b, c, g: (row(b, c, g), g)),
        rows8=pl.BlockSpec((None, PAIRS_G, 8, CHUNK), lambda b, c, g: (row(b, c, g), g, 0, 0)),
        dskip=pl.BlockSpec((1, GROUP_W), lambda b, c, g: (0, g)),
        state=pl.BlockSpec((None, PAIRS_G, N_STATE, 128), lambda b, c, g: (row(b, c, g), g, 0, 0)),
        narrow=pl.BlockSpec((CHUNK, 128), lambda b, c, g: (row(b, c, g), g)))


def ssd_fwd(xa, acT, dtC, acC, dskip_l, bl, s, name):
    t = bl * s
    nc = s // CHUNK
    ac4 = acT.reshape(bl * nc, N_PAIRS, 2, CHUNK)

    def body(x_ref, b_ref, c_ref, ac_ref, dtc_ref, acc_ref, dsk_ref, y_ref, prev_ref, st_ref):
        c = pl.program_id(1)
        g = pl.program_id(2)
        bm = b_ref[...]
        cm = c_ref[...]
        cb = _dot(cm, bm, _NT)
        tri = _tri((CHUNK, CHUNK), lambda r, c: r >= c)

        @pl.when(c == 0)
        def _():
            for p in range(PAIRS_G):
                st_ref[g * PAIRS_G + p] = jnp.zeros((N_STATE, 128), F32)

        for p in range(PAIRS_G):
            hp = g * PAIRS_G + p
            cs = slice(p * 128, (p + 1) * 128)
            x = x_ref[:, cs].astype(F32)
            left, dtl, acl, xdt, dks, aclast = _ssd_pair(x, dtc_ref[:, cs], acc_ref[:, cs], ac_ref[p], tri)
            xdt_b = xdt.astype(BF16)
            ys = [_dot((cb * dks[e]).astype(BF16), xdt_b) for e in range(2)]
            st = st_ref[hp]
            y_off = _dot(cm, st.astype(BF16)) * jnp.exp(acl)
            y_ref[:, cs] = (jnp.where(left, ys[0], ys[1]) + y_off + x * dsk_ref[:, cs]).astype(BF16)
            xw = (xdt * jnp.exp(aclast - acl)).astype(BF16)
            prev_ref[p] = st
            st_ref[hp] = st * jnp.exp(aclast) + _dot(bm, xw, _TN)

    sp = _ssd_specs(nc, False)
    return pl.pallas_call(
        body, name=name, grid=(bl, nc, N_GROUPS),
        in_specs=[sp["wide"], sp["bmat"], sp["cmat"], sp["rows2"], sp["cols"], sp["cols"], sp["dskip"]],
        out_specs=[sp["wide"], sp["state"]],
        out_shape=[jax.ShapeDtypeStruct((t, SSM_W), BF16),
                   jax.ShapeDtypeStruct((bl * nc, N_PAIRS, N_STATE, 128), F32)],
        scratch_shapes=[pltpu.VMEM((N_PAIRS, N_STATE, 128), F32)],
        compiler_params=_params("parallel", "arbitrary", "arbitrary"),
    )(xa, xa, xa, ac4, dtC, acC, dskip_l)


def ssd_bwd(dy, xa, acT, dtC, acC, dskip_l, prev, bl, s, name):
    t = bl * s
    nc = s // CHUNK
    ac4 = acT.reshape(bl * nc, N_PAIRS, 2, CHUNK)

    def body(dy_ref, x_ref, b_ref, c_ref, ac_ref, dtc_ref, acc_ref, dsk_ref, prev_ref,
             dx_ref, db_ref, dc_ref, dd_ref, dsk_out_ref, dp_ref):
        b = pl.program_id(0)
        cr = pl.program_id(1)
        g = pl.program_id(2)

        @pl.when(cr == 0)
        def _():
            for p in range(PAIRS_G):
                dp_ref[g * PAIRS_G + p] = jnp.zeros((N_STATE, 128), F32)

        @pl.when((b == 0) & (cr == 0) & (g == 0))
        def _():
            dsk_out_ref[...] = jnp.zeros(dsk_out_ref.shape, F32)

        bm = b_ref[...]
        cm = c_ref[...]
        cb = _dot(cm, bm, _NT)
        tri = _tri((CHUNK, CHUNK), lambda r, c: r >= c)
        lane = lax.broadcasted_iota(jnp.int32, (CHUNK, 128), 1)
        lrow = lax.broadcasted_iota(jnp.int32, (1, CHUNK), 1)
        krow = lax.broadcasted_iota(jnp.int32, (8, CHUNK), 0)
        dcb = jnp.zeros((CHUNK, CHUNK), F32)
        dc_acc = jnp.zeros((CHUNK, N_STATE), F32)
        db_acc = jnp.zeros((CHUNK, N_STATE), F32)
        for p in range(PAIRS_G):
            hp = g * PAIRS_G + p
            cs = slice(p * 128, (p + 1) * 128)
            x = x_ref[:, cs].astype(F32)
            left, dtl, acl, xdt, dks, aclast = _ssd_pair(x, dtc_ref[:, cs], acc_ref[:, cs], ac_ref[p], tri)
            dyv = dy_ref[:, cs].astype(F32)
            dy_b = dyv.astype(BF16)
            xdt_b = xdt.astype(BF16)
            st = prev_ref[p]
            st_b = st.astype(BF16)
            ea = jnp.exp(acl)
            ds = jnp.exp(aclast - acl)
            cdl = jnp.exp(aclast)
            xw = xdt * ds
            masks = [left, jnp.logical_not(left)]

            dsk_out_ref[hp] = dsk_out_ref[hp] + jnp.sum(dyv * x, axis=0, keepdims=True)

            yo = _dot(cm, st_b)
            dyo_b = (dyv * ea).astype(BF16)
            yoff_term = dyv * yo * ea
            dc_acc = dc_acc + _dot(dyo_b, st_b, _NT)
            dst = _dot(cm, dyo_b, _TN)
            dsv = dp_ref[hp]
            dsv_b = dsv.astype(BF16)
            dxw = _dot(bm, dsv_b)
            db_acc = db_acc + _dot(xw.astype(BF16), dsv_b, _NT)
            dxdt = dxw * ds
            qv = dxw * xw
            end_term = dsv * st * cdl
            dp_ref[hp] = dsv * cdl + dst

            cols = jnp.zeros((CHUNK, 128), F32)
            rows = []
            for e in range(2):
                m = cb * dks[e]
                dy_e = jnp.where(masks[e], dyv, 0.0).astype(BF16)
                dm = _dot(dy_e, xdt_b, _NT)
                w = dm * m
                dcb = dcb + dm * dks[e]
                dxdt = dxdt + jnp.where(masks[e], _dot(m.astype(BF16), dy_b, _TN), 0.0)
                dac_col = jnp.sum(w + jnp.where(masks[e], yoff_term - qv, 0.0), axis=1, keepdims=True)
                cols = jnp.where(lane == 2 + e, dac_col, cols)
                tail = jnp.sum(jnp.where(masks[e], qv + end_term, 0.0))
                rows.append(jnp.where(lrow == CHUNK - 1, tail, 0.0) - jnp.sum(w, axis=0, keepdims=True))
            dx_ref[:, cs] = (dxdt * dtl + dyv * dsk_ref[:, cs]).astype(BF16)
            ddt_l = dxdt * x
            for e in range(2):
                cols = jnp.where(lane == e, jnp.sum(jnp.where(masks[e], ddt_l, 0.0), axis=1, keepdims=True), cols)
            dd_ref[p] = cols.T[0:8] + jnp.where(krow == 2, rows[0], 0.0) + jnp.where(krow == 3, rows[1], 0.0)
        dcb_b = dcb.astype(BF16)
        dc_ref[...] = dc_acc + _dot(dcb_b, bm)
        db_ref[...] = db_acc + _dot(dcb_b, cm, _TN)

    sp = _ssd_specs(nc, True)
    dx, db, dc, dd, dsk = pl.pallas_call(
        body, name=name, grid=(bl, nc, N_GROUPS),
        in_specs=[sp["wide"], sp["wide"], sp["bmat"], sp["cmat"], sp["rows2"], sp["cols"], sp["cols"], sp["dskip"],
                  sp["state"]],
        out_specs=[sp["wide"], sp["narrow"], sp["narrow"], sp["rows8"],
                   pl.BlockSpec((N_PAIRS, 1, 128), lambda b, c, g: (0, 0, 0))],
        out_shape=[jax.ShapeDtypeStruct((t, SSM_W), BF16),
                   jax.ShapeDtypeStruct((t, N_GROUPS * N_STATE), F32),
                   jax.ShapeDtypeStruct((t, N_GROUPS * N_STATE), F32),
                   jax.ShapeDtypeStruct((bl * nc, N_PAIRS, 8, CHUNK), F32),
                   jax.ShapeDtypeStruct((N_PAIRS, 1, 128), F32)],
        scratch_shapes=[pltpu.VMEM((N_PAIRS, N_STATE, 128), F32)],
        compiler_params=_params("arbitrary", "arbitrary", "arbitrary"),
    )(dy, xa, xa, xa, ac4, dtC, acC, dskip_l, prev)
    ddtT = dd[:, :, 0:2, :].reshape(bl * nc, N_HEADS, CHUNK)
    dacT = dd[:, :, 2:4, :].reshape(bl * nc, N_HEADS, CHUNK)
    return dx, db, dc, ddtT, dacT, dsk.reshape(N_PAIRS, 128)


def gnorm_fwd(y, proj, w, name):
    t = y.shape[0]
    tr = min(256, t)
    zb = OFF_Z // SSM_W

    def body(y_ref, z_ref, w_ref, o_ref):
        z = z_ref[...].astype(F32)
        yg = y_ref[...].astype(F32) * z * _sigmoid(z)
        r = lax.rsqrt(jnp.mean(yg * yg, axis=-1, keepdims=True) + EPS)
        o_ref[...] = (yg * r * w_ref[...]).astype(BF16)

    return pl.pallas_call(
        body, name=name, grid=(t // tr,),
        in_specs=[pl.BlockSpec((tr, SSM_W), lambda i: (i, 0)), pl.BlockSpec((tr, SSM_W), lambda i: (i, zb)),
                  pl.BlockSpec((1, SSM_W), lambda i: (0, 0))],
        out_specs=pl.BlockSpec((tr, SSM_W), lambda i: (i, 0)),
        out_shape=jax.ShapeDtypeStruct((t, SSM_W), BF16),
        compiler_params=_params("parallel"),
    )(y, proj, w.reshape(1, SSM_W))


def gnorm_bwd(ds, y, proj, w, name):
    t = y.shape[0]
    tr = min(256, t)
    zb = OFF_Z // SSM_W

    def body(ds_ref, y_ref, z_ref, w_ref, dy_ref, dz_ref, dw_ref):
        z = z_ref[...].astype(F32)
        yv = y_ref[...].astype(F32)
        sg = _sigmoid(z)
        sz = z * sg
        yg = yv * sz
        r = lax.rsqrt(jnp.mean(yg * yg, axis=-1, keepdims=True) + EPS)
        xh = yg * r
        g = ds_ref[...].astype(F32)
        dxh = g * w_ref[...]
        dyg = r * (dxh - xh * jnp.mean(dxh * xh, axis=-1, keepdims=True))
        dy_ref[...] = (dyg * sz).astype(BF16)
        dz_ref[...] = (dyg * yv * sg * (1.0 + z * (1.0 - sg))).astype(BF16)
        part = jnp.sum(g * xh, axis=0, keepdims=True)

        @pl.when(pl.program_id(0) == 0)
        def _():
            dw_ref[...] = part

        @pl.when(pl.program_id(0) > 0)
        def _():
            dw_ref[...] += part

    return pl.pallas_call(
        body, name=name, grid=(t // tr,),
        in_specs=[pl.BlockSpec((tr, SSM_W), lambda i: (i, 0)), pl.BlockSpec((tr, SSM_W), lambda i: (i, 0)),
                  pl.BlockSpec((tr, SSM_W), lambda i: (i, zb)), pl.BlockSpec((1, SSM_W), lambda i: (0, 0))],
        out_specs=[pl.BlockSpec((tr, SSM_W), lambda i: (i, 0)), pl.BlockSpec((tr, SSM_W), lambda i: (i, 0)),
                   pl.BlockSpec((1, SSM_W), lambda i: (0, 0))],
        out_shape=[jax.ShapeDtypeStruct((t, SSM_W), BF16), jax.ShapeDtypeStruct((t, SSM_W), BF16),
                   jax.ShapeDtypeStruct((1, SSM_W), F32)],
        compiler_params=_params("arbitrary"),
    )(ds, y, proj, w.reshape(1, SSM_W))


def _pool_mixed(u, g, row):
    win = 2 << g
    acc = u
    for k in range(g + 1):
        acc = acc + _shift_down(acc, 1 << k, row)
    inv = 1.0 / jnp.minimum(row + 1, win).astype(F32)
    return acc * inv - u, inv


def pool_fwd(proj, pool_w, pool_scale, bl, s, name):
    t = bl * s

    def body(u_ref, g_ref, w_ref, sc_ref, o_ref):
        row = lax.broadcasted_iota(jnp.int32, (s, POOL_GD), 0)
        for g in range(POOL_G):
            cs = slice(g * POOL_GD, (g + 1) * POOL_GD)
            u = u_ref[:, cs].astype(F32)
            mixed, _ = _pool_mixed(u, g, row)
            pm = _dot(mixed.astype(BF16), w_ref[g])
            gate = g_ref[:, cs].astype(F32)
            o_ref[:, cs] = (pm * sc_ref[:, cs] * gate * _sigmoid(gate)).astype(BF16)

    return pl.pallas_call(
        body, name=name, grid=(bl,),
        in_specs=[pl.BlockSpec((s, POOL_W), lambda b: (b, OFF_PU // POOL_W)),
                  pl.BlockSpec((s, POOL_W), lambda b: (b, OFF_PG // POOL_W)),
                  pl.BlockSpec((POOL_G, POOL_GD, POOL_GD), lambda b: (0, 0, 0)),
                  pl.BlockSpec((1, POOL_W), lambda b: (0, 0))],
        out_specs=pl.BlockSpec((s, POOL_W), lambda b: (b, 0)),
        out_shape=jax.ShapeDtypeStruct((t, POOL_W), BF16),
        compiler_params=_params("parallel"),
    )(proj, proj, pool_w, pool_scale.reshape(1, POOL_W))


def pool_bwd(dp, proj, pool_w, pool_scale, bl, s, name):
    t = bl * s

    def body(dp_ref, u_ref, g_ref, w_ref, sc_ref, du_ref, dg_ref, dw_ref, dsc_ref):
        row = lax.broadcasted_iota(jnp.int32, (s, POOL_GD), 0)
        first = pl.program_id(0) == 0
        for g in range(POOL_G):
            cs = slice(g * POOL_GD, (g + 1) * POOL_GD)
            u = u_ref[:, cs].astype(F32)
            mixed, inv = _pool_mixed(u, g, row)
            mixed_b = mixed.astype(BF16)
            wg = w_ref[g]
            pm = _dot(mixed_b, wg)
            gate = g_ref[:, cs].astype(F32)
            sg = _sigmoid(gate)
            d = dp_ref[:, cs].astype(F32)
            sc = sc_ref[:, cs]
            dpm = (d * sc * gate * sg).astype(BF16)
            dg_ref[:, cs] = (d * pm * sc * sg * (1.0 + gate * (1.0 - sg))).astype(BF16)
            dsc = jnp.sum(d * pm * gate * sg, axis=0, keepdims=True)
            dwg = _dot(mixed_b, dpm, _TN)
            dmixed = _dot(dpm, wg, _NT)
            acc = dmixed * inv
            for k in range(g + 1):
                acc = acc + _shift_up(acc, 1 << k, row, s)
            du_ref[:, cs] = (acc - dmixed).astype(BF16)

            @pl.when(first)
            def _():
                dw_ref[g] = dwg
                dsc_ref[:, cs] = dsc

            @pl.when(jnp.logical_not(first))
            def _():
                dw_ref[g] = dw_ref[g] + dwg
                dsc_ref[:, cs] = dsc_ref[:, cs] + dsc

    return pl.pallas_call(
        body, name=name, grid=(bl,),
        in_specs=[pl.BlockSpec((s, POOL_W), lambda b: (b, 0)),
                  pl.BlockSpec((s, POOL_W), lambda b: (b, OFF_PU // POOL_W)),
                  pl.BlockSpec((s, POOL_W), lambda b: (b, OFF_PG // POOL_W)),
                  pl.BlockSpec((POOL_G, POOL_GD, POOL_GD), lambda b: (0, 0, 0)),
                  pl.BlockSpec((1, POOL_W), lambda b: (0, 0))],
        out_specs=[pl.BlockSpec((s, POOL_W), lambda b: (b, 0)), pl.BlockSpec((s, POOL_W), lambda b: (b, 0)),
                   pl.BlockSpec((POOL_G, POOL_GD, POOL_GD), lambda b: (0, 0, 0)),
                   pl.BlockSpec((1, POOL_W), lambda b: (0, 0))],
        out_shape=[jax.ShapeDtypeStruct((t, POOL_W), BF16), jax.ShapeDtypeStruct((t, POOL_W), BF16),
                   jax.ShapeDtypeStruct((POOL_G, POOL_GD, POOL_GD), F32), jax.ShapeDtypeStruct((1, POOL_W), F32)],
        compiler_params=_params("arbitrary"),
    )(dp, proj, proj, pool_w, pool_scale.reshape(1, POOL_W))


SB_SCALE = 64 ** -0.5


KB = 256


def _sb_block(qe, kj, mask, rr, upper):
    z = _dot(qe, kj, _NT).astype(BF16)
    lb = jnp.minimum(z, 0.0) - jnp.log(1.0 + jnp.exp(-jnp.abs(z)))
    lom = lb - z if mask is None else jnp.where(mask, lb - z, 0.0)
    later = _dot(lom, upper) + rr
    return lb, lom, later


def _sb_masks(i):
    lane = lax.broadcasted_iota(jnp.int32, (QB, 128), 1)
    row = lax.broadcasted_iota(jnp.int32, (2 * QB, KB), 0) % QB
    col = lax.broadcasted_iota(jnp.int32, (2 * QB, KB), 1)
    causal = lambda jb: col + (jb * KB - i * QB) < row
    return lane, lane < 64, causal


def _stack_heads(x, left):
    zero = jnp.zeros_like(x)
    return jnp.concatenate([jnp.where(left, x, zero), jnp.where(left, zero, x)], axis=0)


SB_GROUP = 4
SB_GW = SB_GROUP * 128


def sb_fwd(proj, bl, s, name, comm=None):
    t = bl * s
    nq = s // QB
    group, gw = SB_PAIRS, SB_W
    qb0, kb0, vb0, gb0 = OFF_QKV // gw, (OFF_QKV + SB_W) // gw, (OFF_QKV + 2 * SB_W) // gw, OFF_SBG // gw
    grid = (bl, SB_PAIRS // group, nq)
    x_in, x_args, x_out, x_shape, x_scratch, x_start, x_wait = _hosted_exchange(
        comm, grid, relay_at=(bl - 1, 0, (3 * nq) // 4))

    def body(*refs):
        q_ref, k_ref, v_ref, g_ref = refs[:4]
        og_ref, o_ref, r_ref = refs[4 + len(x_in):7 + len(x_in)]
        x_refs = refs[4:4 + len(x_in)] + refs[7 + len(x_in):]
        x_start(x_refs)
        i = pl.program_id(2)
        lane, left, causal = _sb_masks(i)
        upper = _tri((KB, KB), lambda r, c: r > c).astype(BF16)
        cols = [slice(p * 128, (p + 1) * 128) for p in range(group)]
        qcats = [_stack_heads(q_ref[:, cs] * SB_SCALE, left) for cs in cols]
        zero = qcats[0].astype(F32) * 0.0

        def block(jb, carry, diagonal):
            rows = pl.ds(pl.multiple_of(jb * KB, KB), KB)
            mask = causal(jb) if diagonal else None
            out = []
            for p, cs in enumerate(cols):
                acc, rr, rt = carry[p]
                lb, lom, later = _sb_block(qcats[p], k_ref[rows, cs], mask, rr, upper)
                att = jnp.exp(lb + later)
                if diagonal:
                    att = jnp.where(mask, att, 0.0)
                acc = acc + _dot(att.astype(BF16), v_ref[rows, cs])
                rt = jnp.where(lane == jb, rr[:QB], jnp.where(lane == 8 + jb, rr[QB:], rt))
                out.append((acc, rr + jnp.sum(lom, axis=1, keepdims=True, dtype=F32), rt))
            return tuple(out)

        carry = block(i, tuple((zero, zero[:, :1], zero[:QB]) for _ in cols), True)
        carry = lax.fori_loop(0, i, lambda jj, c: block(i - 1 - jj, c, False), carry)
        for p, cs in enumerate(cols):
            acc, _, rtile = carry[p]
            o = jnp.where(left, acc[:QB], acc[QB:])
            gate = g_ref[:, cs].astype(F32)
            o_ref[:, cs] = o.astype(BF16)
            og_ref[:, cs] = (o * gate * _sigmoid(gate)).astype(BF16)
            r_ref[p] = rtile
        x_wait(x_refs)

    rowblk = lambda b, g, i: (b * nq + i, g)
    return pl.pallas_call(
        body, name=name, grid=grid,
        in_specs=[pl.BlockSpec((QB, gw), lambda b, g, i: (b * nq + i, qb0 + g)),
                  pl.BlockSpec((s, gw), lambda b, g, i: (b, kb0 + g)),
                  pl.BlockSpec((s, gw), lambda b, g, i: (b, vb0 + g)),
                  pl.BlockSpec((QB, gw), lambda b, g, i: (b * nq + i, gb0 + g))] + x_in,
        out_specs=[pl.BlockSpec((QB, gw), rowblk), pl.BlockSpec((QB, gw), rowblk),
                   pl.BlockSpec((None, group, QB, 128), lambda b, g, i: (b * nq + i, g, 0, 0))] + x_out,
        out_shape=[jax.ShapeDtypeStruct((t, SB_W), BF16), jax.ShapeDtypeStruct((t, SB_W), BF16),
                   jax.ShapeDtypeStruct((bl * nq, SB_PAIRS, QB, 128), F32)] + x_shape,
        scratch_shapes=x_scratch,
        compiler_params=_params("arbitrary", "arbitrary", "arbitrary"),
    )(proj, proj, proj, proj, *x_args)


def sb_bwd(dsb, o, rsave, proj, bl, s, name, comm=None):
    t = bl * s
    nq = s // QB
    qb0, kb0, vb0, gb0 = OFF_QKV // SB_GW, (OFF_QKV + SB_W) // SB_GW, (OFF_QKV + 2 * SB_W) // SB_GW, OFF_SBG // SB_GW
    grid = (bl, SB_PAIRS // SB_GROUP, nq)
    x_in, x_args, x_out, x_shape, x_scratch, x_start, x_wait = _hosted_exchange(
        comm, grid, relay_at=(bl - 1, SB_PAIRS // SB_GROUP - 1, 0))

    def body(*refs):
        n = len(x_in)
        d_ref, o_ref, r_ref, q_ref, k_ref, v_ref, g_ref = refs[:7]
        dq_ref, dk_ref, dv_ref, dg_ref = refs[7 + n:11 + n]
        dk_acc, dv_acc = refs[11 + 2 * n:13 + 2 * n]
        x_refs = refs[7:7 + n] + refs[11 + n:11 + 2 * n] + refs[13 + 2 * n:]
        x_start(x_refs)
        i = pl.program_id(2)

        @pl.when(i == 0)
        def _():
            dk_acc[...] = jnp.zeros(dk_acc.shape, F32)
            dv_acc[...] = jnp.zeros(dv_acc.shape, F32)

        lane, left, causal = _sb_masks(i)
        upper = _tri((KB, KB), lambda r, c: r > c).astype(BF16)
        lower = _tri((KB, KB), lambda r, c: r < c).astype(BF16)
        cols = [slice(p * 128, (p + 1) * 128) for p in range(SB_GROUP)]
        qcats, docats = [], []
        for cs in cols:
            gate = g_ref[:, cs].astype(F32)
            sg = _sigmoid(gate)
            d = d_ref[:, cs].astype(F32)
            dg_ref[:, cs] = (d * o_ref[:, cs].astype(F32) * sg * (1.0 + gate * (1.0 - sg))).astype(BF16)
            docats.append(_stack_heads((d * gate * sg).astype(BF16), left))
            qcats.append(_stack_heads(q_ref[:, cs] * SB_SCALE, left))
        qcat_ts = [qc.astype(F32).T.astype(BF16) for qc in qcats]
        docat_ts = [dc.astype(F32).T.astype(BF16) for dc in docats]
        zero = qcats[0].astype(F32) * 0.0

        def block(jb, carry, diagonal):
            rows = pl.ds(pl.multiple_of(jb * KB, KB), KB)
            mask = causal(jb) if diagonal else None
            out = []
            for p, cs in enumerate(cols):
                dq, gcar = carry[p]
                kj = k_ref[rows, cs]
                vj = v_ref[rows, cs]
                rtile = r_ref[p]
                rr = jnp.concatenate(
                    [jnp.sum(jnp.where(lane == jb, rtile, 0.0), axis=1, keepdims=True),
                     jnp.sum(jnp.where(lane == 8 + jb, rtile, 0.0), axis=1, keepdims=True)], axis=0)
                lb, lom, later = _sb_block(qcats[p], kj, mask, rr, upper)
                att = jnp.exp(lb + later)
                if diagonal:
                    att = jnp.where(mask, att, 0.0)
                de = att * _dot(docats[p], vj, _NT)
                gpre = _dot(de.astype(BF16), lower) + gcar
                sig = jnp.exp(lb)
                dz = de * (1.0 - sig) - gpre * sig
                if diagonal:
                    dz = jnp.where(mask, dz, 0.0)
                dz = dz.astype(BF16)
                dk_acc[jb, cs, :] = dk_acc[jb, cs, :] + _dot(qcat_ts[p], dz)
                dv_acc[jb, cs, :] = dv_acc[jb, cs, :] + _dot(docat_ts[p], att.astype(BF16))
                out.append((dq + _dot(dz, kj), gcar + jnp.sum(de, axis=1, keepdims=True)))
            return tuple(out)

        carry = lax.fori_loop(0, i, lambda jb, c: block(jb, c, False), tuple((zero, zero[:, :1]) for _ in cols))
        carry = block(i, carry, True)
        for p, cs in enumerate(cols):
            dq = carry[p][0]
            dq_ref[:, cs] = (jnp.where(left, dq[:QB], dq[QB:]) * SB_SCALE).astype(BF16)

        @pl.when(i == nq - 1)
        def _():
            for kb in range(s // KB):
                for cs in cols:
                    dk_ref[kb * KB:(kb + 1) * KB, cs] = dk_acc[kb, cs, :].T.astype(BF16)
                    dv_ref[kb * KB:(kb + 1) * KB, cs] = dv_acc[kb, cs, :].T.astype(BF16)

        x_wait(x_refs)

    rowblk = lambda b, g, i: (b * nq + i, g)
    seqblk = lambda b, g, i: (b, g)
    return pl.pallas_call(
        body, name=name, grid=grid,
        in_specs=[pl.BlockSpec((QB, SB_GW), rowblk), pl.BlockSpec((QB, SB_GW), rowblk),
                  pl.BlockSpec((None, SB_GROUP, QB, 128), lambda b, g, i: (b * nq + i, g, 0, 0)),
                  pl.BlockSpec((QB, SB_GW), lambda b, g, i: (b * nq + i, qb0 + g)),
                  pl.BlockSpec((s, SB_GW), lambda b, g, i: (b, kb0 + g)),
                  pl.BlockSpec((s, SB_GW), lambda b, g, i: (b, vb0 + g)),
                  pl.BlockSpec((QB, SB_GW), lambda b, g, i: (b * nq + i, gb0 + g))] + x_in,
        out_specs=[pl.BlockSpec((QB, SB_GW), rowblk), pl.BlockSpec((s, SB_GW), seqblk),
                   pl.BlockSpec((s, SB_GW), seqblk), pl.BlockSpec((QB, SB_GW), rowblk)] + x_out,
        out_shape=[jax.ShapeDtypeStruct((t, SB_W), BF16)] * 4 + x_shape,
        scratch_shapes=[pltpu.VMEM((s // KB, SB_GW, KB), F32), pltpu.VMEM((s // KB, SB_GW, KB), F32)] + x_scratch,
        compiler_params=_params("arbitrary", "arbitrary", "arbitrary"),
    )(dsb, o, rsave, proj, proj, proj, proj, *x_args)


def merge_fwd(proj, ys, yp, yb, name):
    t = ys.shape[0]
    tr = min(512, t)

    def body(m_ref, ys_ref, yp_ref, yb_ref, o_ref):
        acc = jnp.zeros((tr, D), F32)
        for k, ref in enumerate((ys_ref, yp_ref, yb_ref)):
            acc = acc + _sigmoid(m_ref[:, k * D:(k + 1) * D].astype(F32)) * ref[...].astype(F32)
        o_ref[...] = acc.astype(BF16)

    rowblk = pl.BlockSpec((tr, D), lambda i: (i, 0))
    return pl.pallas_call(
        body, name=name, grid=(t // tr,),
        in_specs=[pl.BlockSpec((tr, 3 * D), lambda i: (i, 0)), rowblk, rowblk, rowblk],
        out_specs=rowblk,
        out_shape=jax.ShapeDtypeStruct((t, D), BF16),
        compiler_params=_params("parallel"),
    )(proj, ys, yp, yb)


def merge_bwd(dm, proj, ys, yp, yb, name):
    t = ys.shape[0]
    tr = min(512, t)

    def body(dm_ref, m_ref, ys_ref, yp_ref, yb_ref, d0_ref, d1_ref, d2_ref, dl_ref):
        dmv = dm_ref[...].astype(F32)
        for k, (ref, dref) in enumerate(((ys_ref, d0_ref), (yp_ref, d1_ref), (yb_ref, d2_ref))):
            g = _sigmoid(m_ref[:, k * D:(k + 1) * D].astype(F32))
            dref[...] = (g * dmv).astype(BF16)
            dl_ref[:, k * D:(k + 1) * D] = (dmv * ref[...].astype(F32) * g * (1.0 - g)).astype(BF16)

    rowblk = pl.BlockSpec((tr, D), lambda i: (i, 0))
    wide = pl.BlockSpec((tr, 3 * D), lambda i: (i, 0))
    return pl.pallas_call(
        body, name=name, grid=(t // tr,),
        in_specs=[rowblk, wide, rowblk, rowblk, rowblk],
        out_specs=[rowblk, rowblk, rowblk, wide],
        out_shape=[jax.ShapeDtypeStruct((t, D), BF16)] * 3 + [jax.ShapeDtypeStruct((t, 3 * D), BF16)],
        compiler_params=_params("parallel"),
    )(dm, proj, ys, yp, yb)


def layer_fwd(x, lw, bl, s, tag, comm=None):
    t = bl * s
    h = rmsnorm_fwd(x, lw["norm_w"], f"norm_fwd{tag}")
    proj = matmul(h, lw["w_in"], "nt", BF16, f"in_proj{tag}", tn=2048)
    xa = conv_fwd(proj, lw["conv_w"], lw["conv_b"], bl, s, f"conv_fwd{tag}")
    dtT, acT, dtC, acC = dt_fwd(proj, lw["dt_bias"], lw["a_log"], t, f"dt_fwd{tag}")
    dskip_l = jnp.repeat(lw["d_skip"], 64).reshape(1, SSM_W)
    y, prev = ssd_fwd(xa, acT, dtC, acC, dskip_l, bl, s, f"ssd_fwd{tag}")
    s_out = gnorm_fwd(y, proj, lw["ssm_norm_w"], f"gnorm_fwd{tag}")
    p_out = pool_fwd(proj, lw["pool_w"], lw["pool_scale"], bl, s, f"pool_fwd{tag}")
    sb_out, sb_o, sb_r, *carried = sb_fwd(proj, bl, s, f"sb_fwd{tag}", comm)
    ys = matmul(s_out, lw["w_proj_ssm"], "nn", BF16, f"proj_ssm{tag}")
    yp = matmul(p_out, lw["w_proj_pool"], "nn", BF16, f"proj_pool{tag}")
    yb = matmul(sb_out, lw["w_proj_sb"], "nn", BF16, f"proj_sb{tag}")
    merged = merge_fwd(proj, ys, yp, yb, f"merge_fwd{tag}")
    x_next = matmul(merged, lw["w_out"], "nn", F32, f"out_proj{tag}", residual=x)
    saved = dict(x=x, h=h, proj=proj, xa=xa, dtT=dtT, acT=acT, dtC=dtC, acC=acC, y=y, prev=prev, s_out=s_out, p_out=p_out,
                 sb_out=sb_out, sb_o=sb_o, sb_r=sb_r, ys=ys, yp=yp, yb=yb, merged=merged)
    return x_next, saved, (carried[0] if carried else None)


def layer_bwd(dx, dx_b, lw, sv, bl, s, tag, comm, own_slabs):
    t = bl * s
    g = {}
    dmerged = matmul(dx_b, lw["w_out"], "nt", BF16, f"d_merged{tag}")
    g["w_out"] = matmul(sv["merged"], dx_b, "tn", BF16, f"dw_out{tag}")
    dys, dyp, dyb, dlogit = merge_bwd(dmerged, sv["proj"], sv["ys"], sv["yp"], sv["yb"], f"merge_bwd{tag}")
    ds_out = matmul(dys, lw["w_proj_ssm"], "nt", BF16, f"d_sout{tag}")
    g["w_proj_ssm"] = matmul(sv["s_out"], dys, "tn", BF16, f"dw_proj_ssm{tag}")
    dp_out = matmul(dyp, lw["w_proj_pool"], "nt", BF16, f"d_pout{tag}")
    g["w_proj_pool"] = matmul(sv["p_out"], dyp, "tn", BF16, f"dw_proj_pool{tag}")
    dsb_out = matmul(dyb, lw["w_proj_sb"], "nt", BF16, f"d_sbout{tag}")
    g["w_proj_sb"] = matmul(sv["sb_out"], dyb, "tn", BF16, f"dw_proj_sb{tag}")
    dy, dz, dnw = gnorm_bwd(ds_out, sv["y"], sv["proj"], lw["ssm_norm_w"], f"gnorm_bwd{tag}")
    g["ssm_norm_w"] = dnw[0]
    dskip_l = jnp.repeat(lw["d_skip"], 64).reshape(1, SSM_W)
    dxs, db, dc, ddtT, dacT, dsk = ssd_bwd(dy, sv["xa"], sv["acT"], sv["dtC"], sv["acC"], dskip_l, sv["prev"], bl, s,
                                           f"ssd_bwd{tag}")
    g["d_skip"] = jnp.sum(dsk.reshape(N_HEADS, 64), axis=1)
    ddt_raw, da, dbias = dt_bwd(ddtT, dacT, sv["dtT"], sv["proj"], lw["dt_bias"], lw["a_log"], t, f"dt_bwd{tag}")
    g["a_log"] = da * (-jnp.exp(lw["a_log"]))
    g["dt_bias"] = dbias
    dxa = jnp.concatenate([dxs, db.astype(BF16), dc.astype(BF16)], axis=1)
    dxbc, dcw, dcb = conv_bwd(dxa, sv["proj"], lw["conv_w"], lw["conv_b"], bl, s, f"conv_bwd{tag}")
    g["conv_w"] = dcw
    g["conv_b"] = dcb
    dpu, dpg, dpw, dpsc = pool_bwd(dp_out, sv["proj"], lw["pool_w"], lw["pool_scale"], bl, s, f"pool_bwd{tag}")
    g["pool_w"] = dpw
    g["pool_scale"] = dpsc[0]
    dq, dk, dv, dsbg, *carried = sb_bwd(dsb_out, sv["sb_o"], sv["sb_r"], sv["proj"], bl, s, f"sb_bwd{tag}", comm)
    dproj = concat_columns([dlogit, dsbg, dpu, dpg, dz, dq, dk, dv, dxbc, ddt_raw], PC, f"d_proj{tag}")
    g["w_in"] = matmul(dproj, sv["h"], "tn", BF16, f"dw_in{tag}", tk=2048)
    own_comm = own_slabs(g) if own_slabs is not None else None
    dh = matmul(dproj, lw["w_in"], "nn", F32, f"d_h{tag}", tk=2048, comm=own_comm)
    dh, own_got = dh if own_comm is not None else (dh, None)
    dx_in, dx_in_b, dnorm = rmsnorm_bwd(dh, sv["x"], lw["norm_w"], dx, f"norm_bwd{tag}")
    g["norm_w"] = dnorm[0]
    return dx_in, dx_in_b, g, (carried[0] if carried else None), (own_comm[0] if own_comm else None), own_got


def concat_columns(parts, width, name):
    t = parts[0].shape[0]
    tr = min(256, t)
    widths = [p.shape[1] for p in parts]
    used = sum(widths)

    def body(*refs):
        o_ref = refs[-1]
        off = 0
        for ref, w in zip(refs[:-1], widths):
            o_ref[:, off:off + w] = ref[...]
            off += w
        if width > used:
            o_ref[:, used:] = jnp.zeros((tr, width - used), BF16)

    return pl.pallas_call(
        body, name=name, grid=(t // tr,),
        in_specs=[pl.BlockSpec((tr, w), lambda i: (i, 0)) for w in widths],
        out_specs=pl.BlockSpec((tr, width), lambda i: (i, 0)),
        out_shape=jax.ShapeDtypeStruct((t, width), BF16),
        compiler_params=_params("parallel"),
    )(*parts)


_PAD_PIECES = ((10784, 3072), (9760, 1024), (4640, 1024), (5664, 1024), (0, 2048), (6688, 3072), (2048, 2560), (4608, 32))
_UNPAD_PIECES = ((OFF_Z, 2048), (OFF_XBC, 2560), (OFF_DT, 32), (OFF_PU, 1024), (OFF_PG, 1024), (OFF_QKV, 3072),
                 (OFF_SBG, 1024), (OFF_MERGE, 3072))


def pad_rows(wt):
    pieces = [wt[o:o + n] for o, n in _PAD_PIECES]
    return jnp.concatenate(pieces + [jnp.zeros((PC - IN_COLS, wt.shape[1]), wt.dtype)], axis=0)


def unpad_rows(wp):
    return jnp.concatenate([wp[o:o + n] for o, n in _UNPAD_PIECES], axis=0)


MESH = pl.DeviceIdType.MESH
ANY = pl.BlockSpec(memory_space=pl.ANY)


def _coords():
    return lax.axis_index("x"), lax.axis_index("y"), lax.axis_index("c")


def _peer(p):
    x, y, c = _coords()
    return (1 - x if p & 4 else x, 1 - y if p & 2 else y, 1 - c if p & 1 else c)


def _flat(pos):
    return 4 * pos[0] + 2 * pos[1] + pos[2]


def _chip(pos):
    return 2 * pos[0] + pos[1]


def _exchange_copies(v_ref, out_ref, send_sems, recv_sems, local_sem, mode):
    x, y, c = _coords()
    me = _flat((x, y, c))
    sibling = (x, y, 1 - c)
    chips = [(1 - x if j & 2 else x, 1 - y if j & 1 else y) for j in range(1, 4)]

    def copy(k, src, landing, to):
        return pltpu.make_async_remote_copy(src_ref=src, dst_ref=out_ref.at[landing], send_sem=send_sems.at[k],
                                            recv_sem=recv_sems.at[k], device_id=to, device_id_type=MESH)

    if mode == "direct":
        local = pltpu.make_async_copy(v_ref, out_ref.at[me], local_sem)
        first = [copy(p - 1, v_ref, me, _peer(p)) for p in range(1, N_DEV)]
        last = [copy(p - 1, v_ref, _flat(_peer(p)), _peer(p)) for p in range(1, N_DEV)]
        return local, first, [], last
    if mode == "gather":
        local = pltpu.make_async_copy(v_ref, out_ref.at[me], local_sem)
        first = [copy(0, v_ref, me, sibling)] + [copy(1 + j, v_ref, me, (*ch, c)) for j, ch in enumerate(chips)]
        relay = [(copy(1 + j, v_ref, _flat((*ch, c)), (*ch, c)),
                  copy(4 + j, out_ref.at[_flat((*ch, c))], _flat((*ch, c)), sibling)) for j, ch in enumerate(chips)]
        last = [copy(0, v_ref, _flat(sibling), sibling)] + [
            copy(4 + j, v_ref, _flat((*ch, 1 - c)), sibling) for j, ch in enumerate(chips)]
        return local, first, relay, last
    assert mode == "chips"
    mine = _chip((x, y))
    local = pltpu.make_async_copy(v_ref.at[mine], out_ref.at[mine], local_sem)
    first = [copy(j, v_ref.at[_chip(ch)], mine, (*ch, c)) for j, ch in enumerate(chips)]
    last = [copy(j, v_ref.at[mine], _chip(ch), (*ch, c)) for j, ch in enumerate(chips)]
    return local, first, [], last


def _swap_copies(s0_ref, s1_ref, out_ref, send_sems, recv_sems, local_sem):
    x, y, c = _coords()

    def four(src_ref):
        return [pltpu.make_async_remote_copy(src_ref=src_ref.at[j], dst_ref=out_ref.at[j], send_sem=send_sems.at[j],
                                             recv_sem=recv_sems.at[j], device_id=(x, y, 1 - c), device_id_type=MESH)
                for j in range(4)]

    return c, four(s1_ref), four(s0_ref)


def _exchange_start(*refs_and_mode):
    if refs_and_mode[-1] == "swap":
        c, from_core0, from_core1 = _swap_copies(*refs_and_mode[:-1])
        for core, copies in ((0, from_core0), (1, from_core1)):
            @pl.when(c == core)
            def _():
                for cp in copies:
                    cp.start()
        return
    local, first, _, _ = _exchange_copies(*refs_and_mode)
    local.start()
    for cp in first:
        cp.start()


def _exchange_relay(*refs_and_mode):
    if refs_and_mode[-1] == "swap":
        return
    for arrival, onward in _exchange_copies(*refs_and_mode)[2]:
        arrival.wait_recv()
        onward.start()


def _exchange_finish(*refs_and_mode):
    if refs_and_mode[-1] == "swap":
        _, four, _ = _swap_copies(*refs_and_mode[:-1])
        for cp in four:
            cp.wait_recv()
        for cp in four:
            cp.wait_send()
        return
    local, first, relay, last = _exchange_copies(*refs_and_mode)
    for cp in last:
        cp.wait_recv()
    for cp in first + [onward for _, onward in relay]:
        cp.wait_send()
    local.wait()


def _exchange_shape(v, mode):
    if mode == "swap":
        return jax.ShapeDtypeStruct(tuple(v[0].shape), v[0].dtype)
    return jax.ShapeDtypeStruct(tuple(v.shape) if mode == "chips" else (N_DEV,) + tuple(v.shape), v.dtype)


def _exchange_sems():
    return [pltpu.SemaphoreType.DMA((N_DEV - 1,)), pltpu.SemaphoreType.DMA((N_DEV - 1,)), pltpu.SemaphoreType.DMA]


def exchange(v, mode, name):
    def body(*refs):
        _exchange_start(*refs, mode)
        _exchange_relay(*refs, mode)
        _exchange_finish(*refs, mode)

    args = list(v) if mode == "swap" else [v]
    return pl.pallas_call(
        body, name=name,
        in_specs=[ANY] * len(args), out_specs=ANY,
        out_shape=_exchange_shape(v, mode),
        scratch_shapes=_exchange_sems(),
    )(*args)


def _hosted_exchange(comm, grid, relay_at):
    if comm is None:
        return [], [], [], [], [], (lambda refs: None), (lambda refs: None)
    v, mode = comm

    def at(step):
        cond = None
        for axis, want in enumerate(step):
            term = pl.program_id(axis) == want
            cond = term if cond is None else jnp.logical_and(cond, term)
        return cond

    def start(refs):
        @pl.when(at([0] * len(grid)))
        def _():
            _exchange_start(*refs, mode)

        if mode == "gather":
            @pl.when(at(relay_at))
            def _():
                _exchange_relay(*refs, mode)

    def wait(refs):
        @pl.when(at([n - 1 for n in grid]))
        def _():
            _exchange_finish(*refs, mode)

    args = list(v) if mode == "swap" else [v]
    return [ANY] * len(args), args, [ANY], [_exchange_shape(v, mode)], _exchange_sems(), start, wait


def pair_sum(slabs, got, name):
    _, r, c = got.shape
    tr = r // 4 if r % 64 == 0 else r

    def body(s0_ref, s1_ref, got_ref, o_ref):
        mine = jnp.where(lax.axis_index("c") == 0, s0_ref[...].astype(F32), s1_ref[...].astype(F32))
        o_ref[...] = (mine + got_ref[...].astype(F32)).astype(BF16)

    blk = pl.BlockSpec((None, tr, c), lambda j, i: (j, i, 0))
    return pl.pallas_call(
        body, name=name, grid=(4, r // tr),
        in_specs=[blk, blk, blk], out_specs=blk,
        out_shape=jax.ShapeDtypeStruct(got.shape, BF16),
        compiler_params=_params("parallel", "parallel"),
    )(*slabs, got)


def sum_slabs(v, name):
    n, r, c = v.shape
    tr = r // 4 if r % 64 == 0 else r

    def body(v_ref, o_ref):
        acc = v_ref[0].astype(F32)
        for k in range(1, n):
            acc = acc + v_ref[k].astype(F32)
        o_ref[...] = acc

    return pl.pallas_call(
        body, name=name, grid=(r // tr,),
        in_specs=[pl.BlockSpec((n, tr, c), lambda i: (0, i, 0))],
        out_specs=pl.BlockSpec((tr, c), lambda i: (i, 0)),
        out_shape=jax.ShapeDtypeStruct((r, c), F32),
        compiler_params=_params("parallel"),
    )(v)


def adamw(w, g, m, v, name):
    r, c = w.shape
    tr = next((cand for cand in (256, 128, 64, 32, 16, 8) if r % cand == 0), r)

    def body(w_ref, g_ref, m_ref, v_ref, d_ref, mo_ref, vo_ref):
        gv = g_ref[...]
        mn = ADAM_B1 * m_ref[...] + (1.0 - ADAM_B1) * gv
        vn = ADAM_B2 * v_ref[...] + (1.0 - ADAM_B2) * (gv * gv)
        m_hat = mn / (1.0 - ADAM_B1 ** ADAM_STEP)
        v_hat = vn / (1.0 - ADAM_B2 ** ADAM_STEP)
        d_ref[...] = -ADAM_LR * (m_hat / (jnp.sqrt(v_hat) + ADAM_EPS) + ADAM_WD * w_ref[...])
        mo_ref[...] = mn
        vo_ref[...] = vn

    blk = pl.BlockSpec((tr, c), lambda i: (i, 0))
    return pl.pallas_call(
        body, name=name, grid=(r // tr,),
        in_specs=[blk] * 4, out_specs=[blk] * 3,
        out_shape=[jax.ShapeDtypeStruct((r, c), F32)] * 3,
        compiler_params=_params("parallel"),
    )(w, g, m, v)


BIG = ("w_proj_ssm", "w_proj_pool", "w_proj_sb", "w_out", "pool_w", "w_in")
SHARD_IN = IN_COLS // N_DEV
BIG_ROWS = {"w_proj_ssm": SSM_W // N_DEV, "w_proj_pool": POOL_W // N_DEV, "w_proj_sb": SB_W // N_DEV,
            "w_out": D // N_DEV, "pool_w": POOL_G * (POOL_GD // N_DEV) * POOL_GD // D, "w_in": SHARD_IN}
PACK_C = D
PACK_R = 2432

REPLICATED = ("norm_w", "conv_b", "dt_bias", "a_log", "d_skip", "ssm_norm_w", "pool_scale")
WEIGHTS = ("norm_w", "w_in", "conv_w", "conv_b", "dt_bias", "a_log", "d_skip", "ssm_norm_w", "pool_w",
           "pool_scale", "w_proj_ssm", "w_proj_pool", "w_proj_sb", "w_out", "final_norm_w")


def _size(shape):
    n = 1
    for d in shape:
        n *= d
    return n


def _pad_flat(flat, n):
    return jnp.concatenate([flat, jnp.zeros((n - flat.shape[0],), flat.dtype)])


def _row_offsets():
    offs, off = {}, 0
    for n in BIG:
        offs[n] = off
        off += BIG_ROWS[n]
    return offs, off


def pack_shards(parts):
    rows = [parts[n].reshape(BIG_ROWS[n], PACK_C) for n in BIG]
    rows[-1] = jnp.pad(rows[-1], ((0, PACK_R - _row_offsets()[1]), (0, 0)))
    return jnp.concatenate(rows, axis=0)


def unpack_shards(packed):
    offs, _ = _row_offsets()
    out = {}
    for n in BIG:
        seg = packed[offs[n]:offs[n] + BIG_ROWS[n]]
        if n == "w_in":
            out[n] = seg.T
        elif n == "pool_w":
            out[n] = seg.reshape(POOL_G, POOL_GD // N_DEV, POOL_GD)
        else:
            out[n] = seg
    return out


def unpack_gathered(g):
    offs, _ = _row_offsets()
    out = {}
    for n in BIG:
        seg = g[:, offs[n]:offs[n] + BIG_ROWS[n], :]
        if n == "w_in":
            out[n] = pad_rows(seg.reshape(IN_COLS, D))
        elif n == "pool_w":
            out[n] = seg.reshape(N_DEV, POOL_G, POOL_GD // N_DEV, POOL_GD).transpose(1, 0, 2, 3).reshape(
                POOL_G, POOL_GD, POOL_GD)
        else:
            out[n] = seg.reshape(N_DEV * BIG_ROWS[n], D)
    return out


def pack_slabs(g):
    segs = []
    for n in BIG:
        if n == "w_in":
            w = unpad_rows(g[n])
        elif n == "pool_w":
            w = g[n].reshape(POOL_G, N_DEV, POOL_GD // N_DEV, POOL_GD).transpose(1, 0, 2, 3)
        else:
            w = g[n]
        segs.append(w.reshape(N_DEV // 2, 2, BIG_ROWS[n], PACK_C).astype(BF16))
    segs[-1] = jnp.pad(segs[-1], ((0, 0), (0, 0), (0, PACK_R - _row_offsets()[1]), (0, 0)))
    return tuple(jnp.concatenate([seg[:, core] for seg in segs], axis=1) for core in range(2))


SMALL_ROWS = 544


def pack_small(vals):
    flat = jnp.concatenate([v.reshape(-1) for v in vals])
    return _pad_flat(flat, SMALL_ROWS * 128).reshape(SMALL_ROWS, 128)


def unpack_small(packed, shapes):
    flat = packed.reshape(-1)
    out, off = [], 0
    for shp in shapes:
        out.append(flat[off:off + _size(shp)].reshape(shp))
        off += _size(shp)
    return out


def kernel(x, norm_w, w_in, conv_w, conv_b, dt_bias, a_log, d_skip, ssm_norm_w, pool_w, pool_scale, w_proj_ssm, w_proj_pool, w_proj_sb, w_out, final_norm_w, loss_target, m_norm_w, m_w_in, m_conv_w, m_conv_b, m_dt_bias, m_a_log, m_d_skip, m_ssm_norm_w, m_pool_w, m_pool_scale, m_w_proj_ssm, m_w_proj_pool, m_w_proj_sb, m_w_out, m_final_norm_w, v_norm_w, v_w_in, v_conv_w, v_conv_b, v_dt_bias, v_a_log, v_d_skip, v_ssm_norm_w, v_pool_w, v_pool_scale, v_w_proj_ssm, v_w_proj_pool, v_w_proj_sb, v_w_out, v_final_norm_w):
    wts = dict(norm_w=norm_w, w_in=w_in, conv_w=conv_w, conv_b=conv_b, dt_bias=dt_bias, a_log=a_log, d_skip=d_skip,
               ssm_norm_w=ssm_norm_w, pool_w=pool_w, pool_scale=pool_scale, w_proj_ssm=w_proj_ssm,
               w_proj_pool=w_proj_pool, w_proj_sb=w_proj_sb, w_out=w_out, final_norm_w=final_norm_w)
    mom = dict(norm_w=m_norm_w, w_in=m_w_in, conv_w=m_conv_w, conv_b=m_conv_b, dt_bias=m_dt_bias, a_log=m_a_log,
               d_skip=m_d_skip, ssm_norm_w=m_ssm_norm_w, pool_w=m_pool_w, pool_scale=m_pool_scale,
               w_proj_ssm=m_w_proj_ssm, w_proj_pool=m_w_proj_pool, w_proj_sb=m_w_proj_sb, w_out=m_w_out,
               final_norm_w=m_final_norm_w)
    var = dict(norm_w=v_norm_w, w_in=v_w_in, conv_w=v_conv_w, conv_b=v_conv_b, dt_bias=v_dt_bias, a_log=v_a_log,
               d_skip=v_d_skip, ssm_norm_w=v_ssm_norm_w, pool_w=v_pool_w, pool_scale=v_pool_scale,
               w_proj_ssm=v_w_proj_ssm, w_proj_pool=v_w_proj_pool, w_proj_sb=v_w_proj_sb, w_out=v_w_out,
               final_norm_w=v_final_norm_w)
    bl, s, _ = x.shape
    t = bl * s
    me = _flat(_coords())

    cw = exchange(conv_w.reshape(40, 128), "direct", "gather_conv_w")
    conv_w_full = cw.reshape(N_DEV, DEPTH, CONV_K, CONV_CH // N_DEV).transpose(1, 2, 0, 3).reshape(
        DEPTH, CONV_K, CONV_CH)

    xc = x.reshape(t, D)
    layer_w, saved = [], []
    packed = [pack_shards({n: (wts[n][l].T if n == "w_in" else wts[n][l]).astype(BF16) for n in BIG})
              for l in range(DEPTH)]
    gathered = exchange(packed[0], "gather", "gather_w0")
    for l in range(DEPTH):
        lw = unpack_gathered(gathered)
        for n in REPLICATED:
            lw[n] = wts[n][l]
        lw["conv_w"] = conv_w_full[l]
        xc, sv, gathered = layer_fwd(xc, lw, bl, s, f"_l{l}", (packed[l + 1], "gather") if l + 1 < DEPTH else None)
        layer_w.append(lw)
        saved.append(sv)

    loss_part, dx, dx_b, dfinal = final_loss(xc, final_norm_w, loss_target.reshape(t, D), "final_loss")
    loss = lax.psum(loss_part[0, 0], ("x", "y", "c"))

    grads = [None] * DEPTH
    big_sum = [None] * DEPTH
    def last_layer_slabs(g):
        slabs = pack_slabs(g)
        return pair_sum(slabs, exchange(slabs, "swap", "pair_swap0"), "pair_sum0"), "chips"

    chip_sums = None
    for l in reversed(range(DEPTH)):
        dx, dx_b, g, got, own_sent, own_got = layer_bwd(
            dx, dx_b, layer_w[l], saved[l], bl, s, f"_l{l}", (chip_sums, "chips") if chip_sums is not None else None,
            last_layer_slabs if l == 0 else (lambda g: (pack_slabs(g), "swap")))
        if got is not None:
            big_sum[l + 1] = unpack_shards(sum_slabs(got, f"sum_g{l + 1}"))
        grads[l] = g
        if l > 0:
            chip_sums = pair_sum(own_sent, own_got, f"pair_sum{l}")
    big_sum[0] = unpack_shards(sum_slabs(own_got, "sum_g0"))
    grad_x = dx.reshape(bl, s, D)

    small_names = REPLICATED + ("conv_w",)
    small_vals = [jnp.stack([grads[l][n] for l in range(DEPTH)]) for n in small_names] + [dfinal[0]]
    small_shapes = [v.shape for v in small_vals]
    small_all = exchange(pack_small(small_vals), "direct", "gather_small")
    small_sum = unpack_small(sum_slabs(small_all, "sum_small"), small_shapes)
    gsum = dict(zip(small_names + ("final_norm_w",), small_sum))
    conv_g_full = gsum["conv_w"]
    gsum["conv_w"] = lax.dynamic_slice_in_dim(conv_g_full, me * (CONV_CH // N_DEV), CONV_CH // N_DEV, axis=2)
    for n in BIG:
        gsum[n] = jnp.stack([big_sum[l][n] for l in range(DEPTH)])

    delta, new_m, new_v = {}, {}, {}
    for n in BIG + ("conv_w",):
        shp = wts[n].shape
        two_d = (-1, shp[-1])
        d2, m2, v2 = adamw(wts[n].reshape(two_d), gsum[n].reshape(two_d), mom[n].reshape(two_d),
                           var[n].reshape(two_d), f"adamw_{n}")
        delta[n], new_m[n], new_v[n] = d2.reshape(shp), m2.reshape(shp), v2.reshape(shp)
    rep = REPLICATED + ("final_norm_w",)
    rep_shapes = [wts[n].shape for n in rep]
    d2, m2, v2 = adamw(pack_small([wts[n] for n in rep]), pack_small([gsum[n] for n in rep]),
                       pack_small([mom[n] for n in rep]), pack_small([var[n] for n in rep]), "adamw_small")
    for n, dv, mv, vv in zip(rep, unpack_small(d2, rep_shapes), unpack_small(m2, rep_shapes),
                             unpack_small(v2, rep_shapes)):
        delta[n], new_m[n], new_v[n] = dv, mv, vv

    return (loss, grad_x, *[gsum[n] for n in WEIGHTS], *[delta[n] for n in WEIGHTS],
            *[new_m[n] for n in WEIGHTS], *[new_v[n] for n in WEIGHTS])
```
